```python
import jax, jax.numpy as jnp
from jax import lax
import numpy as np

D_MODEL = 1024
BATCH = 8
SEQ = 4096
DEPTH = 1

CHUNK = 64
N_HEADS_SB = 8
HEAD_DIM_SB = 64
D_SB = N_HEADS_SB * HEAD_DIM_SB
N_GROUPS_SGU = 8
GROUP_DIM_SGU = 64
D_SGU = N_GROUPS_SGU * GROUP_DIM_SGU
SGU_CHUNK = 128
Q_BLOCK = 128
EPS = 1e-6
IN_WIDTHS = (D_SB, D_SB, D_SB, D_SB, D_SGU, D_SGU, D_SGU, D_MODEL, D_MODEL)
D_IN = 4 * D_SB + 3 * D_SGU + 2 * D_MODEL

kernel_name = 'stickbreak_sgu_gated_hybrid'


def rmsnorm(x, g):
    xf = x.astype(jnp.float32)
    y = xf * lax.rsqrt(jnp.mean(xf * xf, axis=-1, keepdims=True) + EPS)
    return (y * g.astype(jnp.float32)).astype(x.dtype)


def split_points():
    pts, acc = [], 0
    for w in IN_WIDTHS[:-1]:
        acc += w
        pts.append(acc)
    return pts


def stick_breaking_attention(q, k, v):
    b, s, h, dh = q.shape
    scale = dh ** -0.5
    qf = q.astype(jnp.float32).transpose(0, 2, 1, 3) * scale
    kf = k.astype(jnp.float32).transpose(0, 2, 1, 3)
    vf = v.astype(jnp.float32).transpose(0, 2, 1, 3)
    outs = []
    for start in range(0, s, Q_BLOCK):
        end = start + Q_BLOCK
        qb = qf[:, :, start:end]
        kb = kf[:, :, :end]
        vb = vf[:, :, :end]
        z = jnp.einsum('bhqd,bhkd->bhqk', qb, kb)
        t_idx = start + jnp.arange(Q_BLOCK)[:, None]
        s_idx = jnp.arange(end)[None, :]
        before = s_idx < t_idx
        log_keep = jnp.where(before, jax.nn.log_sigmoid(-z), 0.0)
        log_stick = lax.cumsum(log_keep, axis=3, reverse=True) - log_keep
        log_w = jax.nn.log_sigmoid(z) + log_stick
        w = jnp.where(before, jnp.exp(log_w), 0.0)
        outs.append(jnp.einsum('bhqk,bhkd->bhqd', w, vb))
    o = jnp.concatenate(outs, axis=2)
    return o.transpose(0, 2, 1, 3).astype(q.dtype)


def spatial_gating(u, v, ln_g, ln_b, w_s, b_s):
    b, s, g, c = v.shape
    vf = v.astype(jnp.float32)
    mu = jnp.mean(vf, axis=-1, keepdims=True)
    var = jnp.mean(jnp.square(vf - mu), axis=-1, keepdims=True)
    vn = (vf - mu) * lax.rsqrt(var + EPS) * ln_g.astype(jnp.float32) + ln_b.astype(jnp.float32)
    vn = vn.reshape(b, s // SGU_CHUNK, SGU_CHUNK, g, c)
    pos = jnp.arange(SGU_CHUNK)
    mask = (pos[None, :] // CHUNK) <= (pos[:, None] // CHUNK)
    w = jnp.where(mask[None], w_s.astype(jnp.float32), 0.0)
    mixed = jnp.einsum('gts,bnsgc->bntgc', w, vn) + b_s.astype(jnp.float32).T[None, None, :, :, None]
    return u * mixed.reshape(b, s, g, c).astype(u.dtype)


def _fwd_setup_inputs(seed: int = 0) -> dict:
    key = jax.random.key(seed)
    ks = jax.random.split(key, 13)
    f32 = jnp.float32
    x = jax.random.normal(ks[0], (BATCH, SEQ, D_MODEL), f32)
    norm_g = 1.0 + 0.1 * jax.random.normal(ks[1], (DEPTH, D_MODEL), f32)
    w_in = jax.random.normal(ks[2], (DEPTH, D_MODEL, D_IN), f32) * D_MODEL ** -0.5
    sgu_ln_g = 1.0 + 0.1 * jax.random.normal(ks[3], (DEPTH, N_GROUPS_SGU, GROUP_DIM_SGU), f32)
    sgu_ln_b = 0.1 * jax.random.normal(ks[4], (DEPTH, N_GROUPS_SGU, GROUP_DIM_SGU), f32)
    w_spatial = jax.random.normal(ks[5], (DEPTH, N_GROUPS_SGU, SGU_CHUNK, SGU_CHUNK), f32) * SGU_CHUNK ** -0.5
    b_spatial = 1.0 + 0.1 * jax.random.normal(ks[6], (DEPTH, N_GROUPS_SGU, SGU_CHUNK), f32)
    w_up_a = jax.random.normal(ks[7], (DEPTH, D_SB, D_MODEL), f32) * D_SB ** -0.5
    w_up_b = jax.random.normal(ks[8], (DEPTH, D_SGU, D_MODEL), f32) * D_SGU ** -0.5
    w_out = jax.random.normal(ks[9], (DEPTH, D_MODEL, D_MODEL), f32) * D_MODEL ** -0.5
    final_norm_g = 1.0 + 0.1 * jax.random.normal(ks[10], (D_MODEL,), f32)
    return {'x': x, 'norm_g': norm_g, 'w_in': w_in, 'sgu_ln_g': sgu_ln_g, 'sgu_ln_b': sgu_ln_b,
            'w_spatial': w_spatial, 'b_spatial': b_spatial, 'w_up_a': w_up_a, 'w_up_b': w_up_b,
            'w_out': w_out, 'final_norm_g': final_norm_g}


def _fwd_reference(x, norm_g, w_in, sgu_ln_g, sgu_ln_b, w_spatial, b_spatial, w_up_a, w_up_b, w_out, final_norm_g):
    b, s, _ = x.shape
    pts = split_points()
    for l in range(DEPTH):
        h = rmsnorm(x, norm_g[l])
        proj = jnp.einsum('bsd,de->bse', h, w_in[l])
        q, k, v, z_a, u_b, v_b, z_b, g_a, g_b = jnp.split(proj, pts, axis=-1)
        y_a = stick_breaking_attention(
            q.reshape(b, s, N_HEADS_SB, HEAD_DIM_SB),
            k.reshape(b, s, N_HEADS_SB, HEAD_DIM_SB),
            v.reshape(b, s, N_HEADS_SB, HEAD_DIM_SB)).reshape(b, s, D_SB) * jax.nn.silu(z_a)
        y_b = spatial_gating(
            jax.nn.gelu(u_b).reshape(b, s, N_GROUPS_SGU, GROUP_DIM_SGU),
            jax.nn.gelu(v_b).reshape(b, s, N_GROUPS_SGU, GROUP_DIM_SGU),
            sgu_ln_g[l], sgu_ln_b[l], w_spatial[l], b_spatial[l]).reshape(b, s, D_SGU) * jax.nn.silu(z_b)
        p_a = jnp.einsum('bse,ed->bsd', y_a, w_up_a[l])
        p_b = jnp.einsum('bse,ed->bsd', y_b, w_up_b[l])
        merged = jax.nn.sigmoid(g_a) * p_a + jax.nn.sigmoid(g_b) * p_b
        x = x + jnp.einsum('bsd,de->bse', merged, w_out[l])
    return rmsnorm(x, final_norm_g)


import jax as _jax
import jax.numpy as _jnp

TWIN_FORMAT = 'train_step'
FWD_PARAMS = ['x', 'norm_g', 'w_in', 'sgu_ln_g', 'sgu_ln_b', 'w_spatial', 'b_spatial', 'w_up_a', 'w_up_b', 'w_out', 'final_norm_g']
TWIN_WEIGHTS = ['norm_g', 'w_in', 'sgu_ln_g', 'sgu_ln_b', 'w_spatial', 'b_spatial', 'w_up_a', 'w_up_b', 'w_out', 'final_norm_g']
TWIN_DIFF_INPUT = 'x'
TWIN_INPUTS = ['x', 'norm_g', 'w_in', 'sgu_ln_g', 'sgu_ln_b', 'w_spatial', 'b_spatial', 'w_up_a', 'w_up_b', 'w_out', 'final_norm_g', 'loss_target', 'm_norm_g', 'm_w_in', 'm_sgu_ln_g', 'm_sgu_ln_b', 'm_w_spatial', 'm_b_spatial', 'm_w_up_a', 'm_w_up_b', 'm_w_out', 'm_final_norm_g', 'v_norm_g', 'v_w_in', 'v_sgu_ln_g', 'v_sgu_ln_b', 'v_w_spatial', 'v_b_spatial', 'v_w_up_a', 'v_w_up_b', 'v_w_out', 'v_final_norm_g']
TWIN_OUTPUTS = ['loss', 'grad_x', 'grad_norm_g', 'grad_w_in', 'grad_sgu_ln_g', 'grad_sgu_ln_b', 'grad_w_spatial', 'grad_b_spatial', 'grad_w_up_a', 'grad_w_up_b', 'grad_w_out', 'grad_final_norm_g', 'delta_norm_g', 'delta_w_in', 'delta_sgu_ln_g', 'delta_sgu_ln_b', 'delta_w_spatial', 'delta_b_spatial', 'delta_w_up_a', 'delta_w_up_b', 'delta_w_out', 'delta_final_norm_g', 'new_m_norm_g', 'new_m_w_in', 'new_m_sgu_ln_g', 'new_m_sgu_ln_b', 'new_m_w_spatial', 'new_m_b_spatial', 'new_m_w_up_a', 'new_m_w_up_b', 'new_m_w_out', 'new_m_final_norm_g', 'new_v_norm_g', 'new_v_w_in', 'new_v_sgu_ln_g', 'new_v_sgu_ln_b', 'new_v_w_spatial', 'new_v_b_spatial', 'new_v_w_up_a', 'new_v_w_up_b', 'new_v_w_out', 'new_v_final_norm_g']
TWIN_LEAF_KINDS = {'loss': 'loss', 'grad_x': 'grad_x', 'grad_norm_g': 'grad_w', 'grad_w_in': 'grad_w', 'grad_sgu_ln_g': 'grad_w', 'grad_sgu_ln_b': 'grad_w', 'grad_w_spatial': 'grad_w', 'grad_b_spatial': 'grad_w', 'grad_w_up_a': 'grad_w', 'grad_w_up_b': 'grad_w', 'grad_w_out': 'grad_w', 'grad_final_norm_g': 'grad_w', 'delta_norm_g': 'delta_w', 'delta_w_in': 'delta_w', 'delta_sgu_ln_g': 'delta_w', 'delta_sgu_ln_b': 'delta_w', 'delta_w_spatial': 'delta_w', 'delta_b_spatial': 'delta_w', 'delta_w_up_a': 'delta_w', 'delta_w_up_b': 'delta_w', 'delta_w_out': 'delta_w', 'delta_final_norm_g': 'delta_w', 'new_m_norm_g': 'new_m', 'new_m_w_in': 'new_m', 'new_m_sgu_ln_g': 'new_m', 'new_m_sgu_ln_b': 'new_m', 'new_m_w_spatial': 'new_m', 'new_m_b_spatial': 'new_m', 'new_m_w_up_a': 'new_m', 'new_m_w_up_b': 'new_m', 'new_m_w_out': 'new_m', 'new_m_final_norm_g': 'new_m', 'new_v_norm_g': 'new_v', 'new_v_w_in': 'new_v', 'new_v_sgu_ln_g': 'new_v', 'new_v_sgu_ln_b': 'new_v', 'new_v_w_spatial': 'new_v', 'new_v_b_spatial': 'new_v', 'new_v_w_up_a': 'new_v', 'new_v_w_up_b': 'new_v', 'new_v_w_out': 'new_v', 'new_v_final_norm_g': 'new_v'}


def _forward(args):
    return _fwd_reference(*[args[k] for k in FWD_PARAMS])


def _output_shape():
    out = _jax.eval_shape(lambda: _forward(_fwd_setup_inputs(0)))
    return out.shape, out.dtype

N_MICROBATCH = 1
ADAM_LR = 0.001
ADAM_B1 = 0.9
ADAM_B2 = 0.999
ADAM_EPS = 1e-08
ADAM_WD = 0.01
ADAM_STEP = 10
PER_EXAMPLE_BATCH_AXIS = {'x': 0, 'loss_target': 0}
SHARED_INPUTS = []
_WEIGHT_DTYPES = {'norm_g': _jnp.float32, 'w_in': _jnp.float32, 'sgu_ln_g': _jnp.float32, 'sgu_ln_b': _jnp.float32, 'w_spatial': _jnp.float32, 'b_spatial': _jnp.float32, 'w_up_a': _jnp.float32, 'w_up_b': _jnp.float32, 'w_out': _jnp.float32, 'final_norm_g': _jnp.float32}
MOMENT_SCALE = {'norm_g': 1.038361e-01, 'w_in': 4.324731e-02, 'sgu_ln_g': 4.954246e-02, 'sgu_ln_b': 4.535404e-02, 'w_spatial': 3.136713e-02, 'b_spatial': 3.977120e-02, 'w_up_a': 3.661656e-02, 'w_up_b': 4.915944e-02, 'w_out': 6.196143e-02, 'final_norm_g': 3.218584e+01}


def _to_microbatches(a, axis):
    t = _jnp.moveaxis(a, axis, 0)
    t = t.reshape((N_MICROBATCH, t.shape[0] // N_MICROBATCH) + t.shape[1:])
    return _jnp.moveaxis(t, 1, axis + 1)


def setup_inputs(seed: int = 0) -> dict:
    inp = _fwd_setup_inputs(seed)
    key = _jax.random.fold_in(_jax.random.key(seed), 7919)
    shape, _ = _output_shape()
    out = dict(inp)
    out["loss_target"] = _jax.random.normal(_jax.random.fold_in(key, 0), shape, _jnp.float32)
    for i, name in enumerate(TWIN_WEIGHTS):
        w = inp[name].astype(_jnp.float32)
        if MOMENT_SCALE is None:
            s = _jnp.sqrt(_jnp.mean(_jnp.square(w)) + 1e-30)
        else:
            s = MOMENT_SCALE[name]
        km, kv = _jax.random.split(_jax.random.fold_in(key, i + 1))
        out[name] = w
        out["m_" + name] = s * _jax.random.normal(km, w.shape, _jnp.float32)
        out["v_" + name] = (s * s) * _jax.random.uniform(kv, w.shape, _jnp.float32, 0.5, 1.5)
    if N_MICROBATCH > 1:
        for name, axis in PER_EXAMPLE_BATCH_AXIS.items():
            out[name] = _to_microbatches(out[name], axis)
    return {'x': out['x'], 'norm_g': out['norm_g'], 'w_in': out['w_in'], 'sgu_ln_g': out['sgu_ln_g'], 'sgu_ln_b': out['sgu_ln_b'], 'w_spatial': out['w_spatial'], 'b_spatial': out['b_spatial'], 'w_up_a': out['w_up_a'], 'w_up_b': out['w_up_b'], 'w_out': out['w_out'], 'final_norm_g': out['final_norm_g'], 'loss_target': out['loss_target'], 'm_norm_g': out['m_norm_g'], 'm_w_in': out['m_w_in'], 'm_sgu_ln_g': out['m_sgu_ln_g'], 'm_sgu_ln_b': out['m_sgu_ln_b'], 'm_w_spatial': out['m_w_spatial'], 'm_b_spatial': out['m_b_spatial'], 'm_w_up_a': out['m_w_up_a'], 'm_w_up_b': out['m_w_up_b'], 'm_w_out': out['m_w_out'], 'm_final_norm_g': out['m_final_norm_g'], 'v_norm_g': out['v_norm_g'], 'v_w_in': out['v_w_in'], 'v_sgu_ln_g': out['v_sgu_ln_g'], 'v_sgu_ln_b': out['v_sgu_ln_b'], 'v_w_spatial': out['v_w_spatial'], 'v_b_spatial': out['v_b_spatial'], 'v_w_up_a': out['v_w_up_a'], 'v_w_up_b': out['v_w_up_b'], 'v_w_out': out['v_w_out'], 'v_final_norm_g': out['v_final_norm_g']}


def _loss(weights, diff, rest, loss_target):
    with _jax.named_scope("forward"):
        args = {**rest, TWIN_DIFF_INPUT: diff, **{k: w.astype(_WEIGHT_DTYPES[k]) for k, w in weights.items()}}
        y = _forward(args)
    with _jax.named_scope("loss_head"):
        err = _jnp.square(y.astype(_jnp.float32) - loss_target)
        return 0.5 * _jnp.sum(_jnp.mean(err, axis=-1)) if err.ndim else 0.5 * err


def _adamw(w, g, m, v):
    m = ADAM_B1 * m + (1.0 - ADAM_B1) * g
    v = ADAM_B2 * v + (1.0 - ADAM_B2) * _jnp.square(g)
    m_hat = m / (1.0 - ADAM_B1 ** ADAM_STEP)
    v_hat = v / (1.0 - ADAM_B2 ** ADAM_STEP)
    delta = -ADAM_LR * (m_hat / (_jnp.sqrt(v_hat) + ADAM_EPS) + ADAM_WD * w)
    return delta, m, v


def reference(x, norm_g, w_in, sgu_ln_g, sgu_ln_b, w_spatial, b_spatial, w_up_a, w_up_b, w_out, final_norm_g, loss_target, m_norm_g, m_w_in, m_sgu_ln_g, m_sgu_ln_b, m_w_spatial, m_b_spatial, m_w_up_a, m_w_up_b, m_w_out, m_final_norm_g, v_norm_g, v_w_in, v_sgu_ln_g, v_sgu_ln_b, v_w_spatial, v_b_spatial, v_w_up_a, v_w_up_b, v_w_out, v_final_norm_g):
    given = dict(x=x, norm_g=norm_g, w_in=w_in, sgu_ln_g=sgu_ln_g, sgu_ln_b=sgu_ln_b, w_spatial=w_spatial, b_spatial=b_spatial, w_up_a=w_up_a, w_up_b=w_up_b, w_out=w_out, final_norm_g=final_norm_g, loss_target=loss_target, m_norm_g=m_norm_g, m_w_in=m_w_in, m_sgu_ln_g=m_sgu_ln_g, m_sgu_ln_b=m_sgu_ln_b, m_w_spatial=m_w_spatial, m_b_spatial=m_b_spatial, m_w_up_a=m_w_up_a, m_w_up_b=m_w_up_b, m_w_out=m_w_out, m_final_norm_g=m_final_norm_g, v_norm_g=v_norm_g, v_w_in=v_w_in, v_sgu_ln_g=v_sgu_ln_g, v_sgu_ln_b=v_sgu_ln_b, v_w_spatial=v_w_spatial, v_b_spatial=v_b_spatial, v_w_up_a=v_w_up_a, v_w_up_b=v_w_up_b, v_w_out=v_w_out, v_final_norm_g=v_final_norm_g)
    weights = {n: given[n] for n in TWIN_WEIGHTS}
    shared = {n: given[n] for n in SHARED_INPUTS}
    per_example = {n: given[n] for n in ['x']}
    grad_fn = _jax.value_and_grad(_loss, argnums=(0, 1))

    def one_microbatch(ex, loss_target):
        ex = dict(ex)
        diff = ex.pop(TWIN_DIFF_INPUT)
        return grad_fn(weights, diff, {**shared, **ex}, loss_target)

    if N_MICROBATCH == 1:
        loss, (grad_w, grad_x) = one_microbatch(per_example, given["loss_target"])
    else:
        def body(carry, xs):
            loss_sum, grad_sum = carry
            l_k, (gw_k, gx_k) = one_microbatch(xs[0], xs[1])
            with _jax.named_scope("update"):
                return (loss_sum + l_k, _jax.tree.map(_jnp.add, grad_sum, gw_k)), gx_k

        init = (_jnp.zeros((), _jnp.float32), _jax.tree.map(_jnp.zeros_like, weights))
        (loss, grad_w), grad_x = _jax.lax.scan(body, init, (per_example, given["loss_target"]))
    with _jax.named_scope("update"):
        delta_w, new_m, new_v = {}, {}, {}
        for n in TWIN_WEIGHTS:
            delta_w[n], new_m[n], new_v[n] = _adamw(weights[n], grad_w[n], given["m_" + n], given["v_" + n])
    return (loss, grad_x, *[grad_w[n] for n in TWIN_WEIGHTS], *[delta_w[n] for n in TWIN_WEIGHTS],
            *[new_m[n] for n in TWIN_WEIGHTS], *[new_v[n] for n in TWIN_WEIGHTS])
```

```python
import functools
import math

import jax
import jax.numpy as jnp
from jax import lax
from jax.experimental import pallas as pl
from jax.experimental.pallas import tpu as pltpu

F32 = jnp.float32
BF16 = jnp.bfloat16
MESH = pl.DeviceIdType.MESH

N_DEV = 8
N_HEADS = 8
HEAD_DIM = 64
D_SB = N_HEADS * HEAD_DIM
N_GROUPS = 8
GROUP_DIM = 64
D_SGU = N_GROUPS * GROUP_DIM
SGU_CHUNK = 128
CHUNK = 64
EPS = 1e-6
LANES = 128
N_PAIRS = N_HEADS // 2
QKV_COLS = 3 * D_SB
ATT_BLOCK = 256

ADAM_LR = 0.001
ADAM_B1 = 0.9
ADAM_B2 = 0.999
ADAM_EPS = 1e-08
ADAM_WD = 0.01
ADAM_STEP = 10

VMEM_LIMIT = 56 * 1024 * 1024


def _cparams(sem=None, vmem=VMEM_LIMIT):
    return pltpu.CompilerParams(dimension_semantics=sem, vmem_limit_bytes=vmem)


def _dot(a, b):
    return jnp.dot(a, b, preferred_element_type=F32)


def _dot_nt(a, b):
    return lax.dot_general(a, b, (((1,), (1,)), ((), ())), preferred_element_type=F32)


def _dot_tn(a, b):
    return lax.dot_general(a, b, (((0,), (0,)), ((), ())), preferred_element_type=F32)


def _split_hi_lo(a):
    hi = a.astype(BF16)
    lo = (a - hi.astype(F32)).astype(BF16)
    return hi, lo


def _dot_hilo(a, b_bf16):
    hi, lo = _split_hi_lo(a)
    return _dot(hi, b_bf16) + _dot(lo, b_bf16)


def _sigmoid(x):
    return 1.0 / (1.0 + jnp.exp(-x))


_GELU_C = math.sqrt(2.0 / math.pi)


def _gelu_and_grad(x):
    x2 = x * x
    inner = _GELU_C * (x + 0.044715 * (x2 * x))
    t = jnp.tanh(inner)
    cdf = 0.5 * (1.0 + t)
    g = x * cdf
    dg = cdf + x * (0.5 * (1.0 - t * t)) * (_GELU_C * (1.0 + 3.0 * 0.044715 * x2))
    return g, dg


def _coords():
    return lax.axis_index("x"), lax.axis_index("y"), lax.axis_index("c")


def _dev_index(px, py, pc):
    return 4 * px + 2 * py + pc


def _allgather(blocks, name):
    n = len(blocks)

    def body(*refs):
        ins, outs = refs[:n], refs[n:2 * n]
        send_sems, recv_sems, local_sems = refs[2 * n:]
        x, y, c = _coords()
        me, sibling = (x, y, c), (x, y, 1 - c)
        chips = [(1 - x, y), (x, 1 - y), (1 - x, 1 - y)]

        def copy(a, k, block, to, src=None):
            dst = outs[a].at[_dev_index(*block)]
            return pltpu.make_async_remote_copy(
                src_ref=dst if src is None else src, dst_ref=dst,
                send_sem=send_sems.at[a, k], recv_sem=recv_sems.at[a, k],
                device_id=to, device_id_type=MESH)

        mine = [pltpu.make_async_copy(ins[a], outs[a].at[_dev_index(*me)], local_sems.at[a]) for a in range(n)]
        for cp in mine:
            cp.start()
        first = []
        for a in range(n):
            first.append(copy(a, 0, me, sibling, src=ins[a]))
            first += [copy(a, 1 + j, me, (*chip, c), src=ins[a]) for j, chip in enumerate(chips)]
        for cp in first:
            cp.start()
        passed = []
        for j, chip in enumerate(chips):
            for a in range(n):
                copy(a, 1 + j, (*chip, c), me).wait_recv()
                fwd = copy(a, 4 + j, (*chip, c), sibling)
                fwd.start()
                passed.append(fwd)
        for a in range(n):
            copy(a, 0, sibling, me).wait_recv()
            for j, chip in enumerate(chips):
                copy(a, 4 + j, (*chip, 1 - c), me).wait_recv()
        for cp in first + passed:
            cp.wait_send()
        for cp in mine:
            cp.wait()

    any_spec = pl.BlockSpec(memory_space=pl.ANY)
    return pl.pallas_call(
        body, name=name,
        out_shape=[jax.ShapeDtypeStruct((N_DEV,) + b.shape, b.dtype) for b in blocks],
        in_specs=[any_spec] * n, out_specs=[any_spec] * n,
        scratch_shapes=[pltpu.SemaphoreType.DMA((n, 7)), pltpu.SemaphoreType.DMA((n, 7)),
                        pltpu.SemaphoreType.DMA((n,))],
    )(*blocks)


def _push_sibling(arrs, name):
    n = len(arrs)

    def body(*refs):
        ins, outs = refs[:n], refs[n:2 * n]
        send_sems, recv_sems = refs[2 * n:]
        x, y, c = _coords()
        sibling = (x, y, 1 - c)
        copies = []
        for a in range(n):
            for k in range(4):
                copies.append(pltpu.make_async_remote_copy(
                    src_ref=ins[a].at[k, 1 - c], dst_ref=outs[a].at[k],
                    send_sem=send_sems.at[a, k], recv_sem=recv_sems.at[a, k],
                    device_id=sibling, device_id_type=MESH))
        for cp in copies:
            cp.start()
        for cp in copies:
            cp.wait()

    any_spec = pl.BlockSpec(memory_space=pl.ANY)
    return pl.pallas_call(
        body, name=name,
        out_shape=[jax.ShapeDtypeStruct((4,) + a.shape[2:], a.dtype) for a in arrs],
        in_specs=[any_spec] * n, out_specs=[any_spec] * n,
        scratch_shapes=[pltpu.SemaphoreType.DMA((n, 4)), pltpu.SemaphoreType.DMA((n, 4))],
    )(*arrs)


def _push_chips(arrs, name):
    n = len(arrs)

    def body(*refs):
        ins, outs = refs[:n], refs[n:2 * n]
        send_sems, recv_sems = refs[2 * n:]
        x, y, c = _coords()
        chips = [(1 - x, y), (x, 1 - y), (1 - x, 1 - y)]
        copies = []
        for a in range(n):
            for r, (px, py) in enumerate(chips):
                copies.append(pltpu.make_async_remote_copy(
                    src_ref=ins[a].at[2 * px + py], dst_ref=outs[a].at[r],
                    send_sem=send_sems.at[a, r], recv_sem=recv_sems.at[a, r],
                    device_id=(px, py, c), device_id_type=MESH))
        for cp in copies:
            cp.start()
        for cp in copies:
            cp.wait()

    any_spec = pl.BlockSpec(memory_space=pl.ANY)
    return pl.pallas_call(
        body, name=name,
        out_shape=[jax.ShapeDtypeStruct((3,) + a.shape[1:], a.dtype) for a in arrs],
        in_specs=[any_spec] * n, out_specs=[any_spec] * n,
        scratch_shapes=[pltpu.SemaphoreType.DMA((n, 3)), pltpu.SemaphoreType.DMA((n, 3))],
    )(*arrs)


def _chip_partial_sum(own, land, core, name):
    _, _, rows, cols = own.shape
    tr = min(rows, 256)

    def body(core_ref, own_ref, land_ref, out_ref):
        del core_ref
        out_ref[...] = (own_ref[...].astype(F32) + land_ref[...].astype(F32)).astype(out_ref.dtype)

    return pl.pallas_call(
        body, name=name,
        out_shape=jax.ShapeDtypeStruct((4, rows, cols), own.dtype),
        grid_spec=pltpu.PrefetchScalarGridSpec(
            num_scalar_prefetch=1, grid=(4, rows // tr),
            in_specs=[pl.BlockSpec((None, None, tr, cols), lambda k, r, core: (k, core[0], r, 0)),
                      pl.BlockSpec((None, tr, cols), lambda k, r, core: (k, r, 0))],
            out_specs=pl.BlockSpec((None, tr, cols), lambda k, r, core: (k, r, 0))),
        compiler_params=_cparams(("parallel", "parallel")),
    )(core, own, land)


def _adamw_math(w, g, m, v):
    m = ADAM_B1 * m + (1.0 - ADAM_B1) * g
    v = ADAM_B2 * v + (1.0 - ADAM_B2) * (g * g)
    m_hat = m / (1.0 - ADAM_B1 ** ADAM_STEP)
    v_hat = v / (1.0 - ADAM_B2 ** ADAM_STEP)
    delta = -ADAM_LR * (m_hat / (jnp.sqrt(v_hat) + ADAM_EPS) + ADAM_WD * w)
    return delta, m, v


def _adam_shard(cp, land, chip, w, m, v, name):
    rows, cols = w.shape
    tr = min(rows, 256)

    def body(chip_ref, cp_ref, land_ref, w_ref, m_ref, v_ref, g_out, d_out, m_out, v_out):
        del chip_ref
        g = cp_ref[...].astype(F32)
        for r in range(3):
            g = g + land_ref[r].astype(F32)
        delta, m_new, v_new = _adamw_math(w_ref[...], g, m_ref[...], v_ref[...])
        g_out[...] = g
        d_out[...] = delta
        m_out[...] = m_new
        v_out[...] = v_new

    tile = pl.BlockSpec((tr, cols), lambda r, chip: (r, 0))
    out = jax.ShapeDtypeStruct((rows, cols), F32)
    return pl.pallas_call(
        body, name=name, out_shape=[out] * 4,
        grid_spec=pltpu.PrefetchScalarGridSpec(
            num_scalar_prefetch=1, grid=(rows // tr,),
            in_specs=[pl.BlockSpec((None, tr, cols), lambda r, chip: (chip[0], r, 0)),
                      pl.BlockSpec((3, tr, cols), lambda r, chip: (0, r, 0)),
                      tile, tile, tile],
            out_specs=[tile] * 4),
        compiler_params=_cparams(("parallel",)),
    )(chip, cp, land, w, m, v)


def _adam_small(parts, w, m, v, name):
    rows, cols = w.shape
    tr = rows // 4

    def body(p_ref, w_ref, m_ref, v_ref, g_out, d_out, m_out, v_out):
        g = p_ref[0]
        for d in range(1, N_DEV):
            g = g + p_ref[d]
        delta, m_new, v_new = _adamw_math(w_ref[...], g, m_ref[...], v_ref[...])
        g_out[...] = g
        d_out[...] = delta
        m_out[...] = m_new
        v_out[...] = v_new

    tile = pl.BlockSpec((tr, cols), lambda r: (r, 0))
    out = jax.ShapeDtypeStruct((rows, cols), F32)
    return pl.pallas_call(
        body, name=name, out_shape=[out] * 4, grid=(rows // tr,),
        in_specs=[pl.BlockSpec((N_DEV, tr, cols), lambda r: (0, r, 0)), tile, tile, tile],
        out_specs=[tile] * 4,
        compiler_params=_cparams(("parallel",)),
    )(parts, w, m, v)


def _rmsnorm_fwd(x, g, name):
    s, d = x.shape
    ts = 512

    def body(x_ref, g_ref, h_ref, ht_ref):
        xv = x_ref[...]
        rstd = lax.rsqrt(jnp.mean(xv * xv, axis=-1, keepdims=True) + EPS)
        h = xv * rstd * g_ref[...]
        h_ref[...] = h.astype(BF16)
        ht_ref[...] = h.T.astype(BF16)

    return pl.pallas_call(
        body, name=name,
        out_shape=[jax.ShapeDtypeStruct((s, d), BF16), jax.ShapeDtypeStruct((d, s), BF16)],
        grid=(s // ts,),
        in_specs=[pl.BlockSpec((ts, d), lambda i: (i, 0)), pl.BlockSpec((1, d), lambda i: (0, 0))],
        out_specs=[pl.BlockSpec((ts, d), lambda i: (i, 0)), pl.BlockSpec((d, ts), lambda i: (0, i))],
        compiler_params=_cparams(("parallel",)),
    )(x, g)


def _inproj(h, w, col0, ncols, out_dtype, name):
    s, d = h.shape
    tm, tn = min(s, 1024), 512
    j0 = col0 // tn

    def body(h_ref, w_ref, o_ref):
        o_ref[...] = _dot(h_ref[...], w_ref[...]).astype(o_ref.dtype)

    return pl.pallas_call(
        body, name=name, out_shape=jax.ShapeDtypeStruct((s, ncols), out_dtype),
        grid=(s // tm, ncols // tn),
        in_specs=[pl.BlockSpec((tm, d), lambda i, j: (i, 0)), pl.BlockSpec((d, tn), lambda i, j: (0, j + j0))],
        out_specs=pl.BlockSpec((tm, tn), lambda i, j: (i, j)),
        compiler_params=_cparams(("parallel", "arbitrary")),
    )(h, w)


def _dw_in(ht, dproj, name):
    d, s = ht.shape
    n = dproj.shape[1]
    tn, tk = 512, min(s, 1024)
    nk = s // tk

    def body(a_ref, b_ref, o_ref, acc_ref):
        k = pl.program_id(1)

        @pl.when(k == 0)
        def _():
            acc_ref[...] = jnp.zeros_like(acc_ref)

        acc_ref[...] += _dot(a_ref[...], b_ref[...])

        @pl.when(k == nk - 1)
        def _():
            o_ref[...] = acc_ref[...].astype(o_ref.dtype)

    return pl.pallas_call(
        body, name=name, out_shape=jax.ShapeDtypeStruct((d, n), BF16),
        grid=(n // tn, nk),
        in_specs=[pl.BlockSpec((d, tk), lambda j, k: (0, k)), pl.BlockSpec((tk, tn), lambda j, k: (k, j))],
        out_specs=pl.BlockSpec((d, tn), lambda j, k: (0, j)),
        scratch_shapes=[pltpu.VMEM((d, tn), F32)],
        compiler_params=_cparams(("parallel", "arbitrary")),
    )(ht, dproj)


def _dh_and_grad_x(dproj, w, x, g, dx2, name):
    s, n = dproj.shape
    d = w.shape[0]
    tm, tk = 512, 512
    nk = n // tk
    nm = s // tm

    def body(a_ref, w_ref, x_ref, g_ref, dx2_ref, gx_ref, dg_ref, acc_ref):
        i, k = pl.program_id(0), pl.program_id(1)

        @pl.when(k == 0)
        def _():
            acc_ref[...] = jnp.zeros_like(acc_ref)

        @pl.when((i == 0) & (k == 0))
        def _():
            dg_ref[...] = jnp.zeros_like(dg_ref)

        acc_ref[...] += _dot_nt(a_ref[...], w_ref[...])

        @pl.when(k == nk - 1)
        def _():
            dh = acc_ref[...]
            xv = x_ref[...]
            rstd = lax.rsqrt(jnp.mean(xv * xv, axis=-1, keepdims=True) + EPS)
            xhat = xv * rstd
            dg_ref[...] += jnp.sum((dh * xhat).reshape(tm // 8, 8, d), axis=0)
            dxh = dh * g_ref[...]
            gx_ref[...] = dx2_ref[...] + rstd * (dxh - xhat * jnp.mean(dxh * xhat, axis=-1, keepdims=True))

    return pl.pallas_call(
        body, name=name,
        out_shape=[jax.ShapeDtypeStruct((s, d), F32), jax.ShapeDtypeStruct((8, d), F32)],
        grid=(nm, nk),
        in_specs=[pl.BlockSpec((tm, tk), lambda i, k: (i, k)), pl.BlockSpec((d, tk), lambda i, k: (0, k)),
                  pl.BlockSpec((tm, d), lambda i, k: (i, 0)), pl.BlockSpec((1, d), lambda i, k: (0, 0)),
                  pl.BlockSpec((tm, d), lambda i, k: (i, 0))],
        out_specs=[pl.BlockSpec((tm, d), lambda i, k: (i, 0)), pl.BlockSpec((8, d), lambda i, k: (0, 0))],
        scratch_shapes=[pltpu.VMEM((tm, d), F32)],
        compiler_params=_cparams(("arbitrary", "arbitrary")),
    )(dproj, w, x, g, dx2)


def _log_sigmoids(z):
    l1p = jnp.log(1.0 + jnp.exp(-jnp.abs(z)))
    ls = jnp.minimum(z, 0.0) - l1p
    return ls, ls - z


def _strict_lower_ones(n):
    row = lax.broadcasted_iota(jnp.int32, (n, n), 0)
    col = lax.broadcasted_iota(jnp.int32, (n, n), 1)
    return row, col, (row > col).astype(BF16)


def _attn_fwd(qkv, name):
    s = qkv.shape[0]
    tb = ATT_BLOCK
    nq = s // tb

    def body(q_ref, k_ref, v_ref, o_ref, rs_ref, acc_ref, r_ref, rsv_ref):
        i, hh = pl.program_id(1), pl.program_id(2)
        lane = lax.broadcasted_iota(jnp.int32, (tb, LANES), 1)
        hmask = (lane // HEAD_DIM) == hh
        qm = jnp.where(hmask, q_ref[...], jnp.zeros((), BF16)) * jnp.asarray(HEAD_DIM ** -0.5, BF16)
        row, col, tri = _strict_lower_ones(tb)
        acc_ref[...] = jnp.zeros_like(acc_ref)
        r_ref[...] = jnp.zeros_like(r_ref)
        rsv_ref[...] = jnp.zeros_like(rsv_ref)

        def block(j, diag):
            rows = pl.ds(pl.multiple_of(j * tb, tb), tb)
            k2, v2 = k_ref[rows, :], v_ref[rows, :]
            z = _dot_nt(qm, k2)
            ls, lk = _log_sigmoids(z)
            if diag:
                lk = jnp.where(col < row, lk, 0.0)
            r = r_ref[...]
            logw = ls + _dot_hilo(lk, tri) + jnp.tile(r, (1, tb // LANES))
            w = jnp.exp(logw)
            if diag:
                w = jnp.where(col < row, w, 0.0)
            acc_ref[...] += _dot(w.astype(BF16), v2)
            rsv_ref[...] = jnp.where(lane == j, r, rsv_ref[...])
            r_ref[...] = r + jnp.sum(lk, axis=1, keepdims=True)

        block(i, True)

        def step(jj, carry):
            block(i - 1 - jj, False)
            return carry

        lax.fori_loop(0, i, step, 0)
        res = acc_ref[...]

        @pl.when(hh == 0)
        def _():
            o_ref[...] = jnp.where(hmask, res, 0.0)

        @pl.when(hh == 1)
        def _():
            o_ref[...] = jnp.where(hmask, res, o_ref[...])

        rs_ref[...] = rsv_ref[...]

    return pl.pallas_call(
        body, name=name,
        out_shape=[jax.ShapeDtypeStruct((s, D_SB), F32), jax.ShapeDtypeStruct((N_HEADS, s, LANES), F32)],
        grid=(N_PAIRS, nq, 2),
        in_specs=[pl.BlockSpec((tb, LANES), lambda p, i, hh: (i, 3 * p)),
                  pl.BlockSpec((s, LANES), lambda p, i, hh: (0, 3 * p + 1)),
                  pl.BlockSpec((s, LANES), lambda p, i, hh: (0, 3 * p + 2))],
        out_specs=[pl.BlockSpec((tb, LANES), lambda p, i, hh: (i, p)),
                   pl.BlockSpec((None, tb, LANES), lambda p, i, hh: (2 * p + hh, i, 0))],
        scratch_shapes=[pltpu.VMEM((tb, LANES), F32), pltpu.VMEM((tb, LANES), F32), pltpu.VMEM((tb, LANES), F32)],
        compiler_params=_cparams(("parallel", "arbitrary", "arbitrary")),
    )(qkv, qkv, qkv)


def _attn_bwd(qkv, do, rs, dproj, name):
    s = qkv.shape[0]
    tb = ATT_BLOCK
    nq = s // tb
    scale = HEAD_DIM ** -0.5

    def body(qkv_ref, do_ref, rs_ref, dproj_hbm, out_ref, dq_acc, dk_acc, dv_acc, dqi_ref, pc_ref):
        del dproj_hbm
        hh = pl.program_id(1)
        lane = lax.broadcasted_iota(jnp.int32, (tb, LANES), 1)
        hmask = (lane // HEAD_DIM) == hh
        row, col, tri = _strict_lower_ones(tb)
        triu = (row < col).astype(BF16)
        zero = jnp.zeros((), BF16)

        @pl.when(hh == 0)
        def _():
            dq_acc[...] = jnp.zeros_like(dq_acc)
            dk_acc[...] = jnp.zeros_like(dk_acc)
            dv_acc[...] = jnp.zeros_like(dv_acc)

        def qblock(i, carry):
            qrows = pl.ds(pl.multiple_of(i * tb, tb), tb)
            qm = jnp.where(hmask, qkv_ref[qrows, 0:LANES], zero) * jnp.asarray(scale, BF16)
            dom = jnp.where(hmask, do_ref[qrows, :], zero)
            rs_i = rs_ref[qrows, :]
            dqi_ref[...] = jnp.zeros_like(dqi_ref)
            pc_ref[...] = jnp.zeros_like(pc_ref)

            def block(j, diag):
                krows = pl.ds(pl.multiple_of(j * tb, tb), tb)
                k2 = qkv_ref[krows, LANES:2 * LANES]
                v2 = qkv_ref[krows, 2 * LANES:3 * LANES]
                z = _dot_nt(qm, k2)
                ls, lk = _log_sigmoids(z)
                if diag:
                    lk = jnp.where(col < row, lk, 0.0)
                r = jnp.sum(jnp.where(lane == j, rs_i, 0.0), axis=1, keepdims=True)
                w = jnp.exp(ls + _dot_hilo(lk, tri) + r)
                if diag:
                    w = jnp.where(col < row, w, 0.0)
                g = w * _dot_nt(dom, v2)
                pc = pc_ref[...]
                p = _dot(g.astype(BF16), triu) + jnp.tile(pc, (1, tb // LANES))
                sig = jnp.exp(ls)
                dz = g - sig * (g + p)
                if diag:
                    dz = jnp.where(col < row, dz, 0.0)
                dzb = dz.astype(BF16)
                pc_ref[...] = pc + jnp.sum(g, axis=1, keepdims=True)
                dqi_ref[...] += _dot(dzb, jnp.where(hmask, k2, zero))
                dk_acc[krows, :] += _dot_tn(dzb, qm)
                dv_acc[krows, :] += _dot_tn(w.astype(BF16), dom)

            def step(j, c):
                block(j, False)
                return c

            lax.fori_loop(0, i, step, 0)
            block(i, True)
            dq_acc[qrows, :] += dqi_ref[...] * scale
            return carry

        lax.fori_loop(0, nq, qblock, 0)

        @pl.when(hh == 1)
        def _():
            out_ref[:, 0:LANES] = dq_acc[...].astype(BF16)
            out_ref[:, LANES:2 * LANES] = dk_acc[...].astype(BF16)
            out_ref[:, 2 * LANES:3 * LANES] = dv_acc[...].astype(BF16)

    return pl.pallas_call(
        body, name=name,
        out_shape=jax.ShapeDtypeStruct(dproj.shape, BF16),
        grid=(N_PAIRS, 2),
        in_specs=[pl.BlockSpec((s, 3 * LANES), lambda p, hh: (0, p)),
                  pl.BlockSpec((s, LANES), lambda p, hh: (0, p)),
                  pl.BlockSpec((None, s, LANES), lambda p, hh: (2 * p + hh, 0, 0)),
                  pl.BlockSpec(memory_space=pl.ANY)],
        out_specs=pl.BlockSpec((s, 3 * LANES), lambda p, hh: (0, p)),
        scratch_shapes=[pltpu.VMEM((s, LANES), F32), pltpu.VMEM((s, LANES), F32), pltpu.VMEM((s, LANES), F32),
                        pltpu.VMEM((tb, LANES), F32), pltpu.VMEM((tb, LANES), F32)],
        input_output_aliases={3: 0},
        compiler_params=_cparams(("parallel", "arbitrary")),
    )(qkv, do, rs, dproj)


def _mid(o, rest, x, tgt, wua, wub, wout, fg, lng, lnb, wsp, bfull, gavg, name):
    s, d = x.shape
    ts = 256
    nt = s // ts
    nchunk = ts // SGU_CHUNK
    n_rest = rest.shape[1]

    def body(o_ref, rest_ref, x_ref, t_ref, wua_ref, wub_ref, wout_ref, fg_ref, lng_ref, lnb_ref, wsp_ref, bfull_ref,
             gavg_ref, loss_ref, dx2_ref, do_ref, dproj_ref, dwua_ref, dwub_ref, dwout_ref, dfg_ref, dlng_ref,
             dlnb_ref, dwsp_ref, dbfull_ref):
        step = pl.program_id(0)

        @pl.when(step == 0)
        def _():
            for ref in (loss_ref, dwua_ref, dwub_ref, dwout_ref, dfg_ref, dlng_ref, dlnb_ref, dwsp_ref, dbfull_ref):
                ref[...] = jnp.zeros_like(ref)

        gavg = gavg_ref[...]

        def gmean(a):
            return _dot_hilo(a, gavg)

        def colsum8(a):
            return jnp.sum(a.reshape(ts // 8, 8, a.shape[1]), axis=0)

        z_a = rest_ref[:, 0:512]
        u_b = rest_ref[:, 512:1024]
        v_b = rest_ref[:, 1024:1536]
        z_b = rest_ref[:, 1536:2048]
        g_a = rest_ref[:, 2048:2048 + d]
        g_b = rest_ref[:, 2048 + d:2048 + 2 * d]
        ov = o_ref[...]
        sa = _sigmoid(z_a)
        silu_a = z_a * sa
        y_a = ov * silu_a
        ug, dug_du = _gelu_and_grad(u_b)
        vg, dvg_dv = _gelu_and_grad(v_b)
        mu = gmean(vg)
        cen = vg - mu
        rstd_g = lax.rsqrt(gmean(cen * cen) + EPS)
        vhat = cen * rstd_g
        vn = vhat * lng_ref[...] + lnb_ref[...]
        vnb = vn.astype(BF16)

        t_idx = lax.broadcasted_iota(jnp.int32, (SGU_CHUNK, SGU_CHUNK), 0)
        s_idx = lax.broadcasted_iota(jnp.int32, (SGU_CHUNK, SGU_CHUNK), 1)
        causal = (s_idx // CHUNK) <= (t_idx // CHUNK)
        wm = [jnp.where(causal, wsp_ref[g], 0.0) for g in range(N_GROUPS)]
        wmb = [w.astype(BF16) for w in wm]
        wmtb = [w.T.astype(BF16) for w in wm]
        lane = lax.broadcasted_iota(jnp.int32, (SGU_CHUNK, LANES), 1)
        first = lane < GROUP_DIM
        bfull = bfull_ref[...]

        mixed_rows = []
        for n in range(nchunk):
            r0, r1 = n * SGU_CHUNK, (n + 1) * SGU_CHUNK
            pieces = []
            for p in range(N_GROUPS // 2):
                blk = vnb[r0:r1, p * LANES:(p + 1) * LANES]
                pieces.append(jnp.where(first, _dot(wmb[2 * p], blk), _dot(wmb[2 * p + 1], blk)))
            mixed_rows.append(jnp.concatenate(pieces, axis=1) + bfull)
        mixed = jnp.concatenate(mixed_rows, axis=0)
        sg = ug * mixed
        sb = _sigmoid(z_b)
        silu_b = z_b * sb
        y_b = sg * silu_b
        y_ab = y_a.astype(BF16)
        y_bb = y_b.astype(BF16)
        p_a = _dot(y_ab, wua_ref[...])
        p_b = _dot(y_bb, wub_ref[...])
        ga_s = _sigmoid(g_a)
        gb_s = _sigmoid(g_b)
        merged_b = (ga_s * p_a + gb_s * p_b).astype(BF16)
        x2 = x_ref[...] + _dot(merged_b, wout_ref[...])
        rstd = lax.rsqrt(jnp.mean(x2 * x2, axis=-1, keepdims=True) + EPS)
        xhat = x2 * rstd
        fg_v = fg_ref[...]
        diff = xhat * fg_v - t_ref[...]
        loss_ref[...] += 0.5 * jnp.sum(jnp.sum(diff * diff, axis=-1, keepdims=True) * (1.0 / d))

        dy = diff * (1.0 / d)
        dfg_ref[...] += colsum8(dy * xhat)
        dxh = dy * fg_v
        dx2 = rstd * (dxh - xhat * jnp.mean(dxh * xhat, axis=-1, keepdims=True))
        dx2_ref[...] = dx2
        dx2b = dx2.astype(BF16)
        dwout_ref[...] += _dot_tn(merged_b, dx2b)
        dmerged = _dot_nt(dx2b, wout_ref[...])
        dp_a = dmerged * ga_s
        dp_b = dmerged * gb_s
        dproj_ref[:, QKV_COLS + 2048:QKV_COLS + 2048 + d] = (dmerged * p_a * (ga_s * (1.0 - ga_s))).astype(BF16)
        dproj_ref[:, QKV_COLS + 2048 + d:QKV_COLS + 2048 + 2 * d] = (dmerged * p_b * (gb_s * (1.0 - gb_s))).astype(BF16)
        dp_ab = dp_a.astype(BF16)
        dp_bb = dp_b.astype(BF16)
        dwua_ref[...] += _dot_tn(y_ab, dp_ab)
        dwub_ref[...] += _dot_tn(y_bb, dp_bb)
        dy_a = _dot_nt(dp_ab, wua_ref[...])
        dy_b = _dot_nt(dp_bb, wub_ref[...])
        do_ref[...] = (dy_a * silu_a).astype(BF16)
        dproj_ref[:, QKV_COLS:QKV_COLS + 512] = (dy_a * ov * (sa * (1.0 + z_a * (1.0 - sa)))).astype(BF16)
        dsg = dy_b * silu_b
        dproj_ref[:, QKV_COLS + 1536:QKV_COLS + 2048] = (dy_b * sg * (sb * (1.0 + z_b * (1.0 - sb)))).astype(BF16)
        dproj_ref[:, QKV_COLS + 512:QKV_COLS + 1024] = (dsg * mixed * dug_du).astype(BF16)
        dmixed = dsg * ug
        dmb = dmixed.astype(BF16)
        zero = jnp.zeros((), BF16)
        dvn_rows = []
        db = jnp.zeros((SGU_CHUNK, D_SGU), F32)
        for n in range(nchunk):
            r0, r1 = n * SGU_CHUNK, (n + 1) * SGU_CHUNK
            db = db + dmixed[r0:r1, :]
            pieces = []
            for p in range(N_GROUPS // 2):
                cols = slice(p * LANES, (p + 1) * LANES)
                dm_blk = dmb[r0:r1, cols]
                vn_blk = vnb[r0:r1, cols]
                dwsp_ref[2 * p] += _dot_nt(jnp.where(first, dm_blk, zero), vn_blk)
                dwsp_ref[2 * p + 1] += _dot_nt(jnp.where(first, zero, dm_blk), vn_blk)
                pieces.append(jnp.where(first, _dot(wmtb[2 * p], dm_blk), _dot(wmtb[2 * p + 1], dm_blk)))
            dvn_rows.append(jnp.concatenate(pieces, axis=1))
        dbfull_ref[...] += db
        dvn = jnp.concatenate(dvn_rows, axis=0)
        dlng_ref[...] += colsum8(dvn * vhat)
        dlnb_ref[...] += colsum8(dvn)
        dvhat = dvn * lng_ref[...]
        dcen = rstd_g * (dvhat - gmean(dvhat) - vhat * gmean(dvhat * vhat))
        dproj_ref[:, QKV_COLS + 1024:QKV_COLS + 1536] = (dcen * dvg_dv).astype(BF16)

        @pl.when(step == nt - 1)
        def _():
            for g in range(N_GROUPS):
                dwsp_ref[g] = jnp.where(causal, dwsp_ref[g], 0.0)

    def tile(cols):
        return pl.BlockSpec((ts, cols), lambda i: (i, 0))

    def whole(shape):
        return pl.BlockSpec(shape, lambda i: (0,) * len(shape))

    out_shapes = [
        jax.ShapeDtypeStruct((8, LANES), F32),
        jax.ShapeDtypeStruct((s, d), F32),
        jax.ShapeDtypeStruct((s, D_SB), BF16),
        jax.ShapeDtypeStruct((s, QKV_COLS + n_rest), BF16),
        jax.ShapeDtypeStruct((D_SB, d), F32),
        jax.ShapeDtypeStruct((D_SGU, d), F32),
        jax.ShapeDtypeStruct((d, d), F32),
        jax.ShapeDtypeStruct((8, d), F32),
        jax.ShapeDtypeStruct((8, D_SGU), F32),
        jax.ShapeDtypeStruct((8, D_SGU), F32),
        jax.ShapeDtypeStruct((N_GROUPS, SGU_CHUNK, SGU_CHUNK), F32),
        jax.ShapeDtypeStruct((SGU_CHUNK, D_SGU), F32),
    ]
    out_specs = [whole((8, LANES)), tile(d), tile(D_SB), tile(QKV_COLS + n_rest), whole((D_SB, d)), whole((D_SGU, d)),
                 whole((d, d)), whole((8, d)), whole((8, D_SGU)), whole((8, D_SGU)),
                 whole((N_GROUPS, SGU_CHUNK, SGU_CHUNK)), whole((SGU_CHUNK, D_SGU))]
    in_specs = [tile(D_SB), tile(n_rest), tile(d), tile(d), whole((D_SB, d)), whole((D_SGU, d)), whole((d, d)),
                whole((1, d)), whole((1, D_SGU)), whole((1, D_SGU)), whole((N_GROUPS, SGU_CHUNK, SGU_CHUNK)),
                whole((SGU_CHUNK, D_SGU)), whole((D_SGU, D_SGU))]
    return pl.pallas_call(
        body, name=name, out_shape=out_shapes, grid=(nt,), in_specs=in_specs, out_specs=out_specs,
        compiler_params=_cparams(("arbitrary",)),
    )(o, rest, x, tgt, wua, wub, wout, fg, lng, lnb, wsp, bfull, gavg)


def _small_reduce(dfg8, dng8, dlng8, dlnb8, dbfull, name):
    d = dfg8.shape[1]

    def body(dfg_ref, dng_ref, dlng_ref, dlnb_ref, dbfull_ref, fg_out, ng_out, lng_out, lnb_out, b_out):
        fg_out[...] = jnp.sum(dfg_ref[...], axis=0, keepdims=True)
        ng_out[...] = jnp.sum(dng_ref[...], axis=0, keepdims=True)
        lng_out[...] = jnp.sum(dlng_ref[...], axis=0, keepdims=True)
        lnb_out[...] = jnp.sum(dlnb_ref[...], axis=0, keepdims=True)
        grp = lax.broadcasted_iota(jnp.int32, (D_SGU, LANES), 0) // GROUP_DIM
        col = lax.broadcasted_iota(jnp.int32, (D_SGU, LANES), 1)
        sel = (grp == col).astype(BF16)
        a = dbfull_ref[...]
        hi, lo = _split_hi_lo(a)
        lo2 = (a - hi.astype(F32) - lo.astype(F32)).astype(BF16)
        b_out[...] = _dot(hi, sel) + _dot(lo, sel) + _dot(lo2, sel)

    return pl.pallas_call(
        body, name=name,
        out_shape=[jax.ShapeDtypeStruct((1, d), F32), jax.ShapeDtypeStruct((1, d), F32),
                   jax.ShapeDtypeStruct((1, D_SGU), F32), jax.ShapeDtypeStruct((1, D_SGU), F32),
                   jax.ShapeDtypeStruct((SGU_CHUNK, LANES), F32)],
        compiler_params=_cparams(),
    )(dfg8, dng8, dlng8, dlnb8, dbfull)


def _to_perm_cols(w):
    parts = []
    for p in range(N_PAIRS):
        for t in range(3):
            c0 = t * D_SB + p * LANES
            parts.append(w[..., c0:c0 + LANES])
    parts.append(w[..., QKV_COLS:])
    return jnp.concatenate(parts, axis=-1)


def _from_perm_cols(w):
    qkv = [[None] * N_PAIRS for _ in range(3)]
    for p in range(N_PAIRS):
        for t in range(3):
            c0 = (3 * p + t) * LANES
            qkv[t][p] = w[..., c0:c0 + LANES]
    parts = [qkv[t][p] for t in range(3) for p in range(N_PAIRS)]
    parts.append(w[..., QKV_COLS:])
    return jnp.concatenate(parts, axis=-1)


def _block_major_cols(w):
    r, n = w.shape
    return jnp.transpose(w.reshape(r, N_DEV, n // N_DEV), (1, 0, 2))


def _from_block_major_cols(w):
    nb, r, c = w.shape
    return jnp.transpose(w, (1, 0, 2)).reshape(r, nb * c)


def kernel(x, norm_g, w_in, sgu_ln_g, sgu_ln_b, w_spatial, b_spatial, w_up_a, w_up_b, w_out, final_norm_g, loss_target, m_norm_g, m_w_in, m_sgu_ln_g, m_sgu_ln_b, m_w_spatial, m_b_spatial, m_w_up_a, m_w_up_b, m_w_out, m_final_norm_g, v_norm_g, v_w_in, v_sgu_ln_g, v_sgu_ln_b, v_w_spatial, v_b_spatial, v_w_up_a, v_w_up_b, v_w_out, v_final_norm_g):
    s, d = x.shape[1], x.shape[2]
    xs = x[0]
    tgt = loss_target[0]
    cx, cy, cc = _coords()
    core = jnp.reshape(cc, (1,)).astype(jnp.int32)
    chip = jnp.reshape(2 * cx + cy, (1,)).astype(jnp.int32)

    g_win, g_wua, g_wub, g_wout = _allgather(
        [w_in[0].astype(BF16), w_up_a[0].astype(BF16), w_up_b[0].astype(BF16), w_out[0].astype(BF16)], "ag_weights")
    w_full = _to_perm_cols(_from_block_major_cols(g_win))
    wua_full = _from_block_major_cols(g_wua)
    wub_full = _from_block_major_cols(g_wub)
    wout_full = g_wout.reshape(d, d)

    h, ht = _rmsnorm_fwd(xs, norm_g, "norm")
    d_in = w_full.shape[1]
    qkv = _inproj(h, w_full, 0, QKV_COLS, BF16, "inproj_qkv")
    rest = _inproj(h, w_full, QKV_COLS, d_in - QKV_COLS, F32, "inproj_rest")
    o, rs = _attn_fwd(qkv, "attn_fwd")

    lng = sgu_ln_g.reshape(1, D_SGU)
    lnb = sgu_ln_b.reshape(1, D_SGU)
    bfull = jnp.repeat(jnp.transpose(b_spatial[0]), GROUP_DIM, axis=1)
    grp = jnp.arange(D_SGU) // GROUP_DIM
    gavg = jnp.where(grp[:, None] == grp[None, :], 1.0 / GROUP_DIM, 0.0).astype(BF16)
    (loss_b, dx2, do, dproj, dwua, dwub, dwout, dfg8, dlng8, dlnb8, dwsp, dbfull) = _mid(
        o, rest, xs, tgt, wua_full, wub_full, wout_full, final_norm_g.reshape(1, d), lng, lnb, w_spatial[0], bfull,
        gavg, "mid")

    dproj = _attn_bwd(qkv, do, rs, dproj, "attn_bwd")
    dwin_perm = _dw_in(ht, dproj, "dwin")
    grad_x, dng8 = _dh_and_grad_x(dproj, w_full, xs, norm_g, dx2, "dh")

    def blocks42(a):
        return a.reshape((4, 2) + a.shape[1:])

    own = [blocks42(_block_major_cols(_from_perm_cols(dwin_perm))),
           blocks42(_block_major_cols(dwua.astype(BF16))),
           blocks42(_block_major_cols(dwub.astype(BF16))),
           blocks42(dwout.astype(BF16).reshape(N_DEV, d // N_DEV, d))]
    land1 = _push_sibling(own, "rs_sibling")
    cps = [_chip_partial_sum(a, l, core, "cpsum%d" % i) for i, (a, l) in enumerate(zip(own, land1))]
    land2 = _push_chips(cps, "rs_chips")

    dfg, dng, dlng, dlnb, db_tg = _small_reduce(dfg8, dng8, dlng8, dlnb8, dbfull, "small_reduce")
    db = jnp.transpose(db_tg[:, :N_GROUPS])

    def small_pack(ng, fg, lg, lb, bs, ws):
        flat = jnp.concatenate([ng.reshape(-1), fg.reshape(-1), lg.reshape(-1), lb.reshape(-1), bs.reshape(-1),
                                ws.reshape(-1)])
        return flat.reshape(-1, LANES)

    def small_unpack(a):
        flat = a.reshape(-1)
        sizes = [d, d, D_SGU, D_SGU, N_GROUPS * SGU_CHUNK, N_GROUPS * SGU_CHUNK * SGU_CHUNK]
        shapes = [norm_g.shape, final_norm_g.shape, sgu_ln_g.shape, sgu_ln_b.shape, b_spatial.shape, w_spatial.shape]
        out, off = [], 0
        for n, shp in zip(sizes, shapes):
            out.append(flat[off:off + n].reshape(shp))
            off += n
        return out

    (parts,) = _allgather([small_pack(dng, dfg, dlng, dlnb, db, dwsp)], "ag_small")
    sm = _adam_small(parts,
                     small_pack(norm_g, final_norm_g, sgu_ln_g, sgu_ln_b, b_spatial, w_spatial),
                     small_pack(m_norm_g, m_final_norm_g, m_sgu_ln_g, m_sgu_ln_b, m_b_spatial, m_w_spatial),
                     small_pack(v_norm_g, v_final_norm_g, v_sgu_ln_g, v_sgu_ln_b, v_b_spatial, v_w_spatial),
                     "adam_small")
    sm = [small_unpack(a) for a in sm]

    big = []
    for i, (w, m, v) in enumerate([(w_in, m_w_in, v_w_in), (w_up_a, m_w_up_a, v_w_up_a),
                                   (w_up_b, m_w_up_b, v_w_up_b), (w_out, m_w_out, v_w_out)]):
        res = _adam_shard(cps[i], land2[i], chip, w[0], m[0], v[0], "adam%d" % i)
        big.append([r[None] for r in res])

    loss = lax.psum(loss_b[0, 0], ("x", "y", "c"))

    def per_kind(kd):
        return [sm[kd][0], big[0][kd], sm[kd][2], sm[kd][3], sm[kd][5], sm[kd][4], big[1][kd], big[2][kd], big[3][kd],
                sm[kd][1]]

    return (loss, grad_x[None], *per_kind(0), *per_kind(1), *per_kind(2), *per_kind(3))
```

```python
import functools
import math

import jax
import jax.numpy as jnp
from jax import lax
from jax.experimental import pallas as pl
from jax.experimental.pallas import tpu as pltpu

F32 = jnp.float32
BF16 = jnp.bfloat16
MESH = pl.DeviceIdType.MESH

N_DEV = 8
N_HEADS = 8
HEAD_DIM = 64
D_SB = N_HEADS * HEAD_DIM
N_GROUPS = 8
GROUP_DIM = 64
D_SGU = N_GROUPS * GROUP_DIM
SGU_CHUNK = 128
CHUNK = 64
EPS = 1e-6
LANES = 128
N_PAIRS = N_HEADS // 2
QKV_COLS = 3 * D_SB
ATT_BLOCK = 256
CARRY_FLOOR = -110.0
R_UNREACHED = -1e30

ADAM_LR = 0.001
ADAM_B1 = 0.9
ADAM_B2 = 0.999
ADAM_EPS = 1e-08
ADAM_WD = 0.01
ADAM_STEP = 10

VMEM_LIMIT = 56 * 1024 * 1024


def _cparams(sem=None, vmem=VMEM_LIMIT):
    return pltpu.CompilerParams(dimension_semantics=sem, vmem_limit_bytes=vmem)


def _dot(a, b):
    return jnp.dot(a, b, preferred_element_type=F32)


def _dot_nt(a, b):
    return lax.dot_general(a, b, (((1,), (1,)), ((), ())), preferred_element_type=F32)


def _dot_tn(a, b):
    return lax.dot_general(a, b, (((0,), (0,)), ((), ())), preferred_element_type=F32)


def _split_hi_lo(a):
    hi = a.astype(BF16)
    lo = (a - hi.astype(F32)).astype(BF16)
    return hi, lo


def _dot_hilo(a, b_bf16):
    hi, lo = _split_hi_lo(a)
    return _dot(hi, b_bf16) + _dot(lo, b_bf16)


def _sigmoid(x):
    return 1.0 / (1.0 + jnp.exp(-x))


_GELU_C = math.sqrt(2.0 / math.pi)


def _gelu_and_grad(x):
    x2 = x * x
    inner = _GELU_C * (x + 0.044715 * (x2 * x))
    t = jnp.tanh(inner)
    cdf = 0.5 * (1.0 + t)
    g = x * cdf
    dg = cdf + x * (0.5 * (1.0 - t * t)) * (_GELU_C * (1.0 + 3.0 * 0.044715 * x2))
    return g, dg


def _coords():
    return lax.axis_index("x"), lax.axis_index("y"), lax.axis_index("c")


def _dev_index(px, py, pc):
    return 4 * px + 2 * py + pc


def _allgather(blocks, name):
    n = len(blocks)

    def body(*refs):
        ins, outs = refs[:n], refs[n:2 * n]
        send_sems, recv_sems, local_sems = refs[2 * n:]
        x, y, c = _coords()
        me, sibling = (x, y, c), (x, y, 1 - c)
        chips = [(1 - x, y), (x, 1 - y), (1 - x, 1 - y)]

        def copy(a, k, block, to, src=None):
            dst = outs[a].at[_dev_index(*block)]
            return pltpu.make_async_remote_copy(
                src_ref=dst if src is None else src, dst_ref=dst,
                send_sem=send_sems.at[a, k], recv_sem=recv_sems.at[a, k],
                device_id=to, device_id_type=MESH)

        mine = [pltpu.make_async_copy(ins[a], outs[a].at[_dev_index(*me)], local_sems.at[a]) for a in range(n)]
        for cp in mine:
            cp.start()
        first = []
        for a in range(n):
            first.append(copy(a, 0, me, sibling, src=ins[a]))
            first += [copy(a, 1 + j, me, (*chip, c), src=ins[a]) for j, chip in enumerate(chips)]
        for cp in first:
            cp.start()
        passed = []
        for j, chip in enumerate(chips):
            for a in range(n):
                copy(a, 1 + j, (*chip, c), me).wait_recv()
                fwd = copy(a, 4 + j, (*chip, c), sibling)
                fwd.start()
                passed.append(fwd)
        for a in range(n):
            copy(a, 0, sibling, me).wait_recv()
            for j, chip in enumerate(chips):
                copy(a, 4 + j, (*chip, 1 - c), me).wait_recv()
        for cp in first + passed:
            cp.wait_send()
        for cp in mine:
            cp.wait()

    any_spec = pl.BlockSpec(memory_space=pl.ANY)
    return pl.pallas_call(
        body, name=name,
        out_shape=[jax.ShapeDtypeStruct((N_DEV,) + b.shape, b.dtype) for b in blocks],
        in_specs=[any_spec] * n, out_specs=[any_spec] * n,
        scratch_shapes=[pltpu.SemaphoreType.DMA((n, 7)), pltpu.SemaphoreType.DMA((n, 7)),
                        pltpu.SemaphoreType.DMA((n,))],
    )(*blocks)


def _push_sibling(arrs, name):
    n = len(arrs)

    def body(*refs):
        ins, outs = refs[:n], refs[n:2 * n]
        send_sems, recv_sems = refs[2 * n:]
        x, y, c = _coords()
        sibling = (x, y, 1 - c)
        copies = []
        for a in range(n):
            for k in range(4):
                copies.append(pltpu.make_async_remote_copy(
                    src_ref=ins[a].at[k, 1 - c], dst_ref=outs[a].at[k],
                    send_sem=send_sems.at[a, k], recv_sem=recv_sems.at[a, k],
                    device_id=sibling, device_id_type=MESH))
        for cp in copies:
            cp.start()
        for cp in copies:
            cp.wait()

    any_spec = pl.BlockSpec(memory_space=pl.ANY)
    return pl.pallas_call(
        body, name=name,
        out_shape=[jax.ShapeDtypeStruct((4,) + a.shape[2:], a.dtype) for a in arrs],
        in_specs=[any_spec] * n, out_specs=[any_spec] * n,
        scratch_shapes=[pltpu.SemaphoreType.DMA((n, 4)), pltpu.SemaphoreType.DMA((n, 4))],
    )(*arrs)


def _push_chips(arrs, name):
    n = len(arrs)

    def body(*refs):
        ins, outs = refs[:n], refs[n:2 * n]
        send_sems, recv_sems = refs[2 * n:]
        x, y, c = _coords()
        chips = [(1 - x, y), (x, 1 - y), (1 - x, 1 - y)]
        copies = []
        for a in range(n):
            for r, (px, py) in enumerate(chips):
                copies.append(pltpu.make_async_remote_copy(
                    src_ref=ins[a].at[2 * px + py], dst_ref=outs[a].at[r],
                    send_sem=send_sems.at[a, r], recv_sem=recv_sems.at[a, r],
                    device_id=(px, py, c), device_id_type=MESH))
        for cp in copies:
            cp.start()
        for cp in copies:
            cp.wait()

    any_spec = pl.BlockSpec(memory_space=pl.ANY)
    return pl.pallas_call(
        body, name=name,
        out_shape=[jax.ShapeDtypeStruct((3,) + a.shape[1:], a.dtype) for a in arrs],
        in_specs=[any_spec] * n, out_specs=[any_spec] * n,
        scratch_shapes=[pltpu.SemaphoreType.DMA((n, 3)), pltpu.SemaphoreType.DMA((n, 3))],
    )(*arrs)


def _chip_partial_sum(own, land, core, name):
    _, _, rows, cols = own.shape
    tr = min(rows, 256)

    def body(core_ref, own_ref, land_ref, out_ref):
        del core_ref
        out_ref[...] = (own_ref[...].astype(F32) + land_ref[...].astype(F32)).astype(out_ref.dtype)

    return pl.pallas_call(
        body, name=name,
        out_shape=jax.ShapeDtypeStruct((4, rows, cols), own.dtype),
        grid_spec=pltpu.PrefetchScalarGridSpec(
            num_scalar_prefetch=1, grid=(4, rows // tr),
            in_specs=[pl.BlockSpec((None, None, tr, cols), lambda k, r, core: (k, core[0], r, 0)),
                      pl.BlockSpec((None, tr, cols), lambda k, r, core: (k, r, 0))],
            out_specs=pl.BlockSpec((None, tr, cols), lambda k, r, core: (k, r, 0))),
        compiler_params=_cparams(("parallel", "parallel")),
    )(core, own, land)


def _adamw_math(w, g, m, v):
    m = ADAM_B1 * m + (1.0 - ADAM_B1) * g
    v = ADAM_B2 * v + (1.0 - ADAM_B2) * (g * g)
    m_hat = m / (1.0 - ADAM_B1 ** ADAM_STEP)
    v_hat = v / (1.0 - ADAM_B2 ** ADAM_STEP)
    delta = -ADAM_LR * (m_hat / (jnp.sqrt(v_hat) + ADAM_EPS) + ADAM_WD * w)
    return delta, m, v


def _adam_shard(cp, land, chip, w, m, v, name):
    rows, cols = w.shape
    tr = min(rows, 256)

    def body(chip_ref, cp_ref, land_ref, w_ref, m_ref, v_ref, g_out, d_out, m_out, v_out):
        del chip_ref
        g = cp_ref[...].astype(F32)
        for r in range(3):
            g = g + land_ref[r].astype(F32)
        delta, m_new, v_new = _adamw_math(w_ref[...], g, m_ref[...], v_ref[...])
        g_out[...] = g
        d_out[...] = delta
        m_out[...] = m_new
        v_out[...] = v_new

    tile = pl.BlockSpec((tr, cols), lambda r, chip: (r, 0))
    out = jax.ShapeDtypeStruct((rows, cols), F32)
    return pl.pallas_call(
        body, name=name, out_shape=[out] * 4,
        grid_spec=pltpu.PrefetchScalarGridSpec(
            num_scalar_prefetch=1, grid=(rows // tr,),
            in_specs=[pl.BlockSpec((None, tr, cols), lambda r, chip: (chip[0], r, 0)),
                      pl.BlockSpec((3, tr, cols), lambda r, chip: (0, r, 0)),
                      tile, tile, tile],
            out_specs=[tile] * 4),
        compiler_params=_cparams(("parallel",)),
    )(chip, cp, land, w, m, v)


def _adam_small(parts, w, m, v, name):
    rows, cols = w.shape
    tr = rows // 4

    def body(p_ref, w_ref, m_ref, v_ref, g_out, d_out, m_out, v_out):
        g = p_ref[0]
        for d in range(1, N_DEV):
            g = g + p_ref[d]
        delta, m_new, v_new = _adamw_math(w_ref[...], g, m_ref[...], v_ref[...])
        g_out[...] = g
        d_out[...] = delta
        m_out[...] = m_new
        v_out[...] = v_new

    tile = pl.BlockSpec((tr, cols), lambda r: (r, 0))
    out = jax.ShapeDtypeStruct((rows, cols), F32)
    return pl.pallas_call(
        body, name=name, out_shape=[out] * 4, grid=(rows // tr,),
        in_specs=[pl.BlockSpec((N_DEV, tr, cols), lambda r: (0, r, 0)), tile, tile, tile],
        out_specs=[tile] * 4,
        compiler_params=_cparams(("parallel",)),
    )(parts, w, m, v)


def _rmsnorm_fwd(x, g, name):
    s, d = x.shape
    ts = 512

    def body(x_ref, g_ref, h_ref, ht_ref):
        xv = x_ref[...]
        rstd = lax.rsqrt(jnp.mean(xv * xv, axis=-1, keepdims=True) + EPS)
        h = xv * rstd * g_ref[...]
        h_ref[...] = h.astype(BF16)
        ht_ref[...] = h.T.astype(BF16)

    return pl.pallas_call(
        body, name=name,
        out_shape=[jax.ShapeDtypeStruct((s, d), BF16), jax.ShapeDtypeStruct((d, s), BF16)],
        grid=(s // ts,),
        in_specs=[pl.BlockSpec((ts, d), lambda i: (i, 0)), pl.BlockSpec((1, d), lambda i: (0, 0))],
        out_specs=[pl.BlockSpec((ts, d), lambda i: (i, 0)), pl.BlockSpec((d, ts), lambda i: (0, i))],
        compiler_params=_cparams(("parallel",)),
    )(x, g)


def _inproj(h, w, col0, ncols, out_dtype, name):
    s, d = h.shape
    tm, tn = min(s, 1024), 512
    j0 = col0 // tn

    def body(h_ref, w_ref, o_ref):
        o_ref[...] = _dot(h_ref[...], w_ref[...]).astype(o_ref.dtype)

    return pl.pallas_call(
        body, name=name, out_shape=jax.ShapeDtypeStruct((s, ncols), out_dtype),
        grid=(s // tm, ncols // tn),
        in_specs=[pl.BlockSpec((tm, d), lambda i, j: (i, 0)), pl.BlockSpec((d, tn), lambda i, j: (0, j + j0))],
        out_specs=pl.BlockSpec((tm, tn), lambda i, j: (i, j)),
        compiler_params=_cparams(("parallel", "arbitrary")),
    )(h, w)


def _dw_in(ht, dproj, name):
    d, s = ht.shape
    n = dproj.shape[1]
    tn, tk = 512, min(s, 1024)
    nk = s // tk

    def body(a_ref, b_ref, o_ref, acc_ref):
        k = pl.program_id(1)

        @pl.when(k == 0)
        def _():
            acc_ref[...] = jnp.zeros_like(acc_ref)

        acc_ref[...] += _dot(a_ref[...], b_ref[...])

        @pl.when(k == nk - 1)
        def _():
            o_ref[...] = acc_ref[...].astype(o_ref.dtype)

    return pl.pallas_call(
        body, name=name, out_shape=jax.ShapeDtypeStruct((d, n), BF16),
        grid=(n // tn, nk),
        in_specs=[pl.BlockSpec((d, tk), lambda j, k: (0, k)), pl.BlockSpec((tk, tn), lambda j, k: (k, j))],
        out_specs=pl.BlockSpec((d, tn), lambda j, k: (0, j)),
        scratch_shapes=[pltpu.VMEM((d, tn), F32)],
        compiler_params=_cparams(("parallel", "arbitrary")),
    )(ht, dproj)


def _dh_and_grad_x(dproj, w, x, g, dx2, name):
    s, n = dproj.shape
    d = w.shape[0]
    tm, tk = 512, 512
    nk = n // tk
    nm = s // tm

    def body(a_ref, w_ref, x_ref, g_ref, dx2_ref, gx_ref, dg_ref, acc_ref):
        i, k = pl.program_id(0), pl.program_id(1)

        @pl.when(k == 0)
        def _():
            acc_ref[...] = jnp.zeros_like(acc_ref)

        @pl.when((i == 0) & (k == 0))
        def _():
            dg_ref[...] = jnp.zeros_like(dg_ref)

        acc_ref[...] += _dot_nt(a_ref[...], w_ref[...])

        @pl.when(k == nk - 1)
        def _():
            dh = acc_ref[...]
            xv = x_ref[...]
            rstd = lax.rsqrt(jnp.mean(xv * xv, axis=-1, keepdims=True) + EPS)
            xhat = xv * rstd
            dg_ref[...] += jnp.sum((dh * xhat).reshape(tm // 8, 8, d), axis=0)
            dxh = dh * g_ref[...]
            gx_ref[...] = dx2_ref[...] + rstd * (dxh - xhat * jnp.mean(dxh * xhat, axis=-1, keepdims=True))

    return pl.pallas_call(
        body, name=name,
        out_shape=[jax.ShapeDtypeStruct((s, d), F32), jax.ShapeDtypeStruct((8, d), F32)],
        grid=(nm, nk),
        in_specs=[pl.BlockSpec((tm, tk), lambda i, k: (i, k)), pl.BlockSpec((d, tk), lambda i, k: (0, k)),
                  pl.BlockSpec((tm, d), lambda i, k: (i, 0)), pl.BlockSpec((1, d), lambda i, k: (0, 0)),
                  pl.BlockSpec((tm, d), lambda i, k: (i, 0))],
        out_specs=[pl.BlockSpec((tm, d), lambda i, k: (i, 0)), pl.BlockSpec((8, d), lambda i, k: (0, 0))],
        scratch_shapes=[pltpu.VMEM((tm, d), F32)],
        compiler_params=_cparams(("arbitrary", "arbitrary")),
    )(dproj, w, x, g, dx2)


def _log_sigmoids(z):
    l1p = jnp.log(1.0 + jnp.exp(-jnp.abs(z)))
    ls = jnp.minimum(z, 0.0) - l1p
    return ls, ls - z


def _strict_lower_ones(n):
    row = lax.broadcasted_iota(jnp.int32, (n, n), 0)
    col = lax.broadcasted_iota(jnp.int32, (n, n), 1)
    return row, col, (row > col).astype(BF16)


def _attn_fwd(qkv, name):
    s = qkv.shape[0]
    tb = ATT_BLOCK
    nq = s // tb

    def body(q_ref, k_ref, v_ref, o_ref, rs_ref, acc_ref, r_ref, rsv_ref):
        i = pl.program_id(1)
        lane = lax.broadcasted_iota(jnp.int32, (tb, LANES), 1)
        hmask = [lane < HEAD_DIM, lane >= HEAD_DIM]
        q2 = q_ref[...]
        qm = [jnp.where(m, q2, jnp.zeros((), BF16)) * jnp.asarray(HEAD_DIM ** -0.5, BF16) for m in hmask]
        row, col, tri = _strict_lower_ones(tb)
        acc_ref[...] = jnp.zeros_like(acc_ref)
        r_ref[...] = jnp.zeros_like(r_ref)
        rsv_ref[...] = jnp.full_like(rsv_ref, R_UNREACHED)

        def block(j, diag):
            rows = pl.ds(pl.multiple_of(j * tb, tb), tb)
            k2, v2 = k_ref[rows, :], v_ref[rows, :]
            for h in range(2):
                z = _dot_nt(qm[h], k2)
                ls, lk = _log_sigmoids(z)
                if diag:
                    lk = jnp.where(col < row, lk, 0.0)
                r = r_ref[h]
                logw = ls + _dot_hilo(lk, tri) + jnp.tile(r, (1, tb // LANES))
                w = jnp.exp(logw)
                if diag:
                    w = jnp.where(col < row, w, 0.0)
                acc_ref[h] += _dot(w.astype(BF16), v2)
                rsv_ref[h] = jnp.where(lane == j, r, rsv_ref[h])
                r_ref[h] = r + jnp.sum(lk, axis=1, keepdims=True)

        block(i, True)

        def live(c):
            jj, rmax = c
            return (jj < i) & (rmax >= CARRY_FLOOR)

        def step(c):
            jj, _ = c
            block(i - 1 - jj, False)
            return jj + 1, jnp.max(r_ref[...])

        lax.while_loop(live, step, (jnp.int32(0), jnp.max(r_ref[...])))
        o_ref[...] = jnp.where(hmask[0], acc_ref[0], acc_ref[1])
        rs_ref[...] = rsv_ref[...]

    return pl.pallas_call(
        body, name=name,
        out_shape=[jax.ShapeDtypeStruct((s, D_SB), F32), jax.ShapeDtypeStruct((N_HEADS, s, LANES), F32)],
        grid=(N_PAIRS, nq),
        in_specs=[pl.BlockSpec((tb, LANES), lambda p, i: (i, 3 * p)),
                  pl.BlockSpec((s, LANES), lambda p, i: (0, 3 * p + 1)),
                  pl.BlockSpec((s, LANES), lambda p, i: (0, 3 * p + 2))],
        out_specs=[pl.BlockSpec((tb, LANES), lambda p, i: (i, p)),
                   pl.BlockSpec((2, tb, LANES), lambda p, i: (p, i, 0))],
        scratch_shapes=[pltpu.VMEM((2, tb, LANES), F32), pltpu.VMEM((2, tb, LANES), F32),
                        pltpu.VMEM((2, tb, LANES), F32)],
        compiler_params=_cparams(("parallel", "arbitrary")),
    )(qkv, qkv, qkv)


def _attn_bwd(qkv, do, rs, dproj, name):
    s = qkv.shape[0]
    tb = ATT_BLOCK
    nq = s // tb
    scale = HEAD_DIM ** -0.5

    def body(qkv_ref, do_ref, rs_ref, dproj_hbm, out_ref, dq_acc, dk_acc, dv_acc, dqi_ref, pc_ref):
        del dproj_hbm
        lane = lax.broadcasted_iota(jnp.int32, (tb, LANES), 1)
        hmask = [lane < HEAD_DIM, lane >= HEAD_DIM]
        row, col, tri = _strict_lower_ones(tb)
        triu = (row < col).astype(BF16)
        zero = jnp.zeros((), BF16)
        dq_acc[...] = jnp.zeros_like(dq_acc)
        dk_acc[...] = jnp.zeros_like(dk_acc)
        dv_acc[...] = jnp.zeros_like(dv_acc)

        def qblock(i, carry):
            qrows = pl.ds(pl.multiple_of(i * tb, tb), tb)
            q2 = qkv_ref[qrows, 0:LANES]
            do2 = do_ref[qrows, :]
            qm = [jnp.where(m, q2, zero) * jnp.asarray(scale, BF16) for m in hmask]
            dom = [jnp.where(m, do2, zero) for m in hmask]
            rs_i = [rs_ref[h, qrows, :] for h in range(2)]
            dqi_ref[...] = jnp.zeros_like(dqi_ref)
            pc_ref[...] = jnp.zeros_like(pc_ref)
            reached = jnp.maximum(jnp.max(rs_i[0], axis=0, keepdims=True), jnp.max(rs_i[1], axis=0, keepdims=True))
            key_block = lax.broadcasted_iota(jnp.int32, (1, LANES), 1)
            n_reached = jnp.sum(((reached >= CARRY_FLOOR) & (key_block <= i)).astype(jnp.int32))

            def block(j, diag):
                krows = pl.ds(pl.multiple_of(j * tb, tb), tb)
                k2 = qkv_ref[krows, LANES:2 * LANES]
                v2 = qkv_ref[krows, 2 * LANES:3 * LANES]
                dq_sum = None
                dk_sum = None
                dv_sum = None
                for h in range(2):
                    z = _dot_nt(qm[h], k2)
                    ls, lk = _log_sigmoids(z)
                    if diag:
                        lk = jnp.where(col < row, lk, 0.0)
                    r = jnp.sum(jnp.where(lane == j, rs_i[h], 0.0), axis=1, keepdims=True)
                    w = jnp.exp(ls + _dot_hilo(lk, tri) + r)
                    if diag:
                        w = jnp.where(col < row, w, 0.0)
                    g = w * _dot_nt(dom[h], v2)
                    pc = pc_ref[h]
                    p = _dot(g.astype(BF16), triu) + jnp.tile(pc, (1, tb // LANES))
                    sig = jnp.exp(ls)
                    dz = g - sig * (g + p)
                    if diag:
                        dz = jnp.where(col < row, dz, 0.0)
                    dzb = dz.astype(BF16)
                    pc_ref[h] = pc + jnp.sum(g, axis=1, keepdims=True)
                    dq_h = _dot(dzb, jnp.where(hmask[h], k2, zero))
                    dk_h = _dot_tn(dzb, qm[h])
                    dv_h = _dot_tn(w.astype(BF16), dom[h])
                    dq_sum = dq_h if dq_sum is None else dq_sum + dq_h
                    dk_sum = dk_h if dk_sum is None else dk_sum + dk_h
                    dv_sum = dv_h if dv_sum is None else dv_sum + dv_h
                dqi_ref[...] += dq_sum
                dk_acc[krows, :] += dk_sum
                dv_acc[krows, :] += dv_sum

            def step(j, c):
                block(j, False)
                return c

            lax.fori_loop(i + 1 - n_reached, i, step, 0)
            block(i, True)
            dq_acc[qrows, :] += dqi_ref[...] * scale
            return carry

        lax.fori_loop(0, nq, qblock, 0)
        out_ref[:, 0:LANES] = dq_acc[...].astype(BF16)
        out_ref[:, LANES:2 * LANES] = dk_acc[...].astype(BF16)
        out_ref[:, 2 * LANES:3 * LANES] = dv_acc[...].astype(BF16)

    return pl.pallas_call(
        body, name=name,
        out_shape=jax.ShapeDtypeStruct(dproj.shape, BF16),
        grid=(N_PAIRS,),
        in_specs=[pl.BlockSpec((s, 3 * LANES), lambda p: (0, p)),
                  pl.BlockSpec((s, LANES), lambda p: (0, p)),
                  pl.BlockSpec((2, s, LANES), lambda p: (p, 0, 0)),
                  pl.BlockSpec(memory_space=pl.ANY)],
        out_specs=pl.BlockSpec((s, 3 * LANES), lambda p: (0, p)),
        scratch_shapes=[pltpu.VMEM((s, LANES), F32), pltpu.VMEM((s, LANES), F32), pltpu.VMEM((s, LANES), F32),
                        pltpu.VMEM((tb, LANES), F32), pltpu.VMEM((2, tb, LANES), F32)],
        input_output_aliases={3: 0},
        compiler_params=_cparams(("parallel",)),
    )(qkv, do, rs, dproj)


def _mid(o, rest, x, tgt, wua, wub, wout, fg, lng, lnb, wsp, bfull, gavg, name):
    s, d = x.shape
    ts = 256
    nt = s // ts
    nchunk = ts // SGU_CHUNK
    n_rest = rest.shape[1]

    def body(o_ref, rest_ref, x_ref, t_ref, wua_ref, wub_ref, wout_ref, fg_ref, lng_ref, lnb_ref, wsp_ref, bfull_ref,
             gavg_ref, loss_ref, dx2_ref, do_ref, dproj_ref, dwua_ref, dwub_ref, dwout_ref, dfg_ref, dlng_ref,
             dlnb_ref, dwsp_ref, dbfull_ref):
        step = pl.program_id(0)

        @pl.when(step == 0)
        def _():
            for ref in (loss_ref, dwua_ref, dwub_ref, dwout_ref, dfg_ref, dlng_ref, dlnb_ref, dwsp_ref, dbfull_ref):
                ref[...] = jnp.zeros_like(ref)

        gavg = gavg_ref[...]

        def gmean(a):
            return _dot_hilo(a, gavg)

        def colsum8(a):
            return jnp.sum(a.reshape(ts // 8, 8, a.shape[1]), axis=0)

        z_a = rest_ref[:, 0:512]
        u_b = rest_ref[:, 512:1024]
        v_b = rest_ref[:, 1024:1536]
        z_b = rest_ref[:, 1536:2048]
        g_a = rest_ref[:, 2048:2048 + d]
        g_b = rest_ref[:, 2048 + d:2048 + 2 * d]
        ov = o_ref[...]
        sa = _sigmoid(z_a)
        silu_a = z_a * sa
        y_a = ov * silu_a
        ug, dug_du = _gelu_and_grad(u_b)
        vg, dvg_dv = _gelu_and_grad(v_b)
        mu = gmean(vg)
        cen = vg - mu
        rstd_g = lax.rsqrt(gmean(cen * cen) + EPS)
        vhat = cen * rstd_g
        vn = vhat * lng_ref[...] + lnb_ref[...]
        vnb = vn.astype(BF16)

        t_idx = lax.broadcasted_iota(jnp.int32, (SGU_CHUNK, SGU_CHUNK), 0)
        s_idx = lax.broadcasted_iota(jnp.int32, (SGU_CHUNK, SGU_CHUNK), 1)
        causal = (s_idx // CHUNK) <= (t_idx // CHUNK)
        wm = [jnp.where(causal, wsp_ref[g], 0.0) for g in range(N_GROUPS)]
        wmb = [w.astype(BF16) for w in wm]
        wmtb = [w.T.astype(BF16) for w in wm]
        lane = lax.broadcasted_iota(jnp.int32, (SGU_CHUNK, LANES), 1)
        first = lane < GROUP_DIM
        bfull = bfull_ref[...]

        mixed_rows = []
        for n in range(nchunk):
            r0, r1 = n * SGU_CHUNK, (n + 1) * SGU_CHUNK
            pieces = []
            for p in range(N_GROUPS // 2):
                blk = vnb[r0:r1, p * LANES:(p + 1) * LANES]
                pieces.append(jnp.where(first, _dot(wmb[2 * p], blk), _dot(wmb[2 * p + 1], blk)))
            mixed_rows.append(jnp.concatenate(pieces, axis=1) + bfull)
        mixed = jnp.concatenate(mixed_rows, axis=0)
        sg = ug * mixed
        sb = _sigmoid(z_b)
        silu_b = z_b * sb
        y_b = sg * silu_b
        y_ab = y_a.astype(BF16)
        y_bb = y_b.astype(BF16)
        p_a = _dot(y_ab, wua_ref[...])
        p_b = _dot(y_bb, wub_ref[...])
        ga_s = _sigmoid(g_a)
        gb_s = _sigmoid(g_b)
        merged_b = (ga_s * p_a + gb_s * p_b).astype(BF16)
        x2 = x_ref[...] + _dot(merged_b, wout_ref[...])
        rstd = lax.rsqrt(jnp.mean(x2 * x2, axis=-1, keepdims=True) + EPS)
        xhat = x2 * rstd
        fg_v = fg_ref[...]
        diff = xhat * fg_v - t_ref[...]
        loss_ref[...] += 0.5 * jnp.sum(jnp.sum(diff * diff, axis=-1, keepdims=True) * (1.0 / d))

        dy = diff * (1.0 / d)
        dfg_ref[...] += colsum8(dy * xhat)
        dxh = dy * fg_v
        dx2 = rstd * (dxh - xhat * jnp.mean(dxh * xhat, axis=-1, keepdims=True))
        dx2_ref[...] = dx2
        dx2b = dx2.astype(BF16)
        dwout_ref[...] += _dot_tn(merged_b, dx2b)
        dmerged = _dot_nt(dx2b, wout_ref[...])
        dp_a = dmerged * ga_s
        dp_b = dmerged * gb_s
        dproj_ref[:, QKV_COLS + 2048:QKV_COLS + 2048 + d] = (dmerged * p_a * (ga_s * (1.0 - ga_s))).astype(BF16)
        dproj_ref[:, QKV_COLS + 2048 + d:QKV_COLS + 2048 + 2 * d] = (dmerged * p_b * (gb_s * (1.0 - gb_s))).astype(BF16)
        dp_ab = dp_a.astype(BF16)
        dp_bb = dp_b.astype(BF16)
        dwua_ref[...] += _dot_tn(y_ab, dp_ab)
        dwub_ref[...] += _dot_tn(y_bb, dp_bb)
        dy_a = _dot_nt(dp_ab, wua_ref[...])
        dy_b = _dot_nt(dp_bb, wub_ref[...])
        do_ref[...] = (dy_a * silu_a).astype(BF16)
        dproj_ref[:, QKV_COLS:QKV_COLS + 512] = (dy_a * ov * (sa * (1.0 + z_a * (1.0 - sa)))).astype(BF16)
        dsg = dy_b * silu_b
        dproj_ref[:, QKV_COLS + 1536:QKV_COLS + 2048] = (dy_b * sg * (sb * (1.0 + z_b * (1.0 - sb)))).astype(BF16)
        dproj_ref[:, QKV_COLS + 512:QKV_COLS + 1024] = (dsg * mixed * dug_du).astype(BF16)
        dmixed = dsg * ug
        dmb = dmixed.astype(BF16)
        zero = jnp.zeros((), BF16)
        dvn_rows = []
        db = jnp.zeros((SGU_CHUNK, D_SGU), F32)
        for n in range(nchunk):
            r0, r1 = n * SGU_CHUNK, (n + 1) * SGU_CHUNK
            db = db + dmixed[r0:r1, :]
            pieces = []
            for p in range(N_GROUPS // 2):
                cols = slice(p * LANES, (p + 1) * LANES)
                dm_blk = dmb[r0:r1, cols]
                vn_blk = vnb[r0:r1, cols]
                dwsp_ref[2 * p] += _dot_nt(jnp.where(first, dm_blk, zero), vn_blk)
                dwsp_ref[2 * p + 1] += _dot_nt(jnp.where(first, zero, dm_blk), vn_blk)
                pieces.append(jnp.where(first, _dot(wmtb[2 * p], dm_blk), _dot(wmtb[2 * p + 1], dm_blk)))
            dvn_rows.append(jnp.concatenate(pieces, axis=1))
        dbfull_ref[...] += db
        dvn = jnp.concatenate(dvn_rows, axis=0)
        dlng_ref[...] += colsum8(dvn * vhat)
        dlnb_ref[...] += colsum8(dvn)
        dvhat = dvn * lng_ref[...]
        dcen = rstd_g * (dvhat - gmean(dvhat) - vhat * gmean(dvhat * vhat))
        dproj_ref[:, QKV_COLS + 1024:QKV_COLS + 1536] = (dcen * dvg_dv).astype(BF16)

        @pl.when(step == nt - 1)
        def _():
            for g in range(N_GROUPS):
                dwsp_ref[g] = jnp.where(causal, dwsp_ref[g], 0.0)

    def tile(cols):
        return pl.BlockSpec((ts, cols), lambda i: (i, 0))

    def whole(shape):
        return pl.BlockSpec(shape, lambda i: (0,) * len(shape))

    out_shapes = [
        jax.ShapeDtypeStruct((8, LANES), F32),
        jax.ShapeDtypeStruct((s, d), F32),
        jax.ShapeDtypeStruct((s, D_SB), BF16),
        jax.ShapeDtypeStruct((s, QKV_COLS + n_rest), BF16),
        jax.ShapeDtypeStruct((D_SB, d), F32),
        jax.ShapeDtypeStruct((D_SGU, d), F32),
        jax.ShapeDtypeStruct((d, d), F32),
        jax.ShapeDtypeStruct((8, d), F32),
        jax.ShapeDtypeStruct((8, D_SGU), F32),
        jax.ShapeDtypeStruct((8, D_SGU), F32),
        jax.ShapeDtypeStruct((N_GROUPS, SGU_CHUNK, SGU_CHUNK), F32),
        jax.ShapeDtypeStruct((SGU_CHUNK, D_SGU), F32),
    ]
    out_specs = [whole((8, LANES)), tile(d), tile(D_SB), tile(QKV_COLS + n_rest), whole((D_SB, d)), whole((D_SGU, d)),
                 whole((d, d)), whole((8, d)), whole((8, D_SGU)), whole((8, D_SGU)),
                 whole((N_GROUPS, SGU_CHUNK, SGU_CHUNK)), whole((SGU_CHUNK, D_SGU))]
    in_specs = [tile(D_SB), tile(n_rest), tile(d), tile(d), whole((D_SB, d)), whole((D_SGU, d)), whole((d, d)),
                whole((1, d)), whole((1, D_SGU)), whole((1, D_SGU)), whole((N_GROUPS, SGU_CHUNK, SGU_CHUNK)),
                whole((SGU_CHUNK, D_SGU)), whole((D_SGU, D_SGU))]
    return pl.pallas_call(
        body, name=name, out_shape=out_shapes, grid=(nt,), in_specs=in_specs, out_specs=out_specs,
        compiler_params=_cparams(("arbitrary",)),
    )(o, rest, x, tgt, wua, wub, wout, fg, lng, lnb, wsp, bfull, gavg)


def _small_reduce(dfg8, dng8, dlng8, dlnb8, dbfull, name):
    d = dfg8.shape[1]

    def body(dfg_ref, dng_ref, dlng_ref, dlnb_ref, dbfull_ref, fg_out, ng_out, lng_out, lnb_out, b_out):
        fg_out[...] = jnp.sum(dfg_ref[...], axis=0, keepdims=True)
        ng_out[...] = jnp.sum(dng_ref[...], axis=0, keepdims=True)
        lng_out[...] = jnp.sum(dlng_ref[...], axis=0, keepdims=True)
        lnb_out[...] = jnp.sum(dlnb_ref[...], axis=0, keepdims=True)
        grp = lax.broadcasted_iota(jnp.int32, (D_SGU, LANES), 0) // GROUP_DIM
        col = lax.broadcasted_iota(jnp.int32, (D_SGU, LANES), 1)
        sel = (grp == col).astype(BF16)
        a = dbfull_ref[...]
        hi, lo = _split_hi_lo(a)
        lo2 = (a - hi.astype(F32) - lo.astype(F32)).astype(BF16)
        b_out[...] = _dot(hi, sel) + _dot(lo, sel) + _dot(lo2, sel)

    return pl.pallas_call(
        body, name=name,
        out_shape=[jax.ShapeDtypeStruct((1, d), F32), jax.ShapeDtypeStruct((1, d), F32),
                   jax.ShapeDtypeStruct((1, D_SGU), F32), jax.ShapeDtypeStruct((1, D_SGU), F32),
                   jax.ShapeDtypeStruct((SGU_CHUNK, LANES), F32)],
        compiler_params=_cparams(),
    )(dfg8, dng8, dlng8, dlnb8, dbfull)


def _to_perm_cols(w):
    parts = []
    for p in range(N_PAIRS):
        for t in range(3):
            c0 = t * D_SB + p * LANES
            parts.append(w[..., c0:c0 + LANES])
    parts.append(w[..., QKV_COLS:])
    return jnp.concatenate(parts, axis=-1)


def _from_perm_cols(w):
    qkv = [[None] * N_PAIRS for _ in range(3)]
    for p in range(N_PAIRS):
        for t in range(3):
            c0 = (3 * p + t) * LANES
            qkv[t][p] = w[..., c0:c0 + LANES]
    parts = [qkv[t][p] for t in range(3) for p in range(N_PAIRS)]
    parts.append(w[..., QKV_COLS:])
    return jnp.concatenate(parts, axis=-1)


def _block_major_cols(w):
    r, n = w.shape
    return jnp.transpose(w.reshape(r, N_DEV, n // N_DEV), (1, 0, 2))


def _from_block_major_cols(w):
    nb, r, c = w.shape
    return jnp.transpose(w, (1, 0, 2)).reshape(r, nb * c)


def kernel(x, norm_g, w_in, sgu_ln_g, sgu_ln_b, w_spatial, b_spatial, w_up_a, w_up_b, w_out, final_norm_g, loss_target, m_norm_g, m_w_in, m_sgu_ln_g, m_sgu_ln_b, m_w_spatial, m_b_spatial, m_w_up_a, m_w_up_b, m_w_out, m_final_norm_g, v_norm_g, v_w_in, v_sgu_ln_g, v_sgu_ln_b, v_w_spatial, v_b_spatial, v_w_up_a, v_w_up_b, v_w_out, v_final_norm_g):
    s, d = x.shape[1], x.shape[2]
    xs = x[0]
    tgt = loss_target[0]
    cx, cy, cc = _coords()
    core = jnp.reshape(cc, (1,)).astype(jnp.int32)
    chip = jnp.reshape(2 * cx + cy, (1,)).astype(jnp.int32)

    g_win, g_wua, g_wub, g_wout = _allgather(
        [w_in[0].astype(BF16), w_up_a[0].astype(BF16), w_up_b[0].astype(BF16), w_out[0].astype(BF16)], "ag_weights")
    w_full = _to_perm_cols(_from_block_major_cols(g_win))
    wua_full = _from_block_major_cols(g_wua)
    wub_full = _from_block_major_cols(g_wub)
    wout_full = g_wout.reshape(d, d)

    h, ht = _rmsnorm_fwd(xs, norm_g, "norm")
    d_in = w_full.shape[1]
    qkv = _inproj(h, w_full, 0, QKV_COLS, BF16, "inproj_qkv")
    rest = _inproj(h, w_full, QKV_COLS, d_in - QKV_COLS, F32, "inproj_rest")
    o, rs = _attn_fwd(qkv, "attn_fwd")

    lng = sgu_ln_g.reshape(1, D_SGU)
    lnb = sgu_ln_b.reshape(1, D_SGU)
    bfull = jnp.repeat(jnp.transpose(b_spatial[0]), GROUP_DIM, axis=1)
    grp = jnp.arange(D_SGU) // GROUP_DIM
    gavg = jnp.where(grp[:, None] == grp[None, :], 1.0 / GROUP_DIM, 0.0).astype(BF16)
    (loss_b, dx2, do, dproj, dwua, dwub, dwout, dfg8, dlng8, dlnb8, dwsp, dbfull) = _mid(
        o, rest, xs, tgt, wua_full, wub_full, wout_full, final_norm_g.reshape(1, d), lng, lnb, w_spatial[0], bfull,
        gavg, "mid")

    dproj = _attn_bwd(qkv, do, rs, dproj, "attn_bwd")
    dwin_perm = _dw_in(ht, dproj, "dwin")
    grad_x, dng8 = _dh_and_grad_x(dproj, w_full, xs, norm_g, dx2, "dh")

    def blocks42(a):
        return a.reshape((4, 2) + a.shape[1:])

    own = [blocks42(_block_major_cols(_from_perm_cols(dwin_perm))),
           blocks42(_block_major_cols(dwua.astype(BF16))),
           blocks42(_block_major_cols(dwub.astype(BF16))),
           blocks42(dwout.astype(BF16).reshape(N_DEV, d // N_DEV, d))]
    land1 = _push_sibling(own, "rs_sibling")
    cps = [_chip_partial_sum(a, l, core, "cpsum%d" % i) for i, (a, l) in enumerate(zip(own, land1))]
    land2 = _push_chips(cps, "rs_chips")

    dfg, dng, dlng, dlnb, db_tg = _small_reduce(dfg8, dng8, dlng8, dlnb8, dbfull, "small_reduce")
    db = jnp.transpose(db_tg[:, :N_GROUPS])

    def small_pack(ng, fg, lg, lb, bs, ws):
        flat = jnp.concatenate([ng.reshape(-1), fg.reshape(-1), lg.reshape(-1), lb.reshape(-1), bs.reshape(-1),
                                ws.reshape(-1)])
        return flat.reshape(-1, LANES)

    def small_unpack(a):
        flat = a.reshape(-1)
        sizes = [d, d, D_SGU, D_SGU, N_GROUPS * SGU_CHUNK, N_GROUPS * SGU_CHUNK * SGU_CHUNK]
        shapes = [norm_g.shape, final_norm_g.shape, sgu_ln_g.shape, sgu_ln_b.shape, b_spatial.shape, w_spatial.shape]
        out, off = [], 0
        for n, shp in zip(sizes, shapes):
            out.append(flat[off:off + n].reshape(shp))
            off += n
        return out

    (parts,) = _allgather([small_pack(dng, dfg, dlng, dlnb, db, dwsp)], "ag_small")
    sm = _adam_small(parts,
                     small_pack(norm_g, final_norm_g, sgu_ln_g, sgu_ln_b, b_spatial, w_spatial),
                     small_pack(m_norm_g, m_final_norm_g, m_sgu_ln_g, m_sgu_ln_b, m_b_spatial, m_w_spatial),
                     small_pack(v_norm_g, v_final_norm_g, v_sgu_ln_g, v_sgu_ln_b, v_b_spatial, v_w_spatial),
                     "adam_small")
    sm = [small_unpack(a) for a in sm]

    big = []
    for i, (w, m, v) in enumerate([(w_in, m_w_in, v_w_in), (w_up_a, m_w_up_a, v_w_up_a),
                                   (w_up_b, m_w_up_b, v_w_up_b), (w_out, m_w_out, v_w_out)]):
        res = _adam_shard(cps[i], land2[i], chip, w[0], m[0], v[0], "adam%d" % i)
        big.append([r[None] for r in res])

    loss = lax.psum(loss_b[0, 0], ("x", "y", "c"))

    def per_kind(kd):
        return [sm[kd][0], big[0][kd], sm[kd][2], sm[kd][3], sm[kd][5], sm[kd][4], big[1][kd], big[2][kd], big[3][kd],
                sm[kd][1]]

    return (loss, grad_x[None], *per_kind(0), *per_kind(1), *per_kind(2), *per_kind(3))
```

```python
import math

import jax
import jax.numpy as jnp
from jax import lax
from jax.experimental import pallas as pl
from jax.experimental.pallas import tpu as pltpu

F32 = jnp.float32
BF16 = jnp.bfloat16
MESH = pl.DeviceIdType.MESH

N_DEV = 8
N_HEADS = 8
HEAD_DIM = 64
D_SB = N_HEADS * HEAD_DIM
N_GROUPS = 8
GROUP_DIM = 64
D_SGU = N_GROUPS * GROUP_DIM
SGU_CHUNK = 128
CHUNK = 64
EPS = 1e-6
LANES = 128
N_PAIRS = N_HEADS // 2
QKV_COLS = 3 * D_SB
ATT_BLOCK = 256
CARRY_FLOOR = -110.0
R_UNREACHED = -1e30

ADAM_LR = 0.001
ADAM_B1 = 0.9
ADAM_B2 = 0.999
ADAM_EPS = 1e-08
ADAM_WD = 0.01
ADAM_STEP = 10

VMEM_LIMIT = 56 * 1024 * 1024


def _cparams(sem=None, vmem=VMEM_LIMIT):
    return pltpu.CompilerParams(dimension_semantics=sem, vmem_limit_bytes=vmem)


def _dot(a, b):
    return jnp.dot(a, b, preferred_element_type=F32)


def _dot_nt(a, b):
    return lax.dot_general(a, b, (((1,), (1,)), ((), ())), preferred_element_type=F32)


def _dot_tn(a, b):
    return lax.dot_general(a, b, (((0,), (0,)), ((), ())), preferred_element_type=F32)


def _split_hi_lo(a):
    hi = a.astype(BF16)
    lo = (a - hi.astype(F32)).astype(BF16)
    return hi, lo


def _dot_hilo(a, b_bf16):
    hi, lo = _split_hi_lo(a)
    return _dot(hi, b_bf16) + _dot(lo, b_bf16)


def _sigmoid(x):
    return 1.0 / (1.0 + jnp.exp(-x))


_GELU_C = math.sqrt(2.0 / math.pi)


def _gelu_and_grad(x):
    x2 = x * x
    inner = _GELU_C * (x + 0.044715 * (x2 * x))
    t = jnp.tanh(inner)
    cdf = 0.5 * (1.0 + t)
    g = x * cdf
    dg = cdf + x * (0.5 * (1.0 - t * t)) * (_GELU_C * (1.0 + 3.0 * 0.044715 * x2))
    return g, dg


def _coords():
    return lax.axis_index("x"), lax.axis_index("y"), lax.axis_index("c")


def _dev_index(px, py, pc):
    return 4 * px + 2 * py + pc


def _allgather(blocks, name):
    n = len(blocks)

    def body(*refs):
        gather = _Gather(refs[:n], refs[n:2 * n], *refs[2 * n:])
        gather.issue()
        gather.finish()

    any_spec = pl.BlockSpec(memory_space=pl.ANY)
    return pl.pallas_call(
        body, name=name,
        out_shape=_gather_out_shapes(blocks),
        in_specs=[any_spec] * n, out_specs=[any_spec] * n,
        scratch_shapes=_gather_semaphores(n),
    )(*blocks)


def _gather_out_shapes(blocks):
    return [jax.ShapeDtypeStruct((N_DEV,) + b.shape, b.dtype) for b in blocks]


def _gather_semaphores(n):
    return [pltpu.SemaphoreType.DMA((n, 7)), pltpu.SemaphoreType.DMA((n, 7)), pltpu.SemaphoreType.DMA((n,))]


class _Gather:
    def __init__(self, ins, outs, send_sems, recv_sems, local_sems):
        self.ins, self.outs = ins, outs
        self.send_sems, self.recv_sems, self.local_sems = send_sems, recv_sems, local_sems
        self.n = len(ins)
        x, y, c = _coords()
        self.c = c
        self.me, self.sibling = (x, y, c), (x, y, 1 - c)
        self.chips = [(1 - x, y), (x, 1 - y), (1 - x, 1 - y)]

    def _copy(self, a, k, block, to, src=None):
        dst = self.outs[a].at[_dev_index(*block)]
        return pltpu.make_async_remote_copy(
            src_ref=dst if src is None else src, dst_ref=dst,
            send_sem=self.send_sems.at[a, k], recv_sem=self.recv_sems.at[a, k],
            device_id=to, device_id_type=MESH)

    def _mine(self):
        return [pltpu.make_async_copy(self.ins[a], self.outs[a].at[_dev_index(*self.me)], self.local_sems.at[a])
                for a in range(self.n)]

    def _first(self):
        first = []
        for a in range(self.n):
            first.append(self._copy(a, 0, self.me, self.sibling, src=self.ins[a]))
            first += [self._copy(a, 1 + j, self.me, (*chip, self.c), src=self.ins[a])
                      for j, chip in enumerate(self.chips)]
        return first

    def issue(self):
        for cp in self._mine() + self._first():
            cp.start()

    def finish(self):
        c = self.c
        passed = []
        for j, chip in enumerate(self.chips):
            for a in range(self.n):
                self._copy(a, 1 + j, (*chip, c), self.me).wait_recv()
                fwd = self._copy(a, 4 + j, (*chip, c), self.sibling)
                fwd.start()
                passed.append(fwd)
        for a in range(self.n):
            self._copy(a, 0, self.sibling, self.me).wait_recv()
            for j, chip in enumerate(self.chips):
                self._copy(a, 4 + j, (*chip, 1 - c), self.me).wait_recv()
        for cp in self._first() + passed:
            cp.wait_send()
        for cp in self._mine():
            cp.wait()


def _push_sibling(arrs, name):
    n = len(arrs)

    def body(*refs):
        ins, outs = refs[:n], refs[n:2 * n]
        send_sems, recv_sems = refs[2 * n:]
        x, y, c = _coords()
        sibling = (x, y, 1 - c)
        copies = []
        for a in range(n):
            for k in range(4):
                copies.append(pltpu.make_async_remote_copy(
                    src_ref=ins[a].at[k, 1 - c], dst_ref=outs[a].at[k],
                    send_sem=send_sems.at[a, k], recv_sem=recv_sems.at[a, k],
                    device_id=sibling, device_id_type=MESH))
        for cp in copies:
            cp.start()
        for cp in copies:
            cp.wait()

    any_spec = pl.BlockSpec(memory_space=pl.ANY)
    return pl.pallas_call(
        body, name=name,
        out_shape=[jax.ShapeDtypeStruct((4,) + a.shape[2:], a.dtype) for a in arrs],
        in_specs=[any_spec] * n, out_specs=[any_spec] * n,
        scratch_shapes=[pltpu.SemaphoreType.DMA((n, 4)), pltpu.SemaphoreType.DMA((n, 4))],
    )(*arrs)


def _chip_push_copies(ins, outs, send_sems, recv_sems):
    x, y, c = _coords()
    chips = [(1 - x, y), (x, 1 - y), (1 - x, 1 - y)]
    return [pltpu.make_async_remote_copy(
        src_ref=ins[a].at[2 * px + py], dst_ref=outs[a].at[r],
        send_sem=send_sems.at[a, r], recv_sem=recv_sems.at[a, r],
        device_id=(px, py, c), device_id_type=MESH)
        for a in range(len(ins)) for r, (px, py) in enumerate(chips)]


def _chip_partial_sum(own, land, core, name):
    _, _, rows, cols = own.shape
    tr = min(rows, 256)

    def body(core_ref, own_ref, land_ref, out_ref):
        del core_ref
        out_ref[...] = (own_ref[...].astype(F32) + land_ref[...].astype(F32)).astype(out_ref.dtype)

    return pl.pallas_call(
        body, name=name,
        out_shape=jax.ShapeDtypeStruct((4, rows, cols), own.dtype),
        grid_spec=pltpu.PrefetchScalarGridSpec(
            num_scalar_prefetch=1, grid=(4, rows // tr),
            in_specs=[pl.BlockSpec((None, None, tr, cols), lambda k, r, core: (k, core[0], r, 0)),
                      pl.BlockSpec((None, tr, cols), lambda k, r, core: (k, r, 0))],
            out_specs=pl.BlockSpec((None, tr, cols), lambda k, r, core: (k, r, 0))),
        compiler_params=_cparams(("parallel", "parallel")),
    )(core, own, land)


def _adamw_math(w, g, m, v):
    m = ADAM_B1 * m + (1.0 - ADAM_B1) * g
    v = ADAM_B2 * v + (1.0 - ADAM_B2) * (g * g)
    m_hat = m / (1.0 - ADAM_B1 ** ADAM_STEP)
    v_hat = v / (1.0 - ADAM_B2 ** ADAM_STEP)
    delta = -ADAM_LR * (m_hat / (jnp.sqrt(v_hat) + ADAM_EPS) + ADAM_WD * w)
    return delta, m, v


def _adam_shard(cp, land, chip, w, m, v, name):
    rows, cols = w.shape
    tr = min(rows, 256)

    def body(chip_ref, cp_ref, land_ref, w_ref, m_ref, v_ref, g_out, d_out, m_out, v_out):
        del chip_ref
        g = cp_ref[...].astype(F32)
        for r in range(3):
            g = g + land_ref[r].astype(F32)
        delta, m_new, v_new = _adamw_math(w_ref[...], g, m_ref[...], v_ref[...])
        g_out[...] = g
        d_out[...] = delta
        m_out[...] = m_new
        v_out[...] = v_new

    tile = pl.BlockSpec((tr, cols), lambda r, chip: (r, 0))
    out = jax.ShapeDtypeStruct((rows, cols), F32)
    return pl.pallas_call(
        body, name=name, out_shape=[out] * 4,
        grid_spec=pltpu.PrefetchScalarGridSpec(
            num_scalar_prefetch=1, grid=(rows // tr,),
            in_specs=[pl.BlockSpec((None, tr, cols), lambda r, chip: (chip[0], r, 0)),
                      pl.BlockSpec((3, tr, cols), lambda r, chip: (0, r, 0)),
                      tile, tile, tile],
            out_specs=[tile] * 4),
        compiler_params=_cparams(("parallel",)),
    )(chip, cp, land, w, m, v)


def _adam_small(parts, ws, ms, vs, name):
    n = len(ws)

    def body(*refs):
        p_refs, w_refs, m_refs, v_refs = refs[:n], refs[n:2 * n], refs[2 * n:3 * n], refs[3 * n:4 * n]
        outs = refs[4 * n:]
        for i in range(n):
            g = p_refs[i][0]
            for dev in range(1, N_DEV):
                g = g + p_refs[i][dev]
            if g.shape[0] != w_refs[i].shape[0]:
                g = jnp.sum(g, axis=0, keepdims=True)
            delta, m_new, v_new = _adamw_math(w_refs[i][...], g, m_refs[i][...], v_refs[i][...])
            outs[i][...] = g
            outs[n + i][...] = delta
            outs[2 * n + i][...] = m_new
            outs[3 * n + i][...] = v_new

    out_shapes = [jax.ShapeDtypeStruct(w.shape, F32) for w in ws] * 4
    res = pl.pallas_call(body, name=name, out_shape=out_shapes, compiler_params=_cparams())(*parts, *ws, *ms, *vs)
    return [res[k * n:(k + 1) * n] for k in range(4)]


def _rmsnorm_fwd(x, g, name):
    s, d = x.shape
    ts = 512

    def body(x_ref, g_ref, h_ref, ht_ref):
        xv = x_ref[...]
        rstd = lax.rsqrt(jnp.mean(xv * xv, axis=-1, keepdims=True) + EPS)
        h = xv * rstd * g_ref[...]
        h_ref[...] = h.astype(BF16)
        ht_ref[...] = h.T.astype(BF16)

    return pl.pallas_call(
        body, name=name,
        out_shape=[jax.ShapeDtypeStruct((s, d), BF16), jax.ShapeDtypeStruct((d, s), BF16)],
        grid=(s // ts,),
        in_specs=[pl.BlockSpec((ts, d), lambda i: (i, 0)), pl.BlockSpec((1, d), lambda i: (0, 0))],
        out_specs=[pl.BlockSpec((ts, d), lambda i: (i, 0)), pl.BlockSpec((d, ts), lambda i: (0, i))],
        compiler_params=_cparams(("parallel",)),
    )(x, g)


def _inproj(h, w, col0, ncols, out_dtype, name):
    s, d = h.shape
    tn = 256
    j0 = col0 // tn

    def body(h_ref, w_ref, o_ref):
        o_ref[...] = _dot(h_ref[...], w_ref[...]).astype(o_ref.dtype)

    return pl.pallas_call(
        body, name=name, out_shape=jax.ShapeDtypeStruct((s, ncols), out_dtype),
        grid=(ncols // tn,),
        in_specs=[pl.BlockSpec((s, d), lambda j: (0, 0)), pl.BlockSpec((d, tn), lambda j: (0, j + j0))],
        out_specs=pl.BlockSpec((s, tn), lambda j: (0, j)),
        compiler_params=_cparams(("parallel",)),
    )(h, w)


def _dw_in(ht, dproj, smalls, name):
    d, s = ht.shape
    n = dproj.shape[1]
    tn = 512
    nj = n // tn
    ns = len(smalls)

    def body(a_ref, b_ref, *rest):
        small_in, o_ref, small_out, sems = rest[:ns], rest[ns], rest[ns + 1:2 * ns + 1], rest[2 * ns + 1:]
        j = pl.program_id(0)
        gather = _Gather(small_in, small_out, *sems)

        @pl.when(j == 0)
        def _():
            gather.issue()

        o_ref[...] = _dot(a_ref[...], b_ref[...]).astype(o_ref.dtype)

        @pl.when(j == nj - 1)
        def _():
            gather.finish()

    any_spec = pl.BlockSpec(memory_space=pl.ANY)
    res = pl.pallas_call(
        body, name=name,
        out_shape=[jax.ShapeDtypeStruct((d, n), BF16)] + _gather_out_shapes(smalls),
        grid=(nj,),
        in_specs=[pl.BlockSpec((d, s), lambda j: (0, 0)), pl.BlockSpec((s, tn), lambda j: (0, j))] + [any_spec] * ns,
        out_specs=[pl.BlockSpec((d, tn), lambda j: (0, j))] + [any_spec] * ns,
        scratch_shapes=_gather_semaphores(ns),
        compiler_params=_cparams(("arbitrary",)),
    )(ht, dproj, *smalls)
    return res[0], res[1:]


def _dh_and_grad_x(dproj, w, x, g, dx2, cps, name):
    s, n = dproj.shape
    d = w.shape[0]
    tm, tk = min(s, 1024), 512
    nk = n // tk
    nm = s // tm
    nc = len(cps)

    def body(a_ref, w_ref, x_ref, g_ref, dx2_ref, *rest):
        cp_refs, gx_ref, dg_ref = rest[:nc], rest[nc], rest[nc + 1]
        land_refs = rest[nc + 2:2 * nc + 2]
        acc_ref, send_sems, recv_sems = rest[2 * nc + 2:]
        i, k = pl.program_id(0), pl.program_id(1)

        @pl.when((i == 0) & (k == 0))
        def _():
            for cp in _chip_push_copies(cp_refs, land_refs, send_sems, recv_sems):
                cp.start()
            dg_ref[...] = jnp.zeros_like(dg_ref)

        @pl.when(k == 0)
        def _():
            acc_ref[...] = jnp.zeros_like(acc_ref)

        acc_ref[...] += _dot_nt(a_ref[...], w_ref[...])

        @pl.when(k == nk - 1)
        def _():
            dh = acc_ref[...]
            xv = x_ref[...]
            rstd = lax.rsqrt(jnp.mean(xv * xv, axis=-1, keepdims=True) + EPS)
            xhat = xv * rstd
            dg_ref[...] += jnp.sum((dh * xhat).reshape(tm // 8, 8, d), axis=0)
            dxh = dh * g_ref[...]
            gx_ref[...] = dx2_ref[...] + rstd * (dxh - xhat * jnp.mean(dxh * xhat, axis=-1, keepdims=True))

        @pl.when((i == nm - 1) & (k == nk - 1))
        def _():
            for cp in _chip_push_copies(cp_refs, land_refs, send_sems, recv_sems):
                cp.wait()

    any_spec = pl.BlockSpec(memory_space=pl.ANY)
    res = pl.pallas_call(
        body, name=name,
        out_shape=[jax.ShapeDtypeStruct((s, d), F32), jax.ShapeDtypeStruct((8, d), F32)]
        + [jax.ShapeDtypeStruct((3,) + a.shape[1:], a.dtype) for a in cps],
        grid=(nm, nk),
        in_specs=[pl.BlockSpec((tm, tk), lambda i, k: (i, k)), pl.BlockSpec((d, tk), lambda i, k: (0, k)),
                  pl.BlockSpec((tm, d), lambda i, k: (i, 0)), pl.BlockSpec((1, d), lambda i, k: (0, 0)),
                  pl.BlockSpec((tm, d), lambda i, k: (i, 0))] + [any_spec] * nc,
        out_specs=[pl.BlockSpec((tm, d), lambda i, k: (i, 0)), pl.BlockSpec((8, d), lambda i, k: (0, 0))]
        + [any_spec] * nc,
        scratch_shapes=[pltpu.VMEM((tm, d), F32), pltpu.SemaphoreType.DMA((nc, 3)), pltpu.SemaphoreType.DMA((nc, 3))],
        compiler_params=_cparams(("arbitrary", "arbitrary")),
    )(dproj, w, x, g, dx2, *cps)
    return res[0], res[1], res[2:]


def _log_sigmoids(z):
    l1p = jnp.log(1.0 + jnp.exp(-jnp.abs(z)))
    ls = jnp.minimum(z, 0.0) - l1p
    return ls, ls - z


def _strict_lower_ones(n):
    row = lax.broadcasted_iota(jnp.int32, (n, n), 0)
    col = lax.broadcasted_iota(jnp.int32, (n, n), 1)
    return row, col, (row > col).astype(BF16)


def _attn_fwd(qkv, name):
    s = qkv.shape[0]
    tb = ATT_BLOCK
    nq = s // tb

    def body(q_ref, k_ref, v_ref, o_ref, rs_ref, acc_ref, r_ref, rsv_ref):
        i = pl.program_id(1)
        lane = lax.broadcasted_iota(jnp.int32, (tb, LANES), 1)
        hmask = [lane < HEAD_DIM, lane >= HEAD_DIM]
        q2 = q_ref[...]
        qm = [jnp.where(m, q2, jnp.zeros((), BF16)) * jnp.asarray(HEAD_DIM ** -0.5, BF16) for m in hmask]
        row, col, tri = _strict_lower_ones(tb)
        acc_ref[...] = jnp.zeros_like(acc_ref)
        r_ref[...] = jnp.zeros_like(r_ref)
        rsv_ref[...] = jnp.full_like(rsv_ref, R_UNREACHED)

        def block(j, diag):
            rows = pl.ds(pl.multiple_of(j * tb, tb), tb)
            k2, v2 = k_ref[rows, :], v_ref[rows, :]
            for h in range(2):
                z = _dot_nt(qm[h], k2)
                ls, lk = _log_sigmoids(z)
                if diag:
                    lk = jnp.where(col < row, lk, 0.0)
                r = r_ref[h]
                logw = ls + _dot_hilo(lk, tri) + jnp.tile(r, (1, tb // LANES))
                w = jnp.exp(logw)
                if diag:
                    w = jnp.where(col < row, w, 0.0)
                acc_ref[h] += _dot(w.astype(BF16), v2)
                rsv_ref[h] = jnp.where(lane == j, r, rsv_ref[h])
                r_ref[h] = r + jnp.sum(lk, axis=1, keepdims=True)

        block(i, True)

        def live(c):
            jj, rmax = c
            return (jj < i) & (rmax >= CARRY_FLOOR)

        def step(c):
            jj, _ = c
            block(i - 1 - jj, False)
            return jj + 1, jnp.max(r_ref[...])

        lax.while_loop(live, step, (jnp.int32(0), jnp.max(r_ref[...])))
        o_ref[...] = jnp.where(hmask[0], acc_ref[0], acc_ref[1])
        rs_ref[...] = rsv_ref[...]

    return pl.pallas_call(
        body, name=name,
        out_shape=[jax.ShapeDtypeStruct((s, D_SB), F32), jax.ShapeDtypeStruct((N_HEADS, s, LANES), F32)],
        grid=(N_PAIRS, nq),
        in_specs=[pl.BlockSpec((tb, LANES), lambda p, i: (i, p)),
                  pl.BlockSpec((s, LANES), lambda p, i: (0, N_PAIRS + p)),
                  pl.BlockSpec((s, LANES), lambda p, i: (0, 2 * N_PAIRS + p))],
        out_specs=[pl.BlockSpec((tb, LANES), lambda p, i: (i, p)),
                   pl.BlockSpec((2, tb, LANES), lambda p, i: (p, i, 0))],
        scratch_shapes=[pltpu.VMEM((2, tb, LANES), F32), pltpu.VMEM((2, tb, LANES), F32),
                        pltpu.VMEM((2, tb, LANES), F32)],
        compiler_params=_cparams(("parallel", "arbitrary")),
    )(qkv, qkv, qkv)


def _attn_bwd(qkv, do, rs, dproj, name):
    s = qkv.shape[0]
    tb = ATT_BLOCK
    nq = s // tb
    scale = HEAD_DIM ** -0.5

    def body(q_ref, k_ref, v_ref, do_ref, rs_ref, dproj_hbm, out_hbm, dq_acc, dk_acc, dv_acc, dqi_ref, pc_ref,
             stage_ref, out_sems):
        del dproj_hbm
        pair = pl.program_id(0)
        lane = lax.broadcasted_iota(jnp.int32, (tb, LANES), 1)
        hmask = [lane < HEAD_DIM, lane >= HEAD_DIM]
        row, col, tri = _strict_lower_ones(tb)
        triu = (row < col).astype(BF16)
        zero = jnp.zeros((), BF16)
        dq_acc[...] = jnp.zeros_like(dq_acc)
        dk_acc[...] = jnp.zeros_like(dk_acc)
        dv_acc[...] = jnp.zeros_like(dv_acc)

        def qblock(i, carry):
            qrows = pl.ds(pl.multiple_of(i * tb, tb), tb)
            q2 = q_ref[qrows, :]
            do2 = do_ref[qrows, :]
            qm = [jnp.where(m, q2, zero) * jnp.asarray(scale, BF16) for m in hmask]
            dom = [jnp.where(m, do2, zero) for m in hmask]
            rs_i = [rs_ref[h, qrows, :] for h in range(2)]
            dqi_ref[...] = jnp.zeros_like(dqi_ref)
            pc_ref[...] = jnp.zeros_like(pc_ref)
            reached = jnp.maximum(jnp.max(rs_i[0], axis=0, keepdims=True), jnp.max(rs_i[1], axis=0, keepdims=True))
            key_block = lax.broadcasted_iota(jnp.int32, (1, LANES), 1)
            n_reached = jnp.sum(((reached >= CARRY_FLOOR) & (key_block <= i)).astype(jnp.int32))

            def block(j, diag):
                krows = pl.ds(pl.multiple_of(j * tb, tb), tb)
                k2 = k_ref[krows, :]
                v2 = v_ref[krows, :]
                dq_sum = None
                dk_sum = None
                dv_sum = None
                for h in range(2):
                    z = _dot_nt(qm[h], k2)
                    ls, lk = _log_sigmoids(z)
                    if diag:
                        lk = jnp.where(col < row, lk, 0.0)
                    r = jnp.sum(jnp.where(lane == j, rs_i[h], 0.0), axis=1, keepdims=True)
                    w = jnp.exp(ls + _dot_hilo(lk, tri) + r)
                    if diag:
                        w = jnp.where(col < row, w, 0.0)
                    g = w * _dot_nt(dom[h], v2)
                    pc = pc_ref[h]
                    p = _dot(g.astype(BF16), triu) + jnp.tile(pc, (1, tb // LANES))
                    sig = jnp.exp(ls)
                    dz = g - sig * (g + p)
                    if diag:
                        dz = jnp.where(col < row, dz, 0.0)
                    dzb = dz.astype(BF16)
                    pc_ref[h] = pc + jnp.sum(g, axis=1, keepdims=True)
                    dq_h = _dot(dzb, jnp.where(hmask[h], k2, zero))
                    dk_h = _dot_tn(dzb, qm[h])
                    dv_h = _dot_tn(w.astype(BF16), dom[h])
                    dq_sum = dq_h if dq_sum is None else dq_sum + dq_h
                    dk_sum = dk_h if dk_sum is None else dk_sum + dk_h
                    dv_sum = dv_h if dv_sum is None else dv_sum + dv_h
                dqi_ref[...] += dq_sum
                dk_acc[krows, :] += dk_sum
                dv_acc[krows, :] += dv_sum

            def step(j, c):
                block(j, False)
                return c

            lax.fori_loop(i + 1 - n_reached, i, step, 0)
            block(i, True)
            dq_acc[qrows, :] += dqi_ref[...] * scale
            return carry

        lax.fori_loop(0, nq, qblock, 0)
        copies = []
        for t, acc in enumerate((dq_acc, dk_acc, dv_acc)):
            stage_ref[t] = acc[...].astype(BF16)
            col0 = pl.multiple_of(t * D_SB + pair * LANES, LANES)
            copies.append(pltpu.make_async_copy(stage_ref.at[t], out_hbm.at[:, pl.ds(col0, LANES)], out_sems.at[t]))
        for cp in copies:
            cp.start()
        for cp in copies:
            cp.wait()

    return pl.pallas_call(
        body, name=name,
        out_shape=jax.ShapeDtypeStruct(dproj.shape, BF16),
        grid=(N_PAIRS,),
        in_specs=[pl.BlockSpec((s, LANES), lambda p: (0, p)),
                  pl.BlockSpec((s, LANES), lambda p: (0, N_PAIRS + p)),
                  pl.BlockSpec((s, LANES), lambda p: (0, 2 * N_PAIRS + p)),
                  pl.BlockSpec((s, LANES), lambda p: (0, p)),
                  pl.BlockSpec((2, s, LANES), lambda p: (p, 0, 0)),
                  pl.BlockSpec(memory_space=pl.ANY)],
        out_specs=pl.BlockSpec(memory_space=pl.ANY),
        scratch_shapes=[pltpu.VMEM((s, LANES), F32), pltpu.VMEM((s, LANES), F32), pltpu.VMEM((s, LANES), F32),
                        pltpu.VMEM((tb, LANES), F32), pltpu.VMEM((2, tb, LANES), F32),
                        pltpu.VMEM((3, s, LANES), BF16), pltpu.SemaphoreType.DMA((3,))],
        input_output_aliases={5: 0},
        compiler_params=_cparams(("arbitrary",)),
    )(qkv, qkv, qkv, do, rs, dproj)


def _mid(o, rest, x, tgt, wua, wub, wout, fg, lng, lnb, wsp, bfull, gavg, name):
    s, d = x.shape
    ts = 256
    nt = s // ts
    nchunk = ts // SGU_CHUNK
    n_rest = rest.shape[1]

    def body(o_ref, rest_ref, x_ref, t_ref, wua_ref, wub_ref, wout_ref, fg_ref, lng_ref, lnb_ref, wsp_ref, bfull_ref,
             gavg_ref, loss_ref, dx2_ref, do_ref, dproj_ref, dwua_ref, dwub_ref, dwout_ref, dfg_ref, dlng_ref,
             dlnb_ref, dwsp_ref, dbfull_ref):
        step = pl.program_id(0)

        @pl.when(step == 0)
        def _():
            for ref in (loss_ref, dwua_ref, dwub_ref, dwout_ref, dfg_ref, dlng_ref, dlnb_ref, dwsp_ref, dbfull_ref):
                ref[...] = jnp.zeros_like(ref)

        gavg = gavg_ref[...]

        def gmean(a):
            return _dot_hilo(a, gavg)

        def colsum8(a):
            return jnp.sum(a.reshape(ts // 8, 8, a.shape[1]), axis=0)

        z_a = rest_ref[:, 0:512]
        u_b = rest_ref[:, 512:1024]
        v_b = rest_ref[:, 1024:1536]
        z_b = rest_ref[:, 1536:2048]
        g_a = rest_ref[:, 2048:2048 + d]
        g_b = rest_ref[:, 2048 + d:2048 + 2 * d]
        ov = o_ref[...]
        sa = _sigmoid(z_a)
        silu_a = z_a * sa
        y_a = ov * silu_a
        ug, dug_du = _gelu_and_grad(u_b)
        vg, dvg_dv = _gelu_and_grad(v_b)
        mu = gmean(vg)
        cen = vg - mu
        rstd_g = lax.rsqrt(gmean(cen * cen) + EPS)
        vhat = cen * rstd_g
        vn = vhat * lng_ref[...] + lnb_ref[...]
        vnb = vn.astype(BF16)

        t_idx = lax.broadcasted_iota(jnp.int32, (SGU_CHUNK, SGU_CHUNK), 0)
        s_idx = lax.broadcasted_iota(jnp.int32, (SGU_CHUNK, SGU_CHUNK), 1)
        causal = (s_idx // CHUNK) <= (t_idx // CHUNK)
        wm = [jnp.where(causal, wsp_ref[g], 0.0) for g in range(N_GROUPS)]
        wmb = [w.astype(BF16) for w in wm]
        wmtb = [w.T.astype(BF16) for w in wm]
        lane = lax.broadcasted_iota(jnp.int32, (SGU_CHUNK, LANES), 1)
        first = lane < GROUP_DIM
        bfull = bfull_ref[...]

        mixed_rows = []
        for n in range(nchunk):
            r0, r1 = n * SGU_CHUNK, (n + 1) * SGU_CHUNK
            pieces = []
            for p in range(N_GROUPS // 2):
                blk = vnb[r0:r1, p * LANES:(p + 1) * LANES]
                pieces.append(jnp.where(first, _dot(wmb[2 * p], blk), _dot(wmb[2 * p + 1], blk)))
            mixed_rows.append(jnp.concatenate(pieces, axis=1) + bfull)
        mixed = jnp.concatenate(mixed_rows, axis=0)
        sg = ug * mixed
        sb = _sigmoid(z_b)
        silu_b = z_b * sb
        y_b = sg * silu_b
        y_ab = y_a.astype(BF16)
        y_bb = y_b.astype(BF16)
        p_a = _dot(y_ab, wua_ref[...])
        p_b = _dot(y_bb, wub_ref[...])
        ga_s = _sigmoid(g_a)
        gb_s = _sigmoid(g_b)
        merged_b = (ga_s * p_a + gb_s * p_b).astype(BF16)
        x2 = x_ref[...] + _dot(merged_b, wout_ref[...])
        rstd = lax.rsqrt(jnp.mean(x2 * x2, axis=-1, keepdims=True) + EPS)
        xhat = x2 * rstd
        fg_v = fg_ref[...]
        diff = xhat * fg_v - t_ref[...]
        loss_ref[...] += 0.5 * jnp.sum(jnp.sum(diff * diff, axis=-1, keepdims=True) * (1.0 / d))

        dy = diff * (1.0 / d)
        dfg_ref[...] += colsum8(dy * xhat)
        dxh = dy * fg_v
        dx2 = rstd * (dxh - xhat * jnp.mean(dxh * xhat, axis=-1, keepdims=True))
        dx2_ref[...] = dx2
        dx2b = dx2.astype(BF16)
        dwout_ref[...] += _dot_tn(merged_b, dx2b)
        dmerged = _dot_nt(dx2b, wout_ref[...])
        dp_a = dmerged * ga_s
        dp_b = dmerged * gb_s
        dproj_ref[:, QKV_COLS + 2048:QKV_COLS + 2048 + d] = (dmerged * p_a * (ga_s * (1.0 - ga_s))).astype(BF16)
        dproj_ref[:, QKV_COLS + 2048 + d:QKV_COLS + 2048 + 2 * d] = (dmerged * p_b * (gb_s * (1.0 - gb_s))).astype(BF16)
        dp_ab = dp_a.astype(BF16)
        dp_bb = dp_b.astype(BF16)
        dwua_ref[...] += _dot_tn(y_ab, dp_ab)
        dwub_ref[...] += _dot_tn(y_bb, dp_bb)
        dy_a = _dot_nt(dp_ab, wua_ref[...])
        dy_b = _dot_nt(dp_bb, wub_ref[...])
        do_ref[...] = (dy_a * silu_a).astype(BF16)
        dproj_ref[:, QKV_COLS:QKV_COLS + 512] = (dy_a * ov * (sa * (1.0 + z_a * (1.0 - sa)))).astype(BF16)
        dsg = dy_b * silu_b
        dproj_ref[:, QKV_COLS + 1536:QKV_COLS + 2048] = (dy_b * sg * (sb * (1.0 + z_b * (1.0 - sb)))).astype(BF16)
        dproj_ref[:, QKV_COLS + 512:QKV_COLS + 1024] = (dsg * mixed * dug_du).astype(BF16)
        dmixed = dsg * ug
        dmb = dmixed.astype(BF16)
        zero = jnp.zeros((), BF16)
        dvn_rows = []
        db = jnp.zeros((SGU_CHUNK, D_SGU), F32)
        for n in range(nchunk):
            r0, r1 = n * SGU_CHUNK, (n + 1) * SGU_CHUNK
            db = db + dmixed[r0:r1, :]
            pieces = []
            for p in range(N_GROUPS // 2):
                cols = slice(p * LANES, (p + 1) * LANES)
                dm_blk = dmb[r0:r1, cols]
                vn_blk = vnb[r0:r1, cols]
                dwsp_ref[2 * p] += _dot_nt(jnp.where(first, dm_blk, zero), vn_blk)
                dwsp_ref[2 * p + 1] += _dot_nt(jnp.where(first, zero, dm_blk), vn_blk)
                pieces.append(jnp.where(first, _dot(wmtb[2 * p], dm_blk), _dot(wmtb[2 * p + 1], dm_blk)))
            dvn_rows.append(jnp.concatenate(pieces, axis=1))
        dbfull_ref[...] += db
        dvn = jnp.concatenate(dvn_rows, axis=0)
        dlng_ref[...] += colsum8(dvn * vhat)
        dlnb_ref[...] += colsum8(dvn)
        dvhat = dvn * lng_ref[...]
        dcen = rstd_g * (dvhat - gmean(dvhat) - vhat * gmean(dvhat * vhat))
        dproj_ref[:, QKV_COLS + 1024:QKV_COLS + 1536] = (dcen * dvg_dv).astype(BF16)

        @pl.when(step == nt - 1)
        def _():
            for g in range(N_GROUPS):
                dwsp_ref[g] = jnp.where(causal, dwsp_ref[g], 0.0)

    def tile(cols):
        return pl.BlockSpec((ts, cols), lambda i: (i, 0))

    def whole(shape):
        return pl.BlockSpec(shape, lambda i: (0,) * len(shape))

    out_shapes = [
        jax.ShapeDtypeStruct((8, LANES), F32),
        jax.ShapeDtypeStruct((s, d), F32),
        jax.ShapeDtypeStruct((s, D_SB), BF16),
        jax.ShapeDtypeStruct((s, QKV_COLS + n_rest), BF16),
        jax.ShapeDtypeStruct((D_SB, d), F32),
        jax.ShapeDtypeStruct((D_SGU, d), F32),
        jax.ShapeDtypeStruct((d, d), F32),
        jax.ShapeDtypeStruct((8, d), F32),
        jax.ShapeDtypeStruct((8, D_SGU), F32),
        jax.ShapeDtypeStruct((8, D_SGU), F32),
        jax.ShapeDtypeStruct((N_GROUPS, SGU_CHUNK, SGU_CHUNK), F32),
        jax.ShapeDtypeStruct((SGU_CHUNK, D_SGU), F32),
    ]
    out_specs = [whole((8, LANES)), tile(d), tile(D_SB), tile(QKV_COLS + n_rest), whole((D_SB, d)), whole((D_SGU, d)),
                 whole((d, d)), whole((8, d)), whole((8, D_SGU)), whole((8, D_SGU)),
                 whole((N_GROUPS, SGU_CHUNK, SGU_CHUNK)), whole((SGU_CHUNK, D_SGU))]
    in_specs = [tile(D_SB), tile(n_rest), tile(d), tile(d), whole((D_SB, d)), whole((D_SGU, d)), whole((d, d)),
                whole((1, d)), whole((1, D_SGU)), whole((1, D_SGU)), whole((N_GROUPS, SGU_CHUNK, SGU_CHUNK)),
                whole((SGU_CHUNK, D_SGU)), whole((D_SGU, D_SGU))]
    return pl.pallas_call(
        body, name=name, out_shape=out_shapes, grid=(nt,), in_specs=in_specs, out_specs=out_specs,
        compiler_params=_cparams(("arbitrary",)),
    )(o, rest, x, tgt, wua, wub, wout, fg, lng, lnb, wsp, bfull, gavg)


def _small_reduce(dfg8, dlng8, dlnb8, dbfull, name):
    d = dfg8.shape[1]

    def body(dfg_ref, dlng_ref, dlnb_ref, dbfull_ref, fg_out, lng_out, lnb_out, b_out):
        fg_out[...] = jnp.sum(dfg_ref[...], axis=0, keepdims=True)
        lng_out[...] = jnp.sum(dlng_ref[...], axis=0, keepdims=True)
        lnb_out[...] = jnp.sum(dlnb_ref[...], axis=0, keepdims=True)
        grp = lax.broadcasted_iota(jnp.int32, (D_SGU, LANES), 0) // GROUP_DIM
        col = lax.broadcasted_iota(jnp.int32, (D_SGU, LANES), 1)
        sel = (grp == col).astype(BF16)
        a = dbfull_ref[...]
        hi, lo = _split_hi_lo(a)
        lo2 = (a - hi.astype(F32) - lo.astype(F32)).astype(BF16)
        by_pos = _dot(hi, sel) + _dot(lo, sel) + _dot(lo2, sel)
        b_out[...] = by_pos.T[0:N_GROUPS, :]

    return pl.pallas_call(
        body, name=name,
        out_shape=[jax.ShapeDtypeStruct((1, d), F32), jax.ShapeDtypeStruct((1, D_SGU), F32),
                   jax.ShapeDtypeStruct((1, D_SGU), F32), jax.ShapeDtypeStruct((N_GROUPS, SGU_CHUNK), F32)],
        compiler_params=_cparams(),
    )(dfg8, dlng8, dlnb8, dbfull)


def _block_major_cols(w):
    r, n = w.shape
    return jnp.transpose(w.reshape(r, N_DEV, n // N_DEV), (1, 0, 2))


def _from_block_major_cols(w):
    nb, r, c = w.shape
    return jnp.transpose(w, (1, 0, 2)).reshape(r, nb * c)


def kernel(x, norm_g, w_in, sgu_ln_g, sgu_ln_b, w_spatial, b_spatial, w_up_a, w_up_b, w_out, final_norm_g, loss_target, m_norm_g, m_w_in, m_sgu_ln_g, m_sgu_ln_b, m_w_spatial, m_b_spatial, m_w_up_a, m_w_up_b, m_w_out, m_final_norm_g, v_norm_g, v_w_in, v_sgu_ln_g, v_sgu_ln_b, v_w_spatial, v_b_spatial, v_w_up_a, v_w_up_b, v_w_out, v_final_norm_g):
    s, d = x.shape[1], x.shape[2]
    xs = x[0]
    tgt = loss_target[0]
    cx, cy, cc = _coords()
    core = jnp.reshape(cc, (1,)).astype(jnp.int32)
    chip = jnp.reshape(2 * cx + cy, (1,)).astype(jnp.int32)

    g_win, g_wua, g_wub, g_wout = _allgather(
        [w_in[0].astype(BF16), w_up_a[0].astype(BF16), w_up_b[0].astype(BF16), w_out[0].astype(BF16)], "ag_weights")
    w_full = _from_block_major_cols(g_win)
    wua_full = _from_block_major_cols(g_wua)
    wub_full = _from_block_major_cols(g_wub)
    wout_full = g_wout.reshape(d, d)

    h, ht = _rmsnorm_fwd(xs, norm_g, "norm")
    d_in = w_full.shape[1]
    qkv = _inproj(h, w_full, 0, QKV_COLS, BF16, "inproj_qkv")
    rest = _inproj(h, w_full, QKV_COLS, d_in - QKV_COLS, F32, "inproj_rest")
    o, rs = _attn_fwd(qkv, "attn_fwd")

    lng = sgu_ln_g.reshape(1, D_SGU)
    lnb = sgu_ln_b.reshape(1, D_SGU)
    bfull = jnp.repeat(jnp.transpose(b_spatial[0]), GROUP_DIM, axis=1)
    grp = jnp.arange(D_SGU) // GROUP_DIM
    gavg = jnp.where(grp[:, None] == grp[None, :], 1.0 / GROUP_DIM, 0.0).astype(BF16)
    (loss_b, dx2, do, dproj, dwua, dwub, dwout, dfg8, dlng8, dlnb8, dwsp, dbfull) = _mid(
        o, rest, xs, tgt, wua_full, wub_full, wout_full, final_norm_g.reshape(1, d), lng, lnb, w_spatial[0], bfull,
        gavg, "mid")

    dproj = _attn_bwd(qkv, do, rs, dproj, "attn_bwd")
    dfg, dlng, dlnb, db = _small_reduce(dfg8, dlng8, dlnb8, dbfull, "small_reduce")
    wsp_rows = N_GROUPS * SGU_CHUNK
    dwin, small_parts = _dw_in(ht, dproj, [dfg, dlng, dlnb, db, dwsp.reshape(wsp_rows, SGU_CHUNK)], "dwin")

    def blocks42(a):
        return a.reshape((4, 2) + a.shape[1:])

    own = [blocks42(_block_major_cols(dwin)),
           blocks42(_block_major_cols(dwua.astype(BF16))),
           blocks42(_block_major_cols(dwub.astype(BF16))),
           blocks42(dwout.astype(BF16).reshape(N_DEV, d // N_DEV, d))]
    land1 = _push_sibling(own, "rs_sibling")
    cps = [_chip_partial_sum(a, l, core, "cpsum%d" % i) for i, (a, l) in enumerate(zip(own, land1))]
    grad_x, dng8, land2 = _dh_and_grad_x(dproj, w_full, xs, norm_g, dx2, cps, "dh")

    (ng_parts,) = _allgather([dng8], "ag_norm_g")

    def small_layouts(ng, fg, lg, lb, bs, ws):
        return [ng.reshape(1, d), fg.reshape(1, d), lg.reshape(1, D_SGU), lb.reshape(1, D_SGU),
                bs.reshape(N_GROUPS, SGU_CHUNK), ws.reshape(wsp_rows, SGU_CHUNK)]

    sm = _adam_small([ng_parts] + list(small_parts),
                     small_layouts(norm_g, final_norm_g, sgu_ln_g, sgu_ln_b, b_spatial, w_spatial),
                     small_layouts(m_norm_g, m_final_norm_g, m_sgu_ln_g, m_sgu_ln_b, m_b_spatial, m_w_spatial),
                     small_layouts(v_norm_g, v_final_norm_g, v_sgu_ln_g, v_sgu_ln_b, v_b_spatial, v_w_spatial),
                     "adam_small")
    small_shapes = [norm_g.shape, final_norm_g.shape, sgu_ln_g.shape, sgu_ln_b.shape, b_spatial.shape, w_spatial.shape]
    sm = [[a.reshape(shp) for a, shp in zip(kind, small_shapes)] for kind in sm]

    big = []
    for i, (w, m, v) in enumerate([(w_in, m_w_in, v_w_in), (w_up_a, m_w_up_a, v_w_up_a),
                                   (w_up_b, m_w_up_b, v_w_up_b), (w_out, m_w_out, v_w_out)]):
        res = _adam_shard(cps[i], land2[i], chip, w[0], m[0], v[0], "adam%d" % i)
        big.append([r[None] for r in res])

    loss = lax.psum(loss_b[0, 0], ("x", "y", "c"))

    def per_kind(kd):
        return [sm[kd][0], big[0][kd], sm[kd][2], sm[kd][3], sm[kd][5], sm[kd][4], big[1][kd], big[2][kd], big[3][kd],
                sm[kd][1]]

    return (loss, grad_x[None], *per_kind(0), *per_kind(1), *per_kind(2), *per_kind(3))
```

```python
import math

import jax
import jax.numpy as jnp
from jax import lax
from jax.experimental import pallas as pl
from jax.experimental.pallas import tpu as pltpu

F32 = jnp.float32
BF16 = jnp.bfloat16
MESH = pl.DeviceIdType.MESH

N_DEV = 8
N_HEADS = 8
HEAD_DIM = 64
D_SB = N_HEADS * HEAD_DIM
N_GROUPS = 8
GROUP_DIM = 64
D_SGU = N_GROUPS * GROUP_DIM
SGU_CHUNK = 128
CHUNK = 64
EPS = 1e-6
LANES = 128
N_PAIRS = N_HEADS // 2
QKV_COLS = 3 * D_SB
ATT_BLOCK = 256
CARRY_FLOOR = -110.0
R_UNREACHED = -1e30

ADAM_LR = 0.001
ADAM_B1 = 0.9
ADAM_B2 = 0.999
ADAM_EPS = 1e-08
ADAM_WD = 0.01
ADAM_STEP = 10

VMEM_LIMIT = 56 * 1024 * 1024


def _cparams(sem=None, vmem=VMEM_LIMIT):
    return pltpu.CompilerParams(dimension_semantics=sem, vmem_limit_bytes=vmem)


def _dot(a, b):
    return jnp.dot(a, b, preferred_element_type=F32)


def _dot_nt(a, b):
    return lax.dot_general(a, b, (((1,), (1,)), ((), ())), preferred_element_type=F32)


def _dot_tn(a, b):
    return lax.dot_general(a, b, (((0,), (0,)), ((), ())), preferred_element_type=F32)


def _split_hi_lo(a):
    hi = a.astype(BF16)
    lo = (a - hi.astype(F32)).astype(BF16)
    return hi, lo


def _sigmoid(x):
    return 1.0 / (1.0 + jnp.exp(-x))


_GELU_C = math.sqrt(2.0 / math.pi)


def _gelu_and_grad(x):
    x2 = x * x
    inner = _GELU_C * (x + 0.044715 * (x2 * x))
    t = jnp.tanh(inner)
    cdf = 0.5 * (1.0 + t)
    g = x * cdf
    dg = cdf + x * (0.5 * (1.0 - t * t)) * (_GELU_C * (1.0 + 3.0 * 0.044715 * x2))
    return g, dg


def _coords():
    return lax.axis_index("x"), lax.axis_index("y"), lax.axis_index("c")


def _dev_index(px, py, pc):
    return 4 * px + 2 * py + pc


def _allgather(blocks, name):
    n = len(blocks)

    def body(*refs):
        gather = _Gather(refs[:n], refs[n:2 * n], *refs[2 * n:])
        gather.issue()
        gather.finish()

    any_spec = pl.BlockSpec(memory_space=pl.ANY)
    return pl.pallas_call(
        body, name=name,
        out_shape=_gather_out_shapes(blocks),
        in_specs=[any_spec] * n, out_specs=[any_spec] * n,
        scratch_shapes=_gather_semaphores(n),
    )(*blocks)


def _gather_out_shapes(blocks):
    return [jax.ShapeDtypeStruct((N_DEV,) + b.shape, b.dtype) for b in blocks]


def _gather_semaphores(n):
    return [pltpu.SemaphoreType.DMA((n, 7)), pltpu.SemaphoreType.DMA((n, 7)), pltpu.SemaphoreType.DMA((n,))]


class _Gather:
    def __init__(self, ins, outs, send_sems, recv_sems, local_sems):
        self.ins, self.outs = ins, outs
        self.send_sems, self.recv_sems, self.local_sems = send_sems, recv_sems, local_sems
        self.n = len(ins)
        x, y, c = _coords()
        self.c = c
        self.me, self.sibling = (x, y, c), (x, y, 1 - c)
        self.chips = [(1 - x, y), (x, 1 - y), (1 - x, 1 - y)]

    def _copy(self, a, k, block, to, src=None):
        dst = self.outs[a].at[_dev_index(*block)]
        return pltpu.make_async_remote_copy(
            src_ref=dst if src is None else src, dst_ref=dst,
            send_sem=self.send_sems.at[a, k], recv_sem=self.recv_sems.at[a, k],
            device_id=to, device_id_type=MESH)

    def _mine(self):
        return [pltpu.make_async_copy(self.ins[a], self.outs[a].at[_dev_index(*self.me)], self.local_sems.at[a])
                for a in range(self.n)]

    def _first(self):
        first = []
        for a in range(self.n):
            first.append(self._copy(a, 0, self.me, self.sibling, src=self.ins[a]))
            first += [self._copy(a, 1 + j, self.me, (*chip, self.c), src=self.ins[a])
                      for j, chip in enumerate(self.chips)]
        return first

    def issue(self):
        for cp in self._mine() + self._first():
            cp.start()

    def finish(self):
        c = self.c
        passed = []
        for j, chip in enumerate(self.chips):
            for a in range(self.n):
                self._copy(a, 1 + j, (*chip, c), self.me).wait_recv()
                fwd = self._copy(a, 4 + j, (*chip, c), self.sibling)
                fwd.start()
                passed.append(fwd)
        for a in range(self.n):
            self._copy(a, 0, self.sibling, self.me).wait_recv()
            for j, chip in enumerate(self.chips):
                self._copy(a, 4 + j, (*chip, 1 - c), self.me).wait_recv()
        for cp in self._first() + passed:
            cp.wait_send()
        for cp in self._mine():
            cp.wait()


def _push_sibling(arrs, name):
    n = len(arrs)

    def body(*refs):
        ins, outs = refs[:n], refs[n:2 * n]
        send_sems, recv_sems = refs[2 * n:]
        x, y, c = _coords()
        sibling = (x, y, 1 - c)
        copies = []
        for a in range(n):
            for k in range(4):
                copies.append(pltpu.make_async_remote_copy(
                    src_ref=ins[a].at[k, 1 - c], dst_ref=outs[a].at[k],
                    send_sem=send_sems.at[a, k], recv_sem=recv_sems.at[a, k],
                    device_id=sibling, device_id_type=MESH))
        for cp in copies:
            cp.start()
        for cp in copies:
            cp.wait()

    any_spec = pl.BlockSpec(memory_space=pl.ANY)
    return pl.pallas_call(
        body, name=name,
        out_shape=[jax.ShapeDtypeStruct((4,) + a.shape[2:], a.dtype) for a in arrs],
        in_specs=[any_spec] * n, out_specs=[any_spec] * n,
        scratch_shapes=[pltpu.SemaphoreType.DMA((n, 4)), pltpu.SemaphoreType.DMA((n, 4))],
    )(*arrs)


def _chip_push_copies(ins, outs, send_sems, recv_sems):
    x, y, c = _coords()
    chips = [(1 - x, y), (x, 1 - y), (1 - x, 1 - y)]
    return [pltpu.make_async_remote_copy(
        src_ref=ins[a].at[2 * px + py], dst_ref=outs[a].at[r],
        send_sem=send_sems.at[a, r], recv_sem=recv_sems.at[a, r],
        device_id=(px, py, c), device_id_type=MESH)
        for a in range(len(ins)) for r, (px, py) in enumerate(chips)]


def _chip_partial_sum(own, land, core, name):
    _, _, rows, cols = own.shape
    tr = min(rows, 256)

    def body(core_ref, own_ref, land_ref, out_ref):
        del core_ref
        out_ref[...] = (own_ref[...].astype(F32) + land_ref[...].astype(F32)).astype(out_ref.dtype)

    return pl.pallas_call(
        body, name=name,
        out_shape=jax.ShapeDtypeStruct((4, rows, cols), own.dtype),
        grid_spec=pltpu.PrefetchScalarGridSpec(
            num_scalar_prefetch=1, grid=(4, rows // tr),
            in_specs=[pl.BlockSpec((None, None, tr, cols), lambda k, r, core: (k, core[0], r, 0)),
                      pl.BlockSpec((None, tr, cols), lambda k, r, core: (k, r, 0))],
            out_specs=pl.BlockSpec((None, tr, cols), lambda k, r, core: (k, r, 0))),
        compiler_params=_cparams(("parallel", "parallel")),
    )(core, own, land)


def _adamw_math(w, g, m, v):
    m = ADAM_B1 * m + (1.0 - ADAM_B1) * g
    v = ADAM_B2 * v + (1.0 - ADAM_B2) * (g * g)
    m_hat = m / (1.0 - ADAM_B1 ** ADAM_STEP)
    v_hat = v / (1.0 - ADAM_B2 ** ADAM_STEP)
    delta = -ADAM_LR * (m_hat / (jnp.sqrt(v_hat) + ADAM_EPS) + ADAM_WD * w)
    return delta, m, v


def _adam_shard(cp, land, chip, w, m, v, name):
    rows, cols = w.shape
    tr = min(rows, 256)

    def body(chip_ref, cp_ref, land_ref, w_ref, m_ref, v_ref, g_out, d_out, m_out, v_out):
        del chip_ref
        g = cp_ref[...].astype(F32)
        for r in range(3):
            g = g + land_ref[r].astype(F32)
        delta, m_new, v_new = _adamw_math(w_ref[...], g, m_ref[...], v_ref[...])
        g_out[...] = g
        d_out[...] = delta
        m_out[...] = m_new
        v_out[...] = v_new

    tile = pl.BlockSpec((tr, cols), lambda r, chip: (r, 0))
    out = jax.ShapeDtypeStruct((rows, cols), F32)
    return pl.pallas_call(
        body, name=name, out_shape=[out] * 4,
        grid_spec=pltpu.PrefetchScalarGridSpec(
            num_scalar_prefetch=1, grid=(rows // tr,),
            in_specs=[pl.BlockSpec((None, tr, cols), lambda r, chip: (chip[0], r, 0)),
                      pl.BlockSpec((3, tr, cols), lambda r, chip: (0, r, 0)),
                      tile, tile, tile],
            out_specs=[tile] * 4),
        compiler_params=_cparams(("parallel",)),
    )(chip, cp, land, w, m, v)


def _adam_small(parts, ws, ms, vs, loss_parts, name):
    n = len(ws)

    def body(*refs):
        p_refs, w_refs, m_refs, v_refs = refs[:n], refs[n:2 * n], refs[2 * n:3 * n], refs[3 * n:4 * n]
        loss_ref, outs, loss_out = refs[4 * n], refs[4 * n + 1:-1], refs[-1]
        total = loss_ref[0]
        for dev in range(1, N_DEV):
            total = total + loss_ref[dev]
        loss_out[...] = total
        for i in range(n):
            g = p_refs[i][0]
            for dev in range(1, N_DEV):
                g = g + p_refs[i][dev]
            if g.shape[0] != w_refs[i].shape[0]:
                g = jnp.sum(g, axis=0, keepdims=True)
            delta, m_new, v_new = _adamw_math(w_refs[i][...], g, m_refs[i][...], v_refs[i][...])
            outs[i][...] = g
            outs[n + i][...] = delta
            outs[2 * n + i][...] = m_new
            outs[3 * n + i][...] = v_new

    out_shapes = [jax.ShapeDtypeStruct(w.shape, F32) for w in ws] * 4
    out_shapes.append(jax.ShapeDtypeStruct(loss_parts.shape[1:], F32))
    res = pl.pallas_call(body, name=name, out_shape=out_shapes, compiler_params=_cparams())(
        *parts, *ws, *ms, *vs, loss_parts)
    return [res[k * n:(k + 1) * n] for k in range(4)], res[-1]


def _rmsnorm_fwd(x, g, name):
    s, d = x.shape
    ts = 512

    def body(x_ref, g_ref, h_ref, ht_ref):
        xv = x_ref[...]
        rstd = lax.rsqrt(jnp.mean(xv * xv, axis=-1, keepdims=True) + EPS)
        h = xv * rstd * g_ref[...]
        h_ref[...] = h.astype(BF16)
        ht_ref[...] = h.T.astype(BF16)

    return pl.pallas_call(
        body, name=name,
        out_shape=[jax.ShapeDtypeStruct((s, d), BF16), jax.ShapeDtypeStruct((d, s), BF16)],
        grid=(s // ts,),
        in_specs=[pl.BlockSpec((ts, d), lambda i: (i, 0)), pl.BlockSpec((1, d), lambda i: (0, 0))],
        out_specs=[pl.BlockSpec((ts, d), lambda i: (i, 0)), pl.BlockSpec((d, ts), lambda i: (0, i))],
        compiler_params=_cparams(("parallel",)),
    )(x, g)


def _inproj(h, w, col0, ncols, out_dtype, riders, name):
    s, d = h.shape
    tn = 256
    j0 = col0 // tn
    nj = ncols // tn
    nr = len(riders)

    def body(h_ref, w_ref, *rest):
        rider_in, o_ref, rider_out, sems = rest[:nr], rest[nr], rest[nr + 1:2 * nr + 1], rest[2 * nr + 1:]
        j = pl.program_id(0)
        if nr:
            gather = _Gather(rider_in, rider_out, *sems)

            @pl.when(j == 0)
            def _():
                gather.issue()

        o_ref[...] = _dot(h_ref[...], w_ref[...]).astype(o_ref.dtype)

        if nr:
            @pl.when(j == nj - 1)
            def _():
                gather.finish()

    any_spec = pl.BlockSpec(memory_space=pl.ANY)
    res = pl.pallas_call(
        body, name=name, out_shape=[jax.ShapeDtypeStruct((s, ncols), out_dtype)] + _gather_out_shapes(riders),
        grid=(nj,),
        in_specs=[pl.BlockSpec((s, d), lambda j: (0, 0)), pl.BlockSpec((d, tn), lambda j: (0, j + j0))]
        + [any_spec] * nr,
        out_specs=[pl.BlockSpec((s, tn), lambda j: (0, j))] + [any_spec] * nr,
        scratch_shapes=_gather_semaphores(nr) if nr else [],
        compiler_params=_cparams(("arbitrary",) if nr else ("parallel",)),
    )(h, w, *riders)
    return res[0], res[1:]


def _dw_in(ht, dproj, smalls, name):
    d, s = ht.shape
    n = dproj.shape[1]
    tn = 512
    nj = n // tn
    ns = len(smalls)

    def body(a_ref, b_ref, *rest):
        small_in, o_ref, small_out, sems = rest[:ns], rest[ns], rest[ns + 1:2 * ns + 1], rest[2 * ns + 1:]
        j = pl.program_id(0)
        gather = _Gather(small_in, small_out, *sems)

        @pl.when(j == 0)
        def _():
            gather.issue()

        o_ref[...] = _dot(a_ref[...], b_ref[...]).astype(o_ref.dtype)

        @pl.when(j == nj - 1)
        def _():
            gather.finish()

    any_spec = pl.BlockSpec(memory_space=pl.ANY)
    res = pl.pallas_call(
        body, name=name,
        out_shape=[jax.ShapeDtypeStruct((d, n), BF16)] + _gather_out_shapes(smalls),
        grid=(nj,),
        in_specs=[pl.BlockSpec((d, s), lambda j: (0, 0)), pl.BlockSpec((s, tn), lambda j: (0, j))] + [any_spec] * ns,
        out_specs=[pl.BlockSpec((d, tn), lambda j: (0, j))] + [any_spec] * ns,
        scratch_shapes=_gather_semaphores(ns),
        compiler_params=_cparams(("arbitrary",)),
    )(ht, dproj, *smalls)
    return res[0], res[1:]


def _dh_and_grad_x(dproj, w, x, g, dx2, cps, name):
    s, n = dproj.shape
    d = w.shape[0]
    tm, tk = min(s, 1024), 512
    nk = n // tk
    nm = s // tm
    nc = len(cps)

    def body(a_ref, w_ref, x_ref, g_ref, dx2_ref, *rest):
        cp_refs, gx_ref, dg_ref = rest[:nc], rest[nc], rest[nc + 1]
        land_refs = rest[nc + 2:2 * nc + 2]
        acc_ref, send_sems, recv_sems = rest[2 * nc + 2:]
        i, k = pl.program_id(0), pl.program_id(1)

        @pl.when((i == 0) & (k == 0))
        def _():
            for cp in _chip_push_copies(cp_refs, land_refs, send_sems, recv_sems):
                cp.start()
            dg_ref[...] = jnp.zeros_like(dg_ref)

        @pl.when(k == 0)
        def _():
            acc_ref[...] = jnp.zeros_like(acc_ref)

        acc_ref[...] += _dot_nt(a_ref[...], w_ref[...])

        @pl.when(k == nk - 1)
        def _():
            dh = acc_ref[...]
            xv = x_ref[...]
            rstd = lax.rsqrt(jnp.mean(xv * xv, axis=-1, keepdims=True) + EPS)
            xhat = xv * rstd
            dg_ref[...] += jnp.sum((dh * xhat).reshape(tm // 8, 8, d), axis=0)
            dxh = dh * g_ref[...]
            gx_ref[...] = dx2_ref[...] + rstd * (dxh - xhat * jnp.mean(dxh * xhat, axis=-1, keepdims=True))

        @pl.when((i == nm - 1) & (k == nk - 1))
        def _():
            for cp in _chip_push_copies(cp_refs, land_refs, send_sems, recv_sems):
                cp.wait()

    any_spec = pl.BlockSpec(memory_space=pl.ANY)
    res = pl.pallas_call(
        body, name=name,
        out_shape=[jax.ShapeDtypeStruct((s, d), F32), jax.ShapeDtypeStruct((8, d), F32)]
        + [jax.ShapeDtypeStruct((3,) + a.shape[1:], a.dtype) for a in cps],
        grid=(nm, nk),
        in_specs=[pl.BlockSpec((tm, tk), lambda i, k: (i, k)), pl.BlockSpec((d, tk), lambda i, k: (0, k)),
                  pl.BlockSpec((tm, d), lambda i, k: (i, 0)), pl.BlockSpec((1, d), lambda i, k: (0, 0)),
                  pl.BlockSpec((tm, d), lambda i, k: (i, 0))] + [any_spec] * nc,
        out_specs=[pl.BlockSpec((tm, d), lambda i, k: (i, 0)), pl.BlockSpec((8, d), lambda i, k: (0, 0))]
        + [any_spec] * nc,
        scratch_shapes=[pltpu.VMEM((tm, d), F32), pltpu.SemaphoreType.DMA((nc, 3)), pltpu.SemaphoreType.DMA((nc, 3))],
        compiler_params=_cparams(("arbitrary", "arbitrary")),
    )(dproj, w, x, g, dx2, *cps)
    return res[0], res[1], res[2:]


def _log_sigmoids(z):
    l1p = jnp.log(1.0 + jnp.exp(-jnp.abs(z)))
    ls = jnp.minimum(z, 0.0) - l1p
    return ls, ls - z


def _strict_lower_ones(n):
    row = lax.broadcasted_iota(jnp.int32, (n, n), 0)
    col = lax.broadcasted_iota(jnp.int32, (n, n), 1)
    return row, col, (row > col).astype(BF16)


def _attn_fwd(qkv, name):
    s = qkv.shape[0]
    tb = ATT_BLOCK
    nq = s // tb

    def body(q_ref, k_ref, v_ref, o_ref, rs_ref, acc_ref, r_ref, rsv_ref):
        i = pl.program_id(1)
        lane = lax.broadcasted_iota(jnp.int32, (tb, LANES), 1)
        hmask = [lane < HEAD_DIM, lane >= HEAD_DIM]
        q2 = q_ref[...]
        qm = [jnp.where(m, q2, jnp.zeros((), BF16)) * jnp.asarray(HEAD_DIM ** -0.5, BF16) for m in hmask]
        row, col, tri = _strict_lower_ones(tb)
        tri2 = jnp.concatenate([tri, tri], axis=0)
        acc_ref[...] = jnp.zeros_like(acc_ref)
        r_ref[...] = jnp.zeros_like(r_ref)
        rsv_ref[...] = jnp.full_like(rsv_ref, R_UNREACHED)

        def block(j, diag):
            heads = range(2)
            rows = pl.ds(pl.multiple_of(j * tb, tb), tb)
            k2, v2 = k_ref[rows, :], v_ref[rows, :]
            z = [_dot_nt(qm[h], k2) for h in heads]
            ls, lk = [], []
            for h in heads:
                ls_h, lk_h = _log_sigmoids(z[h])
                if diag:
                    lk_h = jnp.where(col < row, lk_h, 0.0)
                ls.append(ls_h)
                lk.append(lk_h)
            r = [r_ref[h] for h in heads]
            suffix = [_dot(jnp.concatenate(_split_hi_lo(lk[h]), axis=1), tri2) for h in heads]
            w = []
            for h in heads:
                w_h = jnp.exp(ls[h] + suffix[h] + jnp.tile(r[h], (1, tb // LANES)))
                if diag:
                    w_h = jnp.where(col < row, w_h, 0.0)
                w.append(w_h.astype(BF16))
            pv = [_dot(w[h], v2) for h in heads]
            for h in heads:
                acc_ref[h] += pv[h]
                rsv_ref[h] = jnp.where(lane == j, r[h], rsv_ref[h])
                r_ref[h] = r[h] + jnp.sum(lk[h], axis=1, keepdims=True)

        block(i, True)

        def live(c):
            jj, rmax = c
            return (jj < i) & (rmax >= CARRY_FLOOR)

        def step(c):
            jj, _ = c
            block(i - 1 - jj, False)
            return jj + 1, jnp.max(r_ref[...])

        lax.while_loop(live, step, (jnp.int32(0), jnp.max(r_ref[...])))
        o_ref[...] = jnp.where(hmask[0], acc_ref[0], acc_ref[1])
        rs_ref[...] = rsv_ref[...]

    return pl.pallas_call(
        body, name=name,
        out_shape=[jax.ShapeDtypeStruct((s, D_SB), F32), jax.ShapeDtypeStruct((N_HEADS, s, LANES), F32)],
        grid=(N_PAIRS, nq),
        in_specs=[pl.BlockSpec((tb, LANES), lambda p, i: (i, p)),
                  pl.BlockSpec((s, LANES), lambda p, i: (0, N_PAIRS + p)),
                  pl.BlockSpec((s, LANES), lambda p, i: (0, 2 * N_PAIRS + p))],
        out_specs=[pl.BlockSpec((tb, LANES), lambda p, i: (i, p)),
                   pl.BlockSpec((2, tb, LANES), lambda p, i: (p, i, 0))],
        scratch_shapes=[pltpu.VMEM((2, tb, LANES), F32), pltpu.VMEM((2, tb, LANES), F32),
                        pltpu.VMEM((2, tb, LANES), F32)],
        compiler_params=_cparams(("parallel", "arbitrary")),
    )(qkv, qkv, qkv)


def _attn_bwd(qkv, do, rs, dproj, name):
    s = qkv.shape[0]
    tb = ATT_BLOCK
    nq = s // tb
    scale = HEAD_DIM ** -0.5

    def body(q_ref, k_ref, v_ref, do_ref, rs_ref, dproj_hbm, out_hbm, dq_acc, dk_acc, dv_acc, dqi_ref, pc_ref,
             stage_ref, out_sems):
        del dproj_hbm
        pair = pl.program_id(0)
        lane = lax.broadcasted_iota(jnp.int32, (tb, LANES), 1)
        hmask = [lane < HEAD_DIM, lane >= HEAD_DIM]
        row, col, tri = _strict_lower_ones(tb)
        triu = (row < col).astype(BF16)
        tri2 = jnp.concatenate([tri, tri], axis=0)
        zero = jnp.zeros((), BF16)
        dq_acc[...] = jnp.zeros_like(dq_acc)
        dk_acc[...] = jnp.zeros_like(dk_acc)
        dv_acc[...] = jnp.zeros_like(dv_acc)

        def qblock(i, carry):
            qrows = pl.ds(pl.multiple_of(i * tb, tb), tb)
            q2 = q_ref[qrows, :]
            do2 = do_ref[qrows, :]
            qm = [jnp.where(m, q2, zero) * jnp.asarray(scale, BF16) for m in hmask]
            dom = [jnp.where(m, do2, zero) for m in hmask]
            rs_i = [rs_ref[h, qrows, :] for h in range(2)]
            dqi_ref[...] = jnp.zeros_like(dqi_ref)
            pc_ref[...] = jnp.zeros_like(pc_ref)
            reached = jnp.maximum(jnp.max(rs_i[0], axis=0, keepdims=True), jnp.max(rs_i[1], axis=0, keepdims=True))
            key_block = lax.broadcasted_iota(jnp.int32, (1, LANES), 1)
            n_reached = jnp.sum(((reached >= CARRY_FLOOR) & (key_block <= i)).astype(jnp.int32))

            def block(j, diag):
                krows = pl.ds(pl.multiple_of(j * tb, tb), tb)
                k2 = k_ref[krows, :]
                v2 = v_ref[krows, :]
                heads = range(2)
                z = [_dot_nt(qm[h], k2) for h in heads]
                dw = [_dot_nt(dom[h], v2) for h in heads]
                ls, lk = [], []
                for h in heads:
                    ls_h, lk_h = _log_sigmoids(z[h])
                    if diag:
                        lk_h = jnp.where(col < row, lk_h, 0.0)
                    ls.append(ls_h)
                    lk.append(lk_h)
                r = [jnp.sum(jnp.where(lane == j, rs_i[h], 0.0), axis=1, keepdims=True) for h in heads]
                suffix = [_dot(jnp.concatenate(_split_hi_lo(lk[h]), axis=1), tri2) for h in heads]
                w, g = [], []
                for h in heads:
                    w_h = jnp.exp(ls[h] + suffix[h] + r[h])
                    if diag:
                        w_h = jnp.where(col < row, w_h, 0.0)
                    w.append(w_h.astype(BF16))
                    g.append(w_h * dw[h])
                pc = [pc_ref[h] for h in heads]
                prefix = [_dot(g[h].astype(BF16), triu) for h in heads]
                dzb = []
                for h in heads:
                    sig = jnp.exp(ls[h])
                    dz = g[h] - sig * (g[h] + prefix[h] + jnp.tile(pc[h], (1, tb // LANES)))
                    if diag:
                        dz = jnp.where(col < row, dz, 0.0)
                    dzb.append(dz.astype(BF16))
                    pc_ref[h] = pc[h] + jnp.sum(g[h], axis=1, keepdims=True)
                dq = [_dot(dzb[h], jnp.where(hmask[h], k2, zero)) for h in heads]
                dk = [_dot_tn(dzb[h], qm[h]) for h in heads]
                dv = [_dot_tn(w[h], dom[h]) for h in heads]
                dqi_ref[...] += dq[0] + dq[1]
                dk_acc[krows, :] += dk[0] + dk[1]
                dv_acc[krows, :] += dv[0] + dv[1]

            def step(j, c):
                block(j, False)
                return c

            lax.fori_loop(i + 1 - n_reached, i, step, 0)
            block(i, True)
            dq_acc[qrows, :] += dqi_ref[...] * scale
            return carry

        lax.fori_loop(0, nq, qblock, 0)
        copies = []
        for t, acc in enumerate((dq_acc, dk_acc, dv_acc)):
            stage_ref[t] = acc[...].astype(BF16)
            col0 = pl.multiple_of(t * D_SB + pair * LANES, LANES)
            copies.append(pltpu.make_async_copy(stage_ref.at[t], out_hbm.at[:, pl.ds(col0, LANES)], out_sems.at[t]))
        for cp in copies:
            cp.start()
        for cp in copies:
            cp.wait()

    return pl.pallas_call(
        body, name=name,
        out_shape=jax.ShapeDtypeStruct(dproj.shape, BF16),
        grid=(N_PAIRS,),
        in_specs=[pl.BlockSpec((s, LANES), lambda p: (0, p)),
                  pl.BlockSpec((s, LANES), lambda p: (0, N_PAIRS + p)),
                  pl.BlockSpec((s, LANES), lambda p: (0, 2 * N_PAIRS + p)),
                  pl.BlockSpec((s, LANES), lambda p: (0, p)),
                  pl.BlockSpec((2, s, LANES), lambda p: (p, 0, 0)),
                  pl.BlockSpec(memory_space=pl.ANY)],
        out_specs=pl.BlockSpec(memory_space=pl.ANY),
        scratch_shapes=[pltpu.VMEM((s, LANES), F32), pltpu.VMEM((s, LANES), F32), pltpu.VMEM((s, LANES), F32),
                        pltpu.VMEM((tb, LANES), F32), pltpu.VMEM((2, tb, LANES), F32),
                        pltpu.VMEM((3, s, LANES), BF16), pltpu.SemaphoreType.DMA((3,))],
        input_output_aliases={5: 0},
        compiler_params=_cparams(("arbitrary",)),
    )(qkv, qkv, qkv, do, rs, dproj)


def _mid(o, rest, x, tgt, wua, wub, wout, fg, lng, lnb, wsp, bfull, gavg, name):
    s, d = x.shape
    ts = 256
    nt = s // ts
    nchunk = ts // SGU_CHUNK
    n_rest = rest.shape[1]

    def body(o_ref, rest_ref, x_ref, t_ref, wua_ref, wub_ref, wout_ref, fg_ref, lng_ref, lnb_ref, wsp_ref, bfull_ref,
             gavg_ref, loss_ref, dx2_ref, do_ref, dproj_ref, dwua_ref, dwub_ref, dwout_ref, dfg_ref, dlng_ref,
             dlnb_ref, dwsp_ref, dbfull_ref):
        step = pl.program_id(0)

        @pl.when(step == 0)
        def _():
            for ref in (loss_ref, dwua_ref, dwub_ref, dwout_ref, dfg_ref, dlng_ref, dlnb_ref, dwsp_ref, dbfull_ref):
                ref[...] = jnp.zeros_like(ref)

        gavg = gavg_ref[...]

        def gmean(a):
            return _dot(a.astype(BF16), gavg)

        def colsum8(a):
            return jnp.sum(a.reshape(ts // 8, 8, a.shape[1]), axis=0)

        z_a = rest_ref[:, 0:512]
        u_b = rest_ref[:, 512:1024]
        v_b = rest_ref[:, 1024:1536]
        z_b = rest_ref[:, 1536:2048]
        g_a = rest_ref[:, 2048:2048 + d]
        g_b = rest_ref[:, 2048 + d:2048 + 2 * d]
        ov = o_ref[...]
        sa = _sigmoid(z_a)
        silu_a = z_a * sa
        y_a = ov * silu_a
        ug, dug_du = _gelu_and_grad(u_b)
        vg, dvg_dv = _gelu_and_grad(v_b)
        mu = gmean(vg)
        cen = vg - mu
        rstd_g = lax.rsqrt(gmean(cen * cen) + EPS)
        vhat = cen * rstd_g
        vn = vhat * lng_ref[...] + lnb_ref[...]
        vnb = vn.astype(BF16)

        t_idx = lax.broadcasted_iota(jnp.int32, (SGU_CHUNK, SGU_CHUNK), 0)
        s_idx = lax.broadcasted_iota(jnp.int32, (SGU_CHUNK, SGU_CHUNK), 1)
        causal = (s_idx // CHUNK) <= (t_idx // CHUNK)
        wm = [jnp.where(causal, wsp_ref[g], 0.0) for g in range(N_GROUPS)]
        wmb = [w.astype(BF16) for w in wm]
        wmtb = [w.T.astype(BF16) for w in wm]
        lane = lax.broadcasted_iota(jnp.int32, (SGU_CHUNK, LANES), 1)
        first = lane < GROUP_DIM
        bfull = bfull_ref[...]

        mixed_rows = []
        for n in range(nchunk):
            r0, r1 = n * SGU_CHUNK, (n + 1) * SGU_CHUNK
            pieces = []
            for p in range(N_GROUPS // 2):
                blk = vnb[r0:r1, p * LANES:(p + 1) * LANES]
                pieces.append(jnp.where(first, _dot(wmb[2 * p], blk), _dot(wmb[2 * p + 1], blk)))
            mixed_rows.append(jnp.concatenate(pieces, axis=1) + bfull)
        mixed = jnp.concatenate(mixed_rows, axis=0)
        sg = ug * mixed
        sb = _sigmoid(z_b)
        silu_b = z_b * sb
        y_b = sg * silu_b
        y_ab = y_a.astype(BF16)
        y_bb = y_b.astype(BF16)
        p_a = _dot(y_ab, wua_ref[...])
        p_b = _dot(y_bb, wub_ref[...])
        ga_s = _sigmoid(g_a)
        gb_s = _sigmoid(g_b)
        merged_b = (ga_s * p_a + gb_s * p_b).astype(BF16)
        x2 = x_ref[...] + _dot(merged_b, wout_ref[...])
        rstd = lax.rsqrt(jnp.mean(x2 * x2, axis=-1, keepdims=True) + EPS)
        xhat = x2 * rstd
        fg_v = fg_ref[...]
        diff = xhat * fg_v - t_ref[...]
        loss_ref[...] += 0.5 * jnp.sum(jnp.sum(diff * diff, axis=-1, keepdims=True) * (1.0 / d))

        dy = diff * (1.0 / d)
        dfg_ref[...] += colsum8(dy * xhat)
        dxh = dy * fg_v
        dx2 = rstd * (dxh - xhat * jnp.mean(dxh * xhat, axis=-1, keepdims=True))
        dx2_ref[...] = dx2
        dx2b = dx2.astype(BF16)
        dwout_ref[...] += _dot_tn(merged_b, dx2b)
        dmerged = _dot_nt(dx2b, wout_ref[...])
        dp_a = dmerged * ga_s
        dp_b = dmerged * gb_s
        dproj_ref[:, QKV_COLS + 2048:QKV_COLS + 2048 + d] = (dmerged * p_a * (ga_s * (1.0 - ga_s))).astype(BF16)
        dproj_ref[:, QKV_COLS + 2048 + d:QKV_COLS + 2048 + 2 * d] = (dmerged * p_b * (gb_s * (1.0 - gb_s))).astype(BF16)
        dp_ab = dp_a.astype(BF16)
        dp_bb = dp_b.astype(BF16)
        dwua_ref[...] += _dot_tn(y_ab, dp_ab)
        dwub_ref[...] += _dot_tn(y_bb, dp_bb)
        dy_a = _dot_nt(dp_ab, wua_ref[...])
        dy_b = _dot_nt(dp_bb, wub_ref[...])
        do_ref[...] = (dy_a * silu_a).astype(BF16)
        dproj_ref[:, QKV_COLS:QKV_COLS + 512] = (dy_a * ov * (sa * (1.0 + z_a * (1.0 - sa)))).astype(BF16)
        dsg = dy_b * silu_b
        dproj_ref[:, QKV_COLS + 1536:QKV_COLS + 2048] = (dy_b * sg * (sb * (1.0 + z_b * (1.0 - sb)))).astype(BF16)
        dproj_ref[:, QKV_COLS + 512:QKV_COLS + 1024] = (dsg * mixed * dug_du).astype(BF16)
        dmixed = dsg * ug
        dmb = dmixed.astype(BF16)
        zero = jnp.zeros((), BF16)
        dvn_rows = []
        db = jnp.zeros((SGU_CHUNK, D_SGU), F32)
        for n in range(nchunk):
            r0, r1 = n * SGU_CHUNK, (n + 1) * SGU_CHUNK
            db = db + dmixed[r0:r1, :]
            pieces = []
            for p in range(N_GROUPS // 2):
                cols = slice(p * LANES, (p + 1) * LANES)
                dm_blk = dmb[r0:r1, cols]
                vn_blk = vnb[r0:r1, cols]
                dwsp_ref[2 * p] += _dot_nt(jnp.where(first, dm_blk, zero), vn_blk)
                dwsp_ref[2 * p + 1] += _dot_nt(jnp.where(first, zero, dm_blk), vn_blk)
                pieces.append(jnp.where(first, _dot(wmtb[2 * p], dm_blk), _dot(wmtb[2 * p + 1], dm_blk)))
            dvn_rows.append(jnp.concatenate(pieces, axis=1))
        dbfull_ref[...] += db
        dvn = jnp.concatenate(dvn_rows, axis=0)
        dlng_ref[...] += colsum8(dvn * vhat)
        dlnb_ref[...] += colsum8(dvn)
        dvhat = dvn * lng_ref[...]
        dcen = rstd_g * (dvhat - gmean(dvhat) - vhat * gmean(dvhat * vhat))
        dproj_ref[:, QKV_COLS + 1024:QKV_COLS + 1536] = (dcen * dvg_dv).astype(BF16)

        @pl.when(step == nt - 1)
        def _():
            for g in range(N_GROUPS):
                dwsp_ref[g] = jnp.where(causal, dwsp_ref[g], 0.0)

    def tile(cols):
        return pl.BlockSpec((ts, cols), lambda i: (i, 0))

    def whole(shape):
        return pl.BlockSpec(shape, lambda i: (0,) * len(shape))

    out_shapes = [
        jax.ShapeDtypeStruct((8, LANES), F32),
        jax.ShapeDtypeStruct((s, d), F32),
        jax.ShapeDtypeStruct((s, D_SB), BF16),
        jax.ShapeDtypeStruct((s, QKV_COLS + n_rest), BF16),
        jax.ShapeDtypeStruct((D_SB, d), F32),
        jax.ShapeDtypeStruct((D_SGU, d), F32),
        jax.ShapeDtypeStruct((d, d), F32),
        jax.ShapeDtypeStruct((8, d), F32),
        jax.ShapeDtypeStruct((8, D_SGU), F32),
        jax.ShapeDtypeStruct((8, D_SGU), F32),
        jax.ShapeDtypeStruct((N_GROUPS, SGU_CHUNK, SGU_CHUNK), F32),
        jax.ShapeDtypeStruct((SGU_CHUNK, D_SGU), F32),
    ]
    out_specs = [whole((8, LANES)), tile(d), tile(D_SB), tile(QKV_COLS + n_rest), whole((D_SB, d)), whole((D_SGU, d)),
                 whole((d, d)), whole((8, d)), whole((8, D_SGU)), whole((8, D_SGU)),
                 whole((N_GROUPS, SGU_CHUNK, SGU_CHUNK)), whole((SGU_CHUNK, D_SGU))]
    in_specs = [tile(D_SB), tile(n_rest), tile(d), tile(d), whole((D_SB, d)), whole((D_SGU, d)), whole((d, d)),
                whole((1, d)), whole((1, D_SGU)), whole((1, D_SGU)), whole((N_GROUPS, SGU_CHUNK, SGU_CHUNK)),
                whole((SGU_CHUNK, D_SGU)), whole((D_SGU, D_SGU))]
    return pl.pallas_call(
        body, name=name, out_shape=out_shapes, grid=(nt,), in_specs=in_specs, out_specs=out_specs,
        compiler_params=_cparams(("arbitrary",)),
    )(o, rest, x, tgt, wua, wub, wout, fg, lng, lnb, wsp, bfull, gavg)


def _small_reduce(dfg8, dlng8, dlnb8, dbfull, name):
    d = dfg8.shape[1]

    def body(dfg_ref, dlng_ref, dlnb_ref, dbfull_ref, fg_out, lng_out, lnb_out, b_out):
        fg_out[...] = jnp.sum(dfg_ref[...], axis=0, keepdims=True)
        lng_out[...] = jnp.sum(dlng_ref[...], axis=0, keepdims=True)
        lnb_out[...] = jnp.sum(dlnb_ref[...], axis=0, keepdims=True)
        grp = lax.broadcasted_iota(jnp.int32, (D_SGU, LANES), 0) // GROUP_DIM
        col = lax.broadcasted_iota(jnp.int32, (D_SGU, LANES), 1)
        sel = (grp == col).astype(BF16)
        a = dbfull_ref[...]
        hi, lo = _split_hi_lo(a)
        lo2 = (a - hi.astype(F32) - lo.astype(F32)).astype(BF16)
        by_pos = _dot(hi, sel) + _dot(lo, sel) + _dot(lo2, sel)
        b_out[...] = by_pos.T[0:N_GROUPS, :]

    return pl.pallas_call(
        body, name=name,
        out_shape=[jax.ShapeDtypeStruct((1, d), F32), jax.ShapeDtypeStruct((1, D_SGU), F32),
                   jax.ShapeDtypeStruct((1, D_SGU), F32), jax.ShapeDtypeStruct((N_GROUPS, SGU_CHUNK), F32)],
        compiler_params=_cparams(),
    )(dfg8, dlng8, dlnb8, dbfull)


def _block_major_cols(w):
    r, n = w.shape
    return jnp.transpose(w.reshape(r, N_DEV, n // N_DEV), (1, 0, 2))


def _from_block_major_cols(w):
    nb, r, c = w.shape
    return jnp.transpose(w, (1, 0, 2)).reshape(r, nb * c)


def kernel(x, norm_g, w_in, sgu_ln_g, sgu_ln_b, w_spatial, b_spatial, w_up_a, w_up_b, w_out, final_norm_g, loss_target, m_norm_g, m_w_in, m_sgu_ln_g, m_sgu_ln_b, m_w_spatial, m_b_spatial, m_w_up_a, m_w_up_b, m_w_out, m_final_norm_g, v_norm_g, v_w_in, v_sgu_ln_g, v_sgu_ln_b, v_w_spatial, v_b_spatial, v_w_up_a, v_w_up_b, v_w_out, v_final_norm_g):
    s, d = x.shape[1], x.shape[2]
    xs = x[0]
    tgt = loss_target[0]
    cx, cy, cc = _coords()
    core = jnp.reshape(cc, (1,)).astype(jnp.int32)
    chip = jnp.reshape(2 * cx + cy, (1,)).astype(jnp.int32)

    (g_win,) = _allgather([w_in[0].astype(BF16)], "ag_w_in")
    w_full = _from_block_major_cols(g_win)

    h, ht = _rmsnorm_fwd(xs, norm_g, "norm")
    d_in = w_full.shape[1]
    qkv, _ = _inproj(h, w_full, 0, QKV_COLS, BF16, [], "inproj_qkv")
    rest, (g_wua, g_wub, g_wout) = _inproj(
        h, w_full, QKV_COLS, d_in - QKV_COLS, F32,
        [w_up_a[0].astype(BF16), w_up_b[0].astype(BF16), w_out[0].astype(BF16)], "inproj_rest")
    wua_full = _from_block_major_cols(g_wua)
    wub_full = _from_block_major_cols(g_wub)
    wout_full = g_wout.reshape(d, d)
    o, rs = _attn_fwd(qkv, "attn_fwd")

    lng = sgu_ln_g.reshape(1, D_SGU)
    lnb = sgu_ln_b.reshape(1, D_SGU)
    bfull = jnp.repeat(jnp.transpose(b_spatial[0]), GROUP_DIM, axis=1)
    grp = jnp.arange(D_SGU) // GROUP_DIM
    gavg = jnp.where(grp[:, None] == grp[None, :], 1.0 / GROUP_DIM, 0.0).astype(BF16)
    (loss_b, dx2, do, dproj, dwua, dwub, dwout, dfg8, dlng8, dlnb8, dwsp, dbfull) = _mid(
        o, rest, xs, tgt, wua_full, wub_full, wout_full, final_norm_g.reshape(1, d), lng, lnb, w_spatial[0], bfull,
        gavg, "mid")

    dproj = _attn_bwd(qkv, do, rs, dproj, "attn_bwd")
    dfg, dlng, dlnb, db = _small_reduce(dfg8, dlng8, dlnb8, dbfull, "small_reduce")
    wsp_rows = N_GROUPS * SGU_CHUNK
    dwin, small_parts = _dw_in(ht, dproj, [dfg, dlng, dlnb, db, dwsp.reshape(wsp_rows, SGU_CHUNK)], "dwin")

    def blocks42(a):
        return a.reshape((4, 2) + a.shape[1:])

    own = [blocks42(_block_major_cols(dwin)),
           blocks42(_block_major_cols(dwua.astype(BF16))),
           blocks42(_block_major_cols(dwub.astype(BF16))),
           blocks42(dwout.astype(BF16).reshape(N_DEV, d // N_DEV, d))]
    land1 = _push_sibling(own, "rs_sibling")
    cps = [_chip_partial_sum(a, l, core, "cpsum%d" % i) for i, (a, l) in enumerate(zip(own, land1))]
    grad_x, dng8, land2 = _dh_and_grad_x(dproj, w_full, xs, norm_g, dx2, cps, "dh")

    ng_parts, loss_parts = _allgather([dng8, loss_b], "ag_tail")

    def small_layouts(ng, fg, lg, lb, bs, ws):
        return [ng.reshape(1, d), fg.reshape(1, d), lg.reshape(1, D_SGU), lb.reshape(1, D_SGU),
                bs.reshape(N_GROUPS, SGU_CHUNK), ws.reshape(wsp_rows, SGU_CHUNK)]

    sm, loss_sum = _adam_small(
        [ng_parts] + list(small_parts),
        small_layouts(norm_g, final_norm_g, sgu_ln_g, sgu_ln_b, b_spatial, w_spatial),
        small_layouts(m_norm_g, m_final_norm_g, m_sgu_ln_g, m_sgu_ln_b, m_b_spatial, m_w_spatial),
        small_layouts(v_norm_g, v_final_norm_g, v_sgu_ln_g, v_sgu_ln_b, v_b_spatial, v_w_spatial),
        loss_parts, "adam_small")
    small_shapes = [norm_g.shape, final_norm_g.shape, sgu_ln_g.shape, sgu_ln_b.shape, b_spatial.shape, w_spatial.shape]
    sm = [[a.reshape(shp) for a, shp in zip(kind, small_shapes)] for kind in sm]

    big = []
    for i, (w, m, v) in enumerate([(w_in, m_w_in, v_w_in), (w_up_a, m_w_up_a, v_w_up_a),
                                   (w_up_b, m_w_up_b, v_w_up_b), (w_out, m_w_out, v_w_out)]):
        res = _adam_shard(cps[i], land2[i], chip, w[0], m[0], v[0], "adam%d" % i)
        big.append([r[None] for r in res])

    loss = loss_sum[0, 0]

    def per_kind(kd):
        return [sm[kd][0], big[0][kd], sm[kd][2], sm[kd][3], sm[kd][5], sm[kd][4], big[1][kd], big[2][kd], big[3][kd],
                sm[kd][1]]

    return (loss, grad_x[None], *per_kind(0), *per_kind(1), *per_kind(2), *per_kind(3))
```

```python
import math

import jax
import jax.numpy as jnp
from jax import lax
from jax.experimental import pallas as pl
from jax.experimental.pallas import tpu as pltpu

F32 = jnp.float32
BF16 = jnp.bfloat16
MESH = pl.DeviceIdType.MESH

N_DEV = 8
N_HEADS = 8
HEAD_DIM = 64
D_SB = N_HEADS * HEAD_DIM
N_GROUPS = 8
GROUP_DIM = 64
D_SGU = N_GROUPS * GROUP_DIM
SGU_CHUNK = 128
CHUNK = 64
EPS = 1e-6
LANES = 128
N_PAIRS = N_HEADS // 2
QKV_COLS = 3 * D_SB
ATT_BLOCK = 256
CARRY_FLOOR = -110.0
R_UNREACHED = -1e30

ADAM_LR = 0.001
ADAM_B1 = 0.9
ADAM_B2 = 0.999
ADAM_EPS = 1e-08
ADAM_WD = 0.01
ADAM_STEP = 10

VMEM_LIMIT = 56 * 1024 * 1024


def _cparams(sem=None, vmem=VMEM_LIMIT):
    return pltpu.CompilerParams(dimension_semantics=sem, vmem_limit_bytes=vmem)


def _dot(a, b):
    return jnp.dot(a, b, preferred_element_type=F32)


def _dot_nt(a, b):
    return lax.dot_general(a, b, (((1,), (1,)), ((), ())), preferred_element_type=F32)


def _dot_tn(a, b):
    return lax.dot_general(a, b, (((0,), (0,)), ((), ())), preferred_element_type=F32)


def _split_hi_lo(a):
    hi = a.astype(BF16)
    lo = (a - hi.astype(F32)).astype(BF16)
    return hi, lo


def _sigmoid(x):
    return 1.0 / (1.0 + jnp.exp(-x))


_GELU_C = math.sqrt(2.0 / math.pi)


def _gelu_and_grad(x):
    x2 = x * x
    inner = _GELU_C * (x + 0.044715 * (x2 * x))
    t = jnp.tanh(inner)
    cdf = 0.5 * (1.0 + t)
    g = x * cdf
    dg = cdf + x * (0.5 * (1.0 - t * t)) * (_GELU_C * (1.0 + 3.0 * 0.044715 * x2))
    return g, dg


def _coords():
    return lax.axis_index("x"), lax.axis_index("y"), lax.axis_index("c")


def _dev_index(px, py, pc):
    return 4 * px + 2 * py + pc


def _allgather(blocks, name):
    n = len(blocks)

    def body(*refs):
        gather = _Gather(refs[:n], refs[n:2 * n], *refs[2 * n:])
        gather.issue()
        gather.finish()

    any_spec = pl.BlockSpec(memory_space=pl.ANY)
    return pl.pallas_call(
        body, name=name,
        out_shape=_gather_out_shapes(blocks),
        in_specs=[any_spec] * n, out_specs=[any_spec] * n,
        scratch_shapes=_gather_semaphores(n),
    )(*blocks)


def _gather_out_shapes(blocks):
    return [jax.ShapeDtypeStruct((N_DEV,) + b.shape, b.dtype) for b in blocks]


def _gather_semaphores(n):
    return [pltpu.SemaphoreType.DMA((n, 7)), pltpu.SemaphoreType.DMA((n, 7)), pltpu.SemaphoreType.DMA((n,))]


class _Gather:
    def __init__(self, ins, outs, send_sems, recv_sems, local_sems):
        self.ins, self.outs = ins, outs
        self.send_sems, self.recv_sems, self.local_sems = send_sems, recv_sems, local_sems
        self.n = len(ins)
        x, y, c = _coords()
        self.c = c
        self.me, self.sibling = (x, y, c), (x, y, 1 - c)
        self.chips = [(1 - x, y), (x, 1 - y), (1 - x, 1 - y)]

    def _copy(self, a, k, block, to, src=None):
        dst = self.outs[a].at[_dev_index(*block)]
        return pltpu.make_async_remote_copy(
            src_ref=dst if src is None else src, dst_ref=dst,
            send_sem=self.send_sems.at[a, k], recv_sem=self.recv_sems.at[a, k],
            device_id=to, device_id_type=MESH)

    def _mine(self):
        return [pltpu.make_async_copy(self.ins[a], self.outs[a].at[_dev_index(*self.me)], self.local_sems.at[a])
                for a in range(self.n)]

    def _first(self):
        first = []
        for a in range(self.n):
            first.append(self._copy(a, 0, self.me, self.sibling, src=self.ins[a]))
            first += [self._copy(a, 1 + j, self.me, (*chip, self.c), src=self.ins[a])
                      for j, chip in enumerate(self.chips)]
        return first

    def issue(self):
        for cp in self._mine() + self._first():
            cp.start()

    def finish(self):
        c = self.c
        passed = []
        for j, chip in enumerate(self.chips):
            for a in range(self.n):
                self._copy(a, 1 + j, (*chip, c), self.me).wait_recv()
                fwd = self._copy(a, 4 + j, (*chip, c), self.sibling)
                fwd.start()
                passed.append(fwd)
        for a in range(self.n):
            self._copy(a, 0, self.sibling, self.me).wait_recv()
            for j, chip in enumerate(self.chips):
                self._copy(a, 4 + j, (*chip, 1 - c), self.me).wait_recv()
        for cp in self._first() + passed:
            cp.wait_send()
        for cp in self._mine():
            cp.wait()


def _push_sibling(arrs, name):
    n = len(arrs)

    def body(*refs):
        ins, outs = refs[:n], refs[n:2 * n]
        send_sems, recv_sems = refs[2 * n:]
        x, y, c = _coords()
        sibling = (x, y, 1 - c)
        copies = []
        for a in range(n):
            for k in range(4):
                copies.append(pltpu.make_async_remote_copy(
                    src_ref=ins[a].at[k, 1 - c], dst_ref=outs[a].at[k],
                    send_sem=send_sems.at[a, k], recv_sem=recv_sems.at[a, k],
                    device_id=sibling, device_id_type=MESH))
        for cp in copies:
            cp.start()
        for cp in copies:
            cp.wait()

    any_spec = pl.BlockSpec(memory_space=pl.ANY)
    return pl.pallas_call(
        body, name=name,
        out_shape=[jax.ShapeDtypeStruct((4,) + a.shape[2:], a.dtype) for a in arrs],
        in_specs=[any_spec] * n, out_specs=[any_spec] * n,
        scratch_shapes=[pltpu.SemaphoreType.DMA((n, 4)), pltpu.SemaphoreType.DMA((n, 4))],
    )(*arrs)


def _chip_push_copies(ins, outs, send_sems, recv_sems):
    x, y, c = _coords()
    chips = [(1 - x, y), (x, 1 - y), (1 - x, 1 - y)]
    return [pltpu.make_async_remote_copy(
        src_ref=ins[a].at[2 * px + py], dst_ref=outs[a].at[r],
        send_sem=send_sems.at[a, r], recv_sem=recv_sems.at[a, r],
        device_id=(px, py, c), device_id_type=MESH)
        for a in range(len(ins)) for r, (px, py) in enumerate(chips)]


def _chip_partial_sum(own, land, core, name):
    _, _, rows, cols = own.shape
    tr = rows

    def body(core_ref, own_ref, land_ref, out_ref):
        del core_ref
        out_ref[...] = (own_ref[...].astype(F32) + land_ref[...].astype(F32)).astype(out_ref.dtype)

    return pl.pallas_call(
        body, name=name,
        out_shape=jax.ShapeDtypeStruct((4, rows, cols), own.dtype),
        grid_spec=pltpu.PrefetchScalarGridSpec(
            num_scalar_prefetch=1, grid=(4, rows // tr),
            in_specs=[pl.BlockSpec((None, None, tr, cols), lambda k, r, core: (k, core[0], r, 0)),
                      pl.BlockSpec((None, tr, cols), lambda k, r, core: (k, r, 0))],
            out_specs=pl.BlockSpec((None, tr, cols), lambda k, r, core: (k, r, 0))),
        compiler_params=_cparams(("parallel", "parallel")),
    )(core, own, land)


def _adamw_math(w, g, m, v):
    m = ADAM_B1 * m + (1.0 - ADAM_B1) * g
    v = ADAM_B2 * v + (1.0 - ADAM_B2) * (g * g)
    m_hat = m / (1.0 - ADAM_B1 ** ADAM_STEP)
    v_hat = v / (1.0 - ADAM_B2 ** ADAM_STEP)
    delta = -ADAM_LR * (m_hat / (jnp.sqrt(v_hat) + ADAM_EPS) + ADAM_WD * w)
    return delta, m, v


def _adam_shard(cp, land, chip, w, m, v, name):
    rows, cols = w.shape
    tr = min(rows, 256)

    def body(chip_ref, cp_ref, land_ref, w_ref, m_ref, v_ref, g_out, d_out, m_out, v_out):
        del chip_ref
        g = cp_ref[...].astype(F32)
        for r in range(3):
            g = g + land_ref[r].astype(F32)
        delta, m_new, v_new = _adamw_math(w_ref[...], g, m_ref[...], v_ref[...])
        g_out[...] = g
        d_out[...] = delta
        m_out[...] = m_new
        v_out[...] = v_new

    tile = pl.BlockSpec((tr, cols), lambda r, chip: (r, 0))
    out = jax.ShapeDtypeStruct((rows, cols), F32)
    return pl.pallas_call(
        body, name=name, out_shape=[out] * 4,
        grid_spec=pltpu.PrefetchScalarGridSpec(
            num_scalar_prefetch=1, grid=(rows // tr,),
            in_specs=[pl.BlockSpec((None, tr, cols), lambda r, chip: (chip[0], r, 0)),
                      pl.BlockSpec((3, tr, cols), lambda r, chip: (0, r, 0)),
                      tile, tile, tile],
            out_specs=[tile] * 4),
        compiler_params=_cparams(("parallel",)),
    )(chip, cp, land, w, m, v)


def _adam_small(parts, ws, ms, vs, loss_parts, name):
    n = len(ws)

    def body(*refs):
        p_refs, w_refs, m_refs, v_refs = refs[:n], refs[n:2 * n], refs[2 * n:3 * n], refs[3 * n:4 * n]
        loss_ref, outs, loss_out = refs[4 * n], refs[4 * n + 1:-1], refs[-1]
        total = loss_ref[0]
        for dev in range(1, N_DEV):
            total = total + loss_ref[dev]
        loss_out[...] = total
        for i in range(n):
            g = p_refs[i][0]
            for dev in range(1, N_DEV):
                g = g + p_refs[i][dev]
            if g.shape[0] != w_refs[i].shape[0]:
                g = jnp.sum(g, axis=0, keepdims=True)
            delta, m_new, v_new = _adamw_math(w_refs[i][...], g, m_refs[i][...], v_refs[i][...])
            outs[i][...] = g
            outs[n + i][...] = delta
            outs[2 * n + i][...] = m_new
            outs[3 * n + i][...] = v_new

    out_shapes = [jax.ShapeDtypeStruct(w.shape, F32) for w in ws] * 4
    out_shapes.append(jax.ShapeDtypeStruct(loss_parts.shape[1:], F32))
    res = pl.pallas_call(body, name=name, out_shape=out_shapes, compiler_params=_cparams())(
        *parts, *ws, *ms, *vs, loss_parts)
    return [res[k * n:(k + 1) * n] for k in range(4)], res[-1]


def _rmsnorm_fwd(x, g, riders, name):
    s, d = x.shape
    ts = 512
    nt = s // ts
    nr = len(riders)

    def body(x_ref, g_ref, *rest):
        rider_in, (h_ref, ht_ref) = rest[:nr], rest[nr:nr + 2]
        rider_out, sems = rest[nr + 2:2 * nr + 2], rest[2 * nr + 2:]
        i = pl.program_id(0)
        gather = _Gather(rider_in, rider_out, *sems)

        @pl.when(i == 0)
        def _():
            gather.issue()

        xv = x_ref[...]
        rstd = lax.rsqrt(jnp.mean(xv * xv, axis=-1, keepdims=True) + EPS)
        h = xv * rstd * g_ref[...]
        h_ref[...] = h.astype(BF16)
        ht_ref[...] = h.T.astype(BF16)

        @pl.when(i == nt - 1)
        def _():
            gather.finish()

    any_spec = pl.BlockSpec(memory_space=pl.ANY)
    res = pl.pallas_call(
        body, name=name,
        out_shape=[jax.ShapeDtypeStruct((s, d), BF16), jax.ShapeDtypeStruct((d, s), BF16)]
        + _gather_out_shapes(riders),
        grid=(nt,),
        in_specs=[pl.BlockSpec((ts, d), lambda i: (i, 0)), pl.BlockSpec((1, d), lambda i: (0, 0))] + [any_spec] * nr,
        out_specs=[pl.BlockSpec((ts, d), lambda i: (i, 0)), pl.BlockSpec((d, ts), lambda i: (0, i))]
        + [any_spec] * nr,
        scratch_shapes=_gather_semaphores(nr),
        compiler_params=_cparams(("arbitrary",)),
    )(x, g, *riders)
    return res[0], res[1], res[2:]


def _inproj(h, w, col0, ncols, out_dtype, riders, name):
    s, d = h.shape
    tn = 256
    j0 = col0 // tn
    nj = ncols // tn
    nr = len(riders)

    def body(h_ref, w_ref, *rest):
        rider_in, o_ref, rider_out, sems = rest[:nr], rest[nr], rest[nr + 1:2 * nr + 1], rest[2 * nr + 1:]
        j = pl.program_id(0)
        if nr:
            gather = _Gather(rider_in, rider_out, *sems)

            @pl.when(j == 0)
            def _():
                gather.issue()

        o_ref[...] = _dot(h_ref[...], w_ref[...]).astype(o_ref.dtype)

        if nr:
            @pl.when(j == nj - 1)
            def _():
                gather.finish()

    any_spec = pl.BlockSpec(memory_space=pl.ANY)
    res = pl.pallas_call(
        body, name=name, out_shape=[jax.ShapeDtypeStruct((s, ncols), out_dtype)] + _gather_out_shapes(riders),
        grid=(nj,),
        in_specs=[pl.BlockSpec((s, d), lambda j: (0, 0)), pl.BlockSpec((d, tn), lambda j: (0, j + j0))]
        + [any_spec] * nr,
        out_specs=[pl.BlockSpec((s, tn), lambda j: (0, j))] + [any_spec] * nr,
        scratch_shapes=_gather_semaphores(nr) if nr else [],
        compiler_params=_cparams(("arbitrary",) if nr else ("parallel",)),
    )(h, w, *riders)
    return res[0], res[1:]


def _dw_in(ht, dproj, smalls, cps, name):
    d, s = ht.shape
    n = dproj.shape[1]
    tn = 512
    nj = n // tn
    ns, nc = len(smalls), len(cps)

    def body(a_ref, b_ref, *rest):
        small_in, cp_in = rest[:ns], rest[ns:ns + nc]
        o_ref = rest[ns + nc]
        small_out, land_out = rest[ns + nc + 1:2 * ns + nc + 1], rest[2 * ns + nc + 1:2 * (ns + nc) + 1]
        sems = rest[2 * (ns + nc) + 1:]
        j = pl.program_id(0)
        gather = _Gather(small_in, small_out, *sems[:3])

        @pl.when(j == 0)
        def _():
            gather.issue()
            for cp in _chip_push_copies(cp_in, land_out, *sems[3:]):
                cp.start()

        o_ref[...] = _dot(a_ref[...], b_ref[...]).astype(o_ref.dtype)

        @pl.when(j == nj - 1)
        def _():
            gather.finish()
            for cp in _chip_push_copies(cp_in, land_out, *sems[3:]):
                cp.wait()

    any_spec = pl.BlockSpec(memory_space=pl.ANY)
    res = pl.pallas_call(
        body, name=name,
        out_shape=[jax.ShapeDtypeStruct((d, n), BF16)] + _gather_out_shapes(smalls)
        + [jax.ShapeDtypeStruct((3,) + a.shape[1:], a.dtype) for a in cps],
        grid=(nj,),
        in_specs=[pl.BlockSpec((d, s), lambda j: (0, 0)), pl.BlockSpec((s, tn), lambda j: (0, j))]
        + [any_spec] * (ns + nc),
        out_specs=[pl.BlockSpec((d, tn), lambda j: (0, j))] + [any_spec] * (ns + nc),
        scratch_shapes=_gather_semaphores(ns) + [pltpu.SemaphoreType.DMA((nc, 3)), pltpu.SemaphoreType.DMA((nc, 3))],
        compiler_params=_cparams(("arbitrary",)),
    )(ht, dproj, *smalls, *cps)
    return res[0], res[1:1 + ns], res[1 + ns:]


def _dh_and_grad_x(dproj, w, x, g, dx2, cps, name):
    s, n = dproj.shape
    d = w.shape[0]
    tm, tk = min(s, 1024), 512
    nk = n // tk
    nm = s // tm
    nc = len(cps)

    def body(a_ref, w_ref, x_ref, g_ref, dx2_ref, *rest):
        cp_refs, gx_ref, dg_ref = rest[:nc], rest[nc], rest[nc + 1]
        land_refs = rest[nc + 2:2 * nc + 2]
        acc_ref, send_sems, recv_sems = rest[2 * nc + 2:]
        i, k = pl.program_id(0), pl.program_id(1)

        @pl.when((i == 0) & (k == 0))
        def _():
            for cp in _chip_push_copies(cp_refs, land_refs, send_sems, recv_sems):
                cp.start()
            dg_ref[...] = jnp.zeros_like(dg_ref)

        @pl.when(k == 0)
        def _():
            acc_ref[...] = jnp.zeros_like(acc_ref)

        acc_ref[...] += _dot_nt(a_ref[...], w_ref[...])

        @pl.when(k == nk - 1)
        def _():
            dh = acc_ref[...]
            xv = x_ref[...]
            rstd = lax.rsqrt(jnp.mean(xv * xv, axis=-1, keepdims=True) + EPS)
            xhat = xv * rstd
            dg_ref[...] += jnp.sum((dh * xhat).reshape(tm // 8, 8, d), axis=0)
            dxh = dh * g_ref[...]
            gx_ref[...] = dx2_ref[...] + rstd * (dxh - xhat * jnp.mean(dxh * xhat, axis=-1, keepdims=True))

        @pl.when((i == nm - 1) & (k == nk - 1))
        def _():
            for cp in _chip_push_copies(cp_refs, land_refs, send_sems, recv_sems):
                cp.wait()

    any_spec = pl.BlockSpec(memory_space=pl.ANY)
    res = pl.pallas_call(
        body, name=name,
        out_shape=[jax.ShapeDtypeStruct((s, d), F32), jax.ShapeDtypeStruct((8, d), F32)]
        + [jax.ShapeDtypeStruct((3,) + a.shape[1:], a.dtype) for a in cps],
        grid=(nm, nk),
        in_specs=[pl.BlockSpec((tm, tk), lambda i, k: (i, k)), pl.BlockSpec((d, tk), lambda i, k: (0, k)),
                  pl.BlockSpec((tm, d), lambda i, k: (i, 0)), pl.BlockSpec((1, d), lambda i, k: (0, 0)),
                  pl.BlockSpec((tm, d), lambda i, k: (i, 0))] + [any_spec] * nc,
        out_specs=[pl.BlockSpec((tm, d), lambda i, k: (i, 0)), pl.BlockSpec((8, d), lambda i, k: (0, 0))]
        + [any_spec] * nc,
        scratch_shapes=[pltpu.VMEM((tm, d), F32), pltpu.SemaphoreType.DMA((nc, 3)), pltpu.SemaphoreType.DMA((nc, 3))],
        compiler_params=_cparams(("arbitrary", "arbitrary")),
    )(dproj, w, x, g, dx2, *cps)
    return res[0], res[1], res[2:]


def _log_sigmoids(z):
    l1p = jnp.log(1.0 + jnp.exp(-jnp.abs(z)))
    ls = jnp.minimum(z, 0.0) - l1p
    return ls, ls - z


def _strict_lower_ones(n):
    row = lax.broadcasted_iota(jnp.int32, (n, n), 0)
    col = lax.broadcasted_iota(jnp.int32, (n, n), 1)
    return row, col, (row > col).astype(BF16)


def _attn_fwd(qkv, name):
    s = qkv.shape[0]
    tb = ATT_BLOCK
    nq = s // tb

    def body(q_ref, k_ref, v_ref, o_ref, rs_ref, acc_ref, r_ref, rsv_ref):
        i = pl.program_id(1)
        lane = lax.broadcasted_iota(jnp.int32, (tb, LANES), 1)
        hmask = [lane < HEAD_DIM, lane >= HEAD_DIM]
        q2 = q_ref[...]
        qm = [jnp.where(m, q2, jnp.zeros((), BF16)) * jnp.asarray(HEAD_DIM ** -0.5, BF16) for m in hmask]
        row, col, tri = _strict_lower_ones(tb)
        tri2 = jnp.concatenate([tri, tri], axis=0)
        acc_ref[...] = jnp.zeros_like(acc_ref)
        r_ref[...] = jnp.zeros_like(r_ref)
        rsv_ref[...] = jnp.full_like(rsv_ref, R_UNREACHED)

        def block(j, diag):
            heads = range(2)
            rows = pl.ds(pl.multiple_of(j * tb, tb), tb)
            k2, v2 = k_ref[rows, :], v_ref[rows, :]
            z = [_dot_nt(qm[h], k2) for h in heads]
            ls, lk = [], []
            for h in heads:
                ls_h, lk_h = _log_sigmoids(z[h])
                if diag:
                    lk_h = jnp.where(col < row, lk_h, 0.0)
                ls.append(ls_h)
                lk.append(lk_h)
            r = [r_ref[h] for h in heads]
            suffix = [_dot(jnp.concatenate(_split_hi_lo(lk[h]), axis=1), tri2) for h in heads]
            w = []
            for h in heads:
                w_h = jnp.exp(ls[h] + suffix[h] + jnp.tile(r[h], (1, tb // LANES)))
                if diag:
                    w_h = jnp.where(col < row, w_h, 0.0)
                w.append(w_h.astype(BF16))
            pv = [_dot(w[h], v2) for h in heads]
            for h in heads:
                acc_ref[h] += pv[h]
                rsv_ref[h] = jnp.where(lane == j, r[h], rsv_ref[h])
                r_ref[h] = r[h] + jnp.sum(lk[h], axis=1, keepdims=True)

        block(i, True)

        def live(c):
            jj, rmax = c
            return (jj < i) & (rmax >= CARRY_FLOOR)

        def step(c):
            jj, _ = c
            block(i - 1 - jj, False)
            return jj + 1, jnp.max(r_ref[...])

        lax.while_loop(live, step, (jnp.int32(0), jnp.max(r_ref[...])))
        o_ref[...] = jnp.where(hmask[0], acc_ref[0], acc_ref[1])
        rs_ref[...] = rsv_ref[...]

    return pl.pallas_call(
        body, name=name,
        out_shape=[jax.ShapeDtypeStruct((s, D_SB), F32), jax.ShapeDtypeStruct((N_HEADS, s, LANES), F32)],
        grid=(N_PAIRS, nq),
        in_specs=[pl.BlockSpec((tb, LANES), lambda p, i: (i, p)),
                  pl.BlockSpec((s, LANES), lambda p, i: (0, N_PAIRS + p)),
                  pl.BlockSpec((s, LANES), lambda p, i: (0, 2 * N_PAIRS + p))],
        out_specs=[pl.BlockSpec((tb, LANES), lambda p, i: (i, p)),
                   pl.BlockSpec((2, tb, LANES), lambda p, i: (p, i, 0))],
        scratch_shapes=[pltpu.VMEM((2, tb, LANES), F32), pltpu.VMEM((2, tb, LANES), F32),
                        pltpu.VMEM((2, tb, LANES), F32)],
        compiler_params=_cparams(("parallel", "arbitrary")),
    )(qkv, qkv, qkv)


def _attn_bwd(qkv, do, rs, dproj, name):
    s = qkv.shape[0]
    tb = ATT_BLOCK
    nq = s // tb
    scale = HEAD_DIM ** -0.5

    def body(q_ref, k_ref, v_ref, do_ref, rs_ref, dproj_hbm, out_hbm, dq_acc, dk_acc, dv_acc, dqi_ref, pc_ref,
             stage_ref, out_sems):
        del dproj_hbm
        pair = pl.program_id(0)
        lane = lax.broadcasted_iota(jnp.int32, (tb, LANES), 1)
        hmask = [lane < HEAD_DIM, lane >= HEAD_DIM]
        row, col, tri = _strict_lower_ones(tb)
        triu = (row < col).astype(BF16)
        tri2 = jnp.concatenate([tri, tri], axis=0)
        zero = jnp.zeros((), BF16)
        dq_acc[...] = jnp.zeros_like(dq_acc)
        dk_acc[...] = jnp.zeros_like(dk_acc)
        dv_acc[...] = jnp.zeros_like(dv_acc)

        def qblock(i, carry):
            qrows = pl.ds(pl.multiple_of(i * tb, tb), tb)
            q2 = q_ref[qrows, :]
            do2 = do_ref[qrows, :]
            qm = [jnp.where(m, q2, zero) * jnp.asarray(scale, BF16) for m in hmask]
            dom = [jnp.where(m, do2, zero) for m in hmask]
            rs_i = [rs_ref[h, qrows, :] for h in range(2)]
            dqi_ref[...] = jnp.zeros_like(dqi_ref)
            pc_ref[...] = jnp.zeros_like(pc_ref)
            reached = jnp.maximum(jnp.max(rs_i[0], axis=0, keepdims=True), jnp.max(rs_i[1], axis=0, keepdims=True))
            key_block = lax.broadcasted_iota(jnp.int32, (1, LANES), 1)
            n_reached = jnp.sum(((reached >= CARRY_FLOOR) & (key_block <= i)).astype(jnp.int32))

            def block(j, diag):
                krows = pl.ds(pl.multiple_of(j * tb, tb), tb)
                k2 = k_ref[krows, :]
                v2 = v_ref[krows, :]
                heads = range(2)
                z = [_dot_nt(qm[h], k2) for h in heads]
                dw = [_dot_nt(dom[h], v2) for h in heads]
                ls, lk = [], []
                for h in heads:
                    ls_h, lk_h = _log_sigmoids(z[h])
                    if diag:
                        lk_h = jnp.where(col < row, lk_h, 0.0)
                    ls.append(ls_h)
                    lk.append(lk_h)
                r = [jnp.sum(jnp.where(lane == j, rs_i[h], 0.0), axis=1, keepdims=True) for h in heads]
                suffix = [_dot(jnp.concatenate(_split_hi_lo(lk[h]), axis=1), tri2) for h in heads]
                w, g = [], []
                for h in heads:
                    w_h = jnp.exp(ls[h] + suffix[h] + r[h])
                    if diag:
                        w_h = jnp.where(col < row, w_h, 0.0)
                    w.append(w_h.astype(BF16))
                    g.append(w_h * dw[h])
                pc = [pc_ref[h] for h in heads]
                prefix = [_dot(g[h].astype(BF16), triu) for h in heads]
                dzb = []
                for h in heads:
                    sig = jnp.exp(ls[h])
                    dz = g[h] - sig * (g[h] + prefix[h] + jnp.tile(pc[h], (1, tb // LANES)))
                    if diag:
                        dz = jnp.where(col < row, dz, 0.0)
                    dzb.append(dz.astype(BF16))
                    pc_ref[h] = pc[h] + jnp.sum(g[h], axis=1, keepdims=True)
                dq = [_dot(dzb[h], jnp.where(hmask[h], k2, zero)) for h in heads]
                dk = [_dot_tn(dzb[h], qm[h]) for h in heads]
                dv = [_dot_tn(w[h], dom[h]) for h in heads]
                dqi_ref[...] += dq[0] + dq[1]
                dk_acc[krows, :] += dk[0] + dk[1]
                dv_acc[krows, :] += dv[0] + dv[1]

            def step(j, c):
                block(j, False)
                return c

            lax.fori_loop(i + 1 - n_reached, i, step, 0)
            block(i, True)
            dq_acc[qrows, :] += dqi_ref[...] * scale
            return carry

        lax.fori_loop(0, nq, qblock, 0)
        copies = []
        for t, acc in enumerate((dq_acc, dk_acc, dv_acc)):
            stage_ref[t] = acc[...].astype(BF16)
            col0 = pl.multiple_of(t * D_SB + pair * LANES, LANES)
            copies.append(pltpu.make_async_copy(stage_ref.at[t], out_hbm.at[:, pl.ds(col0, LANES)], out_sems.at[t]))
        for cp in copies:
            cp.start()
        for cp in copies:
            cp.wait()

    return pl.pallas_call(
        body, name=name,
        out_shape=jax.ShapeDtypeStruct(dproj.shape, BF16),
        grid=(N_PAIRS,),
        in_specs=[pl.BlockSpec((s, LANES), lambda p: (0, p)),
                  pl.BlockSpec((s, LANES), lambda p: (0, N_PAIRS + p)),
                  pl.BlockSpec((s, LANES), lambda p: (0, 2 * N_PAIRS + p)),
                  pl.BlockSpec((s, LANES), lambda p: (0, p)),
                  pl.BlockSpec((2, s, LANES), lambda p: (p, 0, 0)),
                  pl.BlockSpec(memory_space=pl.ANY)],
        out_specs=pl.BlockSpec(memory_space=pl.ANY),
        scratch_shapes=[pltpu.VMEM((s, LANES), F32), pltpu.VMEM((s, LANES), F32), pltpu.VMEM((s, LANES), F32),
                        pltpu.VMEM((tb, LANES), F32), pltpu.VMEM((2, tb, LANES), F32),
                        pltpu.VMEM((3, s, LANES), BF16), pltpu.SemaphoreType.DMA((3,))],
        input_output_aliases={5: 0},
        compiler_params=_cparams(("arbitrary",)),
    )(qkv, qkv, qkv, do, rs, dproj)


def _mid(o, rest, x, tgt, wua, wub, wout, fg, lng, lnb, wsp, bfull, gavg, name):
    s, d = x.shape
    ts = 256
    nt = s // ts
    nchunk = ts // SGU_CHUNK
    n_rest = rest.shape[1]

    def body(o_ref, rest_ref, x_ref, t_ref, wua_ref, wub_ref, wout_ref, fg_ref, lng_ref, lnb_ref, wsp_ref, bfull_ref,
             gavg_ref, loss_ref, dx2_ref, do_ref, dproj_ref, dwua_ref, dwub_ref, dwout_ref, dfg_ref, dlng_ref,
             dlnb_ref, dwsp_ref, dbfull_ref):
        step = pl.program_id(0)

        @pl.when(step == 0)
        def _():
            for ref in (loss_ref, dwua_ref, dwub_ref, dwout_ref, dfg_ref, dlng_ref, dlnb_ref, dwsp_ref, dbfull_ref):
                ref[...] = jnp.zeros_like(ref)

        gavg = gavg_ref[...]

        def gmean(a):
            return _dot(a.astype(BF16), gavg)

        def colsum8(a):
            return jnp.sum(a.reshape(ts // 8, 8, a.shape[1]), axis=0)

        z_a = rest_ref[:, 0:512]
        u_b = rest_ref[:, 512:1024]
        v_b = rest_ref[:, 1024:1536]
        z_b = rest_ref[:, 1536:2048]
        g_a = rest_ref[:, 2048:2048 + d]
        g_b = rest_ref[:, 2048 + d:2048 + 2 * d]
        ov = o_ref[...]
        sa = _sigmoid(z_a)
        silu_a = z_a * sa
        y_a = ov * silu_a
        ug, dug_du = _gelu_and_grad(u_b)
        vg, dvg_dv = _gelu_and_grad(v_b)
        mu = gmean(vg)
        cen = vg - mu
        rstd_g = lax.rsqrt(gmean(cen * cen) + EPS)
        vhat = cen * rstd_g
        vn = vhat * lng_ref[...] + lnb_ref[...]
        vnb = vn.astype(BF16)

        t_idx = lax.broadcasted_iota(jnp.int32, (SGU_CHUNK, SGU_CHUNK), 0)
        s_idx = lax.broadcasted_iota(jnp.int32, (SGU_CHUNK, SGU_CHUNK), 1)
        causal = (s_idx // CHUNK) <= (t_idx // CHUNK)
        wm = [jnp.where(causal, wsp_ref[g], 0.0) for g in range(N_GROUPS)]
        wmb = [w.astype(BF16) for w in wm]
        wmtb = [w.T.astype(BF16) for w in wm]
        lane = lax.broadcasted_iota(jnp.int32, (SGU_CHUNK, LANES), 1)
        first = lane < GROUP_DIM
        bfull = bfull_ref[...]

        mixed_rows = []
        for n in range(nchunk):
            r0, r1 = n * SGU_CHUNK, (n + 1) * SGU_CHUNK
            pieces = []
            for p in range(N_GROUPS // 2):
                blk = vnb[r0:r1, p * LANES:(p + 1) * LANES]
                pieces.append(jnp.where(first, _dot(wmb[2 * p], blk), _dot(wmb[2 * p + 1], blk)))
            mixed_rows.append(jnp.concatenate(pieces, axis=1) + bfull)
        mixed = jnp.concatenate(mixed_rows, axis=0)
        sg = ug * mixed
        sb = _sigmoid(z_b)
        silu_b = z_b * sb
        y_b = sg * silu_b
        y_ab = y_a.astype(BF16)
        y_bb = y_b.astype(BF16)
        p_a = _dot(y_ab, wua_ref[...])
        p_b = _dot(y_bb, wub_ref[...])
        ga_s = _sigmoid(g_a)
        gb_s = _sigmoid(g_b)
        merged_b = (ga_s * p_a + gb_s * p_b).astype(BF16)
        x2 = x_ref[...] + _dot(merged_b, wout_ref[...])
        rstd = lax.rsqrt(jnp.mean(x2 * x2, axis=-1, keepdims=True) + EPS)
        xhat = x2 * rstd
        fg_v = fg_ref[...]
        diff = xhat * fg_v - t_ref[...]
        loss_ref[...] += 0.5 * jnp.sum(jnp.sum(diff * diff, axis=-1, keepdims=True) * (1.0 / d))

        dy = diff * (1.0 / d)
        dfg_ref[...] += colsum8(dy * xhat)
        dxh = dy * fg_v
        dx2 = rstd * (dxh - xhat * jnp.mean(dxh * xhat, axis=-1, keepdims=True))
        dx2_ref[...] = dx2
        dx2b = dx2.astype(BF16)
        dwout_ref[...] += _dot_tn(merged_b, dx2b)
        dmerged = _dot_nt(dx2b, wout_ref[...])
        dp_a = dmerged * ga_s
        dp_b = dmerged * gb_s
        dproj_ref[:, QKV_COLS + 2048:QKV_COLS + 2048 + d] = (dmerged * p_a * (ga_s * (1.0 - ga_s))).astype(BF16)
        dproj_ref[:, QKV_COLS + 2048 + d:QKV_COLS + 2048 + 2 * d] = (dmerged * p_b * (gb_s * (1.0 - gb_s))).astype(BF16)
        dp_ab = dp_a.astype(BF16)
        dp_bb = dp_b.astype(BF16)
        dwua_ref[...] += _dot_tn(y_ab, dp_ab)
        dwub_ref[...] += _dot_tn(y_bb, dp_bb)
        dy_a = _dot_nt(dp_ab, wua_ref[...])
        dy_b = _dot_nt(dp_bb, wub_ref[...])
        do_ref[...] = (dy_a * silu_a).astype(BF16)
        dproj_ref[:, QKV_COLS:QKV_COLS + 512] = (dy_a * ov * (sa * (1.0 + z_a * (1.0 - sa)))).astype(BF16)
        dsg = dy_b * silu_b
        dproj_ref[:, QKV_COLS + 1536:QKV_COLS + 2048] = (dy_b * sg * (sb * (1.0 + z_b * (1.0 - sb)))).astype(BF16)
        dproj_ref[:, QKV_COLS + 512:QKV_COLS + 1024] = (dsg * mixed * dug_du).astype(BF16)
        dmixed = dsg * ug
        dmb = dmixed.astype(BF16)
        zero = jnp.zeros((), BF16)
        dvn_rows = []
        db = jnp.zeros((SGU_CHUNK, D_SGU), F32)
        for n in range(nchunk):
            r0, r1 = n * SGU_CHUNK, (n + 1) * SGU_CHUNK
            db = db + dmixed[r0:r1, :]
            pieces = []
            for p in range(N_GROUPS // 2):
                cols = slice(p * LANES, (p + 1) * LANES)
                dm_blk = dmb[r0:r1, cols]
                vn_blk = vnb[r0:r1, cols]
                dwsp_ref[2 * p] += _dot_nt(jnp.where(first, dm_blk, zero), vn_blk)
                dwsp_ref[2 * p + 1] += _dot_nt(jnp.where(first, zero, dm_blk), vn_blk)
                pieces.append(jnp.where(first, _dot(wmtb[2 * p], dm_blk), _dot(wmtb[2 * p + 1], dm_blk)))
            dvn_rows.append(jnp.concatenate(pieces, axis=1))
        dbfull_ref[...] += db
        dvn = jnp.concatenate(dvn_rows, axis=0)
        dlng_ref[...] += colsum8(dvn * vhat)
        dlnb_ref[...] += colsum8(dvn)
        dvhat = dvn * lng_ref[...]
        dcen = rstd_g * (dvhat - gmean(dvhat) - vhat * gmean(dvhat * vhat))
        dproj_ref[:, QKV_COLS + 1024:QKV_COLS + 1536] = (dcen * dvg_dv).astype(BF16)

        @pl.when(step == nt - 1)
        def _():
            for g in range(N_GROUPS):
                dwsp_ref[g] = jnp.where(causal, dwsp_ref[g], 0.0)

    def tile(cols):
        return pl.BlockSpec((ts, cols), lambda i: (i, 0))

    def whole(shape):
        return pl.BlockSpec(shape, lambda i: (0,) * len(shape))

    out_shapes = [
        jax.ShapeDtypeStruct((8, LANES), F32),
        jax.ShapeDtypeStruct((s, d), F32),
        jax.ShapeDtypeStruct((s, D_SB), BF16),
        jax.ShapeDtypeStruct((s, QKV_COLS + n_rest), BF16),
        jax.ShapeDtypeStruct((D_SB, d), F32),
        jax.ShapeDtypeStruct((D_SGU, d), F32),
        jax.ShapeDtypeStruct((d, d), F32),
        jax.ShapeDtypeStruct((8, d), F32),
        jax.ShapeDtypeStruct((8, D_SGU), F32),
        jax.ShapeDtypeStruct((8, D_SGU), F32),
        jax.ShapeDtypeStruct((N_GROUPS, SGU_CHUNK, SGU_CHUNK), F32),
        jax.ShapeDtypeStruct((SGU_CHUNK, D_SGU), F32),
    ]
    out_specs = [whole((8, LANES)), tile(d), tile(D_SB), tile(QKV_COLS + n_rest), whole((D_SB, d)), whole((D_SGU, d)),
                 whole((d, d)), whole((8, d)), whole((8, D_SGU)), whole((8, D_SGU)),
                 whole((N_GROUPS, SGU_CHUNK, SGU_CHUNK)), whole((SGU_CHUNK, D_SGU))]
    in_specs = [tile(D_SB), tile(n_rest), tile(d), tile(d), whole((D_SB, d)), whole((D_SGU, d)), whole((d, d)),
                whole((1, d)), whole((1, D_SGU)), whole((1, D_SGU)), whole((N_GROUPS, SGU_CHUNK, SGU_CHUNK)),
                whole((SGU_CHUNK, D_SGU)), whole((D_SGU, D_SGU))]
    return pl.pallas_call(
        body, name=name, out_shape=out_shapes, grid=(nt,), in_specs=in_specs, out_specs=out_specs,
        compiler_params=_cparams(("arbitrary",)),
    )(o, rest, x, tgt, wua, wub, wout, fg, lng, lnb, wsp, bfull, gavg)


def _small_reduce(dfg8, dlng8, dlnb8, dbfull, name):
    d = dfg8.shape[1]

    def body(dfg_ref, dlng_ref, dlnb_ref, dbfull_ref, fg_out, lng_out, lnb_out, b_out):
        fg_out[...] = jnp.sum(dfg_ref[...], axis=0, keepdims=True)
        lng_out[...] = jnp.sum(dlng_ref[...], axis=0, keepdims=True)
        lnb_out[...] = jnp.sum(dlnb_ref[...], axis=0, keepdims=True)
        grp = lax.broadcasted_iota(jnp.int32, (D_SGU, LANES), 0) // GROUP_DIM
        col = lax.broadcasted_iota(jnp.int32, (D_SGU, LANES), 1)
        sel = (grp == col).astype(BF16)
        a = dbfull_ref[...]
        hi, lo = _split_hi_lo(a)
        lo2 = (a - hi.astype(F32) - lo.astype(F32)).astype(BF16)
        by_pos = _dot(hi, sel) + _dot(lo, sel) + _dot(lo2, sel)
        b_out[...] = by_pos.T[0:N_GROUPS, :]

    return pl.pallas_call(
        body, name=name,
        out_shape=[jax.ShapeDtypeStruct((1, d), F32), jax.ShapeDtypeStruct((1, D_SGU), F32),
                   jax.ShapeDtypeStruct((1, D_SGU), F32), jax.ShapeDtypeStruct((N_GROUPS, SGU_CHUNK), F32)],
        compiler_params=_cparams(),
    )(dfg8, dlng8, dlnb8, dbfull)


def _block_major_cols(w):
    r, n = w.shape
    return jnp.transpose(w.reshape(r, N_DEV, n // N_DEV), (1, 0, 2))


def _from_block_major_cols(w):
    nb, r, c = w.shape
    return jnp.transpose(w, (1, 0, 2)).reshape(r, nb * c)


def kernel(x, norm_g, w_in, sgu_ln_g, sgu_ln_b, w_spatial, b_spatial, w_up_a, w_up_b, w_out, final_norm_g, loss_target, m_norm_g, m_w_in, m_sgu_ln_g, m_sgu_ln_b, m_w_spatial, m_b_spatial, m_w_up_a, m_w_up_b, m_w_out, m_final_norm_g, v_norm_g, v_w_in, v_sgu_ln_g, v_sgu_ln_b, v_w_spatial, v_b_spatial, v_w_up_a, v_w_up_b, v_w_out, v_final_norm_g):
    s, d = x.shape[1], x.shape[2]
    xs = x[0]
    tgt = loss_target[0]
    cx, cy, cc = _coords()
    core = jnp.reshape(cc, (1,)).astype(jnp.int32)
    chip = jnp.reshape(2 * cx + cy, (1,)).astype(jnp.int32)

    h, ht, (g_win,) = _rmsnorm_fwd(xs, norm_g, [w_in[0].astype(BF16)], "norm")
    w_full = _from_block_major_cols(g_win)
    d_in = w_full.shape[1]
    qkv, _ = _inproj(h, w_full, 0, QKV_COLS, BF16, [], "inproj_qkv")
    rest, (g_wua, g_wub, g_wout) = _inproj(
        h, w_full, QKV_COLS, d_in - QKV_COLS, F32,
        [w_up_a[0].astype(BF16), w_up_b[0].astype(BF16), w_out[0].astype(BF16)], "inproj_rest")
    wua_full = _from_block_major_cols(g_wua)
    wub_full = _from_block_major_cols(g_wub)
    wout_full = g_wout.reshape(d, d)
    o, rs = _attn_fwd(qkv, "attn_fwd")

    lng = sgu_ln_g.reshape(1, D_SGU)
    lnb = sgu_ln_b.reshape(1, D_SGU)
    bfull = jnp.repeat(jnp.transpose(b_spatial[0]), GROUP_DIM, axis=1)
    grp = jnp.arange(D_SGU) // GROUP_DIM
    gavg = jnp.where(grp[:, None] == grp[None, :], 1.0 / GROUP_DIM, 0.0).astype(BF16)
    (loss_b, dx2, do, dproj, dwua, dwub, dwout, dfg8, dlng8, dlnb8, dwsp, dbfull) = _mid(
        o, rest, xs, tgt, wua_full, wub_full, wout_full, final_norm_g.reshape(1, d), lng, lnb, w_spatial[0], bfull,
        gavg, "mid")

    def blocks42(a):
        return a.reshape((4, 2) + a.shape[1:])

    def chip_partials(own, tag):
        land = _push_sibling(own, "rs_sibling_" + tag)
        return [_chip_partial_sum(a, l, core, "cpsum_%s%d" % (tag, i)) for i, (a, l) in enumerate(zip(own, land))]

    cps_up = chip_partials([blocks42(_block_major_cols(dwua.astype(BF16))),
                            blocks42(_block_major_cols(dwub.astype(BF16))),
                            blocks42(dwout.astype(BF16).reshape(N_DEV, d // N_DEV, d))], "up")

    dproj = _attn_bwd(qkv, do, rs, dproj, "attn_bwd")
    dfg, dlng, dlnb, db = _small_reduce(dfg8, dlng8, dlnb8, dbfull, "small_reduce")
    wsp_rows = N_GROUPS * SGU_CHUNK
    dwin, small_parts, land_up = _dw_in(
        ht, dproj, [dfg, dlng, dlnb, db, dwsp.reshape(wsp_rows, SGU_CHUNK)], cps_up, "dwin")

    cps_in = chip_partials([blocks42(_block_major_cols(dwin))], "in")
    grad_x, dng8, land_in = _dh_and_grad_x(dproj, w_full, xs, norm_g, dx2, cps_in, "dh")
    cps = cps_in + cps_up
    land2 = list(land_in) + list(land_up)

    ng_parts, loss_parts = _allgather([dng8, loss_b], "ag_tail")

    def small_layouts(ng, fg, lg, lb, bs, ws):
        return [ng.reshape(1, d), fg.reshape(1, d), lg.reshape(1, D_SGU), lb.reshape(1, D_SGU),
                bs.reshape(N_GROUPS, SGU_CHUNK), ws.reshape(wsp_rows, SGU_CHUNK)]

    sm, loss_sum = _adam_small(
        [ng_parts] + list(small_parts),
        small_layouts(norm_g, final_norm_g, sgu_ln_g, sgu_ln_b, b_spatial, w_spatial),
        small_layouts(m_norm_g, m_final_norm_g, m_sgu_ln_g, m_sgu_ln_b, m_b_spatial, m_w_spatial),
        small_layouts(v_norm_g, v_final_norm_g, v_sgu_ln_g, v_sgu_ln_b, v_b_spatial, v_w_spatial),
        loss_parts, "adam_small")
    small_shapes = [norm_g.shape, final_norm_g.shape, sgu_ln_g.shape, sgu_ln_b.shape, b_spatial.shape, w_spatial.shape]
    sm = [[a.reshape(shp) for a, shp in zip(kind, small_shapes)] for kind in sm]

    big = []
    for i, (w, m, v) in enumerate([(w_in, m_w_in, v_w_in), (w_up_a, m_w_up_a, v_w_up_a),
                                   (w_up_b, m_w_up_b, v_w_up_b), (w_out, m_w_out, v_w_out)]):
        res = _adam_shard(cps[i], land2[i], chip, w[0], m[0], v[0], "adam%d" % i)
        big.append([r[None] for r in res])

    loss = loss_sum[0, 0]

    def per_kind(kd):
        return [sm[kd][0], big[0][kd], sm[kd][2], sm[kd][3], sm[kd][5], sm[kd][4], big[1][kd], big[2][kd], big[3][kd],
                sm[kd][1]]

    return (loss, grad_x[None], *per_kind(0), *per_kind(1), *per_kind(2), *per_kind(3))
```

```python
import math

import jax
import jax.numpy as jnp
from jax import lax
from jax.experimental import pallas as pl
from jax.experimental.pallas import tpu as pltpu

F32 = jnp.float32
BF16 = jnp.bfloat16
MESH = pl.DeviceIdType.MESH

N_DEV = 8
N_HEADS = 8
HEAD_DIM = 64
D_SB = N_HEADS * HEAD_DIM
N_GROUPS = 8
GROUP_DIM = 64
D_SGU = N_GROUPS * GROUP_DIM
SGU_CHUNK = 128
CHUNK = 64
EPS = 1e-6
LANES = 128
N_PAIRS = N_HEADS // 2
QKV_COLS = 3 * D_SB
ATT_BLOCK = 256
CARRY_FLOOR = -110.0
R_UNREACHED = -1e30

ADAM_LR = 0.001
ADAM_B1 = 0.9
ADAM_B2 = 0.999
ADAM_EPS = 1e-08
ADAM_WD = 0.01
ADAM_STEP = 10

VMEM_LIMIT = 56 * 1024 * 1024


def _cparams(sem=None, vmem=VMEM_LIMIT):
    return pltpu.CompilerParams(dimension_semantics=sem, vmem_limit_bytes=vmem)


def _dot(a, b):
    return jnp.dot(a, b, preferred_element_type=F32)


def _dot_nt(a, b):
    return lax.dot_general(a, b, (((1,), (1,)), ((), ())), preferred_element_type=F32)


def _dot_tn(a, b):
    return lax.dot_general(a, b, (((0,), (0,)), ((), ())), preferred_element_type=F32)


def _split_hi_lo(a):
    hi = a.astype(BF16)
    lo = (a - hi.astype(F32)).astype(BF16)
    return hi, lo


def _sigmoid(x):
    return 1.0 / (1.0 + jnp.exp(-x))


_GELU_C = math.sqrt(2.0 / math.pi)


def _gelu_and_grad(x):
    x2 = x * x
    inner = _GELU_C * (x + 0.044715 * (x2 * x))
    t = jnp.tanh(inner)
    cdf = 0.5 * (1.0 + t)
    g = x * cdf
    dg = cdf + x * (0.5 * (1.0 - t * t)) * (_GELU_C * (1.0 + 3.0 * 0.044715 * x2))
    return g, dg


def _coords():
    return lax.axis_index("x"), lax.axis_index("y"), lax.axis_index("c")


def _dev_index(px, py, pc):
    return 4 * px + 2 * py + pc


def _allgather(blocks, name):
    n = len(blocks)

    def body(*refs):
        gather = _Gather(refs[:n], refs[n:2 * n], *refs[2 * n:])
        gather.issue()
        gather.finish()

    any_spec = pl.BlockSpec(memory_space=pl.ANY)
    return pl.pallas_call(
        body, name=name,
        out_shape=_gather_out_shapes(blocks),
        in_specs=[any_spec] * n, out_specs=[any_spec] * n,
        scratch_shapes=_gather_semaphores(n),
    )(*blocks)


def _gather_out_shapes(blocks):
    return [jax.ShapeDtypeStruct((N_DEV,) + b.shape, b.dtype) for b in blocks]


def _gather_semaphores(n):
    return [pltpu.SemaphoreType.DMA((n, 7)), pltpu.SemaphoreType.DMA((n, 7)), pltpu.SemaphoreType.DMA((n,))]


class _Gather:
    def __init__(self, ins, outs, send_sems, recv_sems, local_sems):
        self.ins, self.outs = ins, outs
        self.send_sems, self.recv_sems, self.local_sems = send_sems, recv_sems, local_sems
        self.n = len(ins)
        x, y, c = _coords()
        self.c = c
        self.me, self.sibling = (x, y, c), (x, y, 1 - c)
        self.chips = [(1 - x, y), (x, 1 - y), (1 - x, 1 - y)]

    def _copy(self, a, k, block, to, src=None):
        dst = self.outs[a].at[_dev_index(*block)]
        return pltpu.make_async_remote_copy(
            src_ref=dst if src is None else src, dst_ref=dst,
            send_sem=self.send_sems.at[a, k], recv_sem=self.recv_sems.at[a, k],
            device_id=to, device_id_type=MESH)

    def _mine(self):
        return [pltpu.make_async_copy(self.ins[a], self.outs[a].at[_dev_index(*self.me)], self.local_sems.at[a])
                for a in range(self.n)]

    def _first(self):
        first = []
        for a in range(self.n):
            first.append(self._copy(a, 0, self.me, self.sibling, src=self.ins[a]))
            first += [self._copy(a, 1 + j, self.me, (*chip, self.c), src=self.ins[a])
                      for j, chip in enumerate(self.chips)]
        return first

    def issue(self):
        for cp in self._mine() + self._first():
            cp.start()

    def finish(self):
        c = self.c
        passed = []
        for j, chip in enumerate(self.chips):
            for a in range(self.n):
                self._copy(a, 1 + j, (*chip, c), self.me).wait_recv()
                fwd = self._copy(a, 4 + j, (*chip, c), self.sibling)
                fwd.start()
                passed.append(fwd)
        for a in range(self.n):
            self._copy(a, 0, self.sibling, self.me).wait_recv()
            for j, chip in enumerate(self.chips):
                self._copy(a, 4 + j, (*chip, 1 - c), self.me).wait_recv()
        for cp in self._first() + passed:
            cp.wait_send()
        for cp in self._mine():
            cp.wait()


def _push_sibling(arrs, name):
    n = len(arrs)

    def body(*refs):
        ins, outs = refs[:n], refs[n:2 * n]
        send_sems, recv_sems = refs[2 * n:]
        x, y, c = _coords()
        sibling = (x, y, 1 - c)
        copies = []
        for a in range(n):
            for k in range(4):
                copies.append(pltpu.make_async_remote_copy(
                    src_ref=ins[a].at[k, 1 - c], dst_ref=outs[a].at[k],
                    send_sem=send_sems.at[a, k], recv_sem=recv_sems.at[a, k],
                    device_id=sibling, device_id_type=MESH))
        for cp in copies:
            cp.start()
        for cp in copies:
            cp.wait()

    any_spec = pl.BlockSpec(memory_space=pl.ANY)
    return pl.pallas_call(
        body, name=name,
        out_shape=[jax.ShapeDtypeStruct((4,) + a.shape[2:], a.dtype) for a in arrs],
        in_specs=[any_spec] * n, out_specs=[any_spec] * n,
        scratch_shapes=[pltpu.SemaphoreType.DMA((n, 4)), pltpu.SemaphoreType.DMA((n, 4))],
    )(*arrs)


def _chip_push_copies(ins, outs, send_sems, recv_sems):
    x, y, c = _coords()
    chips = [(1 - x, y), (x, 1 - y), (1 - x, 1 - y)]
    return [pltpu.make_async_remote_copy(
        src_ref=ins[a].at[2 * px + py], dst_ref=outs[a].at[r],
        send_sem=send_sems.at[a, r], recv_sem=recv_sems.at[a, r],
        device_id=(px, py, c), device_id_type=MESH)
        for a in range(len(ins)) for r, (px, py) in enumerate(chips)]


def _chip_partial_sum(own, land, core, name):
    _, _, rows, cols = own.shape
    tr = rows

    def body(core_ref, own_ref, land_ref, out_ref):
        del core_ref
        out_ref[...] = (own_ref[...].astype(F32) + land_ref[...].astype(F32)).astype(out_ref.dtype)

    return pl.pallas_call(
        body, name=name,
        out_shape=jax.ShapeDtypeStruct((4, rows, cols), own.dtype),
        grid_spec=pltpu.PrefetchScalarGridSpec(
            num_scalar_prefetch=1, grid=(4, rows // tr),
            in_specs=[pl.BlockSpec((None, None, tr, cols), lambda k, r, core: (k, core[0], r, 0)),
                      pl.BlockSpec((None, tr, cols), lambda k, r, core: (k, r, 0))],
            out_specs=pl.BlockSpec((None, tr, cols), lambda k, r, core: (k, r, 0))),
        compiler_params=_cparams(("parallel", "parallel")),
    )(core, own, land)


def _adamw_math(w, g, m, v):
    m = ADAM_B1 * m + (1.0 - ADAM_B1) * g
    v = ADAM_B2 * v + (1.0 - ADAM_B2) * (g * g)
    m_hat = m / (1.0 - ADAM_B1 ** ADAM_STEP)
    v_hat = v / (1.0 - ADAM_B2 ** ADAM_STEP)
    delta = -ADAM_LR * (m_hat / (jnp.sqrt(v_hat) + ADAM_EPS) + ADAM_WD * w)
    return delta, m, v


def _adam_shard(cp, land, chip, w, m, v, name):
    rows, cols = w.shape
    tr = rows // 4

    def body(chip_ref, cp_ref, land_ref, w_ref, m_ref, v_ref, g_out, d_out, m_out, v_out):
        del chip_ref
        g = cp_ref[...].astype(F32)
        for r in range(3):
            g = g + land_ref[r].astype(F32)
        delta, m_new, v_new = _adamw_math(w_ref[...], g, m_ref[...], v_ref[...])
        g_out[...] = g
        d_out[...] = delta
        m_out[...] = m_new
        v_out[...] = v_new

    tile = pl.BlockSpec((tr, cols), lambda r, chip: (r, 0))
    out = jax.ShapeDtypeStruct((rows, cols), F32)
    return pl.pallas_call(
        body, name=name, out_shape=[out] * 4,
        grid_spec=pltpu.PrefetchScalarGridSpec(
            num_scalar_prefetch=1, grid=(rows // tr,),
            in_specs=[pl.BlockSpec((None, tr, cols), lambda r, chip: (chip[0], r, 0)),
                      pl.BlockSpec((3, tr, cols), lambda r, chip: (0, r, 0)),
                      tile, tile, tile],
            out_specs=[tile] * 4),
        compiler_params=_cparams(("parallel",)),
    )(chip, cp, land, w, m, v)


def _adam_small(parts, ws, ms, vs, loss_parts, name):
    n = len(ws)

    def body(*refs):
        p_refs, w_refs, m_refs, v_refs = refs[:n], refs[n:2 * n], refs[2 * n:3 * n], refs[3 * n:4 * n]
        loss_ref, outs, loss_out = refs[4 * n], refs[4 * n + 1:-1], refs[-1]
        total = loss_ref[0]
        for dev in range(1, N_DEV):
            total = total + loss_ref[dev]
        loss_out[...] = total
        for i in range(n):
            g = p_refs[i][0]
            for dev in range(1, N_DEV):
                g = g + p_refs[i][dev]
            if g.shape[0] != w_refs[i].shape[0]:
                g = jnp.sum(g, axis=0, keepdims=True)
            delta, m_new, v_new = _adamw_math(w_refs[i][...], g, m_refs[i][...], v_refs[i][...])
            outs[i][...] = g
            outs[n + i][...] = delta
            outs[2 * n + i][...] = m_new
            outs[3 * n + i][...] = v_new

    out_shapes = [jax.ShapeDtypeStruct(w.shape, F32) for w in ws] * 4
    out_shapes.append(jax.ShapeDtypeStruct(loss_parts.shape[1:], F32))
    res = pl.pallas_call(body, name=name, out_shape=out_shapes, compiler_params=_cparams())(
        *parts, *ws, *ms, *vs, loss_parts)
    return [res[k * n:(k + 1) * n] for k in range(4)], res[-1]


def _rmsnorm_fwd(x, g, riders, name):
    s, d = x.shape
    ts = 512
    nt = s // ts
    nr = len(riders)

    def body(x_ref, g_ref, *rest):
        rider_in, (h_ref, ht_ref) = rest[:nr], rest[nr:nr + 2]
        rider_out, sems = rest[nr + 2:2 * nr + 2], rest[2 * nr + 2:]
        i = pl.program_id(0)
        gather = _Gather(rider_in, rider_out, *sems)

        @pl.when(i == 0)
        def _():
            gather.issue()

        xv = x_ref[...]
        rstd = lax.rsqrt(jnp.mean(xv * xv, axis=-1, keepdims=True) + EPS)
        h = xv * rstd * g_ref[...]
        h_ref[...] = h.astype(BF16)
        ht_ref[...] = h.T.astype(BF16)

        @pl.when(i == nt - 1)
        def _():
            gather.finish()

    any_spec = pl.BlockSpec(memory_space=pl.ANY)
    res = pl.pallas_call(
        body, name=name,
        out_shape=[jax.ShapeDtypeStruct((s, d), BF16), jax.ShapeDtypeStruct((d, s), BF16)]
        + _gather_out_shapes(riders),
        grid=(nt,),
        in_specs=[pl.BlockSpec((ts, d), lambda i: (i, 0)), pl.BlockSpec((1, d), lambda i: (0, 0))] + [any_spec] * nr,
        out_specs=[pl.BlockSpec((ts, d), lambda i: (i, 0)), pl.BlockSpec((d, ts), lambda i: (0, i))]
        + [any_spec] * nr,
        scratch_shapes=_gather_semaphores(nr),
        compiler_params=_cparams(("arbitrary",)),
    )(x, g, *riders)
    return res[0], res[1], res[2:]


def _inproj(h, wt, col0, ncols, out_dtype, riders, name):
    s, d = h.shape
    tn = 256
    j0 = col0 // tn
    nj = ncols // tn
    nr = len(riders)

    def body(h_ref, w_ref, *rest):
        rider_in, o_ref, rider_out, sems = rest[:nr], rest[nr], rest[nr + 1:2 * nr + 1], rest[2 * nr + 1:]
        j = pl.program_id(0)
        if nr:
            gather = _Gather(rider_in, rider_out, *sems)

            @pl.when(j == 0)
            def _():
                gather.issue()

        o_ref[...] = _dot_nt(h_ref[...], w_ref[...]).astype(o_ref.dtype)

        if nr:
            @pl.when(j == nj - 1)
            def _():
                gather.finish()

    any_spec = pl.BlockSpec(memory_space=pl.ANY)
    res = pl.pallas_call(
        body, name=name, out_shape=[jax.ShapeDtypeStruct((s, ncols), out_dtype)] + _gather_out_shapes(riders),
        grid=(nj,),
        in_specs=[pl.BlockSpec((s, d), lambda j: (0, 0)), pl.BlockSpec((tn, d), lambda j: (j + j0, 0))]
        + [any_spec] * nr,
        out_specs=[pl.BlockSpec((s, tn), lambda j: (0, j))] + [any_spec] * nr,
        scratch_shapes=_gather_semaphores(nr) if nr else [],
        compiler_params=_cparams(("arbitrary",) if nr else ("parallel",)),
    )(h, wt, *riders)
    return res[0], res[1:]


def _dw_in(ht, dproj, smalls, cps, name):
    d, s = ht.shape
    n = dproj.shape[1]
    tn = 512
    nj = n // tn
    ns, nc = len(smalls), len(cps)

    def body(a_ref, b_ref, *rest):
        small_in, cp_in = rest[:ns], rest[ns:ns + nc]
        o_ref = rest[ns + nc]
        small_out, land_out = rest[ns + nc + 1:2 * ns + nc + 1], rest[2 * ns + nc + 1:2 * (ns + nc) + 1]
        sems = rest[2 * (ns + nc) + 1:]
        j = pl.program_id(0)
        gather = _Gather(small_in, small_out, *sems[:3])

        @pl.when(j == 0)
        def _():
            gather.issue()
            for cp in _chip_push_copies(cp_in, land_out, *sems[3:]):
                cp.start()

        o_ref[...] = _dot(a_ref[...], b_ref[...]).T.astype(o_ref.dtype)

        @pl.when(j == nj - 1)
        def _():
            gather.finish()
            for cp in _chip_push_copies(cp_in, land_out, *sems[3:]):
                cp.wait()

    any_spec = pl.BlockSpec(memory_space=pl.ANY)
    res = pl.pallas_call(
        body, name=name,
        out_shape=[jax.ShapeDtypeStruct((n, d), BF16)] + _gather_out_shapes(smalls)
        + [jax.ShapeDtypeStruct((3,) + a.shape[1:], a.dtype) for a in cps],
        grid=(nj,),
        in_specs=[pl.BlockSpec((d, s), lambda j: (0, 0)), pl.BlockSpec((s, tn), lambda j: (0, j))]
        + [any_spec] * (ns + nc),
        out_specs=[pl.BlockSpec((tn, d), lambda j: (j, 0))] + [any_spec] * (ns + nc),
        scratch_shapes=_gather_semaphores(ns) + [pltpu.SemaphoreType.DMA((nc, 3)), pltpu.SemaphoreType.DMA((nc, 3))],
        compiler_params=_cparams(("arbitrary",)),
    )(ht, dproj, *smalls, *cps)
    return res[0], res[1:1 + ns], res[1 + ns:]


def _dh_and_grad_x(dproj, wt, x, g, dx2, cps, name):
    s, n = dproj.shape
    d = wt.shape[1]
    tm, tk = min(s, 1024), 512
    nk = n // tk
    nm = s // tm
    nc = len(cps)

    def body(a_ref, w_ref, x_ref, g_ref, dx2_ref, *rest):
        cp_refs, gx_ref, dg_ref = rest[:nc], rest[nc], rest[nc + 1]
        land_refs = rest[nc + 2:2 * nc + 2]
        acc_ref, send_sems, recv_sems = rest[2 * nc + 2:]
        i, k = pl.program_id(0), pl.program_id(1)

        @pl.when((i == 0) & (k == 0))
        def _():
            for cp in _chip_push_copies(cp_refs, land_refs, send_sems, recv_sems):
                cp.start()
            dg_ref[...] = jnp.zeros_like(dg_ref)

        @pl.when(k == 0)
        def _():
            acc_ref[...] = jnp.zeros_like(acc_ref)

        acc_ref[...] += _dot(a_ref[...], w_ref[...])

        @pl.when(k == nk - 1)
        def _():
            dh = acc_ref[...]
            xv = x_ref[...]
            rstd = lax.rsqrt(jnp.mean(xv * xv, axis=-1, keepdims=True) + EPS)
            xhat = xv * rstd
            dg_ref[...] += jnp.sum((dh * xhat).reshape(tm // 8, 8, d), axis=0)
            dxh = dh * g_ref[...]
            gx_ref[...] = dx2_ref[...] + rstd * (dxh - xhat * jnp.mean(dxh * xhat, axis=-1, keepdims=True))

        @pl.when((i == nm - 1) & (k == nk - 1))
        def _():
            for cp in _chip_push_copies(cp_refs, land_refs, send_sems, recv_sems):
                cp.wait()

    any_spec = pl.BlockSpec(memory_space=pl.ANY)
    res = pl.pallas_call(
        body, name=name,
        out_shape=[jax.ShapeDtypeStruct((s, d), F32), jax.ShapeDtypeStruct((8, d), F32)]
        + [jax.ShapeDtypeStruct((3,) + a.shape[1:], a.dtype) for a in cps],
        grid=(nm, nk),
        in_specs=[pl.BlockSpec((tm, tk), lambda i, k: (i, k)), pl.BlockSpec((tk, d), lambda i, k: (k, 0)),
                  pl.BlockSpec((tm, d), lambda i, k: (i, 0)), pl.BlockSpec((1, d), lambda i, k: (0, 0)),
                  pl.BlockSpec((tm, d), lambda i, k: (i, 0))] + [any_spec] * nc,
        out_specs=[pl.BlockSpec((tm, d), lambda i, k: (i, 0)), pl.BlockSpec((8, d), lambda i, k: (0, 0))]
        + [any_spec] * nc,
        scratch_shapes=[pltpu.VMEM((tm, d), F32), pltpu.SemaphoreType.DMA((nc, 3)), pltpu.SemaphoreType.DMA((nc, 3))],
        compiler_params=_cparams(("arbitrary", "arbitrary")),
    )(dproj, wt, x, g, dx2, *cps)
    return res[0], res[1], res[2:]


def _log_sigmoids(z):
    l1p = jnp.log(1.0 + jnp.exp(-jnp.abs(z)))
    ls = jnp.minimum(z, 0.0) - l1p
    return ls, ls - z


def _strict_lower_ones(n):
    row = lax.broadcasted_iota(jnp.int32, (n, n), 0)
    col = lax.broadcasted_iota(jnp.int32, (n, n), 1)
    return row, col, (row > col).astype(BF16)


def _attn_fwd(qkv, name):
    s = qkv.shape[0]
    tb = ATT_BLOCK
    nq = s // tb

    def body(q_ref, k_ref, v_ref, o_ref, rs_ref, acc_ref, r_ref, rsv_ref):
        i = pl.program_id(1)
        lane = lax.broadcasted_iota(jnp.int32, (tb, LANES), 1)
        hmask = [lane < HEAD_DIM, lane >= HEAD_DIM]
        q2 = q_ref[...]
        qm = [jnp.where(m, q2, jnp.zeros((), BF16)) * jnp.asarray(HEAD_DIM ** -0.5, BF16) for m in hmask]
        row, col, tri = _strict_lower_ones(tb)
        tri2 = jnp.concatenate([tri, tri], axis=0)
        acc_ref[...] = jnp.zeros_like(acc_ref)
        r_ref[...] = jnp.zeros_like(r_ref)
        rsv_ref[...] = jnp.full_like(rsv_ref, R_UNREACHED)

        def block(j, diag):
            heads = range(2)
            rows = pl.ds(pl.multiple_of(j * tb, tb), tb)
            k2, v2 = k_ref[rows, :], v_ref[rows, :]
            z = [_dot_nt(qm[h], k2) for h in heads]
            ls, lk = [], []
            for h in heads:
                ls_h, lk_h = _log_sigmoids(z[h])
                if diag:
                    lk_h = jnp.where(col < row, lk_h, 0.0)
                ls.append(ls_h)
                lk.append(lk_h)
            r = [r_ref[h] for h in heads]
            suffix = [_dot(jnp.concatenate(_split_hi_lo(lk[h]), axis=1), tri2) for h in heads]
            w = []
            for h in heads:
                w_h = jnp.exp(ls[h] + suffix[h] + jnp.tile(r[h], (1, tb // LANES)))
                if diag:
                    w_h = jnp.where(col < row, w_h, 0.0)
                w.append(w_h.astype(BF16))
            pv = [_dot(w[h], v2) for h in heads]
            for h in heads:
                acc_ref[h] += pv[h]
                rsv_ref[h] = jnp.where(lane == j, r[h], rsv_ref[h])
                r_ref[h] = r[h] + jnp.sum(lk[h], axis=1, keepdims=True)

        block(i, True)

        def live(c):
            jj, rmax = c
            return (jj < i) & (rmax >= CARRY_FLOOR)

        def step(c):
            jj, _ = c
            block(i - 1 - jj, False)
            return jj + 1, jnp.max(r_ref[...])

        lax.while_loop(live, step, (jnp.int32(0), jnp.max(r_ref[...])))
        o_ref[...] = jnp.where(hmask[0], acc_ref[0], acc_ref[1])
        rs_ref[...] = rsv_ref[...]

    return pl.pallas_call(
        body, name=name,
        out_shape=[jax.ShapeDtypeStruct((s, D_SB), F32), jax.ShapeDtypeStruct((N_HEADS, s, LANES), F32)],
        grid=(N_PAIRS, nq),
        in_specs=[pl.BlockSpec((tb, LANES), lambda p, i: (i, p)),
                  pl.BlockSpec((s, LANES), lambda p, i: (0, N_PAIRS + p)),
                  pl.BlockSpec((s, LANES), lambda p, i: (0, 2 * N_PAIRS + p))],
        out_specs=[pl.BlockSpec((tb, LANES), lambda p, i: (i, p)),
                   pl.BlockSpec((2, tb, LANES), lambda p, i: (p, i, 0))],
        scratch_shapes=[pltpu.VMEM((2, tb, LANES), F32), pltpu.VMEM((2, tb, LANES), F32),
                        pltpu.VMEM((2, tb, LANES), F32)],
        compiler_params=_cparams(("parallel", "arbitrary")),
    )(qkv, qkv, qkv)


def _attn_bwd(qkv, do, rs, dproj, name):
    s = qkv.shape[0]
    tb = ATT_BLOCK
    nq = s // tb
    scale = HEAD_DIM ** -0.5

    def body(q_ref, k_ref, v_ref, do_ref, rs_ref, dproj_hbm, out_hbm, dq_acc, dk_acc, dv_acc, dqi_ref, pc_ref,
             stage_ref, out_sems):
        del dproj_hbm
        pair = pl.program_id(0)
        lane = lax.broadcasted_iota(jnp.int32, (tb, LANES), 1)
        hmask = [lane < HEAD_DIM, lane >= HEAD_DIM]
        row, col, tri = _strict_lower_ones(tb)
        triu = (row < col).astype(BF16)
        tri2 = jnp.concatenate([tri, tri], axis=0)
        zero = jnp.zeros((), BF16)
        dq_acc[...] = jnp.zeros_like(dq_acc)
        dk_acc[...] = jnp.zeros_like(dk_acc)
        dv_acc[...] = jnp.zeros_like(dv_acc)

        def qblock(i, carry):
            qrows = pl.ds(pl.multiple_of(i * tb, tb), tb)
            q2 = q_ref[qrows, :]
            do2 = do_ref[qrows, :]
            qm = [jnp.where(m, q2, zero) * jnp.asarray(scale, BF16) for m in hmask]
            dom = [jnp.where(m, do2, zero) for m in hmask]
            rs_i = [rs_ref[h, qrows, :] for h in range(2)]
            dqi_ref[...] = jnp.zeros_like(dqi_ref)
            pc_ref[...] = jnp.zeros_like(pc_ref)
            reached = jnp.maximum(jnp.max(rs_i[0], axis=0, keepdims=True), jnp.max(rs_i[1], axis=0, keepdims=True))
            key_block = lax.broadcasted_iota(jnp.int32, (1, LANES), 1)
            n_reached = jnp.sum(((reached >= CARRY_FLOOR) & (key_block <= i)).astype(jnp.int32))

            def block(j, diag):
                krows = pl.ds(pl.multiple_of(j * tb, tb), tb)
                k2 = k_ref[krows, :]
                v2 = v_ref[krows, :]
                heads = range(2)
                z = [_dot_nt(qm[h], k2) for h in heads]
                dw = [_dot_nt(dom[h], v2) for h in heads]
                ls, lk = [], []
                for h in heads:
                    ls_h, lk_h = _log_sigmoids(z[h])
                    if diag:
                        lk_h = jnp.where(col < row, lk_h, 0.0)
                    ls.append(ls_h)
                    lk.append(lk_h)
                r = [jnp.sum(jnp.where(lane == j, rs_i[h], 0.0), axis=1, keepdims=True) for h in heads]
                suffix = [_dot(jnp.concatenate(_split_hi_lo(lk[h]), axis=1), tri2) for h in heads]
                w, g = [], []
                for h in heads:
                    w_h = jnp.exp(ls[h] + suffix[h] + r[h])
                    if diag:
                        w_h = jnp.where(col < row, w_h, 0.0)
                    w.append(w_h.astype(BF16))
                    g.append(w_h * dw[h])
                pc = [pc_ref[h] for h in heads]
                prefix = [_dot(g[h].astype(BF16), triu) for h in heads]
                dzb = []
                for h in heads:
                    sig = jnp.exp(ls[h])
                    dz = g[h] - sig * (g[h] + prefix[h] + jnp.tile(pc[h], (1, tb // LANES)))
                    if diag:
                        dz = jnp.where(col < row, dz, 0.0)
                    dzb.append(dz.astype(BF16))
                    pc_ref[h] = pc[h] + jnp.sum(g[h], axis=1, keepdims=True)
                dq = [_dot(dzb[h], jnp.where(hmask[h], k2, zero)) for h in heads]
                dk = [_dot_tn(dzb[h], qm[h]) for h in heads]
                dv = [_dot_tn(w[h], dom[h]) for h in heads]
                dqi_ref[...] += dq[0] + dq[1]
                dk_acc[krows, :] += dk[0] + dk[1]
                dv_acc[krows, :] += dv[0] + dv[1]

            def step(j, c):
                block(j, False)
                return c

            lax.fori_loop(i + 1 - n_reached, i, step, 0)
            block(i, True)
            dq_acc[qrows, :] += dqi_ref[...] * scale
            return carry

        lax.fori_loop(0, nq, qblock, 0)
        copies = []
        for t, acc in enumerate((dq_acc, dk_acc, dv_acc)):
            stage_ref[t] = acc[...].astype(BF16)
            col0 = pl.multiple_of(t * D_SB + pair * LANES, LANES)
            copies.append(pltpu.make_async_copy(stage_ref.at[t], out_hbm.at[:, pl.ds(col0, LANES)], out_sems.at[t]))
        for cp in copies:
            cp.start()
        for cp in copies:
            cp.wait()

    return pl.pallas_call(
        body, name=name,
        out_shape=jax.ShapeDtypeStruct(dproj.shape, BF16),
        grid=(N_PAIRS,),
        in_specs=[pl.BlockSpec((s, LANES), lambda p: (0, p)),
                  pl.BlockSpec((s, LANES), lambda p: (0, N_PAIRS + p)),
                  pl.BlockSpec((s, LANES), lambda p: (0, 2 * N_PAIRS + p)),
                  pl.BlockSpec((s, LANES), lambda p: (0, p)),
                  pl.BlockSpec((2, s, LANES), lambda p: (p, 0, 0)),
                  pl.BlockSpec(memory_space=pl.ANY)],
        out_specs=pl.BlockSpec(memory_space=pl.ANY),
        scratch_shapes=[pltpu.VMEM((s, LANES), F32), pltpu.VMEM((s, LANES), F32), pltpu.VMEM((s, LANES), F32),
                        pltpu.VMEM((tb, LANES), F32), pltpu.VMEM((2, tb, LANES), F32),
                        pltpu.VMEM((3, s, LANES), BF16), pltpu.SemaphoreType.DMA((3,))],
        input_output_aliases={5: 0},
        compiler_params=_cparams(("arbitrary",)),
    )(qkv, qkv, qkv, do, rs, dproj)


def _mid(o, rest, x, tgt, wua, wub, wout, fg, lng, lnb, wsp, bfull, gavg, name):
    s, d = x.shape
    ts = 256
    nt = s // ts
    nchunk = ts // SGU_CHUNK
    n_rest = rest.shape[1]

    def body(o_ref, rest_ref, x_ref, t_ref, wua_ref, wub_ref, wout_ref, fg_ref, lng_ref, lnb_ref, wsp_ref, bfull_ref,
             gavg_ref, loss_ref, dx2_ref, do_ref, dproj_ref, dwua_ref, dwub_ref, dwout_ref, dfg_ref, dlng_ref,
             dlnb_ref, dwsp_ref, dbfull_ref):
        step = pl.program_id(0)

        @pl.when(step == 0)
        def _():
            for ref in (loss_ref, dwua_ref, dwub_ref, dwout_ref, dfg_ref, dlng_ref, dlnb_ref, dwsp_ref, dbfull_ref):
                ref[...] = jnp.zeros_like(ref)

        gavg = gavg_ref[...]

        def gmean(a):
            return _dot(a.astype(BF16), gavg)

        def colsum8(a):
            return jnp.sum(a.reshape(ts // 8, 8, a.shape[1]), axis=0)

        z_a = rest_ref[:, 0:512]
        u_b = rest_ref[:, 512:1024]
        v_b = rest_ref[:, 1024:1536]
        z_b = rest_ref[:, 1536:2048]
        g_a = rest_ref[:, 2048:2048 + d]
        g_b = rest_ref[:, 2048 + d:2048 + 2 * d]
        ov = o_ref[...]
        sa = _sigmoid(z_a)
        silu_a = z_a * sa
        y_a = ov * silu_a
        ug, dug_du = _gelu_and_grad(u_b)
        vg, dvg_dv = _gelu_and_grad(v_b)
        mu = gmean(vg)
        cen = vg - mu
        rstd_g = lax.rsqrt(gmean(cen * cen) + EPS)
        vhat = cen * rstd_g
        vn = vhat * lng_ref[...] + lnb_ref[...]
        vnb = vn.astype(BF16)

        t_idx = lax.broadcasted_iota(jnp.int32, (SGU_CHUNK, SGU_CHUNK), 0)
        s_idx = lax.broadcasted_iota(jnp.int32, (SGU_CHUNK, SGU_CHUNK), 1)
        causal = (s_idx // CHUNK) <= (t_idx // CHUNK)
        wm = [jnp.where(causal, wsp_ref[g], 0.0) for g in range(N_GROUPS)]
        wmb = [w.astype(BF16) for w in wm]
        wmtb = [w.T.astype(BF16) for w in wm]
        lane = lax.broadcasted_iota(jnp.int32, (SGU_CHUNK, LANES), 1)
        first = lane < GROUP_DIM
        bfull = bfull_ref[...]

        mixed_rows = []
        for n in range(nchunk):
            r0, r1 = n * SGU_CHUNK, (n + 1) * SGU_CHUNK
            pieces = []
            for p in range(N_GROUPS // 2):
                blk = vnb[r0:r1, p * LANES:(p + 1) * LANES]
                pieces.append(jnp.where(first, _dot(wmb[2 * p], blk), _dot(wmb[2 * p + 1], blk)))
            mixed_rows.append(jnp.concatenate(pieces, axis=1) + bfull)
        mixed = jnp.concatenate(mixed_rows, axis=0)
        sg = ug * mixed
        sb = _sigmoid(z_b)
        silu_b = z_b * sb
        y_b = sg * silu_b
        y_ab = y_a.astype(BF16)
        y_bb = y_b.astype(BF16)
        p_a = _dot(y_ab, wua_ref[...])
        p_b = _dot(y_bb, wub_ref[...])
        ga_s = _sigmoid(g_a)
        gb_s = _sigmoid(g_b)
        merged_b = (ga_s * p_a + gb_s * p_b).astype(BF16)
        x2 = x_ref[...] + _dot(merged_b, wout_ref[...])
        rstd = lax.rsqrt(jnp.mean(x2 * x2, axis=-1, keepdims=True) + EPS)
        xhat = x2 * rstd
        fg_v = fg_ref[...]
        diff = xhat * fg_v - t_ref[...]
        loss_ref[...] += 0.5 * jnp.sum(jnp.sum(diff * diff, axis=-1, keepdims=True) * (1.0 / d))

        dy = diff * (1.0 / d)
        dfg_ref[...] += colsum8(dy * xhat)
        dxh = dy * fg_v
        dx2 = rstd * (dxh - xhat * jnp.mean(dxh * xhat, axis=-1, keepdims=True))
        dx2_ref[...] = dx2
        dx2b = dx2.astype(BF16)
        dwout_ref[...] += _dot_tn(merged_b, dx2b)
        dmerged = _dot_nt(dx2b, wout_ref[...])
        dp_a = dmerged * ga_s
        dp_b = dmerged * gb_s
        dproj_ref[:, QKV_COLS + 2048:QKV_COLS + 2048 + d] = (dmerged * p_a * (ga_s * (1.0 - ga_s))).astype(BF16)
        dproj_ref[:, QKV_COLS + 2048 + d:QKV_COLS + 2048 + 2 * d] = (dmerged * p_b * (gb_s * (1.0 - gb_s))).astype(BF16)
        dp_ab = dp_a.astype(BF16)
        dp_bb = dp_b.astype(BF16)
        dwua_ref[...] += _dot_tn(y_ab, dp_ab)
        dwub_ref[...] += _dot_tn(y_bb, dp_bb)
        dy_a = _dot_nt(dp_ab, wua_ref[...])
        dy_b = _dot_nt(dp_bb, wub_ref[...])
        do_ref[...] = (dy_a * silu_a).astype(BF16)
        dproj_ref[:, QKV_COLS:QKV_COLS + 512] = (dy_a * ov * (sa * (1.0 + z_a * (1.0 - sa)))).astype(BF16)
        dsg = dy_b * silu_b
        dproj_ref[:, QKV_COLS + 1536:QKV_COLS + 2048] = (dy_b * sg * (sb * (1.0 + z_b * (1.0 - sb)))).astype(BF16)
        dproj_ref[:, QKV_COLS + 512:QKV_COLS + 1024] = (dsg * mixed * dug_du).astype(BF16)
        dmixed = dsg * ug
        dmb = dmixed.astype(BF16)
        zero = jnp.zeros((), BF16)
        dvn_rows = []
        db = jnp.zeros((SGU_CHUNK, D_SGU), F32)
        for n in range(nchunk):
            r0, r1 = n * SGU_CHUNK, (n + 1) * SGU_CHUNK
            db = db + dmixed[r0:r1, :]
            pieces = []
            for p in range(N_GROUPS // 2):
                cols = slice(p * LANES, (p + 1) * LANES)
                dm_blk = dmb[r0:r1, cols]
                vn_blk = vnb[r0:r1, cols]
                dwsp_ref[2 * p] += _dot_nt(jnp.where(first, dm_blk, zero), vn_blk)
                dwsp_ref[2 * p + 1] += _dot_nt(jnp.where(first, zero, dm_blk), vn_blk)
                pieces.append(jnp.where(first, _dot(wmtb[2 * p], dm_blk), _dot(wmtb[2 * p + 1], dm_blk)))
            dvn_rows.append(jnp.concatenate(pieces, axis=1))
        dbfull_ref[...] += db
        dvn = jnp.concatenate(dvn_rows, axis=0)
        dlng_ref[...] += colsum8(dvn * vhat)
        dlnb_ref[...] += colsum8(dvn)
        dvhat = dvn * lng_ref[...]
        dcen = rstd_g * (dvhat - gmean(dvhat) - vhat * gmean(dvhat * vhat))
        dproj_ref[:, QKV_COLS + 1024:QKV_COLS + 1536] = (dcen * dvg_dv).astype(BF16)

        @pl.when(step == nt - 1)
        def _():
            for g in range(N_GROUPS):
                dwsp_ref[g] = jnp.where(causal, dwsp_ref[g], 0.0)

    def tile(cols):
        return pl.BlockSpec((ts, cols), lambda i: (i, 0))

    def whole(shape):
        return pl.BlockSpec(shape, lambda i: (0,) * len(shape))

    out_shapes = [
        jax.ShapeDtypeStruct((8, LANES), F32),
        jax.ShapeDtypeStruct((s, d), F32),
        jax.ShapeDtypeStruct((s, D_SB), BF16),
        jax.ShapeDtypeStruct((s, QKV_COLS + n_rest), BF16),
        jax.ShapeDtypeStruct((D_SB, d), F32),
        jax.ShapeDtypeStruct((D_SGU, d), F32),
        jax.ShapeDtypeStruct((d, d), F32),
        jax.ShapeDtypeStruct((8, d), F32),
        jax.ShapeDtypeStruct((8, D_SGU), F32),
        jax.ShapeDtypeStruct((8, D_SGU), F32),
        jax.ShapeDtypeStruct((N_GROUPS, SGU_CHUNK, SGU_CHUNK), F32),
        jax.ShapeDtypeStruct((SGU_CHUNK, D_SGU), F32),
    ]
    out_specs = [whole((8, LANES)), tile(d), tile(D_SB), tile(QKV_COLS + n_rest), whole((D_SB, d)), whole((D_SGU, d)),
                 whole((d, d)), whole((8, d)), whole((8, D_SGU)), whole((8, D_SGU)),
                 whole((N_GROUPS, SGU_CHUNK, SGU_CHUNK)), whole((SGU_CHUNK, D_SGU))]
    in_specs = [tile(D_SB), tile(n_rest), tile(d), tile(d), whole((D_SB, d)), whole((D_SGU, d)), whole((d, d)),
                whole((1, d)), whole((1, D_SGU)), whole((1, D_SGU)), whole((N_GROUPS, SGU_CHUNK, SGU_CHUNK)),
                whole((SGU_CHUNK, D_SGU)), whole((D_SGU, D_SGU))]
    return pl.pallas_call(
        body, name=name, out_shape=out_shapes, grid=(nt,), in_specs=in_specs, out_specs=out_specs,
        compiler_params=_cparams(("arbitrary",)),
    )(o, rest, x, tgt, wua, wub, wout, fg, lng, lnb, wsp, bfull, gavg)


def _small_reduce(dfg8, dlng8, dlnb8, dbfull, name):
    d = dfg8.shape[1]

    def body(dfg_ref, dlng_ref, dlnb_ref, dbfull_ref, fg_out, lng_out, lnb_out, b_out):
        fg_out[...] = jnp.sum(dfg_ref[...], axis=0, keepdims=True)
        lng_out[...] = jnp.sum(dlng_ref[...], axis=0, keepdims=True)
        lnb_out[...] = jnp.sum(dlnb_ref[...], axis=0, keepdims=True)
        grp = lax.broadcasted_iota(jnp.int32, (D_SGU, LANES), 0) // GROUP_DIM
        col = lax.broadcasted_iota(jnp.int32, (D_SGU, LANES), 1)
        sel = (grp == col).astype(BF16)
        a = dbfull_ref[...]
        hi, lo = _split_hi_lo(a)
        lo2 = (a - hi.astype(F32) - lo.astype(F32)).astype(BF16)
        by_pos = _dot(hi, sel) + _dot(lo, sel) + _dot(lo2, sel)
        b_out[...] = by_pos.T[0:N_GROUPS, :]

    return pl.pallas_call(
        body, name=name,
        out_shape=[jax.ShapeDtypeStruct((1, d), F32), jax.ShapeDtypeStruct((1, D_SGU), F32),
                   jax.ShapeDtypeStruct((1, D_SGU), F32), jax.ShapeDtypeStruct((N_GROUPS, SGU_CHUNK), F32)],
        compiler_params=_cparams(),
    )(dfg8, dlng8, dlnb8, dbfull)


def _block_major_cols(w):
    r, n = w.shape
    return jnp.transpose(w.reshape(r, N_DEV, n // N_DEV), (1, 0, 2))


def _from_block_major_cols(w):
    nb, r, c = w.shape
    return jnp.transpose(w, (1, 0, 2)).reshape(r, nb * c)


def kernel(x, norm_g, w_in, sgu_ln_g, sgu_ln_b, w_spatial, b_spatial, w_up_a, w_up_b, w_out, final_norm_g, loss_target, m_norm_g, m_w_in, m_sgu_ln_g, m_sgu_ln_b, m_w_spatial, m_b_spatial, m_w_up_a, m_w_up_b, m_w_out, m_final_norm_g, v_norm_g, v_w_in, v_sgu_ln_g, v_sgu_ln_b, v_w_spatial, v_b_spatial, v_w_up_a, v_w_up_b, v_w_out, v_final_norm_g):
    s, d = x.shape[1], x.shape[2]
    xs = x[0]
    tgt = loss_target[0]
    cx, cy, cc = _coords()
    core = jnp.reshape(cc, (1,)).astype(jnp.int32)
    chip = jnp.reshape(2 * cx + cy, (1,)).astype(jnp.int32)

    h, ht, (g_win,) = _rmsnorm_fwd(xs, norm_g, [jnp.transpose(w_in[0]).astype(BF16)], "norm")
    d_in = N_DEV * w_in.shape[2]
    w_full = g_win.reshape(d_in, d)
    qkv, _ = _inproj(h, w_full, 0, QKV_COLS, BF16, [], "inproj_qkv")
    rest, (g_wua, g_wub, g_wout) = _inproj(
        h, w_full, QKV_COLS, d_in - QKV_COLS, F32,
        [w_up_a[0].astype(BF16), w_up_b[0].astype(BF16), w_out[0].astype(BF16)], "inproj_rest")
    wua_full = _from_block_major_cols(g_wua)
    wub_full = _from_block_major_cols(g_wub)
    wout_full = g_wout.reshape(d, d)
    o, rs = _attn_fwd(qkv, "attn_fwd")

    lng = sgu_ln_g.reshape(1, D_SGU)
    lnb = sgu_ln_b.reshape(1, D_SGU)
    bfull = jnp.repeat(jnp.transpose(b_spatial[0]), GROUP_DIM, axis=1)
    grp = jnp.arange(D_SGU) // GROUP_DIM
    gavg = jnp.where(grp[:, None] == grp[None, :], 1.0 / GROUP_DIM, 0.0).astype(BF16)
    (loss_b, dx2, do, dproj, dwua, dwub, dwout, dfg8, dlng8, dlnb8, dwsp, dbfull) = _mid(
        o, rest, xs, tgt, wua_full, wub_full, wout_full, final_norm_g.reshape(1, d), lng, lnb, w_spatial[0], bfull,
        gavg, "mid")

    def blocks42(a):
        return a.reshape((4, 2) + a.shape[1:])

    def chip_partials(own, tag):
        land = _push_sibling(own, "rs_sibling_" + tag)
        return [_chip_partial_sum(a, l, core, "cpsum_%s%d" % (tag, i)) for i, (a, l) in enumerate(zip(own, land))]

    cps_up = chip_partials([blocks42(_block_major_cols(dwua.astype(BF16))),
                            blocks42(_block_major_cols(dwub.astype(BF16))),
                            blocks42(dwout.astype(BF16).reshape(N_DEV, d // N_DEV, d))], "up")

    dproj = _attn_bwd(qkv, do, rs, dproj, "attn_bwd")
    dfg, dlng, dlnb, db = _small_reduce(dfg8, dlng8, dlnb8, dbfull, "small_reduce")
    wsp_rows = N_GROUPS * SGU_CHUNK
    dwin, small_parts, land_up = _dw_in(
        ht, dproj, [dfg, dlng, dlnb, db, dwsp.reshape(wsp_rows, SGU_CHUNK)], cps_up, "dwin")

    cps_in = chip_partials([dwin.reshape(4, 2, d_in // N_DEV, d)], "in")
    grad_x, dng8, land_in = _dh_and_grad_x(dproj, w_full, xs, norm_g, dx2, cps_in, "dh")
    cps = cps_in + cps_up
    land2 = list(land_in) + list(land_up)

    ng_parts, loss_parts = _allgather([dng8, loss_b], "ag_tail")

    def small_layouts(ng, fg, lg, lb, bs, ws):
        return [ng.reshape(1, d), fg.reshape(1, d), lg.reshape(1, D_SGU), lb.reshape(1, D_SGU),
                bs.reshape(N_GROUPS, SGU_CHUNK), ws.reshape(wsp_rows, SGU_CHUNK)]

    sm, loss_sum = _adam_small(
        [ng_parts] + list(small_parts),
        small_layouts(norm_g, final_norm_g, sgu_ln_g, sgu_ln_b, b_spatial, w_spatial),
        small_layouts(m_norm_g, m_final_norm_g, m_sgu_ln_g, m_sgu_ln_b, m_b_spatial, m_w_spatial),
        small_layouts(v_norm_g, v_final_norm_g, v_sgu_ln_g, v_sgu_ln_b, v_b_spatial, v_w_spatial),
        loss_parts, "adam_small")
    small_shapes = [norm_g.shape, final_norm_g.shape, sgu_ln_g.shape, sgu_ln_b.shape, b_spatial.shape, w_spatial.shape]
    sm = [[a.reshape(shp) for a, shp in zip(kind, small_shapes)] for kind in sm]

    res = _adam_shard(cps[0], land2[0], chip, *[jnp.transpose(a[0]) for a in (w_in, m_w_in, v_w_in)], "adam0")
    big = [[jnp.transpose(r)[None] for r in res]]
    for i, (w, m, v) in enumerate([(w_up_a, m_w_up_a, v_w_up_a), (w_up_b, m_w_up_b, v_w_up_b),
                                   (w_out, m_w_out, v_w_out)], start=1):
        res = _adam_shard(cps[i], land2[i], chip, w[0], m[0], v[0], "adam%d" % i)
        big.append([r[None] for r in res])

    loss = loss_sum[0, 0]

    def per_kind(kd):
        return [sm[kd][0], big[0][kd], sm[kd][2], sm[kd][3], sm[kd][5], sm[kd][4], big[1][kd], big[2][kd], big[3][kd],
                sm[kd][1]]

    return (loss, grad_x[None], *per_kind(0), *per_kind(1), *per_kind(2), *per_kind(3))
```

```python
import functools
import math

import jax
import jax.numpy as jnp
from jax import lax
from jax.experimental import pallas as pl
from jax.experimental.pallas import tpu as pltpu

F32 = jnp.float32
BF16 = jnp.bfloat16
MESH = pl.DeviceIdType.MESH

N_DEV = 8
N_HEADS = 8
HEAD_DIM = 64
D_SB = N_HEADS * HEAD_DIM
N_GROUPS = 8
GROUP_DIM = 64
D_SGU = N_GROUPS * GROUP_DIM
SGU_CHUNK = 128
CHUNK = 64
EPS = 1e-6
LANES = 128
N_PAIRS = N_HEADS // 2
QKV_COLS = 3 * D_SB
ATT_BLOCK = 256
ATT_TILE = 128
CARRY_FLOOR = -90.0
R_UNREACHED = -1e30

ADAM_LR = 0.001
ADAM_B1 = 0.9
ADAM_B2 = 0.999
ADAM_EPS = 1e-08
ADAM_WD = 0.01
ADAM_STEP = 10

VMEM_LIMIT = 56 * 1024 * 1024


def _cparams(sem=None, vmem=VMEM_LIMIT):
    return pltpu.CompilerParams(dimension_semantics=sem, vmem_limit_bytes=vmem)


def _dot(a, b):
    return jnp.dot(a, b, preferred_element_type=F32)


def _dot_nt(a, b):
    return lax.dot_general(a, b, (((1,), (1,)), ((), ())), preferred_element_type=F32)


def _dot_tn(a, b):
    return lax.dot_general(a, b, (((0,), (0,)), ((), ())), preferred_element_type=F32)


def _split_hi_lo(a):
    hi = a.astype(BF16)
    lo = (a - hi.astype(F32)).astype(BF16)
    return hi, lo


def _sigmoid(x):
    return 1.0 / (1.0 + jnp.exp(-x))


_GELU_C = math.sqrt(2.0 / math.pi)


def _gelu_and_grad(x):
    x2 = x * x
    inner = _GELU_C * (x + 0.044715 * (x2 * x))
    t = jnp.tanh(inner)
    cdf = 0.5 * (1.0 + t)
    g = x * cdf
    dg = cdf + x * (0.5 * (1.0 - t * t)) * (_GELU_C * (1.0 + 3.0 * 0.044715 * x2))
    return g, dg


def _coords():
    return lax.axis_index("x"), lax.axis_index("y"), lax.axis_index("c")


def _dev_index(px, py, pc):
    return 4 * px + 2 * py + pc


def _allgather(blocks, name):
    n = len(blocks)

    def body(*refs):
        gather = _Gather(refs[:n], refs[n:2 * n], *refs[2 * n:])
        gather.issue()
        gather.finish()

    any_spec = pl.BlockSpec(memory_space=pl.ANY)
    return pl.pallas_call(
        body, name=name,
        out_shape=_gather_out_shapes(blocks),
        in_specs=[any_spec] * n, out_specs=[any_spec] * n,
        scratch_shapes=_gather_semaphores(n),
    )(*blocks)


def _gather_out_shapes(blocks):
    return [jax.ShapeDtypeStruct((N_DEV,) + b.shape, b.dtype) for b in blocks]


def _gather_semaphores(n):
    return [pltpu.SemaphoreType.DMA((n, 7)), pltpu.SemaphoreType.DMA((n, 7)), pltpu.SemaphoreType.DMA((n,))]


class _Gather:
    def __init__(self, ins, outs, send_sems, recv_sems, local_sems):
        self.ins, self.outs = ins, outs
        self.send_sems, self.recv_sems, self.local_sems = send_sems, recv_sems, local_sems
        self.n = len(ins)
        x, y, c = _coords()
        self.c = c
        self.me, self.sibling = (x, y, c), (x, y, 1 - c)
        self.chips = [(1 - x, y), (x, 1 - y), (1 - x, 1 - y)]

    def _copy(self, a, k, block, to, src=None):
        dst = self.outs[a].at[_dev_index(*block)]
        return pltpu.make_async_remote_copy(
            src_ref=dst if src is None else src, dst_ref=dst,
            send_sem=self.send_sems.at[a, k], recv_sem=self.recv_sems.at[a, k],
            device_id=to, device_id_type=MESH)

    def _mine(self):
        return [pltpu.make_async_copy(self.ins[a], self.outs[a].at[_dev_index(*self.me)], self.local_sems.at[a])
                for a in range(self.n)]

    def _first(self):
        first = []
        for a in range(self.n):
            first.append(self._copy(a, 0, self.me, self.sibling, src=self.ins[a]))
            first += [self._copy(a, 1 + j, self.me, (*chip, self.c), src=self.ins[a])
                      for j, chip in enumerate(self.chips)]
        return first

    def issue(self):
        for cp in self._mine() + self._first():
            cp.start()

    def finish(self):
        c = self.c
        passed = []
        for j, chip in enumerate(self.chips):
            for a in range(self.n):
                self._copy(a, 1 + j, (*chip, c), self.me).wait_recv()
                fwd = self._copy(a, 4 + j, (*chip, c), self.sibling)
                fwd.start()
                passed.append(fwd)
        for a in range(self.n):
            self._copy(a, 0, self.sibling, self.me).wait_recv()
            for j, chip in enumerate(self.chips):
                self._copy(a, 4 + j, (*chip, 1 - c), self.me).wait_recv()
        for cp in self._first() + passed:
            cp.wait_send()
        for cp in self._mine():
            cp.wait()


def _push_sibling(arrs, name):
    n = len(arrs)

    def body(*refs):
        ins, outs = refs[:n], refs[n:2 * n]
        send_sems, recv_sems = refs[2 * n:]
        x, y, c = _coords()
        sibling = (x, y, 1 - c)
        copies = []
        for a in range(n):
            for k in range(4):
                copies.append(pltpu.make_async_remote_copy(
                    src_ref=ins[a].at[k, 1 - c], dst_ref=outs[a].at[k],
                    send_sem=send_sems.at[a, k], recv_sem=recv_sems.at[a, k],
                    device_id=sibling, device_id_type=MESH))
        for cp in copies:
            cp.start()
        for cp in copies:
            cp.wait()

    any_spec = pl.BlockSpec(memory_space=pl.ANY)
    return pl.pallas_call(
        body, name=name,
        out_shape=[jax.ShapeDtypeStruct((4,) + a.shape[2:], a.dtype) for a in arrs],
        in_specs=[any_spec] * n, out_specs=[any_spec] * n,
        scratch_shapes=[pltpu.SemaphoreType.DMA((n, 4)), pltpu.SemaphoreType.DMA((n, 4))],
    )(*arrs)


def _chip_push_copies(ins, outs, send_sems, recv_sems):
    x, y, c = _coords()
    chips = [(1 - x, y), (x, 1 - y), (1 - x, 1 - y)]
    return [pltpu.make_async_remote_copy(
        src_ref=ins[a].at[2 * px + py], dst_ref=outs[a].at[r],
        send_sem=send_sems.at[a, r], recv_sem=recv_sems.at[a, r],
        device_id=(px, py, c), device_id_type=MESH)
        for a in range(len(ins)) for r, (px, py) in enumerate(chips)]


def _chip_partial_sum(own, land, core, name):
    _, _, rows, cols = own.shape
    tr = rows

    def body(core_ref, own_ref, land_ref, out_ref):
        del core_ref
        out_ref[...] = (own_ref[...].astype(F32) + land_ref[...].astype(F32)).astype(out_ref.dtype)

    return pl.pallas_call(
        body, name=name,
        out_shape=jax.ShapeDtypeStruct((4, rows, cols), own.dtype),
        grid_spec=pltpu.PrefetchScalarGridSpec(
            num_scalar_prefetch=1, grid=(4, rows // tr),
            in_specs=[pl.BlockSpec((None, None, tr, cols), lambda k, r, core: (k, core[0], r, 0)),
                      pl.BlockSpec((None, tr, cols), lambda k, r, core: (k, r, 0))],
            out_specs=pl.BlockSpec((None, tr, cols), lambda k, r, core: (k, r, 0))),
        compiler_params=_cparams(("parallel", "parallel")),
    )(core, own, land)


def _adamw_math(w, g, m, v):
    m = ADAM_B1 * m + (1.0 - ADAM_B1) * g
    v = ADAM_B2 * v + (1.0 - ADAM_B2) * (g * g)
    m_hat = m / (1.0 - ADAM_B1 ** ADAM_STEP)
    v_hat = v / (1.0 - ADAM_B2 ** ADAM_STEP)
    delta = -ADAM_LR * (m_hat / (jnp.sqrt(v_hat) + ADAM_EPS) + ADAM_WD * w)
    return delta, m, v


def _adam_shard(cp, land, chip, w, m, v, name):
    rows, cols = w.shape
    tr = rows // 4

    def body(chip_ref, cp_ref, land_ref, w_ref, m_ref, v_ref, g_out, d_out, m_out, v_out):
        del chip_ref
        g = cp_ref[...].astype(F32)
        for r in range(3):
            g = g + land_ref[r].astype(F32)
        delta, m_new, v_new = _adamw_math(w_ref[...], g, m_ref[...], v_ref[...])
        g_out[...] = g
        d_out[...] = delta
        m_out[...] = m_new
        v_out[...] = v_new

    tile = pl.BlockSpec((tr, cols), lambda r, chip: (r, 0))
    out = jax.ShapeDtypeStruct((rows, cols), F32)
    return pl.pallas_call(
        body, name=name, out_shape=[out] * 4,
        grid_spec=pltpu.PrefetchScalarGridSpec(
            num_scalar_prefetch=1, grid=(rows // tr,),
            in_specs=[pl.BlockSpec((None, tr, cols), lambda r, chip: (chip[0], r, 0)),
                      pl.BlockSpec((3, tr, cols), lambda r, chip: (0, r, 0)),
                      tile, tile, tile],
            out_specs=[tile] * 4),
        compiler_params=_cparams(("parallel",)),
    )(chip, cp, land, w, m, v)


def _adam_small(parts, ws, ms, vs, loss_parts, name):
    n = len(ws)

    def body(*refs):
        p_refs, w_refs, m_refs, v_refs = refs[:n], refs[n:2 * n], refs[2 * n:3 * n], refs[3 * n:4 * n]
        loss_ref, outs, loss_out = refs[4 * n], refs[4 * n + 1:-1], refs[-1]
        total = loss_ref[0]
        for dev in range(1, N_DEV):
            total = total + loss_ref[dev]
        loss_out[...] = total
        for i in range(n):
            g = p_refs[i][0]
            for dev in range(1, N_DEV):
                g = g + p_refs[i][dev]
            if g.shape[0] != w_refs[i].shape[0]:
                g = jnp.sum(g, axis=0, keepdims=True)
            delta, m_new, v_new = _adamw_math(w_refs[i][...], g, m_refs[i][...], v_refs[i][...])
            outs[i][...] = g
            outs[n + i][...] = delta
            outs[2 * n + i][...] = m_new
            outs[3 * n + i][...] = v_new

    out_shapes = [jax.ShapeDtypeStruct(w.shape, F32) for w in ws] * 4
    out_shapes.append(jax.ShapeDtypeStruct(loss_parts.shape[1:], F32))
    res = pl.pallas_call(body, name=name, out_shape=out_shapes, compiler_params=_cparams())(
        *parts, *ws, *ms, *vs, loss_parts)
    return [res[k * n:(k + 1) * n] for k in range(4)], res[-1]


def _rmsnorm_fwd(x, g, riders, name):
    s, d = x.shape
    ts = 512
    nt = s // ts
    nr = len(riders)

    def body(x_ref, g_ref, *rest):
        rider_in, (h_ref, ht_ref) = rest[:nr], rest[nr:nr + 2]
        rider_out, sems = rest[nr + 2:2 * nr + 2], rest[2 * nr + 2:]
        i = pl.program_id(0)
        gather = _Gather(rider_in, rider_out, *sems)

        @pl.when(i == 0)
        def _():
            gather.issue()

        xv = x_ref[...]
        rstd = lax.rsqrt(jnp.mean(xv * xv, axis=-1, keepdims=True) + EPS)
        h = xv * rstd * g_ref[...]
        h_ref[...] = h.astype(BF16)
        ht_ref[...] = h.T.astype(BF16)

        @pl.when(i == nt - 1)
        def _():
            gather.finish()

    any_spec = pl.BlockSpec(memory_space=pl.ANY)
    res = pl.pallas_call(
        body, name=name,
        out_shape=[jax.ShapeDtypeStruct((s, d), BF16), jax.ShapeDtypeStruct((d, s), BF16)]
        + _gather_out_shapes(riders),
        grid=(nt,),
        in_specs=[pl.BlockSpec((ts, d), lambda i: (i, 0)), pl.BlockSpec((1, d), lambda i: (0, 0))] + [any_spec] * nr,
        out_specs=[pl.BlockSpec((ts, d), lambda i: (i, 0)), pl.BlockSpec((d, ts), lambda i: (0, i))]
        + [any_spec] * nr,
        scratch_shapes=_gather_semaphores(nr),
        compiler_params=_cparams(("arbitrary",)),
    )(x, g, *riders)
    return res[0], res[1], res[2:]


def _inproj(h, wt, col0, ncols, out_dtype, riders, name):
    s, d = h.shape
    tn = 256
    j0 = col0 // tn
    nj = ncols // tn
    nr = len(riders)

    def body(h_ref, w_ref, *rest):
        rider_in, o_ref, rider_out, sems = rest[:nr], rest[nr], rest[nr + 1:2 * nr + 1], rest[2 * nr + 1:]
        j = pl.program_id(0)
        if nr:
            gather = _Gather(rider_in, rider_out, *sems)

            @pl.when(j == 0)
            def _():
                gather.issue()

        o_ref[...] = _dot_nt(h_ref[...], w_ref[...]).astype(o_ref.dtype)

        if nr:
            @pl.when(j == nj - 1)
            def _():
                gather.finish()

    any_spec = pl.BlockSpec(memory_space=pl.ANY)
    res = pl.pallas_call(
        body, name=name, out_shape=[jax.ShapeDtypeStruct((s, ncols), out_dtype)] + _gather_out_shapes(riders),
        grid=(nj,),
        in_specs=[pl.BlockSpec((s, d), lambda j: (0, 0)), pl.BlockSpec((tn, d), lambda j: (j + j0, 0))]
        + [any_spec] * nr,
        out_specs=[pl.BlockSpec((s, tn), lambda j: (0, j))] + [any_spec] * nr,
        scratch_shapes=_gather_semaphores(nr) if nr else [],
        compiler_params=_cparams(("arbitrary",) if nr else ("parallel",)),
    )(h, wt, *riders)
    return res[0], res[1:]


def _dw_in(ht, dproj, smalls, cps, name):
    d, s = ht.shape
    n = dproj.shape[1]
    tn = 512
    nj = n // tn
    ns, nc = len(smalls), len(cps)

    def body(a_ref, b_ref, *rest):
        small_in, cp_in = rest[:ns], rest[ns:ns + nc]
        o_ref = rest[ns + nc]
        small_out, land_out = rest[ns + nc + 1:2 * ns + nc + 1], rest[2 * ns + nc + 1:2 * (ns + nc) + 1]
        sems = rest[2 * (ns + nc) + 1:]
        j = pl.program_id(0)
        gather = _Gather(small_in, small_out, *sems[:3])

        @pl.when(j == 0)
        def _():
            gather.issue()
            for cp in _chip_push_copies(cp_in, land_out, *sems[3:]):
                cp.start()

        o_ref[...] = _dot(a_ref[...], b_ref[...]).T.astype(o_ref.dtype)

        @pl.when(j == nj - 1)
        def _():
            gather.finish()
            for cp in _chip_push_copies(cp_in, land_out, *sems[3:]):
                cp.wait()

    any_spec = pl.BlockSpec(memory_space=pl.ANY)
    res = pl.pallas_call(
        body, name=name,
        out_shape=[jax.ShapeDtypeStruct((n, d), BF16)] + _gather_out_shapes(smalls)
        + [jax.ShapeDtypeStruct((3,) + a.shape[1:], a.dtype) for a in cps],
        grid=(nj,),
        in_specs=[pl.BlockSpec((d, s), lambda j: (0, 0)), pl.BlockSpec((s, tn), lambda j: (0, j))]
        + [any_spec] * (ns + nc),
        out_specs=[pl.BlockSpec((tn, d), lambda j: (j, 0))] + [any_spec] * (ns + nc),
        scratch_shapes=_gather_semaphores(ns) + [pltpu.SemaphoreType.DMA((nc, 3)), pltpu.SemaphoreType.DMA((nc, 3))],
        compiler_params=_cparams(("arbitrary",)),
    )(ht, dproj, *smalls, *cps)
    return res[0], res[1:1 + ns], res[1 + ns:]


def _dh_and_grad_x(dproj, wt, x, g, dx2, cps, name):
    s, n = dproj.shape
    d = wt.shape[1]
    tm, tk = min(s, 1024), 512
    nk = n // tk
    nm = s // tm
    nc = len(cps)

    def body(a_ref, w_ref, x_ref, g_ref, dx2_ref, *rest):
        cp_refs, gx_ref, dg_ref = rest[:nc], rest[nc], rest[nc + 1]
        land_refs = rest[nc + 2:2 * nc + 2]
        acc_ref, send_sems, recv_sems = rest[2 * nc + 2:]
        i, k = pl.program_id(0), pl.program_id(1)

        @pl.when((i == 0) & (k == 0))
        def _():
            for cp in _chip_push_copies(cp_refs, land_refs, send_sems, recv_sems):
                cp.start()
            dg_ref[...] = jnp.zeros_like(dg_ref)

        @pl.when(k == 0)
        def _():
            acc_ref[...] = jnp.zeros_like(acc_ref)

        acc_ref[...] += _dot(a_ref[...], w_ref[...])

        @pl.when(k == nk - 1)
        def _():
            dh = acc_ref[...]
            xv = x_ref[...]
            rstd = lax.rsqrt(jnp.mean(xv * xv, axis=-1, keepdims=True) + EPS)
            xhat = xv * rstd
            dg_ref[...] += jnp.sum((dh * xhat).reshape(tm // 8, 8, d), axis=0)
            dxh = dh * g_ref[...]
            gx_ref[...] = dx2_ref[...] + rstd * (dxh - xhat * jnp.mean(dxh * xhat, axis=-1, keepdims=True))

        @pl.when((i == nm - 1) & (k == nk - 1))
        def _():
            for cp in _chip_push_copies(cp_refs, land_refs, send_sems, recv_sems):
                cp.wait()

    any_spec = pl.BlockSpec(memory_space=pl.ANY)
    res = pl.pallas_call(
        body, name=name,
        out_shape=[jax.ShapeDtypeStruct((s, d), F32), jax.ShapeDtypeStruct((8, d), F32)]
        + [jax.ShapeDtypeStruct((3,) + a.shape[1:], a.dtype) for a in cps],
        grid=(nm, nk),
        in_specs=[pl.BlockSpec((tm, tk), lambda i, k: (i, k)), pl.BlockSpec((tk, d), lambda i, k: (k, 0)),
                  pl.BlockSpec((tm, d), lambda i, k: (i, 0)), pl.BlockSpec((1, d), lambda i, k: (0, 0)),
                  pl.BlockSpec((tm, d), lambda i, k: (i, 0))] + [any_spec] * nc,
        out_specs=[pl.BlockSpec((tm, d), lambda i, k: (i, 0)), pl.BlockSpec((8, d), lambda i, k: (0, 0))]
        + [any_spec] * nc,
        scratch_shapes=[pltpu.VMEM((tm, d), F32), pltpu.SemaphoreType.DMA((nc, 3)), pltpu.SemaphoreType.DMA((nc, 3))],
        compiler_params=_cparams(("arbitrary", "arbitrary")),
    )(dproj, wt, x, g, dx2, *cps)
    return res[0], res[1], res[2:]


def _log_sigmoids(z):
    l1p = jnp.log(1.0 + jnp.exp(-jnp.abs(z)))
    ls = jnp.minimum(z, 0.0) - l1p
    return ls, ls - z


def _strict_lower_ones(n):
    row = lax.broadcasted_iota(jnp.int32, (n, n), 0)
    col = lax.broadcasted_iota(jnp.int32, (n, n), 1)
    return row, col, (row > col).astype(BF16)


def _attn_fwd(qkv, name):
    s = qkv.shape[0]
    tb = ATT_BLOCK
    nq = s // tb
    per_block = tb // ATT_TILE
    assert s // ATT_TILE <= LANES, "one lane of the saved carries per key tile"

    def body(q_ref, k_ref, v_ref, o_ref, rs_ref, acc_ref, r_ref, rsv_ref):
        i = pl.program_id(1)
        lane = lax.broadcasted_iota(jnp.int32, (tb, LANES), 1)
        hmask = [lane < HEAD_DIM, lane >= HEAD_DIM]
        q2 = q_ref[...]
        qm = [jnp.where(m, q2, jnp.zeros((), BF16)) * jnp.asarray(HEAD_DIM ** -0.5, BF16) for m in hmask]
        row, col, tri = _strict_lower_ones(tb)
        tri_t = tri[0:ATT_TILE, 0:ATT_TILE]
        tri2 = {tb: jnp.concatenate([tri, tri], axis=0), ATT_TILE: jnp.concatenate([tri_t, tri_t], axis=0)}
        acc_ref[...] = jnp.zeros_like(acc_ref)
        r_ref[...] = jnp.zeros_like(r_ref)
        rsv_ref[...] = jnp.full_like(rsv_ref, R_UNREACHED)

        def sweep(segs):
            heads, nseg = range(2), range(len(segs))
            tks = [tb if diag else ATT_TILE for _, diag in segs]
            kv = []
            for (t, _), tk in zip(segs, tks):
                rows = pl.ds(pl.multiple_of(t * tk, tk), tk)
                kv.append((k_ref[rows, :], v_ref[rows, :]))
            z = [[_dot_nt(qm[h], kv[n][0]) for h in heads] for n in nseg]
            ls, lk = [], []
            for n in nseg:
                pairs = [_log_sigmoids(z[n][h]) for h in heads]
                ls.append([p[0] for p in pairs])
                lk.append([jnp.where(col < row, p[1], 0.0) if segs[n][1] else p[1] for p in pairs])
            r, cur = [], [r_ref[h] for h in heads]
            for n in nseg:
                r.append(cur)
                cur = [cur[h] + jnp.sum(lk[n][h], axis=1, keepdims=True) for h in heads]
            suffix = [[_dot(jnp.concatenate(_split_hi_lo(lk[n][h]), axis=1), tri2[tks[n]]) for h in heads]
                      for n in nseg]
            w = []
            for n in nseg:
                w_n = [jnp.exp(ls[n][h] + suffix[n][h] + jnp.tile(r[n][h], (1, tks[n] // LANES))) for h in heads]
                if segs[n][1]:
                    w_n = [jnp.where(col < row, w_h, 0.0) for w_h in w_n]
                w.append([w_h.astype(BF16) for w_h in w_n])
            pv = [[_dot(w[n][h], kv[n][1]) for h in heads] for n in nseg]
            for h in heads:
                acc_ref[h] += functools.reduce(lambda a, b: a + b, [pv[n][h] for n in nseg])
                for n in nseg:
                    if not segs[n][1]:
                        rsv_ref[h] = jnp.where(lane == segs[n][0], r[n][h], rsv_ref[h])
                r_ref[h] = cur[h]

        n_tiles = per_block * i

        @pl.when(i == 0)
        def _():
            sweep([(i, True)])

        @pl.when(i > 0)
        def _():
            sweep([(i, True), (n_tiles - 1, False)])

        def live(c):
            jj, rmax = c
            return (jj < n_tiles) & (rmax >= CARRY_FLOOR)

        def step(c):
            jj, _ = c
            sweep([(n_tiles - 1 - jj, False)])
            return jj + 1, jnp.max(r_ref[...])

        lax.while_loop(live, step, (jnp.int32(1), jnp.max(r_ref[...])))
        o_ref[...] = jnp.where(hmask[0], acc_ref[0], acc_ref[1])
        rs_ref[...] = rsv_ref[...]

    return pl.pallas_call(
        body, name=name,
        out_shape=[jax.ShapeDtypeStruct((s, D_SB), F32), jax.ShapeDtypeStruct((N_HEADS, s, LANES), F32)],
        grid=(N_PAIRS, nq),
        in_specs=[pl.BlockSpec((tb, LANES), lambda p, i: (i, p)),
                  pl.BlockSpec((s, LANES), lambda p, i: (0, N_PAIRS + p)),
                  pl.BlockSpec((s, LANES), lambda p, i: (0, 2 * N_PAIRS + p))],
        out_specs=[pl.BlockSpec((tb, LANES), lambda p, i: (i, p)),
                   pl.BlockSpec((2, tb, LANES), lambda p, i: (p, i, 0))],
        scratch_shapes=[pltpu.VMEM((2, tb, LANES), F32), pltpu.VMEM((2, tb, LANES), F32),
                        pltpu.VMEM((2, tb, LANES), F32)],
        compiler_params=_cparams(("parallel", "arbitrary")),
    )(qkv, qkv, qkv)


def _attn_bwd(qkv, do, rs, dproj, name):
    s = qkv.shape[0]
    tb = ATT_BLOCK
    nq = s // tb
    per_block = tb // ATT_TILE
    scale = HEAD_DIM ** -0.5

    def body(q_ref, k_ref, v_ref, do_ref, rs_ref, dproj_hbm, out_hbm, dq_acc, dk_acc, dv_acc, dqi_ref, pc_ref,
             stage_ref, out_sems):
        del dproj_hbm
        pair = pl.program_id(0)
        lane = lax.broadcasted_iota(jnp.int32, (tb, LANES), 1)
        hmask = [lane < HEAD_DIM, lane >= HEAD_DIM]
        row, col, tri = _strict_lower_ones(tb)
        tri_t = tri[0:ATT_TILE, 0:ATT_TILE]
        tri2 = {tb: jnp.concatenate([tri, tri], axis=0), ATT_TILE: jnp.concatenate([tri_t, tri_t], axis=0)}
        triu_b = (row < col).astype(BF16)
        triu = {tb: triu_b, ATT_TILE: triu_b[0:ATT_TILE, 0:ATT_TILE]}
        zero = jnp.zeros((), BF16)
        dq_acc[...] = jnp.zeros_like(dq_acc)
        dk_acc[...] = jnp.zeros_like(dk_acc)
        dv_acc[...] = jnp.zeros_like(dv_acc)

        def qblock(i, carry):
            qrows = pl.ds(pl.multiple_of(i * tb, tb), tb)
            q2 = q_ref[qrows, :]
            do2 = do_ref[qrows, :]
            qm = [jnp.where(m, q2, zero) * jnp.asarray(scale, BF16) for m in hmask]
            dom = [jnp.where(m, do2, zero) for m in hmask]
            rs_i = [rs_ref[h, qrows, :] for h in range(2)]
            dqi_ref[...] = jnp.zeros_like(dqi_ref)
            pc_ref[...] = jnp.zeros_like(pc_ref)
            reached = jnp.maximum(jnp.max(rs_i[0], axis=0, keepdims=True), jnp.max(rs_i[1], axis=0, keepdims=True))
            n_tiles = per_block * i
            key_tile = lax.broadcasted_iota(jnp.int32, (1, LANES), 1)
            n_reached = jnp.sum(((reached >= CARRY_FLOOR) & (key_tile < n_tiles)).astype(jnp.int32))

            def sweep(segs):
                heads, nseg = range(2), range(len(segs))
                tks = [tb if diag else ATT_TILE for _, diag in segs]
                krows = [pl.ds(pl.multiple_of(t * tk, tk), tk) for (t, _), tk in zip(segs, tks)]
                k2 = [k_ref[rows, :] for rows in krows]
                v2 = [v_ref[rows, :] for rows in krows]
                z = [[_dot_nt(qm[h], k2[n]) for h in heads] for n in nseg]
                dw = [[_dot_nt(dom[h], v2[n]) for h in heads] for n in nseg]
                ls, lk = [], []
                for n in nseg:
                    pairs = [_log_sigmoids(z[n][h]) for h in heads]
                    ls.append([p[0] for p in pairs])
                    lk.append([jnp.where(col < row, p[1], 0.0) if segs[n][1] else p[1] for p in pairs])
                suffix = [[_dot(jnp.concatenate(_split_hi_lo(lk[n][h]), axis=1), tri2[tks[n]]) for h in heads]
                          for n in nseg]
                w, g = [], []
                for n in nseg:
                    w_n = []
                    for h in heads:
                        logw = ls[n][h] + suffix[n][h]
                        if segs[n][1]:
                            w_n.append(jnp.where(col < row, jnp.exp(logw), 0.0))
                        else:
                            carry_in = jnp.sum(jnp.where(lane == segs[n][0], rs_i[h], 0.0), axis=1, keepdims=True)
                            w_n.append(jnp.exp(logw + carry_in))
                    w.append([w_h.astype(BF16) for w_h in w_n])
                    g.append([w_n[h] * dw[n][h] for h in heads])
                prefix = [[_dot(g[n][h].astype(BF16), triu[tks[n]]) for h in heads] for n in nseg]
                pc, cur = [], [pc_ref[h] for h in heads]
                for n in nseg:
                    pc.append(cur)
                    cur = [cur[h] + jnp.sum(g[n][h], axis=1, keepdims=True) for h in heads]
                dzb = []
                for n in nseg:
                    dz_n = []
                    for h in heads:
                        sig = jnp.exp(ls[n][h])
                        dz = g[n][h] - sig * (g[n][h] + prefix[n][h] + jnp.tile(pc[n][h], (1, tks[n] // LANES)))
                        if segs[n][1]:
                            dz = jnp.where(col < row, dz, 0.0)
                        dz_n.append(dz.astype(BF16))
                    dzb.append(dz_n)
                for h in heads:
                    pc_ref[h] = cur[h]
                dq = [[_dot(dzb[n][h], jnp.where(hmask[h][0:tks[n]], k2[n], zero)) for h in heads] for n in nseg]
                dk = [[_dot_tn(dzb[n][h], qm[h]) for h in heads] for n in nseg]
                dv = [[_dot_tn(w[n][h], dom[h]) for h in heads] for n in nseg]
                dqi_ref[...] += functools.reduce(lambda a, b: a + b, [dq[n][h] for n in nseg for h in heads])
                for n in nseg:
                    dk_acc[krows[n], :] += dk[n][0] + dk[n][1]
                    dv_acc[krows[n], :] += dv[n][0] + dv[n][1]

            def step(t, c):
                sweep([(t, False)])
                return c

            lax.fori_loop(n_tiles - n_reached, n_tiles - 1, step, 0)

            @pl.when(i == 0)
            def _():
                sweep([(i, True)])

            @pl.when(i > 0)
            def _():
                sweep([(n_tiles - 1, False), (i, True)])

            dq_acc[qrows, :] += dqi_ref[...] * scale
            return carry

        lax.fori_loop(0, nq, qblock, 0)
        copies = []
        for t, acc in enumerate((dq_acc, dk_acc, dv_acc)):
            stage_ref[t] = acc[...].astype(BF16)
            col0 = pl.multiple_of(t * D_SB + pair * LANES, LANES)
            copies.append(pltpu.make_async_copy(stage_ref.at[t], out_hbm.at[:, pl.ds(col0, LANES)], out_sems.at[t]))
        for cp in copies:
            cp.start()
        for cp in copies:
            cp.wait()

    return pl.pallas_call(
        body, name=name,
        out_shape=jax.ShapeDtypeStruct(dproj.shape, BF16),
        grid=(N_PAIRS,),
        in_specs=[pl.BlockSpec((s, LANES), lambda p: (0, p)),
                  pl.BlockSpec((s, LANES), lambda p: (0, N_PAIRS + p)),
                  pl.BlockSpec((s, LANES), lambda p: (0, 2 * N_PAIRS + p)),
                  pl.BlockSpec((s, LANES), lambda p: (0, p)),
                  pl.BlockSpec((2, s, LANES), lambda p: (p, 0, 0)),
                  pl.BlockSpec(memory_space=pl.ANY)],
        out_specs=pl.BlockSpec(memory_space=pl.ANY),
        scratch_shapes=[pltpu.VMEM((s, LANES), F32), pltpu.VMEM((s, LANES), F32), pltpu.VMEM((s, LANES), F32),
                        pltpu.VMEM((tb, LANES), F32), pltpu.VMEM((2, tb, LANES), F32),
                        pltpu.VMEM((3, s, LANES), BF16), pltpu.SemaphoreType.DMA((3,))],
        input_output_aliases={5: 0},
        compiler_params=_cparams(("arbitrary",)),
    )(qkv, qkv, qkv, do, rs, dproj)


def _mid(o, rest, x, tgt, wua, wub, wout, fg, lng, lnb, wsp, bfull, gavg, name):
    s, d = x.shape
    ts = 256
    nt = s // ts
    nchunk = ts // SGU_CHUNK
    n_rest = rest.shape[1]

    def body(o_ref, rest_ref, x_ref, t_ref, wua_ref, wub_ref, wout_ref, fg_ref, lng_ref, lnb_ref, wsp_ref, bfull_ref,
             gavg_ref, loss_ref, dx2_ref, do_ref, dproj_ref, dwua_ref, dwub_ref, dwout_ref, dfg_ref, dlng_ref,
             dlnb_ref, dwsp_ref, dbfull_ref):
        step = pl.program_id(0)

        @pl.when(step == 0)
        def _():
            for ref in (loss_ref, dwua_ref, dwub_ref, dwout_ref, dfg_ref, dlng_ref, dlnb_ref, dwsp_ref, dbfull_ref):
                ref[...] = jnp.zeros_like(ref)

        gavg = gavg_ref[...]

        def gmean(a):
            return _dot(a.astype(BF16), gavg)

        def colsum8(a):
            return jnp.sum(a.reshape(ts // 8, 8, a.shape[1]), axis=0)

        z_a = rest_ref[:, 0:512]
        u_b = rest_ref[:, 512:1024]
        v_b = rest_ref[:, 1024:1536]
        z_b = rest_ref[:, 1536:2048]
        g_a = rest_ref[:, 2048:2048 + d]
        g_b = rest_ref[:, 2048 + d:2048 + 2 * d]
        ov = o_ref[...]
        sa = _sigmoid(z_a)
        silu_a = z_a * sa
        y_a = ov * silu_a
        ug, dug_du = _gelu_and_grad(u_b)
        vg, dvg_dv = _gelu_and_grad(v_b)
        mu = gmean(vg)
        cen = vg - mu
        rstd_g = lax.rsqrt(gmean(cen * cen) + EPS)
        vhat = cen * rstd_g
        vn = vhat * lng_ref[...] + lnb_ref[...]
        vnb = vn.astype(BF16)

        t_idx = lax.broadcasted_iota(jnp.int32, (SGU_CHUNK, SGU_CHUNK), 0)
        s_idx = lax.broadcasted_iota(jnp.int32, (SGU_CHUNK, SGU_CHUNK), 1)
        causal = (s_idx // CHUNK) <= (t_idx // CHUNK)
        wm = [jnp.where(causal, wsp_ref[g], 0.0) for g in range(N_GROUPS)]
        wmb = [w.astype(BF16) for w in wm]
        wmtb = [w.T.astype(BF16) for w in wm]
        lane = lax.broadcasted_iota(jnp.int32, (SGU_CHUNK, LANES), 1)
        first = lane < GROUP_DIM
        bfull = bfull_ref[...]

        mixed_rows = []
        for n in range(nchunk):
            r0, r1 = n * SGU_CHUNK, (n + 1) * SGU_CHUNK
            pieces = []
            for p in range(N_GROUPS // 2):
                blk = vnb[r0:r1, p * LANES:(p + 1) * LANES]
                pieces.append(jnp.where(first, _dot(wmb[2 * p], blk), _dot(wmb[2 * p + 1], blk)))
            mixed_rows.append(jnp.concatenate(pieces, axis=1) + bfull)
        mixed = jnp.concatenate(mixed_rows, axis=0)
        sg = ug * mixed
        sb = _sigmoid(z_b)
        silu_b = z_b * sb
        y_b = sg * silu_b
        y_ab = y_a.astype(BF16)
        y_bb = y_b.astype(BF16)
        p_a = _dot(y_ab, wua_ref[...])
        p_b = _dot(y_bb, wub_ref[...])
        ga_s = _sigmoid(g_a)
        gb_s = _sigmoid(g_b)
        merged_b = (ga_s * p_a + gb_s * p_b).astype(BF16)
        x2 = x_ref[...] + _dot(merged_b, wout_ref[...])
        rstd = lax.rsqrt(jnp.mean(x2 * x2, axis=-1, keepdims=True) + EPS)
        xhat = x2 * rstd
        fg_v = fg_ref[...]
        diff = xhat * fg_v - t_ref[...]
        loss_ref[...] += 0.5 * jnp.sum(jnp.sum(diff * diff, axis=-1, keepdims=True) * (1.0 / d))

        dy = diff * (1.0 / d)
        dfg_ref[...] += colsum8(dy * xhat)
        dxh = dy * fg_v
        dx2 = rstd * (dxh - xhat * jnp.mean(dxh * xhat, axis=-1, keepdims=True))
        dx2_ref[...] = dx2
        dx2b = dx2.astype(BF16)
        dwout_ref[...] += _dot_tn(merged_b, dx2b)
        dmerged = _dot_nt(dx2b, wout_ref[...])
        dp_a = dmerged * ga_s
        dp_b = dmerged * gb_s
        dproj_ref[:, QKV_COLS + 2048:QKV_COLS + 2048 + d] = (dmerged * p_a * (ga_s * (1.0 - ga_s))).astype(BF16)
        dproj_ref[:, QKV_COLS + 2048 + d:QKV_COLS + 2048 + 2 * d] = (dmerged * p_b * (gb_s * (1.0 - gb_s))).astype(BF16)
        dp_ab = dp_a.astype(BF16)
        dp_bb = dp_b.astype(BF16)
        dwua_ref[...] += _dot_tn(y_ab, dp_ab)
        dwub_ref[...] += _dot_tn(y_bb, dp_bb)
        dy_a = _dot_nt(dp_ab, wua_ref[...])
        dy_b = _dot_nt(dp_bb, wub_ref[...])
        do_ref[...] = (dy_a * silu_a).astype(BF16)
        dproj_ref[:, QKV_COLS:QKV_COLS + 512] = (dy_a * ov * (sa * (1.0 + z_a * (1.0 - sa)))).astype(BF16)
        dsg = dy_b * silu_b
        dproj_ref[:, QKV_COLS + 1536:QKV_COLS + 2048] = (dy_b * sg * (sb * (1.0 + z_b * (1.0 - sb)))).astype(BF16)
        dproj_ref[:, QKV_COLS + 512:QKV_COLS + 1024] = (dsg * mixed * dug_du).astype(BF16)
        dmixed = dsg * ug
        dmb = dmixed.astype(BF16)
        zero = jnp.zeros((), BF16)
        dvn_rows = []
        db = jnp.zeros((SGU_CHUNK, D_SGU), F32)
        for n in range(nchunk):
            r0, r1 = n * SGU_CHUNK, (n + 1) * SGU_CHUNK
            db = db + dmixed[r0:r1, :]
            pieces = []
            for p in range(N_GROUPS // 2):
                cols = slice(p * LANES, (p + 1) * LANES)
                dm_blk = dmb[r0:r1, cols]
                vn_blk = vnb[r0:r1, cols]
                dwsp_ref[2 * p] += _dot_nt(jnp.where(first, dm_blk, zero), vn_blk)
                dwsp_ref[2 * p + 1] += _dot_nt(jnp.where(first, zero, dm_blk), vn_blk)
                pieces.append(jnp.where(first, _dot(wmtb[2 * p], dm_blk), _dot(wmtb[2 * p + 1], dm_blk)))
            dvn_rows.append(jnp.concatenate(pieces, axis=1))
        dbfull_ref[...] += db
        dvn = jnp.concatenate(dvn_rows, axis=0)
        dlng_ref[...] += colsum8(dvn * vhat)
        dlnb_ref[...] += colsum8(dvn)
        dvhat = dvn * lng_ref[...]
        dcen = rstd_g * (dvhat - gmean(dvhat) - vhat * gmean(dvhat * vhat))
        dproj_ref[:, QKV_COLS + 1024:QKV_COLS + 1536] = (dcen * dvg_dv).astype(BF16)

        @pl.when(step == nt - 1)
        def _():
            for g in range(N_GROUPS):
                dwsp_ref[g] = jnp.where(causal, dwsp_ref[g], 0.0)

    def tile(cols):
        return pl.BlockSpec((ts, cols), lambda i: (i, 0))

    def whole(shape):
        return pl.BlockSpec(shape, lambda i: (0,) * len(shape))

    out_shapes = [
        jax.ShapeDtypeStruct((8, LANES), F32),
        jax.ShapeDtypeStruct((s, d), F32),
        jax.ShapeDtypeStruct((s, D_SB), BF16),
        jax.ShapeDtypeStruct((s, QKV_COLS + n_rest), BF16),
        jax.ShapeDtypeStruct((D_SB, d), F32),
        jax.ShapeDtypeStruct((D_SGU, d), F32),
        jax.ShapeDtypeStruct((d, d), F32),
        jax.ShapeDtypeStruct((8, d), F32),
        jax.ShapeDtypeStruct((8, D_SGU), F32),
        jax.ShapeDtypeStruct((8, D_SGU), F32),
        jax.ShapeDtypeStruct((N_GROUPS, SGU_CHUNK, SGU_CHUNK), F32),
        jax.ShapeDtypeStruct((SGU_CHUNK, D_SGU), F32),
    ]
    out_specs = [whole((8, LANES)), tile(d), tile(D_SB), tile(QKV_COLS + n_rest), whole((D_SB, d)), whole((D_SGU, d)),
                 whole((d, d)), whole((8, d)), whole((8, D_SGU)), whole((8, D_SGU)),
                 whole((N_GROUPS, SGU_CHUNK, SGU_CHUNK)), whole((SGU_CHUNK, D_SGU))]
    in_specs = [tile(D_SB), tile(n_rest), tile(d), tile(d), whole((D_SB, d)), whole((D_SGU, d)), whole((d, d)),
                whole((1, d)), whole((1, D_SGU)), whole((1, D_SGU)), whole((N_GROUPS, SGU_CHUNK, SGU_CHUNK)),
                whole((SGU_CHUNK, D_SGU)), whole((D_SGU, D_SGU))]
    return pl.pallas_call(
        body, name=name, out_shape=out_shapes, grid=(nt,), in_specs=in_specs, out_specs=out_specs,
        compiler_params=_cparams(("arbitrary",)),
    )(o, rest, x, tgt, wua, wub, wout, fg, lng, lnb, wsp, bfull, gavg)


def _small_reduce(dfg8, dlng8, dlnb8, dbfull, name):
    d = dfg8.shape[1]

    def body(dfg_ref, dlng_ref, dlnb_ref, dbfull_ref, fg_out, lng_out, lnb_out, b_out):
        fg_out[...] = jnp.sum(dfg_ref[...], axis=0, keepdims=True)
        lng_out[...] = jnp.sum(dlng_ref[...], axis=0, keepdims=True)
        lnb_out[...] = jnp.sum(dlnb_ref[...], axis=0, keepdims=True)
        grp = lax.broadcasted_iota(jnp.int32, (D_SGU, LANES), 0) // GROUP_DIM
        col = lax.broadcasted_iota(jnp.int32, (D_SGU, LANES), 1)
        sel = (grp == col).astype(BF16)
        a = dbfull_ref[...]
        hi, lo = _split_hi_lo(a)
        lo2 = (a - hi.astype(F32) - lo.astype(F32)).astype(BF16)
        by_pos = _dot(hi, sel) + _dot(lo, sel) + _dot(lo2, sel)
        b_out[...] = by_pos.T[0:N_GROUPS, :]

    return pl.pallas_call(
        body, name=name,
        out_shape=[jax.ShapeDtypeStruct((1, d), F32), jax.ShapeDtypeStruct((1, D_SGU), F32),
                   jax.ShapeDtypeStruct((1, D_SGU), F32), jax.ShapeDtypeStruct((N_GROUPS, SGU_CHUNK), F32)],
        compiler_params=_cparams(),
    )(dfg8, dlng8, dlnb8, dbfull)


def _block_major_cols(w):
    r, n = w.shape
    return jnp.transpose(w.reshape(r, N_DEV, n // N_DEV), (1, 0, 2))


def _from_block_major_cols(w):
    nb, r, c = w.shape
    return jnp.transpose(w, (1, 0, 2)).reshape(r, nb * c)


def kernel(x, norm_g, w_in, sgu_ln_g, sgu_ln_b, w_spatial, b_spatial, w_up_a, w_up_b, w_out, final_norm_g, loss_target, m_norm_g, m_w_in, m_sgu_ln_g, m_sgu_ln_b, m_w_spatial, m_b_spatial, m_w_up_a, m_w_up_b, m_w_out, m_final_norm_g, v_norm_g, v_w_in, v_sgu_ln_g, v_sgu_ln_b, v_w_spatial, v_b_spatial, v_w_up_a, v_w_up_b, v_w_out, v_final_norm_g):
    s, d = x.shape[1], x.shape[2]
    xs = x[0]
    tgt = loss_target[0]
    cx, cy, cc = _coords()
    core = jnp.reshape(cc, (1,)).astype(jnp.int32)
    chip = jnp.reshape(2 * cx + cy, (1,)).astype(jnp.int32)

    h, ht, (g_win,) = _rmsnorm_fwd(xs, norm_g, [jnp.transpose(w_in[0]).astype(BF16)], "norm")
    d_in = N_DEV * w_in.shape[2]
    w_full = g_win.reshape(d_in, d)
    qkv, _ = _inproj(h, w_full, 0, QKV_COLS, BF16, [], "inproj_qkv")
    rest, (g_wua, g_wub, g_wout) = _inproj(
        h, w_full, QKV_COLS, d_in - QKV_COLS, F32,
        [w_up_a[0].astype(BF16), w_up_b[0].astype(BF16), w_out[0].astype(BF16)], "inproj_rest")
    wua_full = _from_block_major_cols(g_wua)
    wub_full = _from_block_major_cols(g_wub)
    wout_full = g_wout.reshape(d, d)
    o, rs = _attn_fwd(qkv, "attn_fwd")

    lng = sgu_ln_g.reshape(1, D_SGU)
    lnb = sgu_ln_b.reshape(1, D_SGU)
    bfull = jnp.repeat(jnp.transpose(b_spatial[0]), GROUP_DIM, axis=1)
    grp = jnp.arange(D_SGU) // GROUP_DIM
    gavg = jnp.where(grp[:, None] == grp[None, :], 1.0 / GROUP_DIM, 0.0).astype(BF16)
    (loss_b, dx2, do, dproj, dwua, dwub, dwout, dfg8, dlng8, dlnb8, dwsp, dbfull) = _mid(
        o, rest, xs, tgt, wua_full, wub_full, wout_full, final_norm_g.reshape(1, d), lng, lnb, w_spatial[0], bfull,
        gavg, "mid")

    def blocks42(a):
        return a.reshape((4, 2) + a.shape[1:])

    def chip_partials(own, tag):
        land = _push_sibling(own, "rs_sibling_" + tag)
        return [_chip_partial_sum(a, l, core, "cpsum_%s%d" % (tag, i)) for i, (a, l) in enumerate(zip(own, land))]

    cps_up = chip_partials([blocks42(_block_major_cols(dwua.astype(BF16))),
                            blocks42(_block_major_cols(dwub.astype(BF16))),
                            blocks42(dwout.astype(BF16).reshape(N_DEV, d // N_DEV, d))], "up")

    dproj = _attn_bwd(qkv, do, rs, dproj, "attn_bwd")
    dfg, dlng, dlnb, db = _small_reduce(dfg8, dlng8, dlnb8, dbfull, "small_reduce")
    wsp_rows = N_GROUPS * SGU_CHUNK
    dwin, small_parts, land_up = _dw_in(
        ht, dproj, [dfg, dlng, dlnb, db, dwsp.reshape(wsp_rows, SGU_CHUNK)], cps_up, "dwin")

    cps_in = chip_partials([dwin.reshape(4, 2, d_in // N_DEV, d)], "in")
    grad_x, dng8, land_in = _dh_and_grad_x(dproj, w_full, xs, norm_g, dx2, cps_in, "dh")
    cps = cps_in + cps_up
    land2 = list(land_in) + list(land_up)

    ng_parts, loss_parts = _allgather([dng8, loss_b], "ag_tail")

    def small_layouts(ng, fg, lg, lb, bs, ws):
        return [ng.reshape(1, d), fg.reshape(1, d), lg.reshape(1, D_SGU), lb.reshape(1, D_SGU),
                bs.reshape(N_GROUPS, SGU_CHUNK), ws.reshape(wsp_rows, SGU_CHUNK)]

    sm, loss_sum = _adam_small(
        [ng_parts] + list(small_parts),
        small_layouts(norm_g, final_norm_g, sgu_ln_g, sgu_ln_b, b_spatial, w_spatial),
        small_layouts(m_norm_g, m_final_norm_g, m_sgu_ln_g, m_sgu_ln_b, m_b_spatial, m_w_spatial),
        small_layouts(v_norm_g, v_final_norm_g, v_sgu_ln_g, v_sgu_ln_b, v_b_spatial, v_w_spatial),
        loss_parts, "adam_small")
    small_shapes = [norm_g.shape, final_norm_g.shape, sgu_ln_g.shape, sgu_ln_b.shape, b_spatial.shape, w_spatial.shape]
    sm = [[a.reshape(shp) for a, shp in zip(kind, small_shapes)] for kind in sm]

    res = _adam_shard(cps[0], land2[0], chip, *[jnp.transpose(a[0]) for a in (w_in, m_w_in, v_w_in)], "adam0")
    big = [[jnp.transpose(r)[None] for r in res]]
    for i, (w, m, v) in enumerate([(w_up_a, m_w_up_a, v_w_up_a), (w_up_b, m_w_up_b, v_w_up_b),
                                   (w_out, m_w_out, v_w_out)], start=1):
        res = _adam_shard(cps[i], land2[i], chip, w[0], m[0], v[0], "adam%d" % i)
        big.append([r[None] for r in res])

    loss = loss_sum[0, 0]

    def per_kind(kd):
        return [sm[kd][0], big[0][kd], sm[kd][2], sm[kd][3], sm[kd][5], sm[kd][4], big[1][kd], big[2][kd], big[3][kd],
                sm[kd][1]]

    return (loss, grad_x[None], *per_kind(0), *per_kind(1), *per_kind(2), *per_kind(3))
```

```python
import functools
import math

import jax
import jax.numpy as jnp
from jax import lax
from jax.experimental import pallas as pl
from jax.experimental.pallas import tpu as pltpu

F32 = jnp.float32
BF16 = jnp.bfloat16
MESH = pl.DeviceIdType.MESH

N_DEV = 8
N_HEADS = 8
HEAD_DIM = 64
D_SB = N_HEADS * HEAD_DIM
N_GROUPS = 8
GROUP_DIM = 64
D_SGU = N_GROUPS * GROUP_DIM
SGU_CHUNK = 128
CHUNK = 64
EPS = 1e-6
LANES = 128
N_PAIRS = N_HEADS // 2
QKV_COLS = 3 * D_SB
ATT_BLOCK = 256
ATT_TILE = 128
CARRY_FLOOR = -90.0
R_UNREACHED = -1e30

ADAM_LR = 0.001
ADAM_B1 = 0.9
ADAM_B2 = 0.999
ADAM_EPS = 1e-08
ADAM_WD = 0.01
ADAM_STEP = 10

VMEM_LIMIT = 56 * 1024 * 1024


def _cparams(sem=None, vmem=VMEM_LIMIT):
    return pltpu.CompilerParams(dimension_semantics=sem, vmem_limit_bytes=vmem)


def _dot(a, b):
    return jnp.dot(a, b, preferred_element_type=F32)


def _dot_nt(a, b):
    return lax.dot_general(a, b, (((1,), (1,)), ((), ())), preferred_element_type=F32)


def _dot_tn(a, b):
    return lax.dot_general(a, b, (((0,), (0,)), ((), ())), preferred_element_type=F32)


def _split_hi_lo(a):
    hi = a.astype(BF16)
    lo = (a - hi.astype(F32)).astype(BF16)
    return hi, lo


def _sigmoid(x):
    return 1.0 / (1.0 + jnp.exp(-x))


_GELU_C = math.sqrt(2.0 / math.pi)


def _gelu_and_grad(x):
    x2 = x * x
    inner = _GELU_C * (x + 0.044715 * (x2 * x))
    t = jnp.tanh(inner)
    cdf = 0.5 * (1.0 + t)
    g = x * cdf
    dg = cdf + x * (0.5 * (1.0 - t * t)) * (_GELU_C * (1.0 + 3.0 * 0.044715 * x2))
    return g, dg


def _coords():
    return lax.axis_index("x"), lax.axis_index("y"), lax.axis_index("c")


def _dev_index(px, py, pc):
    return 4 * px + 2 * py + pc


def _allgather(blocks, name):
    n = len(blocks)

    def body(*refs):
        gather = _Gather(refs[:n], refs[n:2 * n], *refs[2 * n:])
        gather.issue()
        gather.finish()

    any_spec = pl.BlockSpec(memory_space=pl.ANY)
    return pl.pallas_call(
        body, name=name,
        out_shape=_gather_out_shapes(blocks),
        in_specs=[any_spec] * n, out_specs=[any_spec] * n,
        scratch_shapes=_gather_semaphores(n),
    )(*blocks)


def _gather_out_shapes(blocks):
    return [jax.ShapeDtypeStruct((N_DEV,) + b.shape, b.dtype) for b in blocks]


def _gather_semaphores(n):
    return [pltpu.SemaphoreType.DMA((n, 7)), pltpu.SemaphoreType.DMA((n, 7)), pltpu.SemaphoreType.DMA((n,))]


class _Gather:
    def __init__(self, ins, outs, send_sems, recv_sems, local_sems):
        self.ins, self.outs = ins, outs
        self.send_sems, self.recv_sems, self.local_sems = send_sems, recv_sems, local_sems
        self.n = len(ins)
        x, y, c = _coords()
        self.c = c
        self.me, self.sibling = (x, y, c), (x, y, 1 - c)
        self.chips = [(1 - x, y), (x, 1 - y), (1 - x, 1 - y)]

    def _copy(self, a, k, block, to, src=None):
        dst = self.outs[a].at[_dev_index(*block)]
        return pltpu.make_async_remote_copy(
            src_ref=dst if src is None else src, dst_ref=dst,
            send_sem=self.send_sems.at[a, k], recv_sem=self.recv_sems.at[a, k],
            device_id=to, device_id_type=MESH)

    def _mine(self):
        return [pltpu.make_async_copy(self.ins[a], self.outs[a].at[_dev_index(*self.me)], self.local_sems.at[a])
                for a in range(self.n)]

    def _first(self):
        first = []
        for a in range(self.n):
            first.append(self._copy(a, 0, self.me, self.sibling, src=self.ins[a]))
            first += [self._copy(a, 1 + j, self.me, (*chip, self.c), src=self.ins[a])
                      for j, chip in enumerate(self.chips)]
        return first

    def issue(self):
        for cp in self._mine() + self._first():
            cp.start()

    def finish(self):
        c = self.c
        passed = []
        for j, chip in enumerate(self.chips):
            for a in range(self.n):
                self._copy(a, 1 + j, (*chip, c), self.me).wait_recv()
                fwd = self._copy(a, 4 + j, (*chip, c), self.sibling)
                fwd.start()
                passed.append(fwd)
        for a in range(self.n):
            self._copy(a, 0, self.sibling, self.me).wait_recv()
            for j, chip in enumerate(self.chips):
                self._copy(a, 4 + j, (*chip, 1 - c), self.me).wait_recv()
        for cp in self._first() + passed:
            cp.wait_send()
        for cp in self._mine():
            cp.wait()


def _push_sibling(arrs, name):
    n = len(arrs)

    def body(*refs):
        ins, outs = refs[:n], refs[n:2 * n]
        send_sems, recv_sems = refs[2 * n:]
        x, y, c = _coords()
        sibling = (x, y, 1 - c)
        copies = []
        for a in range(n):
            for k in range(4):
                copies.append(pltpu.make_async_remote_copy(
                    src_ref=ins[a].at[k, 1 - c], dst_ref=outs[a].at[k],
                    send_sem=send_sems.at[a, k], recv_sem=recv_sems.at[a, k],
                    device_id=sibling, device_id_type=MESH))
        for cp in copies:
            cp.start()
        for cp in copies:
            cp.wait()

    any_spec = pl.BlockSpec(memory_space=pl.ANY)
    return pl.pallas_call(
        body, name=name,
        out_shape=[jax.ShapeDtypeStruct((4,) + a.shape[2:], a.dtype) for a in arrs],
        in_specs=[any_spec] * n, out_specs=[any_spec] * n,
        scratch_shapes=[pltpu.SemaphoreType.DMA((n, 4)), pltpu.SemaphoreType.DMA((n, 4))],
    )(*arrs)


def _chip_push_copies(ins, outs, send_sems, recv_sems):
    x, y, c = _coords()
    chips = [(1 - x, y), (x, 1 - y), (1 - x, 1 - y)]
    return [pltpu.make_async_remote_copy(
        src_ref=ins[a].at[2 * px + py], dst_ref=outs[a].at[r],
        send_sem=send_sems.at[a, r], recv_sem=recv_sems.at[a, r],
        device_id=(px, py, c), device_id_type=MESH)
        for a in range(len(ins)) for r, (px, py) in enumerate(chips)]


def _chip_partial_sum(own, land, core, name):
    _, _, rows, cols = own.shape
    tr = rows

    def body(core_ref, own_ref, land_ref, out_ref):
        del core_ref
        out_ref[...] = (own_ref[...].astype(F32) + land_ref[...].astype(F32)).astype(out_ref.dtype)

    return pl.pallas_call(
        body, name=name,
        out_shape=jax.ShapeDtypeStruct((4, rows, cols), own.dtype),
        grid_spec=pltpu.PrefetchScalarGridSpec(
            num_scalar_prefetch=1, grid=(4, rows // tr),
            in_specs=[pl.BlockSpec((None, None, tr, cols), lambda k, r, core: (k, core[0], r, 0)),
                      pl.BlockSpec((None, tr, cols), lambda k, r, core: (k, r, 0))],
            out_specs=pl.BlockSpec((None, tr, cols), lambda k, r, core: (k, r, 0))),
        compiler_params=_cparams(("parallel", "parallel")),
    )(core, own, land)


def _adamw_math(w, g, m, v):
    m = ADAM_B1 * m + (1.0 - ADAM_B1) * g
    v = ADAM_B2 * v + (1.0 - ADAM_B2) * (g * g)
    m_hat = m / (1.0 - ADAM_B1 ** ADAM_STEP)
    v_hat = v / (1.0 - ADAM_B2 ** ADAM_STEP)
    delta = -ADAM_LR * (m_hat / (jnp.sqrt(v_hat) + ADAM_EPS) + ADAM_WD * w)
    return delta, m, v


def _adam_shard(cp, land, chip, w, m, v, name):
    rows, cols = w.shape
    tr = rows // 4

    def body(chip_ref, cp_ref, land_ref, w_ref, m_ref, v_ref, g_out, d_out, m_out, v_out):
        del chip_ref
        g = cp_ref[...].astype(F32)
        for r in range(3):
            g = g + land_ref[r].astype(F32)
        delta, m_new, v_new = _adamw_math(w_ref[...], g, m_ref[...], v_ref[...])
        g_out[...] = g
        d_out[...] = delta
        m_out[...] = m_new
        v_out[...] = v_new

    tile = pl.BlockSpec((tr, cols), lambda r, chip: (r, 0))
    out = jax.ShapeDtypeStruct((rows, cols), F32)
    return pl.pallas_call(
        body, name=name, out_shape=[out] * 4,
        grid_spec=pltpu.PrefetchScalarGridSpec(
            num_scalar_prefetch=1, grid=(rows // tr,),
            in_specs=[pl.BlockSpec((None, tr, cols), lambda r, chip: (chip[0], r, 0)),
                      pl.BlockSpec((3, tr, cols), lambda r, chip: (0, r, 0)),
                      tile, tile, tile],
            out_specs=[tile] * 4),
        compiler_params=_cparams(("parallel",)),
    )(chip, cp, land, w, m, v)


def _adam_small(parts, ws, ms, vs, loss_parts, name):
    n = len(ws)

    def body(*refs):
        p_refs, w_refs, m_refs, v_refs = refs[:n], refs[n:2 * n], refs[2 * n:3 * n], refs[3 * n:4 * n]
        loss_ref, outs, loss_out = refs[4 * n], refs[4 * n + 1:-1], refs[-1]
        total = loss_ref[0]
        for dev in range(1, N_DEV):
            total = total + loss_ref[dev]
        loss_out[...] = total
        for i in range(n):
            g = p_refs[i][0]
            for dev in range(1, N_DEV):
                g = g + p_refs[i][dev]
            if g.shape[0] != w_refs[i].shape[0]:
                g = jnp.sum(g, axis=0, keepdims=True)
            delta, m_new, v_new = _adamw_math(w_refs[i][...], g, m_refs[i][...], v_refs[i][...])
            outs[i][...] = g
            outs[n + i][...] = delta
            outs[2 * n + i][...] = m_new
            outs[3 * n + i][...] = v_new

    out_shapes = [jax.ShapeDtypeStruct(w.shape, F32) for w in ws] * 4
    out_shapes.append(jax.ShapeDtypeStruct(loss_parts.shape[1:], F32))
    res = pl.pallas_call(body, name=name, out_shape=out_shapes, compiler_params=_cparams())(
        *parts, *ws, *ms, *vs, loss_parts)
    return [res[k * n:(k + 1) * n] for k in range(4)], res[-1]


def _rmsnorm_fwd(x, g, riders, name):
    s, d = x.shape
    ts = 512
    nt = s // ts
    nr = len(riders)

    def body(x_ref, g_ref, *rest):
        rider_in, (h_ref, ht_ref) = rest[:nr], rest[nr:nr + 2]
        rider_out, sems = rest[nr + 2:2 * nr + 2], rest[2 * nr + 2:]
        i = pl.program_id(0)
        gather = _Gather(rider_in, rider_out, *sems)

        @pl.when(i == 0)
        def _():
            gather.issue()

        xv = x_ref[...]
        rstd = lax.rsqrt(jnp.mean(xv * xv, axis=-1, keepdims=True) + EPS)
        h = xv * rstd * g_ref[...]
        h_ref[...] = h.astype(BF16)
        ht_ref[...] = h.T.astype(BF16)

        @pl.when(i == nt - 1)
        def _():
            gather.finish()

    any_spec = pl.BlockSpec(memory_space=pl.ANY)
    res = pl.pallas_call(
        body, name=name,
        out_shape=[jax.ShapeDtypeStruct((s, d), BF16), jax.ShapeDtypeStruct((d, s), BF16)]
        + _gather_out_shapes(riders),
        grid=(nt,),
        in_specs=[pl.BlockSpec((ts, d), lambda i: (i, 0)), pl.BlockSpec((1, d), lambda i: (0, 0))] + [any_spec] * nr,
        out_specs=[pl.BlockSpec((ts, d), lambda i: (i, 0)), pl.BlockSpec((d, ts), lambda i: (0, i))]
        + [any_spec] * nr,
        scratch_shapes=_gather_semaphores(nr),
        compiler_params=_cparams(("arbitrary",)),
    )(x, g, *riders)
    return res[0], res[1], res[2:]


def _inproj(h, wt, col0, ncols, out_dtype, riders, name):
    s, d = h.shape
    tn = 256
    j0 = col0 // tn
    nj = ncols // tn
    nr = len(riders)

    def body(h_ref, w_ref, *rest):
        rider_in, o_ref, rider_out, sems = rest[:nr], rest[nr], rest[nr + 1:2 * nr + 1], rest[2 * nr + 1:]
        j = pl.program_id(0)
        if nr:
            gather = _Gather(rider_in, rider_out, *sems)

            @pl.when(j == 0)
            def _():
                gather.issue()

        o_ref[...] = _dot_nt(h_ref[...], w_ref[...]).astype(o_ref.dtype)

        if nr:
            @pl.when(j == nj - 1)
            def _():
                gather.finish()

    any_spec = pl.BlockSpec(memory_space=pl.ANY)
    res = pl.pallas_call(
        body, name=name, out_shape=[jax.ShapeDtypeStruct((s, ncols), out_dtype)] + _gather_out_shapes(riders),
        grid=(nj,),
        in_specs=[pl.BlockSpec((s, d), lambda j: (0, 0)), pl.BlockSpec((tn, d), lambda j: (j + j0, 0))]
        + [any_spec] * nr,
        out_specs=[pl.BlockSpec((s, tn), lambda j: (0, j))] + [any_spec] * nr,
        scratch_shapes=_gather_semaphores(nr) if nr else [],
        compiler_params=_cparams(("arbitrary",) if nr else ("parallel",)),
    )(h, wt, *riders)
    return res[0], res[1:]


def _dw_in(ht, dproj, smalls, cps, name):
    d, s = ht.shape
    n = dproj.shape[1]
    tn = 512
    nj = n // tn
    ns, nc = len(smalls), len(cps)

    def body(a_ref, b_ref, *rest):
        small_in, cp_in = rest[:ns], rest[ns:ns + nc]
        o_ref = rest[ns + nc]
        small_out, land_out = rest[ns + nc + 1:2 * ns + nc + 1], rest[2 * ns + nc + 1:2 * (ns + nc) + 1]
        sems = rest[2 * (ns + nc) + 1:]
        j = pl.program_id(0)
        gather = _Gather(small_in, small_out, *sems[:3])

        @pl.when(j == 0)
        def _():
            gather.issue()
            for cp in _chip_push_copies(cp_in, land_out, *sems[3:]):
                cp.start()

        o_ref[...] = _dot(a_ref[...], b_ref[...]).T.astype(o_ref.dtype)

        @pl.when(j == nj - 1)
        def _():
            gather.finish()
            for cp in _chip_push_copies(cp_in, land_out, *sems[3:]):
                cp.wait()

    any_spec = pl.BlockSpec(memory_space=pl.ANY)
    res = pl.pallas_call(
        body, name=name,
        out_shape=[jax.ShapeDtypeStruct((n, d), BF16)] + _gather_out_shapes(smalls)
        + [jax.ShapeDtypeStruct((3,) + a.shape[1:], a.dtype) for a in cps],
        grid=(nj,),
        in_specs=[pl.BlockSpec((d, s), lambda j: (0, 0)), pl.BlockSpec((s, tn), lambda j: (0, j))]
        + [any_spec] * (ns + nc),
        out_specs=[pl.BlockSpec((tn, d), lambda j: (j, 0))] + [any_spec] * (ns + nc),
        scratch_shapes=_gather_semaphores(ns) + [pltpu.SemaphoreType.DMA((nc, 3)), pltpu.SemaphoreType.DMA((nc, 3))],
        compiler_params=_cparams(("arbitrary",)),
    )(ht, dproj, *smalls, *cps)
    return res[0], res[1:1 + ns], res[1 + ns:]


def _dh_and_grad_x(dproj, wt, x, g, dx2, cps, name):
    s, n = dproj.shape
    d = wt.shape[1]
    tm, tk = min(s, 1024), 512
    nk = n // tk
    nm = s // tm
    nc = len(cps)

    def body(a_ref, w_ref, x_ref, g_ref, dx2_ref, *rest):
        cp_refs, gx_ref, dg_ref = rest[:nc], rest[nc], rest[nc + 1]
        land_refs = rest[nc + 2:2 * nc + 2]
        acc_ref, send_sems, recv_sems = rest[2 * nc + 2:]
        i, k = pl.program_id(0), pl.program_id(1)

        @pl.when((i == 0) & (k == 0))
        def _():
            for cp in _chip_push_copies(cp_refs, land_refs, send_sems, recv_sems):
                cp.start()
            dg_ref[...] = jnp.zeros_like(dg_ref)

        @pl.when(k == 0)
        def _():
            acc_ref[...] = jnp.zeros_like(acc_ref)

        acc_ref[...] += _dot(a_ref[...], w_ref[...])

        @pl.when(k == nk - 1)
        def _():
            dh = acc_ref[...]
            xv = x_ref[...]
            rstd = lax.rsqrt(jnp.mean(xv * xv, axis=-1, keepdims=True) + EPS)
            xhat = xv * rstd
            dg_ref[...] += jnp.sum((dh * xhat).reshape(tm // 8, 8, d), axis=0)
            dxh = dh * g_ref[...]
            gx_ref[...] = dx2_ref[...] + rstd * (dxh - xhat * jnp.mean(dxh * xhat, axis=-1, keepdims=True))

        @pl.when((i == nm - 1) & (k == nk - 1))
        def _():
            for cp in _chip_push_copies(cp_refs, land_refs, send_sems, recv_sems):
                cp.wait()

    any_spec = pl.BlockSpec(memory_space=pl.ANY)
    res = pl.pallas_call(
        body, name=name,
        out_shape=[jax.ShapeDtypeStruct((s, d), F32), jax.ShapeDtypeStruct((8, d), F32)]
        + [jax.ShapeDtypeStruct((3,) + a.shape[1:], a.dtype) for a in cps],
        grid=(nm, nk),
        in_specs=[pl.BlockSpec((tm, tk), lambda i, k: (i, k)), pl.BlockSpec((tk, d), lambda i, k: (k, 0)),
                  pl.BlockSpec((tm, d), lambda i, k: (i, 0)), pl.BlockSpec((1, d), lambda i, k: (0, 0)),
                  pl.BlockSpec((tm, d), lambda i, k: (i, 0))] + [any_spec] * nc,
        out_specs=[pl.BlockSpec((tm, d), lambda i, k: (i, 0)), pl.BlockSpec((8, d), lambda i, k: (0, 0))]
        + [any_spec] * nc,
        scratch_shapes=[pltpu.VMEM((tm, d), F32), pltpu.SemaphoreType.DMA((nc, 3)), pltpu.SemaphoreType.DMA((nc, 3))],
        compiler_params=_cparams(("arbitrary", "arbitrary")),
    )(dproj, wt, x, g, dx2, *cps)
    return res[0], res[1], res[2:]


def _log_sigmoids(z):
    l1p = jnp.log(1.0 + jnp.exp(-jnp.abs(z)))
    ls = jnp.minimum(z, 0.0) - l1p
    return ls, ls - z


def _strict_lower_ones(n):
    row = lax.broadcasted_iota(jnp.int32, (n, n), 0)
    col = lax.broadcasted_iota(jnp.int32, (n, n), 1)
    return row, col, (row > col).astype(BF16)


def _attn_fwd(qkv, name):
    s = qkv.shape[0]
    tb, tt = ATT_BLOCK, ATT_TILE
    nq = s // tb
    per_block = tb // tt
    assert per_block == 2 and s // tt <= LANES, "two query tiles per grid step; one lane of saved carry per key tile"

    def body(q_ref, k_ref, v_ref, o_ref, rs_ref, extra_ref, acc_ref, r_ref, rsv_ref, rmax_ref):
        i = pl.program_id(1)
        lane = lax.broadcasted_iota(jnp.int32, (tt, LANES), 1)
        hmask = [lane < HEAD_DIM, lane >= HEAD_DIM]
        row, col, tri = _strict_lower_ones(tt)
        tri2 = jnp.concatenate([tri, tri], axis=0)
        below = col < row
        qrows = [slice(u * tt, (u + 1) * tt) for u in range(per_block)]
        qm = [[jnp.where(m, q_ref[qrows[u], :], jnp.zeros((), BF16)) * jnp.asarray(HEAD_DIM ** -0.5, BF16)
               for m in hmask] for u in range(per_block)]
        acc_ref[...] = jnp.zeros_like(acc_ref)
        r_ref[...] = jnp.zeros_like(r_ref)
        rsv_ref[...] = jnp.full_like(rsv_ref, R_UNREACHED)

        def sweep(tiles, chains):
            heads, nch = range(2), range(len(chains))
            kv = []
            for t in tiles:
                rows = pl.ds(pl.multiple_of(t * tt, tt), tt)
                kv.append((k_ref[rows, :], v_ref[rows, :]))
            z = [[_dot_nt(qm[u][h], kv[ti][0]) for h in heads] for u, ti, _ in chains]
            ls, lk = [], []
            for n in nch:
                pairs = [_log_sigmoids(z[n][h]) for h in heads]
                ls.append([p[0] for p in pairs])
                lk.append([jnp.where(below, p[1], 0.0) if chains[n][2] else p[1] for p in pairs])
            cur = {u: [r_ref[h, qrows[u], :] for h in heads] for u in sorted({c[0] for c in chains})}
            r = []
            for n, (u, _, _) in enumerate(chains):
                r.append(cur[u])
                cur[u] = [cur[u][h] + jnp.sum(lk[n][h], axis=1, keepdims=True) for h in heads]
            for u in cur:
                rmax_ref[u] = jnp.max(jnp.maximum(cur[u][0], cur[u][1]))
            suffix = [[_dot(jnp.concatenate(_split_hi_lo(lk[n][h]), axis=1), tri2) for h in heads] for n in nch]
            w = []
            for n in nch:
                w_n = [jnp.exp(ls[n][h] + suffix[n][h] + r[n][h]) for h in heads]
                if chains[n][2]:
                    w_n = [jnp.where(below, w_h, 0.0) for w_h in w_n]
                w.append([w_h.astype(BF16) for w_h in w_n])
            pv = [[_dot(w[n][h], kv[chains[n][1]][1]) for h in heads] for n in nch]
            for u in cur:
                mine = [n for n in nch if chains[n][0] == u]
                for h in heads:
                    acc_ref[h, qrows[u], :] += functools.reduce(lambda a, b: a + b, [pv[n][h] for n in mine])
                    for n in mine:
                        if not chains[n][2]:
                            t = tiles[chains[n][1]]
                            rsv_ref[h, qrows[u], :] = jnp.where(lane == t, r[n][h], rsv_ref[h, qrows[u], :])
                    r_ref[h, qrows[u], :] = cur[u][h]

        first = per_block * i

        @pl.when(i == 0)
        def _():
            sweep([0, 1], [(0, 0, True), (1, 1, True), (1, 0, False)])

        @pl.when(i > 0)
        def _():
            sweep([first - 1, first, first + 1], [(0, 1, True), (1, 2, True), (0, 0, False), (1, 1, False)])

        for u in range(per_block):
            n_left = first + u - 1

            def live(c, n_left=n_left):
                jj, rmax = c
                return (jj < n_left) & (rmax >= CARRY_FLOOR)

            def step(c, u=u, n_left=n_left):
                jj, _ = c
                sweep([n_left - 1 - jj], [(u, 0, False)])
                return jj + 1, rmax_ref[u]

            swept, _ = lax.while_loop(live, step, (jnp.int32(0), rmax_ref[u]))
            extra_ref[pl.program_id(0), first + u] = swept.astype(F32)

        lane_b = lax.broadcasted_iota(jnp.int32, (tb, LANES), 1)
        o_ref[...] = jnp.where(lane_b < HEAD_DIM, acc_ref[0], acc_ref[1])
        rs_ref[...] = rsv_ref[...]

    return pl.pallas_call(
        body, name=name,
        out_shape=[jax.ShapeDtypeStruct((s, D_SB), F32), jax.ShapeDtypeStruct((N_HEADS, s, LANES), F32),
                   jax.ShapeDtypeStruct((N_PAIRS, s // tt), F32)],
        grid=(N_PAIRS, nq),
        in_specs=[pl.BlockSpec((tb, LANES), lambda p, i: (i, p)),
                  pl.BlockSpec((s, LANES), lambda p, i: (0, N_PAIRS + p)),
                  pl.BlockSpec((s, LANES), lambda p, i: (0, 2 * N_PAIRS + p))],
        out_specs=[pl.BlockSpec((tb, LANES), lambda p, i: (i, p)),
                   pl.BlockSpec((2, tb, LANES), lambda p, i: (p, i, 0)),
                   pl.BlockSpec(memory_space=pltpu.SMEM)],
        scratch_shapes=[pltpu.VMEM((2, tb, LANES), F32), pltpu.VMEM((2, tb, LANES), F32),
                        pltpu.VMEM((2, tb, LANES), F32), pltpu.SMEM((per_block,), F32)],
        compiler_params=_cparams(("arbitrary", "arbitrary")),
    )(qkv, qkv, qkv)


def _attn_bwd(qkv, do, rs, extra, dproj, name):
    s = qkv.shape[0]
    tb, tt = ATT_BLOCK, ATT_TILE
    nq = s // tb
    per_block = tb // tt
    scale = HEAD_DIM ** -0.5

    def body(q_ref, k_ref, v_ref, do_ref, rs_ref, extra_ref, dproj_hbm, out_hbm, dq_acc, dk_acc, dv_acc, dqi_ref,
             pc_ref, stage_ref, out_sems):
        del dproj_hbm
        pair = pl.program_id(0)
        lane = lax.broadcasted_iota(jnp.int32, (tt, LANES), 1)
        hmask = [lane < HEAD_DIM, lane >= HEAD_DIM]
        row, col, tri = _strict_lower_ones(tt)
        tri2 = jnp.concatenate([tri, tri], axis=0)
        triu = (row < col).astype(BF16)
        below = col < row
        zero = jnp.zeros((), BF16)
        qrows = [slice(u * tt, (u + 1) * tt) for u in range(per_block)]
        dq_acc[...] = jnp.zeros_like(dq_acc)
        dk_acc[...] = jnp.zeros_like(dk_acc)
        dv_acc[...] = jnp.zeros_like(dv_acc)

        def qblock(i, carry):
            block_rows = pl.ds(pl.multiple_of(i * tb, tb), tb)
            q2 = q_ref[block_rows, :]
            do2 = do_ref[block_rows, :]
            qm = [[jnp.where(m, q2[qrows[u]], zero) * jnp.asarray(scale, BF16) for m in hmask]
                  for u in range(per_block)]
            dom = [[jnp.where(m, do2[qrows[u]], zero) for m in hmask] for u in range(per_block)]
            rs_i = [[rs_ref[h, pl.ds(pl.multiple_of(i * tb + u * tt, tt), tt), :] for h in range(2)]
                    for u in range(per_block)]
            dqi_ref[...] = jnp.zeros_like(dqi_ref)
            pc_ref[...] = jnp.zeros_like(pc_ref)
            first = per_block * i

            def sweep(tiles, chains):
                heads, nch = range(2), range(len(chains))
                krows = [pl.ds(pl.multiple_of(t * tt, tt), tt) for t in tiles]
                k2 = [k_ref[rows, :] for rows in krows]
                v2 = [v_ref[rows, :] for rows in krows]
                z = [[_dot_nt(qm[u][h], k2[ti]) for h in heads] for u, ti, _ in chains]
                dw = [[_dot_nt(dom[u][h], v2[ti]) for h in heads] for u, ti, _ in chains]
                ls, lk = [], []
                for n in nch:
                    pairs = [_log_sigmoids(z[n][h]) for h in heads]
                    ls.append([p[0] for p in pairs])
                    lk.append([jnp.where(below, p[1], 0.0) if chains[n][2] else p[1] for p in pairs])
                suffix = [[_dot(jnp.concatenate(_split_hi_lo(lk[n][h]), axis=1), tri2) for h in heads] for n in nch]
                w, g = [], []
                for n, (u, ti, diag) in enumerate(chains):
                    w_n = []
                    for h in heads:
                        logw = ls[n][h] + suffix[n][h]
                        if diag:
                            w_n.append(jnp.where(below, jnp.exp(logw), 0.0))
                        else:
                            carry_in = jnp.sum(jnp.where(lane == tiles[ti], rs_i[u][h], 0.0), axis=1, keepdims=True)
                            w_n.append(jnp.exp(logw + carry_in))
                    w.append([w_h.astype(BF16) for w_h in w_n])
                    g.append([w_n[h] * dw[n][h] for h in heads])
                prefix = [[_dot(g[n][h].astype(BF16), triu) for h in heads] for n in nch]
                cur = {u: [pc_ref[h, qrows[u], :] for h in heads] for u in sorted({c[0] for c in chains})}
                pc = []
                for n, (u, _, _) in enumerate(chains):
                    pc.append(cur[u])
                    cur[u] = [cur[u][h] + jnp.sum(g[n][h], axis=1, keepdims=True) for h in heads]
                dzb = []
                for n in nch:
                    dz_n = []
                    for h in heads:
                        sig = jnp.exp(ls[n][h])
                        dz = g[n][h] - sig * (g[n][h] + prefix[n][h] + pc[n][h])
                        if chains[n][2]:
                            dz = jnp.where(below, dz, 0.0)
                        dz_n.append(dz.astype(BF16))
                    dzb.append(dz_n)
                for u in cur:
                    for h in heads:
                        pc_ref[h, qrows[u], :] = cur[u][h]
                dq = [[_dot(dzb[n][h], jnp.where(hmask[h], k2[chains[n][1]], zero)) for h in heads] for n in nch]
                dk = [[_dot_tn(dzb[n][h], qm[chains[n][0]][h]) for h in heads] for n in nch]
                dv = [[_dot_tn(w[n][h], dom[chains[n][0]][h]) for h in heads] for n in nch]
                add = lambda a, b: a + b
                for u in cur:
                    dqi_ref[qrows[u], :] += functools.reduce(
                        add, [dq[n][h] for n in nch if chains[n][0] == u for h in heads])
                for ti in range(len(tiles)):
                    mine = [n for n in nch if chains[n][1] == ti]
                    dk_acc[krows[ti], :] += functools.reduce(add, [dk[n][h] for n in mine for h in heads])
                    dv_acc[krows[ti], :] += functools.reduce(add, [dv[n][h] for n in mine for h in heads])

            for u in range(per_block):
                n_left = first + u - 1
                n_extra = jnp.clip(extra_ref[pair, first + u].astype(jnp.int32), 0, jnp.maximum(n_left, 0))

                def step(t, c, u=u):
                    sweep([t], [(u, 0, False)])
                    return c

                lax.fori_loop(n_left - n_extra, n_left, step, 0)

            @pl.when(i == 0)
            def _():
                sweep([0, 1], [(0, 0, True), (1, 0, False), (1, 1, True)])

            @pl.when(i > 0)
            def _():
                sweep([first - 1, first, first + 1], [(0, 0, False), (1, 1, False), (0, 1, True), (1, 2, True)])

            dq_acc[block_rows, :] += dqi_ref[...] * scale
            return carry

        lax.fori_loop(0, nq, qblock, 0)
        copies = []
        for t, acc in enumerate((dq_acc, dk_acc, dv_acc)):
            stage_ref[t] = acc[...].astype(BF16)
            col0 = pl.multiple_of(t * D_SB + pair * LANES, LANES)
            copies.append(pltpu.make_async_copy(stage_ref.at[t], out_hbm.at[:, pl.ds(col0, LANES)], out_sems.at[t]))
        for cp in copies:
            cp.start()
        for cp in copies:
            cp.wait()

    return pl.pallas_call(
        body, name=name,
        out_shape=jax.ShapeDtypeStruct(dproj.shape, BF16),
        grid=(N_PAIRS,),
        in_specs=[pl.BlockSpec((s, LANES), lambda p: (0, p)),
                  pl.BlockSpec((s, LANES), lambda p: (0, N_PAIRS + p)),
                  pl.BlockSpec((s, LANES), lambda p: (0, 2 * N_PAIRS + p)),
                  pl.BlockSpec((s, LANES), lambda p: (0, p)),
                  pl.BlockSpec((2, s, LANES), lambda p: (p, 0, 0)),
                  pl.BlockSpec(memory_space=pltpu.SMEM),
                  pl.BlockSpec(memory_space=pl.ANY)],
        out_specs=pl.BlockSpec(memory_space=pl.ANY),
        scratch_shapes=[pltpu.VMEM((s, LANES), F32), pltpu.VMEM((s, LANES), F32), pltpu.VMEM((s, LANES), F32),
                        pltpu.VMEM((tb, LANES), F32), pltpu.VMEM((2, tb, LANES), F32),
                        pltpu.VMEM((3, s, LANES), BF16), pltpu.SemaphoreType.DMA((3,))],
        input_output_aliases={6: 0},
        compiler_params=_cparams(("arbitrary",)),
    )(qkv, qkv, qkv, do, rs, extra, dproj)


def _mid(o, rest, x, tgt, wua, wub, wout, fg, lng, lnb, wsp, bfull, gavg, name):
    s, d = x.shape
    ts = 256
    nt = s // ts
    nchunk = ts // SGU_CHUNK
    n_rest = rest.shape[1]

    def body(o_ref, rest_ref, x_ref, t_ref, wua_ref, wub_ref, wout_ref, fg_ref, lng_ref, lnb_ref, wsp_ref, bfull_ref,
             gavg_ref, loss_ref, dx2_ref, do_ref, dproj_ref, dwua_ref, dwub_ref, dwout_ref, dfg_ref, dlng_ref,
             dlnb_ref, dwsp_ref, dbfull_ref):
        step = pl.program_id(0)

        @pl.when(step == 0)
        def _():
            for ref in (loss_ref, dwua_ref, dwub_ref, dwout_ref, dfg_ref, dlng_ref, dlnb_ref, dwsp_ref, dbfull_ref):
                ref[...] = jnp.zeros_like(ref)

        gavg = gavg_ref[...]

        def gmean(a):
            return _dot(a.astype(BF16), gavg)

        def colsum8(a):
            return jnp.sum(a.reshape(ts // 8, 8, a.shape[1]), axis=0)

        z_a = rest_ref[:, 0:512]
        u_b = rest_ref[:, 512:1024]
        v_b = rest_ref[:, 1024:1536]
        z_b = rest_ref[:, 1536:2048]
        g_a = rest_ref[:, 2048:2048 + d]
        g_b = rest_ref[:, 2048 + d:2048 + 2 * d]
        ov = o_ref[...]
        sa = _sigmoid(z_a)
        silu_a = z_a * sa
        y_a = ov * silu_a
        ug, dug_du = _gelu_and_grad(u_b)
        vg, dvg_dv = _gelu_and_grad(v_b)
        mu = gmean(vg)
        cen = vg - mu
        rstd_g = lax.rsqrt(gmean(cen * cen) + EPS)
        vhat = cen * rstd_g
        vn = vhat * lng_ref[...] + lnb_ref[...]
        vnb = vn.astype(BF16)

        t_idx = lax.broadcasted_iota(jnp.int32, (SGU_CHUNK, SGU_CHUNK), 0)
        s_idx = lax.broadcasted_iota(jnp.int32, (SGU_CHUNK, SGU_CHUNK), 1)
        causal = (s_idx // CHUNK) <= (t_idx // CHUNK)
        wm = [jnp.where(causal, wsp_ref[g], 0.0) for g in range(N_GROUPS)]
        wmb = [w.astype(BF16) for w in wm]
        wmtb = [w.T.astype(BF16) for w in wm]
        lane = lax.broadcasted_iota(jnp.int32, (SGU_CHUNK, LANES), 1)
        first = lane < GROUP_DIM
        bfull = bfull_ref[...]

        mixed_rows = []
        for n in range(nchunk):
            r0, r1 = n * SGU_CHUNK, (n + 1) * SGU_CHUNK
            pieces = []
            for p in range(N_GROUPS // 2):
                blk = vnb[r0:r1, p * LANES:(p + 1) * LANES]
                pieces.append(jnp.where(first, _dot(wmb[2 * p], blk), _dot(wmb[2 * p + 1], blk)))
            mixed_rows.append(jnp.concatenate(pieces, axis=1) + bfull)
        mixed = jnp.concatenate(mixed_rows, axis=0)
        sg = ug * mixed
        sb = _sigmoid(z_b)
        silu_b = z_b * sb
        y_b = sg * silu_b
        y_ab = y_a.astype(BF16)
        y_bb = y_b.astype(BF16)
        p_a = _dot(y_ab, wua_ref[...])
        p_b = _dot(y_bb, wub_ref[...])
        ga_s = _sigmoid(g_a)
        gb_s = _sigmoid(g_b)
        merged_b = (ga_s * p_a + gb_s * p_b).astype(BF16)
        x2 = x_ref[...] + _dot(merged_b, wout_ref[...])
        rstd = lax.rsqrt(jnp.mean(x2 * x2, axis=-1, keepdims=True) + EPS)
        xhat = x2 * rstd
        fg_v = fg_ref[...]
        diff = xhat * fg_v - t_ref[...]
        loss_ref[...] += 0.5 * jnp.sum(jnp.sum(diff * diff, axis=-1, keepdims=True) * (1.0 / d))

        dy = diff * (1.0 / d)
        dfg_ref[...] += colsum8(dy * xhat)
        dxh = dy * fg_v
        dx2 = rstd * (dxh - xhat * jnp.mean(dxh * xhat, axis=-1, keepdims=True))
        dx2_ref[...] = dx2
        dx2b = dx2.astype(BF16)
        dwout_ref[...] += _dot_tn(merged_b, dx2b)
        dmerged = _dot_nt(dx2b, wout_ref[...])
        dp_a = dmerged * ga_s
        dp_b = dmerged * gb_s
        dproj_ref[:, QKV_COLS + 2048:QKV_COLS + 2048 + d] = (dmerged * p_a * (ga_s * (1.0 - ga_s))).astype(BF16)
        dproj_ref[:, QKV_COLS + 2048 + d:QKV_COLS + 2048 + 2 * d] = (dmerged * p_b * (gb_s * (1.0 - gb_s))).astype(BF16)
        dp_ab = dp_a.astype(BF16)
        dp_bb = dp_b.astype(BF16)
        dwua_ref[...] += _dot_tn(y_ab, dp_ab)
        dwub_ref[...] += _dot_tn(y_bb, dp_bb)
        dy_a = _dot_nt(dp_ab, wua_ref[...])
        dy_b = _dot_nt(dp_bb, wub_ref[...])
        do_ref[...] = (dy_a * silu_a).astype(BF16)
        dproj_ref[:, QKV_COLS:QKV_COLS + 512] = (dy_a * ov * (sa * (1.0 + z_a * (1.0 - sa)))).astype(BF16)
        dsg = dy_b * silu_b
        dproj_ref[:, QKV_COLS + 1536:QKV_COLS + 2048] = (dy_b * sg * (sb * (1.0 + z_b * (1.0 - sb)))).astype(BF16)
        dproj_ref[:, QKV_COLS + 512:QKV_COLS + 1024] = (dsg * mixed * dug_du).astype(BF16)
        dmixed = dsg * ug
        dmb = dmixed.astype(BF16)
        zero = jnp.zeros((), BF16)
        dvn_rows = []
        db = jnp.zeros((SGU_CHUNK, D_SGU), F32)
        for n in range(nchunk):
            r0, r1 = n * SGU_CHUNK, (n + 1) * SGU_CHUNK
            db = db + dmixed[r0:r1, :]
            pieces = []
            for p in range(N_GROUPS // 2):
                cols = slice(p * LANES, (p + 1) * LANES)
                dm_blk = dmb[r0:r1, cols]
                vn_blk = vnb[r0:r1, cols]
                dwsp_ref[2 * p] += _dot_nt(jnp.where(first, dm_blk, zero), vn_blk)
                dwsp_ref[2 * p + 1] += _dot_nt(jnp.where(first, zero, dm_blk), vn_blk)
                pieces.append(jnp.where(first, _dot(wmtb[2 * p], dm_blk), _dot(wmtb[2 * p + 1], dm_blk)))
            dvn_rows.append(jnp.concatenate(pieces, axis=1))
        dbfull_ref[...] += db
        dvn = jnp.concatenate(dvn_rows, axis=0)
        dlng_ref[...] += colsum8(dvn * vhat)
        dlnb_ref[...] += colsum8(dvn)
        dvhat = dvn * lng_ref[...]
        dcen = rstd_g * (dvhat - gmean(dvhat) - vhat * gmean(dvhat * vhat))
        dproj_ref[:, QKV_COLS + 1024:QKV_COLS + 1536] = (dcen * dvg_dv).astype(BF16)

        @pl.when(step == nt - 1)
        def _():
            for g in range(N_GROUPS):
                dwsp_ref[g] = jnp.where(causal, dwsp_ref[g], 0.0)

    def tile(cols):
        return pl.BlockSpec((ts, cols), lambda i: (i, 0))

    def whole(shape):
        return pl.BlockSpec(shape, lambda i: (0,) * len(shape))

    out_shapes = [
        jax.ShapeDtypeStruct((8, LANES), F32),
        jax.ShapeDtypeStruct((s, d), F32),
        jax.ShapeDtypeStruct((s, D_SB), BF16),
        jax.ShapeDtypeStruct((s, QKV_COLS + n_rest), BF16),
        jax.ShapeDtypeStruct((D_SB, d), F32),
        jax.ShapeDtypeStruct((D_SGU, d), F32),
        jax.ShapeDtypeStruct((d, d), F32),
        jax.ShapeDtypeStruct((8, d), F32),
        jax.ShapeDtypeStruct((8, D_SGU), F32),
        jax.ShapeDtypeStruct((8, D_SGU), F32),
        jax.ShapeDtypeStruct((N_GROUPS, SGU_CHUNK, SGU_CHUNK), F32),
        jax.ShapeDtypeStruct((SGU_CHUNK, D_SGU), F32),
    ]
    out_specs = [whole((8, LANES)), tile(d), tile(D_SB), tile(QKV_COLS + n_rest), whole((D_SB, d)), whole((D_SGU, d)),
                 whole((d, d)), whole((8, d)), whole((8, D_SGU)), whole((8, D_SGU)),
                 whole((N_GROUPS, SGU_CHUNK, SGU_CHUNK)), whole((SGU_CHUNK, D_SGU))]
    in_specs = [tile(D_SB), tile(n_rest), tile(d), tile(d), whole((D_SB, d)), whole((D_SGU, d)), whole((d, d)),
                whole((1, d)), whole((1, D_SGU)), whole((1, D_SGU)), whole((N_GROUPS, SGU_CHUNK, SGU_CHUNK)),
                whole((SGU_CHUNK, D_SGU)), whole((D_SGU, D_SGU))]
    return pl.pallas_call(
        body, name=name, out_shape=out_shapes, grid=(nt,), in_specs=in_specs, out_specs=out_specs,
        compiler_params=_cparams(("arbitrary",)),
    )(o, rest, x, tgt, wua, wub, wout, fg, lng, lnb, wsp, bfull, gavg)


def _small_reduce(dfg8, dlng8, dlnb8, dbfull, name):
    d = dfg8.shape[1]

    def body(dfg_ref, dlng_ref, dlnb_ref, dbfull_ref, fg_out, lng_out, lnb_out, b_out):
        fg_out[...] = jnp.sum(dfg_ref[...], axis=0, keepdims=True)
        lng_out[...] = jnp.sum(dlng_ref[...], axis=0, keepdims=True)
        lnb_out[...] = jnp.sum(dlnb_ref[...], axis=0, keepdims=True)
        grp = lax.broadcasted_iota(jnp.int32, (D_SGU, LANES), 0) // GROUP_DIM
        col = lax.broadcasted_iota(jnp.int32, (D_SGU, LANES), 1)
        sel = (grp == col).astype(BF16)
        a = dbfull_ref[...]
        hi, lo = _split_hi_lo(a)
        lo2 = (a - hi.astype(F32) - lo.astype(F32)).astype(BF16)
        by_pos = _dot(hi, sel) + _dot(lo, sel) + _dot(lo2, sel)
        b_out[...] = by_pos.T[0:N_GROUPS, :]

    return pl.pallas_call(
        body, name=name,
        out_shape=[jax.ShapeDtypeStruct((1, d), F32), jax.ShapeDtypeStruct((1, D_SGU), F32),
                   jax.ShapeDtypeStruct((1, D_SGU), F32), jax.ShapeDtypeStruct((N_GROUPS, SGU_CHUNK), F32)],
        compiler_params=_cparams(),
    )(dfg8, dlng8, dlnb8, dbfull)


def _block_major_cols(w):
    r, n = w.shape
    return jnp.transpose(w.reshape(r, N_DEV, n // N_DEV), (1, 0, 2))


def _from_block_major_cols(w):
    nb, r, c = w.shape
    return jnp.transpose(w, (1, 0, 2)).reshape(r, nb * c)


def kernel(x, norm_g, w_in, sgu_ln_g, sgu_ln_b, w_spatial, b_spatial, w_up_a, w_up_b, w_out, final_norm_g, loss_target, m_norm_g, m_w_in, m_sgu_ln_g, m_sgu_ln_b, m_w_spatial, m_b_spatial, m_w_up_a, m_w_up_b, m_w_out, m_final_norm_g, v_norm_g, v_w_in, v_sgu_ln_g, v_sgu_ln_b, v_w_spatial, v_b_spatial, v_w_up_a, v_w_up_b, v_w_out, v_final_norm_g):
    s, d = x.shape[1], x.shape[2]
    xs = x[0]
    tgt = loss_target[0]
    cx, cy, cc = _coords()
    core = jnp.reshape(cc, (1,)).astype(jnp.int32)
    chip = jnp.reshape(2 * cx + cy, (1,)).astype(jnp.int32)

    h, ht, (g_win,) = _rmsnorm_fwd(xs, norm_g, [jnp.transpose(w_in[0]).astype(BF16)], "norm")
    d_in = N_DEV * w_in.shape[2]
    w_full = g_win.reshape(d_in, d)
    qkv, _ = _inproj(h, w_full, 0, QKV_COLS, BF16, [], "inproj_qkv")
    rest, (g_wua, g_wub, g_wout) = _inproj(
        h, w_full, QKV_COLS, d_in - QKV_COLS, F32,
        [w_up_a[0].astype(BF16), w_up_b[0].astype(BF16), w_out[0].astype(BF16)], "inproj_rest")
    wua_full = _from_block_major_cols(g_wua)
    wub_full = _from_block_major_cols(g_wub)
    wout_full = g_wout.reshape(d, d)
    o, rs, extra = _attn_fwd(qkv, "attn_fwd")

    lng = sgu_ln_g.reshape(1, D_SGU)
    lnb = sgu_ln_b.reshape(1, D_SGU)
    bfull = jnp.repeat(jnp.transpose(b_spatial[0]), GROUP_DIM, axis=1)
    grp = jnp.arange(D_SGU) // GROUP_DIM
    gavg = jnp.where(grp[:, None] == grp[None, :], 1.0 / GROUP_DIM, 0.0).astype(BF16)
    (loss_b, dx2, do, dproj, dwua, dwub, dwout, dfg8, dlng8, dlnb8, dwsp, dbfull) = _mid(
        o, rest, xs, tgt, wua_full, wub_full, wout_full, final_norm_g.reshape(1, d), lng, lnb, w_spatial[0], bfull,
        gavg, "mid")

    def blocks42(a):
        return a.reshape((4, 2) + a.shape[1:])

    def chip_partials(own, tag):
        land = _push_sibling(own, "rs_sibling_" + tag)
        return [_chip_partial_sum(a, l, core, "cpsum_%s%d" % (tag, i)) for i, (a, l) in enumerate(zip(own, land))]

    cps_up = chip_partials([blocks42(_block_major_cols(dwua.astype(BF16))),
                            blocks42(_block_major_cols(dwub.astype(BF16))),
                            blocks42(dwout.astype(BF16).reshape(N_DEV, d // N_DEV, d))], "up")

    dproj = _attn_bwd(qkv, do, rs, extra, dproj, "attn_bwd")
    dfg, dlng, dlnb, db = _small_reduce(dfg8, dlng8, dlnb8, dbfull, "small_reduce")
    wsp_rows = N_GROUPS * SGU_CHUNK
    dwin, small_parts, land_up = _dw_in(
        ht, dproj, [dfg, dlng, dlnb, db, dwsp.reshape(wsp_rows, SGU_CHUNK)], cps_up, "dwin")

    cps_in = chip_partials([dwin.reshape(4, 2, d_in // N_DEV, d)], "in")
    grad_x, dng8, land_in = _dh_and_grad_x(dproj, w_full, xs, norm_g, dx2, cps_in, "dh")
    cps = cps_in + cps_up
    land2 = list(land_in) + list(land_up)

    ng_parts, loss_parts = _allgather([dng8, loss_b], "ag_tail")

    def small_layouts(ng, fg, lg, lb, bs, ws):
        return [ng.reshape(1, d), fg.reshape(1, d), lg.reshape(1, D_SGU), lb.reshape(1, D_SGU),
                bs.reshape(N_GROUPS, SGU_CHUNK), ws.reshape(wsp_rows, SGU_CHUNK)]

    sm, loss_sum = _adam_small(
        [ng_parts] + list(small_parts),
        small_layouts(norm_g, final_norm_g, sgu_ln_g, sgu_ln_b, b_spatial, w_spatial),
        small_layouts(m_norm_g, m_final_norm_g, m_sgu_ln_g, m_sgu_ln_b, m_b_spatial, m_w_spatial),
        small_layouts(v_norm_g, v_final_norm_g, v_sgu_ln_g, v_sgu_ln_b, v_b_spatial, v_w_spatial),
        loss_parts, "adam_small")
    small_shapes = [norm_g.shape, final_norm_g.shape, sgu_ln_g.shape, sgu_ln_b.shape, b_spatial.shape, w_spatial.shape]
    sm = [[a.reshape(shp) for a, shp in zip(kind, small_shapes)] for kind in sm]

    res = _adam_shard(cps[0], land2[0], chip, *[jnp.transpose(a[0]) for a in (w_in, m_w_in, v_w_in)], "adam0")
    big = [[jnp.transpose(r)[None] for r in res]]
    for i, (w, m, v) in enumerate([(w_up_a, m_w_up_a, v_w_up_a), (w_up_b, m_w_up_b, v_w_up_b),
                                   (w_out, m_w_out, v_w_out)], start=1):
        res = _adam_shard(cps[i], land2[i], chip, w[0], m[0], v[0], "adam%d" % i)
        big.append([r[None] for r in res])

    loss = loss_sum[0, 0]

    def per_kind(kd):
        return [sm[kd][0], big[0][kd], sm[kd][2], sm[kd][3], sm[kd][5], sm[kd][4], big[1][kd], big[2][kd], big[3][kd],
                sm[kd][1]]

    return (loss, grad_x[None], *per_kind(0), *per_kind(1), *per_kind(2), *per_kind(3))
```

```python
import functools
import math

import jax
import jax.numpy as jnp
from jax import lax
from jax.experimental import pallas as pl
from jax.experimental.pallas import tpu as pltpu

F32 = jnp.float32
BF16 = jnp.bfloat16
MESH = pl.DeviceIdType.MESH

N_DEV = 8
N_HEADS = 8
HEAD_DIM = 64
D_SB = N_HEADS * HEAD_DIM
N_GROUPS = 8
GROUP_DIM = 64
D_SGU = N_GROUPS * GROUP_DIM
SGU_CHUNK = 128
CHUNK = 64
EPS = 1e-6
LANES = 128
N_PAIRS = N_HEADS // 2
QKV_COLS = 3 * D_SB
ATT_BLOCK = 256
ATT_TILE = 128
CARRY_FLOOR = -90.0
R_UNREACHED = -1e30

ADAM_LR = 0.001
ADAM_B1 = 0.9
ADAM_B2 = 0.999
ADAM_EPS = 1e-08
ADAM_WD = 0.01
ADAM_STEP = 10

VMEM_LIMIT = 56 * 1024 * 1024


def _cparams(sem=None, vmem=VMEM_LIMIT):
    return pltpu.CompilerParams(dimension_semantics=sem, vmem_limit_bytes=vmem)


def _dot(a, b):
    return jnp.dot(a, b, preferred_element_type=F32)


def _dot_nt(a, b):
    return lax.dot_general(a, b, (((1,), (1,)), ((), ())), preferred_element_type=F32)


def _dot_tn(a, b):
    return lax.dot_general(a, b, (((0,), (0,)), ((), ())), preferred_element_type=F32)


def _split_hi_lo(a):
    hi = a.astype(BF16)
    lo = (a - hi.astype(F32)).astype(BF16)
    return hi, lo


def _sigmoid(x):
    return 1.0 / (1.0 + jnp.exp(-x))


_GELU_C = math.sqrt(2.0 / math.pi)


def _gelu_and_grad(x):
    x2 = x * x
    inner = _GELU_C * (x + 0.044715 * (x2 * x))
    t = jnp.tanh(inner)
    cdf = 0.5 * (1.0 + t)
    g = x * cdf
    dg = cdf + x * (0.5 * (1.0 - t * t)) * (_GELU_C * (1.0 + 3.0 * 0.044715 * x2))
    return g, dg


def _coords():
    return lax.axis_index("x"), lax.axis_index("y"), lax.axis_index("c")


def _dev_index(px, py, pc):
    return 4 * px + 2 * py + pc


def _allgather(blocks, name):
    n = len(blocks)

    def body(*refs):
        gather = _Gather(refs[:n], refs[n:2 * n], *refs[2 * n:])
        gather.issue()
        gather.finish()

    any_spec = pl.BlockSpec(memory_space=pl.ANY)
    return pl.pallas_call(
        body, name=name,
        out_shape=_gather_out_shapes(blocks),
        in_specs=[any_spec] * n, out_specs=[any_spec] * n,
        scratch_shapes=_gather_semaphores(n),
    )(*blocks)


def _gather_out_shapes(blocks):
    return [jax.ShapeDtypeStruct((N_DEV,) + b.shape, b.dtype) for b in blocks]


def _gather_semaphores(n):
    return [pltpu.SemaphoreType.DMA((n, 7)), pltpu.SemaphoreType.DMA((n, 7)), pltpu.SemaphoreType.DMA((n,))]


class _Gather:
    def __init__(self, ins, outs, send_sems, recv_sems, local_sems):
        self.ins, self.outs = ins, outs
        self.send_sems, self.recv_sems, self.local_sems = send_sems, recv_sems, local_sems
        self.n = len(ins)
        x, y, c = _coords()
        self.c = c
        self.me, self.sibling = (x, y, c), (x, y, 1 - c)
        self.chips = [(1 - x, y), (x, 1 - y), (1 - x, 1 - y)]

    def _copy(self, a, k, block, to, src=None):
        dst = self.outs[a].at[_dev_index(*block)]
        return pltpu.make_async_remote_copy(
            src_ref=dst if src is None else src, dst_ref=dst,
            send_sem=self.send_sems.at[a, k], recv_sem=self.recv_sems.at[a, k],
            device_id=to, device_id_type=MESH)

    def _mine(self):
        return [pltpu.make_async_copy(self.ins[a], self.outs[a].at[_dev_index(*self.me)], self.local_sems.at[a])
                for a in range(self.n)]

    def _first(self):
        first = []
        for a in range(self.n):
            first.append(self._copy(a, 0, self.me, self.sibling, src=self.ins[a]))
            first += [self._copy(a, 1 + j, self.me, (*chip, self.c), src=self.ins[a])
                      for j, chip in enumerate(self.chips)]
        return first

    def issue(self):
        for cp in self._mine() + self._first():
            cp.start()

    def finish(self):
        c = self.c
        passed = []
        for j, chip in enumerate(self.chips):
            for a in range(self.n):
                self._copy(a, 1 + j, (*chip, c), self.me).wait_recv()
                fwd = self._copy(a, 4 + j, (*chip, c), self.sibling)
                fwd.start()
                passed.append(fwd)
        for a in range(self.n):
            self._copy(a, 0, self.sibling, self.me).wait_recv()
            for j, chip in enumerate(self.chips):
                self._copy(a, 4 + j, (*chip, 1 - c), self.me).wait_recv()
        for cp in self._first() + passed:
            cp.wait_send()
        for cp in self._mine():
            cp.wait()


def _push_sibling(arrs, name):
    n = len(arrs)

    def body(*refs):
        ins, outs = refs[:n], refs[n:2 * n]
        send_sems, recv_sems = refs[2 * n:]
        x, y, c = _coords()
        sibling = (x, y, 1 - c)
        copies = []
        for a in range(n):
            for k in range(4):
                copies.append(pltpu.make_async_remote_copy(
                    src_ref=ins[a].at[k, 1 - c], dst_ref=outs[a].at[k],
                    send_sem=send_sems.at[a, k], recv_sem=recv_sems.at[a, k],
                    device_id=sibling, device_id_type=MESH))
        for cp in copies:
            cp.start()
        for cp in copies:
            cp.wait()

    any_spec = pl.BlockSpec(memory_space=pl.ANY)
    return pl.pallas_call(
        body, name=name,
        out_shape=[jax.ShapeDtypeStruct((4,) + a.shape[2:], a.dtype) for a in arrs],
        in_specs=[any_spec] * n, out_specs=[any_spec] * n,
        scratch_shapes=[pltpu.SemaphoreType.DMA((n, 4)), pltpu.SemaphoreType.DMA((n, 4))],
    )(*arrs)


def _chip_push_copies(ins, outs, send_sems, recv_sems):
    x, y, c = _coords()
    chips = [(1 - x, y), (x, 1 - y), (1 - x, 1 - y)]
    return [pltpu.make_async_remote_copy(
        src_ref=ins[a].at[2 * px + py], dst_ref=outs[a].at[r],
        send_sem=send_sems.at[a, r], recv_sem=recv_sems.at[a, r],
        device_id=(px, py, c), device_id_type=MESH)
        for a in range(len(ins)) for r, (px, py) in enumerate(chips)]


def _chip_partial_sum(own, land, core, name):
    _, _, rows, cols = own.shape
    tr = rows

    def body(core_ref, own_ref, land_ref, out_ref):
        del core_ref
        out_ref[...] = (own_ref[...].astype(F32) + land_ref[...].astype(F32)).astype(out_ref.dtype)

    return pl.pallas_call(
        body, name=name,
        out_shape=jax.ShapeDtypeStruct((4, rows, cols), own.dtype),
        grid_spec=pltpu.PrefetchScalarGridSpec(
            num_scalar_prefetch=1, grid=(4, rows // tr),
            in_specs=[pl.BlockSpec((None, None, tr, cols), lambda k, r, core: (k, core[0], r, 0)),
                      pl.BlockSpec((None, tr, cols), lambda k, r, core: (k, r, 0))],
            out_specs=pl.BlockSpec((None, tr, cols), lambda k, r, core: (k, r, 0))),
        compiler_params=_cparams(("parallel", "parallel")),
    )(core, own, land)


def _adamw_math(w, g, m, v):
    m = ADAM_B1 * m + (1.0 - ADAM_B1) * g
    v = ADAM_B2 * v + (1.0 - ADAM_B2) * (g * g)
    m_hat = m / (1.0 - ADAM_B1 ** ADAM_STEP)
    v_hat = v / (1.0 - ADAM_B2 ** ADAM_STEP)
    delta = -ADAM_LR * (m_hat / (jnp.sqrt(v_hat) + ADAM_EPS) + ADAM_WD * w)
    return delta, m, v


def _adam_shard(cps, lands, chip, w, m, v, name):
    rows, cols = w.shape
    tr = rows // 4
    nparts = len(cps)

    def body(chip_ref, *refs):
        del chip_ref
        cp_refs, land_refs = refs[:nparts], refs[nparts:2 * nparts]
        w_ref, m_ref, v_ref, g_out, d_out, m_out, v_out = refs[2 * nparts:]
        parts = []
        for cp_ref, land_ref in zip(cp_refs, land_refs):
            g_k = cp_ref[...].astype(F32)
            for r in range(3):
                g_k = g_k + land_ref[r].astype(F32)
            parts.append(g_k)
        g = parts[0] if nparts == 1 else jnp.concatenate(parts, axis=1)
        delta, m_new, v_new = _adamw_math(w_ref[...], g, m_ref[...], v_ref[...])
        g_out[...] = g
        d_out[...] = delta
        m_out[...] = m_new
        v_out[...] = v_new

    tile = pl.BlockSpec((tr, cols), lambda r, chip: (r, 0))
    out = jax.ShapeDtypeStruct((rows, cols), F32)
    return pl.pallas_call(
        body, name=name, out_shape=[out] * 4,
        grid_spec=pltpu.PrefetchScalarGridSpec(
            num_scalar_prefetch=1, grid=(rows // tr,),
            in_specs=[pl.BlockSpec((None, tr, a.shape[2]), lambda r, chip: (chip[0], r, 0)) for a in cps]
            + [pl.BlockSpec((3, tr, a.shape[2]), lambda r, chip: (0, r, 0)) for a in lands]
            + [tile, tile, tile],
            out_specs=[tile] * 4),
        compiler_params=_cparams(("parallel",)),
    )(chip, *cps, *lands, w, m, v)


def _adam_small(parts, ws, ms, vs, loss_parts, name):
    n = len(ws)

    def body(*refs):
        p_refs, w_refs, m_refs, v_refs = refs[:n], refs[n:2 * n], refs[2 * n:3 * n], refs[3 * n:4 * n]
        loss_ref, outs, loss_out = refs[4 * n], refs[4 * n + 1:-1], refs[-1]
        total = loss_ref[0]
        for dev in range(1, N_DEV):
            total = total + loss_ref[dev]
        loss_out[...] = total
        for i in range(n):
            g = p_refs[i][0]
            for dev in range(1, N_DEV):
                g = g + p_refs[i][dev]
            if g.shape[0] != w_refs[i].shape[0]:
                g = jnp.sum(g, axis=0, keepdims=True)
            delta, m_new, v_new = _adamw_math(w_refs[i][...], g, m_refs[i][...], v_refs[i][...])
            outs[i][...] = g
            outs[n + i][...] = delta
            outs[2 * n + i][...] = m_new
            outs[3 * n + i][...] = v_new

    out_shapes = [jax.ShapeDtypeStruct(w.shape, F32) for w in ws] * 4
    out_shapes.append(jax.ShapeDtypeStruct(loss_parts.shape[1:], F32))
    res = pl.pallas_call(body, name=name, out_shape=out_shapes, compiler_params=_cparams())(
        *parts, *ws, *ms, *vs, loss_parts)
    return [res[k * n:(k + 1) * n] for k in range(4)], res[-1]


def _rmsnorm_fwd(x, g, riders, name):
    s, d = x.shape
    ts = 512
    nt = s // ts
    nr = len(riders)

    def body(x_ref, g_ref, *rest):
        rider_in, (h_ref, ht_ref) = rest[:nr], rest[nr:nr + 2]
        rider_out, sems = rest[nr + 2:2 * nr + 2], rest[2 * nr + 2:]
        i = pl.program_id(0)
        gather = _Gather(rider_in, rider_out, *sems)

        @pl.when(i == 0)
        def _():
            gather.issue()

        xv = x_ref[...]
        rstd = lax.rsqrt(jnp.mean(xv * xv, axis=-1, keepdims=True) + EPS)
        h = xv * rstd * g_ref[...]
        h_ref[...] = h.astype(BF16)
        ht_ref[...] = h.T.astype(BF16)

        @pl.when(i == nt - 1)
        def _():
            gather.finish()

    any_spec = pl.BlockSpec(memory_space=pl.ANY)
    res = pl.pallas_call(
        body, name=name,
        out_shape=[jax.ShapeDtypeStruct((s, d), BF16), jax.ShapeDtypeStruct((d, s), BF16)]
        + _gather_out_shapes(riders),
        grid=(nt,),
        in_specs=[pl.BlockSpec((ts, d), lambda i: (i, 0)), pl.BlockSpec((1, d), lambda i: (0, 0))] + [any_spec] * nr,
        out_specs=[pl.BlockSpec((ts, d), lambda i: (i, 0)), pl.BlockSpec((d, ts), lambda i: (0, i))]
        + [any_spec] * nr,
        scratch_shapes=_gather_semaphores(nr),
        compiler_params=_cparams(("arbitrary",)),
    )(x, g, *riders)
    return res[0], res[1], res[2:]


def _inproj(h, wt, col0, ncols, out_dtype, riders, name):
    s, d = h.shape
    tn = 256
    j0 = col0 // tn
    nj = ncols // tn
    nr = len(riders)

    def body(h_ref, w_ref, *rest):
        rider_in, o_ref, rider_out, sems = rest[:nr], rest[nr], rest[nr + 1:2 * nr + 1], rest[2 * nr + 1:]
        j = pl.program_id(0)
        if nr:
            gather = _Gather(rider_in, rider_out, *sems)

            @pl.when(j == 0)
            def _():
                gather.issue()

        o_ref[...] = _dot_nt(h_ref[...], w_ref[...]).astype(o_ref.dtype)

        if nr:
            @pl.when(j == nj - 1)
            def _():
                gather.finish()

    any_spec = pl.BlockSpec(memory_space=pl.ANY)
    res = pl.pallas_call(
        body, name=name, out_shape=[jax.ShapeDtypeStruct((s, ncols), out_dtype)] + _gather_out_shapes(riders),
        grid=(nj,),
        in_specs=[pl.BlockSpec((s, d), lambda j: (0, 0)), pl.BlockSpec((tn, d), lambda j: (j + j0, 0))]
        + [any_spec] * nr,
        out_specs=[pl.BlockSpec((s, tn), lambda j: (0, j))] + [any_spec] * nr,
        scratch_shapes=_gather_semaphores(nr) if nr else [],
        compiler_params=_cparams(("arbitrary",) if nr else ("parallel",)),
    )(h, wt, *riders)
    return res[0], res[1:]


def _dw_in(ht, dproj, half, smalls, cps, name):
    d, s = ht.shape
    dh = d // 2
    n = dproj.shape[1]
    tn = 512
    nj = n // tn
    ns, nc = len(smalls), len(cps)
    n_gather_sems = 3 if ns else 0

    def body(a_ref, b_ref, *rest):
        small_in, cp_in = rest[:ns], rest[ns:ns + nc]
        o_ref = rest[ns + nc]
        small_out, land_out = rest[ns + nc + 1:2 * ns + nc + 1], rest[2 * ns + nc + 1:2 * (ns + nc) + 1]
        sems = rest[2 * (ns + nc) + 1:]
        j = pl.program_id(0)
        gather = _Gather(small_in, small_out, *sems[:3]) if ns else None

        @pl.when(j == 0)
        def _():
            if gather is not None:
                gather.issue()
            for cp in _chip_push_copies(cp_in, land_out, *sems[n_gather_sems:]):
                cp.start()

        o_ref[...] = _dot(a_ref[...], b_ref[...]).T.astype(o_ref.dtype)

        @pl.when(j == nj - 1)
        def _():
            if gather is not None:
                gather.finish()
            for cp in _chip_push_copies(cp_in, land_out, *sems[n_gather_sems:]):
                cp.wait()

    any_spec = pl.BlockSpec(memory_space=pl.ANY)
    res = pl.pallas_call(
        body, name=name,
        out_shape=[jax.ShapeDtypeStruct((n, dh), BF16)] + _gather_out_shapes(smalls)
        + [jax.ShapeDtypeStruct((3,) + a.shape[1:], a.dtype) for a in cps],
        grid=(nj,),
        in_specs=[pl.BlockSpec((dh, s), lambda j: (half, 0)), pl.BlockSpec((s, tn), lambda j: (0, j))]
        + [any_spec] * (ns + nc),
        out_specs=[pl.BlockSpec((tn, dh), lambda j: (j, 0))] + [any_spec] * (ns + nc),
        scratch_shapes=(_gather_semaphores(ns) if ns else [])
        + [pltpu.SemaphoreType.DMA((nc, 3)), pltpu.SemaphoreType.DMA((nc, 3))],
        compiler_params=_cparams(("arbitrary",)),
    )(ht, dproj, *smalls, *cps)
    return res[0], res[1:1 + ns], res[1 + ns:]


def _dh_and_grad_x(dproj, wt, x, g, dx2, cps, name):
    s, n = dproj.shape
    d = wt.shape[1]
    tm, tk = min(s, 1024), 512
    nk = n // tk
    nm = s // tm
    nc = len(cps)

    def body(a_ref, w_ref, x_ref, g_ref, dx2_ref, *rest):
        cp_refs, gx_ref, dg_ref = rest[:nc], rest[nc], rest[nc + 1]
        land_refs = rest[nc + 2:2 * nc + 2]
        acc_ref, send_sems, recv_sems = rest[2 * nc + 2:]
        i, k = pl.program_id(0), pl.program_id(1)

        @pl.when((i == 0) & (k == 0))
        def _():
            for cp in _chip_push_copies(cp_refs, land_refs, send_sems, recv_sems):
                cp.start()
            dg_ref[...] = jnp.zeros_like(dg_ref)

        @pl.when(k == 0)
        def _():
            acc_ref[...] = jnp.zeros_like(acc_ref)

        acc_ref[...] += _dot(a_ref[...], w_ref[...])

        @pl.when(k == nk - 1)
        def _():
            dh = acc_ref[...]
            xv = x_ref[...]
            rstd = lax.rsqrt(jnp.mean(xv * xv, axis=-1, keepdims=True) + EPS)
            xhat = xv * rstd
            dg_ref[...] += jnp.sum((dh * xhat).reshape(tm // 8, 8, d), axis=0)
            dxh = dh * g_ref[...]
            gx_ref[...] = dx2_ref[...] + rstd * (dxh - xhat * jnp.mean(dxh * xhat, axis=-1, keepdims=True))

        @pl.when((i == nm - 1) & (k == nk - 1))
        def _():
            for cp in _chip_push_copies(cp_refs, land_refs, send_sems, recv_sems):
                cp.wait()

    any_spec = pl.BlockSpec(memory_space=pl.ANY)
    res = pl.pallas_call(
        body, name=name,
        out_shape=[jax.ShapeDtypeStruct((s, d), F32), jax.ShapeDtypeStruct((8, d), F32)]
        + [jax.ShapeDtypeStruct((3,) + a.shape[1:], a.dtype) for a in cps],
        grid=(nm, nk),
        in_specs=[pl.BlockSpec((tm, tk), lambda i, k: (i, k)), pl.BlockSpec((tk, d), lambda i, k: (k, 0)),
                  pl.BlockSpec((tm, d), lambda i, k: (i, 0)), pl.BlockSpec((1, d), lambda i, k: (0, 0)),
                  pl.BlockSpec((tm, d), lambda i, k: (i, 0))] + [any_spec] * nc,
        out_specs=[pl.BlockSpec((tm, d), lambda i, k: (i, 0)), pl.BlockSpec((8, d), lambda i, k: (0, 0))]
        + [any_spec] * nc,
        scratch_shapes=[pltpu.VMEM((tm, d), F32), pltpu.SemaphoreType.DMA((nc, 3)), pltpu.SemaphoreType.DMA((nc, 3))],
        compiler_params=_cparams(("arbitrary", "arbitrary")),
    )(dproj, wt, x, g, dx2, *cps)
    return res[0], res[1], res[2:]


def _log_sigmoids(z):
    l1p = jnp.log(1.0 + jnp.exp(-jnp.abs(z)))
    ls = jnp.minimum(z, 0.0) - l1p
    return ls, ls - z


def _strict_lower_ones(n):
    row = lax.broadcasted_iota(jnp.int32, (n, n), 0)
    col = lax.broadcasted_iota(jnp.int32, (n, n), 1)
    return row, col, (row > col).astype(BF16)


def _attn_fwd(qkv, name):
    s = qkv.shape[0]
    tb, tt = ATT_BLOCK, ATT_TILE
    nq = s // tb
    per_block = tb // tt
    assert per_block == 2 and s // tt <= LANES, "two query tiles per grid step; one lane of saved carry per key tile"

    def body(q_ref, k_ref, v_ref, o_ref, rs_ref, extra_ref, acc_ref, r_ref, rsv_ref, rmax_ref):
        i = pl.program_id(1)
        lane = lax.broadcasted_iota(jnp.int32, (tt, LANES), 1)
        hmask = [lane < HEAD_DIM, lane >= HEAD_DIM]
        row, col, tri = _strict_lower_ones(tt)
        tri2 = jnp.concatenate([tri, tri], axis=0)
        below = col < row
        qrows = [slice(u * tt, (u + 1) * tt) for u in range(per_block)]
        qm = [[jnp.where(m, q_ref[qrows[u], :], jnp.zeros((), BF16)) * jnp.asarray(HEAD_DIM ** -0.5, BF16)
               for m in hmask] for u in range(per_block)]
        acc_ref[...] = jnp.zeros_like(acc_ref)
        r_ref[...] = jnp.zeros_like(r_ref)
        rsv_ref[...] = jnp.full_like(rsv_ref, R_UNREACHED)

        def sweep(tiles, chains):
            heads, nch = range(2), range(len(chains))
            kv = []
            for t in tiles:
                rows = pl.ds(pl.multiple_of(t * tt, tt), tt)
                kv.append((k_ref[rows, :], v_ref[rows, :]))
            z = [[_dot_nt(qm[u][h], kv[ti][0]) for h in heads] for u, ti, _ in chains]
            ls, lk = [], []
            for n in nch:
                pairs = [_log_sigmoids(z[n][h]) for h in heads]
                ls.append([p[0] for p in pairs])
                lk.append([jnp.where(below, p[1], 0.0) if chains[n][2] else p[1] for p in pairs])
            cur = {u: [r_ref[h, qrows[u], :] for h in heads] for u in sorted({c[0] for c in chains})}
            r = []
            for n, (u, _, _) in enumerate(chains):
                r.append(cur[u])
                cur[u] = [cur[u][h] + jnp.sum(lk[n][h], axis=1, keepdims=True) for h in heads]
            for u in cur:
                rmax_ref[u] = jnp.max(jnp.maximum(cur[u][0], cur[u][1]))
            suffix = [[_dot(jnp.concatenate(_split_hi_lo(lk[n][h]), axis=1), tri2) for h in heads] for n in nch]
            w = []
            for n in nch:
                w_n = [jnp.exp(ls[n][h] + suffix[n][h] + r[n][h]) for h in heads]
                if chains[n][2]:
                    w_n = [jnp.where(below, w_h, 0.0) for w_h in w_n]
                w.append([w_h.astype(BF16) for w_h in w_n])
            pv = [[_dot(w[n][h], kv[chains[n][1]][1]) for h in heads] for n in nch]
            for u in cur:
                mine = [n for n in nch if chains[n][0] == u]
                for h in heads:
                    acc_ref[h, qrows[u], :] += functools.reduce(lambda a, b: a + b, [pv[n][h] for n in mine])
                    for n in mine:
                        if not chains[n][2]:
                            t = tiles[chains[n][1]]
                            rsv_ref[h, qrows[u], :] = jnp.where(lane == t, r[n][h], rsv_ref[h, qrows[u], :])
                    r_ref[h, qrows[u], :] = cur[u][h]

        first = per_block * i

        @pl.when(i == 0)
        def _():
            sweep([0, 1], [(0, 0, True), (1, 1, True), (1, 0, False)])

        @pl.when(i > 0)
        def _():
            sweep([first - 1, first, first + 1], [(0, 1, True), (1, 2, True), (0, 0, False), (1, 1, False)])

        for u in range(per_block):
            n_left = first + u - 1

            def live(c, n_left=n_left):
                jj, rmax = c
                return (jj < n_left) & (rmax >= CARRY_FLOOR)

            def step(c, u=u, n_left=n_left):
                jj, _ = c
                sweep([n_left - 1 - jj], [(u, 0, False)])
                return jj + 1, rmax_ref[u]

            swept, _ = lax.while_loop(live, step, (jnp.int32(0), rmax_ref[u]))
            extra_ref[pl.program_id(0), first + u] = swept.astype(F32)

        lane_b = lax.broadcasted_iota(jnp.int32, (tb, LANES), 1)
        o_ref[...] = jnp.where(lane_b < HEAD_DIM, acc_ref[0], acc_ref[1])
        rs_ref[...] = rsv_ref[...]

    return pl.pallas_call(
        body, name=name,
        out_shape=[jax.ShapeDtypeStruct((s, D_SB), F32), jax.ShapeDtypeStruct((N_HEADS, s, LANES), F32),
                   jax.ShapeDtypeStruct((N_PAIRS, s // tt), F32)],
        grid=(N_PAIRS, nq),
        in_specs=[pl.BlockSpec((tb, LANES), lambda p, i: (i, p)),
                  pl.BlockSpec((s, LANES), lambda p, i: (0, N_PAIRS + p)),
                  pl.BlockSpec((s, LANES), lambda p, i: (0, 2 * N_PAIRS + p))],
        out_specs=[pl.BlockSpec((tb, LANES), lambda p, i: (i, p)),
                   pl.BlockSpec((2, tb, LANES), lambda p, i: (p, i, 0)),
                   pl.BlockSpec(memory_space=pltpu.SMEM)],
        scratch_shapes=[pltpu.VMEM((2, tb, LANES), F32), pltpu.VMEM((2, tb, LANES), F32),
                        pltpu.VMEM((2, tb, LANES), F32), pltpu.SMEM((per_block,), F32)],
        compiler_params=_cparams(("arbitrary", "arbitrary")),
    )(qkv, qkv, qkv)


def _attn_bwd(qkv, do, rs, extra, dproj, name):
    s = qkv.shape[0]
    tb, tt = ATT_BLOCK, ATT_TILE
    nq = s // tb
    per_block = tb // tt
    scale = HEAD_DIM ** -0.5

    def body(q_ref, k_ref, v_ref, do_ref, rs_ref, extra_ref, dproj_hbm, out_hbm, dq_acc, dk_acc, dv_acc, dqi_ref,
             pc_ref, stage_ref, out_sems):
        del dproj_hbm
        pair = pl.program_id(0)
        lane = lax.broadcasted_iota(jnp.int32, (tt, LANES), 1)
        hmask = [lane < HEAD_DIM, lane >= HEAD_DIM]
        row, col, tri = _strict_lower_ones(tt)
        tri2 = jnp.concatenate([tri, tri], axis=0)
        triu = (row < col).astype(BF16)
        below = col < row
        zero = jnp.zeros((), BF16)
        qrows = [slice(u * tt, (u + 1) * tt) for u in range(per_block)]
        dq_acc[...] = jnp.zeros_like(dq_acc)
        dk_acc[...] = jnp.zeros_like(dk_acc)
        dv_acc[...] = jnp.zeros_like(dv_acc)

        def qblock(i, carry):
            block_rows = pl.ds(pl.multiple_of(i * tb, tb), tb)
            q2 = q_ref[block_rows, :]
            do2 = do_ref[block_rows, :]
            qm = [[jnp.where(m, q2[qrows[u]], zero) * jnp.asarray(scale, BF16) for m in hmask]
                  for u in range(per_block)]
            dom = [[jnp.where(m, do2[qrows[u]], zero) for m in hmask] for u in range(per_block)]
            rs_i = [[rs_ref[h, pl.ds(pl.multiple_of(i * tb + u * tt, tt), tt), :] for h in range(2)]
                    for u in range(per_block)]
            dqi_ref[...] = jnp.zeros_like(dqi_ref)
            pc_ref[...] = jnp.zeros_like(pc_ref)
            first = per_block * i

            def sweep(tiles, chains):
                heads, nch = range(2), range(len(chains))
                krows = [pl.ds(pl.multiple_of(t * tt, tt), tt) for t in tiles]
                k2 = [k_ref[rows, :] for rows in krows]
                v2 = [v_ref[rows, :] for rows in krows]
                z = [[_dot_nt(qm[u][h], k2[ti]) for h in heads] for u, ti, _ in chains]
                dw = [[_dot_nt(dom[u][h], v2[ti]) for h in heads] for u, ti, _ in chains]
                ls, lk = [], []
                for n in nch:
                    pairs = [_log_sigmoids(z[n][h]) for h in heads]
                    ls.append([p[0] for p in pairs])
                    lk.append([jnp.where(below, p[1], 0.0) if chains[n][2] else p[1] for p in pairs])
                suffix = [[_dot(jnp.concatenate(_split_hi_lo(lk[n][h]), axis=1), tri2) for h in heads] for n in nch]
                w, g = [], []
                for n, (u, ti, diag) in enumerate(chains):
                    w_n = []
                    for h in heads:
                        logw = ls[n][h] + suffix[n][h]
                        if diag:
                            w_n.append(jnp.where(below, jnp.exp(logw), 0.0))
                        else:
                            carry_in = jnp.sum(jnp.where(lane == tiles[ti], rs_i[u][h], 0.0), axis=1, keepdims=True)
                            w_n.append(jnp.exp(logw + carry_in))
                    w.append([w_h.astype(BF16) for w_h in w_n])
                    g.append([w_n[h] * dw[n][h] for h in heads])
                prefix = [[_dot(g[n][h].astype(BF16), triu) for h in heads] for n in nch]
                cur = {u: [pc_ref[h, qrows[u], :] for h in heads] for u in sorted({c[0] for c in chains})}
                pc = []
                for n, (u, _, _) in enumerate(chains):
                    pc.append(cur[u])
                    cur[u] = [cur[u][h] + jnp.sum(g[n][h], axis=1, keepdims=True) for h in heads]
                dzb = []
                for n in nch:
                    dz_n = []
                    for h in heads:
                        sig = jnp.exp(ls[n][h])
                        dz = g[n][h] - sig * (g[n][h] + prefix[n][h] + pc[n][h])
                        if chains[n][2]:
                            dz = jnp.where(below, dz, 0.0)
                        dz_n.append(dz.astype(BF16))
                    dzb.append(dz_n)
                for u in cur:
                    for h in heads:
                        pc_ref[h, qrows[u], :] = cur[u][h]
                dq = [[_dot(dzb[n][h], jnp.where(hmask[h], k2[chains[n][1]], zero)) for h in heads] for n in nch]
                dk = [[_dot_tn(dzb[n][h], qm[chains[n][0]][h]) for h in heads] for n in nch]
                dv = [[_dot_tn(w[n][h], dom[chains[n][0]][h]) for h in heads] for n in nch]
                add = lambda a, b: a + b
                for u in cur:
                    dqi_ref[qrows[u], :] += functools.reduce(
                        add, [dq[n][h] for n in nch if chains[n][0] == u for h in heads])
                for ti in range(len(tiles)):
                    mine = [n for n in nch if chains[n][1] == ti]
                    dk_acc[krows[ti], :] += functools.reduce(add, [dk[n][h] for n in mine for h in heads])
                    dv_acc[krows[ti], :] += functools.reduce(add, [dv[n][h] for n in mine for h in heads])

            for u in range(per_block):
                n_left = first + u - 1
                n_extra = jnp.clip(extra_ref[pair, first + u].astype(jnp.int32), 0, jnp.maximum(n_left, 0))

                def step(t, c, u=u):
                    sweep([t], [(u, 0, False)])
                    return c

                lax.fori_loop(n_left - n_extra, n_left, step, 0)

            @pl.when(i == 0)
            def _():
                sweep([0, 1], [(0, 0, True), (1, 0, False), (1, 1, True)])

            @pl.when(i > 0)
            def _():
                sweep([first - 1, first, first + 1], [(0, 0, False), (1, 1, False), (0, 1, True), (1, 2, True)])

            dq_acc[block_rows, :] += dqi_ref[...] * scale
            return carry

        lax.fori_loop(0, nq, qblock, 0)
        copies = []
        for t, acc in enumerate((dq_acc, dk_acc, dv_acc)):
            stage_ref[t] = acc[...].astype(BF16)
            col0 = pl.multiple_of(t * D_SB + pair * LANES, LANES)
            copies.append(pltpu.make_async_copy(stage_ref.at[t], out_hbm.at[:, pl.ds(col0, LANES)], out_sems.at[t]))
        for cp in copies:
            cp.start()
        for cp in copies:
            cp.wait()

    return pl.pallas_call(
        body, name=name,
        out_shape=jax.ShapeDtypeStruct(dproj.shape, BF16),
        grid=(N_PAIRS,),
        in_specs=[pl.BlockSpec((s, LANES), lambda p: (0, p)),
                  pl.BlockSpec((s, LANES), lambda p: (0, N_PAIRS + p)),
                  pl.BlockSpec((s, LANES), lambda p: (0, 2 * N_PAIRS + p)),
                  pl.BlockSpec((s, LANES), lambda p: (0, p)),
                  pl.BlockSpec((2, s, LANES), lambda p: (p, 0, 0)),
                  pl.BlockSpec(memory_space=pltpu.SMEM),
                  pl.BlockSpec(memory_space=pl.ANY)],
        out_specs=pl.BlockSpec(memory_space=pl.ANY),
        scratch_shapes=[pltpu.VMEM((s, LANES), F32), pltpu.VMEM((s, LANES), F32), pltpu.VMEM((s, LANES), F32),
                        pltpu.VMEM((tb, LANES), F32), pltpu.VMEM((2, tb, LANES), F32),
                        pltpu.VMEM((3, s, LANES), BF16), pltpu.SemaphoreType.DMA((3,))],
        input_output_aliases={6: 0},
        compiler_params=_cparams(("arbitrary",)),
    )(qkv, qkv, qkv, do, rs, extra, dproj)


def _mid(o, rest, x, tgt, wua, wub, wout, fg, lng, lnb, wsp, bfull, gavg, name):
    s, d = x.shape
    ts = 256
    nt = s // ts
    nchunk = ts // SGU_CHUNK
    n_rest = rest.shape[1]

    def body(o_ref, rest_ref, x_ref, t_ref, wua_ref, wub_ref, wout_ref, fg_ref, lng_ref, lnb_ref, wsp_ref, bfull_ref,
             gavg_ref, loss_ref, dx2_ref, do_ref, dproj_ref, dwua_ref, dwub_ref, dwout_ref, dfg_ref, dlng_ref,
             dlnb_ref, dwsp_ref, dbfull_ref):
        step = pl.program_id(0)

        @pl.when(step == 0)
        def _():
            for ref in (loss_ref, dwua_ref, dwub_ref, dwout_ref, dfg_ref, dlng_ref, dlnb_ref, dwsp_ref, dbfull_ref):
                ref[...] = jnp.zeros_like(ref)

        gavg = gavg_ref[...]

        def gmean(a):
            return _dot(a.astype(BF16), gavg)

        def colsum8(a):
            return jnp.sum(a.reshape(ts // 8, 8, a.shape[1]), axis=0)

        z_a = rest_ref[:, 0:512]
        u_b = rest_ref[:, 512:1024]
        v_b = rest_ref[:, 1024:1536]
        z_b = rest_ref[:, 1536:2048]
        g_a = rest_ref[:, 2048:2048 + d]
        g_b = rest_ref[:, 2048 + d:2048 + 2 * d]
        ov = o_ref[...]
        sa = _sigmoid(z_a)
        silu_a = z_a * sa
        y_a = ov * silu_a
        ug, dug_du = _gelu_and_grad(u_b)
        vg, dvg_dv = _gelu_and_grad(v_b)
        mu = gmean(vg)
        cen = vg - mu
        rstd_g = lax.rsqrt(gmean(cen * cen) + EPS)
        vhat = cen * rstd_g
        vn = vhat * lng_ref[...] + lnb_ref[...]
        vnb = vn.astype(BF16)

        t_idx = lax.broadcasted_iota(jnp.int32, (SGU_CHUNK, SGU_CHUNK), 0)
        s_idx = lax.broadcasted_iota(jnp.int32, (SGU_CHUNK, SGU_CHUNK), 1)
        causal = (s_idx // CHUNK) <= (t_idx // CHUNK)
        wm = [jnp.where(causal, wsp_ref[g], 0.0) for g in range(N_GROUPS)]
        wmb = [w.astype(BF16) for w in wm]
        wmtb = [w.T.astype(BF16) for w in wm]
        lane = lax.broadcasted_iota(jnp.int32, (SGU_CHUNK, LANES), 1)
        first = lane < GROUP_DIM
        bfull = bfull_ref[...]

        mixed_rows = []
        for n in range(nchunk):
            r0, r1 = n * SGU_CHUNK, (n + 1) * SGU_CHUNK
            pieces = []
            for p in range(N_GROUPS // 2):
                blk = vnb[r0:r1, p * LANES:(p + 1) * LANES]
                pieces.append(jnp.where(first, _dot(wmb[2 * p], blk), _dot(wmb[2 * p + 1], blk)))
            mixed_rows.append(jnp.concatenate(pieces, axis=1) + bfull)
        mixed = jnp.concatenate(mixed_rows, axis=0)
        sg = ug * mixed
        sb = _sigmoid(z_b)
        silu_b = z_b * sb
        y_b = sg * silu_b
        y_ab = y_a.astype(BF16)
        y_bb = y_b.astype(BF16)
        p_a = _dot(y_ab, wua_ref[...])
        p_b = _dot(y_bb, wub_ref[...])
        ga_s = _sigmoid(g_a)
        gb_s = _sigmoid(g_b)
        merged_b = (ga_s * p_a + gb_s * p_b).astype(BF16)
        x2 = x_ref[...] + _dot(merged_b, wout_ref[...])
        rstd = lax.rsqrt(jnp.mean(x2 * x2, axis=-1, keepdims=True) + EPS)
        xhat = x2 * rstd
        fg_v = fg_ref[...]
        diff = xhat * fg_v - t_ref[...]
        loss_ref[...] += 0.5 * jnp.sum(jnp.sum(diff * diff, axis=-1, keepdims=True) * (1.0 / d))

        dy = diff * (1.0 / d)
        dfg_ref[...] += colsum8(dy * xhat)
        dxh = dy * fg_v
        dx2 = rstd * (dxh - xhat * jnp.mean(dxh * xhat, axis=-1, keepdims=True))
        dx2_ref[...] = dx2
        dx2b = dx2.astype(BF16)
        dwout_ref[...] += _dot_tn(merged_b, dx2b)
        dmerged = _dot_nt(dx2b, wout_ref[...])
        dp_a = dmerged * ga_s
        dp_b = dmerged * gb_s
        dproj_ref[:, QKV_COLS + 2048:QKV_COLS + 2048 + d] = (dmerged * p_a * (ga_s * (1.0 - ga_s))).astype(BF16)
        dproj_ref[:, QKV_COLS + 2048 + d:QKV_COLS + 2048 + 2 * d] = (dmerged * p_b * (gb_s * (1.0 - gb_s))).astype(BF16)
        dp_ab = dp_a.astype(BF16)
        dp_bb = dp_b.astype(BF16)
        dwua_ref[...] += _dot_tn(y_ab, dp_ab)
        dwub_ref[...] += _dot_tn(y_bb, dp_bb)
        dy_a = _dot_nt(dp_ab, wua_ref[...])
        dy_b = _dot_nt(dp_bb, wub_ref[...])
        do_ref[...] = (dy_a * silu_a).astype(BF16)
        dproj_ref[:, QKV_COLS:QKV_COLS + 512] = (dy_a * ov * (sa * (1.0 + z_a * (1.0 - sa)))).astype(BF16)
        dsg = dy_b * silu_b
        dproj_ref[:, QKV_COLS + 1536:QKV_COLS + 2048] = (dy_b * sg * (sb * (1.0 + z_b * (1.0 - sb)))).astype(BF16)
        dproj_ref[:, QKV_COLS + 512:QKV_COLS + 1024] = (dsg * mixed * dug_du).astype(BF16)
        dmixed = dsg * ug
        dmb = dmixed.astype(BF16)
        zero = jnp.zeros((), BF16)
        dvn_rows = []
        db = jnp.zeros((SGU_CHUNK, D_SGU), F32)
        for n in range(nchunk):
            r0, r1 = n * SGU_CHUNK, (n + 1) * SGU_CHUNK
            db = db + dmixed[r0:r1, :]
            pieces = []
            for p in range(N_GROUPS // 2):
                cols = slice(p * LANES, (p + 1) * LANES)
                dm_blk = dmb[r0:r1, cols]
                vn_blk = vnb[r0:r1, cols]
                dwsp_ref[2 * p] += _dot_nt(jnp.where(first, dm_blk, zero), vn_blk)
                dwsp_ref[2 * p + 1] += _dot_nt(jnp.where(first, zero, dm_blk), vn_blk)
                pieces.append(jnp.where(first, _dot(wmtb[2 * p], dm_blk), _dot(wmtb[2 * p + 1], dm_blk)))
            dvn_rows.append(jnp.concatenate(pieces, axis=1))
        dbfull_ref[...] += db
        dvn = jnp.concatenate(dvn_rows, axis=0)
        dlng_ref[...] += colsum8(dvn * vhat)
        dlnb_ref[...] += colsum8(dvn)
        dvhat = dvn * lng_ref[...]
        dcen = rstd_g * (dvhat - gmean(dvhat) - vhat * gmean(dvhat * vhat))
        dproj_ref[:, QKV_COLS + 1024:QKV_COLS + 1536] = (dcen * dvg_dv).astype(BF16)

        @pl.when(step == nt - 1)
        def _():
            for g in range(N_GROUPS):
                dwsp_ref[g] = jnp.where(causal, dwsp_ref[g], 0.0)

    def tile(cols):
        return pl.BlockSpec((ts, cols), lambda i: (i, 0))

    def whole(shape):
        return pl.BlockSpec(shape, lambda i: (0,) * len(shape))

    out_shapes = [
        jax.ShapeDtypeStruct((8, LANES), F32),
        jax.ShapeDtypeStruct((s, d), F32),
        jax.ShapeDtypeStruct((s, D_SB), BF16),
        jax.ShapeDtypeStruct((s, QKV_COLS + n_rest), BF16),
        jax.ShapeDtypeStruct((D_SB, d), F32),
        jax.ShapeDtypeStruct((D_SGU, d), F32),
        jax.ShapeDtypeStruct((d, d), F32),
        jax.ShapeDtypeStruct((8, d), F32),
        jax.ShapeDtypeStruct((8, D_SGU), F32),
        jax.ShapeDtypeStruct((8, D_SGU), F32),
        jax.ShapeDtypeStruct((N_GROUPS, SGU_CHUNK, SGU_CHUNK), F32),
        jax.ShapeDtypeStruct((SGU_CHUNK, D_SGU), F32),
    ]
    out_specs = [whole((8, LANES)), tile(d), tile(D_SB), tile(QKV_COLS + n_rest), whole((D_SB, d)), whole((D_SGU, d)),
                 whole((d, d)), whole((8, d)), whole((8, D_SGU)), whole((8, D_SGU)),
                 whole((N_GROUPS, SGU_CHUNK, SGU_CHUNK)), whole((SGU_CHUNK, D_SGU))]
    in_specs = [tile(D_SB), tile(n_rest), tile(d), tile(d), whole((D_SB, d)), whole((D_SGU, d)), whole((d, d)),
                whole((1, d)), whole((1, D_SGU)), whole((1, D_SGU)), whole((N_GROUPS, SGU_CHUNK, SGU_CHUNK)),
                whole((SGU_CHUNK, D_SGU)), whole((D_SGU, D_SGU))]
    return pl.pallas_call(
        body, name=name, out_shape=out_shapes, grid=(nt,), in_specs=in_specs, out_specs=out_specs,
        compiler_params=_cparams(("arbitrary",)),
    )(o, rest, x, tgt, wua, wub, wout, fg, lng, lnb, wsp, bfull, gavg)


def _small_reduce(dfg8, dlng8, dlnb8, dbfull, name):
    d = dfg8.shape[1]

    def body(dfg_ref, dlng_ref, dlnb_ref, dbfull_ref, fg_out, lng_out, lnb_out, b_out):
        fg_out[...] = jnp.sum(dfg_ref[...], axis=0, keepdims=True)
        lng_out[...] = jnp.sum(dlng_ref[...], axis=0, keepdims=True)
        lnb_out[...] = jnp.sum(dlnb_ref[...], axis=0, keepdims=True)
        grp = lax.broadcasted_iota(jnp.int32, (D_SGU, LANES), 0) // GROUP_DIM
        col = lax.broadcasted_iota(jnp.int32, (D_SGU, LANES), 1)
        sel = (grp == col).astype(BF16)
        a = dbfull_ref[...]
        hi, lo = _split_hi_lo(a)
        lo2 = (a - hi.astype(F32) - lo.astype(F32)).astype(BF16)
        by_pos = _dot(hi, sel) + _dot(lo, sel) + _dot(lo2, sel)
        b_out[...] = by_pos.T[0:N_GROUPS, :]

    return pl.pallas_call(
        body, name=name,
        out_shape=[jax.ShapeDtypeStruct((1, d), F32), jax.ShapeDtypeStruct((1, D_SGU), F32),
                   jax.ShapeDtypeStruct((1, D_SGU), F32), jax.ShapeDtypeStruct((N_GROUPS, SGU_CHUNK), F32)],
        compiler_params=_cparams(),
    )(dfg8, dlng8, dlnb8, dbfull)


def _block_major_cols(w):
    r, n = w.shape
    return jnp.transpose(w.reshape(r, N_DEV, n // N_DEV), (1, 0, 2))


def _from_block_major_cols(w):
    nb, r, c = w.shape
    return jnp.transpose(w, (1, 0, 2)).reshape(r, nb * c)


def kernel(x, norm_g, w_in, sgu_ln_g, sgu_ln_b, w_spatial, b_spatial, w_up_a, w_up_b, w_out, final_norm_g, loss_target, m_norm_g, m_w_in, m_sgu_ln_g, m_sgu_ln_b, m_w_spatial, m_b_spatial, m_w_up_a, m_w_up_b, m_w_out, m_final_norm_g, v_norm_g, v_w_in, v_sgu_ln_g, v_sgu_ln_b, v_w_spatial, v_b_spatial, v_w_up_a, v_w_up_b, v_w_out, v_final_norm_g):
    s, d = x.shape[1], x.shape[2]
    xs = x[0]
    tgt = loss_target[0]
    cx, cy, cc = _coords()
    core = jnp.reshape(cc, (1,)).astype(jnp.int32)
    chip = jnp.reshape(2 * cx + cy, (1,)).astype(jnp.int32)

    h, ht, (g_win,) = _rmsnorm_fwd(xs, norm_g, [jnp.transpose(w_in[0]).astype(BF16)], "norm")
    d_in = N_DEV * w_in.shape[2]
    w_full = g_win.reshape(d_in, d)
    qkv, _ = _inproj(h, w_full, 0, QKV_COLS, BF16, [], "inproj_qkv")
    rest, (g_wua, g_wub, g_wout) = _inproj(
        h, w_full, QKV_COLS, d_in - QKV_COLS, F32,
        [w_up_a[0].astype(BF16), w_up_b[0].astype(BF16), w_out[0].astype(BF16)], "inproj_rest")
    wua_full = _from_block_major_cols(g_wua)
    wub_full = _from_block_major_cols(g_wub)
    wout_full = g_wout.reshape(d, d)
    o, rs, extra = _attn_fwd(qkv, "attn_fwd")

    lng = sgu_ln_g.reshape(1, D_SGU)
    lnb = sgu_ln_b.reshape(1, D_SGU)
    bfull = jnp.repeat(jnp.transpose(b_spatial[0]), GROUP_DIM, axis=1)
    grp = jnp.arange(D_SGU) // GROUP_DIM
    gavg = jnp.where(grp[:, None] == grp[None, :], 1.0 / GROUP_DIM, 0.0).astype(BF16)
    (loss_b, dx2, do, dproj, dwua, dwub, dwout, dfg8, dlng8, dlnb8, dwsp, dbfull) = _mid(
        o, rest, xs, tgt, wua_full, wub_full, wout_full, final_norm_g.reshape(1, d), lng, lnb, w_spatial[0], bfull,
        gavg, "mid")

    def blocks42(a):
        return a.reshape((4, 2) + a.shape[1:])

    def chip_partials(own, tag):
        land = _push_sibling(own, "rs_sibling_" + tag)
        return [_chip_partial_sum(a, l, core, "cpsum_%s%d" % (tag, i)) for i, (a, l) in enumerate(zip(own, land))]

    cps_up = chip_partials([blocks42(_block_major_cols(dwua.astype(BF16))),
                            blocks42(_block_major_cols(dwub.astype(BF16))),
                            blocks42(dwout.astype(BF16).reshape(N_DEV, d // N_DEV, d))], "up")

    dproj = _attn_bwd(qkv, do, rs, extra, dproj, "attn_bwd")
    dfg, dlng, dlnb, db = _small_reduce(dfg8, dlng8, dlnb8, dbfull, "small_reduce")
    wsp_rows = N_GROUPS * SGU_CHUNK
    dwin_a, small_parts, land_up = _dw_in(
        ht, dproj, 0, [dfg, dlng, dlnb, db, dwsp.reshape(wsp_rows, SGU_CHUNK)], cps_up, "dwin_a")
    cps_in_a = chip_partials([dwin_a.reshape(4, 2, d_in // N_DEV, d // 2)], "in_a")
    dwin_b, _, land_in_a = _dw_in(ht, dproj, 1, [], cps_in_a, "dwin_b")
    cps_in_b = chip_partials([dwin_b.reshape(4, 2, d_in // N_DEV, d // 2)], "in_b")
    grad_x, dng8, land_in_b = _dh_and_grad_x(dproj, w_full, xs, norm_g, dx2, cps_in_b, "dh")

    ng_parts, loss_parts = _allgather([dng8, loss_b], "ag_tail")

    def small_layouts(ng, fg, lg, lb, bs, ws):
        return [ng.reshape(1, d), fg.reshape(1, d), lg.reshape(1, D_SGU), lb.reshape(1, D_SGU),
                bs.reshape(N_GROUPS, SGU_CHUNK), ws.reshape(wsp_rows, SGU_CHUNK)]

    sm, loss_sum = _adam_small(
        [ng_parts] + list(small_parts),
        small_layouts(norm_g, final_norm_g, sgu_ln_g, sgu_ln_b, b_spatial, w_spatial),
        small_layouts(m_norm_g, m_final_norm_g, m_sgu_ln_g, m_sgu_ln_b, m_b_spatial, m_w_spatial),
        small_layouts(v_norm_g, v_final_norm_g, v_sgu_ln_g, v_sgu_ln_b, v_b_spatial, v_w_spatial),
        loss_parts, "adam_small")
    small_shapes = [norm_g.shape, final_norm_g.shape, sgu_ln_g.shape, sgu_ln_b.shape, b_spatial.shape, w_spatial.shape]
    sm = [[a.reshape(shp) for a, shp in zip(kind, small_shapes)] for kind in sm]

    res = _adam_shard(cps_in_a + cps_in_b, list(land_in_a) + list(land_in_b), chip,
                      *[jnp.transpose(a[0]) for a in (w_in, m_w_in, v_w_in)], "adam0")
    big = [[jnp.transpose(r)[None] for r in res]]
    for i, (w, m, v) in enumerate([(w_up_a, m_w_up_a, v_w_up_a), (w_up_b, m_w_up_b, v_w_up_b),
                                   (w_out, m_w_out, v_w_out)]):
        res = _adam_shard([cps_up[i]], [land_up[i]], chip, w[0], m[0], v[0], "adam%d" % (i + 1))
        big.append([r[None] for r in res])

    loss = loss_sum[0, 0]

    def per_kind(kd):
        return [sm[kd][0], big[0][kd], sm[kd][2], sm[kd][3], sm[kd][5], sm[kd][4], big[1][kd], big[2][kd], big[3][kd],
                sm[kd][1]]

    return (loss, grad_x[None], *per_kind(0), *per_kind(1), *per_kind(2), *per_kind(3))
```

```python
import functools
import math

import jax
import jax.numpy as jnp
from jax import lax
from jax.experimental import pallas as pl
from jax.experimental.pallas import tpu as pltpu

F32 = jnp.float32
BF16 = jnp.bfloat16
MESH = pl.DeviceIdType.MESH

N_DEV = 8
N_HEADS = 8
HEAD_DIM = 64
D_SB = N_HEADS * HEAD_DIM
N_GROUPS = 8
GROUP_DIM = 64
D_SGU = N_GROUPS * GROUP_DIM
SGU_CHUNK = 128
CHUNK = 64
EPS = 1e-6
LANES = 128
N_PAIRS = N_HEADS // 2
QKV_COLS = 3 * D_SB
ATT_BLOCK = 256
ATT_TILE = 128
CARRY_FLOOR = -90.0
R_UNREACHED = -1e30

ADAM_LR = 0.001
ADAM_B1 = 0.9
ADAM_B2 = 0.999
ADAM_EPS = 1e-08
ADAM_WD = 0.01
ADAM_STEP = 10

VMEM_LIMIT = 56 * 1024 * 1024


def _cparams(sem=None, vmem=VMEM_LIMIT):
    return pltpu.CompilerParams(dimension_semantics=sem, vmem_limit_bytes=vmem)


def _dot(a, b):
    return jnp.dot(a, b, preferred_element_type=F32)


def _dot_nt(a, b):
    return lax.dot_general(a, b, (((1,), (1,)), ((), ())), preferred_element_type=F32)


def _dot_tn(a, b):
    return lax.dot_general(a, b, (((0,), (0,)), ((), ())), preferred_element_type=F32)


def _split_hi_lo(a):
    hi = a.astype(BF16)
    lo = (a - hi.astype(F32)).astype(BF16)
    return hi, lo


def _sigmoid(x):
    return 1.0 / (1.0 + jnp.exp(-x))


_GELU_C = math.sqrt(2.0 / math.pi)


def _gelu_and_grad(x):
    x2 = x * x
    inner = _GELU_C * (x + 0.044715 * (x2 * x))
    t = jnp.tanh(inner)
    cdf = 0.5 * (1.0 + t)
    g = x * cdf
    dg = cdf + x * (0.5 * (1.0 - t * t)) * (_GELU_C * (1.0 + 3.0 * 0.044715 * x2))
    return g, dg


def _coords():
    return lax.axis_index("x"), lax.axis_index("y"), lax.axis_index("c")


def _dev_index(px, py, pc):
    return 4 * px + 2 * py + pc


def _allgather(blocks, name):
    n = len(blocks)

    def body(*refs):
        gather = _Gather(refs[:n], refs[n:2 * n], *refs[2 * n:])
        gather.issue()
        gather.finish()

    any_spec = pl.BlockSpec(memory_space=pl.ANY)
    return pl.pallas_call(
        body, name=name,
        out_shape=_gather_out_shapes(blocks),
        in_specs=[any_spec] * n, out_specs=[any_spec] * n,
        scratch_shapes=_gather_semaphores(n),
    )(*blocks)


def _gather_out_shapes(blocks):
    return [jax.ShapeDtypeStruct((N_DEV,) + b.shape, b.dtype) for b in blocks]


def _gather_semaphores(n):
    return [pltpu.SemaphoreType.DMA((n, 7)), pltpu.SemaphoreType.DMA((n, 7)), pltpu.SemaphoreType.DMA((n,))]


class _Gather:
    def __init__(self, ins, outs, send_sems, recv_sems, local_sems):
        self.ins, self.outs = ins, outs
        self.send_sems, self.recv_sems, self.local_sems = send_sems, recv_sems, local_sems
        self.n = len(ins)
        x, y, c = _coords()
        self.c = c
        self.me, self.sibling = (x, y, c), (x, y, 1 - c)
        self.chips = [(1 - x, y), (x, 1 - y), (1 - x, 1 - y)]

    def _copy(self, a, k, block, to, src=None):
        dst = self.outs[a].at[_dev_index(*block)]
        return pltpu.make_async_remote_copy(
            src_ref=dst if src is None else src, dst_ref=dst,
            send_sem=self.send_sems.at[a, k], recv_sem=self.recv_sems.at[a, k],
            device_id=to, device_id_type=MESH)

    def _mine(self):
        return [pltpu.make_async_copy(self.ins[a], self.outs[a].at[_dev_index(*self.me)], self.local_sems.at[a])
                for a in range(self.n)]

    def _first(self):
        first = []
        for a in range(self.n):
            first.append(self._copy(a, 0, self.me, self.sibling, src=self.ins[a]))
            first += [self._copy(a, 1 + j, self.me, (*chip, self.c), src=self.ins[a])
                      for j, chip in enumerate(self.chips)]
        return first

    def issue(self):
        for cp in self._mine() + self._first():
            cp.start()

    def finish(self):
        c = self.c
        passed = []
        for j, chip in enumerate(self.chips):
            for a in range(self.n):
                self._copy(a, 1 + j, (*chip, c), self.me).wait_recv()
                fwd = self._copy(a, 4 + j, (*chip, c), self.sibling)
                fwd.start()
                passed.append(fwd)
        for a in range(self.n):
            self._copy(a, 0, self.sibling, self.me).wait_recv()
            for j, chip in enumerate(self.chips):
                self._copy(a, 4 + j, (*chip, 1 - c), self.me).wait_recv()
        for cp in self._first() + passed:
            cp.wait_send()
        for cp in self._mine():
            cp.wait()


def _push_sibling(arrs, name):
    n = len(arrs)

    def body(*refs):
        ins, outs = refs[:n], refs[n:2 * n]
        send_sems, recv_sems = refs[2 * n:]
        x, y, c = _coords()
        sibling = (x, y, 1 - c)
        copies = []
        for a in range(n):
            for k in range(4):
                copies.append(pltpu.make_async_remote_copy(
                    src_ref=ins[a].at[k, 1 - c], dst_ref=outs[a].at[k],
                    send_sem=send_sems.at[a, k], recv_sem=recv_sems.at[a, k],
                    device_id=sibling, device_id_type=MESH))
        for cp in copies:
            cp.start()
        for cp in copies:
            cp.wait()

    any_spec = pl.BlockSpec(memory_space=pl.ANY)
    return pl.pallas_call(
        body, name=name,
        out_shape=[jax.ShapeDtypeStruct((4,) + a.shape[2:], a.dtype) for a in arrs],
        in_specs=[any_spec] * n, out_specs=[any_spec] * n,
        scratch_shapes=[pltpu.SemaphoreType.DMA((n, 4)), pltpu.SemaphoreType.DMA((n, 4))],
    )(*arrs)


def _chip_push_copies(ins, outs, send_sems, recv_sems):
    x, y, c = _coords()
    chips = [(1 - x, y), (x, 1 - y), (1 - x, 1 - y)]
    return [pltpu.make_async_remote_copy(
        src_ref=ins[a].at[2 * px + py], dst_ref=outs[a].at[r],
        send_sem=send_sems.at[a, r], recv_sem=recv_sems.at[a, r],
        device_id=(px, py, c), device_id_type=MESH)
        for a in range(len(ins)) for r, (px, py) in enumerate(chips)]


def _chip_partial_sum(own, land, core, name):
    _, _, rows, cols = own.shape
    tr = rows

    def body(core_ref, own_ref, land_ref, out_ref):
        del core_ref
        out_ref[...] = (own_ref[...].astype(F32) + land_ref[...].astype(F32)).astype(out_ref.dtype)

    return pl.pallas_call(
        body, name=name,
        out_shape=jax.ShapeDtypeStruct((4, rows, cols), own.dtype),
        grid_spec=pltpu.PrefetchScalarGridSpec(
            num_scalar_prefetch=1, grid=(4, rows // tr),
            in_specs=[pl.BlockSpec((None, None, tr, cols), lambda k, r, core: (k, core[0], r, 0)),
                      pl.BlockSpec((None, tr, cols), lambda k, r, core: (k, r, 0))],
            out_specs=pl.BlockSpec((None, tr, cols), lambda k, r, core: (k, r, 0))),
        compiler_params=_cparams(("parallel", "parallel")),
    )(core, own, land)


def _adamw_math(w, g, m, v):
    m = ADAM_B1 * m + (1.0 - ADAM_B1) * g
    v = ADAM_B2 * v + (1.0 - ADAM_B2) * (g * g)
    m_hat = m / (1.0 - ADAM_B1 ** ADAM_STEP)
    v_hat = v / (1.0 - ADAM_B2 ** ADAM_STEP)
    delta = -ADAM_LR * (m_hat / (jnp.sqrt(v_hat) + ADAM_EPS) + ADAM_WD * w)
    return delta, m, v


def _adam_shard(cps, lands, chip, w, m, v, name):
    rows, cols = w.shape
    tr = rows // 4
    nparts = len(cps)

    def body(chip_ref, *refs):
        del chip_ref
        cp_refs, land_refs = refs[:nparts], refs[nparts:2 * nparts]
        w_ref, m_ref, v_ref, g_out, d_out, m_out, v_out = refs[2 * nparts:]
        parts = []
        for cp_ref, land_ref in zip(cp_refs, land_refs):
            g_k = cp_ref[...].astype(F32)
            for r in range(3):
                g_k = g_k + land_ref[r].astype(F32)
            parts.append(g_k)
        g = parts[0] if nparts == 1 else jnp.concatenate(parts, axis=1)
        delta, m_new, v_new = _adamw_math(w_ref[...], g, m_ref[...], v_ref[...])
        g_out[...] = g
        d_out[...] = delta
        m_out[...] = m_new
        v_out[...] = v_new

    tile = pl.BlockSpec((tr, cols), lambda r, chip: (r, 0))
    out = jax.ShapeDtypeStruct((rows, cols), F32)
    return pl.pallas_call(
        body, name=name, out_shape=[out] * 4,
        grid_spec=pltpu.PrefetchScalarGridSpec(
            num_scalar_prefetch=1, grid=(rows // tr,),
            in_specs=[pl.BlockSpec((None, tr, a.shape[2]), lambda r, chip: (chip[0], r, 0)) for a in cps]
            + [pl.BlockSpec((3, tr, a.shape[2]), lambda r, chip: (0, r, 0)) for a in lands]
            + [tile, tile, tile],
            out_specs=[tile] * 4),
        compiler_params=_cparams(("parallel",)),
    )(chip, *cps, *lands, w, m, v)


def _adam_small(parts, ws, ms, vs, loss_parts, name):
    n = len(ws)

    def body(*refs):
        p_refs, w_refs, m_refs, v_refs = refs[:n], refs[n:2 * n], refs[2 * n:3 * n], refs[3 * n:4 * n]
        loss_ref, outs, loss_out = refs[4 * n], refs[4 * n + 1:-1], refs[-1]
        total = loss_ref[0]
        for dev in range(1, N_DEV):
            total = total + loss_ref[dev]
        loss_out[...] = total
        for i in range(n):
            g = p_refs[i][0]
            for dev in range(1, N_DEV):
                g = g + p_refs[i][dev]
            if g.shape[0] != w_refs[i].shape[0]:
                g = jnp.sum(g, axis=0, keepdims=True)
            delta, m_new, v_new = _adamw_math(w_refs[i][...], g, m_refs[i][...], v_refs[i][...])
            outs[i][...] = g
            outs[n + i][...] = delta
            outs[2 * n + i][...] = m_new
            outs[3 * n + i][...] = v_new

    out_shapes = [jax.ShapeDtypeStruct(w.shape, F32) for w in ws] * 4
    out_shapes.append(jax.ShapeDtypeStruct(loss_parts.shape[1:], F32))
    res = pl.pallas_call(body, name=name, out_shape=out_shapes, compiler_params=_cparams())(
        *parts, *ws, *ms, *vs, loss_parts)
    return [res[k * n:(k + 1) * n] for k in range(4)], res[-1]


def _rmsnorm_fwd(x, g, riders, name):
    s, d = x.shape
    ts = 512
    nt = s // ts
    nr = len(riders)

    def body(x_ref, g_ref, *rest):
        rider_in, (h_ref, ht_ref) = rest[:nr], rest[nr:nr + 2]
        rider_out, sems = rest[nr + 2:2 * nr + 2], rest[2 * nr + 2:]
        i = pl.program_id(0)
        gather = _Gather(rider_in, rider_out, *sems)

        @pl.when(i == 0)
        def _():
            gather.issue()

        xv = x_ref[...]
        rstd = lax.rsqrt(jnp.mean(xv * xv, axis=-1, keepdims=True) + EPS)
        h = xv * rstd * g_ref[...]
        h_ref[...] = h.astype(BF16)
        ht_ref[...] = h.T.astype(BF16)

        @pl.when(i == nt - 1)
        def _():
            gather.finish()

    any_spec = pl.BlockSpec(memory_space=pl.ANY)
    res = pl.pallas_call(
        body, name=name,
        out_shape=[jax.ShapeDtypeStruct((s, d), BF16), jax.ShapeDtypeStruct((d, s), BF16)]
        + _gather_out_shapes(riders),
        grid=(nt,),
        in_specs=[pl.BlockSpec((ts, d), lambda i: (i, 0)), pl.BlockSpec((1, d), lambda i: (0, 0))] + [any_spec] * nr,
        out_specs=[pl.BlockSpec((ts, d), lambda i: (i, 0)), pl.BlockSpec((d, ts), lambda i: (0, i))]
        + [any_spec] * nr,
        scratch_shapes=_gather_semaphores(nr),
        compiler_params=_cparams(("arbitrary",)),
    )(x, g, *riders)
    return res[0], res[1], res[2:]


def _inproj(h, wt, col0, ncols, out_dtype, riders, name):
    s, d = h.shape
    tn = 256
    j0 = col0 // tn
    nj = ncols // tn
    nr = len(riders)

    def body(h_ref, w_ref, *rest):
        rider_in, o_ref, rider_out, sems = rest[:nr], rest[nr], rest[nr + 1:2 * nr + 1], rest[2 * nr + 1:]
        j = pl.program_id(0)
        if nr:
            gather = _Gather(rider_in, rider_out, *sems)

            @pl.when(j == 0)
            def _():
                gather.issue()

        o_ref[...] = _dot_nt(h_ref[...], w_ref[...]).astype(o_ref.dtype)

        if nr:
            @pl.when(j == nj - 1)
            def _():
                gather.finish()

    any_spec = pl.BlockSpec(memory_space=pl.ANY)
    res = pl.pallas_call(
        body, name=name, out_shape=[jax.ShapeDtypeStruct((s, ncols), out_dtype)] + _gather_out_shapes(riders),
        grid=(nj,),
        in_specs=[pl.BlockSpec((s, d), lambda j: (0, 0)), pl.BlockSpec((tn, d), lambda j: (j + j0, 0))]
        + [any_spec] * nr,
        out_specs=[pl.BlockSpec((s, tn), lambda j: (0, j))] + [any_spec] * nr,
        scratch_shapes=_gather_semaphores(nr) if nr else [],
        compiler_params=_cparams(("arbitrary",) if nr else ("parallel",)),
    )(h, wt, *riders)
    return res[0], res[1:]


def _dw_in(ht, dproj, name):
    d, s = ht.shape
    n = dproj.shape[1]
    tn = 512

    def body(a_ref, b_ref, o_ref):
        o_ref[...] = _dot(a_ref[...], b_ref[...]).T.astype(o_ref.dtype)

    return pl.pallas_call(
        body, name=name, out_shape=jax.ShapeDtypeStruct((n, d), BF16), grid=(n // tn,),
        in_specs=[pl.BlockSpec((d, s), lambda j: (0, 0)), pl.BlockSpec((s, tn), lambda j: (0, j))],
        out_specs=pl.BlockSpec((tn, d), lambda j: (j, 0)),
        compiler_params=_cparams(("parallel",)),
    )(ht, dproj)


def _dh_and_grad_x(dproj, wt, x, g, dx2, cps, name):
    s, n = dproj.shape
    d = wt.shape[1]
    tm, tk = min(s, 1024), 512
    nk = n // tk
    nm = s // tm
    nc = len(cps)

    def body(a_ref, w_ref, x_ref, g_ref, dx2_ref, *rest):
        cp_refs, gx_ref, dg_ref = rest[:nc], rest[nc], rest[nc + 1]
        land_refs = rest[nc + 2:2 * nc + 2]
        acc_ref, send_sems, recv_sems = rest[2 * nc + 2:]
        i, k = pl.program_id(0), pl.program_id(1)

        @pl.when((i == 0) & (k == 0))
        def _():
            for cp in _chip_push_copies(cp_refs, land_refs, send_sems, recv_sems):
                cp.start()
            dg_ref[...] = jnp.zeros_like(dg_ref)

        @pl.when(k == 0)
        def _():
            acc_ref[...] = jnp.zeros_like(acc_ref)

        acc_ref[...] += _dot(a_ref[...], w_ref[...])

        @pl.when(k == nk - 1)
        def _():
            dh = acc_ref[...]
            xv = x_ref[...]
            rstd = lax.rsqrt(jnp.mean(xv * xv, axis=-1, keepdims=True) + EPS)
            xhat = xv * rstd
            dg_ref[...] += jnp.sum((dh * xhat).reshape(tm // 8, 8, d), axis=0)
            dxh = dh * g_ref[...]
            gx_ref[...] = dx2_ref[...] + rstd * (dxh - xhat * jnp.mean(dxh * xhat, axis=-1, keepdims=True))

        @pl.when((i == nm - 1) & (k == nk - 1))
        def _():
            for cp in _chip_push_copies(cp_refs, land_refs, send_sems, recv_sems):
                cp.wait()

    any_spec = pl.BlockSpec(memory_space=pl.ANY)
    res = pl.pallas_call(
        body, name=name,
        out_shape=[jax.ShapeDtypeStruct((s, d), F32), jax.ShapeDtypeStruct((8, d), F32)]
        + [jax.ShapeDtypeStruct((3,) + a.shape[1:], a.dtype) for a in cps],
        grid=(nm, nk),
        in_specs=[pl.BlockSpec((tm, tk), lambda i, k: (i, k)), pl.BlockSpec((tk, d), lambda i, k: (k, 0)),
                  pl.BlockSpec((tm, d), lambda i, k: (i, 0)), pl.BlockSpec((1, d), lambda i, k: (0, 0)),
                  pl.BlockSpec((tm, d), lambda i, k: (i, 0))] + [any_spec] * nc,
        out_specs=[pl.BlockSpec((tm, d), lambda i, k: (i, 0)), pl.BlockSpec((8, d), lambda i, k: (0, 0))]
        + [any_spec] * nc,
        scratch_shapes=[pltpu.VMEM((tm, d), F32), pltpu.SemaphoreType.DMA((nc, 3)), pltpu.SemaphoreType.DMA((nc, 3))],
        compiler_params=_cparams(("arbitrary", "arbitrary")),
    )(dproj, wt, x, g, dx2, *cps)
    return res[0], res[1], res[2:]


def _log_sigmoids(z):
    l1p = jnp.log(1.0 + jnp.exp(-jnp.abs(z)))
    ls = jnp.minimum(z, 0.0) - l1p
    return ls, ls - z


def _strict_lower_ones(n):
    row = lax.broadcasted_iota(jnp.int32, (n, n), 0)
    col = lax.broadcasted_iota(jnp.int32, (n, n), 1)
    return row, col, (row > col).astype(BF16)


def _attn_fwd(qkv, riders, name):
    s = qkv.shape[0]
    tb, tt = ATT_BLOCK, ATT_TILE
    nq = s // tb
    per_block = tb // tt
    assert per_block == 2 and s // tt <= LANES, "two query tiles per grid step; one lane of saved carry per key tile"

    nr = len(riders)

    def sweep_body(gather, q_ref, k_ref, v_ref, o_ref, rs_ref, extra_ref, acc_ref, r_ref, rsv_ref, rmax_ref):
        i = pl.program_id(1)
        lane = lax.broadcasted_iota(jnp.int32, (tt, LANES), 1)
        hmask = [lane < HEAD_DIM, lane >= HEAD_DIM]
        row, col, tri = _strict_lower_ones(tt)
        tri2 = jnp.concatenate([tri, tri], axis=0)
        below = col < row
        qrows = [slice(u * tt, (u + 1) * tt) for u in range(per_block)]
        qm = [[jnp.where(m, q_ref[qrows[u], :], jnp.zeros((), BF16)) * jnp.asarray(HEAD_DIM ** -0.5, BF16)
               for m in hmask] for u in range(per_block)]
        acc_ref[...] = jnp.zeros_like(acc_ref)
        r_ref[...] = jnp.zeros_like(r_ref)
        rsv_ref[...] = jnp.full_like(rsv_ref, R_UNREACHED)

        def sweep(tiles, chains):
            heads, nch = range(2), range(len(chains))
            kv = []
            for t in tiles:
                rows = pl.ds(pl.multiple_of(t * tt, tt), tt)
                kv.append((k_ref[rows, :], v_ref[rows, :]))
            z = [[_dot_nt(qm[u][h], kv[ti][0]) for h in heads] for u, ti, _ in chains]
            ls, lk = [], []
            for n in nch:
                pairs = [_log_sigmoids(z[n][h]) for h in heads]
                ls.append([p[0] for p in pairs])
                lk.append([jnp.where(below, p[1], 0.0) if chains[n][2] else p[1] for p in pairs])
            cur = {u: [r_ref[h, qrows[u], :] for h in heads] for u in sorted({c[0] for c in chains})}
            r = []
            for n, (u, _, _) in enumerate(chains):
                r.append(cur[u])
                cur[u] = [cur[u][h] + jnp.sum(lk[n][h], axis=1, keepdims=True) for h in heads]
            for u in cur:
                rmax_ref[u] = jnp.max(jnp.maximum(cur[u][0], cur[u][1]))
            suffix = [[_dot(jnp.concatenate(_split_hi_lo(lk[n][h]), axis=1), tri2) for h in heads] for n in nch]
            w = []
            for n in nch:
                w_n = [jnp.exp(ls[n][h] + suffix[n][h] + r[n][h]) for h in heads]
                if chains[n][2]:
                    w_n = [jnp.where(below, w_h, 0.0) for w_h in w_n]
                w.append([w_h.astype(BF16) for w_h in w_n])
            pv = [[_dot(w[n][h], kv[chains[n][1]][1]) for h in heads] for n in nch]
            for u in cur:
                mine = [n for n in nch if chains[n][0] == u]
                for h in heads:
                    acc_ref[h, qrows[u], :] += functools.reduce(lambda a, b: a + b, [pv[n][h] for n in mine])
                    for n in mine:
                        if not chains[n][2]:
                            t = tiles[chains[n][1]]
                            rsv_ref[h, qrows[u], :] = jnp.where(lane == t, r[n][h], rsv_ref[h, qrows[u], :])
                    r_ref[h, qrows[u], :] = cur[u][h]

        first = per_block * i

        @pl.when(i == 0)
        def _():
            sweep([0, 1], [(0, 0, True), (1, 1, True), (1, 0, False)])

        @pl.when(i > 0)
        def _():
            sweep([first - 1, first, first + 1], [(0, 1, True), (1, 2, True), (0, 0, False), (1, 1, False)])

        for u in range(per_block):
            n_left = first + u - 1

            def live(c, n_left=n_left):
                jj, rmax = c
                return (jj < n_left) & (rmax >= CARRY_FLOOR)

            def step(c, u=u, n_left=n_left):
                jj, _ = c
                sweep([n_left - 1 - jj], [(u, 0, False)])
                return jj + 1, rmax_ref[u]

            swept, _ = lax.while_loop(live, step, (jnp.int32(0), rmax_ref[u]))
            extra_ref[pl.program_id(0), first + u] = swept.astype(F32)

        lane_b = lax.broadcasted_iota(jnp.int32, (tb, LANES), 1)
        o_ref[...] = jnp.where(lane_b < HEAD_DIM, acc_ref[0], acc_ref[1])
        rs_ref[...] = rsv_ref[...]

        @pl.when((pl.program_id(0) == N_PAIRS - 1) & (i == nq - 1))
        def _():
            gather.finish()

    def body(q_ref, k_ref, v_ref, *rest):
        rider_in, (o_ref, rs_ref, extra_ref) = rest[:nr], rest[nr:nr + 3]
        rider_out = rest[nr + 3:2 * nr + 3]
        acc_ref, r_ref, rsv_ref, rmax_ref = rest[2 * nr + 3:2 * nr + 7]
        gather = _Gather(rider_in, rider_out, *rest[2 * nr + 7:])

        @pl.when((pl.program_id(0) == 0) & (pl.program_id(1) == 0))
        def _():
            gather.issue()

        sweep_body(gather, q_ref, k_ref, v_ref, o_ref, rs_ref, extra_ref, acc_ref, r_ref, rsv_ref, rmax_ref)

    any_spec = pl.BlockSpec(memory_space=pl.ANY)
    res = pl.pallas_call(
        body, name=name,
        out_shape=[jax.ShapeDtypeStruct((s, D_SB), F32), jax.ShapeDtypeStruct((N_HEADS, s, LANES), F32),
                   jax.ShapeDtypeStruct((N_PAIRS, s // tt), F32)] + _gather_out_shapes(riders),
        grid=(N_PAIRS, nq),
        in_specs=[pl.BlockSpec((tb, LANES), lambda p, i: (i, p)),
                  pl.BlockSpec((s, LANES), lambda p, i: (0, N_PAIRS + p)),
                  pl.BlockSpec((s, LANES), lambda p, i: (0, 2 * N_PAIRS + p))] + [any_spec] * nr,
        out_specs=[pl.BlockSpec((tb, LANES), lambda p, i: (i, p)),
                   pl.BlockSpec((2, tb, LANES), lambda p, i: (p, i, 0)),
                   pl.BlockSpec(memory_space=pltpu.SMEM)] + [any_spec] * nr,
        scratch_shapes=[pltpu.VMEM((2, tb, LANES), F32), pltpu.VMEM((2, tb, LANES), F32),
                        pltpu.VMEM((2, tb, LANES), F32), pltpu.SMEM((per_block,), F32)] + _gather_semaphores(nr),
        compiler_params=_cparams(("arbitrary", "arbitrary")),
    )(qkv, qkv, qkv, *riders)
    return res[0], res[1], res[2], res[3:]


def _attn_bwd(qkv, do, rs, extra, dproj, smalls, cps, name):
    s = qkv.shape[0]
    tb, tt = ATT_BLOCK, ATT_TILE
    nq = s // tb
    per_block = tb // tt
    scale = HEAD_DIM ** -0.5

    ns, nc = len(smalls), len(cps)

    def sweep_body(q_ref, k_ref, v_ref, do_ref, rs_ref, extra_ref, dproj_hbm, out_hbm, dq_acc, dk_acc, dv_acc,
                   dqi_ref, pc_ref, stage_ref, out_sems):
        del dproj_hbm
        pair = pl.program_id(0)
        lane = lax.broadcasted_iota(jnp.int32, (tt, LANES), 1)
        hmask = [lane < HEAD_DIM, lane >= HEAD_DIM]
        row, col, tri = _strict_lower_ones(tt)
        tri2 = jnp.concatenate([tri, tri], axis=0)
        triu = (row < col).astype(BF16)
        below = col < row
        zero = jnp.zeros((), BF16)
        qrows = [slice(u * tt, (u + 1) * tt) for u in range(per_block)]
        dq_acc[...] = jnp.zeros_like(dq_acc)
        dk_acc[...] = jnp.zeros_like(dk_acc)
        dv_acc[...] = jnp.zeros_like(dv_acc)

        def qblock(i, carry):
            block_rows = pl.ds(pl.multiple_of(i * tb, tb), tb)
            q2 = q_ref[block_rows, :]
            do2 = do_ref[block_rows, :]
            qm = [[jnp.where(m, q2[qrows[u]], zero) * jnp.asarray(scale, BF16) for m in hmask]
                  for u in range(per_block)]
            dom = [[jnp.where(m, do2[qrows[u]], zero) for m in hmask] for u in range(per_block)]
            rs_i = [[rs_ref[h, pl.ds(pl.multiple_of(i * tb + u * tt, tt), tt), :] for h in range(2)]
                    for u in range(per_block)]
            dqi_ref[...] = jnp.zeros_like(dqi_ref)
            pc_ref[...] = jnp.zeros_like(pc_ref)
            first = per_block * i

            def sweep(tiles, chains):
                heads, nch = range(2), range(len(chains))
                krows = [pl.ds(pl.multiple_of(t * tt, tt), tt) for t in tiles]
                k2 = [k_ref[rows, :] for rows in krows]
                v2 = [v_ref[rows, :] for rows in krows]
                z = [[_dot_nt(qm[u][h], k2[ti]) for h in heads] for u, ti, _ in chains]
                dw = [[_dot_nt(dom[u][h], v2[ti]) for h in heads] for u, ti, _ in chains]
                ls, lk = [], []
                for n in nch:
                    pairs = [_log_sigmoids(z[n][h]) for h in heads]
                    ls.append([p[0] for p in pairs])
                    lk.append([jnp.where(below, p[1], 0.0) if chains[n][2] else p[1] for p in pairs])
                suffix = [[_dot(jnp.concatenate(_split_hi_lo(lk[n][h]), axis=1), tri2) for h in heads] for n in nch]
                w, g = [], []
                for n, (u, ti, diag) in enumerate(chains):
                    w_n = []
                    for h in heads:
                        logw = ls[n][h] + suffix[n][h]
                        if diag:
                            w_n.append(jnp.where(below, jnp.exp(logw), 0.0))
                        else:
                            carry_in = jnp.sum(jnp.where(lane == tiles[ti], rs_i[u][h], 0.0), axis=1, keepdims=True)
                            w_n.append(jnp.exp(logw + carry_in))
                    w.append([w_h.astype(BF16) for w_h in w_n])
                    g.append([w_n[h] * dw[n][h] for h in heads])
                prefix = [[_dot(g[n][h].astype(BF16), triu) for h in heads] for n in nch]
                cur = {u: [pc_ref[h, qrows[u], :] for h in heads] for u in sorted({c[0] for c in chains})}
                pc = []
                for n, (u, _, _) in enumerate(chains):
                    pc.append(cur[u])
                    cur[u] = [cur[u][h] + jnp.sum(g[n][h], axis=1, keepdims=True) for h in heads]
                dzb = []
                for n in nch:
                    dz_n = []
                    for h in heads:
                        sig = jnp.exp(ls[n][h])
                        dz = g[n][h] - sig * (g[n][h] + prefix[n][h] + pc[n][h])
                        if chains[n][2]:
                            dz = jnp.where(below, dz, 0.0)
                        dz_n.append(dz.astype(BF16))
                    dzb.append(dz_n)
                for u in cur:
                    for h in heads:
                        pc_ref[h, qrows[u], :] = cur[u][h]
                dq = [[_dot(dzb[n][h], jnp.where(hmask[h], k2[chains[n][1]], zero)) for h in heads] for n in nch]
                dk = [[_dot_tn(dzb[n][h], qm[chains[n][0]][h]) for h in heads] for n in nch]
                dv = [[_dot_tn(w[n][h], dom[chains[n][0]][h]) for h in heads] for n in nch]
                add = lambda a, b: a + b
                for u in cur:
                    dqi_ref[qrows[u], :] += functools.reduce(
                        add, [dq[n][h] for n in nch if chains[n][0] == u for h in heads])
                for ti in range(len(tiles)):
                    mine = [n for n in nch if chains[n][1] == ti]
                    dk_acc[krows[ti], :] += functools.reduce(add, [dk[n][h] for n in mine for h in heads])
                    dv_acc[krows[ti], :] += functools.reduce(add, [dv[n][h] for n in mine for h in heads])

            for u in range(per_block):
                n_left = first + u - 1
                n_extra = jnp.clip(extra_ref[pair, first + u].astype(jnp.int32), 0, jnp.maximum(n_left, 0))

                def step(t, c, u=u):
                    sweep([t], [(u, 0, False)])
                    return c

                lax.fori_loop(n_left - n_extra, n_left, step, 0)

            @pl.when(i == 0)
            def _():
                sweep([0, 1], [(0, 0, True), (1, 0, False), (1, 1, True)])

            @pl.when(i > 0)
            def _():
                sweep([first - 1, first, first + 1], [(0, 0, False), (1, 1, False), (0, 1, True), (1, 2, True)])

            dq_acc[block_rows, :] += dqi_ref[...] * scale
            return carry

        lax.fori_loop(0, nq, qblock, 0)
        copies = []
        for t, acc in enumerate((dq_acc, dk_acc, dv_acc)):
            stage_ref[t] = acc[...].astype(BF16)
            col0 = pl.multiple_of(t * D_SB + pair * LANES, LANES)
            copies.append(pltpu.make_async_copy(stage_ref.at[t], out_hbm.at[:, pl.ds(col0, LANES)], out_sems.at[t]))
        for cp in copies:
            cp.start()
        for cp in copies:
            cp.wait()

    def body(q_ref, k_ref, v_ref, do_ref, rs_ref, extra_ref, dproj_hbm, *rest):
        small_in, cp_in = rest[:ns], rest[ns:ns + nc]
        out_hbm = rest[ns + nc]
        small_out, land_out = rest[ns + nc + 1:2 * ns + nc + 1], rest[2 * ns + nc + 1:2 * (ns + nc) + 1]
        scratch = rest[2 * (ns + nc) + 1:]
        gather = _Gather(small_in, small_out, *scratch[7:10])
        pair = pl.program_id(0)

        @pl.when(pair == 0)
        def _():
            gather.issue()
            for cp in _chip_push_copies(cp_in, land_out, *scratch[10:]):
                cp.start()

        sweep_body(q_ref, k_ref, v_ref, do_ref, rs_ref, extra_ref, dproj_hbm, out_hbm, *scratch[:7])

        @pl.when(pair == N_PAIRS - 1)
        def _():
            gather.finish()
            for cp in _chip_push_copies(cp_in, land_out, *scratch[10:]):
                cp.wait()

    any_spec = pl.BlockSpec(memory_space=pl.ANY)
    res = pl.pallas_call(
        body, name=name,
        out_shape=[jax.ShapeDtypeStruct(dproj.shape, BF16)] + _gather_out_shapes(smalls)
        + [jax.ShapeDtypeStruct((3,) + a.shape[1:], a.dtype) for a in cps],
        grid=(N_PAIRS,),
        in_specs=[pl.BlockSpec((s, LANES), lambda p: (0, p)),
                  pl.BlockSpec((s, LANES), lambda p: (0, N_PAIRS + p)),
                  pl.BlockSpec((s, LANES), lambda p: (0, 2 * N_PAIRS + p)),
                  pl.BlockSpec((s, LANES), lambda p: (0, p)),
                  pl.BlockSpec((2, s, LANES), lambda p: (p, 0, 0)),
                  pl.BlockSpec(memory_space=pltpu.SMEM),
                  any_spec] + [any_spec] * (ns + nc),
        out_specs=[any_spec] * (1 + ns + nc),
        scratch_shapes=[pltpu.VMEM((s, LANES), F32), pltpu.VMEM((s, LANES), F32), pltpu.VMEM((s, LANES), F32),
                        pltpu.VMEM((tb, LANES), F32), pltpu.VMEM((2, tb, LANES), F32),
                        pltpu.VMEM((3, s, LANES), BF16), pltpu.SemaphoreType.DMA((3,))]
        + _gather_semaphores(ns) + [pltpu.SemaphoreType.DMA((nc, 3)), pltpu.SemaphoreType.DMA((nc, 3))],
        input_output_aliases={6: 0},
        compiler_params=_cparams(("arbitrary",)),
    )(qkv, qkv, qkv, do, rs, extra, dproj, *smalls, *cps)
    return res[0], res[1:1 + ns], res[1 + ns:]


def _mid(o, rest, x, tgt, wua, wub, wout, fg, lng, lnb, wsp, bfull, gavg, name):
    s, d = x.shape
    ts = 256
    nt = s // ts
    nchunk = ts // SGU_CHUNK
    n_rest = rest.shape[1]

    def body(o_ref, rest_ref, x_ref, t_ref, wua_ref, wub_ref, wout_ref, fg_ref, lng_ref, lnb_ref, wsp_ref, bfull_ref,
             gavg_ref, loss_ref, dx2_ref, do_ref, dproj_ref, dwua_ref, dwub_ref, dwout_ref, dfg_ref, dlng_ref,
             dlnb_ref, dwsp_ref, dbfull_ref):
        step = pl.program_id(0)

        @pl.when(step == 0)
        def _():
            for ref in (loss_ref, dwua_ref, dwub_ref, dwout_ref, dfg_ref, dlng_ref, dlnb_ref, dwsp_ref, dbfull_ref):
                ref[...] = jnp.zeros_like(ref)

        gavg = gavg_ref[...]

        def gmean(a):
            return _dot(a.astype(BF16), gavg)

        def colsum8(a):
            return jnp.sum(a.reshape(ts // 8, 8, a.shape[1]), axis=0)

        z_a = rest_ref[:, 0:512]
        u_b = rest_ref[:, 512:1024]
        v_b = rest_ref[:, 1024:1536]
        z_b = rest_ref[:, 1536:2048]
        g_a = rest_ref[:, 2048:2048 + d]
        g_b = rest_ref[:, 2048 + d:2048 + 2 * d]
        ov = o_ref[...]
        sa = _sigmoid(z_a)
        silu_a = z_a * sa
        y_a = ov * silu_a
        ug, dug_du = _gelu_and_grad(u_b)
        vg, dvg_dv = _gelu_and_grad(v_b)
        mu = gmean(vg)
        cen = vg - mu
        rstd_g = lax.rsqrt(gmean(cen * cen) + EPS)
        vhat = cen * rstd_g
        vn = vhat * lng_ref[...] + lnb_ref[...]
        vnb = vn.astype(BF16)

        t_idx = lax.broadcasted_iota(jnp.int32, (SGU_CHUNK, SGU_CHUNK), 0)
        s_idx = lax.broadcasted_iota(jnp.int32, (SGU_CHUNK, SGU_CHUNK), 1)
        causal = (s_idx // CHUNK) <= (t_idx // CHUNK)
        wm = [jnp.where(causal, wsp_ref[g], 0.0) for g in range(N_GROUPS)]
        wmb = [w.astype(BF16) for w in wm]
        wmtb = [w.T.astype(BF16) for w in wm]
        lane = lax.broadcasted_iota(jnp.int32, (SGU_CHUNK, LANES), 1)
        first = lane < GROUP_DIM
        bfull = bfull_ref[...]

        mixed_rows = []
        for n in range(nchunk):
            r0, r1 = n * SGU_CHUNK, (n + 1) * SGU_CHUNK
            pieces = []
            for p in range(N_GROUPS // 2):
                blk = vnb[r0:r1, p * LANES:(p + 1) * LANES]
                pieces.append(jnp.where(first, _dot(wmb[2 * p], blk), _dot(wmb[2 * p + 1], blk)))
            mixed_rows.append(jnp.concatenate(pieces, axis=1) + bfull)
        mixed = jnp.concatenate(mixed_rows, axis=0)
        sg = ug * mixed
        sb = _sigmoid(z_b)
        silu_b = z_b * sb
        y_b = sg * silu_b
        y_ab = y_a.astype(BF16)
        y_bb = y_b.astype(BF16)
        p_a = _dot(y_ab, wua_ref[...])
        p_b = _dot(y_bb, wub_ref[...])
        ga_s = _sigmoid(g_a)
        gb_s = _sigmoid(g_b)
        merged_b = (ga_s * p_a + gb_s * p_b).astype(BF16)
        x2 = x_ref[...] + _dot(merged_b, wout_ref[...])
        rstd = lax.rsqrt(jnp.mean(x2 * x2, axis=-1, keepdims=True) + EPS)
        xhat = x2 * rstd
        fg_v = fg_ref[...]
        diff = xhat * fg_v - t_ref[...]
        loss_ref[...] += 0.5 * jnp.sum(jnp.sum(diff * diff, axis=-1, keepdims=True) * (1.0 / d))

        dy = diff * (1.0 / d)
        dfg_ref[...] += colsum8(dy * xhat)
        dxh = dy * fg_v
        dx2 = rstd * (dxh - xhat * jnp.mean(dxh * xhat, axis=-1, keepdims=True))
        dx2_ref[...] = dx2
        dx2b = dx2.astype(BF16)
        dwout_ref[...] += _dot_tn(merged_b, dx2b)
        dmerged = _dot_nt(dx2b, wout_ref[...])
        dp_a = dmerged * ga_s
        dp_b = dmerged * gb_s
        dproj_ref[:, QKV_COLS + 2048:QKV_COLS + 2048 + d] = (dmerged * p_a * (ga_s * (1.0 - ga_s))).astype(BF16)
        dproj_ref[:, QKV_COLS + 2048 + d:QKV_COLS + 2048 + 2 * d] = (dmerged * p_b * (gb_s * (1.0 - gb_s))).astype(BF16)
        dp_ab = dp_a.astype(BF16)
        dp_bb = dp_b.astype(BF16)
        dwua_ref[...] += _dot_tn(y_ab, dp_ab)
        dwub_ref[...] += _dot_tn(y_bb, dp_bb)
        dy_a = _dot_nt(dp_ab, wua_ref[...])
        dy_b = _dot_nt(dp_bb, wub_ref[...])
        do_ref[...] = (dy_a * silu_a).astype(BF16)
        dproj_ref[:, QKV_COLS:QKV_COLS + 512] = (dy_a * ov * (sa * (1.0 + z_a * (1.0 - sa)))).astype(BF16)
        dsg = dy_b * silu_b
        dproj_ref[:, QKV_COLS + 1536:QKV_COLS + 2048] = (dy_b * sg * (sb * (1.0 + z_b * (1.0 - sb)))).astype(BF16)
        dproj_ref[:, QKV_COLS + 512:QKV_COLS + 1024] = (dsg * mixed * dug_du).astype(BF16)
        dmixed = dsg * ug
        dmb = dmixed.astype(BF16)
        zero = jnp.zeros((), BF16)
        dvn_rows = []
        db = jnp.zeros((SGU_CHUNK, D_SGU), F32)
        for n in range(nchunk):
            r0, r1 = n * SGU_CHUNK, (n + 1) * SGU_CHUNK
            db = db + dmixed[r0:r1, :]
            pieces = []
            for p in range(N_GROUPS // 2):
                cols = slice(p * LANES, (p + 1) * LANES)
                dm_blk = dmb[r0:r1, cols]
                vn_blk = vnb[r0:r1, cols]
                dwsp_ref[2 * p] += _dot_nt(jnp.where(first, dm_blk, zero), vn_blk)
                dwsp_ref[2 * p + 1] += _dot_nt(jnp.where(first, zero, dm_blk), vn_blk)
                pieces.append(jnp.where(first, _dot(wmtb[2 * p], dm_blk), _dot(wmtb[2 * p + 1], dm_blk)))
            dvn_rows.append(jnp.concatenate(pieces, axis=1))
        dbfull_ref[...] += db
        dvn = jnp.concatenate(dvn_rows, axis=0)
        dlng_ref[...] += colsum8(dvn * vhat)
        dlnb_ref[...] += colsum8(dvn)
        dvhat = dvn * lng_ref[...]
        dcen = rstd_g * (dvhat - gmean(dvhat) - vhat * gmean(dvhat * vhat))
        dproj_ref[:, QKV_COLS + 1024:QKV_COLS + 1536] = (dcen * dvg_dv).astype(BF16)

        @pl.when(step == nt - 1)
        def _():
            for g in range(N_GROUPS):
                dwsp_ref[g] = jnp.where(causal, dwsp_ref[g], 0.0)

    def tile(cols):
        return pl.BlockSpec((ts, cols), lambda i: (i, 0))

    def whole(shape):
        return pl.BlockSpec(shape, lambda i: (0,) * len(shape))

    out_shapes = [
        jax.ShapeDtypeStruct((8, LANES), F32),
        jax.ShapeDtypeStruct((s, d), F32),
        jax.ShapeDtypeStruct((s, D_SB), BF16),
        jax.ShapeDtypeStruct((s, QKV_COLS + n_rest), BF16),
        jax.ShapeDtypeStruct((D_SB, d), F32),
        jax.ShapeDtypeStruct((D_SGU, d), F32),
        jax.ShapeDtypeStruct((d, d), F32),
        jax.ShapeDtypeStruct((8, d), F32),
        jax.ShapeDtypeStruct((8, D_SGU), F32),
        jax.ShapeDtypeStruct((8, D_SGU), F32),
        jax.ShapeDtypeStruct((N_GROUPS, SGU_CHUNK, SGU_CHUNK), F32),
        jax.ShapeDtypeStruct((SGU_CHUNK, D_SGU), F32),
    ]
    out_specs = [whole((8, LANES)), tile(d), tile(D_SB), tile(QKV_COLS + n_rest), whole((D_SB, d)), whole((D_SGU, d)),
                 whole((d, d)), whole((8, d)), whole((8, D_SGU)), whole((8, D_SGU)),
                 whole((N_GROUPS, SGU_CHUNK, SGU_CHUNK)), whole((SGU_CHUNK, D_SGU))]
    in_specs = [tile(D_SB), tile(n_rest), tile(d), tile(d), whole((D_SB, d)), whole((D_SGU, d)), whole((d, d)),
                whole((1, d)), whole((1, D_SGU)), whole((1, D_SGU)), whole((N_GROUPS, SGU_CHUNK, SGU_CHUNK)),
                whole((SGU_CHUNK, D_SGU)), whole((D_SGU, D_SGU))]
    return pl.pallas_call(
        body, name=name, out_shape=out_shapes, grid=(nt,), in_specs=in_specs, out_specs=out_specs,
        compiler_params=_cparams(("arbitrary",)),
    )(o, rest, x, tgt, wua, wub, wout, fg, lng, lnb, wsp, bfull, gavg)


def _small_reduce(dfg8, dlng8, dlnb8, dbfull, name):
    d = dfg8.shape[1]

    def body(dfg_ref, dlng_ref, dlnb_ref, dbfull_ref, fg_out, lng_out, lnb_out, b_out):
        row = lax.broadcasted_iota(jnp.int32, (D_SGU, LANES), 0)
        col = lax.broadcasted_iota(jnp.int32, (D_SGU, LANES), 1)

        def select_sum(a, sel):
            hi, lo = _split_hi_lo(a)
            lo2 = (a - hi.astype(F32) - lo.astype(F32)).astype(BF16)
            return _dot(hi, sel) + _dot(lo, sel) + _dot(lo2, sel)

        def by_group(partials):
            v = jnp.sum(partials, axis=0, keepdims=True)
            g_of_lane = lax.broadcasted_iota(jnp.int32, (N_GROUPS, D_SGU), 1) // GROUP_DIM
            g_of_row = lax.broadcasted_iota(jnp.int32, (N_GROUPS, D_SGU), 0)
            spread = jnp.where(g_of_lane == g_of_row, jnp.broadcast_to(v, (N_GROUPS, D_SGU)), 0.0)
            return select_sum(spread, (row % GROUP_DIM == col).astype(BF16))[:, 0:GROUP_DIM]

        fg_out[...] = jnp.sum(dfg_ref[...], axis=0, keepdims=True)
        lng_out[...] = by_group(dlng_ref[...])
        lnb_out[...] = by_group(dlnb_ref[...])
        by_pos = select_sum(dbfull_ref[...], (row // GROUP_DIM == col).astype(BF16))
        b_out[...] = by_pos.T[0:N_GROUPS, :]

    return pl.pallas_call(
        body, name=name,
        out_shape=[jax.ShapeDtypeStruct((1, d), F32), jax.ShapeDtypeStruct((N_GROUPS, GROUP_DIM), F32),
                   jax.ShapeDtypeStruct((N_GROUPS, GROUP_DIM), F32), jax.ShapeDtypeStruct((N_GROUPS, SGU_CHUNK), F32)],
        compiler_params=_cparams(),
    )(dfg8, dlng8, dlnb8, dbfull)


def _block_major_cols(w):
    r, n = w.shape
    return jnp.transpose(w.reshape(r, N_DEV, n // N_DEV), (1, 0, 2))


def _from_block_major_cols(w):
    nb, r, c = w.shape
    return jnp.transpose(w, (1, 0, 2)).reshape(r, nb * c)


def kernel(x, norm_g, w_in, sgu_ln_g, sgu_ln_b, w_spatial, b_spatial, w_up_a, w_up_b, w_out, final_norm_g, loss_target, m_norm_g, m_w_in, m_sgu_ln_g, m_sgu_ln_b, m_w_spatial, m_b_spatial, m_w_up_a, m_w_up_b, m_w_out, m_final_norm_g, v_norm_g, v_w_in, v_sgu_ln_g, v_sgu_ln_b, v_w_spatial, v_b_spatial, v_w_up_a, v_w_up_b, v_w_out, v_final_norm_g):
    s, d = x.shape[1], x.shape[2]
    xs = x[0]
    tgt = loss_target[0]
    cx, cy, cc = _coords()
    core = jnp.reshape(cc, (1,)).astype(jnp.int32)
    chip = jnp.reshape(2 * cx + cy, (1,)).astype(jnp.int32)

    h, ht, (g_win,) = _rmsnorm_fwd(xs, norm_g, [jnp.transpose(w_in[0]).astype(BF16)], "norm")
    d_in = N_DEV * w_in.shape[2]
    w_full = g_win.reshape(d_in, d)
    qkv, _ = _inproj(h, w_full, 0, QKV_COLS, BF16, [], "inproj_qkv")
    rest, _ = _inproj(h, w_full, QKV_COLS, d_in - QKV_COLS, F32, [], "inproj_rest")
    o, rs, extra, (g_wua, g_wub, g_wout) = _attn_fwd(
        qkv, [w_up_a[0].astype(BF16), w_up_b[0].astype(BF16), w_out[0].astype(BF16)], "attn_fwd")
    wua_full = _from_block_major_cols(g_wua)
    wub_full = _from_block_major_cols(g_wub)
    wout_full = g_wout.reshape(d, d)

    lng = sgu_ln_g.reshape(1, D_SGU)
    lnb = sgu_ln_b.reshape(1, D_SGU)
    bfull = jnp.repeat(jnp.transpose(b_spatial[0]), GROUP_DIM, axis=1)
    grp = jnp.arange(D_SGU) // GROUP_DIM
    gavg = jnp.where(grp[:, None] == grp[None, :], 1.0 / GROUP_DIM, 0.0).astype(BF16)
    (loss_b, dx2, do, dproj, dwua, dwub, dwout, dfg8, dlng8, dlnb8, dwsp, dbfull) = _mid(
        o, rest, xs, tgt, wua_full, wub_full, wout_full, final_norm_g.reshape(1, d), lng, lnb, w_spatial[0], bfull,
        gavg, "mid")

    def blocks42(a):
        return a.reshape((4, 2) + a.shape[1:])

    def chip_partials(own, tag):
        land = _push_sibling(own, "rs_sibling_" + tag)
        return [_chip_partial_sum(a, l, core, "cpsum_%s%d" % (tag, i)) for i, (a, l) in enumerate(zip(own, land))]

    cps_up = chip_partials([blocks42(_block_major_cols(dwua.astype(BF16))),
                            blocks42(_block_major_cols(dwub.astype(BF16))),
                            blocks42(dwout.astype(BF16).reshape(N_DEV, d // N_DEV, d))], "up")

    dfg, dlng, dlnb, db = _small_reduce(dfg8, dlng8, dlnb8, dbfull, "small_reduce")
    wsp_rows = N_GROUPS * SGU_CHUNK
    dproj, small_parts, land_up = _attn_bwd(
        qkv, do, rs, extra, dproj, [dfg, dlng, dlnb, db, dwsp.reshape(wsp_rows, SGU_CHUNK)], cps_up, "attn_bwd")
    dwin = _dw_in(ht, dproj, "dwin")

    cps_in = chip_partials([dwin.reshape(4, 2, d_in // N_DEV, d)], "in")
    grad_x, dng8, land_in = _dh_and_grad_x(dproj, w_full, xs, norm_g, dx2, cps_in, "dh")

    ng_parts, loss_parts = _allgather([dng8, loss_b], "ag_tail")

    def small_layouts(ng, fg, lg, lb, bs, ws):
        return [ng.reshape(1, d), fg.reshape(1, d), lg.reshape(N_GROUPS, GROUP_DIM), lb.reshape(N_GROUPS, GROUP_DIM),
                bs.reshape(N_GROUPS, SGU_CHUNK), ws.reshape(wsp_rows, SGU_CHUNK)]

    sm, loss_sum = _adam_small(
        [ng_parts] + list(small_parts),
        small_layouts(norm_g, final_norm_g, sgu_ln_g, sgu_ln_b, b_spatial, w_spatial),
        small_layouts(m_norm_g, m_final_norm_g, m_sgu_ln_g, m_sgu_ln_b, m_b_spatial, m_w_spatial),
        small_layouts(v_norm_g, v_final_norm_g, v_sgu_ln_g, v_sgu_ln_b, v_b_spatial, v_w_spatial),
        loss_parts, "adam_small")
    small_shapes = [norm_g.shape, final_norm_g.shape, sgu_ln_g.shape, sgu_ln_b.shape, b_spatial.shape, w_spatial.shape]
    sm = [[a.reshape(shp) for a, shp in zip(kind, small_shapes)] for kind in sm]

    res = _adam_shard(cps_in, list(land_in), chip, *[jnp.transpose(a[0]) for a in (w_in, m_w_in, v_w_in)], "adam0")
    big = [[jnp.transpose(r)[None] for r in res]]
    for i, (w, m, v) in enumerate([(w_up_a, m_w_up_a, v_w_up_a), (w_up_b, m_w_up_b, v_w_up_b),
                                   (w_out, m_w_out, v_w_out)]):
        res = _adam_shard([cps_up[i]], [land_up[i]], chip, w[0], m[0], v[0], "adam%d" % (i + 1))
        big.append([r[None] for r in res])

    loss = loss_sum[0, 0]

    def per_kind(kd):
        return [sm[kd][0], big[0][kd], sm[kd][2], sm[kd][3], sm[kd][5], sm[kd][4], big[1][kd], big[2][kd], big[3][kd],
                sm[kd][1]]

    return (loss, grad_x[None], *per_kind(0), *per_kind(1), *per_kind(2), *per_kind(3))
```

```python
import functools
import math

import jax
import jax.numpy as jnp
from jax import lax
from jax.experimental import pallas as pl
from jax.experimental.pallas import tpu as pltpu

F32 = jnp.float32
BF16 = jnp.bfloat16
MESH = pl.DeviceIdType.MESH

N_DEV = 8
N_HEADS = 8
HEAD_DIM = 64
D_SB = N_HEADS * HEAD_DIM
N_GROUPS = 8
GROUP_DIM = 64
D_SGU = N_GROUPS * GROUP_DIM
SGU_CHUNK = 128
CHUNK = 64
EPS = 1e-6
LANES = 128
N_PAIRS = N_HEADS // 2
QKV_COLS = 3 * D_SB
ATT_BLOCK = 256
ATT_TILE = 128
CARRY_FLOOR = -90.0
R_UNREACHED = -1e30

ADAM_LR = 0.001
ADAM_B1 = 0.9
ADAM_B2 = 0.999
ADAM_EPS = 1e-08
ADAM_WD = 0.01
ADAM_STEP = 10

VMEM_LIMIT = 56 * 1024 * 1024


def _cparams(sem=None, vmem=VMEM_LIMIT):
    return pltpu.CompilerParams(dimension_semantics=sem, vmem_limit_bytes=vmem)


def _dot(a, b):
    return jnp.dot(a, b, preferred_element_type=F32)


def _dot_nt(a, b):
    return lax.dot_general(a, b, (((1,), (1,)), ((), ())), preferred_element_type=F32)


def _dot_tn(a, b):
    return lax.dot_general(a, b, (((0,), (0,)), ((), ())), preferred_element_type=F32)


def _split_hi_lo(a):
    hi = a.astype(BF16)
    lo = (a - hi.astype(F32)).astype(BF16)
    return hi, lo


def _sigmoid(x):
    return 1.0 / (1.0 + jnp.exp(-x))


_GELU_C = math.sqrt(2.0 / math.pi)


def _gelu_and_grad(x):
    x2 = x * x
    inner = _GELU_C * (x + 0.044715 * (x2 * x))
    t = jnp.tanh(inner)
    cdf = 0.5 * (1.0 + t)
    g = x * cdf
    dg = cdf + x * (0.5 * (1.0 - t * t)) * (_GELU_C * (1.0 + 3.0 * 0.044715 * x2))
    return g, dg


def _coords():
    return lax.axis_index("x"), lax.axis_index("y"), lax.axis_index("c")


def _dev_index(px, py, pc):
    return 4 * px + 2 * py + pc


def _allgather(blocks, name):
    n = len(blocks)

    def body(*refs):
        gather = _Gather(refs[:n], refs[n:2 * n], *refs[2 * n:])
        gather.issue()
        gather.finish()

    any_spec = pl.BlockSpec(memory_space=pl.ANY)
    return pl.pallas_call(
        body, name=name,
        out_shape=_gather_out_shapes(blocks),
        in_specs=[any_spec] * n, out_specs=[any_spec] * n,
        scratch_shapes=_gather_semaphores(n),
    )(*blocks)


def _gather_out_shapes(blocks):
    return [jax.ShapeDtypeStruct((N_DEV,) + b.shape, b.dtype) for b in blocks]


def _gather_semaphores(n):
    return [pltpu.SemaphoreType.DMA((n, 7)), pltpu.SemaphoreType.DMA((n, 7)), pltpu.SemaphoreType.DMA((n,))]


class _Gather:
    def __init__(self, ins, outs, send_sems, recv_sems, local_sems, relay=False):
        self.ins, self.outs = ins, outs
        self.send_sems, self.recv_sems, self.local_sems = send_sems, recv_sems, local_sems
        self.n = len(ins)
        self.relay = relay
        x, y, c = _coords()
        self.c = c
        self.me, self.sibling = (x, y, c), (x, y, 1 - c)
        self.chips = [(1 - x, y), (x, 1 - y), (1 - x, 1 - y)]

    def _copy(self, a, k, block, to, src=None):
        dst = self.outs[a].at[_dev_index(*block)]
        return pltpu.make_async_remote_copy(
            src_ref=dst if src is None else src, dst_ref=dst,
            send_sem=self.send_sems.at[a, k], recv_sem=self.recv_sems.at[a, k],
            device_id=to, device_id_type=MESH)

    def _mine(self):
        return [pltpu.make_async_copy(self.ins[a], self.outs[a].at[_dev_index(*self.me)], self.local_sems.at[a])
                for a in range(self.n)]

    def _first(self):
        first = []
        direct = self.chips[:2] if self.relay else self.chips
        for a in range(self.n):
            first.append(self._copy(a, 0, self.me, self.sibling, src=self.ins[a]))
            first += [self._copy(a, 1 + j, self.me, (*chip, self.c), src=self.ins[a])
                      for j, chip in enumerate(direct)]
        return first

    def issue(self):
        for cp in self._mine() + self._first():
            cp.start()

    def _pass_on(self, a, j):
        chip = self.chips[j]
        self._copy(a, 1 + j, (*chip, self.c), self.me).wait_recv()
        self._copy(a, 4 + j, (*chip, self.c), self.sibling).start()

    def finish(self):
        c = self.c
        if self.relay:
            for core in range(2):
                @pl.when(c == core)
                def _(core=core):
                    j_src, j_dst = core, 1 - core
                    for a in range(self.n):
                        self._pass_on(a, j_src)
                        self._copy(a, 3, (*self.chips[j_src], c), (*self.chips[j_dst], c)).start()
                    for a in range(self.n):
                        self._pass_on(a, j_dst)
                    for a in range(self.n):
                        self._pass_on(a, 2)
        else:
            for j in range(3):
                for a in range(self.n):
                    self._pass_on(a, j)
        for a in range(self.n):
            self._copy(a, 0, self.sibling, self.me).wait_recv()
            for j, chip in enumerate(self.chips):
                self._copy(a, 4 + j, (*chip, 1 - c), self.me).wait_recv()
        for a in range(self.n):
            for k in range(7):
                self._copy(a, k, self.me, self.sibling).wait_send()
        for cp in self._mine():
            cp.wait()


def _push_sibling(arrs, name):
    n = len(arrs)

    def body(*refs):
        ins, outs = refs[:n], refs[n:2 * n]
        send_sems, recv_sems = refs[2 * n:]
        x, y, c = _coords()
        sibling = (x, y, 1 - c)
        copies = []
        for a in range(n):
            for k in range(4):
                copies.append(pltpu.make_async_remote_copy(
                    src_ref=ins[a].at[k, 1 - c], dst_ref=outs[a].at[k],
                    send_sem=send_sems.at[a, k], recv_sem=recv_sems.at[a, k],
                    device_id=sibling, device_id_type=MESH))
        for cp in copies:
            cp.start()
        for cp in copies:
            cp.wait()

    any_spec = pl.BlockSpec(memory_space=pl.ANY)
    return pl.pallas_call(
        body, name=name,
        out_shape=[jax.ShapeDtypeStruct((4,) + a.shape[2:], a.dtype) for a in arrs],
        in_specs=[any_spec] * n, out_specs=[any_spec] * n,
        scratch_shapes=[pltpu.SemaphoreType.DMA((n, 4)), pltpu.SemaphoreType.DMA((n, 4))],
    )(*arrs)


def _chip_push_copies(ins, outs, send_sems, recv_sems):
    x, y, c = _coords()
    chips = [(1 - x, y), (x, 1 - y), (1 - x, 1 - y)]
    return [pltpu.make_async_remote_copy(
        src_ref=ins[a].at[2 * px + py], dst_ref=outs[a].at[r],
        send_sem=send_sems.at[a, r], recv_sem=recv_sems.at[a, r],
        device_id=(px, py, c), device_id_type=MESH)
        for a in range(len(ins)) for r, (px, py) in enumerate(chips)]


def _chip_partial_sum(own, land, core, name):
    _, _, rows, cols = own.shape
    tr = rows

    def body(core_ref, own_ref, land_ref, out_ref):
        del core_ref
        out_ref[...] = (own_ref[...].astype(F32) + land_ref[...].astype(F32)).astype(out_ref.dtype)

    return pl.pallas_call(
        body, name=name,
        out_shape=jax.ShapeDtypeStruct((4, rows, cols), own.dtype),
        grid_spec=pltpu.PrefetchScalarGridSpec(
            num_scalar_prefetch=1, grid=(4, rows // tr),
            in_specs=[pl.BlockSpec((None, None, tr, cols), lambda k, r, core: (k, core[0], r, 0)),
                      pl.BlockSpec((None, tr, cols), lambda k, r, core: (k, r, 0))],
            out_specs=pl.BlockSpec((None, tr, cols), lambda k, r, core: (k, r, 0))),
        compiler_params=_cparams(("parallel", "parallel")),
    )(core, own, land)


def _adamw_math(w, g, m, v):
    m = ADAM_B1 * m + (1.0 - ADAM_B1) * g
    v = ADAM_B2 * v + (1.0 - ADAM_B2) * (g * g)
    m_hat = m / (1.0 - ADAM_B1 ** ADAM_STEP)
    v_hat = v / (1.0 - ADAM_B2 ** ADAM_STEP)
    delta = -ADAM_LR * (m_hat / (jnp.sqrt(v_hat) + ADAM_EPS) + ADAM_WD * w)
    return delta, m, v


def _adam_shard(cps, lands, chip, w, m, v, name):
    rows, cols = w.shape
    tr = rows // 4
    nparts = len(cps)

    def body(chip_ref, *refs):
        del chip_ref
        cp_refs, land_refs = refs[:nparts], refs[nparts:2 * nparts]
        w_ref, m_ref, v_ref, g_out, d_out, m_out, v_out = refs[2 * nparts:]
        parts = []
        for cp_ref, land_ref in zip(cp_refs, land_refs):
            g_k = cp_ref[...].astype(F32)
            for r in range(3):
                g_k = g_k + land_ref[r].astype(F32)
            parts.append(g_k)
        g = parts[0] if nparts == 1 else jnp.concatenate(parts, axis=1)
        delta, m_new, v_new = _adamw_math(w_ref[...], g, m_ref[...], v_ref[...])
        g_out[...] = g
        d_out[...] = delta
        m_out[...] = m_new
        v_out[...] = v_new

    tile = pl.BlockSpec((tr, cols), lambda r, chip: (r, 0))
    out = jax.ShapeDtypeStruct((rows, cols), F32)
    return pl.pallas_call(
        body, name=name, out_shape=[out] * 4,
        grid_spec=pltpu.PrefetchScalarGridSpec(
            num_scalar_prefetch=1, grid=(rows // tr,),
            in_specs=[pl.BlockSpec((None, tr, a.shape[2]), lambda r, chip: (chip[0], r, 0)) for a in cps]
            + [pl.BlockSpec((3, tr, a.shape[2]), lambda r, chip: (0, r, 0)) for a in lands]
            + [tile, tile, tile],
            out_specs=[tile] * 4),
        compiler_params=_cparams(("parallel",)),
    )(chip, *cps, *lands, w, m, v)


def _adam_small(parts, ws, ms, vs, loss_parts, name):
    n = len(ws)

    def body(*refs):
        p_refs, w_refs, m_refs, v_refs = refs[:n], refs[n:2 * n], refs[2 * n:3 * n], refs[3 * n:4 * n]
        loss_ref, outs, loss_out = refs[4 * n], refs[4 * n + 1:-1], refs[-1]
        total = loss_ref[0]
        for dev in range(1, N_DEV):
            total = total + loss_ref[dev]
        loss_out[...] = total
        for i in range(n):
            g = p_refs[i][0]
            for dev in range(1, N_DEV):
                g = g + p_refs[i][dev]
            if g.shape[0] != w_refs[i].shape[0]:
                g = jnp.sum(g, axis=0, keepdims=True)
            delta, m_new, v_new = _adamw_math(w_refs[i][...], g, m_refs[i][...], v_refs[i][...])
            outs[i][...] = g
            outs[n + i][...] = delta
            outs[2 * n + i][...] = m_new
            outs[3 * n + i][...] = v_new

    out_shapes = [jax.ShapeDtypeStruct(w.shape, F32) for w in ws] * 4
    out_shapes.append(jax.ShapeDtypeStruct(loss_parts.shape[1:], F32))
    res = pl.pallas_call(body, name=name, out_shape=out_shapes, compiler_params=_cparams())(
        *parts, *ws, *ms, *vs, loss_parts)
    return [res[k * n:(k + 1) * n] for k in range(4)], res[-1]


def _rmsnorm_fwd(x, g, riders, name):
    s, d = x.shape
    ts = 512
    nt = s // ts
    nr = len(riders)

    def body(x_ref, g_ref, *rest):
        rider_in, (h_ref, ht_ref) = rest[:nr], rest[nr:nr + 2]
        rider_out, sems = rest[nr + 2:2 * nr + 2], rest[2 * nr + 2:]
        i = pl.program_id(0)
        gather = _Gather(rider_in, rider_out, *sems, relay=True)

        @pl.when(i == 0)
        def _():
            gather.issue()

        xv = x_ref[...]
        rstd = lax.rsqrt(jnp.mean(xv * xv, axis=-1, keepdims=True) + EPS)
        h = xv * rstd * g_ref[...]
        h_ref[...] = h.astype(BF16)
        ht_ref[...] = h.T.astype(BF16)

        @pl.when(i == nt - 1)
        def _():
            gather.finish()

    any_spec = pl.BlockSpec(memory_space=pl.ANY)
    res = pl.pallas_call(
        body, name=name,
        out_shape=[jax.ShapeDtypeStruct((s, d), BF16), jax.ShapeDtypeStruct((d, s), BF16)]
        + _gather_out_shapes(riders),
        grid=(nt,),
        in_specs=[pl.BlockSpec((ts, d), lambda i: (i, 0)), pl.BlockSpec((1, d), lambda i: (0, 0))] + [any_spec] * nr,
        out_specs=[pl.BlockSpec((ts, d), lambda i: (i, 0)), pl.BlockSpec((d, ts), lambda i: (0, i))]
        + [any_spec] * nr,
        scratch_shapes=_gather_semaphores(nr),
        compiler_params=_cparams(("arbitrary",)),
    )(x, g, *riders)
    return res[0], res[1], res[2:]


def _inproj(h, wt, col0, ncols, out_dtype, riders, name):
    s, d = h.shape
    tn = 256
    j0 = col0 // tn
    nj = ncols // tn
    nr = len(riders)

    def body(h_ref, w_ref, *rest):
        rider_in, o_ref, rider_out, sems = rest[:nr], rest[nr], rest[nr + 1:2 * nr + 1], rest[2 * nr + 1:]
        j = pl.program_id(0)
        if nr:
            gather = _Gather(rider_in, rider_out, *sems)

            @pl.when(j == 0)
            def _():
                gather.issue()

        o_ref[...] = _dot_nt(h_ref[...], w_ref[...]).astype(o_ref.dtype)

        if nr:
            @pl.when(j == nj - 1)
            def _():
                gather.finish()

    any_spec = pl.BlockSpec(memory_space=pl.ANY)
    res = pl.pallas_call(
        body, name=name, out_shape=[jax.ShapeDtypeStruct((s, ncols), out_dtype)] + _gather_out_shapes(riders),
        grid=(nj,),
        in_specs=[pl.BlockSpec((s, d), lambda j: (0, 0)), pl.BlockSpec((tn, d), lambda j: (j + j0, 0))]
        + [any_spec] * nr,
        out_specs=[pl.BlockSpec((s, tn), lambda j: (0, j))] + [any_spec] * nr,
        scratch_shapes=_gather_semaphores(nr) if nr else [],
        compiler_params=_cparams(("arbitrary",) if nr else ("parallel",)),
    )(h, wt, *riders)
    return res[0], res[1:]


def _dw_in(ht, dproj, name):
    d, s = ht.shape
    n = dproj.shape[1]
    tn = 512

    def body(a_ref, b_ref, o_ref):
        o_ref[...] = _dot(a_ref[...], b_ref[...]).T.astype(o_ref.dtype)

    return pl.pallas_call(
        body, name=name, out_shape=jax.ShapeDtypeStruct((n, d), BF16), grid=(n // tn,),
        in_specs=[pl.BlockSpec((d, s), lambda j: (0, 0)), pl.BlockSpec((s, tn), lambda j: (0, j))],
        out_specs=pl.BlockSpec((tn, d), lambda j: (j, 0)),
        compiler_params=_cparams(("parallel",)),
    )(ht, dproj)


def _dh_and_grad_x(dproj, wt, x, g, dx2, cps, name):
    s, n = dproj.shape
    d = wt.shape[1]
    tm, tk = min(s, 1024), 512
    nk = n // tk
    nm = s // tm
    nc = len(cps)

    def body(a_ref, w_ref, x_ref, g_ref, dx2_ref, *rest):
        cp_refs, gx_ref, dg_ref = rest[:nc], rest[nc], rest[nc + 1]
        land_refs = rest[nc + 2:2 * nc + 2]
        acc_ref, send_sems, recv_sems = rest[2 * nc + 2:]
        i, k = pl.program_id(0), pl.program_id(1)

        @pl.when((i == 0) & (k == 0))
        def _():
            for cp in _chip_push_copies(cp_refs, land_refs, send_sems, recv_sems):
                cp.start()
            dg_ref[...] = jnp.zeros_like(dg_ref)

        @pl.when(k == 0)
        def _():
            acc_ref[...] = jnp.zeros_like(acc_ref)

        acc_ref[...] += _dot(a_ref[...], w_ref[...])

        @pl.when(k == nk - 1)
        def _():
            dh = acc_ref[...]
            xv = x_ref[...]
            rstd = lax.rsqrt(jnp.mean(xv * xv, axis=-1, keepdims=True) + EPS)
            xhat = xv * rstd
            dg_ref[...] += jnp.sum((dh * xhat).reshape(tm // 8, 8, d), axis=0)
            dxh = dh * g_ref[...]
            gx_ref[...] = dx2_ref[...] + rstd * (dxh - xhat * jnp.mean(dxh * xhat, axis=-1, keepdims=True))

        @pl.when((i == nm - 1) & (k == nk - 1))
        def _():
            for cp in _chip_push_copies(cp_refs, land_refs, send_sems, recv_sems):
                cp.wait()

    any_spec = pl.BlockSpec(memory_space=pl.ANY)
    res = pl.pallas_call(
        body, name=name,
        out_shape=[jax.ShapeDtypeStruct((s, d), F32), jax.ShapeDtypeStruct((8, d), F32)]
        + [jax.ShapeDtypeStruct((3,) + a.shape[1:], a.dtype) for a in cps],
        grid=(nm, nk),
        in_specs=[pl.BlockSpec((tm, tk), lambda i, k: (i, k)), pl.BlockSpec((tk, d), lambda i, k: (k, 0)),
                  pl.BlockSpec((tm, d), lambda i, k: (i, 0)), pl.BlockSpec((1, d), lambda i, k: (0, 0)),
                  pl.BlockSpec((tm, d), lambda i, k: (i, 0))] + [any_spec] * nc,
        out_specs=[pl.BlockSpec((tm, d), lambda i, k: (i, 0)), pl.BlockSpec((8, d), lambda i, k: (0, 0))]
        + [any_spec] * nc,
        scratch_shapes=[pltpu.VMEM((tm, d), F32), pltpu.SemaphoreType.DMA((nc, 3)), pltpu.SemaphoreType.DMA((nc, 3))],
        compiler_params=_cparams(("arbitrary", "arbitrary")),
    )(dproj, wt, x, g, dx2, *cps)
    return res[0], res[1], res[2:]


def _log_sigmoids(z):
    l1p = jnp.log(1.0 + jnp.exp(-jnp.abs(z)))
    ls = jnp.minimum(z, 0.0) - l1p
    return ls, ls - z


def _strict_lower_ones(n):
    row = lax.broadcasted_iota(jnp.int32, (n, n), 0)
    col = lax.broadcasted_iota(jnp.int32, (n, n), 1)
    return row, col, (row > col).astype(BF16)


def _attn_fwd(qkv, riders, name):
    s = qkv.shape[0]
    tb, tt = ATT_BLOCK, ATT_TILE
    nq = s // tb
    per_block = tb // tt
    assert per_block == 2 and s // tt <= LANES, "two query tiles per grid step; one lane of saved carry per key tile"

    nr = len(riders)

    def sweep_body(gather, q_ref, k_ref, v_ref, o_ref, rs_ref, extra_ref, acc_ref, r_ref, rsv_ref, rmax_ref):
        i = pl.program_id(1)
        lane = lax.broadcasted_iota(jnp.int32, (tt, LANES), 1)
        hmask = [lane < HEAD_DIM, lane >= HEAD_DIM]
        row, col, tri = _strict_lower_ones(tt)
        tri2 = jnp.concatenate([tri, tri], axis=0)
        below = col < row
        qrows = [slice(u * tt, (u + 1) * tt) for u in range(per_block)]
        qm = [[jnp.where(m, q_ref[qrows[u], :], jnp.zeros((), BF16)) * jnp.asarray(HEAD_DIM ** -0.5, BF16)
               for m in hmask] for u in range(per_block)]
        acc_ref[...] = jnp.zeros_like(acc_ref)
        r_ref[...] = jnp.zeros_like(r_ref)
        rsv_ref[...] = jnp.full_like(rsv_ref, R_UNREACHED)

        def sweep(tiles, chains):
            heads, nch = range(2), range(len(chains))
            kv = []
            for t in tiles:
                rows = pl.ds(pl.multiple_of(t * tt, tt), tt)
                kv.append((k_ref[rows, :], v_ref[rows, :]))
            z = [[_dot_nt(qm[u][h], kv[ti][0]) for h in heads] for u, ti, _ in chains]
            ls, lk = [], []
            for n in nch:
                pairs = [_log_sigmoids(z[n][h]) for h in heads]
                ls.append([p[0] for p in pairs])
                lk.append([jnp.where(below, p[1], 0.0) if chains[n][2] else p[1] for p in pairs])
            cur = {u: [r_ref[h, qrows[u], :] for h in heads] for u in sorted({c[0] for c in chains})}
            r = []
            for n, (u, _, _) in enumerate(chains):
                r.append(cur[u])
                cur[u] = [cur[u][h] + jnp.sum(lk[n][h], axis=1, keepdims=True) for h in heads]
            for u in cur:
                rmax_ref[u] = jnp.max(jnp.maximum(cur[u][0], cur[u][1]))
            suffix = [[_dot(jnp.concatenate(_split_hi_lo(lk[n][h]), axis=1), tri2) for h in heads] for n in nch]
            w = []
            for n in nch:
                w_n = [jnp.exp(ls[n][h] + suffix[n][h] + r[n][h]) for h in heads]
                if chains[n][2]:
                    w_n = [jnp.where(below, w_h, 0.0) for w_h in w_n]
                w.append([w_h.astype(BF16) for w_h in w_n])
            pv = [[_dot(w[n][h], kv[chains[n][1]][1]) for h in heads] for n in nch]
            for u in cur:
                mine = [n for n in nch if chains[n][0] == u]
                for h in heads:
                    acc_ref[h, qrows[u], :] += functools.reduce(lambda a, b: a + b, [pv[n][h] for n in mine])
                    for n in mine:
                        if not chains[n][2]:
                            t = tiles[chains[n][1]]
                            rsv_ref[h, qrows[u], :] = jnp.where(lane == t, r[n][h], rsv_ref[h, qrows[u], :])
                    r_ref[h, qrows[u], :] = cur[u][h]

        first = per_block * i

        @pl.when(i == 0)
        def _():
            sweep([0, 1], [(0, 0, True), (1, 1, True), (1, 0, False)])

        @pl.when(i > 0)
        def _():
            sweep([first - 1, first, first + 1], [(0, 1, True), (1, 2, True), (0, 0, False), (1, 1, False)])

        for u in range(per_block):
            n_left = first + u - 1

            def live(c, n_left=n_left):
                jj, rmax = c
                return (jj < n_left) & (rmax >= CARRY_FLOOR)

            def step(c, u=u, n_left=n_left):
                jj, _ = c
                sweep([n_left - 1 - jj], [(u, 0, False)])
                return jj + 1, rmax_ref[u]

            swept, _ = lax.while_loop(live, step, (jnp.int32(0), rmax_ref[u]))
            extra_ref[pl.program_id(0), first + u] = swept.astype(F32)

        lane_b = lax.broadcasted_iota(jnp.int32, (tb, LANES), 1)
        o_ref[...] = jnp.where(lane_b < HEAD_DIM, acc_ref[0], acc_ref[1])
        rs_ref[...] = rsv_ref[...]

        @pl.when((pl.program_id(0) == N_PAIRS - 1) & (i == nq - 1))
        def _():
            gather.finish()

    def body(q_ref, k_ref, v_ref, *rest):
        rider_in, (o_ref, rs_ref, extra_ref) = rest[:nr], rest[nr:nr + 3]
        rider_out = rest[nr + 3:2 * nr + 3]
        acc_ref, r_ref, rsv_ref, rmax_ref = rest[2 * nr + 3:2 * nr + 7]
        gather = _Gather(rider_in, rider_out, *rest[2 * nr + 7:])

        @pl.when((pl.program_id(0) == 0) & (pl.program_id(1) == 0))
        def _():
            gather.issue()

        sweep_body(gather, q_ref, k_ref, v_ref, o_ref, rs_ref, extra_ref, acc_ref, r_ref, rsv_ref, rmax_ref)

    any_spec = pl.BlockSpec(memory_space=pl.ANY)
    res = pl.pallas_call(
        body, name=name,
        out_shape=[jax.ShapeDtypeStruct((s, D_SB), F32), jax.ShapeDtypeStruct((N_HEADS, s, LANES), F32),
                   jax.ShapeDtypeStruct((N_PAIRS, s // tt), F32)] + _gather_out_shapes(riders),
        grid=(N_PAIRS, nq),
        in_specs=[pl.BlockSpec((tb, LANES), lambda p, i: (i, p)),
                  pl.BlockSpec((s, LANES), lambda p, i: (0, N_PAIRS + p)),
                  pl.BlockSpec((s, LANES), lambda p, i: (0, 2 * N_PAIRS + p))] + [any_spec] * nr,
        out_specs=[pl.BlockSpec((tb, LANES), lambda p, i: (i, p)),
                   pl.BlockSpec((2, tb, LANES), lambda p, i: (p, i, 0)),
                   pl.BlockSpec(memory_space=pltpu.SMEM)] + [any_spec] * nr,
        scratch_shapes=[pltpu.VMEM((2, tb, LANES), F32), pltpu.VMEM((2, tb, LANES), F32),
                        pltpu.VMEM((2, tb, LANES), F32), pltpu.SMEM((per_block,), F32)] + _gather_semaphores(nr),
        compiler_params=_cparams(("arbitrary", "arbitrary")),
    )(qkv, qkv, qkv, *riders)
    return res[0], res[1], res[2], res[3:]


def _attn_bwd(qkv, do, rs, extra, dproj, smalls, cps, name):
    s = qkv.shape[0]
    tb, tt = ATT_BLOCK, ATT_TILE
    nq = s // tb
    per_block = tb // tt
    scale = HEAD_DIM ** -0.5

    ns, nc = len(smalls), len(cps)

    def sweep_body(q_ref, k_ref, v_ref, do_ref, rs_ref, extra_ref, dproj_hbm, out_hbm, dq_acc, dk_acc, dv_acc,
                   dqi_ref, pc_ref, stage_ref, out_sems):
        del dproj_hbm
        pair = pl.program_id(0)
        lane = lax.broadcasted_iota(jnp.int32, (tt, LANES), 1)
        hmask = [lane < HEAD_DIM, lane >= HEAD_DIM]
        row, col, tri = _strict_lower_ones(tt)
        tri2 = jnp.concatenate([tri, tri], axis=0)
        triu = (row < col).astype(BF16)
        below = col < row
        zero = jnp.zeros((), BF16)
        qrows = [slice(u * tt, (u + 1) * tt) for u in range(per_block)]
        dq_acc[...] = jnp.zeros_like(dq_acc)
        dk_acc[...] = jnp.zeros_like(dk_acc)
        dv_acc[...] = jnp.zeros_like(dv_acc)

        def qblock(i, carry):
            block_rows = pl.ds(pl.multiple_of(i * tb, tb), tb)
            q2 = q_ref[block_rows, :]
            do2 = do_ref[block_rows, :]
            qm = [[jnp.where(m, q2[qrows[u]], zero) * jnp.asarray(scale, BF16) for m in hmask]
                  for u in range(per_block)]
            dom = [[jnp.where(m, do2[qrows[u]], zero) for m in hmask] for u in range(per_block)]
            rs_i = [[rs_ref[h, pl.ds(pl.multiple_of(i * tb + u * tt, tt), tt), :] for h in range(2)]
                    for u in range(per_block)]
            dqi_ref[...] = jnp.zeros_like(dqi_ref)
            pc_ref[...] = jnp.zeros_like(pc_ref)
            first = per_block * i

            def sweep(tiles, chains):
                heads, nch = range(2), range(len(chains))
                krows = [pl.ds(pl.multiple_of(t * tt, tt), tt) for t in tiles]
                k2 = [k_ref[rows, :] for rows in krows]
                v2 = [v_ref[rows, :] for rows in krows]
                z = [[_dot_nt(qm[u][h], k2[ti]) for h in heads] for u, ti, _ in chains]
                dw = [[_dot_nt(dom[u][h], v2[ti]) for h in heads] for u, ti, _ in chains]
                ls, lk = [], []
                for n in nch:
                    pairs = [_log_sigmoids(z[n][h]) for h in heads]
                    ls.append([p[0] for p in pairs])
                    lk.append([jnp.where(below, p[1], 0.0) if chains[n][2] else p[1] for p in pairs])
                suffix = [[_dot(jnp.concatenate(_split_hi_lo(lk[n][h]), axis=1), tri2) for h in heads] for n in nch]
                w, g = [], []
                for n, (u, ti, diag) in enumerate(chains):
                    w_n = []
                    for h in heads:
                        logw = ls[n][h] + suffix[n][h]
                        if diag:
                            w_n.append(jnp.where(below, jnp.exp(logw), 0.0))
                        else:
                            carry_in = jnp.sum(jnp.where(lane == tiles[ti], rs_i[u][h], 0.0), axis=1, keepdims=True)
                            w_n.append(jnp.exp(logw + carry_in))
                    w.append([w_h.astype(BF16) for w_h in w_n])
                    g.append([w_n[h] * dw[n][h] for h in heads])
                prefix = [[_dot(g[n][h].astype(BF16), triu) for h in heads] for n in nch]
                cur = {u: [pc_ref[h, qrows[u], :] for h in heads] for u in sorted({c[0] for c in chains})}
                pc = []
                for n, (u, _, _) in enumerate(chains):
                    pc.append(cur[u])
                    cur[u] = [cur[u][h] + jnp.sum(g[n][h], axis=1, keepdims=True) for h in heads]
                dzb = []
                for n in nch:
                    dz_n = []
                    for h in heads:
                        sig = jnp.exp(ls[n][h])
                        dz = g[n][h] - sig * (g[n][h] + prefix[n][h] + pc[n][h])
                        if chains[n][2]:
                            dz = jnp.where(below, dz, 0.0)
                        dz_n.append(dz.astype(BF16))
                    dzb.append(dz_n)
                for u in cur:
                    for h in heads:
                        pc_ref[h, qrows[u], :] = cur[u][h]
                dq = [[_dot(dzb[n][h], jnp.where(hmask[h], k2[chains[n][1]], zero)) for h in heads] for n in nch]
                dk = [[_dot_tn(dzb[n][h], qm[chains[n][0]][h]) for h in heads] for n in nch]
                dv = [[_dot_tn(w[n][h], dom[chains[n][0]][h]) for h in heads] for n in nch]
                add = lambda a, b: a + b
                for u in cur:
                    dqi_ref[qrows[u], :] += functools.reduce(
                        add, [dq[n][h] for n in nch if chains[n][0] == u for h in heads])
                for ti in range(len(tiles)):
                    mine = [n for n in nch if chains[n][1] == ti]
                    dk_acc[krows[ti], :] += functools.reduce(add, [dk[n][h] for n in mine for h in heads])
                    dv_acc[krows[ti], :] += functools.reduce(add, [dv[n][h] for n in mine for h in heads])

            for u in range(per_block):
                n_left = first + u - 1
                n_extra = jnp.clip(extra_ref[pair, first + u].astype(jnp.int32), 0, jnp.maximum(n_left, 0))

                def step(t, c, u=u):
                    sweep([t], [(u, 0, False)])
                    return c

                lax.fori_loop(n_left - n_extra, n_left, step, 0)

            @pl.when(i == 0)
            def _():
                sweep([0, 1], [(0, 0, True), (1, 0, False), (1, 1, True)])

            @pl.when(i > 0)
            def _():
                sweep([first - 1, first, first + 1], [(0, 0, False), (1, 1, False), (0, 1, True), (1, 2, True)])

            dq_acc[block_rows, :] += dqi_ref[...] * scale
            return carry

        lax.fori_loop(0, nq, qblock, 0)
        copies = []
        for t, acc in enumerate((dq_acc, dk_acc, dv_acc)):
            stage_ref[t] = acc[...].astype(BF16)
            col0 = pl.multiple_of(t * D_SB + pair * LANES, LANES)
            copies.append(pltpu.make_async_copy(stage_ref.at[t], out_hbm.at[:, pl.ds(col0, LANES)], out_sems.at[t]))
        for cp in copies:
            cp.start()
        for cp in copies:
            cp.wait()

    def body(q_ref, k_ref, v_ref, do_ref, rs_ref, extra_ref, dproj_hbm, *rest):
        small_in, cp_in = rest[:ns], rest[ns:ns + nc]
        out_hbm = rest[ns + nc]
        small_out, land_out = rest[ns + nc + 1:2 * ns + nc + 1], rest[2 * ns + nc + 1:2 * (ns + nc) + 1]
        scratch = rest[2 * (ns + nc) + 1:]
        gather = _Gather(small_in, small_out, *scratch[7:10])
        pair = pl.program_id(0)

        @pl.when(pair == 0)
        def _():
            gather.issue()
            for cp in _chip_push_copies(cp_in, land_out, *scratch[10:]):
                cp.start()

        sweep_body(q_ref, k_ref, v_ref, do_ref, rs_ref, extra_ref, dproj_hbm, out_hbm, *scratch[:7])

        @pl.when(pair == N_PAIRS - 1)
        def _():
            gather.finish()
            for cp in _chip_push_copies(cp_in, land_out, *scratch[10:]):
                cp.wait()

    any_spec = pl.BlockSpec(memory_space=pl.ANY)
    res = pl.pallas_call(
        body, name=name,
        out_shape=[jax.ShapeDtypeStruct(dproj.shape, BF16)] + _gather_out_shapes(smalls)
        + [jax.ShapeDtypeStruct((3,) + a.shape[1:], a.dtype) for a in cps],
        grid=(N_PAIRS,),
        in_specs=[pl.BlockSpec((s, LANES), lambda p: (0, p)),
                  pl.BlockSpec((s, LANES), lambda p: (0, N_PAIRS + p)),
                  pl.BlockSpec((s, LANES), lambda p: (0, 2 * N_PAIRS + p)),
                  pl.BlockSpec((s, LANES), lambda p: (0, p)),
                  pl.BlockSpec((2, s, LANES), lambda p: (p, 0, 0)),
                  pl.BlockSpec(memory_space=pltpu.SMEM),
                  any_spec] + [any_spec] * (ns + nc),
        out_specs=[any_spec] * (1 + ns + nc),
        scratch_shapes=[pltpu.VMEM((s, LANES), F32), pltpu.VMEM((s, LANES), F32), pltpu.VMEM((s, LANES), F32),
                        pltpu.VMEM((tb, LANES), F32), pltpu.VMEM((2, tb, LANES), F32),
                        pltpu.VMEM((3, s, LANES), BF16), pltpu.SemaphoreType.DMA((3,))]
        + _gather_semaphores(ns) + [pltpu.SemaphoreType.DMA((nc, 3)), pltpu.SemaphoreType.DMA((nc, 3))],
        input_output_aliases={6: 0},
        compiler_params=_cparams(("arbitrary",)),
    )(qkv, qkv, qkv, do, rs, extra, dproj, *smalls, *cps)
    return res[0], res[1:1 + ns], res[1 + ns:]


def _mid(o, rest, x, tgt, wua, wub, wout, fg, lng, lnb, wsp, bfull, gavg, name):
    s, d = x.shape
    ts = 256
    nt = s // ts
    nchunk = ts // SGU_CHUNK
    n_rest = rest.shape[1]

    def body(o_ref, rest_ref, x_ref, t_ref, wua_ref, wub_ref, wout_ref, fg_ref, lng_ref, lnb_ref, wsp_ref, bfull_ref,
             gavg_ref, loss_ref, dx2_ref, do_ref, dproj_ref, dwua_ref, dwub_ref, dwout_ref, dfg_ref, dlng_ref,
             dlnb_ref, dwsp_ref, dbfull_ref):
        step = pl.program_id(0)

        @pl.when(step == 0)
        def _():
            for ref in (loss_ref, dwua_ref, dwub_ref, dwout_ref, dfg_ref, dlng_ref, dlnb_ref, dwsp_ref, dbfull_ref):
                ref[...] = jnp.zeros_like(ref)

        gavg = gavg_ref[...]

        def gmean(a):
            return _dot(a.astype(BF16), gavg)

        def colsum8(a):
            return jnp.sum(a.reshape(ts // 8, 8, a.shape[1]), axis=0)

        z_a = rest_ref[:, 0:512]
        u_b = rest_ref[:, 512:1024]
        v_b = rest_ref[:, 1024:1536]
        z_b = rest_ref[:, 1536:2048]
        g_a = rest_ref[:, 2048:2048 + d]
        g_b = rest_ref[:, 2048 + d:2048 + 2 * d]
        ov = o_ref[...]
        sa = _sigmoid(z_a)
        silu_a = z_a * sa
        y_a = ov * silu_a
        ug, dug_du = _gelu_and_grad(u_b)
        vg, dvg_dv = _gelu_and_grad(v_b)
        mu = gmean(vg)
        cen = vg - mu
        rstd_g = lax.rsqrt(gmean(cen * cen) + EPS)
        vhat = cen * rstd_g
        vn = vhat * lng_ref[...] + lnb_ref[...]
        vnb = vn.astype(BF16)

        t_idx = lax.broadcasted_iota(jnp.int32, (SGU_CHUNK, SGU_CHUNK), 0)
        s_idx = lax.broadcasted_iota(jnp.int32, (SGU_CHUNK, SGU_CHUNK), 1)
        causal = (s_idx // CHUNK) <= (t_idx // CHUNK)
        wm = [jnp.where(causal, wsp_ref[g], 0.0) for g in range(N_GROUPS)]
        wmb = [w.astype(BF16) for w in wm]
        wmtb = [w.T.astype(BF16) for w in wm]
        lane = lax.broadcasted_iota(jnp.int32, (SGU_CHUNK, LANES), 1)
        first = lane < GROUP_DIM
        bfull = bfull_ref[...]

        mixed_rows = []
        for n in range(nchunk):
            r0, r1 = n * SGU_CHUNK, (n + 1) * SGU_CHUNK
            pieces = []
            for p in range(N_GROUPS // 2):
                blk = vnb[r0:r1, p * LANES:(p + 1) * LANES]
                pieces.append(jnp.where(first, _dot(wmb[2 * p], blk), _dot(wmb[2 * p + 1], blk)))
            mixed_rows.append(jnp.concatenate(pieces, axis=1) + bfull)
        mixed = jnp.concatenate(mixed_rows, axis=0)
        sg = ug * mixed
        sb = _sigmoid(z_b)
        silu_b = z_b * sb
        y_b = sg * silu_b
        y_ab = y_a.astype(BF16)
        y_bb = y_b.astype(BF16)
        p_a = _dot(y_ab, wua_ref[...])
        p_b = _dot(y_bb, wub_ref[...])
        ga_s = _sigmoid(g_a)
        gb_s = _sigmoid(g_b)
        merged_b = (ga_s * p_a + gb_s * p_b).astype(BF16)
        x2 = x_ref[...] + _dot(merged_b, wout_ref[...])
        rstd = lax.rsqrt(jnp.mean(x2 * x2, axis=-1, keepdims=True) + EPS)
        xhat = x2 * rstd
        fg_v = fg_ref[...]
        diff = xhat * fg_v - t_ref[...]
        loss_ref[...] += 0.5 * jnp.sum(jnp.sum(diff * diff, axis=-1, keepdims=True) * (1.0 / d))

        dy = diff * (1.0 / d)
        dfg_ref[...] += colsum8(dy * xhat)
        dxh = dy * fg_v
        dx2 = rstd * (dxh - xhat * jnp.mean(dxh * xhat, axis=-1, keepdims=True))
        dx2_ref[...] = dx2
        dx2b = dx2.astype(BF16)
        dwout_ref[...] += _dot_tn(merged_b, dx2b)
        dmerged = _dot_nt(dx2b, wout_ref[...])
        dp_a = dmerged * ga_s
        dp_b = dmerged * gb_s
        dproj_ref[:, QKV_COLS + 2048:QKV_COLS + 2048 + d] = (dmerged * p_a * (ga_s * (1.0 - ga_s))).astype(BF16)
        dproj_ref[:, QKV_COLS + 2048 + d:QKV_COLS + 2048 + 2 * d] = (dmerged * p_b * (gb_s * (1.0 - gb_s))).astype(BF16)
        dp_ab = dp_a.astype(BF16)
        dp_bb = dp_b.astype(BF16)
        dwua_ref[...] += _dot_tn(y_ab, dp_ab)
        dwub_ref[...] += _dot_tn(y_bb, dp_bb)
        dy_a = _dot_nt(dp_ab, wua_ref[...])
        dy_b = _dot_nt(dp_bb, wub_ref[...])
        do_ref[...] = (dy_a * silu_a).astype(BF16)
        dproj_ref[:, QKV_COLS:QKV_COLS + 512] = (dy_a * ov * (sa * (1.0 + z_a * (1.0 - sa)))).astype(BF16)
        dsg = dy_b * silu_b
        dproj_ref[:, QKV_COLS + 1536:QKV_COLS + 2048] = (dy_b * sg * (sb * (1.0 + z_b * (1.0 - sb)))).astype(BF16)
        dproj_ref[:, QKV_COLS + 512:QKV_COLS + 1024] = (dsg * mixed * dug_du).astype(BF16)
        dmixed = dsg * ug
        dmb = dmixed.astype(BF16)
        zero = jnp.zeros((), BF16)
        dvn_rows = []
        db = jnp.zeros((SGU_CHUNK, D_SGU), F32)
        for n in range(nchunk):
            r0, r1 = n * SGU_CHUNK, (n + 1) * SGU_CHUNK
            db = db + dmixed[r0:r1, :]
            pieces = []
            for p in range(N_GROUPS // 2):
                cols = slice(p * LANES, (p + 1) * LANES)
                dm_blk = dmb[r0:r1, cols]
                vn_blk = vnb[r0:r1, cols]
                dwsp_ref[2 * p] += _dot_nt(jnp.where(first, dm_blk, zero), vn_blk)
                dwsp_ref[2 * p + 1] += _dot_nt(jnp.where(first, zero, dm_blk), vn_blk)
                pieces.append(jnp.where(first, _dot(wmtb[2 * p], dm_blk), _dot(wmtb[2 * p + 1], dm_blk)))
            dvn_rows.append(jnp.concatenate(pieces, axis=1))
        dbfull_ref[...] += db
        dvn = jnp.concatenate(dvn_rows, axis=0)
        dlng_ref[...] += colsum8(dvn * vhat)
        dlnb_ref[...] += colsum8(dvn)
        dvhat = dvn * lng_ref[...]
        dcen = rstd_g * (dvhat - gmean(dvhat) - vhat * gmean(dvhat * vhat))
        dproj_ref[:, QKV_COLS + 1024:QKV_COLS + 1536] = (dcen * dvg_dv).astype(BF16)

        @pl.when(step == nt - 1)
        def _():
            for g in range(N_GROUPS):
                dwsp_ref[g] = jnp.where(causal, dwsp_ref[g], 0.0)

    def tile(cols):
        return pl.BlockSpec((ts, cols), lambda i: (i, 0))

    def whole(shape):
        return pl.BlockSpec(shape, lambda i: (0,) * len(shape))

    out_shapes = [
        jax.ShapeDtypeStruct((8, LANES), F32),
        jax.ShapeDtypeStruct((s, d), F32),
        jax.ShapeDtypeStruct((s, D_SB), BF16),
        jax.ShapeDtypeStruct((s, QKV_COLS + n_rest), BF16),
        jax.ShapeDtypeStruct((D_SB, d), F32),
        jax.ShapeDtypeStruct((D_SGU, d), F32),
        jax.ShapeDtypeStruct((d, d), F32),
        jax.ShapeDtypeStruct((8, d), F32),
        jax.ShapeDtypeStruct((8, D_SGU), F32),
        jax.ShapeDtypeStruct((8, D_SGU), F32),
        jax.ShapeDtypeStruct((N_GROUPS, SGU_CHUNK, SGU_CHUNK), F32),
        jax.ShapeDtypeStruct((SGU_CHUNK, D_SGU), F32),
    ]
    out_specs = [whole((8, LANES)), tile(d), tile(D_SB), tile(QKV_COLS + n_rest), whole((D_SB, d)), whole((D_SGU, d)),
                 whole((d, d)), whole((8, d)), whole((8, D_SGU)), whole((8, D_SGU)),
                 whole((N_GROUPS, SGU_CHUNK, SGU_CHUNK)), whole((SGU_CHUNK, D_SGU))]
    in_specs = [tile(D_SB), tile(n_rest), tile(d), tile(d), whole((D_SB, d)), whole((D_SGU, d)), whole((d, d)),
                whole((1, d)), whole((1, D_SGU)), whole((1, D_SGU)), whole((N_GROUPS, SGU_CHUNK, SGU_CHUNK)),
                whole((SGU_CHUNK, D_SGU)), whole((D_SGU, D_SGU))]
    return pl.pallas_call(
        body, name=name, out_shape=out_shapes, grid=(nt,), in_specs=in_specs, out_specs=out_specs,
        compiler_params=_cparams(("arbitrary",)),
    )(o, rest, x, tgt, wua, wub, wout, fg, lng, lnb, wsp, bfull, gavg)


def _small_reduce(dfg8, dlng8, dlnb8, dbfull, name):
    d = dfg8.shape[1]

    def body(dfg_ref, dlng_ref, dlnb_ref, dbfull_ref, fg_out, lng_out, lnb_out, b_out):
        row = lax.broadcasted_iota(jnp.int32, (D_SGU, LANES), 0)
        col = lax.broadcasted_iota(jnp.int32, (D_SGU, LANES), 1)

        def select_sum(a, sel):
            hi, lo = _split_hi_lo(a)
            lo2 = (a - hi.astype(F32) - lo.astype(F32)).astype(BF16)
            return _dot(hi, sel) + _dot(lo, sel) + _dot(lo2, sel)

        def by_group(partials):
            v = jnp.sum(partials, axis=0, keepdims=True)
            g_of_lane = lax.broadcasted_iota(jnp.int32, (N_GROUPS, D_SGU), 1) // GROUP_DIM
            g_of_row = lax.broadcasted_iota(jnp.int32, (N_GROUPS, D_SGU), 0)
            spread = jnp.where(g_of_lane == g_of_row, jnp.broadcast_to(v, (N_GROUPS, D_SGU)), 0.0)
            return select_sum(spread, (row % GROUP_DIM == col).astype(BF16))[:, 0:GROUP_DIM]

        fg_out[...] = jnp.sum(dfg_ref[...], axis=0, keepdims=True)
        lng_out[...] = by_group(dlng_ref[...])
        lnb_out[...] = by_group(dlnb_ref[...])
        by_pos = select_sum(dbfull_ref[...], (row // GROUP_DIM == col).astype(BF16))
        b_out[...] = by_pos.T[0:N_GROUPS, :]

    return pl.pallas_call(
        body, name=name,
        out_shape=[jax.ShapeDtypeStruct((1, d), F32), jax.ShapeDtypeStruct((N_GROUPS, GROUP_DIM), F32),
                   jax.ShapeDtypeStruct((N_GROUPS, GROUP_DIM), F32), jax.ShapeDtypeStruct((N_GROUPS, SGU_CHUNK), F32)],
        compiler_params=_cparams(),
    )(dfg8, dlng8, dlnb8, dbfull)


def _block_major_cols(w):
    r, n = w.shape
    return jnp.transpose(w.reshape(r, N_DEV, n // N_DEV), (1, 0, 2))


def _from_block_major_cols(w):
    nb, r, c = w.shape
    return jnp.transpose(w, (1, 0, 2)).reshape(r, nb * c)


def kernel(x, norm_g, w_in, sgu_ln_g, sgu_ln_b, w_spatial, b_spatial, w_up_a, w_up_b, w_out, final_norm_g, loss_target, m_norm_g, m_w_in, m_sgu_ln_g, m_sgu_ln_b, m_w_spatial, m_b_spatial, m_w_up_a, m_w_up_b, m_w_out, m_final_norm_g, v_norm_g, v_w_in, v_sgu_ln_g, v_sgu_ln_b, v_w_spatial, v_b_spatial, v_w_up_a, v_w_up_b, v_w_out, v_final_norm_g):
    s, d = x.shape[1], x.shape[2]
    xs = x[0]
    tgt = loss_target[0]
    cx, cy, cc = _coords()
    core = jnp.reshape(cc, (1,)).astype(jnp.int32)
    chip = jnp.reshape(2 * cx + cy, (1,)).astype(jnp.int32)

    h, ht, (g_win,) = _rmsnorm_fwd(xs, norm_g, [jnp.transpose(w_in[0]).astype(BF16)], "norm")
    d_in = N_DEV * w_in.shape[2]
    w_full = g_win.reshape(d_in, d)
    qkv, _ = _inproj(h, w_full, 0, QKV_COLS, BF16, [], "inproj_qkv")
    rest, _ = _inproj(h, w_full, QKV_COLS, d_in - QKV_COLS, F32, [], "inproj_rest")
    o, rs, extra, (g_wua, g_wub, g_wout) = _attn_fwd(
        qkv, [w_up_a[0].astype(BF16), w_up_b[0].astype(BF16), w_out[0].astype(BF16)], "attn_fwd")
    wua_full = _from_block_major_cols(g_wua)
    wub_full = _from_block_major_cols(g_wub)
    wout_full = g_wout.reshape(d, d)

    lng = sgu_ln_g.reshape(1, D_SGU)
    lnb = sgu_ln_b.reshape(1, D_SGU)
    bfull = jnp.repeat(jnp.transpose(b_spatial[0]), GROUP_DIM, axis=1)
    grp = jnp.arange(D_SGU) // GROUP_DIM
    gavg = jnp.where(grp[:, None] == grp[None, :], 1.0 / GROUP_DIM, 0.0).astype(BF16)
    (loss_b, dx2, do, dproj, dwua, dwub, dwout, dfg8, dlng8, dlnb8, dwsp, dbfull) = _mid(
        o, rest, xs, tgt, wua_full, wub_full, wout_full, final_norm_g.reshape(1, d), lng, lnb, w_spatial[0], bfull,
        gavg, "mid")

    def blocks42(a):
        return a.reshape((4, 2) + a.shape[1:])

    def chip_partials(own, tag):
        land = _push_sibling(own, "rs_sibling_" + tag)
        return [_chip_partial_sum(a, l, core, "cpsum_%s%d" % (tag, i)) for i, (a, l) in enumerate(zip(own, land))]

    cps_up = chip_partials([blocks42(_block_major_cols(dwua.astype(BF16))),
                            blocks42(_block_major_cols(dwub.astype(BF16))),
                            blocks42(dwout.astype(BF16).reshape(N_DEV, d // N_DEV, d))], "up")

    dfg, dlng, dlnb, db = _small_reduce(dfg8, dlng8, dlnb8, dbfull, "small_reduce")
    wsp_rows = N_GROUPS * SGU_CHUNK
    dproj, small_parts, land_up = _attn_bwd(
        qkv, do, rs, extra, dproj, [dfg, dlng, dlnb, db, dwsp.reshape(wsp_rows, SGU_CHUNK)], cps_up, "attn_bwd")
    dwin = _dw_in(ht, dproj, "dwin")

    cps_in = chip_partials([dwin.reshape(4, 2, d_in // N_DEV, d)], "in")
    grad_x, dng8, land_in = _dh_and_grad_x(dproj, w_full, xs, norm_g, dx2, cps_in, "dh")

    ng_parts, loss_parts = _allgather([dng8, loss_b], "ag_tail")

    def small_layouts(ng, fg, lg, lb, bs, ws):
        return [ng.reshape(1, d), fg.reshape(1, d), lg.reshape(N_GROUPS, GROUP_DIM), lb.reshape(N_GROUPS, GROUP_DIM),
                bs.reshape(N_GROUPS, SGU_CHUNK), ws.reshape(wsp_rows, SGU_CHUNK)]

    sm, loss_sum = _adam_small(
        [ng_parts] + list(small_parts),
        small_layouts(norm_g, final_norm_g, sgu_ln_g, sgu_ln_b, b_spatial, w_spatial),
        small_layouts(m_norm_g, m_final_norm_g, m_sgu_ln_g, m_sgu_ln_b, m_b_spatial, m_w_spatial),
        small_layouts(v_norm_g, v_final_norm_g, v_sgu_ln_g, v_sgu_ln_b, v_b_spatial, v_w_spatial),
        loss_parts, "adam_small")
    small_shapes = [norm_g.shape, final_norm_g.shape, sgu_ln_g.shape, sgu_ln_b.shape, b_spatial.shape, w_spatial.shape]
    sm = [[a.reshape(shp) for a, shp in zip(kind, small_shapes)] for kind in sm]

    res = _adam_shard(cps_in, list(land_in), chip, *[jnp.transpose(a[0]) for a in (w_in, m_w_in, v_w_in)], "adam0")
    big = [[jnp.transpose(r)[None] for r in res]]
    for i, (w, m, v) in enumerate([(w_up_a, m_w_up_a, v_w_up_a), (w_up_b, m_w_up_b, v_w_up_b),
                                   (w_out, m_w_out, v_w_out)]):
        res = _adam_shard([cps_up[i]], [land_up[i]], chip, w[0], m[0], v[0], "adam%d" % (i + 1))
        big.append([r[None] for r in res])

    loss = loss_sum[0, 0]

    def per_kind(kd):
        return [sm[kd][0], big[0][kd], sm[kd][2], sm[kd][3], sm[kd][5], sm[kd][4], big[1][kd], big[2][kd], big[3][kd],
                sm[kd][1]]

    return (loss, grad_x[None], *per_kind(0), *per_kind(1), *per_kind(2), *per_kind(3))
```

```python
import functools
import math

import jax
import jax.numpy as jnp
from jax import lax
from jax.experimental import pallas as pl
from jax.experimental.pallas import tpu as pltpu

F32 = jnp.float32
BF16 = jnp.bfloat16
MESH = pl.DeviceIdType.MESH

N_DEV = 8
N_HEADS = 8
HEAD_DIM = 64
D_SB = N_HEADS * HEAD_DIM
N_GROUPS = 8
GROUP_DIM = 64
D_SGU = N_GROUPS * GROUP_DIM
SGU_CHUNK = 128
CHUNK = 64
EPS = 1e-6
LANES = 128
N_PAIRS = N_HEADS // 2
QKV_COLS = 3 * D_SB
ATT_BLOCK = 256
ATT_TILE = 128
CARRY_FLOOR = -90.0
R_UNREACHED = -1e30

ADAM_LR = 0.001
ADAM_B1 = 0.9
ADAM_B2 = 0.999
ADAM_EPS = 1e-08
ADAM_WD = 0.01
ADAM_STEP = 10

VMEM_LIMIT = 56 * 1024 * 1024


def _cparams(sem=None, vmem=VMEM_LIMIT):
    return pltpu.CompilerParams(dimension_semantics=sem, vmem_limit_bytes=vmem)


def _dot(a, b):
    return jnp.dot(a, b, preferred_element_type=F32)


def _dot_nt(a, b):
    return lax.dot_general(a, b, (((1,), (1,)), ((), ())), preferred_element_type=F32)


def _dot_tn(a, b):
    return lax.dot_general(a, b, (((0,), (0,)), ((), ())), preferred_element_type=F32)


def _split_hi_lo(a):
    hi = a.astype(BF16)
    lo = (a - hi.astype(F32)).astype(BF16)
    return hi, lo


def _sigmoid(x):
    return 1.0 / (1.0 + jnp.exp(-x))


_GELU_C = math.sqrt(2.0 / math.pi)


def _gelu_and_grad(x):
    x2 = x * x
    inner = _GELU_C * (x + 0.044715 * (x2 * x))
    t = jnp.tanh(inner)
    cdf = 0.5 * (1.0 + t)
    g = x * cdf
    dg = cdf + x * (0.5 * (1.0 - t * t)) * (_GELU_C * (1.0 + 3.0 * 0.044715 * x2))
    return g, dg


def _coords():
    return lax.axis_index("x"), lax.axis_index("y"), lax.axis_index("c")


def _dev_index(px, py, pc):
    return 4 * px + 2 * py + pc


def _allgather(blocks, name):
    n = len(blocks)

    def body(*refs):
        gather = _Gather(refs[:n], refs[n:2 * n], *refs[2 * n:])
        gather.issue()
        gather.finish()

    any_spec = pl.BlockSpec(memory_space=pl.ANY)
    return pl.pallas_call(
        body, name=name,
        out_shape=_gather_out_shapes(blocks),
        in_specs=[any_spec] * n, out_specs=[any_spec] * n,
        scratch_shapes=_gather_semaphores(n),
    )(*blocks)


def _gather_out_shapes(blocks):
    return [jax.ShapeDtypeStruct((N_DEV,) + b.shape, b.dtype) for b in blocks]


def _gather_semaphores(n):
    return [pltpu.SemaphoreType.DMA((n, 7)), pltpu.SemaphoreType.DMA((n, 7)), pltpu.SemaphoreType.DMA((n,))]


class _Gather:
    def __init__(self, ins, outs, send_sems, recv_sems, local_sems, relay=False):
        self.ins, self.outs = ins, outs
        self.send_sems, self.recv_sems, self.local_sems = send_sems, recv_sems, local_sems
        self.n = len(ins)
        self.relay = relay
        x, y, c = _coords()
        self.c = c
        self.me, self.sibling = (x, y, c), (x, y, 1 - c)
        self.chips = [(1 - x, y), (x, 1 - y), (1 - x, 1 - y)]

    def _copy(self, a, k, block, to, src=None):
        dst = self.outs[a].at[_dev_index(*block)]
        return pltpu.make_async_remote_copy(
            src_ref=dst if src is None else src, dst_ref=dst,
            send_sem=self.send_sems.at[a, k], recv_sem=self.recv_sems.at[a, k],
            device_id=to, device_id_type=MESH)

    def _mine(self):
        return [pltpu.make_async_copy(self.ins[a], self.outs[a].at[_dev_index(*self.me)], self.local_sems.at[a])
                for a in range(self.n)]

    def _first(self):
        first = []
        direct = self.chips[:2] if self.relay else self.chips
        for a in range(self.n):
            first.append(self._copy(a, 0, self.me, self.sibling, src=self.ins[a]))
            first += [self._copy(a, 1 + j, self.me, (*chip, self.c), src=self.ins[a])
                      for j, chip in enumerate(direct)]
        return first

    def issue(self):
        for cp in self._mine() + self._first():
            cp.start()

    def _pass_on(self, a, j):
        chip = self.chips[j]
        self._copy(a, 1 + j, (*chip, self.c), self.me).wait_recv()
        self._copy(a, 4 + j, (*chip, self.c), self.sibling).start()

    def finish(self):
        c = self.c
        if self.relay:
            for core in range(2):
                @pl.when(c == core)
                def _(core=core):
                    j_src, j_dst = core, 1 - core
                    for a in range(self.n):
                        self._pass_on(a, j_src)
                        self._copy(a, 3, (*self.chips[j_src], c), (*self.chips[j_dst], c)).start()
                    for a in range(self.n):
                        self._pass_on(a, j_dst)
                    for a in range(self.n):
                        self._pass_on(a, 2)
        else:
            for j in range(3):
                for a in range(self.n):
                    self._pass_on(a, j)
        for a in range(self.n):
            self._copy(a, 0, self.sibling, self.me).wait_recv()
            for j, chip in enumerate(self.chips):
                self._copy(a, 4 + j, (*chip, 1 - c), self.me).wait_recv()
        for a in range(self.n):
            for k in range(7):
                self._copy(a, k, self.me, self.sibling).wait_send()
        for cp in self._mine():
            cp.wait()


def _sibling_sum(arrs, core, name):
    n = len(arrs)

    def body(core_ref, *refs):
        del core_ref
        own_hbm, own_blk, land, cp = refs[:n], refs[n:2 * n], refs[2 * n:3 * n], refs[3 * n:4 * n]
        tmp = refs[4 * n:5 * n]
        send_sems, recv_sems, local_sems = refs[5 * n:]
        k = pl.program_id(0)
        x, y, c = _coords()

        def push(a, kk):
            return pltpu.make_async_remote_copy(
                src_ref=own_hbm[a].at[kk, 1 - c], dst_ref=land[a].at[kk],
                send_sem=send_sems.at[a, kk], recv_sem=recv_sems.at[a, kk],
                device_id=(x, y, 1 - c), device_id_type=MESH)

        @pl.when(k == 0)
        def _():
            for a in range(n):
                for kk in range(4):
                    push(a, kk).start()

        for a in range(n):
            push(a, k).wait_recv()
            fetch = pltpu.make_async_copy(land[a].at[k], tmp[a], local_sems.at[a])
            fetch.start()
            fetch.wait()
            cp[a][...] = (own_blk[a][...].astype(F32) + tmp[a][...].astype(F32)).astype(cp[a].dtype)

        @pl.when(k == 3)
        def _():
            for a in range(n):
                for kk in range(4):
                    push(a, kk).wait_send()

    any_spec = pl.BlockSpec(memory_space=pl.ANY)
    blocks = [a.shape[2:] for a in arrs]
    res = pl.pallas_call(
        body, name=name,
        out_shape=[jax.ShapeDtypeStruct((4,) + b, a.dtype) for a, b in zip(arrs, blocks)] * 2,
        grid_spec=pltpu.PrefetchScalarGridSpec(
            num_scalar_prefetch=1, grid=(4,),
            in_specs=[any_spec] * n
            + [pl.BlockSpec((None, None) + b, lambda k, core: (k, core[0], 0, 0)) for b in blocks],
            out_specs=[any_spec] * n + [pl.BlockSpec((None,) + b, lambda k, core: (k, 0, 0)) for b in blocks],
            scratch_shapes=[pltpu.VMEM(b, a.dtype) for a, b in zip(arrs, blocks)]
            + [pltpu.SemaphoreType.DMA((n, 4)), pltpu.SemaphoreType.DMA((n, 4)), pltpu.SemaphoreType.DMA((n,))]),
        compiler_params=_cparams(("arbitrary",)),
    )(core, *arrs, *arrs)
    return list(res[n:])


def _chip_push_copies(ins, outs, send_sems, recv_sems):
    x, y, c = _coords()
    chips = [(1 - x, y), (x, 1 - y), (1 - x, 1 - y)]
    return [pltpu.make_async_remote_copy(
        src_ref=ins[a].at[2 * px + py], dst_ref=outs[a].at[r],
        send_sem=send_sems.at[a, r], recv_sem=recv_sems.at[a, r],
        device_id=(px, py, c), device_id_type=MESH)
        for a in range(len(ins)) for r, (px, py) in enumerate(chips)]


def _adamw_math(w, g, m, v):
    m = ADAM_B1 * m + (1.0 - ADAM_B1) * g
    v = ADAM_B2 * v + (1.0 - ADAM_B2) * (g * g)
    m_hat = m / (1.0 - ADAM_B1 ** ADAM_STEP)
    v_hat = v / (1.0 - ADAM_B2 ** ADAM_STEP)
    delta = -ADAM_LR * (m_hat / (jnp.sqrt(v_hat) + ADAM_EPS) + ADAM_WD * w)
    return delta, m, v


def _adam_shard(cps, lands, chip, w, m, v, name):
    rows, cols = w.shape
    tr = rows // 4
    nparts = len(cps)

    def body(chip_ref, *refs):
        del chip_ref
        cp_refs, land_refs = refs[:nparts], refs[nparts:2 * nparts]
        w_ref, m_ref, v_ref, g_out, d_out, m_out, v_out = refs[2 * nparts:]
        parts = []
        for cp_ref, land_ref in zip(cp_refs, land_refs):
            g_k = cp_ref[...].astype(F32)
            for r in range(3):
                g_k = g_k + land_ref[r].astype(F32)
            parts.append(g_k)
        g = parts[0] if nparts == 1 else jnp.concatenate(parts, axis=1)
        delta, m_new, v_new = _adamw_math(w_ref[...], g, m_ref[...], v_ref[...])
        g_out[...] = g
        d_out[...] = delta
        m_out[...] = m_new
        v_out[...] = v_new

    tile = pl.BlockSpec((tr, cols), lambda r, chip: (r, 0))
    out = jax.ShapeDtypeStruct((rows, cols), F32)
    return pl.pallas_call(
        body, name=name, out_shape=[out] * 4,
        grid_spec=pltpu.PrefetchScalarGridSpec(
            num_scalar_prefetch=1, grid=(rows // tr,),
            in_specs=[pl.BlockSpec((None, tr, a.shape[2]), lambda r, chip: (chip[0], r, 0)) for a in cps]
            + [pl.BlockSpec((3, tr, a.shape[2]), lambda r, chip: (0, r, 0)) for a in lands]
            + [tile, tile, tile],
            out_specs=[tile] * 4),
        compiler_params=_cparams(("parallel",)),
    )(chip, *cps, *lands, w, m, v)


def _adam_small(parts, ws, ms, vs, loss_parts, name):
    n = len(ws)

    def body(*refs):
        p_refs, w_refs, m_refs, v_refs = refs[:n], refs[n:2 * n], refs[2 * n:3 * n], refs[3 * n:4 * n]
        loss_ref, outs, loss_out = refs[4 * n], refs[4 * n + 1:-1], refs[-1]
        total = loss_ref[0]
        for dev in range(1, N_DEV):
            total = total + loss_ref[dev]
        loss_out[...] = total
        for i in range(n):
            g = p_refs[i][0]
            for dev in range(1, N_DEV):
                g = g + p_refs[i][dev]
            if g.shape[0] != w_refs[i].shape[0]:
                g = jnp.sum(g, axis=0, keepdims=True)
            delta, m_new, v_new = _adamw_math(w_refs[i][...], g, m_refs[i][...], v_refs[i][...])
            outs[i][...] = g
            outs[n + i][...] = delta
            outs[2 * n + i][...] = m_new
            outs[3 * n + i][...] = v_new

    out_shapes = [jax.ShapeDtypeStruct(w.shape, F32) for w in ws] * 4
    out_shapes.append(jax.ShapeDtypeStruct(loss_parts.shape[1:], F32))
    res = pl.pallas_call(body, name=name, out_shape=out_shapes, compiler_params=_cparams())(
        *parts, *ws, *ms, *vs, loss_parts)
    return [res[k * n:(k + 1) * n] for k in range(4)], res[-1]


def _rmsnorm_fwd(x, g, riders, name):
    s, d = x.shape
    ts = 512
    nt = s // ts
    nr = len(riders)

    def body(x_ref, g_ref, *rest):
        rider_in, (h_ref, ht_ref) = rest[:nr], rest[nr:nr + 2]
        rider_out, sems = rest[nr + 2:2 * nr + 2], rest[2 * nr + 2:]
        i = pl.program_id(0)
        gather = _Gather(rider_in, rider_out, *sems, relay=True)

        @pl.when(i == 0)
        def _():
            gather.issue()

        xv = x_ref[...]
        rstd = lax.rsqrt(jnp.mean(xv * xv, axis=-1, keepdims=True) + EPS)
        h = xv * rstd * g_ref[...]
        h_ref[...] = h.astype(BF16)
        ht_ref[...] = h.T.astype(BF16)

        @pl.when(i == nt - 1)
        def _():
            gather.finish()

    any_spec = pl.BlockSpec(memory_space=pl.ANY)
    res = pl.pallas_call(
        body, name=name,
        out_shape=[jax.ShapeDtypeStruct((s, d), BF16), jax.ShapeDtypeStruct((d, s), BF16)]
        + _gather_out_shapes(riders),
        grid=(nt,),
        in_specs=[pl.BlockSpec((ts, d), lambda i: (i, 0)), pl.BlockSpec((1, d), lambda i: (0, 0))] + [any_spec] * nr,
        out_specs=[pl.BlockSpec((ts, d), lambda i: (i, 0)), pl.BlockSpec((d, ts), lambda i: (0, i))]
        + [any_spec] * nr,
        scratch_shapes=_gather_semaphores(nr),
        compiler_params=_cparams(("arbitrary",)),
    )(x, g, *riders)
    return res[0], res[1], res[2:]


def _inproj(h, wt, col0, ncols, out_dtype, riders, name):
    s, d = h.shape
    tn = 256
    j0 = col0 // tn
    nj = ncols // tn
    nr = len(riders)

    def body(h_ref, w_ref, *rest):
        rider_in, o_ref, rider_out, sems = rest[:nr], rest[nr], rest[nr + 1:2 * nr + 1], rest[2 * nr + 1:]
        j = pl.program_id(0)
        if nr:
            gather = _Gather(rider_in, rider_out, *sems)

            @pl.when(j == 0)
            def _():
                gather.issue()

        o_ref[...] = _dot_nt(h_ref[...], w_ref[...]).astype(o_ref.dtype)

        if nr:
            @pl.when(j == nj - 1)
            def _():
                gather.finish()

    any_spec = pl.BlockSpec(memory_space=pl.ANY)
    res = pl.pallas_call(
        body, name=name, out_shape=[jax.ShapeDtypeStruct((s, ncols), out_dtype)] + _gather_out_shapes(riders),
        grid=(nj,),
        in_specs=[pl.BlockSpec((s, d), lambda j: (0, 0)), pl.BlockSpec((tn, d), lambda j: (j + j0, 0))]
        + [any_spec] * nr,
        out_specs=[pl.BlockSpec((s, tn), lambda j: (0, j))] + [any_spec] * nr,
        scratch_shapes=_gather_semaphores(nr) if nr else [],
        compiler_params=_cparams(("arbitrary",) if nr else ("parallel",)),
    )(h, wt, *riders)
    return res[0], res[1:]


def _dw_in(ht, dproj, name):
    d, s = ht.shape
    n = dproj.shape[1]
    tn = 512

    def body(a_ref, b_ref, o_ref):
        o_ref[...] = _dot(a_ref[...], b_ref[...]).T.astype(o_ref.dtype)

    return pl.pallas_call(
        body, name=name, out_shape=jax.ShapeDtypeStruct((n, d), BF16), grid=(n // tn,),
        in_specs=[pl.BlockSpec((d, s), lambda j: (0, 0)), pl.BlockSpec((s, tn), lambda j: (0, j))],
        out_specs=pl.BlockSpec((tn, d), lambda j: (j, 0)),
        compiler_params=_cparams(("parallel",)),
    )(ht, dproj)


def _dh_and_grad_x(dproj, wt, x, g, dx2, cps, name):
    s, n = dproj.shape
    d = wt.shape[1]
    tm, tk = min(s, 1024), 512
    nk = n // tk
    nm = s // tm
    nc = len(cps)

    def body(a_ref, w_ref, x_ref, g_ref, dx2_ref, *rest):
        cp_refs, gx_ref, dg_ref = rest[:nc], rest[nc], rest[nc + 1]
        land_refs = rest[nc + 2:2 * nc + 2]
        acc_ref, send_sems, recv_sems = rest[2 * nc + 2:]
        i, k = pl.program_id(0), pl.program_id(1)

        @pl.when((i == 0) & (k == 0))
        def _():
            for cp in _chip_push_copies(cp_refs, land_refs, send_sems, recv_sems):
                cp.start()
            dg_ref[...] = jnp.zeros_like(dg_ref)

        @pl.when(k == 0)
        def _():
            acc_ref[...] = jnp.zeros_like(acc_ref)

        acc_ref[...] += _dot(a_ref[...], w_ref[...])

        @pl.when(k == nk - 1)
        def _():
            dh = acc_ref[...]
            xv = x_ref[...]
            rstd = lax.rsqrt(jnp.mean(xv * xv, axis=-1, keepdims=True) + EPS)
            xhat = xv * rstd
            dg_ref[...] += jnp.sum((dh * xhat).reshape(tm // 8, 8, d), axis=0)
            dxh = dh * g_ref[...]
            gx_ref[...] = dx2_ref[...] + rstd * (dxh - xhat * jnp.mean(dxh * xhat, axis=-1, keepdims=True))

        @pl.when((i == nm - 1) & (k == nk - 1))
        def _():
            for cp in _chip_push_copies(cp_refs, land_refs, send_sems, recv_sems):
                cp.wait()

    any_spec = pl.BlockSpec(memory_space=pl.ANY)
    res = pl.pallas_call(
        body, name=name,
        out_shape=[jax.ShapeDtypeStruct((s, d), F32), jax.ShapeDtypeStruct((8, d), F32)]
        + [jax.ShapeDtypeStruct((3,) + a.shape[1:], a.dtype) for a in cps],
        grid=(nm, nk),
        in_specs=[pl.BlockSpec((tm, tk), lambda i, k: (i, k)), pl.BlockSpec((tk, d), lambda i, k: (k, 0)),
                  pl.BlockSpec((tm, d), lambda i, k: (i, 0)), pl.BlockSpec((1, d), lambda i, k: (0, 0)),
                  pl.BlockSpec((tm, d), lambda i, k: (i, 0))] + [any_spec] * nc,
        out_specs=[pl.BlockSpec((tm, d), lambda i, k: (i, 0)), pl.BlockSpec((8, d), lambda i, k: (0, 0))]
        + [any_spec] * nc,
        scratch_shapes=[pltpu.VMEM((tm, d), F32), pltpu.SemaphoreType.DMA((nc, 3)), pltpu.SemaphoreType.DMA((nc, 3))],
        compiler_params=_cparams(("arbitrary", "arbitrary")),
    )(dproj, wt, x, g, dx2, *cps)
    return res[0], res[1], res[2:]


def _log_sigmoids(z):
    l1p = jnp.log(1.0 + jnp.exp(-jnp.abs(z)))
    ls = jnp.minimum(z, 0.0) - l1p
    return ls, ls - z


def _strict_lower_ones(n):
    row = lax.broadcasted_iota(jnp.int32, (n, n), 0)
    col = lax.broadcasted_iota(jnp.int32, (n, n), 1)
    return row, col, (row > col).astype(BF16)


def _attn_fwd(qkv, riders, name):
    s = qkv.shape[0]
    tb, tt = ATT_BLOCK, ATT_TILE
    nq = s // tb
    per_block = tb // tt
    assert per_block == 2 and s // tt <= LANES, "two query tiles per grid step; one lane of saved carry per key tile"

    nr = len(riders)

    def sweep_body(gather, q_ref, k_ref, v_ref, o_ref, rs_ref, extra_ref, acc_ref, r_ref, rsv_ref, rmax_ref):
        i = pl.program_id(1)
        lane = lax.broadcasted_iota(jnp.int32, (tt, LANES), 1)
        hmask = [lane < HEAD_DIM, lane >= HEAD_DIM]
        row, col, tri = _strict_lower_ones(tt)
        tri2 = jnp.concatenate([tri, tri], axis=0)
        below = col < row
        qrows = [slice(u * tt, (u + 1) * tt) for u in range(per_block)]
        qm = [[jnp.where(m, q_ref[qrows[u], :], jnp.zeros((), BF16)) * jnp.asarray(HEAD_DIM ** -0.5, BF16)
               for m in hmask] for u in range(per_block)]
        acc_ref[...] = jnp.zeros_like(acc_ref)
        r_ref[...] = jnp.zeros_like(r_ref)
        rsv_ref[...] = jnp.full_like(rsv_ref, R_UNREACHED)

        def sweep(tiles, chains):
            heads, nch = range(2), range(len(chains))
            kv = []
            for t in tiles:
                rows = pl.ds(pl.multiple_of(t * tt, tt), tt)
                kv.append((k_ref[rows, :], v_ref[rows, :]))
            z = [[_dot_nt(qm[u][h], kv[ti][0]) for h in heads] for u, ti, _ in chains]
            ls, lk = [], []
            for n in nch:
                pairs = [_log_sigmoids(z[n][h]) for h in heads]
                ls.append([p[0] for p in pairs])
                lk.append([jnp.where(below, p[1], 0.0) if chains[n][2] else p[1] for p in pairs])
            cur = {u: [r_ref[h, qrows[u], :] for h in heads] for u in sorted({c[0] for c in chains})}
            r = []
            for n, (u, _, _) in enumerate(chains):
                r.append(cur[u])
                cur[u] = [cur[u][h] + jnp.sum(lk[n][h], axis=1, keepdims=True) for h in heads]
            for u in cur:
                rmax_ref[u] = jnp.max(jnp.maximum(cur[u][0], cur[u][1]))
            suffix = [[_dot(jnp.concatenate(_split_hi_lo(lk[n][h]), axis=1), tri2) for h in heads] for n in nch]
            w = []
            for n in nch:
                w_n = [jnp.exp(ls[n][h] + suffix[n][h] + r[n][h]) for h in heads]
                if chains[n][2]:
                    w_n = [jnp.where(below, w_h, 0.0) for w_h in w_n]
                w.append([w_h.astype(BF16) for w_h in w_n])
            pv = [[_dot(w[n][h], kv[chains[n][1]][1]) for h in heads] for n in nch]
            for u in cur:
                mine = [n for n in nch if chains[n][0] == u]
                for h in heads:
                    acc_ref[h, qrows[u], :] += functools.reduce(lambda a, b: a + b, [pv[n][h] for n in mine])
                    for n in mine:
                        if not chains[n][2]:
                            t = tiles[chains[n][1]]
                            rsv_ref[h, qrows[u], :] = jnp.where(lane == t, r[n][h], rsv_ref[h, qrows[u], :])
                    r_ref[h, qrows[u], :] = cur[u][h]

        first = per_block * i

        @pl.when(i == 0)
        def _():
            sweep([0, 1], [(0, 0, True), (1, 1, True), (1, 0, False)])

        @pl.when(i > 0)
        def _():
            sweep([first - 1, first, first + 1], [(0, 1, True), (1, 2, True), (0, 0, False), (1, 1, False)])

        for u in range(per_block):
            n_left = first + u - 1

            def live(c, n_left=n_left):
                jj, rmax = c
                return (jj < n_left) & (rmax >= CARRY_FLOOR)

            def step(c, u=u, n_left=n_left):
                jj, _ = c
                sweep([n_left - 1 - jj], [(u, 0, False)])
                return jj + 1, rmax_ref[u]

            swept, _ = lax.while_loop(live, step, (jnp.int32(0), rmax_ref[u]))
            extra_ref[pl.program_id(0), first + u] = swept.astype(F32)

        lane_b = lax.broadcasted_iota(jnp.int32, (tb, LANES), 1)
        o_ref[...] = jnp.where(lane_b < HEAD_DIM, acc_ref[0], acc_ref[1])
        rs_ref[...] = rsv_ref[...]

        @pl.when((pl.program_id(0) == N_PAIRS - 1) & (i == nq - 1))
        def _():
            gather.finish()

    def body(q_ref, k_ref, v_ref, *rest):
        rider_in, (o_ref, rs_ref, extra_ref) = rest[:nr], rest[nr:nr + 3]
        rider_out = rest[nr + 3:2 * nr + 3]
        acc_ref, r_ref, rsv_ref, rmax_ref = rest[2 * nr + 3:2 * nr + 7]
        gather = _Gather(rider_in, rider_out, *rest[2 * nr + 7:])

        @pl.when((pl.program_id(0) == 0) & (pl.program_id(1) == 0))
        def _():
            gather.issue()

        sweep_body(gather, q_ref, k_ref, v_ref, o_ref, rs_ref, extra_ref, acc_ref, r_ref, rsv_ref, rmax_ref)

    any_spec = pl.BlockSpec(memory_space=pl.ANY)
    res = pl.pallas_call(
        body, name=name,
        out_shape=[jax.ShapeDtypeStruct((s, D_SB), F32), jax.ShapeDtypeStruct((N_HEADS, s, LANES), F32),
                   jax.ShapeDtypeStruct((N_PAIRS, s // tt), F32)] + _gather_out_shapes(riders),
        grid=(N_PAIRS, nq),
        in_specs=[pl.BlockSpec((tb, LANES), lambda p, i: (i, p)),
                  pl.BlockSpec((s, LANES), lambda p, i: (0, N_PAIRS + p)),
                  pl.BlockSpec((s, LANES), lambda p, i: (0, 2 * N_PAIRS + p))] + [any_spec] * nr,
        out_specs=[pl.BlockSpec((tb, LANES), lambda p, i: (i, p)),
                   pl.BlockSpec((2, tb, LANES), lambda p, i: (p, i, 0)),
                   pl.BlockSpec(memory_space=pltpu.SMEM)] + [any_spec] * nr,
        scratch_shapes=[pltpu.VMEM((2, tb, LANES), F32), pltpu.VMEM((2, tb, LANES), F32),
                        pltpu.VMEM((2, tb, LANES), F32), pltpu.SMEM((per_block,), F32)] + _gather_semaphores(nr),
        compiler_params=_cparams(("arbitrary", "arbitrary")),
    )(qkv, qkv, qkv, *riders)
    return res[0], res[1], res[2], res[3:]


def _attn_bwd(qkv, do, rs, extra, dproj, smalls, cps, name):
    s = qkv.shape[0]
    tb, tt = ATT_BLOCK, ATT_TILE
    nq = s // tb
    per_block = tb // tt
    scale = HEAD_DIM ** -0.5

    ns, nc = len(smalls), len(cps)

    def sweep_body(q_ref, k_ref, v_ref, do_ref, rs_ref, extra_ref, dproj_hbm, out_hbm, dq_acc, dk_acc, dv_acc,
                   dqi_ref, pc_ref, stage_ref, out_sems):
        del dproj_hbm
        pair = pl.program_id(0)
        lane = lax.broadcasted_iota(jnp.int32, (tt, LANES), 1)
        hmask = [lane < HEAD_DIM, lane >= HEAD_DIM]
        row, col, tri = _strict_lower_ones(tt)
        tri2 = jnp.concatenate([tri, tri], axis=0)
        triu = (row < col).astype(BF16)
        below = col < row
        zero = jnp.zeros((), BF16)
        qrows = [slice(u * tt, (u + 1) * tt) for u in range(per_block)]
        dq_acc[...] = jnp.zeros_like(dq_acc)
        dk_acc[...] = jnp.zeros_like(dk_acc)
        dv_acc[...] = jnp.zeros_like(dv_acc)

        def qblock(i, carry):
            block_rows = pl.ds(pl.multiple_of(i * tb, tb), tb)
            q2 = q_ref[block_rows, :]
            do2 = do_ref[block_rows, :]
            qm = [[jnp.where(m, q2[qrows[u]], zero) * jnp.asarray(scale, BF16) for m in hmask]
                  for u in range(per_block)]
            dom = [[jnp.where(m, do2[qrows[u]], zero) for m in hmask] for u in range(per_block)]
            rs_i = [[rs_ref[h, pl.ds(pl.multiple_of(i * tb + u * tt, tt), tt), :] for h in range(2)]
                    for u in range(per_block)]
            dqi_ref[...] = jnp.zeros_like(dqi_ref)
            pc_ref[...] = jnp.zeros_like(pc_ref)
            first = per_block * i

            def sweep(tiles, chains):
                heads, nch = range(2), range(len(chains))
                krows = [pl.ds(pl.multiple_of(t * tt, tt), tt) for t in tiles]
                k2 = [k_ref[rows, :] for rows in krows]
                v2 = [v_ref[rows, :] for rows in krows]
                z = [[_dot_nt(qm[u][h], k2[ti]) for h in heads] for u, ti, _ in chains]
                dw = [[_dot_nt(dom[u][h], v2[ti]) for h in heads] for u, ti, _ in chains]
                ls, lk = [], []
                for n in nch:
                    pairs = [_log_sigmoids(z[n][h]) for h in heads]
                    ls.append([p[0] for p in pairs])
                    lk.append([jnp.where(below, p[1], 0.0) if chains[n][2] else p[1] for p in pairs])
                suffix = [[_dot(jnp.concatenate(_split_hi_lo(lk[n][h]), axis=1), tri2) for h in heads] for n in nch]
                w, g = [], []
                for n, (u, ti, diag) in enumerate(chains):
                    w_n = []
                    for h in heads:
                        logw = ls[n][h] + suffix[n][h]
                        if diag:
                            w_n.append(jnp.where(below, jnp.exp(logw), 0.0))
                        else:
                            carry_in = jnp.sum(jnp.where(lane == tiles[ti], rs_i[u][h], 0.0), axis=1, keepdims=True)
                            w_n.append(jnp.exp(logw + carry_in))
                    w.append([w_h.astype(BF16) for w_h in w_n])
                    g.append([w_n[h] * dw[n][h] for h in heads])
                prefix = [[_dot(g[n][h].astype(BF16), triu) for h in heads] for n in nch]
                cur = {u: [pc_ref[h, qrows[u], :] for h in heads] for u in sorted({c[0] for c in chains})}
                pc = []
                for n, (u, _, _) in enumerate(chains):
                    pc.append(cur[u])
                    cur[u] = [cur[u][h] + jnp.sum(g[n][h], axis=1, keepdims=True) for h in heads]
                dzb = []
                for n in nch:
                    dz_n = []
                    for h in heads:
                        sig = jnp.exp(ls[n][h])
                        dz = g[n][h] - sig * (g[n][h] + prefix[n][h] + pc[n][h])
                        if chains[n][2]:
                            dz = jnp.where(below, dz, 0.0)
                        dz_n.append(dz.astype(BF16))
                    dzb.append(dz_n)
                for u in cur:
                    for h in heads:
                        pc_ref[h, qrows[u], :] = cur[u][h]
                dq = [[_dot(dzb[n][h], jnp.where(hmask[h], k2[chains[n][1]], zero)) for h in heads] for n in nch]
                dk = [[_dot_tn(dzb[n][h], qm[chains[n][0]][h]) for h in heads] for n in nch]
                dv = [[_dot_tn(w[n][h], dom[chains[n][0]][h]) for h in heads] for n in nch]
                add = lambda a, b: a + b
                for u in cur:
                    dqi_ref[qrows[u], :] += functools.reduce(
                        add, [dq[n][h] for n in nch if chains[n][0] == u for h in heads])
                for ti in range(len(tiles)):
                    mine = [n for n in nch if chains[n][1] == ti]
                    dk_acc[krows[ti], :] += functools.reduce(add, [dk[n][h] for n in mine for h in heads])
                    dv_acc[krows[ti], :] += functools.reduce(add, [dv[n][h] for n in mine for h in heads])

            for u in range(per_block):
                n_left = first + u - 1
                n_extra = jnp.clip(extra_ref[pair, first + u].astype(jnp.int32), 0, jnp.maximum(n_left, 0))

                def step(t, c, u=u):
                    sweep([t], [(u, 0, False)])
                    return c

                lax.fori_loop(n_left - n_extra, n_left, step, 0)

            @pl.when(i == 0)
            def _():
                sweep([0, 1], [(0, 0, True), (1, 0, False), (1, 1, True)])

            @pl.when(i > 0)
            def _():
                sweep([first - 1, first, first + 1], [(0, 0, False), (1, 1, False), (0, 1, True), (1, 2, True)])

            dq_acc[block_rows, :] += dqi_ref[...] * scale
            return carry

        lax.fori_loop(0, nq, qblock, 0)
        copies = []
        for t, acc in enumerate((dq_acc, dk_acc, dv_acc)):
            stage_ref[t] = acc[...].astype(BF16)
            col0 = pl.multiple_of(t * D_SB + pair * LANES, LANES)
            copies.append(pltpu.make_async_copy(stage_ref.at[t], out_hbm.at[:, pl.ds(col0, LANES)], out_sems.at[t]))
        for cp in copies:
            cp.start()
        for cp in copies:
            cp.wait()

    def body(q_ref, k_ref, v_ref, do_ref, rs_ref, extra_ref, dproj_hbm, *rest):
        small_in, cp_in = rest[:ns], rest[ns:ns + nc]
        out_hbm = rest[ns + nc]
        small_out, land_out = rest[ns + nc + 1:2 * ns + nc + 1], rest[2 * ns + nc + 1:2 * (ns + nc) + 1]
        scratch = rest[2 * (ns + nc) + 1:]
        gather = _Gather(small_in, small_out, *scratch[7:10])
        pair = pl.program_id(0)

        @pl.when(pair == 0)
        def _():
            gather.issue()
            for cp in _chip_push_copies(cp_in, land_out, *scratch[10:]):
                cp.start()

        sweep_body(q_ref, k_ref, v_ref, do_ref, rs_ref, extra_ref, dproj_hbm, out_hbm, *scratch[:7])

        @pl.when(pair == N_PAIRS - 1)
        def _():
            gather.finish()
            for cp in _chip_push_copies(cp_in, land_out, *scratch[10:]):
                cp.wait()

    any_spec = pl.BlockSpec(memory_space=pl.ANY)
    res = pl.pallas_call(
        body, name=name,
        out_shape=[jax.ShapeDtypeStruct(dproj.shape, BF16)] + _gather_out_shapes(smalls)
        + [jax.ShapeDtypeStruct((3,) + a.shape[1:], a.dtype) for a in cps],
        grid=(N_PAIRS,),
        in_specs=[pl.BlockSpec((s, LANES), lambda p: (0, p)),
                  pl.BlockSpec((s, LANES), lambda p: (0, N_PAIRS + p)),
                  pl.BlockSpec((s, LANES), lambda p: (0, 2 * N_PAIRS + p)),
                  pl.BlockSpec((s, LANES), lambda p: (0, p)),
                  pl.BlockSpec((2, s, LANES), lambda p: (p, 0, 0)),
                  pl.BlockSpec(memory_space=pltpu.SMEM),
                  any_spec] + [any_spec] * (ns + nc),
        out_specs=[any_spec] * (1 + ns + nc),
        scratch_shapes=[pltpu.VMEM((s, LANES), F32), pltpu.VMEM((s, LANES), F32), pltpu.VMEM((s, LANES), F32),
                        pltpu.VMEM((tb, LANES), F32), pltpu.VMEM((2, tb, LANES), F32),
                        pltpu.VMEM((3, s, LANES), BF16), pltpu.SemaphoreType.DMA((3,))]
        + _gather_semaphores(ns) + [pltpu.SemaphoreType.DMA((nc, 3)), pltpu.SemaphoreType.DMA((nc, 3))],
        input_output_aliases={6: 0},
        compiler_params=_cparams(("arbitrary",)),
    )(qkv, qkv, qkv, do, rs, extra, dproj, *smalls, *cps)
    return res[0], res[1:1 + ns], res[1 + ns:]


def _mid(o, rest, x, tgt, wua, wub, wout, fg, lng, lnb, wsp, bfull, gavg, name):
    s, d = x.shape
    ts = 256
    nt = s // ts
    nchunk = ts // SGU_CHUNK
    n_rest = rest.shape[1]

    def body(o_ref, rest_ref, x_ref, t_ref, wua_ref, wub_ref, wout_ref, fg_ref, lng_ref, lnb_ref, wsp_ref, bfull_ref,
             gavg_ref, loss_ref, dx2_ref, do_ref, dproj_ref, dwua_ref, dwub_ref, dwout_ref, dfg_ref, dlng_ref,
             dlnb_ref, dwsp_ref, dbfull_ref):
        step = pl.program_id(0)

        @pl.when(step == 0)
        def _():
            for ref in (loss_ref, dwua_ref, dwub_ref, dwout_ref, dfg_ref, dlng_ref, dlnb_ref, dwsp_ref, dbfull_ref):
                ref[...] = jnp.zeros_like(ref)

        gavg = gavg_ref[...]

        def gmean(a):
            return _dot(a.astype(BF16), gavg)

        def colsum8(a):
            return jnp.sum(a.reshape(ts // 8, 8, a.shape[1]), axis=0)

        z_a = rest_ref[:, 0:512]
        u_b = rest_ref[:, 512:1024]
        v_b = rest_ref[:, 1024:1536]
        z_b = rest_ref[:, 1536:2048]
        g_a = rest_ref[:, 2048:2048 + d]
        g_b = rest_ref[:, 2048 + d:2048 + 2 * d]
        ov = o_ref[...]
        sa = _sigmoid(z_a)
        silu_a = z_a * sa
        y_a = ov * silu_a
        ug, dug_du = _gelu_and_grad(u_b)
        vg, dvg_dv = _gelu_and_grad(v_b)
        mu = gmean(vg)
        cen = vg - mu
        rstd_g = lax.rsqrt(gmean(cen * cen) + EPS)
        vhat = cen * rstd_g
        vn = vhat * lng_ref[...] + lnb_ref[...]
        vnb = vn.astype(BF16)

        t_idx = lax.broadcasted_iota(jnp.int32, (SGU_CHUNK, SGU_CHUNK), 0)
        s_idx = lax.broadcasted_iota(jnp.int32, (SGU_CHUNK, SGU_CHUNK), 1)
        causal = (s_idx // CHUNK) <= (t_idx // CHUNK)
        wm = [jnp.where(causal, wsp_ref[g], 0.0) for g in range(N_GROUPS)]
        wmb = [w.astype(BF16) for w in wm]
        wmtb = [w.T.astype(BF16) for w in wm]
        lane = lax.broadcasted_iota(jnp.int32, (SGU_CHUNK, LANES), 1)
        first = lane < GROUP_DIM
        bfull = bfull_ref[...]

        mixed_rows = []
        for n in range(nchunk):
            r0, r1 = n * SGU_CHUNK, (n + 1) * SGU_CHUNK
            pieces = []
            for p in range(N_GROUPS // 2):
                blk = vnb[r0:r1, p * LANES:(p + 1) * LANES]
                pieces.append(jnp.where(first, _dot(wmb[2 * p], blk), _dot(wmb[2 * p + 1], blk)))
            mixed_rows.append(jnp.concatenate(pieces, axis=1) + bfull)
        mixed = jnp.concatenate(mixed_rows, axis=0)
        sg = ug * mixed
        sb = _sigmoid(z_b)
        silu_b = z_b * sb
        y_b = sg * silu_b
        y_ab = y_a.astype(BF16)
        y_bb = y_b.astype(BF16)
        p_a = _dot(y_ab, wua_ref[...])
        p_b = _dot(y_bb, wub_ref[...])
        ga_s = _sigmoid(g_a)
        gb_s = _sigmoid(g_b)
        merged_b = (ga_s * p_a + gb_s * p_b).astype(BF16)
        x2 = x_ref[...] + _dot(merged_b, wout_ref[...])
        rstd = lax.rsqrt(jnp.mean(x2 * x2, axis=-1, keepdims=True) + EPS)
        xhat = x2 * rstd
        fg_v = fg_ref[...]
        diff = xhat * fg_v - t_ref[...]
        loss_ref[...] += 0.5 * jnp.sum(jnp.sum(diff * diff, axis=-1, keepdims=True) * (1.0 / d))

        dy = diff * (1.0 / d)
        dfg_ref[...] += colsum8(dy * xhat)
        dxh = dy * fg_v
        dx2 = rstd * (dxh - xhat * jnp.mean(dxh * xhat, axis=-1, keepdims=True))
        dx2_ref[...] = dx2
        dx2b = dx2.astype(BF16)
        dwout_ref[...] += _dot_tn(merged_b, dx2b)
        dmerged = _dot_nt(dx2b, wout_ref[...])
        dp_a = dmerged * ga_s
        dp_b = dmerged * gb_s
        dproj_ref[:, QKV_COLS + 2048:QKV_COLS + 2048 + d] = (dmerged * p_a * (ga_s * (1.0 - ga_s))).astype(BF16)
        dproj_ref[:, QKV_COLS + 2048 + d:QKV_COLS + 2048 + 2 * d] = (dmerged * p_b * (gb_s * (1.0 - gb_s))).astype(BF16)
        dp_ab = dp_a.astype(BF16)
        dp_bb = dp_b.astype(BF16)
        dwua_ref[...] += _dot_tn(y_ab, dp_ab)
        dwub_ref[...] += _dot_tn(y_bb, dp_bb)
        dy_a = _dot_nt(dp_ab, wua_ref[...])
        dy_b = _dot_nt(dp_bb, wub_ref[...])
        do_ref[...] = (dy_a * silu_a).astype(BF16)
        dproj_ref[:, QKV_COLS:QKV_COLS + 512] = (dy_a * ov * (sa * (1.0 + z_a * (1.0 - sa)))).astype(BF16)
        dsg = dy_b * silu_b
        dproj_ref[:, QKV_COLS + 1536:QKV_COLS + 2048] = (dy_b * sg * (sb * (1.0 + z_b * (1.0 - sb)))).astype(BF16)
        dproj_ref[:, QKV_COLS + 512:QKV_COLS + 1024] = (dsg * mixed * dug_du).astype(BF16)
        dmixed = dsg * ug
        dmb = dmixed.astype(BF16)
        zero = jnp.zeros((), BF16)
        dvn_rows = []
        db = jnp.zeros((SGU_CHUNK, D_SGU), F32)
        for n in range(nchunk):
            r0, r1 = n * SGU_CHUNK, (n + 1) * SGU_CHUNK
            db = db + dmixed[r0:r1, :]
            pieces = []
            for p in range(N_GROUPS // 2):
                cols = slice(p * LANES, (p + 1) * LANES)
                dm_blk = dmb[r0:r1, cols]
                vn_blk = vnb[r0:r1, cols]
                dwsp_ref[2 * p] += _dot_nt(jnp.where(first, dm_blk, zero), vn_blk)
                dwsp_ref[2 * p + 1] += _dot_nt(jnp.where(first, zero, dm_blk), vn_blk)
                pieces.append(jnp.where(first, _dot(wmtb[2 * p], dm_blk), _dot(wmtb[2 * p + 1], dm_blk)))
            dvn_rows.append(jnp.concatenate(pieces, axis=1))
        dbfull_ref[...] += db
        dvn = jnp.concatenate(dvn_rows, axis=0)
        dlng_ref[...] += colsum8(dvn * vhat)
        dlnb_ref[...] += colsum8(dvn)
        dvhat = dvn * lng_ref[...]
        dcen = rstd_g * (dvhat - gmean(dvhat) - vhat * gmean(dvhat * vhat))
        dproj_ref[:, QKV_COLS + 1024:QKV_COLS + 1536] = (dcen * dvg_dv).astype(BF16)

        @pl.when(step == nt - 1)
        def _():
            for g in range(N_GROUPS):
                dwsp_ref[g] = jnp.where(causal, dwsp_ref[g], 0.0)

    def tile(cols):
        return pl.BlockSpec((ts, cols), lambda i: (i, 0))

    def whole(shape):
        return pl.BlockSpec(shape, lambda i: (0,) * len(shape))

    out_shapes = [
        jax.ShapeDtypeStruct((8, LANES), F32),
        jax.ShapeDtypeStruct((s, d), F32),
        jax.ShapeDtypeStruct((s, D_SB), BF16),
        jax.ShapeDtypeStruct((s, QKV_COLS + n_rest), BF16),
        jax.ShapeDtypeStruct((D_SB, d), F32),
        jax.ShapeDtypeStruct((D_SGU, d), F32),
        jax.ShapeDtypeStruct((d, d), F32),
        jax.ShapeDtypeStruct((8, d), F32),
        jax.ShapeDtypeStruct((8, D_SGU), F32),
        jax.ShapeDtypeStruct((8, D_SGU), F32),
        jax.ShapeDtypeStruct((N_GROUPS, SGU_CHUNK, SGU_CHUNK), F32),
        jax.ShapeDtypeStruct((SGU_CHUNK, D_SGU), F32),
    ]
    out_specs = [whole((8, LANES)), tile(d), tile(D_SB), tile(QKV_COLS + n_rest), whole((D_SB, d)), whole((D_SGU, d)),
                 whole((d, d)), whole((8, d)), whole((8, D_SGU)), whole((8, D_SGU)),
                 whole((N_GROUPS, SGU_CHUNK, SGU_CHUNK)), whole((SGU_CHUNK, D_SGU))]
    in_specs = [tile(D_SB), tile(n_rest), tile(d), tile(d), whole((D_SB, d)), whole((D_SGU, d)), whole((d, d)),
                whole((1, d)), whole((1, D_SGU)), whole((1, D_SGU)), whole((N_GROUPS, SGU_CHUNK, SGU_CHUNK)),
                whole((SGU_CHUNK, D_SGU)), whole((D_SGU, D_SGU))]
    return pl.pallas_call(
        body, name=name, out_shape=out_shapes, grid=(nt,), in_specs=in_specs, out_specs=out_specs,
        compiler_params=_cparams(("arbitrary",)),
    )(o, rest, x, tgt, wua, wub, wout, fg, lng, lnb, wsp, bfull, gavg)


def _small_reduce(dfg8, dlng8, dlnb8, dbfull, name):
    d = dfg8.shape[1]

    def body(dfg_ref, dlng_ref, dlnb_ref, dbfull_ref, fg_out, lng_out, lnb_out, b_out):
        row = lax.broadcasted_iota(jnp.int32, (D_SGU, LANES), 0)
        col = lax.broadcasted_iota(jnp.int32, (D_SGU, LANES), 1)

        def select_sum(a, sel):
            hi, lo = _split_hi_lo(a)
            lo2 = (a - hi.astype(F32) - lo.astype(F32)).astype(BF16)
            return _dot(hi, sel) + _dot(lo, sel) + _dot(lo2, sel)

        def by_group(partials):
            v = jnp.sum(partials, axis=0, keepdims=True)
            g_of_lane = lax.broadcasted_iota(jnp.int32, (N_GROUPS, D_SGU), 1) // GROUP_DIM
            g_of_row = lax.broadcasted_iota(jnp.int32, (N_GROUPS, D_SGU), 0)
            spread = jnp.where(g_of_lane == g_of_row, jnp.broadcast_to(v, (N_GROUPS, D_SGU)), 0.0)
            return select_sum(spread, (row % GROUP_DIM == col).astype(BF16))[:, 0:GROUP_DIM]

        fg_out[...] = jnp.sum(dfg_ref[...], axis=0, keepdims=True)
        lng_out[...] = by_group(dlng_ref[...])
        lnb_out[...] = by_group(dlnb_ref[...])
        by_pos = select_sum(dbfull_ref[...], (row // GROUP_DIM == col).astype(BF16))
        b_out[...] = by_pos.T[0:N_GROUPS, :]

    return pl.pallas_call(
        body, name=name,
        out_shape=[jax.ShapeDtypeStruct((1, d), F32), jax.ShapeDtypeStruct((N_GROUPS, GROUP_DIM), F32),
                   jax.ShapeDtypeStruct((N_GROUPS, GROUP_DIM), F32), jax.ShapeDtypeStruct((N_GROUPS, SGU_CHUNK), F32)],
        compiler_params=_cparams(),
    )(dfg8, dlng8, dlnb8, dbfull)


def _block_major_cols(w):
    r, n = w.shape
    return jnp.transpose(w.reshape(r, N_DEV, n // N_DEV), (1, 0, 2))


def _from_block_major_cols(w):
    nb, r, c = w.shape
    return jnp.transpose(w, (1, 0, 2)).reshape(r, nb * c)


def kernel(x, norm_g, w_in, sgu_ln_g, sgu_ln_b, w_spatial, b_spatial, w_up_a, w_up_b, w_out, final_norm_g, loss_target, m_norm_g, m_w_in, m_sgu_ln_g, m_sgu_ln_b, m_w_spatial, m_b_spatial, m_w_up_a, m_w_up_b, m_w_out, m_final_norm_g, v_norm_g, v_w_in, v_sgu_ln_g, v_sgu_ln_b, v_w_spatial, v_b_spatial, v_w_up_a, v_w_up_b, v_w_out, v_final_norm_g):
    s, d = x.shape[1], x.shape[2]
    xs = x[0]
    tgt = loss_target[0]
    cx, cy, cc = _coords()
    core = jnp.reshape(cc, (1,)).astype(jnp.int32)
    chip = jnp.reshape(2 * cx + cy, (1,)).astype(jnp.int32)

    h, ht, (g_win,) = _rmsnorm_fwd(xs, norm_g, [jnp.transpose(w_in[0]).astype(BF16)], "norm")
    d_in = N_DEV * w_in.shape[2]
    w_full = g_win.reshape(d_in, d)
    qkv, _ = _inproj(h, w_full, 0, QKV_COLS, BF16, [], "inproj_qkv")
    rest, _ = _inproj(h, w_full, QKV_COLS, d_in - QKV_COLS, F32, [], "inproj_rest")
    o, rs, extra, (g_wua, g_wub, g_wout) = _attn_fwd(
        qkv, [w_up_a[0].astype(BF16), w_up_b[0].astype(BF16), w_out[0].astype(BF16)], "attn_fwd")
    wua_full = _from_block_major_cols(g_wua)
    wub_full = _from_block_major_cols(g_wub)
    wout_full = g_wout.reshape(d, d)

    lng = sgu_ln_g.reshape(1, D_SGU)
    lnb = sgu_ln_b.reshape(1, D_SGU)
    bfull = jnp.repeat(jnp.transpose(b_spatial[0]), GROUP_DIM, axis=1)
    grp = jnp.arange(D_SGU) // GROUP_DIM
    gavg = jnp.where(grp[:, None] == grp[None, :], 1.0 / GROUP_DIM, 0.0).astype(BF16)
    (loss_b, dx2, do, dproj, dwua, dwub, dwout, dfg8, dlng8, dlnb8, dwsp, dbfull) = _mid(
        o, rest, xs, tgt, wua_full, wub_full, wout_full, final_norm_g.reshape(1, d), lng, lnb, w_spatial[0], bfull,
        gavg, "mid")

    def blocks42(a):
        return a.reshape((4, 2) + a.shape[1:])

    cps_up = _sibling_sum([blocks42(_block_major_cols(dwua.astype(BF16))),
                           blocks42(_block_major_cols(dwub.astype(BF16))),
                           blocks42(dwout.astype(BF16).reshape(N_DEV, d // N_DEV, d))], core, "rs_sibling_up")

    dfg, dlng, dlnb, db = _small_reduce(dfg8, dlng8, dlnb8, dbfull, "small_reduce")
    wsp_rows = N_GROUPS * SGU_CHUNK
    dproj, small_parts, land_up = _attn_bwd(
        qkv, do, rs, extra, dproj, [dfg, dlng, dlnb, db, dwsp.reshape(wsp_rows, SGU_CHUNK)], cps_up, "attn_bwd")
    dwin = _dw_in(ht, dproj, "dwin")

    cps_in = _sibling_sum([dwin.reshape(4, 2, d_in // N_DEV, d)], core, "rs_sibling_in")
    grad_x, dng8, land_in = _dh_and_grad_x(dproj, w_full, xs, norm_g, dx2, cps_in, "dh")

    ng_parts, loss_parts = _allgather([dng8, loss_b], "ag_tail")

    def small_layouts(ng, fg, lg, lb, bs, ws):
        return [ng.reshape(1, d), fg.reshape(1, d), lg.reshape(N_GROUPS, GROUP_DIM), lb.reshape(N_GROUPS, GROUP_DIM),
                bs.reshape(N_GROUPS, SGU_CHUNK), ws.reshape(wsp_rows, SGU_CHUNK)]

    sm, loss_sum = _adam_small(
        [ng_parts] + list(small_parts),
        small_layouts(norm_g, final_norm_g, sgu_ln_g, sgu_ln_b, b_spatial, w_spatial),
        small_layouts(m_norm_g, m_final_norm_g, m_sgu_ln_g, m_sgu_ln_b, m_b_spatial, m_w_spatial),
        small_layouts(v_norm_g, v_final_norm_g, v_sgu_ln_g, v_sgu_ln_b, v_b_spatial, v_w_spatial),
        loss_parts, "adam_small")
    small_shapes = [norm_g.shape, final_norm_g.shape, sgu_ln_g.shape, sgu_ln_b.shape, b_spatial.shape, w_spatial.shape]
    sm = [[a.reshape(shp) for a, shp in zip(kind, small_shapes)] for kind in sm]

    res = _adam_shard(cps_in, list(land_in), chip, *[jnp.transpose(a[0]) for a in (w_in, m_w_in, v_w_in)], "adam0")
    big = [[jnp.transpose(r)[None] for r in res]]
    for i, (w, m, v) in enumerate([(w_up_a, m_w_up_a, v_w_up_a), (w_up_b, m_w_up_b, v_w_up_b),
                                   (w_out, m_w_out, v_w_out)]):
        res = _adam_shard([cps_up[i]], [land_up[i]], chip, w[0], m[0], v[0], "adam%d" % (i + 1))
        big.append([r[None] for r in res])

    loss = loss_sum[0, 0]

    def per_kind(kd):
        return [sm[kd][0], big[0][kd], sm[kd][2], sm[kd][3], sm[kd][5], sm[kd][4], big[1][kd], big[2][kd], big[3][kd],
                sm[kd][1]]

    return (loss, grad_x[None], *per_kind(0), *per_kind(1), *per_kind(2), *per_kind(3))
```

```python
import functools
import math

import jax
import jax.numpy as jnp
from jax import lax
from jax.experimental import pallas as pl
from jax.experimental.pallas import tpu as pltpu

F32 = jnp.float32
BF16 = jnp.bfloat16
MESH = pl.DeviceIdType.MESH

N_DEV = 8
N_HEADS = 8
HEAD_DIM = 64
D_SB = N_HEADS * HEAD_DIM
N_GROUPS = 8
GROUP_DIM = 64
D_SGU = N_GROUPS * GROUP_DIM
SGU_CHUNK = 128
CHUNK = 64
EPS = 1e-6
LANES = 128
N_PAIRS = N_HEADS // 2
QKV_COLS = 3 * D_SB
ATT_BLOCK = 256
ATT_TILE = 128
CARRY_FLOOR = -90.0
R_UNREACHED = -1e30

ADAM_LR = 0.001
ADAM_B1 = 0.9
ADAM_B2 = 0.999
ADAM_EPS = 1e-08
ADAM_WD = 0.01
ADAM_STEP = 10

VMEM_LIMIT = 56 * 1024 * 1024


def _cparams(sem=None, vmem=VMEM_LIMIT):
    return pltpu.CompilerParams(dimension_semantics=sem, vmem_limit_bytes=vmem)


def _dot(a, b):
    return jnp.dot(a, b, preferred_element_type=F32)


def _dot_nt(a, b):
    return lax.dot_general(a, b, (((1,), (1,)), ((), ())), preferred_element_type=F32)


def _dot_tn(a, b):
    return lax.dot_general(a, b, (((0,), (0,)), ((), ())), preferred_element_type=F32)


def _split_hi_lo(a):
    hi = a.astype(BF16)
    lo = (a - hi.astype(F32)).astype(BF16)
    return hi, lo


def _sigmoid(x):
    return 1.0 / (1.0 + jnp.exp(-x))


_GELU_C = math.sqrt(2.0 / math.pi)


def _gelu_and_grad(x):
    x2 = x * x
    inner = _GELU_C * (x + 0.044715 * (x2 * x))
    t = jnp.tanh(inner)
    cdf = 0.5 * (1.0 + t)
    g = x * cdf
    dg = cdf + x * (0.5 * (1.0 - t * t)) * (_GELU_C * (1.0 + 3.0 * 0.044715 * x2))
    return g, dg


def _coords():
    return lax.axis_index("x"), lax.axis_index("y"), lax.axis_index("c")


def _dev_index(px, py, pc):
    return 4 * px + 2 * py + pc


def _allgather(blocks, name):
    n = len(blocks)

    def body(*refs):
        gather = _Gather(refs[:n], refs[n:2 * n], *refs[2 * n:])
        gather.issue()
        gather.finish()

    any_spec = pl.BlockSpec(memory_space=pl.ANY)
    return pl.pallas_call(
        body, name=name,
        out_shape=_gather_out_shapes(blocks),
        in_specs=[any_spec] * n, out_specs=[any_spec] * n,
        scratch_shapes=_gather_semaphores(n),
    )(*blocks)


def _gather_out_shapes(blocks):
    return [jax.ShapeDtypeStruct((N_DEV,) + b.shape, b.dtype) for b in blocks]


def _gather_semaphores(n):
    return [pltpu.SemaphoreType.DMA((n, 7)), pltpu.SemaphoreType.DMA((n, 7)), pltpu.SemaphoreType.DMA((n,))]


class _Gather:
    def __init__(self, ins, outs, send_sems, recv_sems, local_sems, relay=False):
        self.ins, self.outs = ins, outs
        self.send_sems, self.recv_sems, self.local_sems = send_sems, recv_sems, local_sems
        self.n = len(ins)
        self.relay = relay
        x, y, c = _coords()
        self.c = c
        self.me, self.sibling = (x, y, c), (x, y, 1 - c)
        self.chips = [(1 - x, y), (x, 1 - y), (1 - x, 1 - y)]

    def _copy(self, a, k, block, to, src=None):
        dst = self.outs[a].at[_dev_index(*block)]
        return pltpu.make_async_remote_copy(
            src_ref=dst if src is None else src, dst_ref=dst,
            send_sem=self.send_sems.at[a, k], recv_sem=self.recv_sems.at[a, k],
            device_id=to, device_id_type=MESH)

    def _mine(self):
        return [pltpu.make_async_copy(self.ins[a], self.outs[a].at[_dev_index(*self.me)], self.local_sems.at[a])
                for a in range(self.n)]

    def _first(self):
        first = []
        direct = self.chips[:2] if self.relay else self.chips
        for a in range(self.n):
            first.append(self._copy(a, 0, self.me, self.sibling, src=self.ins[a]))
            first += [self._copy(a, 1 + j, self.me, (*chip, self.c), src=self.ins[a])
                      for j, chip in enumerate(direct)]
        return first

    def issue(self):
        for cp in self._mine() + self._first():
            cp.start()

    def _pass_on(self, a, j):
        chip = self.chips[j]
        self._copy(a, 1 + j, (*chip, self.c), self.me).wait_recv()
        self._copy(a, 4 + j, (*chip, self.c), self.sibling).start()

    def finish(self):
        c = self.c
        if self.relay:
            for core in range(2):
                @pl.when(c == core)
                def _(core=core):
                    j_src, j_dst = core, 1 - core
                    for a in range(self.n):
                        self._pass_on(a, j_src)
                        self._copy(a, 3, (*self.chips[j_src], c), (*self.chips[j_dst], c)).start()
                    for a in range(self.n):
                        self._pass_on(a, j_dst)
                    for a in range(self.n):
                        self._pass_on(a, 2)
        else:
            for j in range(3):
                for a in range(self.n):
                    self._pass_on(a, j)
        for a in range(self.n):
            self._copy(a, 0, self.sibling, self.me).wait_recv()
            for j, chip in enumerate(self.chips):
                self._copy(a, 4 + j, (*chip, 1 - c), self.me).wait_recv()
        for a in range(self.n):
            for k in range(7):
                self._copy(a, k, self.me, self.sibling).wait_send()
        for cp in self._mine():
            cp.wait()


def _push_sibling(arrs, name):
    n = len(arrs)

    def body(*refs):
        ins, outs = refs[:n], refs[n:2 * n]
        send_sems, recv_sems = refs[2 * n:]
        x, y, c = _coords()
        sibling = (x, y, 1 - c)
        copies = []
        for a in range(n):
            for k in range(4):
                copies.append(pltpu.make_async_remote_copy(
                    src_ref=ins[a].at[k, 1 - c], dst_ref=outs[a].at[k],
                    send_sem=send_sems.at[a, k], recv_sem=recv_sems.at[a, k],
                    device_id=sibling, device_id_type=MESH))
        for cp in copies:
            cp.start()
        for cp in copies:
            cp.wait()

    any_spec = pl.BlockSpec(memory_space=pl.ANY)
    return pl.pallas_call(
        body, name=name,
        out_shape=[jax.ShapeDtypeStruct((4,) + a.shape[2:], a.dtype) for a in arrs],
        in_specs=[any_spec] * n, out_specs=[any_spec] * n,
        scratch_shapes=[pltpu.SemaphoreType.DMA((n, 4)), pltpu.SemaphoreType.DMA((n, 4))],
    )(*arrs)


def _chip_push_copies(ins, outs, send_sems, recv_sems):
    x, y, c = _coords()
    chips = [(1 - x, y), (x, 1 - y), (1 - x, 1 - y)]
    return [pltpu.make_async_remote_copy(
        src_ref=ins[a].at[2 * px + py], dst_ref=outs[a].at[r],
        send_sem=send_sems.at[a, r], recv_sem=recv_sems.at[a, r],
        device_id=(px, py, c), device_id_type=MESH)
        for a in range(len(ins)) for r, (px, py) in enumerate(chips)]


def _chip_partial_sum(own, land, core, name):
    _, _, rows, cols = own.shape
    tr = rows

    def body(core_ref, own_ref, land_ref, out_ref):
        del core_ref
        out_ref[...] = (own_ref[...].astype(F32) + land_ref[...].astype(F32)).astype(out_ref.dtype)

    return pl.pallas_call(
        body, name=name,
        out_shape=jax.ShapeDtypeStruct((4, rows, cols), own.dtype),
        grid_spec=pltpu.PrefetchScalarGridSpec(
            num_scalar_prefetch=1, grid=(4, rows // tr),
            in_specs=[pl.BlockSpec((None, None, tr, cols), lambda k, r, core: (k, core[0], r, 0)),
                      pl.BlockSpec((None, tr, cols), lambda k, r, core: (k, r, 0))],
            out_specs=pl.BlockSpec((None, tr, cols), lambda k, r, core: (k, r, 0))),
        compiler_params=_cparams(("parallel", "parallel")),
    )(core, own, land)


def _adamw_math(w, g, m, v):
    m = ADAM_B1 * m + (1.0 - ADAM_B1) * g
    v = ADAM_B2 * v + (1.0 - ADAM_B2) * (g * g)
    m_hat = m / (1.0 - ADAM_B1 ** ADAM_STEP)
    v_hat = v / (1.0 - ADAM_B2 ** ADAM_STEP)
    delta = -ADAM_LR * (m_hat / (jnp.sqrt(v_hat) + ADAM_EPS) + ADAM_WD * w)
    return delta, m, v


def _adam_shard(cps, lands, chip, w, m, v, name):
    rows, cols = w.shape
    tr = rows // 4
    nparts = len(cps)

    def body(chip_ref, *refs):
        del chip_ref
        cp_refs, land_refs = refs[:nparts], refs[nparts:2 * nparts]
        w_ref, m_ref, v_ref, g_out, d_out, m_out, v_out = refs[2 * nparts:]
        parts = []
        for cp_ref, land_ref in zip(cp_refs, land_refs):
            g_k = cp_ref[...].astype(F32)
            for r in range(3):
                g_k = g_k + land_ref[r].astype(F32)
            parts.append(g_k)
        g = parts[0] if nparts == 1 else jnp.concatenate(parts, axis=1)
        delta, m_new, v_new = _adamw_math(w_ref[...], g, m_ref[...], v_ref[...])
        g_out[...] = g
        d_out[...] = delta
        m_out[...] = m_new
        v_out[...] = v_new

    tile = pl.BlockSpec((tr, cols), lambda r, chip: (r, 0))
    out = jax.ShapeDtypeStruct((rows, cols), F32)
    return pl.pallas_call(
        body, name=name, out_shape=[out] * 4,
        grid_spec=pltpu.PrefetchScalarGridSpec(
            num_scalar_prefetch=1, grid=(rows // tr,),
            in_specs=[pl.BlockSpec((None, tr, a.shape[2]), lambda r, chip: (chip[0], r, 0)) for a in cps]
            + [pl.BlockSpec((3, tr, a.shape[2]), lambda r, chip: (0, r, 0)) for a in lands]
            + [tile, tile, tile],
            out_specs=[tile] * 4),
        compiler_params=_cparams(("parallel",)),
    )(chip, *cps, *lands, w, m, v)


def _adam_shards_whole(cps, lands, chip, ws, ms, vs, name):
    n = len(ws)

    def body(chip_ref, *refs):
        cp_refs, land_refs = refs[:n], refs[n:2 * n]
        w_refs, m_refs, v_refs = refs[2 * n:3 * n], refs[3 * n:4 * n], refs[4 * n:5 * n]
        outs = refs[5 * n:]
        for i in range(n):
            g = cp_refs[i][chip_ref[0]].astype(F32)
            for r in range(3):
                g = g + land_refs[i][r].astype(F32)
            delta, m_new, v_new = _adamw_math(w_refs[i][...], g, m_refs[i][...], v_refs[i][...])
            outs[i][...] = g
            outs[n + i][...] = delta
            outs[2 * n + i][...] = m_new
            outs[3 * n + i][...] = v_new

    vmem = pl.BlockSpec(memory_space=pltpu.VMEM)
    res = pl.pallas_call(
        body, name=name, out_shape=[jax.ShapeDtypeStruct(w.shape, F32) for w in ws] * 4,
        in_specs=[pl.BlockSpec(memory_space=pltpu.SMEM)] + [vmem] * (5 * n), out_specs=[vmem] * (4 * n),
        compiler_params=_cparams(),
    )(chip, *cps, *lands, *ws, *ms, *vs)
    return [res[k * n:(k + 1) * n] for k in range(4)]


def _adam_small(parts, ws, ms, vs, loss_parts, name):
    n = len(ws)

    def body(*refs):
        p_refs, w_refs, m_refs, v_refs = refs[:n], refs[n:2 * n], refs[2 * n:3 * n], refs[3 * n:4 * n]
        loss_ref, outs, loss_out = refs[4 * n], refs[4 * n + 1:-1], refs[-1]
        total = loss_ref[0]
        for dev in range(1, N_DEV):
            total = total + loss_ref[dev]
        loss_out[...] = total
        for i in range(n):
            g = p_refs[i][0]
            for dev in range(1, N_DEV):
                g = g + p_refs[i][dev]
            if g.shape[0] != w_refs[i].shape[0]:
                g = jnp.sum(g, axis=0, keepdims=True)
            delta, m_new, v_new = _adamw_math(w_refs[i][...], g, m_refs[i][...], v_refs[i][...])
            outs[i][...] = g
            outs[n + i][...] = delta
            outs[2 * n + i][...] = m_new
            outs[3 * n + i][...] = v_new

    out_shapes = [jax.ShapeDtypeStruct(w.shape, F32) for w in ws] * 4
    out_shapes.append(jax.ShapeDtypeStruct(loss_parts.shape[1:], F32))
    res = pl.pallas_call(body, name=name, out_shape=out_shapes, compiler_params=_cparams())(
        *parts, *ws, *ms, *vs, loss_parts)
    return [res[k * n:(k + 1) * n] for k in range(4)], res[-1]


def _rmsnorm_fwd(x, g, riders, name):
    s, d = x.shape
    ts = 512
    nt = s // ts
    nr = len(riders)

    def body(x_ref, g_ref, *rest):
        rider_in, (h_ref, ht_ref) = rest[:nr], rest[nr:nr + 2]
        rider_out, sems = rest[nr + 2:2 * nr + 2], rest[2 * nr + 2:]
        i = pl.program_id(0)
        gather = _Gather(rider_in, rider_out, *sems, relay=True)

        @pl.when(i == 0)
        def _():
            gather.issue()

        xv = x_ref[...]
        rstd = lax.rsqrt(jnp.mean(xv * xv, axis=-1, keepdims=True) + EPS)
        h = xv * rstd * g_ref[...]
        h_ref[...] = h.astype(BF16)
        ht_ref[...] = h.T.astype(BF16)

        @pl.when(i == nt - 1)
        def _():
            gather.finish()

    any_spec = pl.BlockSpec(memory_space=pl.ANY)
    res = pl.pallas_call(
        body, name=name,
        out_shape=[jax.ShapeDtypeStruct((s, d), BF16), jax.ShapeDtypeStruct((d, s), BF16)]
        + _gather_out_shapes(riders),
        grid=(nt,),
        in_specs=[pl.BlockSpec((ts, d), lambda i: (i, 0)), pl.BlockSpec((1, d), lambda i: (0, 0))] + [any_spec] * nr,
        out_specs=[pl.BlockSpec((ts, d), lambda i: (i, 0)), pl.BlockSpec((d, ts), lambda i: (0, i))]
        + [any_spec] * nr,
        scratch_shapes=_gather_semaphores(nr),
        compiler_params=_cparams(("arbitrary",)),
    )(x, g, *riders)
    return res[0], res[1], res[2:]


def _inproj(h, wt, name):
    s, d = h.shape
    n = wt.shape[0]
    tn = 256
    n_qkv = QKV_COLS // tn

    def body(h_ref, w_ref, qkv_ref, rest_ref):
        j = pl.program_id(0)
        res = _dot_nt(h_ref[...], w_ref[...])

        @pl.when(j < n_qkv)
        def _():
            qkv_ref[...] = res.astype(BF16)

        @pl.when(j >= n_qkv)
        def _():
            rest_ref[...] = res

    return pl.pallas_call(
        body, name=name,
        out_shape=[jax.ShapeDtypeStruct((s, QKV_COLS), BF16), jax.ShapeDtypeStruct((s, n - QKV_COLS), F32)],
        grid=(n // tn,),
        in_specs=[pl.BlockSpec((s, d), lambda j: (0, 0)), pl.BlockSpec((tn, d), lambda j: (j, 0))],
        out_specs=[pl.BlockSpec((s, tn), lambda j: (0, jnp.minimum(j, n_qkv - 1))),
                   pl.BlockSpec((s, tn), lambda j: (0, jnp.maximum(j - n_qkv, 0)))],
        compiler_params=_cparams(("arbitrary",)),
    )(h, wt)


def _dw_in(ht, dproj, name):
    d, s = ht.shape
    n = dproj.shape[1]
    tn = 512

    def body(a_ref, b_ref, o_ref):
        o_ref[...] = _dot(a_ref[...], b_ref[...]).T.astype(o_ref.dtype)

    return pl.pallas_call(
        body, name=name, out_shape=jax.ShapeDtypeStruct((n, d), BF16), grid=(n // tn,),
        in_specs=[pl.BlockSpec((d, s), lambda j: (0, 0)), pl.BlockSpec((s, tn), lambda j: (0, j))],
        out_specs=pl.BlockSpec((tn, d), lambda j: (j, 0)),
        compiler_params=_cparams(("parallel",)),
    )(ht, dproj)


def _dh_and_grad_x(dproj, wt, x, g, dx2, cps, name):
    s, n = dproj.shape
    d = wt.shape[1]
    tm, tk = min(s, 1024), 512
    nk = n // tk
    nm = s // tm
    nc = len(cps)

    def body(a_ref, w_ref, x_ref, g_ref, dx2_ref, *rest):
        cp_refs, gx_ref, dg_ref = rest[:nc], rest[nc], rest[nc + 1]
        land_refs = rest[nc + 2:2 * nc + 2]
        acc_ref, send_sems, recv_sems = rest[2 * nc + 2:]
        i, k = pl.program_id(0), pl.program_id(1)

        @pl.when((i == 0) & (k == 0))
        def _():
            for cp in _chip_push_copies(cp_refs, land_refs, send_sems, recv_sems):
                cp.start()
            dg_ref[...] = jnp.zeros_like(dg_ref)

        @pl.when(k == 0)
        def _():
            acc_ref[...] = jnp.zeros_like(acc_ref)

        acc_ref[...] += _dot(a_ref[...], w_ref[...])

        @pl.when(k == nk - 1)
        def _():
            dh = acc_ref[...]
            xv = x_ref[...]
            rstd = lax.rsqrt(jnp.mean(xv * xv, axis=-1, keepdims=True) + EPS)
            xhat = xv * rstd
            dg_ref[...] += jnp.sum((dh * xhat).reshape(tm // 8, 8, d), axis=0)
            dxh = dh * g_ref[...]
            gx_ref[...] = dx2_ref[...] + rstd * (dxh - xhat * jnp.mean(dxh * xhat, axis=-1, keepdims=True))

        @pl.when((i == nm - 1) & (k == nk - 1))
        def _():
            for cp in _chip_push_copies(cp_refs, land_refs, send_sems, recv_sems):
                cp.wait()

    any_spec = pl.BlockSpec(memory_space=pl.ANY)
    res = pl.pallas_call(
        body, name=name,
        out_shape=[jax.ShapeDtypeStruct((s, d), F32), jax.ShapeDtypeStruct((8, d), F32)]
        + [jax.ShapeDtypeStruct((3,) + a.shape[1:], a.dtype) for a in cps],
        grid=(nm, nk),
        in_specs=[pl.BlockSpec((tm, tk), lambda i, k: (i, k)), pl.BlockSpec((tk, d), lambda i, k: (k, 0)),
                  pl.BlockSpec((tm, d), lambda i, k: (i, 0)), pl.BlockSpec((1, d), lambda i, k: (0, 0)),
                  pl.BlockSpec((tm, d), lambda i, k: (i, 0))] + [any_spec] * nc,
        out_specs=[pl.BlockSpec((tm, d), lambda i, k: (i, 0)), pl.BlockSpec((8, d), lambda i, k: (0, 0))]
        + [any_spec] * nc,
        scratch_shapes=[pltpu.VMEM((tm, d), F32), pltpu.SemaphoreType.DMA((nc, 3)), pltpu.SemaphoreType.DMA((nc, 3))],
        compiler_params=_cparams(("arbitrary", "arbitrary")),
    )(dproj, wt, x, g, dx2, *cps)
    return res[0], res[1], res[2:]


def _log_sigmoids(z):
    l1p = jnp.log(1.0 + jnp.exp(-jnp.abs(z)))
    ls = jnp.minimum(z, 0.0) - l1p
    return ls, ls - z


def _strict_lower_ones(n):
    row = lax.broadcasted_iota(jnp.int32, (n, n), 0)
    col = lax.broadcasted_iota(jnp.int32, (n, n), 1)
    return row, col, (row > col).astype(BF16)


def _attn_fwd(qkv, riders, name):
    s = qkv.shape[0]
    tb, tt = ATT_BLOCK, ATT_TILE
    nq = s // tb
    per_block = tb // tt
    assert per_block == 2 and s // tt <= LANES, "two query tiles per grid step; one lane of saved carry per key tile"

    nr = len(riders)

    def sweep_body(gather, q_ref, k_ref, v_ref, o_ref, rs_ref, extra_ref, acc_ref, r_ref, rsv_ref, rmax_ref):
        i = pl.program_id(1)
        lane = lax.broadcasted_iota(jnp.int32, (tt, LANES), 1)
        hmask = [lane < HEAD_DIM, lane >= HEAD_DIM]
        row, col, tri = _strict_lower_ones(tt)
        tri2 = jnp.concatenate([tri, tri], axis=0)
        below = col < row
        qrows = [slice(u * tt, (u + 1) * tt) for u in range(per_block)]
        qm = [[jnp.where(m, q_ref[qrows[u], :], jnp.zeros((), BF16)) * jnp.asarray(HEAD_DIM ** -0.5, BF16)
               for m in hmask] for u in range(per_block)]
        acc_ref[...] = jnp.zeros_like(acc_ref)
        r_ref[...] = jnp.zeros_like(r_ref)
        rsv_ref[...] = jnp.full_like(rsv_ref, R_UNREACHED)

        def sweep(tiles, chains):
            heads, nch = range(2), range(len(chains))
            kv = []
            for t in tiles:
                rows = pl.ds(pl.multiple_of(t * tt, tt), tt)
                kv.append((k_ref[rows, :], v_ref[rows, :]))
            z = [[_dot_nt(qm[u][h], kv[ti][0]) for h in heads] for u, ti, _ in chains]
            ls, lk = [], []
            for n in nch:
                pairs = [_log_sigmoids(z[n][h]) for h in heads]
                ls.append([p[0] for p in pairs])
                lk.append([jnp.where(below, p[1], 0.0) if chains[n][2] else p[1] for p in pairs])
            cur = {u: [r_ref[h, qrows[u], :] for h in heads] for u in sorted({c[0] for c in chains})}
            r = []
            for n, (u, _, _) in enumerate(chains):
                r.append(cur[u])
                cur[u] = [cur[u][h] + jnp.sum(lk[n][h], axis=1, keepdims=True) for h in heads]
            for u in cur:
                rmax_ref[u] = jnp.max(jnp.maximum(cur[u][0], cur[u][1]))
            suffix = [[_dot(jnp.concatenate(_split_hi_lo(lk[n][h]), axis=1), tri2) for h in heads] for n in nch]
            w = []
            for n in nch:
                w_n = [jnp.exp(ls[n][h] + suffix[n][h] + r[n][h]) for h in heads]
                if chains[n][2]:
                    w_n = [jnp.where(below, w_h, 0.0) for w_h in w_n]
                w.append([w_h.astype(BF16) for w_h in w_n])
            pv = [[_dot(w[n][h], kv[chains[n][1]][1]) for h in heads] for n in nch]
            for u in cur:
                mine = [n for n in nch if chains[n][0] == u]
                for h in heads:
                    acc_ref[h, qrows[u], :] += functools.reduce(lambda a, b: a + b, [pv[n][h] for n in mine])
                    for n in mine:
                        if not chains[n][2]:
                            t = tiles[chains[n][1]]
                            rsv_ref[h, qrows[u], :] = jnp.where(lane == t, r[n][h], rsv_ref[h, qrows[u], :])
                    r_ref[h, qrows[u], :] = cur[u][h]

        first = per_block * i

        @pl.when(i == 0)
        def _():
            sweep([0, 1], [(0, 0, True), (1, 1, True), (1, 0, False)])

        @pl.when(i > 0)
        def _():
            sweep([first - 1, first, first + 1], [(0, 1, True), (1, 2, True), (0, 0, False), (1, 1, False)])

        for u in range(per_block):
            n_left = first + u - 1

            def live(c, n_left=n_left):
                jj, rmax = c
                return (jj < n_left) & (rmax >= CARRY_FLOOR)

            def step(c, u=u, n_left=n_left):
                jj, _ = c
                sweep([n_left - 1 - jj], [(u, 0, False)])
                return jj + 1, rmax_ref[u]

            swept, _ = lax.while_loop(live, step, (jnp.int32(0), rmax_ref[u]))
            extra_ref[pl.program_id(0), first + u] = swept.astype(F32)

        lane_b = lax.broadcasted_iota(jnp.int32, (tb, LANES), 1)
        o_ref[...] = jnp.where(lane_b < HEAD_DIM, acc_ref[0], acc_ref[1])
        rs_ref[...] = rsv_ref[...]

        @pl.when((pl.program_id(0) == N_PAIRS - 1) & (i == nq - 1))
        def _():
            gather.finish()

    def body(q_ref, k_ref, v_ref, *rest):
        rider_in, (o_ref, rs_ref, extra_ref) = rest[:nr], rest[nr:nr + 3]
        rider_out = rest[nr + 3:2 * nr + 3]
        acc_ref, r_ref, rsv_ref, rmax_ref = rest[2 * nr + 3:2 * nr + 7]
        gather = _Gather(rider_in, rider_out, *rest[2 * nr + 7:])

        @pl.when((pl.program_id(0) == 0) & (pl.program_id(1) == 0))
        def _():
            gather.issue()

        sweep_body(gather, q_ref, k_ref, v_ref, o_ref, rs_ref, extra_ref, acc_ref, r_ref, rsv_ref, rmax_ref)

    any_spec = pl.BlockSpec(memory_space=pl.ANY)
    res = pl.pallas_call(
        body, name=name,
        out_shape=[jax.ShapeDtypeStruct((s, D_SB), F32), jax.ShapeDtypeStruct((N_HEADS, s, LANES), F32),
                   jax.ShapeDtypeStruct((N_PAIRS, s // tt), F32)] + _gather_out_shapes(riders),
        grid=(N_PAIRS, nq),
        in_specs=[pl.BlockSpec((tb, LANES), lambda p, i: (i, p)),
                  pl.BlockSpec((s, LANES), lambda p, i: (0, N_PAIRS + p)),
                  pl.BlockSpec((s, LANES), lambda p, i: (0, 2 * N_PAIRS + p))] + [any_spec] * nr,
        out_specs=[pl.BlockSpec((tb, LANES), lambda p, i: (i, p)),
                   pl.BlockSpec((2, tb, LANES), lambda p, i: (p, i, 0)),
                   pl.BlockSpec(memory_space=pltpu.SMEM)] + [any_spec] * nr,
        scratch_shapes=[pltpu.VMEM((2, tb, LANES), F32), pltpu.VMEM((2, tb, LANES), F32),
                        pltpu.VMEM((2, tb, LANES), F32), pltpu.SMEM((per_block,), F32)] + _gather_semaphores(nr),
        compiler_params=_cparams(("arbitrary", "arbitrary")),
    )(qkv, qkv, qkv, *riders)
    return res[0], res[1], res[2], res[3:]


def _attn_bwd(qkv, do, rs, extra, dproj, smalls, cps, name):
    s = qkv.shape[0]
    tb, tt = ATT_BLOCK, ATT_TILE
    nq = s // tb
    per_block = tb // tt
    scale = HEAD_DIM ** -0.5

    ns, nc = len(smalls), len(cps)

    def sweep_body(q_ref, k_ref, v_ref, do_ref, rs_ref, extra_ref, dproj_hbm, out_hbm, dq_acc, dk_acc, dv_acc,
                   dqi_ref, pc_ref, stage_ref, out_sems):
        del dproj_hbm
        pair = pl.program_id(0)
        lane = lax.broadcasted_iota(jnp.int32, (tt, LANES), 1)
        hmask = [lane < HEAD_DIM, lane >= HEAD_DIM]
        row, col, tri = _strict_lower_ones(tt)
        tri2 = jnp.concatenate([tri, tri], axis=0)
        triu = (row < col).astype(BF16)
        below = col < row
        zero = jnp.zeros((), BF16)
        qrows = [slice(u * tt, (u + 1) * tt) for u in range(per_block)]
        dq_acc[...] = jnp.zeros_like(dq_acc)
        dk_acc[...] = jnp.zeros_like(dk_acc)
        dv_acc[...] = jnp.zeros_like(dv_acc)

        def qblock(i, carry):
            block_rows = pl.ds(pl.multiple_of(i * tb, tb), tb)
            q2 = q_ref[block_rows, :]
            do2 = do_ref[block_rows, :]
            qm = [[jnp.where(m, q2[qrows[u]], zero) * jnp.asarray(scale, BF16) for m in hmask]
                  for u in range(per_block)]
            dom = [[jnp.where(m, do2[qrows[u]], zero) for m in hmask] for u in range(per_block)]
            rs_i = [[rs_ref[h, pl.ds(pl.multiple_of(i * tb + u * tt, tt), tt), :] for h in range(2)]
                    for u in range(per_block)]
            dqi_ref[...] = jnp.zeros_like(dqi_ref)
            pc_ref[...] = jnp.zeros_like(pc_ref)
            first = per_block * i

            def sweep(tiles, chains):
                heads, nch = range(2), range(len(chains))
                krows = [pl.ds(pl.multiple_of(t * tt, tt), tt) for t in tiles]
                k2 = [k_ref[rows, :] for rows in krows]
                v2 = [v_ref[rows, :] for rows in krows]
                z = [[_dot_nt(qm[u][h], k2[ti]) for h in heads] for u, ti, _ in chains]
                dw = [[_dot_nt(dom[u][h], v2[ti]) for h in heads] for u, ti, _ in chains]
                ls, lk = [], []
                for n in nch:
                    pairs = [_log_sigmoids(z[n][h]) for h in heads]
                    ls.append([p[0] for p in pairs])
                    lk.append([jnp.where(below, p[1], 0.0) if chains[n][2] else p[1] for p in pairs])
                suffix = [[_dot(jnp.concatenate(_split_hi_lo(lk[n][h]), axis=1), tri2) for h in heads] for n in nch]
                w, g = [], []
                for n, (u, ti, diag) in enumerate(chains):
                    w_n = []
                    for h in heads:
                        logw = ls[n][h] + suffix[n][h]
                        if diag:
                            w_n.append(jnp.where(below, jnp.exp(logw), 0.0))
                        else:
                            carry_in = jnp.sum(jnp.where(lane == tiles[ti], rs_i[u][h], 0.0), axis=1, keepdims=True)
                            w_n.append(jnp.exp(logw + carry_in))
                    w.append([w_h.astype(BF16) for w_h in w_n])
                    g.append([w_n[h] * dw[n][h] for h in heads])
                prefix = [[_dot(g[n][h].astype(BF16), triu) for h in heads] for n in nch]
                cur = {u: [pc_ref[h, qrows[u], :] for h in heads] for u in sorted({c[0] for c in chains})}
                pc = []
                for n, (u, _, _) in enumerate(chains):
                    pc.append(cur[u])
                    cur[u] = [cur[u][h] + jnp.sum(g[n][h], axis=1, keepdims=True) for h in heads]
                dzb = []
                for n in nch:
                    dz_n = []
                    for h in heads:
                        sig = jnp.exp(ls[n][h])
                        dz = g[n][h] - sig * (g[n][h] + prefix[n][h] + pc[n][h])
                        if chains[n][2]:
                            dz = jnp.where(below, dz, 0.0)
                        dz_n.append(dz.astype(BF16))
                    dzb.append(dz_n)
                for u in cur:
                    for h in heads:
                        pc_ref[h, qrows[u], :] = cur[u][h]
                dq = [[_dot(dzb[n][h], jnp.where(hmask[h], k2[chains[n][1]], zero)) for h in heads] for n in nch]
                dk = [[_dot_tn(dzb[n][h], qm[chains[n][0]][h]) for h in heads] for n in nch]
                dv = [[_dot_tn(w[n][h], dom[chains[n][0]][h]) for h in heads] for n in nch]
                add = lambda a, b: a + b
                for u in cur:
                    dqi_ref[qrows[u], :] += functools.reduce(
                        add, [dq[n][h] for n in nch if chains[n][0] == u for h in heads])
                for ti in range(len(tiles)):
                    mine = [n for n in nch if chains[n][1] == ti]
                    dk_acc[krows[ti], :] += functools.reduce(add, [dk[n][h] for n in mine for h in heads])
                    dv_acc[krows[ti], :] += functools.reduce(add, [dv[n][h] for n in mine for h in heads])

            for u in range(per_block):
                n_left = first + u - 1
                n_extra = jnp.clip(extra_ref[pair, first + u].astype(jnp.int32), 0, jnp.maximum(n_left, 0))

                def step(t, c, u=u):
                    sweep([t], [(u, 0, False)])
                    return c

                lax.fori_loop(n_left - n_extra, n_left, step, 0)

            @pl.when(i == 0)
            def _():
                sweep([0, 1], [(0, 0, True), (1, 0, False), (1, 1, True)])

            @pl.when(i > 0)
            def _():
                sweep([first - 1, first, first + 1], [(0, 0, False), (1, 1, False), (0, 1, True), (1, 2, True)])

            dq_acc[block_rows, :] += dqi_ref[...] * scale
            return carry

        lax.fori_loop(0, nq, qblock, 0)
        copies = []
        for t, acc in enumerate((dq_acc, dk_acc, dv_acc)):
            stage_ref[t] = acc[...].astype(BF16)
            col0 = pl.multiple_of(t * D_SB + pair * LANES, LANES)
            copies.append(pltpu.make_async_copy(stage_ref.at[t], out_hbm.at[:, pl.ds(col0, LANES)], out_sems.at[t]))
        for cp in copies:
            cp.start()
        for cp in copies:
            cp.wait()

    def body(q_ref, k_ref, v_ref, do_ref, rs_ref, extra_ref, dproj_hbm, *rest):
        small_in, cp_in = rest[:ns], rest[ns:ns + nc]
        out_hbm = rest[ns + nc]
        small_out, land_out = rest[ns + nc + 1:2 * ns + nc + 1], rest[2 * ns + nc + 1:2 * (ns + nc) + 1]
        scratch = rest[2 * (ns + nc) + 1:]
        gather = _Gather(small_in, small_out, *scratch[7:10])
        pair = pl.program_id(0)

        @pl.when(pair == 0)
        def _():
            gather.issue()
            for cp in _chip_push_copies(cp_in, land_out, *scratch[10:]):
                cp.start()

        sweep_body(q_ref, k_ref, v_ref, do_ref, rs_ref, extra_ref, dproj_hbm, out_hbm, *scratch[:7])

        @pl.when(pair == N_PAIRS - 1)
        def _():
            gather.finish()
            for cp in _chip_push_copies(cp_in, land_out, *scratch[10:]):
                cp.wait()

    any_spec = pl.BlockSpec(memory_space=pl.ANY)
    res = pl.pallas_call(
        body, name=name,
        out_shape=[jax.ShapeDtypeStruct(dproj.shape, BF16)] + _gather_out_shapes(smalls)
        + [jax.ShapeDtypeStruct((3,) + a.shape[1:], a.dtype) for a in cps],
        grid=(N_PAIRS,),
        in_specs=[pl.BlockSpec((s, LANES), lambda p: (0, p)),
                  pl.BlockSpec((s, LANES), lambda p: (0, N_PAIRS + p)),
                  pl.BlockSpec((s, LANES), lambda p: (0, 2 * N_PAIRS + p)),
                  pl.BlockSpec((s, LANES), lambda p: (0, p)),
                  pl.BlockSpec((2, s, LANES), lambda p: (p, 0, 0)),
                  pl.BlockSpec(memory_space=pltpu.SMEM),
                  any_spec] + [any_spec] * (ns + nc),
        out_specs=[any_spec] * (1 + ns + nc),
        scratch_shapes=[pltpu.VMEM((s, LANES), F32), pltpu.VMEM((s, LANES), F32), pltpu.VMEM((s, LANES), F32),
                        pltpu.VMEM((tb, LANES), F32), pltpu.VMEM((2, tb, LANES), F32),
                        pltpu.VMEM((3, s, LANES), BF16), pltpu.SemaphoreType.DMA((3,))]
        + _gather_semaphores(ns) + [pltpu.SemaphoreType.DMA((nc, 3)), pltpu.SemaphoreType.DMA((nc, 3))],
        input_output_aliases={6: 0},
        compiler_params=_cparams(("arbitrary",)),
    )(qkv, qkv, qkv, do, rs, extra, dproj, *smalls, *cps)
    return res[0], res[1:1 + ns], res[1 + ns:]


def _mid(o, rest, x, tgt, wua, wub, wout, fg, lng, lnb, wsp, bfull, gavg, name):
    s, d = x.shape
    ts = 256
    nt = s // ts
    nchunk = ts // SGU_CHUNK
    n_rest = rest.shape[1]

    def body(o_ref, rest_ref, x_ref, t_ref, wua_ref, wub_ref, wout_ref, fg_ref, lng_ref, lnb_ref, wsp_ref, bfull_ref,
             gavg_ref, loss_ref, dx2_ref, do_ref, dproj_ref, dwua_ref, dwub_ref, dwout_ref, dfg_ref, dlng_ref,
             dlnb_ref, dwsp_ref, dbfull_ref):
        step = pl.program_id(0)

        @pl.when(step == 0)
        def _():
            for ref in (loss_ref, dwua_ref, dwub_ref, dwout_ref, dfg_ref, dlng_ref, dlnb_ref, dwsp_ref, dbfull_ref):
                ref[...] = jnp.zeros_like(ref)

        gavg = gavg_ref[...]

        def gmean(a):
            return _dot(a.astype(BF16), gavg)

        def colsum8(a):
            return jnp.sum(a.reshape(ts // 8, 8, a.shape[1]), axis=0)

        z_a = rest_ref[:, 0:512]
        u_b = rest_ref[:, 512:1024]
        v_b = rest_ref[:, 1024:1536]
        z_b = rest_ref[:, 1536:2048]
        g_a = rest_ref[:, 2048:2048 + d]
        g_b = rest_ref[:, 2048 + d:2048 + 2 * d]
        ov = o_ref[...]
        sa = _sigmoid(z_a)
        silu_a = z_a * sa
        y_a = ov * silu_a
        ug, dug_du = _gelu_and_grad(u_b)
        vg, dvg_dv = _gelu_and_grad(v_b)
        mu = gmean(vg)
        cen = vg - mu
        rstd_g = lax.rsqrt(gmean(cen * cen) + EPS)
        vhat = cen * rstd_g
        vn = vhat * lng_ref[...] + lnb_ref[...]
        vnb = vn.astype(BF16)

        t_idx = lax.broadcasted_iota(jnp.int32, (SGU_CHUNK, SGU_CHUNK), 0)
        s_idx = lax.broadcasted_iota(jnp.int32, (SGU_CHUNK, SGU_CHUNK), 1)
        causal = (s_idx // CHUNK) <= (t_idx // CHUNK)
        wm = [jnp.where(causal, wsp_ref[g], 0.0) for g in range(N_GROUPS)]
        wmb = [w.astype(BF16) for w in wm]
        wmtb = [w.T.astype(BF16) for w in wm]
        lane = lax.broadcasted_iota(jnp.int32, (SGU_CHUNK, LANES), 1)
        first = lane < GROUP_DIM
        bfull = bfull_ref[...]

        mixed_rows = []
        for n in range(nchunk):
            r0, r1 = n * SGU_CHUNK, (n + 1) * SGU_CHUNK
            pieces = []
            for p in range(N_GROUPS // 2):
                blk = vnb[r0:r1, p * LANES:(p + 1) * LANES]
                pieces.append(jnp.where(first, _dot(wmb[2 * p], blk), _dot(wmb[2 * p + 1], blk)))
            mixed_rows.append(jnp.concatenate(pieces, axis=1) + bfull)
        mixed = jnp.concatenate(mixed_rows, axis=0)
        sg = ug * mixed
        sb = _sigmoid(z_b)
        silu_b = z_b * sb
        y_b = sg * silu_b
        y_ab = y_a.astype(BF16)
        y_bb = y_b.astype(BF16)
        p_a = _dot(y_ab, wua_ref[...])
        p_b = _dot(y_bb, wub_ref[...])
        ga_s = _sigmoid(g_a)
        gb_s = _sigmoid(g_b)
        merged_b = (ga_s * p_a + gb_s * p_b).astype(BF16)
        x2 = x_ref[...] + _dot(merged_b, wout_ref[...])
        rstd = lax.rsqrt(jnp.mean(x2 * x2, axis=-1, keepdims=True) + EPS)
        xhat = x2 * rstd
        fg_v = fg_ref[...]
        diff = xhat * fg_v - t_ref[...]
        loss_ref[...] += 0.5 * jnp.sum(jnp.sum(diff * diff, axis=-1, keepdims=True) * (1.0 / d))

        dy = diff * (1.0 / d)
        dfg_ref[...] += colsum8(dy * xhat)
        dxh = dy * fg_v
        dx2 = rstd * (dxh - xhat * jnp.mean(dxh * xhat, axis=-1, keepdims=True))
        dx2_ref[...] = dx2
        dx2b = dx2.astype(BF16)
        dwout_ref[...] += _dot_tn(merged_b, dx2b)
        dmerged = _dot_nt(dx2b, wout_ref[...])
        dp_a = dmerged * ga_s
        dp_b = dmerged * gb_s
        dproj_ref[:, QKV_COLS + 2048:QKV_COLS + 2048 + d] = (dmerged * p_a * (ga_s * (1.0 - ga_s))).astype(BF16)
        dproj_ref[:, QKV_COLS + 2048 + d:QKV_COLS + 2048 + 2 * d] = (dmerged * p_b * (gb_s * (1.0 - gb_s))).astype(BF16)
        dp_ab = dp_a.astype(BF16)
        dp_bb = dp_b.astype(BF16)
        dwua_ref[...] += _dot_tn(y_ab, dp_ab)
        dwub_ref[...] += _dot_tn(y_bb, dp_bb)
        dy_a = _dot_nt(dp_ab, wua_ref[...])
        dy_b = _dot_nt(dp_bb, wub_ref[...])
        do_ref[...] = (dy_a * silu_a).astype(BF16)
        dproj_ref[:, QKV_COLS:QKV_COLS + 512] = (dy_a * ov * (sa * (1.0 + z_a * (1.0 - sa)))).astype(BF16)
        dsg = dy_b * silu_b
        dproj_ref[:, QKV_COLS + 1536:QKV_COLS + 2048] = (dy_b * sg * (sb * (1.0 + z_b * (1.0 - sb)))).astype(BF16)
        dproj_ref[:, QKV_COLS + 512:QKV_COLS + 1024] = (dsg * mixed * dug_du).astype(BF16)
        dmixed = dsg * ug
        dmb = dmixed.astype(BF16)
        zero = jnp.zeros((), BF16)
        dvn_rows = []
        db = jnp.zeros((SGU_CHUNK, D_SGU), F32)
        for n in range(nchunk):
            r0, r1 = n * SGU_CHUNK, (n + 1) * SGU_CHUNK
            db = db + dmixed[r0:r1, :]
            pieces = []
            for p in range(N_GROUPS // 2):
                cols = slice(p * LANES, (p + 1) * LANES)
                dm_blk = dmb[r0:r1, cols]
                vn_blk = vnb[r0:r1, cols]
                dwsp_ref[2 * p] += _dot_nt(jnp.where(first, dm_blk, zero), vn_blk)
                dwsp_ref[2 * p + 1] += _dot_nt(jnp.where(first, zero, dm_blk), vn_blk)
                pieces.append(jnp.where(first, _dot(wmtb[2 * p], dm_blk), _dot(wmtb[2 * p + 1], dm_blk)))
            dvn_rows.append(jnp.concatenate(pieces, axis=1))
        dbfull_ref[...] += db
        dvn = jnp.concatenate(dvn_rows, axis=0)
        dlng_ref[...] += colsum8(dvn * vhat)
        dlnb_ref[...] += colsum8(dvn)
        dvhat = dvn * lng_ref[...]
        dcen = rstd_g * (dvhat - gmean(dvhat) - vhat * gmean(dvhat * vhat))
        dproj_ref[:, QKV_COLS + 1024:QKV_COLS + 1536] = (dcen * dvg_dv).astype(BF16)

        @pl.when(step == nt - 1)
        def _():
            for g in range(N_GROUPS):
                dwsp_ref[g] = jnp.where(causal, dwsp_ref[g], 0.0)

    def tile(cols):
        return pl.BlockSpec((ts, cols), lambda i: (i, 0))

    def whole(shape):
        return pl.BlockSpec(shape, lambda i: (0,) * len(shape))

    out_shapes = [
        jax.ShapeDtypeStruct((8, LANES), F32),
        jax.ShapeDtypeStruct((s, d), F32),
        jax.ShapeDtypeStruct((s, D_SB), BF16),
        jax.ShapeDtypeStruct((s, QKV_COLS + n_rest), BF16),
        jax.ShapeDtypeStruct((D_SB, d), F32),
        jax.ShapeDtypeStruct((D_SGU, d), F32),
        jax.ShapeDtypeStruct((d, d), F32),
        jax.ShapeDtypeStruct((8, d), F32),
        jax.ShapeDtypeStruct((8, D_SGU), F32),
        jax.ShapeDtypeStruct((8, D_SGU), F32),
        jax.ShapeDtypeStruct((N_GROUPS, SGU_CHUNK, SGU_CHUNK), F32),
        jax.ShapeDtypeStruct((SGU_CHUNK, D_SGU), F32),
    ]
    out_specs = [whole((8, LANES)), tile(d), tile(D_SB), tile(QKV_COLS + n_rest), whole((D_SB, d)), whole((D_SGU, d)),
                 whole((d, d)), whole((8, d)), whole((8, D_SGU)), whole((8, D_SGU)),
                 whole((N_GROUPS, SGU_CHUNK, SGU_CHUNK)), whole((SGU_CHUNK, D_SGU))]
    in_specs = [tile(D_SB), tile(n_rest), tile(d), tile(d), whole((D_SB, d)), whole((D_SGU, d)), whole((d, d)),
                whole((1, d)), whole((1, D_SGU)), whole((1, D_SGU)), whole((N_GROUPS, SGU_CHUNK, SGU_CHUNK)),
                whole((SGU_CHUNK, D_SGU)), whole((D_SGU, D_SGU))]
    return pl.pallas_call(
        body, name=name, out_shape=out_shapes, grid=(nt,), in_specs=in_specs, out_specs=out_specs,
        compiler_params=_cparams(("arbitrary",)),
    )(o, rest, x, tgt, wua, wub, wout, fg, lng, lnb, wsp, bfull, gavg)


def _small_reduce(dfg8, dlng8, dlnb8, dbfull, name):
    d = dfg8.shape[1]

    def body(dfg_ref, dlng_ref, dlnb_ref, dbfull_ref, fg_out, lng_out, lnb_out, b_out):
        row = lax.broadcasted_iota(jnp.int32, (D_SGU, LANES), 0)
        col = lax.broadcasted_iota(jnp.int32, (D_SGU, LANES), 1)

        def select_sum(a, sel):
            hi, lo = _split_hi_lo(a)
            lo2 = (a - hi.astype(F32) - lo.astype(F32)).astype(BF16)
            return _dot(hi, sel) + _dot(lo, sel) + _dot(lo2, sel)

        def by_group(partials):
            v = jnp.sum(partials, axis=0, keepdims=True)
            g_of_lane = lax.broadcasted_iota(jnp.int32, (N_GROUPS, D_SGU), 1) // GROUP_DIM
            g_of_row = lax.broadcasted_iota(jnp.int32, (N_GROUPS, D_SGU), 0)
            spread = jnp.where(g_of_lane == g_of_row, jnp.broadcast_to(v, (N_GROUPS, D_SGU)), 0.0)
            return select_sum(spread, (row % GROUP_DIM == col).astype(BF16))[:, 0:GROUP_DIM]

        fg_out[...] = jnp.sum(dfg_ref[...], axis=0, keepdims=True)
        lng_out[...] = by_group(dlng_ref[...])
        lnb_out[...] = by_group(dlnb_ref[...])
        by_pos = select_sum(dbfull_ref[...], (row // GROUP_DIM == col).astype(BF16))
        b_out[...] = by_pos.T[0:N_GROUPS, :]

    return pl.pallas_call(
        body, name=name,
        out_shape=[jax.ShapeDtypeStruct((1, d), F32), jax.ShapeDtypeStruct((N_GROUPS, GROUP_DIM), F32),
                   jax.ShapeDtypeStruct((N_GROUPS, GROUP_DIM), F32), jax.ShapeDtypeStruct((N_GROUPS, SGU_CHUNK), F32)],
        compiler_params=_cparams(),
    )(dfg8, dlng8, dlnb8, dbfull)


def _block_major_cols(w):
    r, n = w.shape
    return jnp.transpose(w.reshape(r, N_DEV, n // N_DEV), (1, 0, 2))


def _from_block_major_cols(w):
    nb, r, c = w.shape
    return jnp.transpose(w, (1, 0, 2)).reshape(r, nb * c)


def kernel(x, norm_g, w_in, sgu_ln_g, sgu_ln_b, w_spatial, b_spatial, w_up_a, w_up_b, w_out, final_norm_g, loss_target, m_norm_g, m_w_in, m_sgu_ln_g, m_sgu_ln_b, m_w_spatial, m_b_spatial, m_w_up_a, m_w_up_b, m_w_out, m_final_norm_g, v_norm_g, v_w_in, v_sgu_ln_g, v_sgu_ln_b, v_w_spatial, v_b_spatial, v_w_up_a, v_w_up_b, v_w_out, v_final_norm_g):
    s, d = x.shape[1], x.shape[2]
    xs = x[0]
    tgt = loss_target[0]
    cx, cy, cc = _coords()
    core = jnp.reshape(cc, (1,)).astype(jnp.int32)
    chip = jnp.reshape(2 * cx + cy, (1,)).astype(jnp.int32)

    h, ht, (g_win,) = _rmsnorm_fwd(xs, norm_g, [jnp.transpose(w_in[0]).astype(BF16)], "norm")
    d_in = N_DEV * w_in.shape[2]
    w_full = g_win.reshape(d_in, d)
    qkv, rest = _inproj(h, w_full, "inproj")
    o, rs, extra, (g_wua, g_wub, g_wout) = _attn_fwd(
        qkv, [w_up_a[0].astype(BF16), w_up_b[0].astype(BF16), w_out[0].astype(BF16)], "attn_fwd")
    wua_full = _from_block_major_cols(g_wua)
    wub_full = _from_block_major_cols(g_wub)
    wout_full = g_wout.reshape(d, d)

    lng = sgu_ln_g.reshape(1, D_SGU)
    lnb = sgu_ln_b.reshape(1, D_SGU)
    bfull = jnp.repeat(jnp.transpose(b_spatial[0]), GROUP_DIM, axis=1)
    grp = jnp.arange(D_SGU) // GROUP_DIM
    gavg = jnp.where(grp[:, None] == grp[None, :], 1.0 / GROUP_DIM, 0.0).astype(BF16)
    (loss_b, dx2, do, dproj, dwua, dwub, dwout, dfg8, dlng8, dlnb8, dwsp, dbfull) = _mid(
        o, rest, xs, tgt, wua_full, wub_full, wout_full, final_norm_g.reshape(1, d), lng, lnb, w_spatial[0], bfull,
        gavg, "mid")

    def blocks42(a):
        return a.reshape((4, 2) + a.shape[1:])

    def chip_partials(own, tag):
        land = _push_sibling(own, "rs_sibling_" + tag)
        return [_chip_partial_sum(a, l, core, "cpsum_%s%d" % (tag, i)) for i, (a, l) in enumerate(zip(own, land))]

    cps_up = chip_partials([blocks42(_block_major_cols(dwua.astype(BF16))),
                            blocks42(_block_major_cols(dwub.astype(BF16))),
                            blocks42(dwout.astype(BF16).reshape(N_DEV, d // N_DEV, d))], "up")

    dfg, dlng, dlnb, db = _small_reduce(dfg8, dlng8, dlnb8, dbfull, "small_reduce")
    wsp_rows = N_GROUPS * SGU_CHUNK
    dproj, small_parts, land_up = _attn_bwd(
        qkv, do, rs, extra, dproj, [dfg, dlng, dlnb, db, dwsp.reshape(wsp_rows, SGU_CHUNK)], cps_up, "attn_bwd")
    dwin = _dw_in(ht, dproj, "dwin")

    cps_in = chip_partials([dwin.reshape(4, 2, d_in // N_DEV, d)], "in")
    grad_x, dng8, land_in = _dh_and_grad_x(dproj, w_full, xs, norm_g, dx2, cps_in, "dh")

    ng_parts, loss_parts = _allgather([dng8, loss_b], "ag_tail")

    def small_layouts(ng, fg, lg, lb, bs, ws):
        return [ng.reshape(1, d), fg.reshape(1, d), lg.reshape(N_GROUPS, GROUP_DIM), lb.reshape(N_GROUPS, GROUP_DIM),
                bs.reshape(N_GROUPS, SGU_CHUNK), ws.reshape(wsp_rows, SGU_CHUNK)]

    sm, loss_sum = _adam_small(
        [ng_parts] + list(small_parts),
        small_layouts(norm_g, final_norm_g, sgu_ln_g, sgu_ln_b, b_spatial, w_spatial),
        small_layouts(m_norm_g, m_final_norm_g, m_sgu_ln_g, m_sgu_ln_b, m_b_spatial, m_w_spatial),
        small_layouts(v_norm_g, v_final_norm_g, v_sgu_ln_g, v_sgu_ln_b, v_b_spatial, v_w_spatial),
        loss_parts, "adam_small")
    small_shapes = [norm_g.shape, final_norm_g.shape, sgu_ln_g.shape, sgu_ln_b.shape, b_spatial.shape, w_spatial.shape]
    sm = [[a.reshape(shp) for a, shp in zip(kind, small_shapes)] for kind in sm]

    res = _adam_shard(cps_in, list(land_in), chip, *[jnp.transpose(a[0]) for a in (w_in, m_w_in, v_w_in)], "adam0")
    big = [[jnp.transpose(r)[None] for r in res]]
    res = _adam_shards_whole(cps_up, list(land_up), chip, [w_up_a[0], w_up_b[0], w_out[0]],
                             [m_w_up_a[0], m_w_up_b[0], m_w_out[0]], [v_w_up_a[0], v_w_up_b[0], v_w_out[0]], "adam_up")
    for i in range(3):
        big.append([res[kd][i][None] for kd in range(4)])

    loss = loss_sum[0, 0]

    def per_kind(kd):
        return [sm[kd][0], big[0][kd], sm[kd][2], sm[kd][3], sm[kd][5], sm[kd][4], big[1][kd], big[2][kd], big[3][kd],
                sm[kd][1]]

    return (loss, grad_x[None], *per_kind(0), *per_kind(1), *per_kind(2), *per_kind(3))
```

```python
import functools
import math

import jax
import jax.numpy as jnp
from jax import lax
from jax.experimental import pallas as pl
from jax.experimental.pallas import tpu as pltpu

F32 = jnp.float32
BF16 = jnp.bfloat16
MESH = pl.DeviceIdType.MESH

N_DEV = 8
N_HEADS = 8
HEAD_DIM = 64
D_SB = N_HEADS * HEAD_DIM
N_GROUPS = 8
GROUP_DIM = 64
D_SGU = N_GROUPS * GROUP_DIM
SGU_CHUNK = 128
CHUNK = 64
EPS = 1e-6
LANES = 128
N_PAIRS = N_HEADS // 2
QKV_COLS = 3 * D_SB
ATT_BLOCK = 256
ATT_TILE = 128
CARRY_FLOOR = -90.0
R_UNREACHED = -1e30

ADAM_LR = 0.001
ADAM_B1 = 0.9
ADAM_B2 = 0.999
ADAM_EPS = 1e-08
ADAM_WD = 0.01
ADAM_STEP = 10

VMEM_LIMIT = 56 * 1024 * 1024


def _cparams(sem=None, vmem=VMEM_LIMIT):
    return pltpu.CompilerParams(dimension_semantics=sem, vmem_limit_bytes=vmem)


def _dot(a, b):
    return jnp.dot(a, b, preferred_element_type=F32)


def _dot_nt(a, b):
    return lax.dot_general(a, b, (((1,), (1,)), ((), ())), preferred_element_type=F32)


def _dot_tn(a, b):
    return lax.dot_general(a, b, (((0,), (0,)), ((), ())), preferred_element_type=F32)


def _split_hi_lo(a):
    hi = a.astype(BF16)
    lo = (a - hi.astype(F32)).astype(BF16)
    return hi, lo


def _sigmoid(x):
    return 1.0 / (1.0 + jnp.exp(-x))


_GELU_C = math.sqrt(2.0 / math.pi)


def _gelu_and_grad(x):
    x2 = x * x
    inner = _GELU_C * (x + 0.044715 * (x2 * x))
    t = jnp.tanh(inner)
    cdf = 0.5 * (1.0 + t)
    g = x * cdf
    dg = cdf + x * (0.5 * (1.0 - t * t)) * (_GELU_C * (1.0 + 3.0 * 0.044715 * x2))
    return g, dg


def _coords():
    return lax.axis_index("x"), lax.axis_index("y"), lax.axis_index("c")


def _dev_index(px, py, pc):
    return 4 * px + 2 * py + pc


def _allgather(blocks, name):
    n = len(blocks)

    def body(*refs):
        gather = _Gather(refs[:n], refs[n:2 * n], *refs[2 * n:])
        gather.issue()
        gather.finish()

    any_spec = pl.BlockSpec(memory_space=pl.ANY)
    return pl.pallas_call(
        body, name=name,
        out_shape=_gather_out_shapes(blocks),
        in_specs=[any_spec] * n, out_specs=[any_spec] * n,
        scratch_shapes=_gather_semaphores(n),
    )(*blocks)


def _gather_out_shapes(blocks):
    return [jax.ShapeDtypeStruct((N_DEV,) + b.shape, b.dtype) for b in blocks]


def _gather_semaphores(n):
    return [pltpu.SemaphoreType.DMA((n, 7)), pltpu.SemaphoreType.DMA((n, 7)), pltpu.SemaphoreType.DMA((n,))]


class _Gather:
    def __init__(self, ins, outs, send_sems, recv_sems, local_sems, relay=False):
        self.ins, self.outs = ins, outs
        self.send_sems, self.recv_sems, self.local_sems = send_sems, recv_sems, local_sems
        self.n = len(ins)
        self.relay = relay
        x, y, c = _coords()
        self.c = c
        self.me, self.sibling = (x, y, c), (x, y, 1 - c)
        self.chips = [(1 - x, y), (x, 1 - y), (1 - x, 1 - y)]

    def _copy(self, a, k, block, to, src=None):
        dst = self.outs[a].at[_dev_index(*block)]
        return pltpu.make_async_remote_copy(
            src_ref=dst if src is None else src, dst_ref=dst,
            send_sem=self.send_sems.at[a, k], recv_sem=self.recv_sems.at[a, k],
            device_id=to, device_id_type=MESH)

    def _mine(self):
        return [pltpu.make_async_copy(self.ins[a], self.outs[a].at[_dev_index(*self.me)], self.local_sems.at[a])
                for a in range(self.n)]

    def _first(self):
        first = []
        direct = self.chips[:2] if self.relay else self.chips
        for a in range(self.n):
            first.append(self._copy(a, 0, self.me, self.sibling, src=self.ins[a]))
            first += [self._copy(a, 1 + j, self.me, (*chip, self.c), src=self.ins[a])
                      for j, chip in enumerate(direct)]
        return first

    def issue(self):
        for cp in self._mine() + self._first():
            cp.start()

    def _pass_on(self, a, j):
        chip = self.chips[j]
        self._copy(a, 1 + j, (*chip, self.c), self.me).wait_recv()
        self._copy(a, 4 + j, (*chip, self.c), self.sibling).start()

    def finish(self):
        c = self.c
        if self.relay:
            for core in range(2):
                @pl.when(c == core)
                def _(core=core):
                    j_src, j_dst = core, 1 - core
                    for a in range(self.n):
                        self._pass_on(a, j_src)
                        self._copy(a, 3, (*self.chips[j_src], c), (*self.chips[j_dst], c)).start()
                    for a in range(self.n):
                        self._pass_on(a, j_dst)
                    for a in range(self.n):
                        self._pass_on(a, 2)
        else:
            for j in range(3):
                for a in range(self.n):
                    self._pass_on(a, j)
        for a in range(self.n):
            self._copy(a, 0, self.sibling, self.me).wait_recv()
            for j, chip in enumerate(self.chips):
                self._copy(a, 4 + j, (*chip, 1 - c), self.me).wait_recv()
        for a in range(self.n):
            for k in range(7):
                self._copy(a, k, self.me, self.sibling).wait_send()
        for cp in self._mine():
            cp.wait()


def _push_sibling(arrs, name):
    n = len(arrs)

    def body(*refs):
        ins, outs = refs[:n], refs[n:2 * n]
        send_sems, recv_sems = refs[2 * n:]
        x, y, c = _coords()
        sibling = (x, y, 1 - c)
        copies = []
        for a in range(n):
            for k in range(4):
                copies.append(pltpu.make_async_remote_copy(
                    src_ref=ins[a].at[k, 1 - c], dst_ref=outs[a].at[k],
                    send_sem=send_sems.at[a, k], recv_sem=recv_sems.at[a, k],
                    device_id=sibling, device_id_type=MESH))
        for cp in copies:
            cp.start()
        for cp in copies:
            cp.wait()

    any_spec = pl.BlockSpec(memory_space=pl.ANY)
    return pl.pallas_call(
        body, name=name,
        out_shape=[jax.ShapeDtypeStruct((4,) + a.shape[2:], a.dtype) for a in arrs],
        in_specs=[any_spec] * n, out_specs=[any_spec] * n,
        scratch_shapes=[pltpu.SemaphoreType.DMA((n, 4)), pltpu.SemaphoreType.DMA((n, 4))],
    )(*arrs)


def _chip_push_copies(ins, outs, send_sems, recv_sems):
    x, y, c = _coords()
    chips = [(1 - x, y), (x, 1 - y), (1 - x, 1 - y)]
    return [pltpu.make_async_remote_copy(
        src_ref=ins[a].at[2 * px + py], dst_ref=outs[a].at[r],
        send_sem=send_sems.at[a, r], recv_sem=recv_sems.at[a, r],
        device_id=(px, py, c), device_id_type=MESH)
        for a in range(len(ins)) for r, (px, py) in enumerate(chips)]


def _chip_partial_sums(owns, lands, core, name):
    n = len(owns)
    blocks = [a.shape[2:] for a in owns]

    def body(core_ref, *refs):
        del core_ref
        for own_ref, land_ref, out_ref in zip(refs[:n], refs[n:2 * n], refs[2 * n:]):
            out_ref[...] = (own_ref[...].astype(F32) + land_ref[...].astype(F32)).astype(out_ref.dtype)

    return pl.pallas_call(
        body, name=name,
        out_shape=[jax.ShapeDtypeStruct((4,) + b, a.dtype) for a, b in zip(owns, blocks)],
        grid_spec=pltpu.PrefetchScalarGridSpec(
            num_scalar_prefetch=1, grid=(4,),
            in_specs=[pl.BlockSpec((None, None) + b, lambda k, core: (k, core[0], 0, 0)) for b in blocks]
            + [pl.BlockSpec((None,) + b, lambda k, core: (k, 0, 0)) for b in blocks],
            out_specs=[pl.BlockSpec((None,) + b, lambda k, core: (k, 0, 0)) for b in blocks]),
        compiler_params=_cparams(("parallel",)),
    )(core, *owns, *lands)


def _adamw_math(w, g, m, v):
    m = ADAM_B1 * m + (1.0 - ADAM_B1) * g
    v = ADAM_B2 * v + (1.0 - ADAM_B2) * (g * g)
    m_hat = m / (1.0 - ADAM_B1 ** ADAM_STEP)
    v_hat = v / (1.0 - ADAM_B2 ** ADAM_STEP)
    delta = -ADAM_LR * (m_hat / (jnp.sqrt(v_hat) + ADAM_EPS) + ADAM_WD * w)
    return delta, m, v


def _adam_shard(cps, lands, chip, w, m, v, name):
    rows, cols = w.shape
    tr = rows // 4
    nparts = len(cps)

    def body(chip_ref, *refs):
        del chip_ref
        cp_refs, land_refs = refs[:nparts], refs[nparts:2 * nparts]
        w_ref, m_ref, v_ref, g_out, d_out, m_out, v_out = refs[2 * nparts:]
        parts = []
        for cp_ref, land_ref in zip(cp_refs, land_refs):
            g_k = cp_ref[...].astype(F32)
            for r in range(3):
                g_k = g_k + land_ref[r].astype(F32)
            parts.append(g_k)
        g = parts[0] if nparts == 1 else jnp.concatenate(parts, axis=1)
        delta, m_new, v_new = _adamw_math(w_ref[...], g, m_ref[...], v_ref[...])
        g_out[...] = g
        d_out[...] = delta
        m_out[...] = m_new
        v_out[...] = v_new

    tile = pl.BlockSpec((tr, cols), lambda r, chip: (r, 0))
    out = jax.ShapeDtypeStruct((rows, cols), F32)
    return pl.pallas_call(
        body, name=name, out_shape=[out] * 4,
        grid_spec=pltpu.PrefetchScalarGridSpec(
            num_scalar_prefetch=1, grid=(rows // tr,),
            in_specs=[pl.BlockSpec((None, tr, a.shape[2]), lambda r, chip: (chip[0], r, 0)) for a in cps]
            + [pl.BlockSpec((3, tr, a.shape[2]), lambda r, chip: (0, r, 0)) for a in lands]
            + [tile, tile, tile],
            out_specs=[tile] * 4),
        compiler_params=_cparams(("parallel",)),
    )(chip, *cps, *lands, w, m, v)


def _adam_shards_whole(cps, lands, chip, ws, ms, vs, name):
    n = len(ws)

    def body(chip_ref, *refs):
        cp_refs, land_refs = refs[:n], refs[n:2 * n]
        w_refs, m_refs, v_refs = refs[2 * n:3 * n], refs[3 * n:4 * n], refs[4 * n:5 * n]
        outs = refs[5 * n:]
        for i in range(n):
            g = cp_refs[i][chip_ref[0]].astype(F32)
            for r in range(3):
                g = g + land_refs[i][r].astype(F32)
            delta, m_new, v_new = _adamw_math(w_refs[i][...], g, m_refs[i][...], v_refs[i][...])
            outs[i][...] = g
            outs[n + i][...] = delta
            outs[2 * n + i][...] = m_new
            outs[3 * n + i][...] = v_new

    vmem = pl.BlockSpec(memory_space=pltpu.VMEM)
    res = pl.pallas_call(
        body, name=name, out_shape=[jax.ShapeDtypeStruct(w.shape, F32) for w in ws] * 4,
        in_specs=[pl.BlockSpec(memory_space=pltpu.SMEM)] + [vmem] * (5 * n), out_specs=[vmem] * (4 * n),
        compiler_params=_cparams(),
    )(chip, *cps, *lands, *ws, *ms, *vs)
    return [res[k * n:(k + 1) * n] for k in range(4)]


def _adam_small(parts, ws, ms, vs, loss_parts, name):
    n = len(ws)

    def body(*refs):
        p_refs, w_refs, m_refs, v_refs = refs[:n], refs[n:2 * n], refs[2 * n:3 * n], refs[3 * n:4 * n]
        loss_ref, outs, loss_out = refs[4 * n], refs[4 * n + 1:-1], refs[-1]
        total = loss_ref[0]
        for dev in range(1, N_DEV):
            total = total + loss_ref[dev]
        loss_out[...] = total
        for i in range(n):
            g = p_refs[i][0]
            for dev in range(1, N_DEV):
                g = g + p_refs[i][dev]
            if g.shape[0] != w_refs[i].shape[0]:
                g = jnp.sum(g, axis=0, keepdims=True)
            delta, m_new, v_new = _adamw_math(w_refs[i][...], g, m_refs[i][...], v_refs[i][...])
            outs[i][...] = g
            outs[n + i][...] = delta
            outs[2 * n + i][...] = m_new
            outs[3 * n + i][...] = v_new

    out_shapes = [jax.ShapeDtypeStruct(w.shape, F32) for w in ws] * 4
    out_shapes.append(jax.ShapeDtypeStruct(loss_parts.shape[1:], F32))
    res = pl.pallas_call(body, name=name, out_shape=out_shapes, compiler_params=_cparams())(
        *parts, *ws, *ms, *vs, loss_parts)
    return [res[k * n:(k + 1) * n] for k in range(4)], res[-1]


def _rmsnorm_fwd(x, g, riders, name):
    s, d = x.shape
    ts = 512
    nt = s // ts
    nr = len(riders)

    def body(x_ref, g_ref, *rest):
        rider_in, (h_ref, ht_ref) = rest[:nr], rest[nr:nr + 2]
        rider_out, staged = rest[nr + 2:2 * nr + 2], rest[2 * nr + 2:3 * nr + 2]
        sems = rest[3 * nr + 2:]
        i = pl.program_id(0)
        gather = _Gather(staged, rider_out, *sems, relay=True)

        @pl.when(i == 0)
        def _():
            for src, dst in zip(rider_in, staged):
                dst[...] = src[...].astype(BF16)
            gather.issue()

        xv = x_ref[...]
        rstd = lax.rsqrt(jnp.mean(xv * xv, axis=-1, keepdims=True) + EPS)
        h = xv * rstd * g_ref[...]
        h_ref[...] = h.astype(BF16)
        ht_ref[...] = h.T.astype(BF16)

        @pl.when(i == nt - 1)
        def _():
            gather.finish()

    any_spec = pl.BlockSpec(memory_space=pl.ANY)
    res = pl.pallas_call(
        body, name=name,
        out_shape=[jax.ShapeDtypeStruct((s, d), BF16), jax.ShapeDtypeStruct((d, s), BF16)]
        + [jax.ShapeDtypeStruct((N_DEV,) + r.shape, BF16) for r in riders],
        grid=(nt,),
        in_specs=[pl.BlockSpec((ts, d), lambda i: (i, 0)), pl.BlockSpec((1, d), lambda i: (0, 0))]
        + [pl.BlockSpec(r.shape, lambda i: (0, 0)) for r in riders],
        out_specs=[pl.BlockSpec((ts, d), lambda i: (i, 0)), pl.BlockSpec((d, ts), lambda i: (0, i))]
        + [any_spec] * nr,
        scratch_shapes=[pltpu.VMEM(r.shape, BF16) for r in riders] + _gather_semaphores(nr),
        compiler_params=_cparams(("arbitrary",)),
    )(x, g, *riders)
    return res[0], res[1], res[2:]


def _inproj(h, wt, name):
    s, d = h.shape
    n = wt.shape[0]
    tn = 256
    n_qkv = QKV_COLS // tn

    def body(h_ref, w_ref, qkv_ref, rest_ref):
        j = pl.program_id(0)
        res = _dot_nt(h_ref[...], w_ref[...])

        @pl.when(j < n_qkv)
        def _():
            qkv_ref[...] = res.astype(BF16)

        @pl.when(j >= n_qkv)
        def _():
            rest_ref[...] = res

    return pl.pallas_call(
        body, name=name,
        out_shape=[jax.ShapeDtypeStruct((s, QKV_COLS), BF16), jax.ShapeDtypeStruct((s, n - QKV_COLS), F32)],
        grid=(n // tn,),
        in_specs=[pl.BlockSpec((s, d), lambda j: (0, 0)), pl.BlockSpec((tn, d), lambda j: (j, 0))],
        out_specs=[pl.BlockSpec((s, tn), lambda j: (0, jnp.minimum(j, n_qkv - 1))),
                   pl.BlockSpec((s, tn), lambda j: (0, jnp.maximum(j - n_qkv, 0)))],
        compiler_params=_cparams(("arbitrary",)),
    )(h, wt)


def _dw_in(ht, dproj, name):
    d, s = ht.shape
    n = dproj.shape[1]
    tn = 512

    def body(a_ref, b_ref, o_ref):
        o_ref[...] = _dot(a_ref[...], b_ref[...]).T.astype(o_ref.dtype)

    return pl.pallas_call(
        body, name=name, out_shape=jax.ShapeDtypeStruct((n, d), BF16), grid=(n // tn,),
        in_specs=[pl.BlockSpec((d, s), lambda j: (0, 0)), pl.BlockSpec((s, tn), lambda j: (0, j))],
        out_specs=pl.BlockSpec((tn, d), lambda j: (j, 0)),
        compiler_params=_cparams(("parallel",)),
    )(ht, dproj)


def _dh_and_grad_x(dproj, wt, x, g, dx2, cps, name):
    s, n = dproj.shape
    d = wt.shape[1]
    tm, tk = min(s, 1024), 512
    nk = n // tk
    nm = s // tm
    nc = len(cps)

    def body(a_ref, w_ref, x_ref, g_ref, dx2_ref, *rest):
        cp_refs, gx_ref, dg_ref = rest[:nc], rest[nc], rest[nc + 1]
        land_refs = rest[nc + 2:2 * nc + 2]
        acc_ref, send_sems, recv_sems = rest[2 * nc + 2:]
        i, k = pl.program_id(0), pl.program_id(1)

        @pl.when((i == 0) & (k == 0))
        def _():
            for cp in _chip_push_copies(cp_refs, land_refs, send_sems, recv_sems):
                cp.start()
            dg_ref[...] = jnp.zeros_like(dg_ref)

        @pl.when(k == 0)
        def _():
            acc_ref[...] = jnp.zeros_like(acc_ref)

        acc_ref[...] += _dot(a_ref[...], w_ref[...])

        @pl.when(k == nk - 1)
        def _():
            dh = acc_ref[...]
            xv = x_ref[...]
            rstd = lax.rsqrt(jnp.mean(xv * xv, axis=-1, keepdims=True) + EPS)
            xhat = xv * rstd
            dg_ref[...] += jnp.sum((dh * xhat).reshape(tm // 8, 8, d), axis=0)
            dxh = dh * g_ref[...]
            gx_ref[...] = dx2_ref[...] + rstd * (dxh - xhat * jnp.mean(dxh * xhat, axis=-1, keepdims=True))

        @pl.when((i == nm - 1) & (k == nk - 1))
        def _():
            for cp in _chip_push_copies(cp_refs, land_refs, send_sems, recv_sems):
                cp.wait()

    any_spec = pl.BlockSpec(memory_space=pl.ANY)
    res = pl.pallas_call(
        body, name=name,
        out_shape=[jax.ShapeDtypeStruct((s, d), F32), jax.ShapeDtypeStruct((8, d), F32)]
        + [jax.ShapeDtypeStruct((3,) + a.shape[1:], a.dtype) for a in cps],
        grid=(nm, nk),
        in_specs=[pl.BlockSpec((tm, tk), lambda i, k: (i, k)), pl.BlockSpec((tk, d), lambda i, k: (k, 0)),
                  pl.BlockSpec((tm, d), lambda i, k: (i, 0)), pl.BlockSpec((1, d), lambda i, k: (0, 0)),
                  pl.BlockSpec((tm, d), lambda i, k: (i, 0))] + [any_spec] * nc,
        out_specs=[pl.BlockSpec((tm, d), lambda i, k: (i, 0)), pl.BlockSpec((8, d), lambda i, k: (0, 0))]
        + [any_spec] * nc,
        scratch_shapes=[pltpu.VMEM((tm, d), F32), pltpu.SemaphoreType.DMA((nc, 3)), pltpu.SemaphoreType.DMA((nc, 3))],
        compiler_params=_cparams(("arbitrary", "arbitrary")),
    )(dproj, wt, x, g, dx2, *cps)
    return res[0], res[1], res[2:]


def _log_sigmoids(z):
    l1p = jnp.log(1.0 + jnp.exp(-jnp.abs(z)))
    ls = jnp.minimum(z, 0.0) - l1p
    return ls, ls - z


def _strict_lower_ones(n):
    row = lax.broadcasted_iota(jnp.int32, (n, n), 0)
    col = lax.broadcasted_iota(jnp.int32, (n, n), 1)
    return row, col, (row > col).astype(BF16)


def _attn_fwd(qkv, riders, name):
    s = qkv.shape[0]
    tb, tt = ATT_BLOCK, ATT_TILE
    nq = s // tb
    per_block = tb // tt
    assert per_block == 2 and s // tt <= LANES, "two query tiles per grid step; one lane of saved carry per key tile"

    nr = len(riders)

    def sweep_body(gather, q_ref, k_ref, v_ref, o_ref, rs_ref, extra_ref, acc_ref, r_ref, rsv_ref, rmax_ref):
        i = pl.program_id(1)
        lane = lax.broadcasted_iota(jnp.int32, (tt, LANES), 1)
        hmask = [lane < HEAD_DIM, lane >= HEAD_DIM]
        row, col, tri = _strict_lower_ones(tt)
        tri2 = jnp.concatenate([tri, tri], axis=0)
        below = col < row
        qrows = [slice(u * tt, (u + 1) * tt) for u in range(per_block)]
        qm = [[jnp.where(m, q_ref[qrows[u], :], jnp.zeros((), BF16)) * jnp.asarray(HEAD_DIM ** -0.5, BF16)
               for m in hmask] for u in range(per_block)]
        acc_ref[...] = jnp.zeros_like(acc_ref)
        r_ref[...] = jnp.zeros_like(r_ref)
        rsv_ref[...] = jnp.full_like(rsv_ref, R_UNREACHED)

        def sweep(tiles, chains):
            heads, nch = range(2), range(len(chains))
            kv = []
            for t in tiles:
                rows = pl.ds(pl.multiple_of(t * tt, tt), tt)
                kv.append((k_ref[rows, :], v_ref[rows, :]))
            z = [[_dot_nt(qm[u][h], kv[ti][0]) for h in heads] for u, ti, _ in chains]
            ls, lk = [], []
            for n in nch:
                pairs = [_log_sigmoids(z[n][h]) for h in heads]
                ls.append([p[0] for p in pairs])
                lk.append([jnp.where(below, p[1], 0.0) if chains[n][2] else p[1] for p in pairs])
            cur = {u: [r_ref[h, qrows[u], :] for h in heads] for u in sorted({c[0] for c in chains})}
            r = []
            for n, (u, _, _) in enumerate(chains):
                r.append(cur[u])
                cur[u] = [cur[u][h] + jnp.sum(lk[n][h], axis=1, keepdims=True) for h in heads]
            for u in cur:
                rmax_ref[u] = jnp.max(jnp.maximum(cur[u][0], cur[u][1]))
            suffix = [[_dot(jnp.concatenate(_split_hi_lo(lk[n][h]), axis=1), tri2) for h in heads] for n in nch]
            w = []
            for n in nch:
                w_n = [jnp.exp(ls[n][h] + suffix[n][h] + r[n][h]) for h in heads]
                if chains[n][2]:
                    w_n = [jnp.where(below, w_h, 0.0) for w_h in w_n]
                w.append([w_h.astype(BF16) for w_h in w_n])
            pv = [[_dot(w[n][h], kv[chains[n][1]][1]) for h in heads] for n in nch]
            for u in cur:
                mine = [n for n in nch if chains[n][0] == u]
                for h in heads:
                    acc_ref[h, qrows[u], :] += functools.reduce(lambda a, b: a + b, [pv[n][h] for n in mine])
                    for n in mine:
                        if not chains[n][2]:
                            t = tiles[chains[n][1]]
                            rsv_ref[h, qrows[u], :] = jnp.where(lane == t, r[n][h], rsv_ref[h, qrows[u], :])
                    r_ref[h, qrows[u], :] = cur[u][h]

        first = per_block * i

        @pl.when(i == 0)
        def _():
            sweep([0, 1], [(0, 0, True), (1, 1, True), (1, 0, False)])

        @pl.when(i > 0)
        def _():
            sweep([first - 1, first, first + 1], [(0, 1, True), (1, 2, True), (0, 0, False), (1, 1, False)])

        for u in range(per_block):
            n_left = first + u - 1

            def live(c, n_left=n_left):
                jj, rmax = c
                return (jj < n_left) & (rmax >= CARRY_FLOOR)

            def step(c, u=u, n_left=n_left):
                jj, _ = c
                sweep([n_left - 1 - jj], [(u, 0, False)])
                return jj + 1, rmax_ref[u]

            swept, _ = lax.while_loop(live, step, (jnp.int32(0), rmax_ref[u]))
            extra_ref[pl.program_id(0), first + u] = swept.astype(F32)

        lane_b = lax.broadcasted_iota(jnp.int32, (tb, LANES), 1)
        o_ref[...] = jnp.where(lane_b < HEAD_DIM, acc_ref[0], acc_ref[1])
        rs_ref[...] = rsv_ref[...]

        @pl.when((pl.program_id(0) == N_PAIRS - 1) & (i == nq - 1))
        def _():
            gather.finish()

    def body(q_ref, k_ref, v_ref, *rest):
        rider_in, (o_ref, rs_ref, extra_ref) = rest[:nr], rest[nr:nr + 3]
        rider_out = rest[nr + 3:2 * nr + 3]
        acc_ref, r_ref, rsv_ref, rmax_ref = rest[2 * nr + 3:2 * nr + 7]
        gather = _Gather(rider_in, rider_out, *rest[2 * nr + 7:])

        @pl.when((pl.program_id(0) == 0) & (pl.program_id(1) == 0))
        def _():
            gather.issue()

        sweep_body(gather, q_ref, k_ref, v_ref, o_ref, rs_ref, extra_ref, acc_ref, r_ref, rsv_ref, rmax_ref)

    any_spec = pl.BlockSpec(memory_space=pl.ANY)
    res = pl.pallas_call(
        body, name=name,
        out_shape=[jax.ShapeDtypeStruct((s, D_SB), F32), jax.ShapeDtypeStruct((N_HEADS, s, LANES), F32),
                   jax.ShapeDtypeStruct((N_PAIRS, s // tt), F32)] + _gather_out_shapes(riders),
        grid=(N_PAIRS, nq),
        in_specs=[pl.BlockSpec((tb, LANES), lambda p, i: (i, p)),
                  pl.BlockSpec((s, LANES), lambda p, i: (0, N_PAIRS + p)),
                  pl.BlockSpec((s, LANES), lambda p, i: (0, 2 * N_PAIRS + p))] + [any_spec] * nr,
        out_specs=[pl.BlockSpec((tb, LANES), lambda p, i: (i, p)),
                   pl.BlockSpec((2, tb, LANES), lambda p, i: (p, i, 0)),
                   pl.BlockSpec(memory_space=pltpu.SMEM)] + [any_spec] * nr,
        scratch_shapes=[pltpu.VMEM((2, tb, LANES), F32), pltpu.VMEM((2, tb, LANES), F32),
                        pltpu.VMEM((2, tb, LANES), F32), pltpu.SMEM((per_block,), F32)] + _gather_semaphores(nr),
        compiler_params=_cparams(("arbitrary", "arbitrary")),
    )(qkv, qkv, qkv, *riders)
    return res[0], res[1], res[2], res[3:]


def _attn_bwd(qkv, do, rs, extra, dproj, smalls, cps, name):
    s = qkv.shape[0]
    tb, tt = ATT_BLOCK, ATT_TILE
    nq = s // tb
    per_block = tb // tt
    scale = HEAD_DIM ** -0.5

    ns, nc = len(smalls), len(cps)

    def sweep_body(q_ref, k_ref, v_ref, do_ref, rs_ref, extra_ref, dproj_hbm, out_hbm, dq_acc, dk_acc, dv_acc,
                   dqi_ref, pc_ref, stage_ref, out_sems):
        del dproj_hbm
        pair = pl.program_id(0)
        lane = lax.broadcasted_iota(jnp.int32, (tt, LANES), 1)
        hmask = [lane < HEAD_DIM, lane >= HEAD_DIM]
        row, col, tri = _strict_lower_ones(tt)
        tri2 = jnp.concatenate([tri, tri], axis=0)
        triu = (row < col).astype(BF16)
        below = col < row
        zero = jnp.zeros((), BF16)
        qrows = [slice(u * tt, (u + 1) * tt) for u in range(per_block)]
        dq_acc[...] = jnp.zeros_like(dq_acc)
        dk_acc[...] = jnp.zeros_like(dk_acc)
        dv_acc[...] = jnp.zeros_like(dv_acc)

        def qblock(i, carry):
            block_rows = pl.ds(pl.multiple_of(i * tb, tb), tb)
            q2 = q_ref[block_rows, :]
            do2 = do_ref[block_rows, :]
            qm = [[jnp.where(m, q2[qrows[u]], zero) * jnp.asarray(scale, BF16) for m in hmask]
                  for u in range(per_block)]
            dom = [[jnp.where(m, do2[qrows[u]], zero) for m in hmask] for u in range(per_block)]
            rs_i = [[rs_ref[h, pl.ds(pl.multiple_of(i * tb + u * tt, tt), tt), :] for h in range(2)]
                    for u in range(per_block)]
            dqi_ref[...] = jnp.zeros_like(dqi_ref)
            pc_ref[...] = jnp.zeros_like(pc_ref)
            first = per_block * i

            def sweep(tiles, chains):
                heads, nch = range(2), range(len(chains))
                krows = [pl.ds(pl.multiple_of(t * tt, tt), tt) for t in tiles]
                k2 = [k_ref[rows, :] for rows in krows]
                v2 = [v_ref[rows, :] for rows in krows]
                z = [[_dot_nt(qm[u][h], k2[ti]) for h in heads] for u, ti, _ in chains]
                dw = [[_dot_nt(dom[u][h], v2[ti]) for h in heads] for u, ti, _ in chains]
                ls, lk = [], []
                for n in nch:
                    pairs = [_log_sigmoids(z[n][h]) for h in heads]
                    ls.append([p[0] for p in pairs])
                    lk.append([jnp.where(below, p[1], 0.0) if chains[n][2] else p[1] for p in pairs])
                suffix = [[_dot(jnp.concatenate(_split_hi_lo(lk[n][h]), axis=1), tri2) for h in heads] for n in nch]
                w, g = [], []
                for n, (u, ti, diag) in enumerate(chains):
                    w_n = []
                    for h in heads:
                        logw = ls[n][h] + suffix[n][h]
                        if diag:
                            w_n.append(jnp.where(below, jnp.exp(logw), 0.0))
                        else:
                            carry_in = jnp.sum(jnp.where(lane == tiles[ti], rs_i[u][h], 0.0), axis=1, keepdims=True)
                            w_n.append(jnp.exp(logw + carry_in))
                    w.append([w_h.astype(BF16) for w_h in w_n])
                    g.append([w_n[h] * dw[n][h] for h in heads])
                prefix = [[_dot(g[n][h].astype(BF16), triu) for h in heads] for n in nch]
                cur = {u: [pc_ref[h, qrows[u], :] for h in heads] for u in sorted({c[0] for c in chains})}
                pc = []
                for n, (u, _, _) in enumerate(chains):
                    pc.append(cur[u])
                    cur[u] = [cur[u][h] + jnp.sum(g[n][h], axis=1, keepdims=True) for h in heads]
                dzb = []
                for n in nch:
                    dz_n = []
                    for h in heads:
                        sig = jnp.exp(ls[n][h])
                        dz = g[n][h] - sig * (g[n][h] + prefix[n][h] + pc[n][h])
                        if chains[n][2]:
                            dz = jnp.where(below, dz, 0.0)
                        dz_n.append(dz.astype(BF16))
                    dzb.append(dz_n)
                for u in cur:
                    for h in heads:
                        pc_ref[h, qrows[u], :] = cur[u][h]
                dq = [[_dot(dzb[n][h], jnp.where(hmask[h], k2[chains[n][1]], zero)) for h in heads] for n in nch]
                dk = [[_dot_tn(dzb[n][h], qm[chains[n][0]][h]) for h in heads] for n in nch]
                dv = [[_dot_tn(w[n][h], dom[chains[n][0]][h]) for h in heads] for n in nch]
                add = lambda a, b: a + b
                for u in cur:
                    dqi_ref[qrows[u], :] += functools.reduce(
                        add, [dq[n][h] for n in nch if chains[n][0] == u for h in heads])
                for ti in range(len(tiles)):
                    mine = [n for n in nch if chains[n][1] == ti]
                    dk_acc[krows[ti], :] += functools.reduce(add, [dk[n][h] for n in mine for h in heads])
                    dv_acc[krows[ti], :] += functools.reduce(add, [dv[n][h] for n in mine for h in heads])

            for u in range(per_block):
                n_left = first + u - 1
                n_extra = jnp.clip(extra_ref[pair, first + u].astype(jnp.int32), 0, jnp.maximum(n_left, 0))

                def step(t, c, u=u):
                    sweep([t], [(u, 0, False)])
                    return c

                lax.fori_loop(n_left - n_extra, n_left, step, 0)

            @pl.when(i == 0)
            def _():
                sweep([0, 1], [(0, 0, True), (1, 0, False), (1, 1, True)])

            @pl.when(i > 0)
            def _():
                sweep([first - 1, first, first + 1], [(0, 0, False), (1, 1, False), (0, 1, True), (1, 2, True)])

            dq_acc[block_rows, :] += dqi_ref[...] * scale
            return carry

        lax.fori_loop(0, nq, qblock, 0)
        copies = []
        for t, acc in enumerate((dq_acc, dk_acc, dv_acc)):
            stage_ref[t] = acc[...].astype(BF16)
            col0 = pl.multiple_of(t * D_SB + pair * LANES, LANES)
            copies.append(pltpu.make_async_copy(stage_ref.at[t], out_hbm.at[:, pl.ds(col0, LANES)], out_sems.at[t]))
        for cp in copies:
            cp.start()
        for cp in copies:
            cp.wait()

    def body(q_ref, k_ref, v_ref, do_ref, rs_ref, extra_ref, dproj_hbm, *rest):
        small_in, cp_in = rest[:ns], rest[ns:ns + nc]
        out_hbm = rest[ns + nc]
        small_out, land_out = rest[ns + nc + 1:2 * ns + nc + 1], rest[2 * ns + nc + 1:2 * (ns + nc) + 1]
        scratch = rest[2 * (ns + nc) + 1:]
        gather = _Gather(small_in, small_out, *scratch[7:10])
        pair = pl.program_id(0)

        @pl.when(pair == 0)
        def _():
            gather.issue()
            for cp in _chip_push_copies(cp_in, land_out, *scratch[10:]):
                cp.start()

        sweep_body(q_ref, k_ref, v_ref, do_ref, rs_ref, extra_ref, dproj_hbm, out_hbm, *scratch[:7])

        @pl.when(pair == N_PAIRS - 1)
        def _():
            gather.finish()
            for cp in _chip_push_copies(cp_in, land_out, *scratch[10:]):
                cp.wait()

    any_spec = pl.BlockSpec(memory_space=pl.ANY)
    res = pl.pallas_call(
        body, name=name,
        out_shape=[jax.ShapeDtypeStruct(dproj.shape, BF16)] + _gather_out_shapes(smalls)
        + [jax.ShapeDtypeStruct((3,) + a.shape[1:], a.dtype) for a in cps],
        grid=(N_PAIRS,),
        in_specs=[pl.BlockSpec((s, LANES), lambda p: (0, p)),
                  pl.BlockSpec((s, LANES), lambda p: (0, N_PAIRS + p)),
                  pl.BlockSpec((s, LANES), lambda p: (0, 2 * N_PAIRS + p)),
                  pl.BlockSpec((s, LANES), lambda p: (0, p)),
                  pl.BlockSpec((2, s, LANES), lambda p: (p, 0, 0)),
                  pl.BlockSpec(memory_space=pltpu.SMEM),
                  any_spec] + [any_spec] * (ns + nc),
        out_specs=[any_spec] * (1 + ns + nc),
        scratch_shapes=[pltpu.VMEM((s, LANES), F32), pltpu.VMEM((s, LANES), F32), pltpu.VMEM((s, LANES), F32),
                        pltpu.VMEM((tb, LANES), F32), pltpu.VMEM((2, tb, LANES), F32),
                        pltpu.VMEM((3, s, LANES), BF16), pltpu.SemaphoreType.DMA((3,))]
        + _gather_semaphores(ns) + [pltpu.SemaphoreType.DMA((nc, 3)), pltpu.SemaphoreType.DMA((nc, 3))],
        input_output_aliases={6: 0},
        compiler_params=_cparams(("arbitrary",)),
    )(qkv, qkv, qkv, do, rs, extra, dproj, *smalls, *cps)
    return res[0], res[1:1 + ns], res[1 + ns:]


def _mid(o, rest, x, tgt, wua, wub, wout, fg, lng, lnb, wsp, bfull, gavg, name):
    s, d = x.shape
    ts = 256
    nt = s // ts
    nchunk = ts // SGU_CHUNK
    n_rest = rest.shape[1]

    def body(o_ref, rest_ref, x_ref, t_ref, wua_ref, wub_ref, wout_ref, fg_ref, lng_ref, lnb_ref, wsp_ref, bfull_ref,
             gavg_ref, loss_ref, dx2_ref, do_ref, dproj_ref, dwua_ref, dwub_ref, dwout_ref, dfg_ref, dlng_ref,
             dlnb_ref, dwsp_ref, dbfull_ref):
        step = pl.program_id(0)

        @pl.when(step == 0)
        def _():
            for ref in (loss_ref, dwua_ref, dwub_ref, dwout_ref, dfg_ref, dlng_ref, dlnb_ref, dwsp_ref, dbfull_ref):
                ref[...] = jnp.zeros_like(ref)

        gavg = gavg_ref[...]

        def gmean(a):
            return _dot(a.astype(BF16), gavg)

        def colsum8(a):
            return jnp.sum(a.reshape(ts // 8, 8, a.shape[1]), axis=0)

        z_a = rest_ref[:, 0:512]
        u_b = rest_ref[:, 512:1024]
        v_b = rest_ref[:, 1024:1536]
        z_b = rest_ref[:, 1536:2048]
        g_a = rest_ref[:, 2048:2048 + d]
        g_b = rest_ref[:, 2048 + d:2048 + 2 * d]
        ov = o_ref[...]
        sa = _sigmoid(z_a)
        silu_a = z_a * sa
        y_a = ov * silu_a
        ug, dug_du = _gelu_and_grad(u_b)
        vg, dvg_dv = _gelu_and_grad(v_b)
        mu = gmean(vg)
        cen = vg - mu
        rstd_g = lax.rsqrt(gmean(cen * cen) + EPS)
        vhat = cen * rstd_g
        vn = vhat * lng_ref[...] + lnb_ref[...]
        vnb = vn.astype(BF16)

        t_idx = lax.broadcasted_iota(jnp.int32, (SGU_CHUNK, SGU_CHUNK), 0)
        s_idx = lax.broadcasted_iota(jnp.int32, (SGU_CHUNK, SGU_CHUNK), 1)
        causal = (s_idx // CHUNK) <= (t_idx // CHUNK)
        wm = [jnp.where(causal, wsp_ref[g], 0.0) for g in range(N_GROUPS)]
        wmb = [w.astype(BF16) for w in wm]
        wmtb = [w.T.astype(BF16) for w in wm]
        lane = lax.broadcasted_iota(jnp.int32, (SGU_CHUNK, LANES), 1)
        first = lane < GROUP_DIM
        bfull = bfull_ref[...]

        mixed_rows = []
        for n in range(nchunk):
            r0, r1 = n * SGU_CHUNK, (n + 1) * SGU_CHUNK
            pieces = []
            for p in range(N_GROUPS // 2):
                blk = vnb[r0:r1, p * LANES:(p + 1) * LANES]
                pieces.append(jnp.where(first, _dot(wmb[2 * p], blk), _dot(wmb[2 * p + 1], blk)))
            mixed_rows.append(jnp.concatenate(pieces, axis=1) + bfull)
        mixed = jnp.concatenate(mixed_rows, axis=0)
        sg = ug * mixed
        sb = _sigmoid(z_b)
        silu_b = z_b * sb
        y_b = sg * silu_b
        y_ab = y_a.astype(BF16)
        y_bb = y_b.astype(BF16)
        p_a = _dot(y_ab, wua_ref[...])
        p_b = _dot(y_bb, wub_ref[...])
        ga_s = _sigmoid(g_a)
        gb_s = _sigmoid(g_b)
        merged_b = (ga_s * p_a + gb_s * p_b).astype(BF16)
        x2 = x_ref[...] + _dot(merged_b, wout_ref[...])
        rstd = lax.rsqrt(jnp.mean(x2 * x2, axis=-1, keepdims=True) + EPS)
        xhat = x2 * rstd
        fg_v = fg_ref[...]
        diff = xhat * fg_v - t_ref[...]
        loss_ref[...] += 0.5 * jnp.sum(jnp.sum(diff * diff, axis=-1, keepdims=True) * (1.0 / d))

        dy = diff * (1.0 / d)
        dfg_ref[...] += colsum8(dy * xhat)
        dxh = dy * fg_v
        dx2 = rstd * (dxh - xhat * jnp.mean(dxh * xhat, axis=-1, keepdims=True))
        dx2_ref[...] = dx2
        dx2b = dx2.astype(BF16)
        dwout_ref[...] += _dot_tn(merged_b, dx2b)
        dmerged = _dot_nt(dx2b, wout_ref[...])
        dp_a = dmerged * ga_s
        dp_b = dmerged * gb_s
        dproj_ref[:, QKV_COLS + 2048:QKV_COLS + 2048 + d] = (dmerged * p_a * (ga_s * (1.0 - ga_s))).astype(BF16)
        dproj_ref[:, QKV_COLS + 2048 + d:QKV_COLS + 2048 + 2 * d] = (dmerged * p_b * (gb_s * (1.0 - gb_s))).astype(BF16)
        dp_ab = dp_a.astype(BF16)
        dp_bb = dp_b.astype(BF16)
        dwua_ref[...] += _dot_tn(y_ab, dp_ab)
        dwub_ref[...] += _dot_tn(y_bb, dp_bb)
        dy_a = _dot_nt(dp_ab, wua_ref[...])
        dy_b = _dot_nt(dp_bb, wub_ref[...])
        do_ref[...] = (dy_a * silu_a).astype(BF16)
        dproj_ref[:, QKV_COLS:QKV_COLS + 512] = (dy_a * ov * (sa * (1.0 + z_a * (1.0 - sa)))).astype(BF16)
        dsg = dy_b * silu_b
        dproj_ref[:, QKV_COLS + 1536:QKV_COLS + 2048] = (dy_b * sg * (sb * (1.0 + z_b * (1.0 - sb)))).astype(BF16)
        dproj_ref[:, QKV_COLS + 512:QKV_COLS + 1024] = (dsg * mixed * dug_du).astype(BF16)
        dmixed = dsg * ug
        dmb = dmixed.astype(BF16)
        zero = jnp.zeros((), BF16)
        dvn_rows = []
        db = jnp.zeros((SGU_CHUNK, D_SGU), F32)
        for n in range(nchunk):
            r0, r1 = n * SGU_CHUNK, (n + 1) * SGU_CHUNK
            db = db + dmixed[r0:r1, :]
            pieces = []
            for p in range(N_GROUPS // 2):
                cols = slice(p * LANES, (p + 1) * LANES)
                dm_blk = dmb[r0:r1, cols]
                vn_blk = vnb[r0:r1, cols]
                dwsp_ref[2 * p] += _dot_nt(jnp.where(first, dm_blk, zero), vn_blk)
                dwsp_ref[2 * p + 1] += _dot_nt(jnp.where(first, zero, dm_blk), vn_blk)
                pieces.append(jnp.where(first, _dot(wmtb[2 * p], dm_blk), _dot(wmtb[2 * p + 1], dm_blk)))
            dvn_rows.append(jnp.concatenate(pieces, axis=1))
        dbfull_ref[...] += db
        dvn = jnp.concatenate(dvn_rows, axis=0)
        dlng_ref[...] += colsum8(dvn * vhat)
        dlnb_ref[...] += colsum8(dvn)
        dvhat = dvn * lng_ref[...]
        dcen = rstd_g * (dvhat - gmean(dvhat) - vhat * gmean(dvhat * vhat))
        dproj_ref[:, QKV_COLS + 1024:QKV_COLS + 1536] = (dcen * dvg_dv).astype(BF16)

        @pl.when(step == nt - 1)
        def _():
            for g in range(N_GROUPS):
                dwsp_ref[g] = jnp.where(causal, dwsp_ref[g], 0.0)

    def tile(cols):
        return pl.BlockSpec((ts, cols), lambda i: (i, 0))

    def whole(shape):
        return pl.BlockSpec(shape, lambda i: (0,) * len(shape))

    out_shapes = [
        jax.ShapeDtypeStruct((8, LANES), F32),
        jax.ShapeDtypeStruct((s, d), F32),
        jax.ShapeDtypeStruct((s, D_SB), BF16),
        jax.ShapeDtypeStruct((s, QKV_COLS + n_rest), BF16),
        jax.ShapeDtypeStruct((D_SB, d), F32),
        jax.ShapeDtypeStruct((D_SGU, d), F32),
        jax.ShapeDtypeStruct((d, d), F32),
        jax.ShapeDtypeStruct((8, d), F32),
        jax.ShapeDtypeStruct((8, D_SGU), F32),
        jax.ShapeDtypeStruct((8, D_SGU), F32),
        jax.ShapeDtypeStruct((N_GROUPS, SGU_CHUNK, SGU_CHUNK), F32),
        jax.ShapeDtypeStruct((SGU_CHUNK, D_SGU), F32),
    ]
    out_specs = [whole((8, LANES)), tile(d), tile(D_SB), tile(QKV_COLS + n_rest), whole((D_SB, d)), whole((D_SGU, d)),
                 whole((d, d)), whole((8, d)), whole((8, D_SGU)), whole((8, D_SGU)),
                 whole((N_GROUPS, SGU_CHUNK, SGU_CHUNK)), whole((SGU_CHUNK, D_SGU))]
    in_specs = [tile(D_SB), tile(n_rest), tile(d), tile(d), whole((D_SB, d)), whole((D_SGU, d)), whole((d, d)),
                whole((1, d)), whole((1, D_SGU)), whole((1, D_SGU)), whole((N_GROUPS, SGU_CHUNK, SGU_CHUNK)),
                whole((SGU_CHUNK, D_SGU)), whole((D_SGU, D_SGU))]
    return pl.pallas_call(
        body, name=name, out_shape=out_shapes, grid=(nt,), in_specs=in_specs, out_specs=out_specs,
        compiler_params=_cparams(("arbitrary",)),
    )(o, rest, x, tgt, wua, wub, wout, fg, lng, lnb, wsp, bfull, gavg)


def _small_reduce(dfg8, dlng8, dlnb8, dbfull, name):
    d = dfg8.shape[1]

    def body(dfg_ref, dlng_ref, dlnb_ref, dbfull_ref, fg_out, lng_out, lnb_out, b_out):
        row = lax.broadcasted_iota(jnp.int32, (D_SGU, LANES), 0)
        col = lax.broadcasted_iota(jnp.int32, (D_SGU, LANES), 1)

        def select_sum(a, sel):
            hi, lo = _split_hi_lo(a)
            lo2 = (a - hi.astype(F32) - lo.astype(F32)).astype(BF16)
            return _dot(hi, sel) + _dot(lo, sel) + _dot(lo2, sel)

        def by_group(partials):
            v = jnp.sum(partials, axis=0, keepdims=True)
            g_of_lane = lax.broadcasted_iota(jnp.int32, (N_GROUPS, D_SGU), 1) // GROUP_DIM
            g_of_row = lax.broadcasted_iota(jnp.int32, (N_GROUPS, D_SGU), 0)
            spread = jnp.where(g_of_lane == g_of_row, jnp.broadcast_to(v, (N_GROUPS, D_SGU)), 0.0)
            return select_sum(spread, (row % GROUP_DIM == col).astype(BF16))[:, 0:GROUP_DIM]

        fg_out[...] = jnp.sum(dfg_ref[...], axis=0, keepdims=True)
        lng_out[...] = by_group(dlng_ref[...])
        lnb_out[...] = by_group(dlnb_ref[...])
        by_pos = select_sum(dbfull_ref[...], (row // GROUP_DIM == col).astype(BF16))
        b_out[...] = by_pos.T[0:N_GROUPS, :]

    return pl.pallas_call(
        body, name=name,
        out_shape=[jax.ShapeDtypeStruct((1, d), F32), jax.ShapeDtypeStruct((N_GROUPS, GROUP_DIM), F32),
                   jax.ShapeDtypeStruct((N_GROUPS, GROUP_DIM), F32), jax.ShapeDtypeStruct((N_GROUPS, SGU_CHUNK), F32)],
        compiler_params=_cparams(),
    )(dfg8, dlng8, dlnb8, dbfull)


def _block_major_cols(w):
    r, n = w.shape
    return jnp.transpose(w.reshape(r, N_DEV, n // N_DEV), (1, 0, 2))


def _from_block_major_cols(w):
    nb, r, c = w.shape
    return jnp.transpose(w, (1, 0, 2)).reshape(r, nb * c)


def kernel(x, norm_g, w_in, sgu_ln_g, sgu_ln_b, w_spatial, b_spatial, w_up_a, w_up_b, w_out, final_norm_g, loss_target, m_norm_g, m_w_in, m_sgu_ln_g, m_sgu_ln_b, m_w_spatial, m_b_spatial, m_w_up_a, m_w_up_b, m_w_out, m_final_norm_g, v_norm_g, v_w_in, v_sgu_ln_g, v_sgu_ln_b, v_w_spatial, v_b_spatial, v_w_up_a, v_w_up_b, v_w_out, v_final_norm_g):
    s, d = x.shape[1], x.shape[2]
    xs = x[0]
    tgt = loss_target[0]
    cx, cy, cc = _coords()
    core = jnp.reshape(cc, (1,)).astype(jnp.int32)
    chip = jnp.reshape(2 * cx + cy, (1,)).astype(jnp.int32)

    h, ht, (g_win,) = _rmsnorm_fwd(xs, norm_g, [jnp.transpose(w_in[0])], "norm")
    d_in = N_DEV * w_in.shape[2]
    w_full = g_win.reshape(d_in, d)
    qkv, rest = _inproj(h, w_full, "inproj")
    o, rs, extra, (g_wua, g_wub, g_wout) = _attn_fwd(
        qkv, [w_up_a[0].astype(BF16), w_up_b[0].astype(BF16), w_out[0].astype(BF16)], "attn_fwd")
    wua_full = _from_block_major_cols(g_wua)
    wub_full = _from_block_major_cols(g_wub)
    wout_full = g_wout.reshape(d, d)

    lng = sgu_ln_g.reshape(1, D_SGU)
    lnb = sgu_ln_b.reshape(1, D_SGU)
    bfull = jnp.repeat(jnp.transpose(b_spatial[0]), GROUP_DIM, axis=1)
    grp = jnp.arange(D_SGU) // GROUP_DIM
    gavg = jnp.where(grp[:, None] == grp[None, :], 1.0 / GROUP_DIM, 0.0).astype(BF16)
    (loss_b, dx2, do, dproj, dwua, dwub, dwout, dfg8, dlng8, dlnb8, dwsp, dbfull) = _mid(
        o, rest, xs, tgt, wua_full, wub_full, wout_full, final_norm_g.reshape(1, d), lng, lnb, w_spatial[0], bfull,
        gavg, "mid")

    def blocks42(a):
        return a.reshape((4, 2) + a.shape[1:])

    def chip_partials(own, tag):
        land = _push_sibling(own, "rs_sibling_" + tag)
        return list(_chip_partial_sums(own, list(land), core, "cpsum_" + tag))

    cps_up = chip_partials([blocks42(_block_major_cols(dwua.astype(BF16))),
                            blocks42(_block_major_cols(dwub.astype(BF16))),
                            blocks42(dwout.astype(BF16).reshape(N_DEV, d // N_DEV, d))], "up")

    dfg, dlng, dlnb, db = _small_reduce(dfg8, dlng8, dlnb8, dbfull, "small_reduce")
    wsp_rows = N_GROUPS * SGU_CHUNK
    dproj, small_parts, land_up = _attn_bwd(
        qkv, do, rs, extra, dproj, [dfg, dlng, dlnb, db, dwsp.reshape(wsp_rows, SGU_CHUNK)], cps_up, "attn_bwd")
    dwin = _dw_in(ht, dproj, "dwin")

    cps_in = chip_partials([dwin.reshape(4, 2, d_in // N_DEV, d)], "in")
    grad_x, dng8, land_in = _dh_and_grad_x(dproj, w_full, xs, norm_g, dx2, cps_in, "dh")

    ng_parts, loss_parts = _allgather([dng8, loss_b], "ag_tail")

    def small_layouts(ng, fg, lg, lb, bs, ws):
        return [ng.reshape(1, d), fg.reshape(1, d), lg.reshape(N_GROUPS, GROUP_DIM), lb.reshape(N_GROUPS, GROUP_DIM),
                bs.reshape(N_GROUPS, SGU_CHUNK), ws.reshape(wsp_rows, SGU_CHUNK)]

    sm, loss_sum = _adam_small(
        [ng_parts] + list(small_parts),
        small_layouts(norm_g, final_norm_g, sgu_ln_g, sgu_ln_b, b_spatial, w_spatial),
        small_layouts(m_norm_g, m_final_norm_g, m_sgu_ln_g, m_sgu_ln_b, m_b_spatial, m_w_spatial),
        small_layouts(v_norm_g, v_final_norm_g, v_sgu_ln_g, v_sgu_ln_b, v_b_spatial, v_w_spatial),
        loss_parts, "adam_small")
    small_shapes = [norm_g.shape, final_norm_g.shape, sgu_ln_g.shape, sgu_ln_b.shape, b_spatial.shape, w_spatial.shape]
    sm = [[a.reshape(shp) for a, shp in zip(kind, small_shapes)] for kind in sm]

    res = _adam_shard(cps_in, list(land_in), chip, *[jnp.transpose(a[0]) for a in (w_in, m_w_in, v_w_in)], "adam0")
    big = [[jnp.transpose(r)[None] for r in res]]
    res = _adam_shards_whole(cps_up, list(land_up), chip, [w_up_a[0], w_up_b[0], w_out[0]],
                             [m_w_up_a[0], m_w_up_b[0], m_w_out[0]], [v_w_up_a[0], v_w_up_b[0], v_w_out[0]], "adam_up")
    for i in range(3):
        big.append([res[kd][i][None] for kd in range(4)])

    loss = loss_sum[0, 0]

    def per_kind(kd):
        return [sm[kd][0], big[0][kd], sm[kd][2], sm[kd][3], sm[kd][5], sm[kd][4], big[1][kd], big[2][kd], big[3][kd],
                sm[kd][1]]

    return (loss, grad_x[None], *per_kind(0), *per_kind(1), *per_kind(2), *per_kind(3))
```

```python
import functools
import math

import jax
import jax.numpy as jnp
from jax import lax
from jax.experimental import pallas as pl
from jax.experimental.pallas import tpu as pltpu

F32 = jnp.float32
BF16 = jnp.bfloat16
MESH = pl.DeviceIdType.MESH

N_DEV = 8
N_HEADS = 8
HEAD_DIM = 64
D_SB = N_HEADS * HEAD_DIM
N_GROUPS = 8
GROUP_DIM = 64
D_SGU = N_GROUPS * GROUP_DIM
SGU_CHUNK = 128
CHUNK = 64
EPS = 1e-6
LANES = 128
N_PAIRS = N_HEADS // 2
QKV_COLS = 3 * D_SB
ATT_BLOCK = 256
ATT_TILE = 128
CARRY_FLOOR = -90.0
R_UNREACHED = -1e30

ADAM_LR = 0.001
ADAM_B1 = 0.9
ADAM_B2 = 0.999
ADAM_EPS = 1e-08
ADAM_WD = 0.01
ADAM_STEP = 10

VMEM_LIMIT = 56 * 1024 * 1024


def _cparams(sem=None, vmem=VMEM_LIMIT):
    return pltpu.CompilerParams(dimension_semantics=sem, vmem_limit_bytes=vmem)


def _dot(a, b):
    return jnp.dot(a, b, preferred_element_type=F32)


def _dot_nt(a, b):
    return lax.dot_general(a, b, (((1,), (1,)), ((), ())), preferred_element_type=F32)


def _dot_tn(a, b):
    return lax.dot_general(a, b, (((0,), (0,)), ((), ())), preferred_element_type=F32)


def _split_hi_lo(a):
    hi = a.astype(BF16)
    lo = (a - hi.astype(F32)).astype(BF16)
    return hi, lo


def _sigmoid(x):
    return 1.0 / (1.0 + jnp.exp(-x))


_GELU_C = math.sqrt(2.0 / math.pi)


def _gelu_and_grad(x):
    x2 = x * x
    inner = _GELU_C * (x + 0.044715 * (x2 * x))
    t = jnp.tanh(inner)
    cdf = 0.5 * (1.0 + t)
    g = x * cdf
    dg = cdf + x * (0.5 * (1.0 - t * t)) * (_GELU_C * (1.0 + 3.0 * 0.044715 * x2))
    return g, dg


def _coords():
    return lax.axis_index("x"), lax.axis_index("y"), lax.axis_index("c")


def _dev_index(px, py, pc):
    return 4 * px + 2 * py + pc


def _allgather(blocks, name):
    n = len(blocks)

    def body(*refs):
        gather = _Gather(refs[:n], refs[n:2 * n], *refs[2 * n:])
        gather.issue()
        gather.finish()

    any_spec = pl.BlockSpec(memory_space=pl.ANY)
    return pl.pallas_call(
        body, name=name,
        out_shape=_gather_out_shapes(blocks),
        in_specs=[any_spec] * n, out_specs=[any_spec] * n,
        scratch_shapes=_gather_semaphores(n),
    )(*blocks)


def _gather_out_shapes(blocks):
    return [jax.ShapeDtypeStruct((N_DEV,) + b.shape, b.dtype) for b in blocks]


def _gather_semaphores(n):
    return [pltpu.SemaphoreType.DMA((n, 7)), pltpu.SemaphoreType.DMA((n, 7)), pltpu.SemaphoreType.DMA((n,))]


class _Gather:
    def __init__(self, ins, outs, send_sems, recv_sems, local_sems, relay=False):
        self.ins, self.outs = ins, outs
        self.send_sems, self.recv_sems, self.local_sems = send_sems, recv_sems, local_sems
        self.n = len(ins)
        self.relay = relay
        x, y, c = _coords()
        self.c = c
        self.me, self.sibling = (x, y, c), (x, y, 1 - c)
        self.chips = [(1 - x, y), (x, 1 - y), (1 - x, 1 - y)]

    def _copy(self, a, k, block, to, src=None):
        dst = self.outs[a].at[_dev_index(*block)]
        return pltpu.make_async_remote_copy(
            src_ref=dst if src is None else src, dst_ref=dst,
            send_sem=self.send_sems.at[a, k], recv_sem=self.recv_sems.at[a, k],
            device_id=to, device_id_type=MESH)

    def _mine(self):
        return [pltpu.make_async_copy(self.ins[a], self.outs[a].at[_dev_index(*self.me)], self.local_sems.at[a])
                for a in range(self.n)]

    def _first(self):
        first = []
        direct = self.chips[:2] if self.relay else self.chips
        for a in range(self.n):
            first.append(self._copy(a, 0, self.me, self.sibling, src=self.ins[a]))
            first += [self._copy(a, 1 + j, self.me, (*chip, self.c), src=self.ins[a])
                      for j, chip in enumerate(direct)]
        return first

    def issue(self):
        for cp in self._mine() + self._first():
            cp.start()

    def _pass_on(self, a, j):
        chip = self.chips[j]
        self._copy(a, 1 + j, (*chip, self.c), self.me).wait_recv()
        self._copy(a, 4 + j, (*chip, self.c), self.sibling).start()

    def finish(self):
        c = self.c
        if self.relay:
            for core in range(2):
                @pl.when(c == core)
                def _(core=core):
                    j_src, j_dst = core, 1 - core
                    for a in range(self.n):
                        self._pass_on(a, j_src)
                        self._copy(a, 3, (*self.chips[j_src], c), (*self.chips[j_dst], c)).start()
                    for a in range(self.n):
                        self._pass_on(a, j_dst)
                    for a in range(self.n):
                        self._pass_on(a, 2)
        else:
            for j in range(3):
                for a in range(self.n):
                    self._pass_on(a, j)
        for a in range(self.n):
            self._copy(a, 0, self.sibling, self.me).wait_recv()
            for j, chip in enumerate(self.chips):
                self._copy(a, 4 + j, (*chip, 1 - c), self.me).wait_recv()
        for a in range(self.n):
            for k in range(7):
                self._copy(a, k, self.me, self.sibling).wait_send()
        for cp in self._mine():
            cp.wait()


def _push_sibling(arrs, name):
    n = len(arrs)

    def body(*refs):
        ins, outs = refs[:n], refs[n:2 * n]
        send_sems, recv_sems = refs[2 * n:]
        x, y, c = _coords()
        sibling = (x, y, 1 - c)
        copies = []
        for a in range(n):
            for k in range(4):
                copies.append(pltpu.make_async_remote_copy(
                    src_ref=ins[a].at[k, 1 - c], dst_ref=outs[a].at[k],
                    send_sem=send_sems.at[a, k], recv_sem=recv_sems.at[a, k],
                    device_id=sibling, device_id_type=MESH))
        for cp in copies:
            cp.start()
        for cp in copies:
            cp.wait()

    any_spec = pl.BlockSpec(memory_space=pl.ANY)
    return pl.pallas_call(
        body, name=name,
        out_shape=[jax.ShapeDtypeStruct((4,) + a.shape[2:], a.dtype) for a in arrs],
        in_specs=[any_spec] * n, out_specs=[any_spec] * n,
        scratch_shapes=[pltpu.SemaphoreType.DMA((n, 4)), pltpu.SemaphoreType.DMA((n, 4))],
    )(*arrs)


def _chip_push_copies(ins, outs, send_sems, recv_sems):
    x, y, c = _coords()
    chips = [(1 - x, y), (x, 1 - y), (1 - x, 1 - y)]
    return [pltpu.make_async_remote_copy(
        src_ref=ins[a].at[2 * px + py], dst_ref=outs[a].at[r],
        send_sem=send_sems.at[a, r], recv_sem=recv_sems.at[a, r],
        device_id=(px, py, c), device_id_type=MESH)
        for a in range(len(ins)) for r, (px, py) in enumerate(chips)]


def _direct_push_copies(ins, outs, send_sems, recv_sems):
    x, y, c = _coords()
    copies = []
    for a in range(len(ins)):
        for r in range(1, N_DEV):
            px = 1 - x if r & 4 else x
            py = 1 - y if r & 2 else y
            pc = 1 - c if r & 1 else c
            copies.append(pltpu.make_async_remote_copy(
                src_ref=ins[a].at[_dev_index(px, py, pc)], dst_ref=outs[a].at[r - 1],
                send_sem=send_sems.at[a, r - 1], recv_sem=recv_sems.at[a, r - 1],
                device_id=(px, py, pc), device_id_type=MESH))
    return copies


def _chip_partial_sums(owns, lands, core, name):
    n = len(owns)
    blocks = [a.shape[2:] for a in owns]

    def body(core_ref, *refs):
        del core_ref
        for own_ref, land_ref, out_ref in zip(refs[:n], refs[n:2 * n], refs[2 * n:]):
            out_ref[...] = (own_ref[...].astype(F32) + land_ref[...].astype(F32)).astype(out_ref.dtype)

    return pl.pallas_call(
        body, name=name,
        out_shape=[jax.ShapeDtypeStruct((4,) + b, a.dtype) for a, b in zip(owns, blocks)],
        grid_spec=pltpu.PrefetchScalarGridSpec(
            num_scalar_prefetch=1, grid=(4,),
            in_specs=[pl.BlockSpec((None, None) + b, lambda k, core: (k, core[0], 0, 0)) for b in blocks]
            + [pl.BlockSpec((None,) + b, lambda k, core: (k, 0, 0)) for b in blocks],
            out_specs=[pl.BlockSpec((None,) + b, lambda k, core: (k, 0, 0)) for b in blocks]),
        compiler_params=_cparams(("parallel",)),
    )(core, *owns, *lands)


def _adamw_math(w, g, m, v):
    m = ADAM_B1 * m + (1.0 - ADAM_B1) * g
    v = ADAM_B2 * v + (1.0 - ADAM_B2) * (g * g)
    m_hat = m / (1.0 - ADAM_B1 ** ADAM_STEP)
    v_hat = v / (1.0 - ADAM_B2 ** ADAM_STEP)
    delta = -ADAM_LR * (m_hat / (jnp.sqrt(v_hat) + ADAM_EPS) + ADAM_WD * w)
    return delta, m, v


def _adam_shard(cps, lands, chip, w, m, v, name):
    rows, cols = w.shape
    tr = rows // 4
    nparts = len(cps)

    def body(chip_ref, *refs):
        del chip_ref
        cp_refs, land_refs = refs[:nparts], refs[nparts:2 * nparts]
        w_ref, m_ref, v_ref, g_out, d_out, m_out, v_out = refs[2 * nparts:]
        parts = []
        for cp_ref, land_ref in zip(cp_refs, land_refs):
            g_k = cp_ref[...].astype(F32)
            for r in range(3):
                g_k = g_k + land_ref[r].astype(F32)
            parts.append(g_k)
        g = parts[0] if nparts == 1 else jnp.concatenate(parts, axis=1)
        delta, m_new, v_new = _adamw_math(w_ref[...], g, m_ref[...], v_ref[...])
        g_out[...] = g
        d_out[...] = delta
        m_out[...] = m_new
        v_out[...] = v_new

    tile = pl.BlockSpec((tr, cols), lambda r, chip: (r, 0))
    out = jax.ShapeDtypeStruct((rows, cols), F32)
    return pl.pallas_call(
        body, name=name, out_shape=[out] * 4,
        grid_spec=pltpu.PrefetchScalarGridSpec(
            num_scalar_prefetch=1, grid=(rows // tr,),
            in_specs=[pl.BlockSpec((None, tr, a.shape[2]), lambda r, chip: (chip[0], r, 0)) for a in cps]
            + [pl.BlockSpec((3, tr, a.shape[2]), lambda r, chip: (0, r, 0)) for a in lands]
            + [tile, tile, tile],
            out_specs=[tile] * 4),
        compiler_params=_cparams(("parallel",)),
    )(chip, *cps, *lands, w, m, v)


def _adam_shards_whole(owns, lands, dev, ws, ms, vs, name):
    n = len(ws)

    def body(dev_ref, *refs):
        own_refs, land_refs = refs[:n], refs[n:2 * n]
        w_refs, m_refs, v_refs = refs[2 * n:3 * n], refs[3 * n:4 * n], refs[4 * n:5 * n]
        outs = refs[5 * n:]
        for i in range(n):
            g = own_refs[i][dev_ref[0]].astype(F32)
            for r in range(N_DEV - 1):
                g = g + land_refs[i][r].astype(F32)
            delta, m_new, v_new = _adamw_math(w_refs[i][...], g, m_refs[i][...], v_refs[i][...])
            outs[i][...] = g
            outs[n + i][...] = delta
            outs[2 * n + i][...] = m_new
            outs[3 * n + i][...] = v_new

    vmem = pl.BlockSpec(memory_space=pltpu.VMEM)
    res = pl.pallas_call(
        body, name=name, out_shape=[jax.ShapeDtypeStruct(w.shape, F32) for w in ws] * 4,
        in_specs=[pl.BlockSpec(memory_space=pltpu.SMEM)] + [vmem] * (5 * n), out_specs=[vmem] * (4 * n),
        compiler_params=_cparams(),
    )(dev, *owns, *lands, *ws, *ms, *vs)
    return [res[k * n:(k + 1) * n] for k in range(4)]


def _adam_small(parts, ws, ms, vs, loss_parts, name):
    n = len(ws)

    def body(*refs):
        p_refs, w_refs, m_refs, v_refs = refs[:n], refs[n:2 * n], refs[2 * n:3 * n], refs[3 * n:4 * n]
        loss_ref, outs, loss_out = refs[4 * n], refs[4 * n + 1:-1], refs[-1]
        total = loss_ref[0]
        for dev in range(1, N_DEV):
            total = total + loss_ref[dev]
        loss_out[...] = total
        for i in range(n):
            g = p_refs[i][0]
            for dev in range(1, N_DEV):
                g = g + p_refs[i][dev]
            if g.shape[0] != w_refs[i].shape[0]:
                g = jnp.sum(g, axis=0, keepdims=True)
            delta, m_new, v_new = _adamw_math(w_refs[i][...], g, m_refs[i][...], v_refs[i][...])
            outs[i][...] = g
            outs[n + i][...] = delta
            outs[2 * n + i][...] = m_new
            outs[3 * n + i][...] = v_new

    out_shapes = [jax.ShapeDtypeStruct(w.shape, F32) for w in ws] * 4
    out_shapes.append(jax.ShapeDtypeStruct(loss_parts.shape[1:], F32))
    res = pl.pallas_call(body, name=name, out_shape=out_shapes, compiler_params=_cparams())(
        *parts, *ws, *ms, *vs, loss_parts)
    return [res[k * n:(k + 1) * n] for k in range(4)], res[-1]


def _rmsnorm_fwd(x, g, riders, name):
    s, d = x.shape
    ts = 512
    nt = s // ts
    nr = len(riders)

    def body(x_ref, g_ref, *rest):
        rider_in, (h_ref, ht_ref) = rest[:nr], rest[nr:nr + 2]
        rider_out, staged = rest[nr + 2:2 * nr + 2], rest[2 * nr + 2:3 * nr + 2]
        sems = rest[3 * nr + 2:]
        i = pl.program_id(0)
        gather = _Gather(staged, rider_out, *sems, relay=True)

        @pl.when(i == 0)
        def _():
            for src, dst in zip(rider_in, staged):
                dst[...] = src[...].astype(BF16)
            gather.issue()

        xv = x_ref[...]
        rstd = lax.rsqrt(jnp.mean(xv * xv, axis=-1, keepdims=True) + EPS)
        h = xv * rstd * g_ref[...]
        h_ref[...] = h.astype(BF16)
        ht_ref[...] = h.T.astype(BF16)

        @pl.when(i == nt - 1)
        def _():
            gather.finish()

    any_spec = pl.BlockSpec(memory_space=pl.ANY)
    res = pl.pallas_call(
        body, name=name,
        out_shape=[jax.ShapeDtypeStruct((s, d), BF16), jax.ShapeDtypeStruct((d, s), BF16)]
        + [jax.ShapeDtypeStruct((N_DEV,) + r.shape, BF16) for r in riders],
        grid=(nt,),
        in_specs=[pl.BlockSpec((ts, d), lambda i: (i, 0)), pl.BlockSpec((1, d), lambda i: (0, 0))]
        + [pl.BlockSpec(r.shape, lambda i: (0, 0)) for r in riders],
        out_specs=[pl.BlockSpec((ts, d), lambda i: (i, 0)), pl.BlockSpec((d, ts), lambda i: (0, i))]
        + [any_spec] * nr,
        scratch_shapes=[pltpu.VMEM(r.shape, BF16) for r in riders] + _gather_semaphores(nr),
        compiler_params=_cparams(("arbitrary",)),
    )(x, g, *riders)
    return res[0], res[1], res[2:]


def _inproj(h, wt, name):
    s, d = h.shape
    n = wt.shape[0]
    tn = 256
    n_qkv = QKV_COLS // tn

    def body(h_ref, w_ref, qkv_ref, rest_ref):
        j = pl.program_id(0)
        res = _dot_nt(h_ref[...], w_ref[...])

        @pl.when(j < n_qkv)
        def _():
            qkv_ref[...] = res.astype(BF16)

        @pl.when(j >= n_qkv)
        def _():
            rest_ref[...] = res

    return pl.pallas_call(
        body, name=name,
        out_shape=[jax.ShapeDtypeStruct((s, QKV_COLS), BF16), jax.ShapeDtypeStruct((s, n - QKV_COLS), F32)],
        grid=(n // tn,),
        in_specs=[pl.BlockSpec((s, d), lambda j: (0, 0)), pl.BlockSpec((tn, d), lambda j: (j, 0))],
        out_specs=[pl.BlockSpec((s, tn), lambda j: (0, jnp.minimum(j, n_qkv - 1))),
                   pl.BlockSpec((s, tn), lambda j: (0, jnp.maximum(j - n_qkv, 0)))],
        compiler_params=_cparams(("arbitrary",)),
    )(h, wt)


def _dw_in(ht, dproj, name):
    d, s = ht.shape
    n = dproj.shape[1]
    tn = 512

    def body(a_ref, b_ref, o_ref):
        o_ref[...] = _dot(a_ref[...], b_ref[...]).T.astype(o_ref.dtype)

    return pl.pallas_call(
        body, name=name, out_shape=jax.ShapeDtypeStruct((n, d), BF16), grid=(n // tn,),
        in_specs=[pl.BlockSpec((d, s), lambda j: (0, 0)), pl.BlockSpec((s, tn), lambda j: (0, j))],
        out_specs=pl.BlockSpec((tn, d), lambda j: (j, 0)),
        compiler_params=_cparams(("parallel",)),
    )(ht, dproj)


def _dh_and_grad_x(dproj, wt, x, g, dx2, cps, name):
    s, n = dproj.shape
    d = wt.shape[1]
    tm, tk = min(s, 1024), 512
    nk = n // tk
    nm = s // tm
    nc = len(cps)

    def body(a_ref, w_ref, x_ref, g_ref, dx2_ref, *rest):
        cp_refs, gx_ref, dg_ref = rest[:nc], rest[nc], rest[nc + 1]
        land_refs = rest[nc + 2:2 * nc + 2]
        acc_ref, send_sems, recv_sems = rest[2 * nc + 2:]
        i, k = pl.program_id(0), pl.program_id(1)

        @pl.when((i == 0) & (k == 0))
        def _():
            for cp in _chip_push_copies(cp_refs, land_refs, send_sems, recv_sems):
                cp.start()
            dg_ref[...] = jnp.zeros_like(dg_ref)

        @pl.when(k == 0)
        def _():
            acc_ref[...] = jnp.zeros_like(acc_ref)

        acc_ref[...] += _dot(a_ref[...], w_ref[...])

        @pl.when(k == nk - 1)
        def _():
            dh = acc_ref[...]
            xv = x_ref[...]
            rstd = lax.rsqrt(jnp.mean(xv * xv, axis=-1, keepdims=True) + EPS)
            xhat = xv * rstd
            dg_ref[...] += jnp.sum((dh * xhat).reshape(tm // 8, 8, d), axis=0)
            dxh = dh * g_ref[...]
            gx_ref[...] = dx2_ref[...] + rstd * (dxh - xhat * jnp.mean(dxh * xhat, axis=-1, keepdims=True))

        @pl.when((i == nm - 1) & (k == nk - 1))
        def _():
            for cp in _chip_push_copies(cp_refs, land_refs, send_sems, recv_sems):
                cp.wait()

    any_spec = pl.BlockSpec(memory_space=pl.ANY)
    res = pl.pallas_call(
        body, name=name,
        out_shape=[jax.ShapeDtypeStruct((s, d), F32), jax.ShapeDtypeStruct((8, d), F32)]
        + [jax.ShapeDtypeStruct((3,) + a.shape[1:], a.dtype) for a in cps],
        grid=(nm, nk),
        in_specs=[pl.BlockSpec((tm, tk), lambda i, k: (i, k)), pl.BlockSpec((tk, d), lambda i, k: (k, 0)),
                  pl.BlockSpec((tm, d), lambda i, k: (i, 0)), pl.BlockSpec((1, d), lambda i, k: (0, 0)),
                  pl.BlockSpec((tm, d), lambda i, k: (i, 0))] + [any_spec] * nc,
        out_specs=[pl.BlockSpec((tm, d), lambda i, k: (i, 0)), pl.BlockSpec((8, d), lambda i, k: (0, 0))]
        + [any_spec] * nc,
        scratch_shapes=[pltpu.VMEM((tm, d), F32), pltpu.SemaphoreType.DMA((nc, 3)), pltpu.SemaphoreType.DMA((nc, 3))],
        compiler_params=_cparams(("arbitrary", "arbitrary")),
    )(dproj, wt, x, g, dx2, *cps)
    return res[0], res[1], res[2:]


def _log_sigmoids(z):
    l1p = jnp.log(1.0 + jnp.exp(-jnp.abs(z)))
    ls = jnp.minimum(z, 0.0) - l1p
    return ls, ls - z


def _strict_lower_ones(n):
    row = lax.broadcasted_iota(jnp.int32, (n, n), 0)
    col = lax.broadcasted_iota(jnp.int32, (n, n), 1)
    return row, col, (row > col).astype(BF16)


def _attn_fwd(qkv, riders, name):
    s = qkv.shape[0]
    tb, tt = ATT_BLOCK, ATT_TILE
    nq = s // tb
    per_block = tb // tt
    assert per_block == 2 and s // tt <= LANES, "two query tiles per grid step; one lane of saved carry per key tile"

    nr = len(riders)

    def sweep_body(gather, q_ref, k_ref, v_ref, o_ref, rs_ref, extra_ref, acc_ref, r_ref, rsv_ref, rmax_ref):
        i = pl.program_id(1)
        lane = lax.broadcasted_iota(jnp.int32, (tt, LANES), 1)
        hmask = [lane < HEAD_DIM, lane >= HEAD_DIM]
        row, col, tri = _strict_lower_ones(tt)
        tri2 = jnp.concatenate([tri, tri], axis=0)
        below = col < row
        qrows = [slice(u * tt, (u + 1) * tt) for u in range(per_block)]
        qm = [[jnp.where(m, q_ref[qrows[u], :], jnp.zeros((), BF16)) * jnp.asarray(HEAD_DIM ** -0.5, BF16)
               for m in hmask] for u in range(per_block)]
        acc_ref[...] = jnp.zeros_like(acc_ref)
        r_ref[...] = jnp.zeros_like(r_ref)
        rsv_ref[...] = jnp.full_like(rsv_ref, R_UNREACHED)

        def sweep(tiles, chains):
            heads, nch = range(2), range(len(chains))
            kv = []
            for t in tiles:
                rows = pl.ds(pl.multiple_of(t * tt, tt), tt)
                kv.append((k_ref[rows, :], v_ref[rows, :]))
            z = [[_dot_nt(qm[u][h], kv[ti][0]) for h in heads] for u, ti, _ in chains]
            ls, lk = [], []
            for n in nch:
                pairs = [_log_sigmoids(z[n][h]) for h in heads]
                ls.append([p[0] for p in pairs])
                lk.append([jnp.where(below, p[1], 0.0) if chains[n][2] else p[1] for p in pairs])
            cur = {u: [r_ref[h, qrows[u], :] for h in heads] for u in sorted({c[0] for c in chains})}
            r = []
            for n, (u, _, _) in enumerate(chains):
                r.append(cur[u])
                cur[u] = [cur[u][h] + jnp.sum(lk[n][h], axis=1, keepdims=True) for h in heads]
            for u in cur:
                rmax_ref[u] = jnp.max(jnp.maximum(cur[u][0], cur[u][1]))
            suffix = [[_dot(jnp.concatenate(_split_hi_lo(lk[n][h]), axis=1), tri2) for h in heads] for n in nch]
            w = []
            for n in nch:
                w_n = [jnp.exp(ls[n][h] + suffix[n][h] + r[n][h]) for h in heads]
                if chains[n][2]:
                    w_n = [jnp.where(below, w_h, 0.0) for w_h in w_n]
                w.append([w_h.astype(BF16) for w_h in w_n])
            pv = [[_dot(w[n][h], kv[chains[n][1]][1]) for h in heads] for n in nch]
            for u in cur:
                mine = [n for n in nch if chains[n][0] == u]
                for h in heads:
                    acc_ref[h, qrows[u], :] += functools.reduce(lambda a, b: a + b, [pv[n][h] for n in mine])
                    for n in mine:
                        if not chains[n][2]:
                            t = tiles[chains[n][1]]
                            rsv_ref[h, qrows[u], :] = jnp.where(lane == t, r[n][h], rsv_ref[h, qrows[u], :])
                    r_ref[h, qrows[u], :] = cur[u][h]

        first = per_block * i

        @pl.when(i == 0)
        def _():
            sweep([0, 1], [(0, 0, True), (1, 1, True), (1, 0, False)])

        @pl.when(i > 0)
        def _():
            sweep([first - 1, first, first + 1], [(0, 1, True), (1, 2, True), (0, 0, False), (1, 1, False)])

        for u in range(per_block):
            n_left = first + u - 1

            def live(c, n_left=n_left):
                jj, rmax = c
                return (jj < n_left) & (rmax >= CARRY_FLOOR)

            def step(c, u=u, n_left=n_left):
                jj, _ = c
                sweep([n_left - 1 - jj], [(u, 0, False)])
                return jj + 1, rmax_ref[u]

            swept, _ = lax.while_loop(live, step, (jnp.int32(0), rmax_ref[u]))
            extra_ref[pl.program_id(0), first + u] = swept.astype(F32)

        lane_b = lax.broadcasted_iota(jnp.int32, (tb, LANES), 1)
        o_ref[...] = jnp.where(lane_b < HEAD_DIM, acc_ref[0], acc_ref[1])
        rs_ref[...] = rsv_ref[...]

        @pl.when((pl.program_id(0) == N_PAIRS - 1) & (i == nq - 1))
        def _():
            gather.finish()

    def body(q_ref, k_ref, v_ref, *rest):
        rider_in, (o_ref, rs_ref, extra_ref) = rest[:nr], rest[nr:nr + 3]
        rider_out = rest[nr + 3:2 * nr + 3]
        acc_ref, r_ref, rsv_ref, rmax_ref = rest[2 * nr + 3:2 * nr + 7]
        gather = _Gather(rider_in, rider_out, *rest[2 * nr + 7:])

        @pl.when((pl.program_id(0) == 0) & (pl.program_id(1) == 0))
        def _():
            gather.issue()

        sweep_body(gather, q_ref, k_ref, v_ref, o_ref, rs_ref, extra_ref, acc_ref, r_ref, rsv_ref, rmax_ref)

    any_spec = pl.BlockSpec(memory_space=pl.ANY)
    res = pl.pallas_call(
        body, name=name,
        out_shape=[jax.ShapeDtypeStruct((s, D_SB), F32), jax.ShapeDtypeStruct((N_HEADS, s, LANES), F32),
                   jax.ShapeDtypeStruct((N_PAIRS, s // tt), F32)] + _gather_out_shapes(riders),
        grid=(N_PAIRS, nq),
        in_specs=[pl.BlockSpec((tb, LANES), lambda p, i: (i, p)),
                  pl.BlockSpec((s, LANES), lambda p, i: (0, N_PAIRS + p)),
                  pl.BlockSpec((s, LANES), lambda p, i: (0, 2 * N_PAIRS + p))] + [any_spec] * nr,
        out_specs=[pl.BlockSpec((tb, LANES), lambda p, i: (i, p)),
                   pl.BlockSpec((2, tb, LANES), lambda p, i: (p, i, 0)),
                   pl.BlockSpec(memory_space=pltpu.SMEM)] + [any_spec] * nr,
        scratch_shapes=[pltpu.VMEM((2, tb, LANES), F32), pltpu.VMEM((2, tb, LANES), F32),
                        pltpu.VMEM((2, tb, LANES), F32), pltpu.SMEM((per_block,), F32)] + _gather_semaphores(nr),
        compiler_params=_cparams(("arbitrary", "arbitrary")),
    )(qkv, qkv, qkv, *riders)
    return res[0], res[1], res[2], res[3:]


def _attn_bwd(qkv, do, rs, extra, dproj, smalls, cps, name):
    s = qkv.shape[0]
    tb, tt = ATT_BLOCK, ATT_TILE
    nq = s // tb
    per_block = tb // tt
    scale = HEAD_DIM ** -0.5

    ns, nc = len(smalls), len(cps)

    def sweep_body(q_ref, k_ref, v_ref, do_ref, rs_ref, extra_ref, dproj_hbm, out_hbm, dq_acc, dk_acc, dv_acc,
                   dqi_ref, pc_ref, stage_ref, out_sems):
        del dproj_hbm
        pair = pl.program_id(0)
        lane = lax.broadcasted_iota(jnp.int32, (tt, LANES), 1)
        hmask = [lane < HEAD_DIM, lane >= HEAD_DIM]
        row, col, tri = _strict_lower_ones(tt)
        tri2 = jnp.concatenate([tri, tri], axis=0)
        triu = (row < col).astype(BF16)
        below = col < row
        zero = jnp.zeros((), BF16)
        qrows = [slice(u * tt, (u + 1) * tt) for u in range(per_block)]
        dq_acc[...] = jnp.zeros_like(dq_acc)
        dk_acc[...] = jnp.zeros_like(dk_acc)
        dv_acc[...] = jnp.zeros_like(dv_acc)

        def qblock(i, carry):
            block_rows = pl.ds(pl.multiple_of(i * tb, tb), tb)
            q2 = q_ref[block_rows, :]
            do2 = do_ref[block_rows, :]
            qm = [[jnp.where(m, q2[qrows[u]], zero) * jnp.asarray(scale, BF16) for m in hmask]
                  for u in range(per_block)]
            dom = [[jnp.where(m, do2[qrows[u]], zero) for m in hmask] for u in range(per_block)]
            rs_i = [[rs_ref[h, pl.ds(pl.multiple_of(i * tb + u * tt, tt), tt), :] for h in range(2)]
                    for u in range(per_block)]
            dqi_ref[...] = jnp.zeros_like(dqi_ref)
            pc_ref[...] = jnp.zeros_like(pc_ref)
            first = per_block * i

            def sweep(tiles, chains):
                heads, nch = range(2), range(len(chains))
                krows = [pl.ds(pl.multiple_of(t * tt, tt), tt) for t in tiles]
                k2 = [k_ref[rows, :] for rows in krows]
                v2 = [v_ref[rows, :] for rows in krows]
                z = [[_dot_nt(qm[u][h], k2[ti]) for h in heads] for u, ti, _ in chains]
                dw = [[_dot_nt(dom[u][h], v2[ti]) for h in heads] for u, ti, _ in chains]
                ls, lk = [], []
                for n in nch:
                    pairs = [_log_sigmoids(z[n][h]) for h in heads]
                    ls.append([p[0] for p in pairs])
                    lk.append([jnp.where(below, p[1], 0.0) if chains[n][2] else p[1] for p in pairs])
                suffix = [[_dot(jnp.concatenate(_split_hi_lo(lk[n][h]), axis=1), tri2) for h in heads] for n in nch]
                w, g = [], []
                for n, (u, ti, diag) in enumerate(chains):
                    w_n = []
                    for h in heads:
                        logw = ls[n][h] + suffix[n][h]
                        if diag:
                            w_n.append(jnp.where(below, jnp.exp(logw), 0.0))
                        else:
                            carry_in = jnp.sum(jnp.where(lane == tiles[ti], rs_i[u][h], 0.0), axis=1, keepdims=True)
                            w_n.append(jnp.exp(logw + carry_in))
                    w.append([w_h.astype(BF16) for w_h in w_n])
                    g.append([w_n[h] * dw[n][h] for h in heads])
                prefix = [[_dot(g[n][h].astype(BF16), triu) for h in heads] for n in nch]
                cur = {u: [pc_ref[h, qrows[u], :] for h in heads] for u in sorted({c[0] for c in chains})}
                pc = []
                for n, (u, _, _) in enumerate(chains):
                    pc.append(cur[u])
                    cur[u] = [cur[u][h] + jnp.sum(g[n][h], axis=1, keepdims=True) for h in heads]
                dzb = []
                for n in nch:
                    dz_n = []
                    for h in heads:
                        sig = jnp.exp(ls[n][h])
                        dz = g[n][h] - sig * (g[n][h] + prefix[n][h] + pc[n][h])
                        if chains[n][2]:
                            dz = jnp.where(below, dz, 0.0)
                        dz_n.append(dz.astype(BF16))
                    dzb.append(dz_n)
                for u in cur:
                    for h in heads:
                        pc_ref[h, qrows[u], :] = cur[u][h]
                dq = [[_dot(dzb[n][h], jnp.where(hmask[h], k2[chains[n][1]], zero)) for h in heads] for n in nch]
                dk = [[_dot_tn(dzb[n][h], qm[chains[n][0]][h]) for h in heads] for n in nch]
                dv = [[_dot_tn(w[n][h], dom[chains[n][0]][h]) for h in heads] for n in nch]
                add = lambda a, b: a + b
                for u in cur:
                    dqi_ref[qrows[u], :] += functools.reduce(
                        add, [dq[n][h] for n in nch if chains[n][0] == u for h in heads])
                for ti in range(len(tiles)):
                    mine = [n for n in nch if chains[n][1] == ti]
                    dk_acc[krows[ti], :] += functools.reduce(add, [dk[n][h] for n in mine for h in heads])
                    dv_acc[krows[ti], :] += functools.reduce(add, [dv[n][h] for n in mine for h in heads])

            for u in range(per_block):
                n_left = first + u - 1
                n_extra = jnp.clip(extra_ref[pair, first + u].astype(jnp.int32), 0, jnp.maximum(n_left, 0))

                def step(t, c, u=u):
                    sweep([t], [(u, 0, False)])
                    return c

                lax.fori_loop(n_left - n_extra, n_left, step, 0)

            @pl.when(i == 0)
            def _():
                sweep([0, 1], [(0, 0, True), (1, 0, False), (1, 1, True)])

            @pl.when(i > 0)
            def _():
                sweep([first - 1, first, first + 1], [(0, 0, False), (1, 1, False), (0, 1, True), (1, 2, True)])

            dq_acc[block_rows, :] += dqi_ref[...] * scale
            return carry

        lax.fori_loop(0, nq, qblock, 0)
        copies = []
        for t, acc in enumerate((dq_acc, dk_acc, dv_acc)):
            stage_ref[t] = acc[...].astype(BF16)
            col0 = pl.multiple_of(t * D_SB + pair * LANES, LANES)
            copies.append(pltpu.make_async_copy(stage_ref.at[t], out_hbm.at[:, pl.ds(col0, LANES)], out_sems.at[t]))
        for cp in copies:
            cp.start()
        for cp in copies:
            cp.wait()

    def body(q_ref, k_ref, v_ref, do_ref, rs_ref, extra_ref, dproj_hbm, *rest):
        small_in, cp_in = rest[:ns], rest[ns:ns + nc]
        out_hbm = rest[ns + nc]
        small_out, land_out = rest[ns + nc + 1:2 * ns + nc + 1], rest[2 * ns + nc + 1:2 * (ns + nc) + 1]
        scratch = rest[2 * (ns + nc) + 1:]
        gather = _Gather(small_in, small_out, *scratch[7:10])
        pair = pl.program_id(0)

        @pl.when(pair == 0)
        def _():
            gather.issue()
            for cp in _direct_push_copies(cp_in, land_out, *scratch[10:]):
                cp.start()

        sweep_body(q_ref, k_ref, v_ref, do_ref, rs_ref, extra_ref, dproj_hbm, out_hbm, *scratch[:7])

        @pl.when(pair == N_PAIRS - 1)
        def _():
            gather.finish()
            for cp in _direct_push_copies(cp_in, land_out, *scratch[10:]):
                cp.wait()

    any_spec = pl.BlockSpec(memory_space=pl.ANY)
    res = pl.pallas_call(
        body, name=name,
        out_shape=[jax.ShapeDtypeStruct(dproj.shape, BF16)] + _gather_out_shapes(smalls)
        + [jax.ShapeDtypeStruct((N_DEV - 1,) + a.shape[1:], a.dtype) for a in cps],
        grid=(N_PAIRS,),
        in_specs=[pl.BlockSpec((s, LANES), lambda p: (0, p)),
                  pl.BlockSpec((s, LANES), lambda p: (0, N_PAIRS + p)),
                  pl.BlockSpec((s, LANES), lambda p: (0, 2 * N_PAIRS + p)),
                  pl.BlockSpec((s, LANES), lambda p: (0, p)),
                  pl.BlockSpec((2, s, LANES), lambda p: (p, 0, 0)),
                  pl.BlockSpec(memory_space=pltpu.SMEM),
                  any_spec] + [any_spec] * (ns + nc),
        out_specs=[any_spec] * (1 + ns + nc),
        scratch_shapes=[pltpu.VMEM((s, LANES), F32), pltpu.VMEM((s, LANES), F32), pltpu.VMEM((s, LANES), F32),
                        pltpu.VMEM((tb, LANES), F32), pltpu.VMEM((2, tb, LANES), F32),
                        pltpu.VMEM((3, s, LANES), BF16), pltpu.SemaphoreType.DMA((3,))]
        + _gather_semaphores(ns)
        + [pltpu.SemaphoreType.DMA((nc, N_DEV - 1)), pltpu.SemaphoreType.DMA((nc, N_DEV - 1))],
        input_output_aliases={6: 0},
        compiler_params=_cparams(("arbitrary",)),
    )(qkv, qkv, qkv, do, rs, extra, dproj, *smalls, *cps)
    return res[0], res[1:1 + ns], res[1 + ns:]


def _mid(o, rest, x, tgt, wua, wub, wout, fg, lng, lnb, wsp, bfull, gavg, name):
    s, d = x.shape
    ts = 256
    nt = s // ts
    nchunk = ts // SGU_CHUNK
    n_rest = rest.shape[1]

    def body(o_ref, rest_ref, x_ref, t_ref, wua_ref, wub_ref, wout_ref, fg_ref, lng_ref, lnb_ref, wsp_ref, bfull_ref,
             gavg_ref, loss_ref, dx2_ref, do_ref, dproj_ref, dwua_ref, dwub_ref, dwout_ref, dfg_ref, dlng_ref,
             dlnb_ref, dwsp_ref, dbfull_ref):
        step = pl.program_id(0)

        @pl.when(step == 0)
        def _():
            for ref in (loss_ref, dwua_ref, dwub_ref, dwout_ref, dfg_ref, dlng_ref, dlnb_ref, dwsp_ref, dbfull_ref):
                ref[...] = jnp.zeros_like(ref)

        gavg = gavg_ref[...]

        def gmean(a):
            return _dot(a.astype(BF16), gavg)

        def colsum8(a):
            return jnp.sum(a.reshape(ts // 8, 8, a.shape[1]), axis=0)

        z_a = rest_ref[:, 0:512]
        u_b = rest_ref[:, 512:1024]
        v_b = rest_ref[:, 1024:1536]
        z_b = rest_ref[:, 1536:2048]
        g_a = rest_ref[:, 2048:2048 + d]
        g_b = rest_ref[:, 2048 + d:2048 + 2 * d]
        ov = o_ref[...]
        sa = _sigmoid(z_a)
        silu_a = z_a * sa
        y_a = ov * silu_a
        ug, dug_du = _gelu_and_grad(u_b)
        vg, dvg_dv = _gelu_and_grad(v_b)
        mu = gmean(vg)
        cen = vg - mu
        rstd_g = lax.rsqrt(gmean(cen * cen) + EPS)
        vhat = cen * rstd_g
        vn = vhat * lng_ref[...] + lnb_ref[...]
        vnb = vn.astype(BF16)

        t_idx = lax.broadcasted_iota(jnp.int32, (SGU_CHUNK, SGU_CHUNK), 0)
        s_idx = lax.broadcasted_iota(jnp.int32, (SGU_CHUNK, SGU_CHUNK), 1)
        causal = (s_idx // CHUNK) <= (t_idx // CHUNK)
        wm = [jnp.where(causal, wsp_ref[g], 0.0) for g in range(N_GROUPS)]
        wmb = [w.astype(BF16) for w in wm]
        wmtb = [w.T.astype(BF16) for w in wm]
        lane = lax.broadcasted_iota(jnp.int32, (SGU_CHUNK, LANES), 1)
        first = lane < GROUP_DIM
        bfull = bfull_ref[...]

        mixed_rows = []
        for n in range(nchunk):
            r0, r1 = n * SGU_CHUNK, (n + 1) * SGU_CHUNK
            pieces = []
            for p in range(N_GROUPS // 2):
                blk = vnb[r0:r1, p * LANES:(p + 1) * LANES]
                pieces.append(jnp.where(first, _dot(wmb[2 * p], blk), _dot(wmb[2 * p + 1], blk)))
            mixed_rows.append(jnp.concatenate(pieces, axis=1) + bfull)
        mixed = jnp.concatenate(mixed_rows, axis=0)
        sg = ug * mixed
        sb = _sigmoid(z_b)
        silu_b = z_b * sb
        y_b = sg * silu_b
        y_ab = y_a.astype(BF16)
        y_bb = y_b.astype(BF16)
        p_a = _dot(y_ab, wua_ref[...])
        p_b = _dot(y_bb, wub_ref[...])
        ga_s = _sigmoid(g_a)
        gb_s = _sigmoid(g_b)
        merged_b = (ga_s * p_a + gb_s * p_b).astype(BF16)
        x2 = x_ref[...] + _dot(merged_b, wout_ref[...])
        rstd = lax.rsqrt(jnp.mean(x2 * x2, axis=-1, keepdims=True) + EPS)
        xhat = x2 * rstd
        fg_v = fg_ref[...]
        diff = xhat * fg_v - t_ref[...]
        loss_ref[...] += 0.5 * jnp.sum(jnp.sum(diff * diff, axis=-1, keepdims=True) * (1.0 / d))

        dy = diff * (1.0 / d)
        dfg_ref[...] += colsum8(dy * xhat)
        dxh = dy * fg_v
        dx2 = rstd * (dxh - xhat * jnp.mean(dxh * xhat, axis=-1, keepdims=True))
        dx2_ref[...] = dx2
        dx2b = dx2.astype(BF16)
        dwout_ref[...] += _dot_tn(merged_b, dx2b)
        dmerged = _dot_nt(dx2b, wout_ref[...])
        dp_a = dmerged * ga_s
        dp_b = dmerged * gb_s
        dproj_ref[:, QKV_COLS + 2048:QKV_COLS + 2048 + d] = (dmerged * p_a * (ga_s * (1.0 - ga_s))).astype(BF16)
        dproj_ref[:, QKV_COLS + 2048 + d:QKV_COLS + 2048 + 2 * d] = (dmerged * p_b * (gb_s * (1.0 - gb_s))).astype(BF16)
        dp_ab = dp_a.astype(BF16)
        dp_bb = dp_b.astype(BF16)
        dwua_ref[...] += _dot_tn(y_ab, dp_ab)
        dwub_ref[...] += _dot_tn(y_bb, dp_bb)
        dy_a = _dot_nt(dp_ab, wua_ref[...])
        dy_b = _dot_nt(dp_bb, wub_ref[...])
        do_ref[...] = (dy_a * silu_a).astype(BF16)
        dproj_ref[:, QKV_COLS:QKV_COLS + 512] = (dy_a * ov * (sa * (1.0 + z_a * (1.0 - sa)))).astype(BF16)
        dsg = dy_b * silu_b
        dproj_ref[:, QKV_COLS + 1536:QKV_COLS + 2048] = (dy_b * sg * (sb * (1.0 + z_b * (1.0 - sb)))).astype(BF16)
        dproj_ref[:, QKV_COLS + 512:QKV_COLS + 1024] = (dsg * mixed * dug_du).astype(BF16)
        dmixed = dsg * ug
        dmb = dmixed.astype(BF16)
        zero = jnp.zeros((), BF16)
        dvn_rows = []
        db = jnp.zeros((SGU_CHUNK, D_SGU), F32)
        for n in range(nchunk):
            r0, r1 = n * SGU_CHUNK, (n + 1) * SGU_CHUNK
            db = db + dmixed[r0:r1, :]
            pieces = []
            for p in range(N_GROUPS // 2):
                cols = slice(p * LANES, (p + 1) * LANES)
                dm_blk = dmb[r0:r1, cols]
                vn_blk = vnb[r0:r1, cols]
                dwsp_ref[2 * p] += _dot_nt(jnp.where(first, dm_blk, zero), vn_blk)
                dwsp_ref[2 * p + 1] += _dot_nt(jnp.where(first, zero, dm_blk), vn_blk)
                pieces.append(jnp.where(first, _dot(wmtb[2 * p], dm_blk), _dot(wmtb[2 * p + 1], dm_blk)))
            dvn_rows.append(jnp.concatenate(pieces, axis=1))
        dbfull_ref[...] += db
        dvn = jnp.concatenate(dvn_rows, axis=0)
        dlng_ref[...] += colsum8(dvn * vhat)
        dlnb_ref[...] += colsum8(dvn)
        dvhat = dvn * lng_ref[...]
        dcen = rstd_g * (dvhat - gmean(dvhat) - vhat * gmean(dvhat * vhat))
        dproj_ref[:, QKV_COLS + 1024:QKV_COLS + 1536] = (dcen * dvg_dv).astype(BF16)

        @pl.when(step == nt - 1)
        def _():
            for g in range(N_GROUPS):
                dwsp_ref[g] = jnp.where(causal, dwsp_ref[g], 0.0)

    def tile(cols):
        return pl.BlockSpec((ts, cols), lambda i: (i, 0))

    def whole(shape):
        return pl.BlockSpec(shape, lambda i: (0,) * len(shape))

    out_shapes = [
        jax.ShapeDtypeStruct((8, LANES), F32),
        jax.ShapeDtypeStruct((s, d), F32),
        jax.ShapeDtypeStruct((s, D_SB), BF16),
        jax.ShapeDtypeStruct((s, QKV_COLS + n_rest), BF16),
        jax.ShapeDtypeStruct((D_SB, d), F32),
        jax.ShapeDtypeStruct((D_SGU, d), F32),
        jax.ShapeDtypeStruct((d, d), F32),
        jax.ShapeDtypeStruct((8, d), F32),
        jax.ShapeDtypeStruct((8, D_SGU), F32),
        jax.ShapeDtypeStruct((8, D_SGU), F32),
        jax.ShapeDtypeStruct((N_GROUPS, SGU_CHUNK, SGU_CHUNK), F32),
        jax.ShapeDtypeStruct((SGU_CHUNK, D_SGU), F32),
    ]
    out_specs = [whole((8, LANES)), tile(d), tile(D_SB), tile(QKV_COLS + n_rest), whole((D_SB, d)), whole((D_SGU, d)),
                 whole((d, d)), whole((8, d)), whole((8, D_SGU)), whole((8, D_SGU)),
                 whole((N_GROUPS, SGU_CHUNK, SGU_CHUNK)), whole((SGU_CHUNK, D_SGU))]
    in_specs = [tile(D_SB), tile(n_rest), tile(d), tile(d), whole((D_SB, d)), whole((D_SGU, d)), whole((d, d)),
                whole((1, d)), whole((1, D_SGU)), whole((1, D_SGU)), whole((N_GROUPS, SGU_CHUNK, SGU_CHUNK)),
                whole((SGU_CHUNK, D_SGU)), whole((D_SGU, D_SGU))]
    return pl.pallas_call(
        body, name=name, out_shape=out_shapes, grid=(nt,), in_specs=in_specs, out_specs=out_specs,
        compiler_params=_cparams(("arbitrary",)),
    )(o, rest, x, tgt, wua, wub, wout, fg, lng, lnb, wsp, bfull, gavg)


def _small_reduce(dfg8, dlng8, dlnb8, dbfull, name):
    d = dfg8.shape[1]

    def body(dfg_ref, dlng_ref, dlnb_ref, dbfull_ref, fg_out, lng_out, lnb_out, b_out):
        row = lax.broadcasted_iota(jnp.int32, (D_SGU, LANES), 0)
        col = lax.broadcasted_iota(jnp.int32, (D_SGU, LANES), 1)

        def select_sum(a, sel):
            hi, lo = _split_hi_lo(a)
            lo2 = (a - hi.astype(F32) - lo.astype(F32)).astype(BF16)
            return _dot(hi, sel) + _dot(lo, sel) + _dot(lo2, sel)

        def by_group(partials):
            v = jnp.sum(partials, axis=0, keepdims=True)
            g_of_lane = lax.broadcasted_iota(jnp.int32, (N_GROUPS, D_SGU), 1) // GROUP_DIM
            g_of_row = lax.broadcasted_iota(jnp.int32, (N_GROUPS, D_SGU), 0)
            spread = jnp.where(g_of_lane == g_of_row, jnp.broadcast_to(v, (N_GROUPS, D_SGU)), 0.0)
            return select_sum(spread, (row % GROUP_DIM == col).astype(BF16))[:, 0:GROUP_DIM]

        fg_out[...] = jnp.sum(dfg_ref[...], axis=0, keepdims=True)
        lng_out[...] = by_group(dlng_ref[...])
        lnb_out[...] = by_group(dlnb_ref[...])
        by_pos = select_sum(dbfull_ref[...], (row // GROUP_DIM == col).astype(BF16))
        b_out[...] = by_pos.T[0:N_GROUPS, :]

    return pl.pallas_call(
        body, name=name,
        out_shape=[jax.ShapeDtypeStruct((1, d), F32), jax.ShapeDtypeStruct((N_GROUPS, GROUP_DIM), F32),
                   jax.ShapeDtypeStruct((N_GROUPS, GROUP_DIM), F32), jax.ShapeDtypeStruct((N_GROUPS, SGU_CHUNK), F32)],
        compiler_params=_cparams(),
    )(dfg8, dlng8, dlnb8, dbfull)


def _block_major_cols(w):
    r, n = w.shape
    return jnp.transpose(w.reshape(r, N_DEV, n // N_DEV), (1, 0, 2))


def _from_block_major_cols(w):
    nb, r, c = w.shape
    return jnp.transpose(w, (1, 0, 2)).reshape(r, nb * c)


def kernel(x, norm_g, w_in, sgu_ln_g, sgu_ln_b, w_spatial, b_spatial, w_up_a, w_up_b, w_out, final_norm_g, loss_target, m_norm_g, m_w_in, m_sgu_ln_g, m_sgu_ln_b, m_w_spatial, m_b_spatial, m_w_up_a, m_w_up_b, m_w_out, m_final_norm_g, v_norm_g, v_w_in, v_sgu_ln_g, v_sgu_ln_b, v_w_spatial, v_b_spatial, v_w_up_a, v_w_up_b, v_w_out, v_final_norm_g):
    s, d = x.shape[1], x.shape[2]
    xs = x[0]
    tgt = loss_target[0]
    cx, cy, cc = _coords()
    core = jnp.reshape(cc, (1,)).astype(jnp.int32)
    chip = jnp.reshape(2 * cx + cy, (1,)).astype(jnp.int32)

    h, ht, (g_win,) = _rmsnorm_fwd(xs, norm_g, [jnp.transpose(w_in[0])], "norm")
    d_in = N_DEV * w_in.shape[2]
    w_full = g_win.reshape(d_in, d)
    qkv, rest = _inproj(h, w_full, "inproj")
    o, rs, extra, (g_wua, g_wub, g_wout) = _attn_fwd(
        qkv, [w_up_a[0].astype(BF16), w_up_b[0].astype(BF16), w_out[0].astype(BF16)], "attn_fwd")
    wua_full = _from_block_major_cols(g_wua)
    wub_full = _from_block_major_cols(g_wub)
    wout_full = g_wout.reshape(d, d)

    lng = sgu_ln_g.reshape(1, D_SGU)
    lnb = sgu_ln_b.reshape(1, D_SGU)
    bfull = jnp.repeat(jnp.transpose(b_spatial[0]), GROUP_DIM, axis=1)
    grp = jnp.arange(D_SGU) // GROUP_DIM
    gavg = jnp.where(grp[:, None] == grp[None, :], 1.0 / GROUP_DIM, 0.0).astype(BF16)
    (loss_b, dx2, do, dproj, dwua, dwub, dwout, dfg8, dlng8, dlnb8, dwsp, dbfull) = _mid(
        o, rest, xs, tgt, wua_full, wub_full, wout_full, final_norm_g.reshape(1, d), lng, lnb, w_spatial[0], bfull,
        gavg, "mid")

    own_up = [_block_major_cols(dwua.astype(BF16)), _block_major_cols(dwub.astype(BF16)),
              dwout.astype(BF16).reshape(N_DEV, d // N_DEV, d)]
    dfg, dlng, dlnb, db = _small_reduce(dfg8, dlng8, dlnb8, dbfull, "small_reduce")
    wsp_rows = N_GROUPS * SGU_CHUNK
    dproj, small_parts, land_up = _attn_bwd(
        qkv, do, rs, extra, dproj, [dfg, dlng, dlnb, db, dwsp.reshape(wsp_rows, SGU_CHUNK)], own_up, "attn_bwd")
    dwin = _dw_in(ht, dproj, "dwin")

    own_in = [dwin.reshape(4, 2, d_in // N_DEV, d)]
    cps_in = list(_chip_partial_sums(own_in, list(_push_sibling(own_in, "rs_sibling_in")), core, "cpsum_in"))
    grad_x, dng8, land_in = _dh_and_grad_x(dproj, w_full, xs, norm_g, dx2, cps_in, "dh")

    ng_parts, loss_parts = _allgather([dng8, loss_b], "ag_tail")

    def small_layouts(ng, fg, lg, lb, bs, ws):
        return [ng.reshape(1, d), fg.reshape(1, d), lg.reshape(N_GROUPS, GROUP_DIM), lb.reshape(N_GROUPS, GROUP_DIM),
                bs.reshape(N_GROUPS, SGU_CHUNK), ws.reshape(wsp_rows, SGU_CHUNK)]

    sm, loss_sum = _adam_small(
        [ng_parts] + list(small_parts),
        small_layouts(norm_g, final_norm_g, sgu_ln_g, sgu_ln_b, b_spatial, w_spatial),
        small_layouts(m_norm_g, m_final_norm_g, m_sgu_ln_g, m_sgu_ln_b, m_b_spatial, m_w_spatial),
        small_layouts(v_norm_g, v_final_norm_g, v_sgu_ln_g, v_sgu_ln_b, v_b_spatial, v_w_spatial),
        loss_parts, "adam_small")
    small_shapes = [norm_g.shape, final_norm_g.shape, sgu_ln_g.shape, sgu_ln_b.shape, b_spatial.shape, w_spatial.shape]
    sm = [[a.reshape(shp) for a, shp in zip(kind, small_shapes)] for kind in sm]

    res = _adam_shard(cps_in, list(land_in), chip, *[jnp.transpose(a[0]) for a in (w_in, m_w_in, v_w_in)], "adam0")
    big = [[jnp.transpose(r)[None] for r in res]]
    dev = jnp.reshape(_dev_index(cx, cy, cc), (1,)).astype(jnp.int32)
    res = _adam_shards_whole(own_up, list(land_up), dev, [w_up_a[0], w_up_b[0], w_out[0]],
                             [m_w_up_a[0], m_w_up_b[0], m_w_out[0]], [v_w_up_a[0], v_w_up_b[0], v_w_out[0]], "adam_up")
    for i in range(3):
        big.append([res[kd][i][None] for kd in range(4)])

    loss = loss_sum[0, 0]

    def per_kind(kd):
        return [sm[kd][0], big[0][kd], sm[kd][2], sm[kd][3], sm[kd][5], sm[kd][4], big[1][kd], big[2][kd], big[3][kd],
                sm[kd][1]]

    return (loss, grad_x[None], *per_kind(0), *per_kind(1), *per_kind(2), *per_kind(3))
```

```python
import functools
import math

import jax
import jax.numpy as jnp
from jax import lax
from jax.experimental import pallas as pl
from jax.experimental.pallas import tpu as pltpu

F32 = jnp.float32
BF16 = jnp.bfloat16
MESH = pl.DeviceIdType.MESH

N_DEV = 8
N_HEADS = 8
HEAD_DIM = 64
D_SB = N_HEADS * HEAD_DIM
N_GROUPS = 8
GROUP_DIM = 64
D_SGU = N_GROUPS * GROUP_DIM
SGU_CHUNK = 128
CHUNK = 64
EPS = 1e-6
LANES = 128
N_PAIRS = N_HEADS // 2
QKV_COLS = 3 * D_SB
ATT_BLOCK = 256
ATT_TILE = 128
CARRY_FLOOR = -90.0
R_UNREACHED = -1e30

ADAM_LR = 0.001
ADAM_B1 = 0.9
ADAM_B2 = 0.999
ADAM_EPS = 1e-08
ADAM_WD = 0.01
ADAM_STEP = 10

VMEM_LIMIT = 56 * 1024 * 1024


def _cparams(sem=None, vmem=VMEM_LIMIT):
    return pltpu.CompilerParams(dimension_semantics=sem, vmem_limit_bytes=vmem)


def _dot(a, b):
    return jnp.dot(a, b, preferred_element_type=F32)


def _dot_nt(a, b):
    return lax.dot_general(a, b, (((1,), (1,)), ((), ())), preferred_element_type=F32)


def _dot_tn(a, b):
    return lax.dot_general(a, b, (((0,), (0,)), ((), ())), preferred_element_type=F32)


def _split_hi_lo(a):
    hi = a.astype(BF16)
    lo = (a - hi.astype(F32)).astype(BF16)
    return hi, lo


def _sigmoid(x):
    return 1.0 / (1.0 + jnp.exp(-x))


_GELU_C = math.sqrt(2.0 / math.pi)


def _gelu_and_grad(x):
    x2 = x * x
    inner = _GELU_C * (x + 0.044715 * (x2 * x))
    t = jnp.tanh(inner)
    cdf = 0.5 * (1.0 + t)
    g = x * cdf
    dg = cdf + x * (0.5 * (1.0 - t * t)) * (_GELU_C * (1.0 + 3.0 * 0.044715 * x2))
    return g, dg


def _coords():
    return lax.axis_index("x"), lax.axis_index("y"), lax.axis_index("c")


def _dev_index(px, py, pc):
    return 4 * px + 2 * py + pc


def _allgather(blocks, name):
    n = len(blocks)

    def body(*refs):
        gather = _Gather(refs[:n], refs[n:2 * n], *refs[2 * n:])
        gather.issue()
        gather.finish()

    any_spec = pl.BlockSpec(memory_space=pl.ANY)
    return pl.pallas_call(
        body, name=name,
        out_shape=_gather_out_shapes(blocks),
        in_specs=[any_spec] * n, out_specs=[any_spec] * n,
        scratch_shapes=_gather_semaphores(n),
    )(*blocks)


def _gather_out_shapes(blocks):
    return [jax.ShapeDtypeStruct((N_DEV,) + b.shape, b.dtype) for b in blocks]


def _gather_semaphores(n):
    return [pltpu.SemaphoreType.DMA((n, 7)), pltpu.SemaphoreType.DMA((n, 7)), pltpu.SemaphoreType.DMA((n,))]


class _Gather:
    def __init__(self, ins, outs, send_sems, recv_sems, local_sems, relay=False):
        self.ins, self.outs = ins, outs
        self.send_sems, self.recv_sems, self.local_sems = send_sems, recv_sems, local_sems
        self.n = len(ins)
        self.relay = relay
        x, y, c = _coords()
        self.c = c
        self.me, self.sibling = (x, y, c), (x, y, 1 - c)
        self.chips = [(1 - x, y), (x, 1 - y), (1 - x, 1 - y)]

    def _copy(self, a, k, block, to, src=None):
        dst = self.outs[a].at[_dev_index(*block)]
        return pltpu.make_async_remote_copy(
            src_ref=dst if src is None else src, dst_ref=dst,
            send_sem=self.send_sems.at[a, k], recv_sem=self.recv_sems.at[a, k],
            device_id=to, device_id_type=MESH)

    def _mine(self):
        return [pltpu.make_async_copy(self.ins[a], self.outs[a].at[_dev_index(*self.me)], self.local_sems.at[a])
                for a in range(self.n)]

    def _first(self):
        first = []
        direct = self.chips[:2] if self.relay else self.chips
        for a in range(self.n):
            first.append(self._copy(a, 0, self.me, self.sibling, src=self.ins[a]))
            first += [self._copy(a, 1 + j, self.me, (*chip, self.c), src=self.ins[a])
                      for j, chip in enumerate(direct)]
        return first

    def issue(self):
        for cp in self._mine() + self._first():
            cp.start()

    def _pass_on(self, a, j):
        chip = self.chips[j]
        self._copy(a, 1 + j, (*chip, self.c), self.me).wait_recv()
        self._copy(a, 4 + j, (*chip, self.c), self.sibling).start()

    def finish(self):
        c = self.c
        if self.relay:
            for core in range(2):
                @pl.when(c == core)
                def _(core=core):
                    j_src, j_dst = core, 1 - core
                    for a in range(self.n):
                        self._pass_on(a, j_src)
                        self._copy(a, 3, (*self.chips[j_src], c), (*self.chips[j_dst], c)).start()
                    for a in range(self.n):
                        self._pass_on(a, j_dst)
                    for a in range(self.n):
                        self._pass_on(a, 2)
        else:
            for j in range(3):
                for a in range(self.n):
                    self._pass_on(a, j)
        for a in range(self.n):
            self._copy(a, 0, self.sibling, self.me).wait_recv()
            for j, chip in enumerate(self.chips):
                self._copy(a, 4 + j, (*chip, 1 - c), self.me).wait_recv()
        for a in range(self.n):
            for k in range(7):
                self._copy(a, k, self.me, self.sibling).wait_send()
        for cp in self._mine():
            cp.wait()


def _push_sibling(arrs, name):
    n = len(arrs)

    def body(*refs):
        ins, outs = refs[:n], refs[n:2 * n]
        send_sems, recv_sems = refs[2 * n:]
        x, y, c = _coords()
        sibling = (x, y, 1 - c)
        copies = []
        for a in range(n):
            for k in range(4):
                copies.append(pltpu.make_async_remote_copy(
                    src_ref=ins[a].at[k, 1 - c], dst_ref=outs[a].at[k],
                    send_sem=send_sems.at[a, k], recv_sem=recv_sems.at[a, k],
                    device_id=sibling, device_id_type=MESH))
        for cp in copies:
            cp.start()
        for cp in copies:
            cp.wait()

    any_spec = pl.BlockSpec(memory_space=pl.ANY)
    return pl.pallas_call(
        body, name=name,
        out_shape=[jax.ShapeDtypeStruct((4,) + a.shape[2:], a.dtype) for a in arrs],
        in_specs=[any_spec] * n, out_specs=[any_spec] * n,
        scratch_shapes=[pltpu.SemaphoreType.DMA((n, 4)), pltpu.SemaphoreType.DMA((n, 4))],
    )(*arrs)


def _chip_push_copies(ins, outs, send_sems, recv_sems):
    x, y, c = _coords()
    chips = [(1 - x, y), (x, 1 - y), (1 - x, 1 - y)]
    return [pltpu.make_async_remote_copy(
        src_ref=ins[a].at[2 * px + py], dst_ref=outs[a].at[r],
        send_sem=send_sems.at[a, r], recv_sem=recv_sems.at[a, r],
        device_id=(px, py, c), device_id_type=MESH)
        for a in range(len(ins)) for r, (px, py) in enumerate(chips)]


def _direct_push_copies(ins, outs, send_sems, recv_sems):
    x, y, c = _coords()
    copies = []
    for a in range(len(ins)):
        for r in range(1, N_DEV):
            px = 1 - x if r & 4 else x
            py = 1 - y if r & 2 else y
            pc = 1 - c if r & 1 else c
            copies.append(pltpu.make_async_remote_copy(
                src_ref=ins[a].at[_dev_index(px, py, pc)], dst_ref=outs[a].at[r - 1],
                send_sem=send_sems.at[a, r - 1], recv_sem=recv_sems.at[a, r - 1],
                device_id=(px, py, pc), device_id_type=MESH))
    return copies


def _chip_partial_sums(owns, lands, core, name):
    n = len(owns)
    blocks = [a.shape[2:] for a in owns]

    def body(core_ref, *refs):
        del core_ref
        for own_ref, land_ref, out_ref in zip(refs[:n], refs[n:2 * n], refs[2 * n:]):
            out_ref[...] = (own_ref[...].astype(F32) + land_ref[...].astype(F32)).astype(out_ref.dtype)

    return pl.pallas_call(
        body, name=name,
        out_shape=[jax.ShapeDtypeStruct((4,) + b, a.dtype) for a, b in zip(owns, blocks)],
        grid_spec=pltpu.PrefetchScalarGridSpec(
            num_scalar_prefetch=1, grid=(4,),
            in_specs=[pl.BlockSpec((None, None) + b, lambda k, core: (k, core[0], 0, 0)) for b in blocks]
            + [pl.BlockSpec((None,) + b, lambda k, core: (k, 0, 0)) for b in blocks],
            out_specs=[pl.BlockSpec((None,) + b, lambda k, core: (k, 0, 0)) for b in blocks]),
        compiler_params=_cparams(("parallel",)),
    )(core, *owns, *lands)


def _adamw_math(w, g, m, v):
    m = ADAM_B1 * m + (1.0 - ADAM_B1) * g
    v = ADAM_B2 * v + (1.0 - ADAM_B2) * (g * g)
    m_hat = m / (1.0 - ADAM_B1 ** ADAM_STEP)
    v_hat = v / (1.0 - ADAM_B2 ** ADAM_STEP)
    delta = -ADAM_LR * (m_hat / (jnp.sqrt(v_hat) + ADAM_EPS) + ADAM_WD * w)
    return delta, m, v


def _adam_shard(cps, lands, chip, w, m, v, name):
    rows, cols = w.shape
    tr = rows // 4
    nparts = len(cps)

    def body(chip_ref, *refs):
        del chip_ref
        cp_refs, land_refs = refs[:nparts], refs[nparts:2 * nparts]
        w_ref, m_ref, v_ref, g_out, d_out, m_out, v_out = refs[2 * nparts:]
        parts = []
        for cp_ref, land_ref in zip(cp_refs, land_refs):
            g_k = cp_ref[...].astype(F32)
            for r in range(3):
                g_k = g_k + land_ref[r].astype(F32)
            parts.append(g_k)
        g = parts[0] if nparts == 1 else jnp.concatenate(parts, axis=1)
        delta, m_new, v_new = _adamw_math(w_ref[...], g, m_ref[...], v_ref[...])
        g_out[...] = g
        d_out[...] = delta
        m_out[...] = m_new
        v_out[...] = v_new

    tile = pl.BlockSpec((tr, cols), lambda r, chip: (r, 0))
    out = jax.ShapeDtypeStruct((rows, cols), F32)
    return pl.pallas_call(
        body, name=name, out_shape=[out] * 4,
        grid_spec=pltpu.PrefetchScalarGridSpec(
            num_scalar_prefetch=1, grid=(rows // tr,),
            in_specs=[pl.BlockSpec((None, tr, a.shape[2]), lambda r, chip: (chip[0], r, 0)) for a in cps]
            + [pl.BlockSpec((3, tr, a.shape[2]), lambda r, chip: (0, r, 0)) for a in lands]
            + [tile, tile, tile],
            out_specs=[tile] * 4),
        compiler_params=_cparams(("parallel",)),
    )(chip, *cps, *lands, w, m, v)


def _adam_shards_whole(owns, lands, dev, ws, ms, vs, name):
    n = len(ws)

    def body(dev_ref, *refs):
        own_refs, land_refs = refs[:n], refs[n:2 * n]
        w_refs, m_refs, v_refs = refs[2 * n:3 * n], refs[3 * n:4 * n], refs[4 * n:5 * n]
        outs = refs[5 * n:]
        for i in range(n):
            g = own_refs[i][dev_ref[0]].astype(F32)
            for r in range(N_DEV - 1):
                g = g + land_refs[i][r].astype(F32)
            delta, m_new, v_new = _adamw_math(w_refs[i][...], g, m_refs[i][...], v_refs[i][...])
            outs[i][...] = g
            outs[n + i][...] = delta
            outs[2 * n + i][...] = m_new
            outs[3 * n + i][...] = v_new

    vmem = pl.BlockSpec(memory_space=pltpu.VMEM)
    res = pl.pallas_call(
        body, name=name, out_shape=[jax.ShapeDtypeStruct(w.shape, F32) for w in ws] * 4,
        in_specs=[pl.BlockSpec(memory_space=pltpu.SMEM)] + [vmem] * (5 * n), out_specs=[vmem] * (4 * n),
        compiler_params=_cparams(),
    )(dev, *owns, *lands, *ws, *ms, *vs)
    return [res[k * n:(k + 1) * n] for k in range(4)]


def _adam_small(parts, ws, ms, vs, loss_parts, name):
    n = len(ws)

    def body(*refs):
        p_refs, w_refs, m_refs, v_refs = refs[:n], refs[n:2 * n], refs[2 * n:3 * n], refs[3 * n:4 * n]
        loss_ref, outs, loss_out = refs[4 * n], refs[4 * n + 1:-1], refs[-1]
        total = loss_ref[0]
        for dev in range(1, N_DEV):
            total = total + loss_ref[dev]
        loss_out[...] = total
        for i in range(n):
            g = p_refs[i][0]
            for dev in range(1, N_DEV):
                g = g + p_refs[i][dev]
            if g.shape[0] != w_refs[i].shape[0]:
                g = jnp.sum(g, axis=0, keepdims=True)
            delta, m_new, v_new = _adamw_math(w_refs[i][...], g, m_refs[i][...], v_refs[i][...])
            outs[i][...] = g
            outs[n + i][...] = delta
            outs[2 * n + i][...] = m_new
            outs[3 * n + i][...] = v_new

    out_shapes = [jax.ShapeDtypeStruct(w.shape, F32) for w in ws] * 4
    out_shapes.append(jax.ShapeDtypeStruct(loss_parts.shape[1:], F32))
    res = pl.pallas_call(body, name=name, out_shape=out_shapes, compiler_params=_cparams())(
        *parts, *ws, *ms, *vs, loss_parts)
    return [res[k * n:(k + 1) * n] for k in range(4)], res[-1]


def _rmsnorm_fwd(x, g, riders, name):
    s, d = x.shape
    ts = 512
    nt = s // ts
    nr = len(riders)

    def body(x_ref, g_ref, *rest):
        rider_in, (h_ref, ht_ref) = rest[:nr], rest[nr:nr + 2]
        rider_out, staged = rest[nr + 2:2 * nr + 2], rest[2 * nr + 2:3 * nr + 2]
        sems = rest[3 * nr + 2:]
        i = pl.program_id(0)
        gather = _Gather(staged, rider_out, *sems, relay=True)

        @pl.when(i == 0)
        def _():
            for src, dst in zip(rider_in, staged):
                dst[...] = src[...].astype(BF16)
            gather.issue()

        xv = x_ref[...]
        rstd = lax.rsqrt(jnp.mean(xv * xv, axis=-1, keepdims=True) + EPS)
        h = xv * rstd * g_ref[...]
        h_ref[...] = h.astype(BF16)
        ht_ref[...] = h.T.astype(BF16)

        @pl.when(i == nt - 1)
        def _():
            gather.finish()

    any_spec = pl.BlockSpec(memory_space=pl.ANY)
    res = pl.pallas_call(
        body, name=name,
        out_shape=[jax.ShapeDtypeStruct((s, d), BF16), jax.ShapeDtypeStruct((d, s), BF16)]
        + [jax.ShapeDtypeStruct((N_DEV,) + r.shape, BF16) for r in riders],
        grid=(nt,),
        in_specs=[pl.BlockSpec((ts, d), lambda i: (i, 0)), pl.BlockSpec((1, d), lambda i: (0, 0))]
        + [pl.BlockSpec(r.shape, lambda i: (0, 0)) for r in riders],
        out_specs=[pl.BlockSpec((ts, d), lambda i: (i, 0)), pl.BlockSpec((d, ts), lambda i: (0, i))]
        + [any_spec] * nr,
        scratch_shapes=[pltpu.VMEM(r.shape, BF16) for r in riders] + _gather_semaphores(nr),
        compiler_params=_cparams(("arbitrary",)),
    )(x, g, *riders)
    return res[0], res[1], res[2:]


def _inproj(h, wt, name):
    s, d = h.shape
    n = wt.shape[0]
    tn = 256
    n_qkv = QKV_COLS // tn

    def body(h_ref, w_ref, qkv_ref, rest_ref):
        j = pl.program_id(0)
        res = _dot_nt(h_ref[...], w_ref[...])

        @pl.when(j < n_qkv)
        def _():
            qkv_ref[...] = res.astype(BF16)

        @pl.when(j >= n_qkv)
        def _():
            rest_ref[...] = res

    return pl.pallas_call(
        body, name=name,
        out_shape=[jax.ShapeDtypeStruct((s, QKV_COLS), BF16), jax.ShapeDtypeStruct((s, n - QKV_COLS), F32)],
        grid=(n // tn,),
        in_specs=[pl.BlockSpec((s, d), lambda j: (0, 0)), pl.BlockSpec((tn, d), lambda j: (j, 0))],
        out_specs=[pl.BlockSpec((s, tn), lambda j: (0, jnp.minimum(j, n_qkv - 1))),
                   pl.BlockSpec((s, tn), lambda j: (0, jnp.maximum(j - n_qkv, 0)))],
        compiler_params=_cparams(("arbitrary",)),
    )(h, wt)


def _dw_in(ht, dproj, name):
    d, s = ht.shape
    n = dproj.shape[1]
    tn = 512

    def body(a_ref, b_ref, o_ref):
        o_ref[...] = _dot(a_ref[...], b_ref[...]).T.astype(o_ref.dtype)

    return pl.pallas_call(
        body, name=name, out_shape=jax.ShapeDtypeStruct((n, d), BF16), grid=(n // tn,),
        in_specs=[pl.BlockSpec((d, s), lambda j: (0, 0)), pl.BlockSpec((s, tn), lambda j: (0, j))],
        out_specs=pl.BlockSpec((tn, d), lambda j: (j, 0)),
        compiler_params=_cparams(("parallel",)),
    )(ht, dproj)


def _dh_and_grad_x(dproj, wt, x, g, dx2, cps, name):
    s, n = dproj.shape
    d = wt.shape[1]
    tm, tk = min(s, 1024), 512
    nk = n // tk
    nm = s // tm
    nc = len(cps)

    def body(a_ref, w_ref, x_ref, g_ref, dx2_ref, *rest):
        cp_refs, gx_ref, dg_ref = rest[:nc], rest[nc], rest[nc + 1]
        land_refs = rest[nc + 2:2 * nc + 2]
        acc_ref, send_sems, recv_sems = rest[2 * nc + 2:]
        i, k = pl.program_id(0), pl.program_id(1)

        @pl.when((i == 0) & (k == 0))
        def _():
            for cp in _chip_push_copies(cp_refs, land_refs, send_sems, recv_sems):
                cp.start()
            dg_ref[...] = jnp.zeros_like(dg_ref)

        @pl.when(k == 0)
        def _():
            acc_ref[...] = jnp.zeros_like(acc_ref)

        acc_ref[...] += _dot(a_ref[...], w_ref[...])

        @pl.when(k == nk - 1)
        def _():
            dh = acc_ref[...]
            xv = x_ref[...]
            rstd = lax.rsqrt(jnp.mean(xv * xv, axis=-1, keepdims=True) + EPS)
            xhat = xv * rstd
            dg_ref[...] += jnp.sum((dh * xhat).reshape(tm // 8, 8, d), axis=0)
            dxh = dh * g_ref[...]
            gx_ref[...] = dx2_ref[...] + rstd * (dxh - xhat * jnp.mean(dxh * xhat, axis=-1, keepdims=True))

        @pl.when((i == nm - 1) & (k == nk - 1))
        def _():
            for cp in _chip_push_copies(cp_refs, land_refs, send_sems, recv_sems):
                cp.wait()

    any_spec = pl.BlockSpec(memory_space=pl.ANY)
    res = pl.pallas_call(
        body, name=name,
        out_shape=[jax.ShapeDtypeStruct((s, d), F32), jax.ShapeDtypeStruct((8, d), F32)]
        + [jax.ShapeDtypeStruct((3,) + a.shape[1:], a.dtype) for a in cps],
        grid=(nm, nk),
        in_specs=[pl.BlockSpec((tm, tk), lambda i, k: (i, k)), pl.BlockSpec((tk, d), lambda i, k: (k, 0)),
                  pl.BlockSpec((tm, d), lambda i, k: (i, 0)), pl.BlockSpec((1, d), lambda i, k: (0, 0)),
                  pl.BlockSpec((tm, d), lambda i, k: (i, 0))] + [any_spec] * nc,
        out_specs=[pl.BlockSpec((tm, d), lambda i, k: (i, 0)), pl.BlockSpec((8, d), lambda i, k: (0, 0))]
        + [any_spec] * nc,
        scratch_shapes=[pltpu.VMEM((tm, d), F32), pltpu.SemaphoreType.DMA((nc, 3)), pltpu.SemaphoreType.DMA((nc, 3))],
        compiler_params=_cparams(("arbitrary", "arbitrary")),
    )(dproj, wt, x, g, dx2, *cps)
    return res[0], res[1], res[2:]


def _log_sigmoids(z):
    l1p = jnp.log(1.0 + jnp.exp(-jnp.abs(z)))
    ls = jnp.minimum(z, 0.0) - l1p
    return ls, ls - z


def _strict_lower_ones(n):
    row = lax.broadcasted_iota(jnp.int32, (n, n), 0)
    col = lax.broadcasted_iota(jnp.int32, (n, n), 1)
    return row, col, (row > col).astype(BF16)


def _attn_fwd(qkv, riders, name):
    s = qkv.shape[0]
    tb, tt = ATT_BLOCK, ATT_TILE
    nq = s // tb
    per_block = tb // tt
    assert per_block == 2 and s // tt <= LANES, "two query tiles per grid step; one lane of saved carry per key tile"

    nr = len(riders)

    def sweep_body(gather, q_ref, k_ref, v_ref, o_ref, rs_ref, extra_ref, acc_ref, r_ref, rsv_ref, rmax_ref):
        i = pl.program_id(1)
        lane = lax.broadcasted_iota(jnp.int32, (tt, LANES), 1)
        hmask = [lane < HEAD_DIM, lane >= HEAD_DIM]
        row, col, tri = _strict_lower_ones(tt)
        tri2 = jnp.concatenate([tri, tri], axis=0)
        below = col < row
        qrows = [slice(u * tt, (u + 1) * tt) for u in range(per_block)]
        qm = [[jnp.where(m, q_ref[qrows[u], :], jnp.zeros((), BF16)) * jnp.asarray(HEAD_DIM ** -0.5, BF16)
               for m in hmask] for u in range(per_block)]
        acc_ref[...] = jnp.zeros_like(acc_ref)
        r_ref[...] = jnp.zeros_like(r_ref)
        rsv_ref[...] = jnp.full_like(rsv_ref, R_UNREACHED)

        def sweep(tiles, chains):
            heads, nch = range(2), range(len(chains))
            kv = []
            for t in tiles:
                rows = pl.ds(pl.multiple_of(t * tt, tt), tt)
                kv.append((k_ref[rows, :], v_ref[rows, :]))
            z = [[_dot_nt(qm[u][h], kv[ti][0]) for h in heads] for u, ti, _ in chains]
            ls, lk = [], []
            for n in nch:
                pairs = [_log_sigmoids(z[n][h]) for h in heads]
                ls.append([p[0] for p in pairs])
                lk.append([jnp.where(below, p[1], 0.0) if chains[n][2] else p[1] for p in pairs])
            cur = {u: [r_ref[h, qrows[u], :] for h in heads] for u in sorted({c[0] for c in chains})}
            r = []
            for n, (u, _, _) in enumerate(chains):
                r.append(cur[u])
                cur[u] = [cur[u][h] + jnp.sum(lk[n][h], axis=1, keepdims=True) for h in heads]
            for u in cur:
                rmax_ref[u] = jnp.max(jnp.maximum(cur[u][0], cur[u][1]))
            suffix = [[_dot(jnp.concatenate(_split_hi_lo(lk[n][h]), axis=1), tri2) for h in heads] for n in nch]
            w = []
            for n in nch:
                w_n = [jnp.exp(ls[n][h] + suffix[n][h] + r[n][h]) for h in heads]
                if chains[n][2]:
                    w_n = [jnp.where(below, w_h, 0.0) for w_h in w_n]
                w.append([w_h.astype(BF16) for w_h in w_n])
            pv = [[_dot(w[n][h], kv[chains[n][1]][1]) for h in heads] for n in nch]
            for u in cur:
                mine = [n for n in nch if chains[n][0] == u]
                for h in heads:
                    acc_ref[h, qrows[u], :] += functools.reduce(lambda a, b: a + b, [pv[n][h] for n in mine])
                    for n in mine:
                        if not chains[n][2]:
                            t = tiles[chains[n][1]]
                            rsv_ref[h, qrows[u], :] = jnp.where(lane == t, r[n][h], rsv_ref[h, qrows[u], :])
                    r_ref[h, qrows[u], :] = cur[u][h]

        first = per_block * i

        @pl.when(i == 0)
        def _():
            sweep([0, 1], [(0, 0, True), (1, 1, True), (1, 0, False)])

        @pl.when(i > 0)
        def _():
            sweep([first - 1, first, first + 1], [(0, 1, True), (1, 2, True), (0, 0, False), (1, 1, False)])

        for u in range(per_block):
            n_left = first + u - 1

            def live(c, n_left=n_left):
                jj, rmax = c
                return (jj < n_left) & (rmax >= CARRY_FLOOR)

            def step(c, u=u, n_left=n_left):
                jj, _ = c
                sweep([n_left - 1 - jj], [(u, 0, False)])
                return jj + 1, rmax_ref[u]

            swept, _ = lax.while_loop(live, step, (jnp.int32(0), rmax_ref[u]))
            extra_ref[pl.program_id(0), first + u] = swept.astype(F32)

        lane_b = lax.broadcasted_iota(jnp.int32, (tb, LANES), 1)
        o_ref[...] = jnp.where(lane_b < HEAD_DIM, acc_ref[0], acc_ref[1])
        rs_ref[...] = rsv_ref[...]

        @pl.when((pl.program_id(0) == N_PAIRS - 1) & (i == nq - 1))
        def _():
            gather.finish()

    def body(q_ref, k_ref, v_ref, *rest):
        rider_in, (o_ref, rs_ref, extra_ref) = rest[:nr], rest[nr:nr + 3]
        rider_out = rest[nr + 3:2 * nr + 3]
        acc_ref, r_ref, rsv_ref, rmax_ref = rest[2 * nr + 3:2 * nr + 7]
        gather = _Gather(rider_in, rider_out, *rest[2 * nr + 7:])

        @pl.when((pl.program_id(0) == 0) & (pl.program_id(1) == 0))
        def _():
            gather.issue()

        sweep_body(gather, q_ref, k_ref, v_ref, o_ref, rs_ref, extra_ref, acc_ref, r_ref, rsv_ref, rmax_ref)

    any_spec = pl.BlockSpec(memory_space=pl.ANY)
    res = pl.pallas_call(
        body, name=name,
        out_shape=[jax.ShapeDtypeStruct((s, D_SB), F32), jax.ShapeDtypeStruct((N_HEADS, s, LANES), F32),
                   jax.ShapeDtypeStruct((N_PAIRS, s // tt), F32)] + _gather_out_shapes(riders),
        grid=(N_PAIRS, nq),
        in_specs=[pl.BlockSpec((tb, LANES), lambda p, i: (i, p)),
                  pl.BlockSpec((s, LANES), lambda p, i: (0, N_PAIRS + p)),
                  pl.BlockSpec((s, LANES), lambda p, i: (0, 2 * N_PAIRS + p))] + [any_spec] * nr,
        out_specs=[pl.BlockSpec((tb, LANES), lambda p, i: (i, p)),
                   pl.BlockSpec((2, tb, LANES), lambda p, i: (p, i, 0)),
                   pl.BlockSpec(memory_space=pltpu.SMEM)] + [any_spec] * nr,
        scratch_shapes=[pltpu.VMEM((2, tb, LANES), F32), pltpu.VMEM((2, tb, LANES), F32),
                        pltpu.VMEM((2, tb, LANES), F32), pltpu.SMEM((per_block,), F32)] + _gather_semaphores(nr),
        compiler_params=_cparams(("arbitrary", "arbitrary")),
    )(qkv, qkv, qkv, *riders)
    return res[0], res[1], res[2], res[3:]


def _attn_bwd(qkv, do, rs, extra, dproj, smalls, cps, name):
    s = qkv.shape[0]
    tb, tt = ATT_BLOCK, ATT_TILE
    nq = s // tb
    per_block = tb // tt
    scale = HEAD_DIM ** -0.5

    ns, nc = len(smalls), len(cps)

    def sweep_body(q_ref, k_ref, v_ref, do_ref, rs_ref, extra_ref, dproj_hbm, out_hbm, dq_acc, dk_acc, dv_acc,
                   dqi_ref, pc_ref, stage_ref, out_sems):
        del dproj_hbm
        pair = pl.program_id(0)
        lane = lax.broadcasted_iota(jnp.int32, (tt, LANES), 1)
        hmask = [lane < HEAD_DIM, lane >= HEAD_DIM]
        row, col, tri = _strict_lower_ones(tt)
        tri2 = jnp.concatenate([tri, tri], axis=0)
        triu = (row < col).astype(BF16)
        below = col < row
        zero = jnp.zeros((), BF16)
        qrows = [slice(u * tt, (u + 1) * tt) for u in range(per_block)]
        dq_acc[...] = jnp.zeros_like(dq_acc)
        dk_acc[...] = jnp.zeros_like(dk_acc)
        dv_acc[...] = jnp.zeros_like(dv_acc)

        def qblock(i, carry):
            block_rows = pl.ds(pl.multiple_of(i * tb, tb), tb)
            q2 = q_ref[block_rows, :]
            do2 = do_ref[block_rows, :]
            qm = [[jnp.where(m, q2[qrows[u]], zero) * jnp.asarray(scale, BF16) for m in hmask]
                  for u in range(per_block)]
            dom = [[jnp.where(m, do2[qrows[u]], zero) for m in hmask] for u in range(per_block)]
            rs_i = [[rs_ref[h, pl.ds(pl.multiple_of(i * tb + u * tt, tt), tt), :] for h in range(2)]
                    for u in range(per_block)]
            dqi_ref[...] = jnp.zeros_like(dqi_ref)
            pc_ref[...] = jnp.zeros_like(pc_ref)
            first = per_block * i

            def sweep(tiles, chains):
                heads, nch = range(2), range(len(chains))
                krows = [pl.ds(pl.multiple_of(t * tt, tt), tt) for t in tiles]
                k2 = [k_ref[rows, :] for rows in krows]
                v2 = [v_ref[rows, :] for rows in krows]
                z = [[_dot_nt(qm[u][h], k2[ti]) for h in heads] for u, ti, _ in chains]
                dw = [[_dot_nt(dom[u][h], v2[ti]) for h in heads] for u, ti, _ in chains]
                ls, lk = [], []
                for n in nch:
                    pairs = [_log_sigmoids(z[n][h]) for h in heads]
                    ls.append([p[0] for p in pairs])
                    lk.append([jnp.where(below, p[1], 0.0) if chains[n][2] else p[1] for p in pairs])
                suffix = [[_dot(jnp.concatenate(_split_hi_lo(lk[n][h]), axis=1), tri2) for h in heads] for n in nch]
                w, g = [], []
                for n, (u, ti, diag) in enumerate(chains):
                    w_n = []
                    for h in heads:
                        logw = ls[n][h] + suffix[n][h]
                        if diag:
                            w_n.append(jnp.where(below, jnp.exp(logw), 0.0))
                        else:
                            carry_in = jnp.sum(jnp.where(lane == tiles[ti], rs_i[u][h], 0.0), axis=1, keepdims=True)
                            w_n.append(jnp.exp(logw + carry_in))
                    w.append([w_h.astype(BF16) for w_h in w_n])
                    g.append([w_n[h] * dw[n][h] for h in heads])
                prefix = [[_dot(g[n][h].astype(BF16), triu) for h in heads] for n in nch]
                cur = {u: [pc_ref[h, qrows[u], :] for h in heads] for u in sorted({c[0] for c in chains})}
                pc = []
                for n, (u, _, _) in enumerate(chains):
                    pc.append(cur[u])
                    cur[u] = [cur[u][h] + jnp.sum(g[n][h], axis=1, keepdims=True) for h in heads]
                dzb = []
                for n in nch:
                    dz_n = []
                    for h in heads:
                        sig = jnp.exp(ls[n][h])
                        dz = g[n][h] - sig * (g[n][h] + prefix[n][h] + pc[n][h])
                        if chains[n][2]:
                            dz = jnp.where(below, dz, 0.0)
                        dz_n.append(dz.astype(BF16))
                    dzb.append(dz_n)
                for u in cur:
                    for h in heads:
                        pc_ref[h, qrows[u], :] = cur[u][h]
                dq = [[_dot(dzb[n][h], jnp.where(hmask[h], k2[chains[n][1]], zero)) for h in heads] for n in nch]
                dk = [[_dot_tn(dzb[n][h], qm[chains[n][0]][h]) for h in heads] for n in nch]
                dv = [[_dot_tn(w[n][h], dom[chains[n][0]][h]) for h in heads] for n in nch]
                add = lambda a, b: a + b
                for u in cur:
                    dqi_ref[qrows[u], :] += functools.reduce(
                        add, [dq[n][h] for n in nch if chains[n][0] == u for h in heads])
                for ti in range(len(tiles)):
                    mine = [n for n in nch if chains[n][1] == ti]
                    dk_acc[krows[ti], :] += functools.reduce(add, [dk[n][h] for n in mine for h in heads])
                    dv_acc[krows[ti], :] += functools.reduce(add, [dv[n][h] for n in mine for h in heads])

            for u in range(per_block):
                n_left = first + u - 1
                n_extra = jnp.clip(extra_ref[pair, first + u].astype(jnp.int32), 0, jnp.maximum(n_left, 0))

                def step(t, c, u=u):
                    sweep([t], [(u, 0, False)])
                    return c

                lax.fori_loop(n_left - n_extra, n_left, step, 0)

            @pl.when(i == 0)
            def _():
                sweep([0, 1], [(0, 0, True), (1, 0, False), (1, 1, True)])

            @pl.when(i > 0)
            def _():
                sweep([first - 1, first, first + 1], [(0, 0, False), (1, 1, False), (0, 1, True), (1, 2, True)])

            dq_acc[block_rows, :] += dqi_ref[...] * scale
            return carry

        lax.fori_loop(0, nq, qblock, 0)
        copies = []
        for t, acc in enumerate((dq_acc, dk_acc, dv_acc)):
            stage_ref[t] = acc[...].astype(BF16)
            col0 = pl.multiple_of(t * D_SB + pair * LANES, LANES)
            copies.append(pltpu.make_async_copy(stage_ref.at[t], out_hbm.at[:, pl.ds(col0, LANES)], out_sems.at[t]))
        for cp in copies:
            cp.start()
        for cp in copies:
            cp.wait()

    def body(q_ref, k_ref, v_ref, do_ref, rs_ref, extra_ref, dproj_hbm, *rest):
        small_in, cp_in = rest[:ns], rest[ns:ns + nc]
        out_hbm = rest[ns + nc]
        small_out, land_out = rest[ns + nc + 1:2 * ns + nc + 1], rest[2 * ns + nc + 1:2 * (ns + nc) + 1]
        scratch = rest[2 * (ns + nc) + 1:]
        gather = _Gather(small_in, small_out, *scratch[7:10])
        pair = pl.program_id(0)

        @pl.when(pair == 0)
        def _():
            gather.issue()
            for cp in _direct_push_copies(cp_in, land_out, *scratch[10:]):
                cp.start()

        sweep_body(q_ref, k_ref, v_ref, do_ref, rs_ref, extra_ref, dproj_hbm, out_hbm, *scratch[:7])

        @pl.when(pair == N_PAIRS - 1)
        def _():
            gather.finish()
            for cp in _direct_push_copies(cp_in, land_out, *scratch[10:]):
                cp.wait()

    any_spec = pl.BlockSpec(memory_space=pl.ANY)
    res = pl.pallas_call(
        body, name=name,
        out_shape=[jax.ShapeDtypeStruct(dproj.shape, BF16)] + _gather_out_shapes(smalls)
        + [jax.ShapeDtypeStruct((N_DEV - 1,) + a.shape[1:], a.dtype) for a in cps],
        grid=(N_PAIRS,),
        in_specs=[pl.BlockSpec((s, LANES), lambda p: (0, p)),
                  pl.BlockSpec((s, LANES), lambda p: (0, N_PAIRS + p)),
                  pl.BlockSpec((s, LANES), lambda p: (0, 2 * N_PAIRS + p)),
                  pl.BlockSpec((s, LANES), lambda p: (0, p)),
                  pl.BlockSpec((2, s, LANES), lambda p: (p, 0, 0)),
                  pl.BlockSpec(memory_space=pltpu.SMEM),
                  any_spec] + [any_spec] * (ns + nc),
        out_specs=[any_spec] * (1 + ns + nc),
        scratch_shapes=[pltpu.VMEM((s, LANES), F32), pltpu.VMEM((s, LANES), F32), pltpu.VMEM((s, LANES), F32),
                        pltpu.VMEM((tb, LANES), F32), pltpu.VMEM((2, tb, LANES), F32),
                        pltpu.VMEM((3, s, LANES), BF16), pltpu.SemaphoreType.DMA((3,))]
        + _gather_semaphores(ns)
        + [pltpu.SemaphoreType.DMA((nc, N_DEV - 1)), pltpu.SemaphoreType.DMA((nc, N_DEV - 1))],
        input_output_aliases={6: 0},
        compiler_params=_cparams(("arbitrary",)),
    )(qkv, qkv, qkv, do, rs, extra, dproj, *smalls, *cps)
    return res[0], res[1:1 + ns], res[1 + ns:]


def _mid(o, rest, x, tgt, wua, wub, wout, fg, lng, lnb, wsp, bfull, gavg, name):
    s, d = x.shape
    ts = 256
    nt = s // ts
    nchunk = ts // SGU_CHUNK
    n_rest = rest.shape[1]

    def body(o_ref, rest_ref, x_ref, t_ref, wua_ref, wub_ref, wout_ref, fg_ref, lng_ref, lnb_ref, wsp_ref, bfull_ref,
             gavg_ref, loss_ref, dx2_ref, do_ref, dproj_ref, dwua_out, dwub_out, dwout_out, dfg_ref, dlng_ref,
             dlnb_ref, dwsp_ref, dbfull_ref, dwua_ref, dwub_ref, dwout_ref):
        step = pl.program_id(0)

        @pl.when(step == 0)
        def _():
            for ref in (loss_ref, dwua_ref, dwub_ref, dwout_ref, dfg_ref, dlng_ref, dlnb_ref, dwsp_ref, dbfull_ref):
                ref[...] = jnp.zeros_like(ref)

        gavg = gavg_ref[...]

        def gmean(a):
            return _dot(a.astype(BF16), gavg)

        def colsum8(a):
            return jnp.sum(a.reshape(ts // 8, 8, a.shape[1]), axis=0)

        z_a = rest_ref[:, 0:512]
        u_b = rest_ref[:, 512:1024]
        v_b = rest_ref[:, 1024:1536]
        z_b = rest_ref[:, 1536:2048]
        g_a = rest_ref[:, 2048:2048 + d]
        g_b = rest_ref[:, 2048 + d:2048 + 2 * d]
        ov = o_ref[...]
        sa = _sigmoid(z_a)
        silu_a = z_a * sa
        y_a = ov * silu_a
        ug, dug_du = _gelu_and_grad(u_b)
        vg, dvg_dv = _gelu_and_grad(v_b)
        mu = gmean(vg)
        cen = vg - mu
        rstd_g = lax.rsqrt(gmean(cen * cen) + EPS)
        vhat = cen * rstd_g
        vn = vhat * lng_ref[...] + lnb_ref[...]
        vnb = vn.astype(BF16)

        t_idx = lax.broadcasted_iota(jnp.int32, (SGU_CHUNK, SGU_CHUNK), 0)
        s_idx = lax.broadcasted_iota(jnp.int32, (SGU_CHUNK, SGU_CHUNK), 1)
        causal = (s_idx // CHUNK) <= (t_idx // CHUNK)
        wm = [jnp.where(causal, wsp_ref[g], 0.0) for g in range(N_GROUPS)]
        wmb = [w.astype(BF16) for w in wm]
        wmtb = [w.T.astype(BF16) for w in wm]
        lane = lax.broadcasted_iota(jnp.int32, (SGU_CHUNK, LANES), 1)
        first = lane < GROUP_DIM
        bfull = bfull_ref[...]

        mixed_rows = []
        for n in range(nchunk):
            r0, r1 = n * SGU_CHUNK, (n + 1) * SGU_CHUNK
            pieces = []
            for p in range(N_GROUPS // 2):
                blk = vnb[r0:r1, p * LANES:(p + 1) * LANES]
                pieces.append(jnp.where(first, _dot(wmb[2 * p], blk), _dot(wmb[2 * p + 1], blk)))
            mixed_rows.append(jnp.concatenate(pieces, axis=1) + bfull)
        mixed = jnp.concatenate(mixed_rows, axis=0)
        sg = ug * mixed
        sb = _sigmoid(z_b)
        silu_b = z_b * sb
        y_b = sg * silu_b
        y_ab = y_a.astype(BF16)
        y_bb = y_b.astype(BF16)
        p_a = _dot(y_ab, wua_ref[...])
        p_b = _dot(y_bb, wub_ref[...])
        ga_s = _sigmoid(g_a)
        gb_s = _sigmoid(g_b)
        merged_b = (ga_s * p_a + gb_s * p_b).astype(BF16)
        x2 = x_ref[...] + _dot(merged_b, wout_ref[...])
        rstd = lax.rsqrt(jnp.mean(x2 * x2, axis=-1, keepdims=True) + EPS)
        xhat = x2 * rstd
        fg_v = fg_ref[...]
        diff = xhat * fg_v - t_ref[...]
        loss_ref[...] += 0.5 * jnp.sum(jnp.sum(diff * diff, axis=-1, keepdims=True) * (1.0 / d))

        dy = diff * (1.0 / d)
        dfg_ref[...] += colsum8(dy * xhat)
        dxh = dy * fg_v
        dx2 = rstd * (dxh - xhat * jnp.mean(dxh * xhat, axis=-1, keepdims=True))
        dx2_ref[...] = dx2
        dx2b = dx2.astype(BF16)
        dwout_ref[...] += _dot_tn(merged_b, dx2b)
        dmerged = _dot_nt(dx2b, wout_ref[...])
        dp_a = dmerged * ga_s
        dp_b = dmerged * gb_s
        dproj_ref[:, QKV_COLS + 2048:QKV_COLS + 2048 + d] = (dmerged * p_a * (ga_s * (1.0 - ga_s))).astype(BF16)
        dproj_ref[:, QKV_COLS + 2048 + d:QKV_COLS + 2048 + 2 * d] = (dmerged * p_b * (gb_s * (1.0 - gb_s))).astype(BF16)
        dp_ab = dp_a.astype(BF16)
        dp_bb = dp_b.astype(BF16)
        dwua_ref[...] += _dot_tn(y_ab, dp_ab)
        dwub_ref[...] += _dot_tn(y_bb, dp_bb)
        dy_a = _dot_nt(dp_ab, wua_ref[...])
        dy_b = _dot_nt(dp_bb, wub_ref[...])
        do_ref[...] = (dy_a * silu_a).astype(BF16)
        dproj_ref[:, QKV_COLS:QKV_COLS + 512] = (dy_a * ov * (sa * (1.0 + z_a * (1.0 - sa)))).astype(BF16)
        dsg = dy_b * silu_b
        dproj_ref[:, QKV_COLS + 1536:QKV_COLS + 2048] = (dy_b * sg * (sb * (1.0 + z_b * (1.0 - sb)))).astype(BF16)
        dproj_ref[:, QKV_COLS + 512:QKV_COLS + 1024] = (dsg * mixed * dug_du).astype(BF16)
        dmixed = dsg * ug
        dmb = dmixed.astype(BF16)
        zero = jnp.zeros((), BF16)
        dvn_rows = []
        db = jnp.zeros((SGU_CHUNK, D_SGU), F32)
        for n in range(nchunk):
            r0, r1 = n * SGU_CHUNK, (n + 1) * SGU_CHUNK
            db = db + dmixed[r0:r1, :]
            pieces = []
            for p in range(N_GROUPS // 2):
                cols = slice(p * LANES, (p + 1) * LANES)
                dm_blk = dmb[r0:r1, cols]
                vn_blk = vnb[r0:r1, cols]
                dwsp_ref[2 * p] += _dot_nt(jnp.where(first, dm_blk, zero), vn_blk)
                dwsp_ref[2 * p + 1] += _dot_nt(jnp.where(first, zero, dm_blk), vn_blk)
                pieces.append(jnp.where(first, _dot(wmtb[2 * p], dm_blk), _dot(wmtb[2 * p + 1], dm_blk)))
            dvn_rows.append(jnp.concatenate(pieces, axis=1))
        dbfull_ref[...] += db
        dvn = jnp.concatenate(dvn_rows, axis=0)
        dlng_ref[...] += colsum8(dvn * vhat)
        dlnb_ref[...] += colsum8(dvn)
        dvhat = dvn * lng_ref[...]
        dcen = rstd_g * (dvhat - gmean(dvhat) - vhat * gmean(dvhat * vhat))
        dproj_ref[:, QKV_COLS + 1024:QKV_COLS + 1536] = (dcen * dvg_dv).astype(BF16)

        @pl.when(step == nt - 1)
        def _():
            for g in range(N_GROUPS):
                dwsp_ref[g] = jnp.where(causal, dwsp_ref[g], 0.0)
            cb = d // N_DEV
            for k in range(N_DEV):
                dwua_out[k] = dwua_ref[:, k * cb:(k + 1) * cb].astype(BF16)
                dwub_out[k] = dwub_ref[:, k * cb:(k + 1) * cb].astype(BF16)
            dwout_out[...] = dwout_ref[...].astype(BF16)

    def tile(cols):
        return pl.BlockSpec((ts, cols), lambda i: (i, 0))

    def whole(shape):
        return pl.BlockSpec(shape, lambda i: (0,) * len(shape))

    out_shapes = [
        jax.ShapeDtypeStruct((8, LANES), F32),
        jax.ShapeDtypeStruct((s, d), F32),
        jax.ShapeDtypeStruct((s, D_SB), BF16),
        jax.ShapeDtypeStruct((s, QKV_COLS + n_rest), BF16),
        jax.ShapeDtypeStruct((N_DEV, D_SB, d // N_DEV), BF16),
        jax.ShapeDtypeStruct((N_DEV, D_SGU, d // N_DEV), BF16),
        jax.ShapeDtypeStruct((d, d), BF16),
        jax.ShapeDtypeStruct((8, d), F32),
        jax.ShapeDtypeStruct((8, D_SGU), F32),
        jax.ShapeDtypeStruct((8, D_SGU), F32),
        jax.ShapeDtypeStruct((N_GROUPS, SGU_CHUNK, SGU_CHUNK), F32),
        jax.ShapeDtypeStruct((SGU_CHUNK, D_SGU), F32),
    ]
    out_specs = [whole((8, LANES)), tile(d), tile(D_SB), tile(QKV_COLS + n_rest), whole((N_DEV, D_SB, d // N_DEV)),
                 whole((N_DEV, D_SGU, d // N_DEV)), whole((d, d)), whole((8, d)), whole((8, D_SGU)), whole((8, D_SGU)),
                 whole((N_GROUPS, SGU_CHUNK, SGU_CHUNK)), whole((SGU_CHUNK, D_SGU))]
    in_specs = [tile(D_SB), tile(n_rest), tile(d), tile(d), whole((D_SB, d)), whole((D_SGU, d)), whole((d, d)),
                whole((1, d)), whole((1, D_SGU)), whole((1, D_SGU)), whole((N_GROUPS, SGU_CHUNK, SGU_CHUNK)),
                whole((SGU_CHUNK, D_SGU)), whole((D_SGU, D_SGU))]
    return pl.pallas_call(
        body, name=name, out_shape=out_shapes, grid=(nt,), in_specs=in_specs, out_specs=out_specs,
        scratch_shapes=[pltpu.VMEM((D_SB, d), F32), pltpu.VMEM((D_SGU, d), F32), pltpu.VMEM((d, d), F32)],
        compiler_params=_cparams(("arbitrary",)),
    )(o, rest, x, tgt, wua, wub, wout, fg, lng, lnb, wsp, bfull, gavg)


def _small_reduce(dfg8, dlng8, dlnb8, dbfull, name):
    d = dfg8.shape[1]

    def body(dfg_ref, dlng_ref, dlnb_ref, dbfull_ref, fg_out, lng_out, lnb_out, b_out):
        row = lax.broadcasted_iota(jnp.int32, (D_SGU, LANES), 0)
        col = lax.broadcasted_iota(jnp.int32, (D_SGU, LANES), 1)

        def select_sum(a, sel):
            hi, lo = _split_hi_lo(a)
            lo2 = (a - hi.astype(F32) - lo.astype(F32)).astype(BF16)
            return _dot(hi, sel) + _dot(lo, sel) + _dot(lo2, sel)

        def by_group(partials):
            v = jnp.sum(partials, axis=0, keepdims=True)
            g_of_lane = lax.broadcasted_iota(jnp.int32, (N_GROUPS, D_SGU), 1) // GROUP_DIM
            g_of_row = lax.broadcasted_iota(jnp.int32, (N_GROUPS, D_SGU), 0)
            spread = jnp.where(g_of_lane == g_of_row, jnp.broadcast_to(v, (N_GROUPS, D_SGU)), 0.0)
            return select_sum(spread, (row % GROUP_DIM == col).astype(BF16))[:, 0:GROUP_DIM]

        fg_out[...] = jnp.sum(dfg_ref[...], axis=0, keepdims=True)
        lng_out[...] = by_group(dlng_ref[...])
        lnb_out[...] = by_group(dlnb_ref[...])
        by_pos = select_sum(dbfull_ref[...], (row // GROUP_DIM == col).astype(BF16))
        b_out[...] = by_pos.T[0:N_GROUPS, :]

    return pl.pallas_call(
        body, name=name,
        out_shape=[jax.ShapeDtypeStruct((1, d), F32), jax.ShapeDtypeStruct((N_GROUPS, GROUP_DIM), F32),
                   jax.ShapeDtypeStruct((N_GROUPS, GROUP_DIM), F32), jax.ShapeDtypeStruct((N_GROUPS, SGU_CHUNK), F32)],
        compiler_params=_cparams(),
    )(dfg8, dlng8, dlnb8, dbfull)


def _from_block_major_cols(w):
    nb, r, c = w.shape
    return jnp.transpose(w, (1, 0, 2)).reshape(r, nb * c)


def kernel(x, norm_g, w_in, sgu_ln_g, sgu_ln_b, w_spatial, b_spatial, w_up_a, w_up_b, w_out, final_norm_g, loss_target, m_norm_g, m_w_in, m_sgu_ln_g, m_sgu_ln_b, m_w_spatial, m_b_spatial, m_w_up_a, m_w_up_b, m_w_out, m_final_norm_g, v_norm_g, v_w_in, v_sgu_ln_g, v_sgu_ln_b, v_w_spatial, v_b_spatial, v_w_up_a, v_w_up_b, v_w_out, v_final_norm_g):
    s, d = x.shape[1], x.shape[2]
    xs = x[0]
    tgt = loss_target[0]
    cx, cy, cc = _coords()
    core = jnp.reshape(cc, (1,)).astype(jnp.int32)
    chip = jnp.reshape(2 * cx + cy, (1,)).astype(jnp.int32)

    h, ht, (g_win,) = _rmsnorm_fwd(xs, norm_g, [jnp.transpose(w_in[0])], "norm")
    d_in = N_DEV * w_in.shape[2]
    w_full = g_win.reshape(d_in, d)
    qkv, rest = _inproj(h, w_full, "inproj")
    o, rs, extra, (g_wua, g_wub, g_wout) = _attn_fwd(
        qkv, [w_up_a[0].astype(BF16), w_up_b[0].astype(BF16), w_out[0].astype(BF16)], "attn_fwd")
    wua_full = _from_block_major_cols(g_wua)
    wub_full = _from_block_major_cols(g_wub)
    wout_full = g_wout.reshape(d, d)

    lng = sgu_ln_g.reshape(1, D_SGU)
    lnb = sgu_ln_b.reshape(1, D_SGU)
    bfull = jnp.repeat(jnp.transpose(b_spatial[0]), GROUP_DIM, axis=1)
    grp = jnp.arange(D_SGU) // GROUP_DIM
    gavg = jnp.where(grp[:, None] == grp[None, :], 1.0 / GROUP_DIM, 0.0).astype(BF16)
    (loss_b, dx2, do, dproj, dwua, dwub, dwout, dfg8, dlng8, dlnb8, dwsp, dbfull) = _mid(
        o, rest, xs, tgt, wua_full, wub_full, wout_full, final_norm_g.reshape(1, d), lng, lnb, w_spatial[0], bfull,
        gavg, "mid")

    own_up = [dwua, dwub, dwout.reshape(N_DEV, d // N_DEV, d)]
    dfg, dlng, dlnb, db = _small_reduce(dfg8, dlng8, dlnb8, dbfull, "small_reduce")
    wsp_rows = N_GROUPS * SGU_CHUNK
    dproj, small_parts, land_up = _attn_bwd(
        qkv, do, rs, extra, dproj, [dfg, dlng, dlnb, db, dwsp.reshape(wsp_rows, SGU_CHUNK)], own_up, "attn_bwd")
    dwin = _dw_in(ht, dproj, "dwin")

    own_in = [dwin.reshape(4, 2, d_in // N_DEV, d)]
    cps_in = list(_chip_partial_sums(own_in, list(_push_sibling(own_in, "rs_sibling_in")), core, "cpsum_in"))
    grad_x, dng8, land_in = _dh_and_grad_x(dproj, w_full, xs, norm_g, dx2, cps_in, "dh")

    ng_parts, loss_parts = _allgather([dng8, loss_b], "ag_tail")

    def small_layouts(ng, fg, lg, lb, bs, ws):
        return [ng.reshape(1, d), fg.reshape(1, d), lg.reshape(N_GROUPS, GROUP_DIM), lb.reshape(N_GROUPS, GROUP_DIM),
                bs.reshape(N_GROUPS, SGU_CHUNK), ws.reshape(wsp_rows, SGU_CHUNK)]

    sm, loss_sum = _adam_small(
        [ng_parts] + list(small_parts),
        small_layouts(norm_g, final_norm_g, sgu_ln_g, sgu_ln_b, b_spatial, w_spatial),
        small_layouts(m_norm_g, m_final_norm_g, m_sgu_ln_g, m_sgu_ln_b, m_b_spatial, m_w_spatial),
        small_layouts(v_norm_g, v_final_norm_g, v_sgu_ln_g, v_sgu_ln_b, v_b_spatial, v_w_spatial),
        loss_parts, "adam_small")
    small_shapes = [norm_g.shape, final_norm_g.shape, sgu_ln_g.shape, sgu_ln_b.shape, b_spatial.shape, w_spatial.shape]
    sm = [[a.reshape(shp) for a, shp in zip(kind, small_shapes)] for kind in sm]

    res = _adam_shard(cps_in, list(land_in), chip, *[jnp.transpose(a[0]) for a in (w_in, m_w_in, v_w_in)], "adam0")
    big = [[jnp.transpose(r)[None] for r in res]]
    dev = jnp.reshape(_dev_index(cx, cy, cc), (1,)).astype(jnp.int32)
    res = _adam_shards_whole(own_up, list(land_up), dev, [w_up_a[0], w_up_b[0], w_out[0]],
                             [m_w_up_a[0], m_w_up_b[0], m_w_out[0]], [v_w_up_a[0], v_w_up_b[0], v_w_out[0]], "adam_up")
    for i in range(3):
        big.append([res[kd][i][None] for kd in range(4)])

    loss = loss_sum[0, 0]

    def per_kind(kd):
        return [sm[kd][0], big[0][kd], sm[kd][2], sm[kd][3], sm[kd][5], sm[kd][4], big[1][kd], big[2][kd], big[3][kd],
                sm[kd][1]]

    return (loss, grad_x[None], *per_kind(0), *per_kind(1), *per_kind(2), *per_kind(3))
```

```python
import functools
import math

import jax
import jax.numpy as jnp
from jax import lax
from jax.experimental import pallas as pl
from jax.experimental.pallas import tpu as pltpu

F32 = jnp.float32
BF16 = jnp.bfloat16
MESH = pl.DeviceIdType.MESH

N_DEV = 8
N_HEADS = 8
HEAD_DIM = 64
D_SB = N_HEADS * HEAD_DIM
N_GROUPS = 8
GROUP_DIM = 64
D_SGU = N_GROUPS * GROUP_DIM
SGU_CHUNK = 128
CHUNK = 64
EPS = 1e-6
LANES = 128
N_PAIRS = N_HEADS // 2
QKV_COLS = 3 * D_SB
ATT_BLOCK = 256
ATT_TILE = 128
CARRY_FLOOR = -90.0
R_UNREACHED = -1e30

ADAM_LR = 0.001
ADAM_B1 = 0.9
ADAM_B2 = 0.999
ADAM_EPS = 1e-08
ADAM_WD = 0.01
ADAM_STEP = 10

VMEM_LIMIT = 56 * 1024 * 1024


def _cparams(sem=None, vmem=VMEM_LIMIT):
    return pltpu.CompilerParams(dimension_semantics=sem, vmem_limit_bytes=vmem)


def _dot(a, b):
    return jnp.dot(a, b, preferred_element_type=F32)


def _dot_nt(a, b):
    return lax.dot_general(a, b, (((1,), (1,)), ((), ())), preferred_element_type=F32)


def _dot_tn(a, b):
    return lax.dot_general(a, b, (((0,), (0,)), ((), ())), preferred_element_type=F32)


def _split_hi_lo(a):
    hi = a.astype(BF16)
    lo = (a - hi.astype(F32)).astype(BF16)
    return hi, lo


def _sigmoid(x):
    return 1.0 / (1.0 + jnp.exp(-x))


_GELU_C = math.sqrt(2.0 / math.pi)


def _gelu_and_grad(x):
    x2 = x * x
    inner = _GELU_C * (x + 0.044715 * (x2 * x))
    t = jnp.tanh(inner)
    cdf = 0.5 * (1.0 + t)
    g = x * cdf
    dg = cdf + x * (0.5 * (1.0 - t * t)) * (_GELU_C * (1.0 + 3.0 * 0.044715 * x2))
    return g, dg


def _coords():
    return lax.axis_index("x"), lax.axis_index("y"), lax.axis_index("c")


def _dev_index(px, py, pc):
    return 4 * px + 2 * py + pc


def _allgather(blocks, name):
    n = len(blocks)

    def body(*refs):
        gather = _Gather(refs[:n], refs[n:2 * n], *refs[2 * n:])
        gather.issue()
        gather.finish()

    any_spec = pl.BlockSpec(memory_space=pl.ANY)
    return pl.pallas_call(
        body, name=name,
        out_shape=_gather_out_shapes(blocks),
        in_specs=[any_spec] * n, out_specs=[any_spec] * n,
        scratch_shapes=_gather_semaphores(n),
    )(*blocks)


def _gather_out_shapes(blocks):
    return [jax.ShapeDtypeStruct((N_DEV,) + b.shape, b.dtype) for b in blocks]


def _gather_semaphores(n):
    return [pltpu.SemaphoreType.DMA((n, 7)), pltpu.SemaphoreType.DMA((n, 7)), pltpu.SemaphoreType.DMA((n,))]


class _Gather:
    def __init__(self, ins, outs, send_sems, recv_sems, local_sems, relay=False):
        self.ins, self.outs = ins, outs
        self.send_sems, self.recv_sems, self.local_sems = send_sems, recv_sems, local_sems
        self.n = len(ins)
        self.relay = relay
        x, y, c = _coords()
        self.c = c
        self.me, self.sibling = (x, y, c), (x, y, 1 - c)
        self.chips = [(1 - x, y), (x, 1 - y), (1 - x, 1 - y)]

    def _copy(self, a, k, block, to, src=None):
        dst = self.outs[a].at[_dev_index(*block)]
        return pltpu.make_async_remote_copy(
            src_ref=dst if src is None else src, dst_ref=dst,
            send_sem=self.send_sems.at[a, k], recv_sem=self.recv_sems.at[a, k],
            device_id=to, device_id_type=MESH)

    def _mine(self):
        return [pltpu.make_async_copy(self.ins[a], self.outs[a].at[_dev_index(*self.me)], self.local_sems.at[a])
                for a in range(self.n)]

    def _first(self):
        first = []
        direct = self.chips[:2] if self.relay else self.chips
        for a in range(self.n):
            first.append(self._copy(a, 0, self.me, self.sibling, src=self.ins[a]))
            first += [self._copy(a, 1 + j, self.me, (*chip, self.c), src=self.ins[a])
                      for j, chip in enumerate(direct)]
        return first

    def issue(self):
        for cp in self._mine() + self._first():
            cp.start()

    def _pass_on(self, a, j):
        chip = self.chips[j]
        self._copy(a, 1 + j, (*chip, self.c), self.me).wait_recv()
        self._copy(a, 4 + j, (*chip, self.c), self.sibling).start()

    def finish(self):
        c = self.c
        if self.relay:
            for core in range(2):
                @pl.when(c == core)
                def _(core=core):
                    j_src, j_dst = core, 1 - core
                    for a in range(self.n):
                        self._pass_on(a, j_src)
                        self._copy(a, 3, (*self.chips[j_src], c), (*self.chips[j_dst], c)).start()
                    for a in range(self.n):
                        self._pass_on(a, j_dst)
                    for a in range(self.n):
                        self._pass_on(a, 2)
        else:
            for j in range(3):
                for a in range(self.n):
                    self._pass_on(a, j)
        for a in range(self.n):
            self._copy(a, 0, self.sibling, self.me).wait_recv()
            for j, chip in enumerate(self.chips):
                self._copy(a, 4 + j, (*chip, 1 - c), self.me).wait_recv()
        for a in range(self.n):
            for k in range(7):
                self._copy(a, k, self.me, self.sibling).wait_send()
        for cp in self._mine():
            cp.wait()


def _push_sibling(arrs, name):
    n = len(arrs)

    def body(*refs):
        ins, outs = refs[:n], refs[n:2 * n]
        send_sems, recv_sems = refs[2 * n:]
        x, y, c = _coords()
        sibling = (x, y, 1 - c)
        copies = []
        for a in range(n):
            for k in range(4):
                copies.append(pltpu.make_async_remote_copy(
                    src_ref=ins[a].at[k, 1 - c], dst_ref=outs[a].at[k],
                    send_sem=send_sems.at[a, k], recv_sem=recv_sems.at[a, k],
                    device_id=sibling, device_id_type=MESH))
        for cp in copies:
            cp.start()
        for cp in copies:
            cp.wait()

    any_spec = pl.BlockSpec(memory_space=pl.ANY)
    return pl.pallas_call(
        body, name=name,
        out_shape=[jax.ShapeDtypeStruct((4,) + a.shape[2:], a.dtype) for a in arrs],
        in_specs=[any_spec] * n, out_specs=[any_spec] * n,
        scratch_shapes=[pltpu.SemaphoreType.DMA((n, 4)), pltpu.SemaphoreType.DMA((n, 4))],
    )(*arrs)


def _chip_push_copies(ins, outs, send_sems, recv_sems):
    x, y, c = _coords()
    chips = [(1 - x, y), (x, 1 - y), (1 - x, 1 - y)]
    return [pltpu.make_async_remote_copy(
        src_ref=ins[a].at[2 * px + py], dst_ref=outs[a].at[r],
        send_sem=send_sems.at[a, r], recv_sem=recv_sems.at[a, r],
        device_id=(px, py, c), device_id_type=MESH)
        for a in range(len(ins)) for r, (px, py) in enumerate(chips)]


def _direct_push_copies(ins, outs, send_sems, recv_sems):
    x, y, c = _coords()
    copies = []
    for a in range(len(ins)):
        for r in range(1, N_DEV):
            px = 1 - x if r & 4 else x
            py = 1 - y if r & 2 else y
            pc = 1 - c if r & 1 else c
            copies.append(pltpu.make_async_remote_copy(
                src_ref=ins[a].at[_dev_index(px, py, pc)], dst_ref=outs[a].at[r - 1],
                send_sem=send_sems.at[a, r - 1], recv_sem=recv_sems.at[a, r - 1],
                device_id=(px, py, pc), device_id_type=MESH))
    return copies


def _chip_partial_sums(owns, lands, core, name):
    n = len(owns)
    blocks = [a.shape[2:] for a in owns]

    def body(core_ref, *refs):
        del core_ref
        for own_ref, land_ref, out_ref in zip(refs[:n], refs[n:2 * n], refs[2 * n:]):
            out_ref[...] = (own_ref[...].astype(F32) + land_ref[...].astype(F32)).astype(out_ref.dtype)

    return pl.pallas_call(
        body, name=name,
        out_shape=[jax.ShapeDtypeStruct((4,) + b, a.dtype) for a, b in zip(owns, blocks)],
        grid_spec=pltpu.PrefetchScalarGridSpec(
            num_scalar_prefetch=1, grid=(4,),
            in_specs=[pl.BlockSpec((None, None) + b, lambda k, core: (k, core[0], 0, 0)) for b in blocks]
            + [pl.BlockSpec((None,) + b, lambda k, core: (k, 0, 0)) for b in blocks],
            out_specs=[pl.BlockSpec((None,) + b, lambda k, core: (k, 0, 0)) for b in blocks]),
        compiler_params=_cparams(("parallel",)),
    )(core, *owns, *lands)


def _adamw_math(w, g, m, v):
    m = ADAM_B1 * m + (1.0 - ADAM_B1) * g
    v = ADAM_B2 * v + (1.0 - ADAM_B2) * (g * g)
    m_hat = m / (1.0 - ADAM_B1 ** ADAM_STEP)
    v_hat = v / (1.0 - ADAM_B2 ** ADAM_STEP)
    delta = -ADAM_LR * (m_hat / (jnp.sqrt(v_hat) + ADAM_EPS) + ADAM_WD * w)
    return delta, m, v


def _adam_shard(cps, lands, chip, w, m, v, name):
    rows, cols = w.shape
    tr = rows // 4
    nparts = len(cps)

    def body(chip_ref, *refs):
        del chip_ref
        cp_refs, land_refs = refs[:nparts], refs[nparts:2 * nparts]
        w_ref, m_ref, v_ref, g_out, d_out, m_out, v_out = refs[2 * nparts:]
        parts = []
        for cp_ref, land_ref in zip(cp_refs, land_refs):
            g_k = cp_ref[...].astype(F32)
            for r in range(3):
                g_k = g_k + land_ref[r].astype(F32)
            parts.append(g_k)
        g = parts[0] if nparts == 1 else jnp.concatenate(parts, axis=1)
        delta, m_new, v_new = _adamw_math(w_ref[...], g, m_ref[...], v_ref[...])
        g_out[...] = g
        d_out[...] = delta
        m_out[...] = m_new
        v_out[...] = v_new

    tile = pl.BlockSpec((tr, cols), lambda r, chip: (r, 0))
    out = jax.ShapeDtypeStruct((rows, cols), F32)
    return pl.pallas_call(
        body, name=name, out_shape=[out] * 4,
        grid_spec=pltpu.PrefetchScalarGridSpec(
            num_scalar_prefetch=1, grid=(rows // tr,),
            in_specs=[pl.BlockSpec((None, tr, a.shape[2]), lambda r, chip: (chip[0], r, 0)) for a in cps]
            + [pl.BlockSpec((3, tr, a.shape[2]), lambda r, chip: (0, r, 0)) for a in lands]
            + [tile, tile, tile],
            out_specs=[tile] * 4),
        compiler_params=_cparams(("parallel",)),
    )(chip, *cps, *lands, w, m, v)


def _adam_shards_whole(owns, lands, dev, ws, ms, vs, name):
    n = len(ws)

    def body(dev_ref, *refs):
        own_refs, land_refs = refs[:n], refs[n:2 * n]
        w_refs, m_refs, v_refs = refs[2 * n:3 * n], refs[3 * n:4 * n], refs[4 * n:5 * n]
        outs = refs[5 * n:]
        for i in range(n):
            g = own_refs[i][dev_ref[0]].astype(F32)
            for r in range(N_DEV - 1):
                g = g + land_refs[i][r].astype(F32)
            delta, m_new, v_new = _adamw_math(w_refs[i][...], g, m_refs[i][...], v_refs[i][...])
            outs[i][...] = g
            outs[n + i][...] = delta
            outs[2 * n + i][...] = m_new
            outs[3 * n + i][...] = v_new

    vmem = pl.BlockSpec(memory_space=pltpu.VMEM)
    res = pl.pallas_call(
        body, name=name, out_shape=[jax.ShapeDtypeStruct(w.shape, F32) for w in ws] * 4,
        in_specs=[pl.BlockSpec(memory_space=pltpu.SMEM)] + [vmem] * (5 * n), out_specs=[vmem] * (4 * n),
        compiler_params=_cparams(),
    )(dev, *owns, *lands, *ws, *ms, *vs)
    return [res[k * n:(k + 1) * n] for k in range(4)]


def _adam_small(parts, ws, ms, vs, loss_parts, name):
    n = len(ws)

    def body(*refs):
        p_refs, w_refs, m_refs, v_refs = refs[:n], refs[n:2 * n], refs[2 * n:3 * n], refs[3 * n:4 * n]
        loss_ref, outs, loss_out = refs[4 * n], refs[4 * n + 1:-1], refs[-1]
        total = loss_ref[0]
        for dev in range(1, N_DEV):
            total = total + loss_ref[dev]
        loss_out[...] = total
        for i in range(n):
            g = p_refs[i][0]
            for dev in range(1, N_DEV):
                g = g + p_refs[i][dev]
            if g.shape[0] != w_refs[i].shape[0]:
                g = jnp.sum(g, axis=0, keepdims=True)
            delta, m_new, v_new = _adamw_math(w_refs[i][...], g, m_refs[i][...], v_refs[i][...])
            outs[i][...] = g
            outs[n + i][...] = delta
            outs[2 * n + i][...] = m_new
            outs[3 * n + i][...] = v_new

    out_shapes = [jax.ShapeDtypeStruct(w.shape, F32) for w in ws] * 4
    out_shapes.append(jax.ShapeDtypeStruct(loss_parts.shape[1:], F32))
    res = pl.pallas_call(body, name=name, out_shape=out_shapes, compiler_params=_cparams())(
        *parts, *ws, *ms, *vs, loss_parts)
    return [res[k * n:(k + 1) * n] for k in range(4)], res[-1]


def _rmsnorm_fwd(x, g, riders, name):
    s, d = x.shape
    ts = 512
    nt = s // ts
    nr = len(riders)

    def body(x_ref, g_ref, *rest):
        rider_in, (h_ref, ht_ref) = rest[:nr], rest[nr:nr + 2]
        rider_out, staged = rest[nr + 2:2 * nr + 2], rest[2 * nr + 2:3 * nr + 2]
        sems = rest[3 * nr + 2:]
        i = pl.program_id(0)
        gather = _Gather(staged, rider_out, *sems, relay=True)

        @pl.when(i == 0)
        def _():
            for src, dst in zip(rider_in, staged):
                dst[...] = src[...].astype(BF16)
            gather.issue()

        xv = x_ref[...]
        rstd = lax.rsqrt(jnp.mean(xv * xv, axis=-1, keepdims=True) + EPS)
        h = xv * rstd * g_ref[...]
        h_ref[...] = h.astype(BF16)
        ht_ref[...] = h.T.astype(BF16)

        @pl.when(i == nt - 1)
        def _():
            gather.finish()

    any_spec = pl.BlockSpec(memory_space=pl.ANY)
    res = pl.pallas_call(
        body, name=name,
        out_shape=[jax.ShapeDtypeStruct((s, d), BF16), jax.ShapeDtypeStruct((d, s), BF16)]
        + [jax.ShapeDtypeStruct((N_DEV,) + r.shape, BF16) for r in riders],
        grid=(nt,),
        in_specs=[pl.BlockSpec((ts, d), lambda i: (i, 0)), pl.BlockSpec((1, d), lambda i: (0, 0))]
        + [pl.BlockSpec(r.shape, lambda i: (0, 0)) for r in riders],
        out_specs=[pl.BlockSpec((ts, d), lambda i: (i, 0)), pl.BlockSpec((d, ts), lambda i: (0, i))]
        + [any_spec] * nr,
        scratch_shapes=[pltpu.VMEM(r.shape, BF16) for r in riders] + _gather_semaphores(nr),
        compiler_params=_cparams(("arbitrary",)),
    )(x, g, *riders)
    return res[0], res[1], res[2:]


def _inproj(h, wt, name):
    s, d = h.shape
    n = wt.shape[0]
    tn = 256
    n_qkv = QKV_COLS // tn

    def body(h_ref, w_ref, qkv_ref, rest_ref):
        j = pl.program_id(0)
        res = _dot_nt(h_ref[...], w_ref[...])

        @pl.when(j < n_qkv)
        def _():
            qkv_ref[...] = res.astype(BF16)

        @pl.when(j >= n_qkv)
        def _():
            rest_ref[...] = res

    return pl.pallas_call(
        body, name=name,
        out_shape=[jax.ShapeDtypeStruct((s, QKV_COLS), BF16), jax.ShapeDtypeStruct((s, n - QKV_COLS), F32)],
        grid=(n // tn,),
        in_specs=[pl.BlockSpec((s, d), lambda j: (0, 0)), pl.BlockSpec((tn, d), lambda j: (j, 0))],
        out_specs=[pl.BlockSpec((s, tn), lambda j: (0, jnp.minimum(j, n_qkv - 1))),
                   pl.BlockSpec((s, tn), lambda j: (0, jnp.maximum(j - n_qkv, 0)))],
        compiler_params=_cparams(("arbitrary",)),
    )(h, wt)


def _dw_in(ht, dproj, name):
    d, s = ht.shape
    n = dproj.shape[1]
    tn = 512

    def body(a_ref, b_ref, o_ref):
        o_ref[...] = _dot(a_ref[...], b_ref[...]).T.astype(o_ref.dtype)

    return pl.pallas_call(
        body, name=name, out_shape=jax.ShapeDtypeStruct((n, d), BF16), grid=(n // tn,),
        in_specs=[pl.BlockSpec((d, s), lambda j: (0, 0)), pl.BlockSpec((s, tn), lambda j: (0, j))],
        out_specs=pl.BlockSpec((tn, d), lambda j: (j, 0)),
        compiler_params=_cparams(("parallel",)),
    )(ht, dproj)


def _dh_and_grad_x(dproj, wt, x, g, dx2, cps, name):
    s, n = dproj.shape
    d = wt.shape[1]
    tm = min(s, 1024)
    tk = next(t for t in (1408, 512) if n % t == 0)
    nk = n // tk
    nm = s // tm
    nc = len(cps)

    def body(a_ref, w_ref, x_ref, g_ref, dx2_ref, *rest):
        cp_refs, gx_ref, dg_ref = rest[:nc], rest[nc], rest[nc + 1]
        land_refs = rest[nc + 2:2 * nc + 2]
        acc_ref, send_sems, recv_sems = rest[2 * nc + 2:]
        i, k = pl.program_id(0), pl.program_id(1)

        @pl.when((i == 0) & (k == 0))
        def _():
            for cp in _chip_push_copies(cp_refs, land_refs, send_sems, recv_sems):
                cp.start()
            dg_ref[...] = jnp.zeros_like(dg_ref)

        @pl.when(k == 0)
        def _():
            acc_ref[...] = jnp.zeros_like(acc_ref)

        acc_ref[...] += _dot(a_ref[...], w_ref[...])

        @pl.when(k == nk - 1)
        def _():
            dh = acc_ref[...]
            xv = x_ref[...]
            rstd = lax.rsqrt(jnp.mean(xv * xv, axis=-1, keepdims=True) + EPS)
            xhat = xv * rstd
            dg_ref[...] += jnp.sum((dh * xhat).reshape(tm // 8, 8, d), axis=0)
            dxh = dh * g_ref[...]
            gx_ref[...] = dx2_ref[...] + rstd * (dxh - xhat * jnp.mean(dxh * xhat, axis=-1, keepdims=True))

        @pl.when((i == nm - 1) & (k == nk - 1))
        def _():
            for cp in _chip_push_copies(cp_refs, land_refs, send_sems, recv_sems):
                cp.wait()

    any_spec = pl.BlockSpec(memory_space=pl.ANY)
    res = pl.pallas_call(
        body, name=name,
        out_shape=[jax.ShapeDtypeStruct((s, d), F32), jax.ShapeDtypeStruct((8, d), F32)]
        + [jax.ShapeDtypeStruct((3,) + a.shape[1:], a.dtype) for a in cps],
        grid=(nm, nk),
        in_specs=[pl.BlockSpec((tm, tk), lambda i, k: (i, k)), pl.BlockSpec((tk, d), lambda i, k: (k, 0)),
                  pl.BlockSpec((tm, d), lambda i, k: (i, 0)), pl.BlockSpec((1, d), lambda i, k: (0, 0)),
                  pl.BlockSpec((tm, d), lambda i, k: (i, 0))] + [any_spec] * nc,
        out_specs=[pl.BlockSpec((tm, d), lambda i, k: (i, 0)), pl.BlockSpec((8, d), lambda i, k: (0, 0))]
        + [any_spec] * nc,
        scratch_shapes=[pltpu.VMEM((tm, d), F32), pltpu.SemaphoreType.DMA((nc, 3)), pltpu.SemaphoreType.DMA((nc, 3))],
        compiler_params=_cparams(("arbitrary", "arbitrary")),
    )(dproj, wt, x, g, dx2, *cps)
    return res[0], res[1], res[2:]


def _log_sigmoids(z):
    l1p = jnp.log(1.0 + jnp.exp(-jnp.abs(z)))
    ls = jnp.minimum(z, 0.0) - l1p
    return ls, ls - z


def _strict_lower_ones(n):
    row = lax.broadcasted_iota(jnp.int32, (n, n), 0)
    col = lax.broadcasted_iota(jnp.int32, (n, n), 1)
    return row, col, (row > col).astype(BF16)


def _attn_fwd(qkv, riders, name):
    s = qkv.shape[0]
    tb, tt = ATT_BLOCK, ATT_TILE
    nq = s // tb
    per_block = tb // tt
    assert per_block == 2 and s // tt <= LANES, "two query tiles per grid step; one lane of saved carry per key tile"

    nr = len(riders)

    def sweep_body(gather, q_ref, k_ref, v_ref, o_ref, rs_ref, extra_ref, acc_ref, r_ref, rsv_ref, rmax_ref):
        i = pl.program_id(1)
        lane = lax.broadcasted_iota(jnp.int32, (tt, LANES), 1)
        hmask = [lane < HEAD_DIM, lane >= HEAD_DIM]
        row, col, tri = _strict_lower_ones(tt)
        tri2 = jnp.concatenate([tri, tri], axis=0)
        below = col < row
        qrows = [slice(u * tt, (u + 1) * tt) for u in range(per_block)]
        qm = [[jnp.where(m, q_ref[qrows[u], :], jnp.zeros((), BF16)) * jnp.asarray(HEAD_DIM ** -0.5, BF16)
               for m in hmask] for u in range(per_block)]
        acc_ref[...] = jnp.zeros_like(acc_ref)
        r_ref[...] = jnp.zeros_like(r_ref)
        rsv_ref[...] = jnp.full_like(rsv_ref, R_UNREACHED)

        def sweep(tiles, chains):
            heads, nch = range(2), range(len(chains))
            kv = []
            for t in tiles:
                rows = pl.ds(pl.multiple_of(t * tt, tt), tt)
                kv.append((k_ref[rows, :], v_ref[rows, :]))
            z = [[_dot_nt(qm[u][h], kv[ti][0]) for h in heads] for u, ti, _ in chains]
            ls, lk = [], []
            for n in nch:
                pairs = [_log_sigmoids(z[n][h]) for h in heads]
                ls.append([p[0] for p in pairs])
                lk.append([jnp.where(below, p[1], 0.0) if chains[n][2] else p[1] for p in pairs])
            cur = {u: [r_ref[h, qrows[u], :] for h in heads] for u in sorted({c[0] for c in chains})}
            r = []
            for n, (u, _, _) in enumerate(chains):
                r.append(cur[u])
                cur[u] = [cur[u][h] + jnp.sum(lk[n][h], axis=1, keepdims=True) for h in heads]
            for u in cur:
                rmax_ref[u] = jnp.max(jnp.maximum(cur[u][0], cur[u][1]))
            suffix = [[_dot(jnp.concatenate(_split_hi_lo(lk[n][h]), axis=1), tri2) for h in heads] for n in nch]
            w = []
            for n in nch:
                w_n = [jnp.exp(ls[n][h] + suffix[n][h] + r[n][h]) for h in heads]
                if chains[n][2]:
                    w_n = [jnp.where(below, w_h, 0.0) for w_h in w_n]
                w.append([w_h.astype(BF16) for w_h in w_n])
            pv = [[_dot(w[n][h], kv[chains[n][1]][1]) for h in heads] for n in nch]
            for u in cur:
                mine = [n for n in nch if chains[n][0] == u]
                for h in heads:
                    acc_ref[h, qrows[u], :] += functools.reduce(lambda a, b: a + b, [pv[n][h] for n in mine])
                    for n in mine:
                        if not chains[n][2]:
                            t = tiles[chains[n][1]]
                            rsv_ref[h, qrows[u], :] = jnp.where(lane == t, r[n][h], rsv_ref[h, qrows[u], :])
                    r_ref[h, qrows[u], :] = cur[u][h]

        first = per_block * i

        @pl.when(i == 0)
        def _():
            sweep([0, 1], [(0, 0, True), (1, 1, True), (1, 0, False)])

        @pl.when(i > 0)
        def _():
            sweep([first - 1, first, first + 1], [(0, 1, True), (1, 2, True), (0, 0, False), (1, 1, False)])

        for u in range(per_block):
            n_left = first + u - 1

            def live(c, n_left=n_left):
                jj, rmax = c
                return (jj < n_left) & (rmax >= CARRY_FLOOR)

            def step(c, u=u, n_left=n_left):
                jj, _ = c
                sweep([n_left - 1 - jj], [(u, 0, False)])
                return jj + 1, rmax_ref[u]

            swept, _ = lax.while_loop(live, step, (jnp.int32(0), rmax_ref[u]))
            extra_ref[pl.program_id(0), first + u] = swept.astype(F32)

        lane_b = lax.broadcasted_iota(jnp.int32, (tb, LANES), 1)
        o_ref[...] = jnp.where(lane_b < HEAD_DIM, acc_ref[0], acc_ref[1])
        rs_ref[...] = rsv_ref[...]

        @pl.when((pl.program_id(0) == N_PAIRS - 1) & (i == nq - 1))
        def _():
            gather.finish()

    def body(q_ref, k_ref, v_ref, *rest):
        rider_in, (o_ref, rs_ref, extra_ref) = rest[:nr], rest[nr:nr + 3]
        rider_out = rest[nr + 3:2 * nr + 3]
        acc_ref, r_ref, rsv_ref, rmax_ref = rest[2 * nr + 3:2 * nr + 7]
        gather = _Gather(rider_in, rider_out, *rest[2 * nr + 7:])

        @pl.when((pl.program_id(0) == 0) & (pl.program_id(1) == 0))
        def _():
            gather.issue()

        sweep_body(gather, q_ref, k_ref, v_ref, o_ref, rs_ref, extra_ref, acc_ref, r_ref, rsv_ref, rmax_ref)

    any_spec = pl.BlockSpec(memory_space=pl.ANY)
    res = pl.pallas_call(
        body, name=name,
        out_shape=[jax.ShapeDtypeStruct((s, D_SB), F32), jax.ShapeDtypeStruct((N_HEADS, s, LANES), F32),
                   jax.ShapeDtypeStruct((N_PAIRS, s // tt), F32)] + _gather_out_shapes(riders),
        grid=(N_PAIRS, nq),
        in_specs=[pl.BlockSpec((tb, LANES), lambda p, i: (i, p)),
                  pl.BlockSpec((s, LANES), lambda p, i: (0, N_PAIRS + p)),
                  pl.BlockSpec((s, LANES), lambda p, i: (0, 2 * N_PAIRS + p))] + [any_spec] * nr,
        out_specs=[pl.BlockSpec((tb, LANES), lambda p, i: (i, p)),
                   pl.BlockSpec((2, tb, LANES), lambda p, i: (p, i, 0)),
                   pl.BlockSpec(memory_space=pltpu.SMEM)] + [any_spec] * nr,
        scratch_shapes=[pltpu.VMEM((2, tb, LANES), F32), pltpu.VMEM((2, tb, LANES), F32),
                        pltpu.VMEM((2, tb, LANES), F32), pltpu.SMEM((per_block,), F32)] + _gather_semaphores(nr),
        compiler_params=_cparams(("arbitrary", "arbitrary")),
    )(qkv, qkv, qkv, *riders)
    return res[0], res[1], res[2], res[3:]


def _attn_bwd(qkv, do, rs, extra, dproj, smalls, cps, name):
    s = qkv.shape[0]
    tb, tt = ATT_BLOCK, ATT_TILE
    nq = s // tb
    per_block = tb // tt
    scale = HEAD_DIM ** -0.5

    ns, nc = len(smalls), len(cps)

    def sweep_body(q_ref, k_ref, v_ref, do_ref, rs_ref, extra_ref, dproj_hbm, out_hbm, dq_acc, dk_acc, dv_acc,
                   dqi_ref, pc_ref, stage_ref, out_sems):
        del dproj_hbm
        pair = pl.program_id(0)
        lane = lax.broadcasted_iota(jnp.int32, (tt, LANES), 1)
        hmask = [lane < HEAD_DIM, lane >= HEAD_DIM]
        row, col, tri = _strict_lower_ones(tt)
        tri2 = jnp.concatenate([tri, tri], axis=0)
        triu = (row < col).astype(BF16)
        below = col < row
        zero = jnp.zeros((), BF16)
        qrows = [slice(u * tt, (u + 1) * tt) for u in range(per_block)]
        dq_acc[...] = jnp.zeros_like(dq_acc)
        dk_acc[...] = jnp.zeros_like(dk_acc)
        dv_acc[...] = jnp.zeros_like(dv_acc)

        def qblock(i, carry):
            block_rows = pl.ds(pl.multiple_of(i * tb, tb), tb)
            q2 = q_ref[block_rows, :]
            do2 = do_ref[block_rows, :]
            qm = [[jnp.where(m, q2[qrows[u]], zero) * jnp.asarray(scale, BF16) for m in hmask]
                  for u in range(per_block)]
            dom = [[jnp.where(m, do2[qrows[u]], zero) for m in hmask] for u in range(per_block)]
            rs_i = [[rs_ref[h, pl.ds(pl.multiple_of(i * tb + u * tt, tt), tt), :] for h in range(2)]
                    for u in range(per_block)]
            dqi_ref[...] = jnp.zeros_like(dqi_ref)
            pc_ref[...] = jnp.zeros_like(pc_ref)
            first = per_block * i

            def sweep(tiles, chains):
                heads, nch = range(2), range(len(chains))
                krows = [pl.ds(pl.multiple_of(t * tt, tt), tt) for t in tiles]
                k2 = [k_ref[rows, :] for rows in krows]
                v2 = [v_ref[rows, :] for rows in krows]
                z = [[_dot_nt(qm[u][h], k2[ti]) for h in heads] for u, ti, _ in chains]
                dw = [[_dot_nt(dom[u][h], v2[ti]) for h in heads] for u, ti, _ in chains]
                ls, lk = [], []
                for n in nch:
                    pairs = [_log_sigmoids(z[n][h]) for h in heads]
                    ls.append([p[0] for p in pairs])
                    lk.append([jnp.where(below, p[1], 0.0) if chains[n][2] else p[1] for p in pairs])
                suffix = [[_dot(jnp.concatenate(_split_hi_lo(lk[n][h]), axis=1), tri2) for h in heads] for n in nch]
                w, g = [], []
                for n, (u, ti, diag) in enumerate(chains):
                    w_n = []
                    for h in heads:
                        logw = ls[n][h] + suffix[n][h]
                        if diag:
                            w_n.append(jnp.where(below, jnp.exp(logw), 0.0))
                        else:
                            carry_in = jnp.sum(jnp.where(lane == tiles[ti], rs_i[u][h], 0.0), axis=1, keepdims=True)
                            w_n.append(jnp.exp(logw + carry_in))
                    w.append([w_h.astype(BF16) for w_h in w_n])
                    g.append([w_n[h] * dw[n][h] for h in heads])
                prefix = [[_dot(g[n][h].astype(BF16), triu) for h in heads] for n in nch]
                cur = {u: [pc_ref[h, qrows[u], :] for h in heads] for u in sorted({c[0] for c in chains})}
                pc = []
                for n, (u, _, _) in enumerate(chains):
                    pc.append(cur[u])
                    cur[u] = [cur[u][h] + jnp.sum(g[n][h], axis=1, keepdims=True) for h in heads]
                dzb = []
                for n in nch:
                    dz_n = []
                    for h in heads:
                        sig = jnp.exp(ls[n][h])
                        dz = g[n][h] - sig * (g[n][h] + prefix[n][h] + pc[n][h])
                        if chains[n][2]:
                            dz = jnp.where(below, dz, 0.0)
                        dz_n.append(dz.astype(BF16))
                    dzb.append(dz_n)
                for u in cur:
                    for h in heads:
                        pc_ref[h, qrows[u], :] = cur[u][h]
                dq = [[_dot(dzb[n][h], jnp.where(hmask[h], k2[chains[n][1]], zero)) for h in heads] for n in nch]
                dk = [[_dot_tn(dzb[n][h], qm[chains[n][0]][h]) for h in heads] for n in nch]
                dv = [[_dot_tn(w[n][h], dom[chains[n][0]][h]) for h in heads] for n in nch]
                add = lambda a, b: a + b
                for u in cur:
                    dqi_ref[qrows[u], :] += functools.reduce(
                        add, [dq[n][h] for n in nch if chains[n][0] == u for h in heads])
                for ti in range(len(tiles)):
                    mine = [n for n in nch if chains[n][1] == ti]
                    dk_acc[krows[ti], :] += functools.reduce(add, [dk[n][h] for n in mine for h in heads])
                    dv_acc[krows[ti], :] += functools.reduce(add, [dv[n][h] for n in mine for h in heads])

            for u in range(per_block):
                n_left = first + u - 1
                n_extra = jnp.clip(extra_ref[pair, first + u].astype(jnp.int32), 0, jnp.maximum(n_left, 0))

                def step(t, c, u=u):
                    sweep([t], [(u, 0, False)])
                    return c

                lax.fori_loop(n_left - n_extra, n_left, step, 0)

            @pl.when(i == 0)
            def _():
                sweep([0, 1], [(0, 0, True), (1, 0, False), (1, 1, True)])

            @pl.when(i > 0)
            def _():
                sweep([first - 1, first, first + 1], [(0, 0, False), (1, 1, False), (0, 1, True), (1, 2, True)])

            dq_acc[block_rows, :] += dqi_ref[...] * scale
            return carry

        lax.fori_loop(0, nq, qblock, 0)
        copies = []
        for t, acc in enumerate((dq_acc, dk_acc, dv_acc)):
            stage_ref[t] = acc[...].astype(BF16)
            col0 = pl.multiple_of(t * D_SB + pair * LANES, LANES)
            copies.append(pltpu.make_async_copy(stage_ref.at[t], out_hbm.at[:, pl.ds(col0, LANES)], out_sems.at[t]))
        for cp in copies:
            cp.start()
        for cp in copies:
            cp.wait()

    def body(q_ref, k_ref, v_ref, do_ref, rs_ref, extra_ref, dproj_hbm, *rest):
        small_in, cp_in = rest[:ns], rest[ns:ns + nc]
        out_hbm = rest[ns + nc]
        small_out, land_out = rest[ns + nc + 1:2 * ns + nc + 1], rest[2 * ns + nc + 1:2 * (ns + nc) + 1]
        scratch = rest[2 * (ns + nc) + 1:]
        gather = _Gather(small_in, small_out, *scratch[7:10])
        pair = pl.program_id(0)

        @pl.when(pair == 0)
        def _():
            gather.issue()
            for cp in _direct_push_copies(cp_in, land_out, *scratch[10:]):
                cp.start()

        sweep_body(q_ref, k_ref, v_ref, do_ref, rs_ref, extra_ref, dproj_hbm, out_hbm, *scratch[:7])

        @pl.when(pair == N_PAIRS - 1)
        def _():
            gather.finish()
            for cp in _direct_push_copies(cp_in, land_out, *scratch[10:]):
                cp.wait()

    any_spec = pl.BlockSpec(memory_space=pl.ANY)
    res = pl.pallas_call(
        body, name=name,
        out_shape=[jax.ShapeDtypeStruct(dproj.shape, BF16)] + _gather_out_shapes(smalls)
        + [jax.ShapeDtypeStruct((N_DEV - 1,) + a.shape[1:], a.dtype) for a in cps],
        grid=(N_PAIRS,),
        in_specs=[pl.BlockSpec((s, LANES), lambda p: (0, p)),
                  pl.BlockSpec((s, LANES), lambda p: (0, N_PAIRS + p)),
                  pl.BlockSpec((s, LANES), lambda p: (0, 2 * N_PAIRS + p)),
                  pl.BlockSpec((s, LANES), lambda p: (0, p)),
                  pl.BlockSpec((2, s, LANES), lambda p: (p, 0, 0)),
                  pl.BlockSpec(memory_space=pltpu.SMEM),
                  any_spec] + [any_spec] * (ns + nc),
        out_specs=[any_spec] * (1 + ns + nc),
        scratch_shapes=[pltpu.VMEM((s, LANES), F32), pltpu.VMEM((s, LANES), F32), pltpu.VMEM((s, LANES), F32),
                        pltpu.VMEM((tb, LANES), F32), pltpu.VMEM((2, tb, LANES), F32),
                        pltpu.VMEM((3, s, LANES), BF16), pltpu.SemaphoreType.DMA((3,))]
        + _gather_semaphores(ns)
        + [pltpu.SemaphoreType.DMA((nc, N_DEV - 1)), pltpu.SemaphoreType.DMA((nc, N_DEV - 1))],
        input_output_aliases={6: 0},
        compiler_params=_cparams(("arbitrary",)),
    )(qkv, qkv, qkv, do, rs, extra, dproj, *smalls, *cps)
    return res[0], res[1:1 + ns], res[1 + ns:]


def _mid(o, rest, x, tgt, wua, wub, wout, fg, lng, lnb, wsp, bfull, gavg, name):
    s, d = x.shape
    ts = 256
    nt = s // ts
    nchunk = ts // SGU_CHUNK
    n_rest = rest.shape[1]

    def body(o_ref, rest_ref, x_ref, t_ref, wua_ref, wub_ref, wout_ref, fg_ref, lng_ref, lnb_ref, wsp_ref, bfull_ref,
             gavg_ref, loss_ref, dx2_ref, do_ref, dproj_ref, dwua_out, dwub_out, dwout_out, dfg_ref, dlng_ref,
             dlnb_ref, dwsp_ref, dbfull_ref, dwua_ref, dwub_ref, dwout_ref):
        step = pl.program_id(0)

        @pl.when(step == 0)
        def _():
            for ref in (loss_ref, dwua_ref, dwub_ref, dwout_ref, dfg_ref, dlng_ref, dlnb_ref, dwsp_ref, dbfull_ref):
                ref[...] = jnp.zeros_like(ref)

        gavg = gavg_ref[...]

        def gmean(a):
            return _dot(a.astype(BF16), gavg)

        def colsum8(a):
            return jnp.sum(a.reshape(ts // 8, 8, a.shape[1]), axis=0)

        z_a = rest_ref[:, 0:512]
        u_b = rest_ref[:, 512:1024]
        v_b = rest_ref[:, 1024:1536]
        z_b = rest_ref[:, 1536:2048]
        g_a = rest_ref[:, 2048:2048 + d]
        g_b = rest_ref[:, 2048 + d:2048 + 2 * d]
        ov = o_ref[...]
        sa = _sigmoid(z_a)
        silu_a = z_a * sa
        y_a = ov * silu_a
        ug, dug_du = _gelu_and_grad(u_b)
        vg, dvg_dv = _gelu_and_grad(v_b)
        mu = gmean(vg)
        cen = vg - mu
        rstd_g = lax.rsqrt(gmean(cen * cen) + EPS)
        vhat = cen * rstd_g
        vn = vhat * lng_ref[...] + lnb_ref[...]
        vnb = vn.astype(BF16)

        t_idx = lax.broadcasted_iota(jnp.int32, (SGU_CHUNK, SGU_CHUNK), 0)
        s_idx = lax.broadcasted_iota(jnp.int32, (SGU_CHUNK, SGU_CHUNK), 1)
        causal = (s_idx // CHUNK) <= (t_idx // CHUNK)
        wm = [jnp.where(causal, wsp_ref[g], 0.0) for g in range(N_GROUPS)]
        wmb = [w.astype(BF16) for w in wm]
        wmtb = [w.T.astype(BF16) for w in wm]
        lane = lax.broadcasted_iota(jnp.int32, (SGU_CHUNK, LANES), 1)
        first = lane < GROUP_DIM
        bfull = bfull_ref[...]

        mixed_rows = []
        for n in range(nchunk):
            r0, r1 = n * SGU_CHUNK, (n + 1) * SGU_CHUNK
            pieces = []
            for p in range(N_GROUPS // 2):
                blk = vnb[r0:r1, p * LANES:(p + 1) * LANES]
                pieces.append(jnp.where(first, _dot(wmb[2 * p], blk), _dot(wmb[2 * p + 1], blk)))
            mixed_rows.append(jnp.concatenate(pieces, axis=1) + bfull)
        mixed = jnp.concatenate(mixed_rows, axis=0)
        sg = ug * mixed
        sb = _sigmoid(z_b)
        silu_b = z_b * sb
        y_b = sg * silu_b
        y_ab = y_a.astype(BF16)
        y_bb = y_b.astype(BF16)
        p_a = _dot(y_ab, wua_ref[...])
        p_b = _dot(y_bb, wub_ref[...])
        ga_s = _sigmoid(g_a)
        gb_s = _sigmoid(g_b)
        merged_b = (ga_s * p_a + gb_s * p_b).astype(BF16)
        x2 = x_ref[...] + _dot(merged_b, wout_ref[...])
        rstd = lax.rsqrt(jnp.mean(x2 * x2, axis=-1, keepdims=True) + EPS)
        xhat = x2 * rstd
        fg_v = fg_ref[...]
        diff = xhat * fg_v - t_ref[...]
        loss_ref[...] += 0.5 * jnp.sum(jnp.sum(diff * diff, axis=-1, keepdims=True) * (1.0 / d))

        dy = diff * (1.0 / d)
        dfg_ref[...] += colsum8(dy * xhat)
        dxh = dy * fg_v
        dx2 = rstd * (dxh - xhat * jnp.mean(dxh * xhat, axis=-1, keepdims=True))
        dx2_ref[...] = dx2
        dx2b = dx2.astype(BF16)
        dwout_ref[...] += _dot_tn(merged_b, dx2b)
        dmerged = _dot_nt(dx2b, wout_ref[...])
        dp_a = dmerged * ga_s
        dp_b = dmerged * gb_s
        dproj_ref[:, QKV_COLS + 2048:QKV_COLS + 2048 + d] = (dmerged * p_a * (ga_s * (1.0 - ga_s))).astype(BF16)
        dproj_ref[:, QKV_COLS + 2048 + d:QKV_COLS + 2048 + 2 * d] = (dmerged * p_b * (gb_s * (1.0 - gb_s))).astype(BF16)
        dp_ab = dp_a.astype(BF16)
        dp_bb = dp_b.astype(BF16)
        dwua_ref[...] += _dot_tn(y_ab, dp_ab)
        dwub_ref[...] += _dot_tn(y_bb, dp_bb)
        dy_a = _dot_nt(dp_ab, wua_ref[...])
        dy_b = _dot_nt(dp_bb, wub_ref[...])
        do_ref[...] = (dy_a * silu_a).astype(BF16)
        dproj_ref[:, QKV_COLS:QKV_COLS + 512] = (dy_a * ov * (sa * (1.0 + z_a * (1.0 - sa)))).astype(BF16)
        dsg = dy_b * silu_b
        dproj_ref[:, QKV_COLS + 1536:QKV_COLS + 2048] = (dy_b * sg * (sb * (1.0 + z_b * (1.0 - sb)))).astype(BF16)
        dproj_ref[:, QKV_COLS + 512:QKV_COLS + 1024] = (dsg * mixed * dug_du).astype(BF16)
        dmixed = dsg * ug
        dmb = dmixed.astype(BF16)
        zero = jnp.zeros((), BF16)
        dvn_rows = []
        db = jnp.zeros((SGU_CHUNK, D_SGU), F32)
        for n in range(nchunk):
            r0, r1 = n * SGU_CHUNK, (n + 1) * SGU_CHUNK
            db = db + dmixed[r0:r1, :]
            pieces = []
            for p in range(N_GROUPS // 2):
                cols = slice(p * LANES, (p + 1) * LANES)
                dm_blk = dmb[r0:r1, cols]
                vn_blk = vnb[r0:r1, cols]
                dwsp_ref[2 * p] += _dot_nt(jnp.where(first, dm_blk, zero), vn_blk)
                dwsp_ref[2 * p + 1] += _dot_nt(jnp.where(first, zero, dm_blk), vn_blk)
                pieces.append(jnp.where(first, _dot(wmtb[2 * p], dm_blk), _dot(wmtb[2 * p + 1], dm_blk)))
            dvn_rows.append(jnp.concatenate(pieces, axis=1))
        dbfull_ref[...] += db
        dvn = jnp.concatenate(dvn_rows, axis=0)
        dlng_ref[...] += colsum8(dvn * vhat)
        dlnb_ref[...] += colsum8(dvn)
        dvhat = dvn * lng_ref[...]
        dcen = rstd_g * (dvhat - gmean(dvhat) - vhat * gmean(dvhat * vhat))
        dproj_ref[:, QKV_COLS + 1024:QKV_COLS + 1536] = (dcen * dvg_dv).astype(BF16)

        @pl.when(step == nt - 1)
        def _():
            for g in range(N_GROUPS):
                dwsp_ref[g] = jnp.where(causal, dwsp_ref[g], 0.0)
            cb = d // N_DEV
            for k in range(N_DEV):
                dwua_out[k] = dwua_ref[:, k * cb:(k + 1) * cb].astype(BF16)
                dwub_out[k] = dwub_ref[:, k * cb:(k + 1) * cb].astype(BF16)
            dwout_out[...] = dwout_ref[...].astype(BF16)

    def tile(cols):
        return pl.BlockSpec((ts, cols), lambda i: (i, 0))

    def whole(shape):
        return pl.BlockSpec(shape, lambda i: (0,) * len(shape))

    out_shapes = [
        jax.ShapeDtypeStruct((8, LANES), F32),
        jax.ShapeDtypeStruct((s, d), F32),
        jax.ShapeDtypeStruct((s, D_SB), BF16),
        jax.ShapeDtypeStruct((s, QKV_COLS + n_rest), BF16),
        jax.ShapeDtypeStruct((N_DEV, D_SB, d // N_DEV), BF16),
        jax.ShapeDtypeStruct((N_DEV, D_SGU, d // N_DEV), BF16),
        jax.ShapeDtypeStruct((d, d), BF16),
        jax.ShapeDtypeStruct((8, d), F32),
        jax.ShapeDtypeStruct((8, D_SGU), F32),
        jax.ShapeDtypeStruct((8, D_SGU), F32),
        jax.ShapeDtypeStruct((N_GROUPS, SGU_CHUNK, SGU_CHUNK), F32),
        jax.ShapeDtypeStruct((SGU_CHUNK, D_SGU), F32),
    ]
    out_specs = [whole((8, LANES)), tile(d), tile(D_SB), tile(QKV_COLS + n_rest), whole((N_DEV, D_SB, d // N_DEV)),
                 whole((N_DEV, D_SGU, d // N_DEV)), whole((d, d)), whole((8, d)), whole((8, D_SGU)), whole((8, D_SGU)),
                 whole((N_GROUPS, SGU_CHUNK, SGU_CHUNK)), whole((SGU_CHUNK, D_SGU))]
    in_specs = [tile(D_SB), tile(n_rest), tile(d), tile(d), whole((D_SB, d)), whole((D_SGU, d)), whole((d, d)),
                whole((1, d)), whole((1, D_SGU)), whole((1, D_SGU)), whole((N_GROUPS, SGU_CHUNK, SGU_CHUNK)),
                whole((SGU_CHUNK, D_SGU)), whole((D_SGU, D_SGU))]
    return pl.pallas_call(
        body, name=name, out_shape=out_shapes, grid=(nt,), in_specs=in_specs, out_specs=out_specs,
        scratch_shapes=[pltpu.VMEM((D_SB, d), F32), pltpu.VMEM((D_SGU, d), F32), pltpu.VMEM((d, d), F32)],
        compiler_params=_cparams(("arbitrary",)),
    )(o, rest, x, tgt, wua, wub, wout, fg, lng, lnb, wsp, bfull, gavg)


def _small_reduce(dfg8, dlng8, dlnb8, dbfull, name):
    d = dfg8.shape[1]

    def body(dfg_ref, dlng_ref, dlnb_ref, dbfull_ref, fg_out, lng_out, lnb_out, b_out):
        row = lax.broadcasted_iota(jnp.int32, (D_SGU, LANES), 0)
        col = lax.broadcasted_iota(jnp.int32, (D_SGU, LANES), 1)

        def select_sum(a, sel):
            hi, lo = _split_hi_lo(a)
            lo2 = (a - hi.astype(F32) - lo.astype(F32)).astype(BF16)
            return _dot(hi, sel) + _dot(lo, sel) + _dot(lo2, sel)

        def by_group(partials):
            v = jnp.sum(partials, axis=0, keepdims=True)
            g_of_lane = lax.broadcasted_iota(jnp.int32, (N_GROUPS, D_SGU), 1) // GROUP_DIM
            g_of_row = lax.broadcasted_iota(jnp.int32, (N_GROUPS, D_SGU), 0)
            spread = jnp.where(g_of_lane == g_of_row, jnp.broadcast_to(v, (N_GROUPS, D_SGU)), 0.0)
            return select_sum(spread, (row % GROUP_DIM == col).astype(BF16))[:, 0:GROUP_DIM]

        fg_out[...] = jnp.sum(dfg_ref[...], axis=0, keepdims=True)
        lng_out[...] = by_group(dlng_ref[...])
        lnb_out[...] = by_group(dlnb_ref[...])
        by_pos = select_sum(dbfull_ref[...], (row // GROUP_DIM == col).astype(BF16))
        b_out[...] = by_pos.T[0:N_GROUPS, :]

    return pl.pallas_call(
        body, name=name,
        out_shape=[jax.ShapeDtypeStruct((1, d), F32), jax.ShapeDtypeStruct((N_GROUPS, GROUP_DIM), F32),
                   jax.ShapeDtypeStruct((N_GROUPS, GROUP_DIM), F32), jax.ShapeDtypeStruct((N_GROUPS, SGU_CHUNK), F32)],
        compiler_params=_cparams(),
    )(dfg8, dlng8, dlnb8, dbfull)


def _from_block_major_cols(w):
    nb, r, c = w.shape
    return jnp.transpose(w, (1, 0, 2)).reshape(r, nb * c)


def kernel(x, norm_g, w_in, sgu_ln_g, sgu_ln_b, w_spatial, b_spatial, w_up_a, w_up_b, w_out, final_norm_g, loss_target, m_norm_g, m_w_in, m_sgu_ln_g, m_sgu_ln_b, m_w_spatial, m_b_spatial, m_w_up_a, m_w_up_b, m_w_out, m_final_norm_g, v_norm_g, v_w_in, v_sgu_ln_g, v_sgu_ln_b, v_w_spatial, v_b_spatial, v_w_up_a, v_w_up_b, v_w_out, v_final_norm_g):
    s, d = x.shape[1], x.shape[2]
    xs = x[0]
    tgt = loss_target[0]
    cx, cy, cc = _coords()
    core = jnp.reshape(cc, (1,)).astype(jnp.int32)
    chip = jnp.reshape(2 * cx + cy, (1,)).astype(jnp.int32)

    h, ht, (g_win,) = _rmsnorm_fwd(xs, norm_g, [jnp.transpose(w_in[0])], "norm")
    d_in = N_DEV * w_in.shape[2]
    w_full = g_win.reshape(d_in, d)
    qkv, rest = _inproj(h, w_full, "inproj")
    o, rs, extra, (g_wua, g_wub, g_wout) = _attn_fwd(
        qkv, [w_up_a[0].astype(BF16), w_up_b[0].astype(BF16), w_out[0].astype(BF16)], "attn_fwd")
    wua_full = _from_block_major_cols(g_wua)
    wub_full = _from_block_major_cols(g_wub)
    wout_full = g_wout.reshape(d, d)

    lng = sgu_ln_g.reshape(1, D_SGU)
    lnb = sgu_ln_b.reshape(1, D_SGU)
    bfull = jnp.repeat(jnp.transpose(b_spatial[0]), GROUP_DIM, axis=1)
    grp = jnp.arange(D_SGU) // GROUP_DIM
    gavg = jnp.where(grp[:, None] == grp[None, :], 1.0 / GROUP_DIM, 0.0).astype(BF16)
    (loss_b, dx2, do, dproj, dwua, dwub, dwout, dfg8, dlng8, dlnb8, dwsp, dbfull) = _mid(
        o, rest, xs, tgt, wua_full, wub_full, wout_full, final_norm_g.reshape(1, d), lng, lnb, w_spatial[0], bfull,
        gavg, "mid")

    own_up = [dwua, dwub, dwout.reshape(N_DEV, d // N_DEV, d)]
    dfg, dlng, dlnb, db = _small_reduce(dfg8, dlng8, dlnb8, dbfull, "small_reduce")
    wsp_rows = N_GROUPS * SGU_CHUNK
    dproj, small_parts, land_up = _attn_bwd(
        qkv, do, rs, extra, dproj, [dfg, dlng, dlnb, db, dwsp.reshape(wsp_rows, SGU_CHUNK)], own_up, "attn_bwd")
    dwin = _dw_in(ht, dproj, "dwin")

    own_in = [dwin.reshape(4, 2, d_in // N_DEV, d)]
    cps_in = list(_chip_partial_sums(own_in, list(_push_sibling(own_in, "rs_sibling_in")), core, "cpsum_in"))
    grad_x, dng8, land_in = _dh_and_grad_x(dproj, w_full, xs, norm_g, dx2, cps_in, "dh")

    ng_parts, loss_parts = _allgather([dng8, loss_b], "ag_tail")

    def small_layouts(ng, fg, lg, lb, bs, ws):
        return [ng.reshape(1, d), fg.reshape(1, d), lg.reshape(N_GROUPS, GROUP_DIM), lb.reshape(N_GROUPS, GROUP_DIM),
                bs.reshape(N_GROUPS, SGU_CHUNK), ws.reshape(wsp_rows, SGU_CHUNK)]

    sm, loss_sum = _adam_small(
        [ng_parts] + list(small_parts),
        small_layouts(norm_g, final_norm_g, sgu_ln_g, sgu_ln_b, b_spatial, w_spatial),
        small_layouts(m_norm_g, m_final_norm_g, m_sgu_ln_g, m_sgu_ln_b, m_b_spatial, m_w_spatial),
        small_layouts(v_norm_g, v_final_norm_g, v_sgu_ln_g, v_sgu_ln_b, v_b_spatial, v_w_spatial),
        loss_parts, "adam_small")
    small_shapes = [norm_g.shape, final_norm_g.shape, sgu_ln_g.shape, sgu_ln_b.shape, b_spatial.shape, w_spatial.shape]
    sm = [[a.reshape(shp) for a, shp in zip(kind, small_shapes)] for kind in sm]

    res = _adam_shard(cps_in, list(land_in), chip, *[jnp.transpose(a[0]) for a in (w_in, m_w_in, v_w_in)], "adam0")
    big = [[jnp.transpose(r)[None] for r in res]]
    dev = jnp.reshape(_dev_index(cx, cy, cc), (1,)).astype(jnp.int32)
    res = _adam_shards_whole(own_up, list(land_up), dev, [w_up_a[0], w_up_b[0], w_out[0]],
                             [m_w_up_a[0], m_w_up_b[0], m_w_out[0]], [v_w_up_a[0], v_w_up_b[0], v_w_out[0]], "adam_up")
    for i in range(3):
        big.append([res[kd][i][None] for kd in range(4)])

    loss = loss_sum[0, 0]

    def per_kind(kd):
        return [sm[kd][0], big[0][kd], sm[kd][2], sm[kd][3], sm[kd][5], sm[kd][4], big[1][kd], big[2][kd], big[3][kd],
                sm[kd][1]]

    return (loss, grad_x[None], *per_kind(0), *per_kind(1), *per_kind(2), *per_kind(3))
```

```python
import functools
import math

import jax
import jax.numpy as jnp
from jax import lax
from jax.experimental import pallas as pl
from jax.experimental.pallas import tpu as pltpu

F32 = jnp.float32
BF16 = jnp.bfloat16
MESH = pl.DeviceIdType.MESH

N_DEV = 8
N_HEADS = 8
HEAD_DIM = 64
D_SB = N_HEADS * HEAD_DIM
N_GROUPS = 8
GROUP_DIM = 64
D_SGU = N_GROUPS * GROUP_DIM
SGU_CHUNK = 128
CHUNK = 64
EPS = 1e-6
LANES = 128
N_PAIRS = N_HEADS // 2
QKV_COLS = 3 * D_SB
ATT_BLOCK = 256
ATT_TILE = 128
CARRY_FLOOR = -90.0
R_UNREACHED = -1e30

ADAM_LR = 0.001
ADAM_B1 = 0.9
ADAM_B2 = 0.999
ADAM_EPS = 1e-08
ADAM_WD = 0.01
ADAM_STEP = 10

VMEM_LIMIT = 56 * 1024 * 1024


def _cparams(sem=None, vmem=VMEM_LIMIT):
    return pltpu.CompilerParams(dimension_semantics=sem, vmem_limit_bytes=vmem)


def _dot(a, b):
    return jnp.dot(a, b, preferred_element_type=F32)


def _dot_nt(a, b):
    return lax.dot_general(a, b, (((1,), (1,)), ((), ())), preferred_element_type=F32)


def _dot_tn(a, b):
    return lax.dot_general(a, b, (((0,), (0,)), ((), ())), preferred_element_type=F32)


def _split_hi_lo(a):
    hi = a.astype(BF16)
    lo = (a - hi.astype(F32)).astype(BF16)
    return hi, lo


def _sigmoid(x):
    return 1.0 / (1.0 + jnp.exp(-x))


_GELU_C = math.sqrt(2.0 / math.pi)


def _gelu_and_grad(x):
    x2 = x * x
    inner = _GELU_C * (x + 0.044715 * (x2 * x))
    t = jnp.tanh(inner)
    cdf = 0.5 * (1.0 + t)
    g = x * cdf
    dg = cdf + x * (0.5 * (1.0 - t * t)) * (_GELU_C * (1.0 + 3.0 * 0.044715 * x2))
    return g, dg


def _coords():
    return lax.axis_index("x"), lax.axis_index("y"), lax.axis_index("c")


def _dev_index(px, py, pc):
    return 4 * px + 2 * py + pc


def _allgather(blocks, name):
    n = len(blocks)

    def body(*refs):
        gather = _Gather(refs[:n], refs[n:2 * n], *refs[2 * n:])
        gather.issue()
        gather.finish()

    any_spec = pl.BlockSpec(memory_space=pl.ANY)
    return pl.pallas_call(
        body, name=name,
        out_shape=_gather_out_shapes(blocks),
        in_specs=[any_spec] * n, out_specs=[any_spec] * n,
        scratch_shapes=_gather_semaphores(n),
    )(*blocks)


def _gather_out_shapes(blocks):
    return [jax.ShapeDtypeStruct((N_DEV,) + b.shape, b.dtype) for b in blocks]


def _gather_semaphores(n):
    return [pltpu.SemaphoreType.DMA((n, 7)), pltpu.SemaphoreType.DMA((n, 7)), pltpu.SemaphoreType.DMA((n,))]


class _Gather:
    def __init__(self, ins, outs, send_sems, recv_sems, local_sems, relay=False):
        self.ins, self.outs = ins, outs
        self.send_sems, self.recv_sems, self.local_sems = send_sems, recv_sems, local_sems
        self.n = len(ins)
        self.relay = relay
        x, y, c = _coords()
        self.c = c
        self.me, self.sibling = (x, y, c), (x, y, 1 - c)
        self.chips = [(1 - x, y), (x, 1 - y), (1 - x, 1 - y)]

    def _copy(self, a, k, block, to, src=None):
        dst = self.outs[a].at[_dev_index(*block)]
        return pltpu.make_async_remote_copy(
            src_ref=dst if src is None else src, dst_ref=dst,
            send_sem=self.send_sems.at[a, k], recv_sem=self.recv_sems.at[a, k],
            device_id=to, device_id_type=MESH)

    def _mine(self):
        return [pltpu.make_async_copy(self.ins[a], self.outs[a].at[_dev_index(*self.me)], self.local_sems.at[a])
                for a in range(self.n)]

    def _first(self):
        first = []
        direct = self.chips[:2] if self.relay else self.chips
        for a in range(self.n):
            first.append(self._copy(a, 0, self.me, self.sibling, src=self.ins[a]))
            first += [self._copy(a, 1 + j, self.me, (*chip, self.c), src=self.ins[a])
                      for j, chip in enumerate(direct)]
        return first

    def issue(self):
        for cp in self._mine() + self._first():
            cp.start()

    def _pass_on(self, a, j):
        chip = self.chips[j]
        self._copy(a, 1 + j, (*chip, self.c), self.me).wait_recv()
        self._copy(a, 4 + j, (*chip, self.c), self.sibling).start()

    def finish(self):
        c = self.c
        if self.relay:
            for core in range(2):
                @pl.when(c == core)
                def _(core=core):
                    j_src, j_dst = core, 1 - core
                    for a in range(self.n):
                        self._pass_on(a, j_src)
                        self._copy(a, 3, (*self.chips[j_src], c), (*self.chips[j_dst], c)).start()
                    for a in range(self.n):
                        self._pass_on(a, j_dst)
                    for a in range(self.n):
                        self._pass_on(a, 2)
        else:
            for j in range(3):
                for a in range(self.n):
                    self._pass_on(a, j)
        for a in range(self.n):
            self._copy(a, 0, self.sibling, self.me).wait_recv()
            for j, chip in enumerate(self.chips):
                self._copy(a, 4 + j, (*chip, 1 - c), self.me).wait_recv()
        for a in range(self.n):
            for k in range(7):
                self._copy(a, k, self.me, self.sibling).wait_send()
        for cp in self._mine():
            cp.wait()


def _push_sibling(arrs, name):
    n = len(arrs)

    def body(*refs):
        ins, outs = refs[:n], refs[n:2 * n]
        send_sems, recv_sems = refs[2 * n:]
        x, y, c = _coords()
        sibling = (x, y, 1 - c)
        copies = []
        for a in range(n):
            for k in range(4):
                copies.append(pltpu.make_async_remote_copy(
                    src_ref=ins[a].at[k, 1 - c], dst_ref=outs[a].at[k],
                    send_sem=send_sems.at[a, k], recv_sem=recv_sems.at[a, k],
                    device_id=sibling, device_id_type=MESH))
        for cp in copies:
            cp.start()
        for cp in copies:
            cp.wait()

    any_spec = pl.BlockSpec(memory_space=pl.ANY)
    return pl.pallas_call(
        body, name=name,
        out_shape=[jax.ShapeDtypeStruct((4,) + a.shape[2:], a.dtype) for a in arrs],
        in_specs=[any_spec] * n, out_specs=[any_spec] * n,
        scratch_shapes=[pltpu.SemaphoreType.DMA((n, 4)), pltpu.SemaphoreType.DMA((n, 4))],
    )(*arrs)


def _chip_push_copies(ins, outs, send_sems, recv_sems):
    x, y, c = _coords()
    chips = [(1 - x, y), (x, 1 - y), (1 - x, 1 - y)]
    return [pltpu.make_async_remote_copy(
        src_ref=ins[a].at[2 * px + py], dst_ref=outs[a].at[r],
        send_sem=send_sems.at[a, r], recv_sem=recv_sems.at[a, r],
        device_id=(px, py, c), device_id_type=MESH)
        for a in range(len(ins)) for r, (px, py) in enumerate(chips)]


def _direct_push_copies(ins, outs, send_sems, recv_sems):
    x, y, c = _coords()
    copies = []
    for a in range(len(ins)):
        for r in range(1, N_DEV):
            px = 1 - x if r & 4 else x
            py = 1 - y if r & 2 else y
            pc = 1 - c if r & 1 else c
            copies.append(pltpu.make_async_remote_copy(
                src_ref=ins[a].at[_dev_index(px, py, pc)], dst_ref=outs[a].at[r - 1],
                send_sem=send_sems.at[a, r - 1], recv_sem=recv_sems.at[a, r - 1],
                device_id=(px, py, pc), device_id_type=MESH))
    return copies


def _chip_partial_sums(owns, lands, core, name):
    n = len(owns)
    blocks = [a.shape[2:] for a in owns]

    def body(core_ref, *refs):
        del core_ref
        for own_ref, land_ref, out_ref in zip(refs[:n], refs[n:2 * n], refs[2 * n:]):
            out_ref[...] = (own_ref[...].astype(F32) + land_ref[...].astype(F32)).astype(out_ref.dtype)

    return pl.pallas_call(
        body, name=name,
        out_shape=[jax.ShapeDtypeStruct((4,) + b, a.dtype) for a, b in zip(owns, blocks)],
        grid_spec=pltpu.PrefetchScalarGridSpec(
            num_scalar_prefetch=1, grid=(4,),
            in_specs=[pl.BlockSpec((None, None) + b, lambda k, core: (k, core[0], 0, 0)) for b in blocks]
            + [pl.BlockSpec((None,) + b, lambda k, core: (k, 0, 0)) for b in blocks],
            out_specs=[pl.BlockSpec((None,) + b, lambda k, core: (k, 0, 0)) for b in blocks]),
        compiler_params=_cparams(("parallel",)),
    )(core, *owns, *lands)


def _adamw_math(w, g, m, v):
    m = ADAM_B1 * m + (1.0 - ADAM_B1) * g
    v = ADAM_B2 * v + (1.0 - ADAM_B2) * (g * g)
    m_hat = m / (1.0 - ADAM_B1 ** ADAM_STEP)
    v_hat = v / (1.0 - ADAM_B2 ** ADAM_STEP)
    delta = -ADAM_LR * (m_hat / (jnp.sqrt(v_hat) + ADAM_EPS) + ADAM_WD * w)
    return delta, m, v


def _adam_shard(cps, lands, chip, w, m, v, name):
    rows, cols = w.shape
    tr = rows // 4
    nparts = len(cps)

    def body(chip_ref, *refs):
        del chip_ref
        cp_refs, land_refs = refs[:nparts], refs[nparts:2 * nparts]
        w_ref, m_ref, v_ref, g_out, d_out, m_out, v_out = refs[2 * nparts:]
        parts = []
        for cp_ref, land_ref in zip(cp_refs, land_refs):
            g_k = cp_ref[...].astype(F32)
            for r in range(3):
                g_k = g_k + land_ref[r].astype(F32)
            parts.append(g_k)
        g = parts[0] if nparts == 1 else jnp.concatenate(parts, axis=1)
        delta, m_new, v_new = _adamw_math(w_ref[...], g, m_ref[...], v_ref[...])
        g_out[...] = g
        d_out[...] = delta
        m_out[...] = m_new
        v_out[...] = v_new

    tile = pl.BlockSpec((tr, cols), lambda r, chip: (r, 0))
    out = jax.ShapeDtypeStruct((rows, cols), F32)
    return pl.pallas_call(
        body, name=name, out_shape=[out] * 4,
        grid_spec=pltpu.PrefetchScalarGridSpec(
            num_scalar_prefetch=1, grid=(rows // tr,),
            in_specs=[pl.BlockSpec((None, tr, a.shape[2]), lambda r, chip: (chip[0], r, 0)) for a in cps]
            + [pl.BlockSpec((3, tr, a.shape[2]), lambda r, chip: (0, r, 0)) for a in lands]
            + [tile, tile, tile],
            out_specs=[tile] * 4),
        compiler_params=_cparams(("parallel",)),
    )(chip, *cps, *lands, w, m, v)


def _adam_shards_whole(owns, lands, dev, ws, ms, vs, name):
    n = len(ws)

    def body(dev_ref, *refs):
        own_refs, land_refs = refs[:n], refs[n:2 * n]
        w_refs, m_refs, v_refs = refs[2 * n:3 * n], refs[3 * n:4 * n], refs[4 * n:5 * n]
        outs = refs[5 * n:]
        for i in range(n):
            g = own_refs[i][dev_ref[0]].astype(F32)
            for r in range(N_DEV - 1):
                g = g + land_refs[i][r].astype(F32)
            delta, m_new, v_new = _adamw_math(w_refs[i][...], g, m_refs[i][...], v_refs[i][...])
            outs[i][...] = g
            outs[n + i][...] = delta
            outs[2 * n + i][...] = m_new
            outs[3 * n + i][...] = v_new

    vmem = pl.BlockSpec(memory_space=pltpu.VMEM)
    res = pl.pallas_call(
        body, name=name, out_shape=[jax.ShapeDtypeStruct(w.shape, F32) for w in ws] * 4,
        in_specs=[pl.BlockSpec(memory_space=pltpu.SMEM)] + [vmem] * (5 * n), out_specs=[vmem] * (4 * n),
        compiler_params=_cparams(),
    )(dev, *owns, *lands, *ws, *ms, *vs)
    return [res[k * n:(k + 1) * n] for k in range(4)]


def _adam_small(parts, ws, ms, vs, loss_parts, name):
    n = len(ws)

    def body(*refs):
        p_refs, w_refs, m_refs, v_refs = refs[:n], refs[n:2 * n], refs[2 * n:3 * n], refs[3 * n:4 * n]
        loss_ref, outs, loss_out = refs[4 * n], refs[4 * n + 1:-1], refs[-1]
        total = loss_ref[0]
        for dev in range(1, N_DEV):
            total = total + loss_ref[dev]
        loss_out[...] = total
        for i in range(n):
            g = p_refs[i][0]
            for dev in range(1, N_DEV):
                g = g + p_refs[i][dev]
            if g.shape[0] != w_refs[i].shape[0]:
                g = jnp.sum(g, axis=0, keepdims=True)
            delta, m_new, v_new = _adamw_math(w_refs[i][...], g, m_refs[i][...], v_refs[i][...])
            outs[i][...] = g
            outs[n + i][...] = delta
            outs[2 * n + i][...] = m_new
            outs[3 * n + i][...] = v_new

    out_shapes = [jax.ShapeDtypeStruct(w.shape, F32) for w in ws] * 4
    out_shapes.append(jax.ShapeDtypeStruct(loss_parts.shape[1:], F32))
    res = pl.pallas_call(body, name=name, out_shape=out_shapes, compiler_params=_cparams())(
        *parts, *ws, *ms, *vs, loss_parts)
    return [res[k * n:(k + 1) * n] for k in range(4)], res[-1]


def _rmsnorm_fwd(x, g, riders, name):
    s, d = x.shape
    ts = 512
    nt = s // ts
    nr = len(riders)

    def body(x_ref, g_ref, *rest):
        rider_in, (h_ref, ht_ref) = rest[:nr], rest[nr:nr + 2]
        rider_out, staged = rest[nr + 2:2 * nr + 2], rest[2 * nr + 2:3 * nr + 2]
        sems = rest[3 * nr + 2:]
        i = pl.program_id(0)
        gather = _Gather(staged, rider_out, *sems, relay=True)

        @pl.when(i == 0)
        def _():
            for src, dst in zip(rider_in, staged):
                dst[...] = src[...].astype(BF16)
            gather.issue()

        xv = x_ref[...]
        rstd = lax.rsqrt(jnp.mean(xv * xv, axis=-1, keepdims=True) + EPS)
        h = xv * rstd * g_ref[...]
        h_ref[...] = h.astype(BF16)
        ht_ref[...] = h.T.astype(BF16)

        @pl.when(i == nt - 1)
        def _():
            gather.finish()

    any_spec = pl.BlockSpec(memory_space=pl.ANY)
    res = pl.pallas_call(
        body, name=name,
        out_shape=[jax.ShapeDtypeStruct((s, d), BF16), jax.ShapeDtypeStruct((d, s), BF16)]
        + [jax.ShapeDtypeStruct((N_DEV,) + r.shape, BF16) for r in riders],
        grid=(nt,),
        in_specs=[pl.BlockSpec((ts, d), lambda i: (i, 0)), pl.BlockSpec((1, d), lambda i: (0, 0))]
        + [pl.BlockSpec(r.shape, lambda i: (0, 0)) for r in riders],
        out_specs=[pl.BlockSpec((ts, d), lambda i: (i, 0)), pl.BlockSpec((d, ts), lambda i: (0, i))]
        + [any_spec] * nr,
        scratch_shapes=[pltpu.VMEM(r.shape, BF16) for r in riders] + _gather_semaphores(nr),
        compiler_params=_cparams(("arbitrary",)),
    )(x, g, *riders)
    return res[0], res[1], res[2:]


def _inproj(h, wt, name):
    s, d = h.shape
    n = wt.shape[0]
    tn = 512
    n_qkv = QKV_COLS // tn

    def body(h_ref, w_ref, qkv_ref, rest_ref):
        j = pl.program_id(0)
        res = _dot_nt(h_ref[...], w_ref[...])

        @pl.when(j < n_qkv)
        def _():
            qkv_ref[...] = res.astype(BF16)

        @pl.when(j >= n_qkv)
        def _():
            rest_ref[...] = res

    return pl.pallas_call(
        body, name=name,
        out_shape=[jax.ShapeDtypeStruct((s, QKV_COLS), BF16), jax.ShapeDtypeStruct((s, n - QKV_COLS), F32)],
        grid=(n // tn,),
        in_specs=[pl.BlockSpec((s, d), lambda j: (0, 0)), pl.BlockSpec((tn, d), lambda j: (j, 0))],
        out_specs=[pl.BlockSpec((s, tn), lambda j: (0, jnp.minimum(j, n_qkv - 1))),
                   pl.BlockSpec((s, tn), lambda j: (0, jnp.maximum(j - n_qkv, 0)))],
        compiler_params=_cparams(("arbitrary",)),
    )(h, wt)


def _dw_in(ht, dproj, name):
    d, s = ht.shape
    n = dproj.shape[1]
    tn = 512

    def body(a_ref, b_ref, o_ref):
        o_ref[...] = _dot(a_ref[...], b_ref[...]).T.astype(o_ref.dtype)

    return pl.pallas_call(
        body, name=name, out_shape=jax.ShapeDtypeStruct((n, d), BF16), grid=(n // tn,),
        in_specs=[pl.BlockSpec((d, s), lambda j: (0, 0)), pl.BlockSpec((s, tn), lambda j: (0, j))],
        out_specs=pl.BlockSpec((tn, d), lambda j: (j, 0)),
        compiler_params=_cparams(("parallel",)),
    )(ht, dproj)


def _dh_and_grad_x(dproj, wt, x, g, dx2, cps, name):
    s, n = dproj.shape
    d = wt.shape[1]
    tm = min(s, 1024)
    tk = next(t for t in (1408, 512) if n % t == 0)
    nk = n // tk
    nm = s // tm
    nc = len(cps)

    def body(a_ref, w_ref, x_ref, g_ref, dx2_ref, *rest):
        cp_refs, gx_ref, dg_ref = rest[:nc], rest[nc], rest[nc + 1]
        land_refs = rest[nc + 2:2 * nc + 2]
        acc_ref, send_sems, recv_sems = rest[2 * nc + 2:]
        i, k = pl.program_id(0), pl.program_id(1)

        @pl.when((i == 0) & (k == 0))
        def _():
            for cp in _chip_push_copies(cp_refs, land_refs, send_sems, recv_sems):
                cp.start()
            dg_ref[...] = jnp.zeros_like(dg_ref)

        @pl.when(k == 0)
        def _():
            acc_ref[...] = jnp.zeros_like(acc_ref)

        acc_ref[...] += _dot(a_ref[...], w_ref[...])

        @pl.when(k == nk - 1)
        def _():
            dh = acc_ref[...]
            xv = x_ref[...]
            rstd = lax.rsqrt(jnp.mean(xv * xv, axis=-1, keepdims=True) + EPS)
            xhat = xv * rstd
            dg_ref[...] += jnp.sum((dh * xhat).reshape(tm // 8, 8, d), axis=0)
            dxh = dh * g_ref[...]
            gx_ref[...] = dx2_ref[...] + rstd * (dxh - xhat * jnp.mean(dxh * xhat, axis=-1, keepdims=True))

        @pl.when((i == nm - 1) & (k == nk - 1))
        def _():
            for cp in _chip_push_copies(cp_refs, land_refs, send_sems, recv_sems):
                cp.wait()

    any_spec = pl.BlockSpec(memory_space=pl.ANY)
    res = pl.pallas_call(
        body, name=name,
        out_shape=[jax.ShapeDtypeStruct((s, d), F32), jax.ShapeDtypeStruct((8, d), F32)]
        + [jax.ShapeDtypeStruct((3,) + a.shape[1:], a.dtype) for a in cps],
        grid=(nm, nk),
        in_specs=[pl.BlockSpec((tm, tk), lambda i, k: (i, k)), pl.BlockSpec((tk, d), lambda i, k: (k, 0)),
                  pl.BlockSpec((tm, d), lambda i, k: (i, 0)), pl.BlockSpec((1, d), lambda i, k: (0, 0)),
                  pl.BlockSpec((tm, d), lambda i, k: (i, 0))] + [any_spec] * nc,
        out_specs=[pl.BlockSpec((tm, d), lambda i, k: (i, 0)), pl.BlockSpec((8, d), lambda i, k: (0, 0))]
        + [any_spec] * nc,
        scratch_shapes=[pltpu.VMEM((tm, d), F32), pltpu.SemaphoreType.DMA((nc, 3)), pltpu.SemaphoreType.DMA((nc, 3))],
        compiler_params=_cparams(("arbitrary", "arbitrary")),
    )(dproj, wt, x, g, dx2, *cps)
    return res[0], res[1], res[2:]


def _log_sigmoids(z):
    l1p = jnp.log(1.0 + jnp.exp(-jnp.abs(z)))
    ls = jnp.minimum(z, 0.0) - l1p
    return ls, ls - z


def _strict_lower_ones(n):
    row = lax.broadcasted_iota(jnp.int32, (n, n), 0)
    col = lax.broadcasted_iota(jnp.int32, (n, n), 1)
    return row, col, (row > col).astype(BF16)


def _attn_fwd(qkv, riders, name):
    s = qkv.shape[0]
    tb, tt = ATT_BLOCK, ATT_TILE
    nq = s // tb
    per_block = tb // tt
    assert per_block == 2 and s // tt <= LANES, "two query tiles per grid step; one lane of saved carry per key tile"

    nr = len(riders)

    def sweep_body(gather, q_ref, k_ref, v_ref, o_ref, rs_ref, extra_ref, acc_ref, r_ref, rsv_ref, rmax_ref):
        i = pl.program_id(1)
        lane = lax.broadcasted_iota(jnp.int32, (tt, LANES), 1)
        hmask = [lane < HEAD_DIM, lane >= HEAD_DIM]
        row, col, tri = _strict_lower_ones(tt)
        tri2 = jnp.concatenate([tri, tri], axis=0)
        below = col < row
        qrows = [slice(u * tt, (u + 1) * tt) for u in range(per_block)]
        qm = [[jnp.where(m, q_ref[qrows[u], :], jnp.zeros((), BF16)) * jnp.asarray(HEAD_DIM ** -0.5, BF16)
               for m in hmask] for u in range(per_block)]
        acc_ref[...] = jnp.zeros_like(acc_ref)
        r_ref[...] = jnp.zeros_like(r_ref)
        rsv_ref[...] = jnp.full_like(rsv_ref, R_UNREACHED)

        def sweep(tiles, chains):
            heads, nch = range(2), range(len(chains))
            kv = []
            for t in tiles:
                rows = pl.ds(pl.multiple_of(t * tt, tt), tt)
                kv.append((k_ref[rows, :], v_ref[rows, :]))
            z = [[_dot_nt(qm[u][h], kv[ti][0]) for h in heads] for u, ti, _ in chains]
            ls, lk = [], []
            for n in nch:
                pairs = [_log_sigmoids(z[n][h]) for h in heads]
                ls.append([p[0] for p in pairs])
                lk.append([jnp.where(below, p[1], 0.0) if chains[n][2] else p[1] for p in pairs])
            cur = {u: [r_ref[h, qrows[u], :] for h in heads] for u in sorted({c[0] for c in chains})}
            r = []
            for n, (u, _, _) in enumerate(chains):
                r.append(cur[u])
                cur[u] = [cur[u][h] + jnp.sum(lk[n][h], axis=1, keepdims=True) for h in heads]
            for u in cur:
                rmax_ref[u] = jnp.max(jnp.maximum(cur[u][0], cur[u][1]))
            suffix = [[_dot(jnp.concatenate(_split_hi_lo(lk[n][h]), axis=1), tri2) for h in heads] for n in nch]
            w = []
            for n in nch:
                w_n = [jnp.exp(ls[n][h] + suffix[n][h] + r[n][h]) for h in heads]
                if chains[n][2]:
                    w_n = [jnp.where(below, w_h, 0.0) for w_h in w_n]
                w.append([w_h.astype(BF16) for w_h in w_n])
            pv = [[_dot(w[n][h], kv[chains[n][1]][1]) for h in heads] for n in nch]
            for u in cur:
                mine = [n for n in nch if chains[n][0] == u]
                for h in heads:
                    acc_ref[h, qrows[u], :] += functools.reduce(lambda a, b: a + b, [pv[n][h] for n in mine])
                    for n in mine:
                        if not chains[n][2]:
                            t = tiles[chains[n][1]]
                            rsv_ref[h, qrows[u], :] = jnp.where(lane == t, r[n][h], rsv_ref[h, qrows[u], :])
                    r_ref[h, qrows[u], :] = cur[u][h]

        first = per_block * i

        @pl.when(i == 0)
        def _():
            sweep([0, 1], [(0, 0, True), (1, 1, True), (1, 0, False)])

        @pl.when(i > 0)
        def _():
            sweep([first - 1, first, first + 1], [(0, 1, True), (1, 2, True), (0, 0, False), (1, 1, False)])

        for u in range(per_block):
            n_left = first + u - 1

            def live(c, n_left=n_left):
                jj, rmax = c
                return (jj < n_left) & (rmax >= CARRY_FLOOR)

            def step(c, u=u, n_left=n_left):
                jj, _ = c
                sweep([n_left - 1 - jj], [(u, 0, False)])
                return jj + 1, rmax_ref[u]

            swept, _ = lax.while_loop(live, step, (jnp.int32(0), rmax_ref[u]))
            extra_ref[pl.program_id(0), first + u] = swept.astype(F32)

        lane_b = lax.broadcasted_iota(jnp.int32, (tb, LANES), 1)
        o_ref[...] = jnp.where(lane_b < HEAD_DIM, acc_ref[0], acc_ref[1])
        rs_ref[...] = rsv_ref[...]

        @pl.when((pl.program_id(0) == N_PAIRS - 1) & (i == nq - 1))
        def _():
            gather.finish()

    def body(q_ref, k_ref, v_ref, *rest):
        rider_in, (o_ref, rs_ref, extra_ref) = rest[:nr], rest[nr:nr + 3]
        rider_out = rest[nr + 3:2 * nr + 3]
        acc_ref, r_ref, rsv_ref, rmax_ref = rest[2 * nr + 3:2 * nr + 7]
        gather = _Gather(rider_in, rider_out, *rest[2 * nr + 7:])

        @pl.when((pl.program_id(0) == 0) & (pl.program_id(1) == 0))
        def _():
            gather.issue()

        sweep_body(gather, q_ref, k_ref, v_ref, o_ref, rs_ref, extra_ref, acc_ref, r_ref, rsv_ref, rmax_ref)

    any_spec = pl.BlockSpec(memory_space=pl.ANY)
    res = pl.pallas_call(
        body, name=name,
        out_shape=[jax.ShapeDtypeStruct((s, D_SB), F32), jax.ShapeDtypeStruct((N_HEADS, s, LANES), F32),
                   jax.ShapeDtypeStruct((N_PAIRS, s // tt), F32)] + _gather_out_shapes(riders),
        grid=(N_PAIRS, nq),
        in_specs=[pl.BlockSpec((tb, LANES), lambda p, i: (i, p)),
                  pl.BlockSpec((s, LANES), lambda p, i: (0, N_PAIRS + p)),
                  pl.BlockSpec((s, LANES), lambda p, i: (0, 2 * N_PAIRS + p))] + [any_spec] * nr,
        out_specs=[pl.BlockSpec((tb, LANES), lambda p, i: (i, p)),
                   pl.BlockSpec((2, tb, LANES), lambda p, i: (p, i, 0)),
                   pl.BlockSpec(memory_space=pltpu.SMEM)] + [any_spec] * nr,
        scratch_shapes=[pltpu.VMEM((2, tb, LANES), F32), pltpu.VMEM((2, tb, LANES), F32),
                        pltpu.VMEM((2, tb, LANES), F32), pltpu.SMEM((per_block,), F32)] + _gather_semaphores(nr),
        compiler_params=_cparams(("arbitrary", "arbitrary")),
    )(qkv, qkv, qkv, *riders)
    return res[0], res[1], res[2], res[3:]


def _attn_bwd(qkv, do, rs, extra, dproj, smalls, cps, name):
    s = qkv.shape[0]
    tb, tt = ATT_BLOCK, ATT_TILE
    nq = s // tb
    per_block = tb // tt
    scale = HEAD_DIM ** -0.5

    ns, nc = len(smalls), len(cps)

    def sweep_body(q_ref, k_ref, v_ref, do_ref, rs_ref, extra_ref, dproj_hbm, out_hbm, dq_acc, dk_acc, dv_acc,
                   dqi_ref, pc_ref, stage_ref, out_sems):
        del dproj_hbm
        pair = pl.program_id(0)
        lane = lax.broadcasted_iota(jnp.int32, (tt, LANES), 1)
        hmask = [lane < HEAD_DIM, lane >= HEAD_DIM]
        row, col, tri = _strict_lower_ones(tt)
        tri2 = jnp.concatenate([tri, tri], axis=0)
        triu = (row < col).astype(BF16)
        below = col < row
        zero = jnp.zeros((), BF16)
        qrows = [slice(u * tt, (u + 1) * tt) for u in range(per_block)]
        dq_acc[...] = jnp.zeros_like(dq_acc)
        dk_acc[...] = jnp.zeros_like(dk_acc)
        dv_acc[...] = jnp.zeros_like(dv_acc)

        def qblock(i, carry):
            block_rows = pl.ds(pl.multiple_of(i * tb, tb), tb)
            q2 = q_ref[block_rows, :]
            do2 = do_ref[block_rows, :]
            qm = [[jnp.where(m, q2[qrows[u]], zero) * jnp.asarray(scale, BF16) for m in hmask]
                  for u in range(per_block)]
            dom = [[jnp.where(m, do2[qrows[u]], zero) for m in hmask] for u in range(per_block)]
            rs_i = [[rs_ref[h, pl.ds(pl.multiple_of(i * tb + u * tt, tt), tt), :] for h in range(2)]
                    for u in range(per_block)]
            dqi_ref[...] = jnp.zeros_like(dqi_ref)
            pc_ref[...] = jnp.zeros_like(pc_ref)
            first = per_block * i

            def sweep(tiles, chains):
                heads, nch = range(2), range(len(chains))
                krows = [pl.ds(pl.multiple_of(t * tt, tt), tt) for t in tiles]
                k2 = [k_ref[rows, :] for rows in krows]
                v2 = [v_ref[rows, :] for rows in krows]
                z = [[_dot_nt(qm[u][h], k2[ti]) for h in heads] for u, ti, _ in chains]
                dw = [[_dot_nt(dom[u][h], v2[ti]) for h in heads] for u, ti, _ in chains]
                ls, lk = [], []
                for n in nch:
                    pairs = [_log_sigmoids(z[n][h]) for h in heads]
                    ls.append([p[0] for p in pairs])
                    lk.append([jnp.where(below, p[1], 0.0) if chains[n][2] else p[1] for p in pairs])
                suffix = [[_dot(jnp.concatenate(_split_hi_lo(lk[n][h]), axis=1), tri2) for h in heads] for n in nch]
                w, g = [], []
                for n, (u, ti, diag) in enumerate(chains):
                    w_n = []
                    for h in heads:
                        logw = ls[n][h] + suffix[n][h]
                        if diag:
                            w_n.append(jnp.where(below, jnp.exp(logw), 0.0))
                        else:
                            carry_in = jnp.sum(jnp.where(lane == tiles[ti], rs_i[u][h], 0.0), axis=1, keepdims=True)
                            w_n.append(jnp.exp(logw + carry_in))
                    w.append([w_h.astype(BF16) for w_h in w_n])
                    g.append([w_n[h] * dw[n][h] for h in heads])
                prefix = [[_dot(g[n][h].astype(BF16), triu) for h in heads] for n in nch]
                cur = {u: [pc_ref[h, qrows[u], :] for h in heads] for u in sorted({c[0] for c in chains})}
                pc = []
                for n, (u, _, _) in enumerate(chains):
                    pc.append(cur[u])
                    cur[u] = [cur[u][h] + jnp.sum(g[n][h], axis=1, keepdims=True) for h in heads]
                dzb = []
                for n in nch:
                    dz_n = []
                    for h in heads:
                        sig = jnp.exp(ls[n][h])
                        dz = g[n][h] - sig * (g[n][h] + prefix[n][h] + pc[n][h])
                        if chains[n][2]:
                            dz = jnp.where(below, dz, 0.0)
                        dz_n.append(dz.astype(BF16))
                    dzb.append(dz_n)
                for u in cur:
                    for h in heads:
                        pc_ref[h, qrows[u], :] = cur[u][h]
                dq = [[_dot(dzb[n][h], jnp.where(hmask[h], k2[chains[n][1]], zero)) for h in heads] for n in nch]
                dk = [[_dot_tn(dzb[n][h], qm[chains[n][0]][h]) for h in heads] for n in nch]
                dv = [[_dot_tn(w[n][h], dom[chains[n][0]][h]) for h in heads] for n in nch]
                add = lambda a, b: a + b
                for u in cur:
                    dqi_ref[qrows[u], :] += functools.reduce(
                        add, [dq[n][h] for n in nch if chains[n][0] == u for h in heads])
                for ti in range(len(tiles)):
                    mine = [n for n in nch if chains[n][1] == ti]
                    dk_acc[krows[ti], :] += functools.reduce(add, [dk[n][h] for n in mine for h in heads])
                    dv_acc[krows[ti], :] += functools.reduce(add, [dv[n][h] for n in mine for h in heads])

            for u in range(per_block):
                n_left = first + u - 1
                n_extra = jnp.clip(extra_ref[pair, first + u].astype(jnp.int32), 0, jnp.maximum(n_left, 0))

                def step(t, c, u=u):
                    sweep([t], [(u, 0, False)])
                    return c

                lax.fori_loop(n_left - n_extra, n_left, step, 0)

            @pl.when(i == 0)
            def _():
                sweep([0, 1], [(0, 0, True), (1, 0, False), (1, 1, True)])

            @pl.when(i > 0)
            def _():
                sweep([first - 1, first, first + 1], [(0, 0, False), (1, 1, False), (0, 1, True), (1, 2, True)])

            dq_acc[block_rows, :] += dqi_ref[...] * scale
            return carry

        lax.fori_loop(0, nq, qblock, 0)
        copies = []
        for t, acc in enumerate((dq_acc, dk_acc, dv_acc)):
            stage_ref[t] = acc[...].astype(BF16)
            col0 = pl.multiple_of(t * D_SB + pair * LANES, LANES)
            copies.append(pltpu.make_async_copy(stage_ref.at[t], out_hbm.at[:, pl.ds(col0, LANES)], out_sems.at[t]))
        for cp in copies:
            cp.start()
        for cp in copies:
            cp.wait()

    def body(q_ref, k_ref, v_ref, do_ref, rs_ref, extra_ref, dproj_hbm, *rest):
        small_in, cp_in = rest[:ns], rest[ns:ns + nc]
        out_hbm = rest[ns + nc]
        small_out, land_out = rest[ns + nc + 1:2 * ns + nc + 1], rest[2 * ns + nc + 1:2 * (ns + nc) + 1]
        scratch = rest[2 * (ns + nc) + 1:]
        gather = _Gather(small_in, small_out, *scratch[7:10])
        pair = pl.program_id(0)

        @pl.when(pair == 0)
        def _():
            gather.issue()
            for cp in _direct_push_copies(cp_in, land_out, *scratch[10:]):
                cp.start()

        sweep_body(q_ref, k_ref, v_ref, do_ref, rs_ref, extra_ref, dproj_hbm, out_hbm, *scratch[:7])

        @pl.when(pair == N_PAIRS - 1)
        def _():
            gather.finish()
            for cp in _direct_push_copies(cp_in, land_out, *scratch[10:]):
                cp.wait()

    any_spec = pl.BlockSpec(memory_space=pl.ANY)
    res = pl.pallas_call(
        body, name=name,
        out_shape=[jax.ShapeDtypeStruct(dproj.shape, BF16)] + _gather_out_shapes(smalls)
        + [jax.ShapeDtypeStruct((N_DEV - 1,) + a.shape[1:], a.dtype) for a in cps],
        grid=(N_PAIRS,),
        in_specs=[pl.BlockSpec((s, LANES), lambda p: (0, p)),
                  pl.BlockSpec((s, LANES), lambda p: (0, N_PAIRS + p)),
                  pl.BlockSpec((s, LANES), lambda p: (0, 2 * N_PAIRS + p)),
                  pl.BlockSpec((s, LANES), lambda p: (0, p)),
                  pl.BlockSpec((2, s, LANES), lambda p: (p, 0, 0)),
                  pl.BlockSpec(memory_space=pltpu.SMEM),
                  any_spec] + [any_spec] * (ns + nc),
        out_specs=[any_spec] * (1 + ns + nc),
        scratch_shapes=[pltpu.VMEM((s, LANES), F32), pltpu.VMEM((s, LANES), F32), pltpu.VMEM((s, LANES), F32),
                        pltpu.VMEM((tb, LANES), F32), pltpu.VMEM((2, tb, LANES), F32),
                        pltpu.VMEM((3, s, LANES), BF16), pltpu.SemaphoreType.DMA((3,))]
        + _gather_semaphores(ns)
        + [pltpu.SemaphoreType.DMA((nc, N_DEV - 1)), pltpu.SemaphoreType.DMA((nc, N_DEV - 1))],
        input_output_aliases={6: 0},
        compiler_params=_cparams(("arbitrary",)),
    )(qkv, qkv, qkv, do, rs, extra, dproj, *smalls, *cps)
    return res[0], res[1:1 + ns], res[1 + ns:]


def _mid(o, rest, x, tgt, wua, wub, wout, fg, lng, lnb, wsp, bfull, gavg, name):
    s, d = x.shape
    ts = 256
    nt = s // ts
    nchunk = ts // SGU_CHUNK
    n_rest = rest.shape[1]

    def body(o_ref, rest_ref, x_ref, t_ref, wua_ref, wub_ref, wout_ref, fg_ref, lng_ref, lnb_ref, wsp_ref, bfull_ref,
             gavg_ref, loss_ref, dx2_ref, do_ref, dproj_ref, dwua_out, dwub_out, dwout_out, dfg_ref, dlng_ref,
             dlnb_ref, dwsp_ref, dbfull_ref, dwua_ref, dwub_ref, dwout_ref):
        step = pl.program_id(0)

        @pl.when(step == 0)
        def _():
            for ref in (loss_ref, dwua_ref, dwub_ref, dwout_ref, dfg_ref, dlng_ref, dlnb_ref, dwsp_ref, dbfull_ref):
                ref[...] = jnp.zeros_like(ref)

        gavg = gavg_ref[...]

        def gmean(a):
            return _dot(a.astype(BF16), gavg)

        def colsum8(a):
            return jnp.sum(a.reshape(ts // 8, 8, a.shape[1]), axis=0)

        z_a = rest_ref[:, 0:512]
        u_b = rest_ref[:, 512:1024]
        v_b = rest_ref[:, 1024:1536]
        z_b = rest_ref[:, 1536:2048]
        g_a = rest_ref[:, 2048:2048 + d]
        g_b = rest_ref[:, 2048 + d:2048 + 2 * d]
        ov = o_ref[...]
        sa = _sigmoid(z_a)
        silu_a = z_a * sa
        y_a = ov * silu_a
        ug, dug_du = _gelu_and_grad(u_b)
        vg, dvg_dv = _gelu_and_grad(v_b)
        mu = gmean(vg)
        cen = vg - mu
        rstd_g = lax.rsqrt(gmean(cen * cen) + EPS)
        vhat = cen * rstd_g
        vn = vhat * lng_ref[...] + lnb_ref[...]
        vnb = vn.astype(BF16)

        t_idx = lax.broadcasted_iota(jnp.int32, (SGU_CHUNK, SGU_CHUNK), 0)
        s_idx = lax.broadcasted_iota(jnp.int32, (SGU_CHUNK, SGU_CHUNK), 1)
        causal = (s_idx // CHUNK) <= (t_idx // CHUNK)
        wm = [jnp.where(causal, wsp_ref[g], 0.0) for g in range(N_GROUPS)]
        wmb = [w.astype(BF16) for w in wm]
        wmtb = [w.T.astype(BF16) for w in wm]
        lane = lax.broadcasted_iota(jnp.int32, (SGU_CHUNK, LANES), 1)
        first = lane < GROUP_DIM
        bfull = bfull_ref[...]

        mixed_rows = []
        for n in range(nchunk):
            r0, r1 = n * SGU_CHUNK, (n + 1) * SGU_CHUNK
            pieces = []
            for p in range(N_GROUPS // 2):
                blk = vnb[r0:r1, p * LANES:(p + 1) * LANES]
                pieces.append(jnp.where(first, _dot(wmb[2 * p], blk), _dot(wmb[2 * p + 1], blk)))
            mixed_rows.append(jnp.concatenate(pieces, axis=1) + bfull)
        mixed = jnp.concatenate(mixed_rows, axis=0)
        sg = ug * mixed
        sb = _sigmoid(z_b)
        silu_b = z_b * sb
        y_b = sg * silu_b
        y_ab = y_a.astype(BF16)
        y_bb = y_b.astype(BF16)
        p_a = _dot(y_ab, wua_ref[...])
        p_b = _dot(y_bb, wub_ref[...])
        ga_s = _sigmoid(g_a)
        gb_s = _sigmoid(g_b)
        merged_b = (ga_s * p_a + gb_s * p_b).astype(BF16)
        x2 = x_ref[...] + _dot(merged_b, wout_ref[...])
        rstd = lax.rsqrt(jnp.mean(x2 * x2, axis=-1, keepdims=True) + EPS)
        xhat = x2 * rstd
        fg_v = fg_ref[...]
        diff = xhat * fg_v - t_ref[...]
        loss_ref[...] += 0.5 * jnp.sum(jnp.sum(diff * diff, axis=-1, keepdims=True) * (1.0 / d))

        dy = diff * (1.0 / d)
        dfg_ref[...] += colsum8(dy * xhat)
        dxh = dy * fg_v
        dx2 = rstd * (dxh - xhat * jnp.mean(dxh * xhat, axis=-1, keepdims=True))
        dx2_ref[...] = dx2
        dx2b = dx2.astype(BF16)
        dwout_ref[...] += _dot_tn(merged_b, dx2b)
        dmerged = _dot_nt(dx2b, wout_ref[...])
        dp_a = dmerged * ga_s
        dp_b = dmerged * gb_s
        dproj_ref[:, QKV_COLS + 2048:QKV_COLS + 2048 + d] = (dmerged * p_a * (ga_s * (1.0 - ga_s))).astype(BF16)
        dproj_ref[:, QKV_COLS + 2048 + d:QKV_COLS + 2048 + 2 * d] = (dmerged * p_b * (gb_s * (1.0 - gb_s))).astype(BF16)
        dp_ab = dp_a.astype(BF16)
        dp_bb = dp_b.astype(BF16)
        dwua_ref[...] += _dot_tn(y_ab, dp_ab)
        dwub_ref[...] += _dot_tn(y_bb, dp_bb)
        dy_a = _dot_nt(dp_ab, wua_ref[...])
        dy_b = _dot_nt(dp_bb, wub_ref[...])
        do_ref[...] = (dy_a * silu_a).astype(BF16)
        dproj_ref[:, QKV_COLS:QKV_COLS + 512] = (dy_a * ov * (sa * (1.0 + z_a * (1.0 - sa)))).astype(BF16)
        dsg = dy_b * silu_b
        dproj_ref[:, QKV_COLS + 1536:QKV_COLS + 2048] = (dy_b * sg * (sb * (1.0 + z_b * (1.0 - sb)))).astype(BF16)
        dproj_ref[:, QKV_COLS + 512:QKV_COLS + 1024] = (dsg * mixed * dug_du).astype(BF16)
        dmixed = dsg * ug
        dmb = dmixed.astype(BF16)
        zero = jnp.zeros((), BF16)
        dvn_rows = []
        db = jnp.zeros((SGU_CHUNK, D_SGU), F32)
        for n in range(nchunk):
            r0, r1 = n * SGU_CHUNK, (n + 1) * SGU_CHUNK
            db = db + dmixed[r0:r1, :]
            pieces = []
            for p in range(N_GROUPS // 2):
                cols = slice(p * LANES, (p + 1) * LANES)
                dm_blk = dmb[r0:r1, cols]
                vn_blk = vnb[r0:r1, cols]
                dwsp_ref[2 * p] += _dot_nt(jnp.where(first, dm_blk, zero), vn_blk)
                dwsp_ref[2 * p + 1] += _dot_nt(jnp.where(first, zero, dm_blk), vn_blk)
                pieces.append(jnp.where(first, _dot(wmtb[2 * p], dm_blk), _dot(wmtb[2 * p + 1], dm_blk)))
            dvn_rows.append(jnp.concatenate(pieces, axis=1))
        dbfull_ref[...] += db
        dvn = jnp.concatenate(dvn_rows, axis=0)
        dlng_ref[...] += colsum8(dvn * vhat)
        dlnb_ref[...] += colsum8(dvn)
        dvhat = dvn * lng_ref[...]
        dcen = rstd_g * (dvhat - gmean(dvhat) - vhat * gmean(dvhat * vhat))
        dproj_ref[:, QKV_COLS + 1024:QKV_COLS + 1536] = (dcen * dvg_dv).astype(BF16)

        @pl.when(step == nt - 1)
        def _():
            for g in range(N_GROUPS):
                dwsp_ref[g] = jnp.where(causal, dwsp_ref[g], 0.0)
            cb = d // N_DEV
            for k in range(N_DEV):
                dwua_out[k] = dwua_ref[:, k * cb:(k + 1) * cb].astype(BF16)
                dwub_out[k] = dwub_ref[:, k * cb:(k + 1) * cb].astype(BF16)
            dwout_out[...] = dwout_ref[...].astype(BF16)

    def tile(cols):
        return pl.BlockSpec((ts, cols), lambda i: (i, 0))

    def whole(shape):
        return pl.BlockSpec(shape, lambda i: (0,) * len(shape))

    out_shapes = [
        jax.ShapeDtypeStruct((8, LANES), F32),
        jax.ShapeDtypeStruct((s, d), F32),
        jax.ShapeDtypeStruct((s, D_SB), BF16),
        jax.ShapeDtypeStruct((s, QKV_COLS + n_rest), BF16),
        jax.ShapeDtypeStruct((N_DEV, D_SB, d // N_DEV), BF16),
        jax.ShapeDtypeStruct((N_DEV, D_SGU, d // N_DEV), BF16),
        jax.ShapeDtypeStruct((d, d), BF16),
        jax.ShapeDtypeStruct((8, d), F32),
        jax.ShapeDtypeStruct((8, D_SGU), F32),
        jax.ShapeDtypeStruct((8, D_SGU), F32),
        jax.ShapeDtypeStruct((N_GROUPS, SGU_CHUNK, SGU_CHUNK), F32),
        jax.ShapeDtypeStruct((SGU_CHUNK, D_SGU), F32),
    ]
    out_specs = [whole((8, LANES)), tile(d), tile(D_SB), tile(QKV_COLS + n_rest), whole((N_DEV, D_SB, d // N_DEV)),
                 whole((N_DEV, D_SGU, d // N_DEV)), whole((d, d)), whole((8, d)), whole((8, D_SGU)), whole((8, D_SGU)),
                 whole((N_GROUPS, SGU_CHUNK, SGU_CHUNK)), whole((SGU_CHUNK, D_SGU))]
    in_specs = [tile(D_SB), tile(n_rest), tile(d), tile(d), whole((D_SB, d)), whole((D_SGU, d)), whole((d, d)),
                whole((1, d)), whole((1, D_SGU)), whole((1, D_SGU)), whole((N_GROUPS, SGU_CHUNK, SGU_CHUNK)),
                whole((SGU_CHUNK, D_SGU)), whole((D_SGU, D_SGU))]
    return pl.pallas_call(
        body, name=name, out_shape=out_shapes, grid=(nt,), in_specs=in_specs, out_specs=out_specs,
        scratch_shapes=[pltpu.VMEM((D_SB, d), F32), pltpu.VMEM((D_SGU, d), F32), pltpu.VMEM((d, d), F32)],
        compiler_params=_cparams(("arbitrary",)),
    )(o, rest, x, tgt, wua, wub, wout, fg, lng, lnb, wsp, bfull, gavg)


def _small_reduce(dfg8, dlng8, dlnb8, dbfull, name):
    d = dfg8.shape[1]

    def body(dfg_ref, dlng_ref, dlnb_ref, dbfull_ref, fg_out, lng_out, lnb_out, b_out):
        row = lax.broadcasted_iota(jnp.int32, (D_SGU, LANES), 0)
        col = lax.broadcasted_iota(jnp.int32, (D_SGU, LANES), 1)

        def select_sum(a, sel):
            hi, lo = _split_hi_lo(a)
            lo2 = (a - hi.astype(F32) - lo.astype(F32)).astype(BF16)
            return _dot(hi, sel) + _dot(lo, sel) + _dot(lo2, sel)

        def by_group(partials):
            v = jnp.sum(partials, axis=0, keepdims=True)
            g_of_lane = lax.broadcasted_iota(jnp.int32, (N_GROUPS, D_SGU), 1) // GROUP_DIM
            g_of_row = lax.broadcasted_iota(jnp.int32, (N_GROUPS, D_SGU), 0)
            spread = jnp.where(g_of_lane == g_of_row, jnp.broadcast_to(v, (N_GROUPS, D_SGU)), 0.0)
            return select_sum(spread, (row % GROUP_DIM == col).astype(BF16))[:, 0:GROUP_DIM]

        fg_out[...] = jnp.sum(dfg_ref[...], axis=0, keepdims=True)
        lng_out[...] = by_group(dlng_ref[...])
        lnb_out[...] = by_group(dlnb_ref[...])
        by_pos = select_sum(dbfull_ref[...], (row // GROUP_DIM == col).astype(BF16))
        b_out[...] = by_pos.T[0:N_GROUPS, :]

    return pl.pallas_call(
        body, name=name,
        out_shape=[jax.ShapeDtypeStruct((1, d), F32), jax.ShapeDtypeStruct((N_GROUPS, GROUP_DIM), F32),
                   jax.ShapeDtypeStruct((N_GROUPS, GROUP_DIM), F32), jax.ShapeDtypeStruct((N_GROUPS, SGU_CHUNK), F32)],
        compiler_params=_cparams(),
    )(dfg8, dlng8, dlnb8, dbfull)


def _from_block_major_cols(w):
    nb, r, c = w.shape
    return jnp.transpose(w, (1, 0, 2)).reshape(r, nb * c)


def kernel(x, norm_g, w_in, sgu_ln_g, sgu_ln_b, w_spatial, b_spatial, w_up_a, w_up_b, w_out, final_norm_g, loss_target, m_norm_g, m_w_in, m_sgu_ln_g, m_sgu_ln_b, m_w_spatial, m_b_spatial, m_w_up_a, m_w_up_b, m_w_out, m_final_norm_g, v_norm_g, v_w_in, v_sgu_ln_g, v_sgu_ln_b, v_w_spatial, v_b_spatial, v_w_up_a, v_w_up_b, v_w_out, v_final_norm_g):
    s, d = x.shape[1], x.shape[2]
    xs = x[0]
    tgt = loss_target[0]
    cx, cy, cc = _coords()
    core = jnp.reshape(cc, (1,)).astype(jnp.int32)
    chip = jnp.reshape(2 * cx + cy, (1,)).astype(jnp.int32)

    h, ht, (g_win,) = _rmsnorm_fwd(xs, norm_g, [jnp.transpose(w_in[0])], "norm")
    d_in = N_DEV * w_in.shape[2]
    w_full = g_win.reshape(d_in, d)
    qkv, rest = _inproj(h, w_full, "inproj")
    o, rs, extra, (g_wua, g_wub, g_wout) = _attn_fwd(
        qkv, [w_up_a[0].astype(BF16), w_up_b[0].astype(BF16), w_out[0].astype(BF16)], "attn_fwd")
    wua_full = _from_block_major_cols(g_wua)
    wub_full = _from_block_major_cols(g_wub)
    wout_full = g_wout.reshape(d, d)

    lng = sgu_ln_g.reshape(1, D_SGU)
    lnb = sgu_ln_b.reshape(1, D_SGU)
    bfull = jnp.repeat(jnp.transpose(b_spatial[0]), GROUP_DIM, axis=1)
    grp = jnp.arange(D_SGU) // GROUP_DIM
    gavg = jnp.where(grp[:, None] == grp[None, :], 1.0 / GROUP_DIM, 0.0).astype(BF16)
    (loss_b, dx2, do, dproj, dwua, dwub, dwout, dfg8, dlng8, dlnb8, dwsp, dbfull) = _mid(
        o, rest, xs, tgt, wua_full, wub_full, wout_full, final_norm_g.reshape(1, d), lng, lnb, w_spatial[0], bfull,
        gavg, "mid")

    own_up = [dwua, dwub, dwout.reshape(N_DEV, d // N_DEV, d)]
    dfg, dlng, dlnb, db = _small_reduce(dfg8, dlng8, dlnb8, dbfull, "small_reduce")
    wsp_rows = N_GROUPS * SGU_CHUNK
    dproj, small_parts, land_up = _attn_bwd(
        qkv, do, rs, extra, dproj, [dfg, dlng, dlnb, db, dwsp.reshape(wsp_rows, SGU_CHUNK)], own_up, "attn_bwd")
    dwin = _dw_in(ht, dproj, "dwin")

    own_in = [dwin.reshape(4, 2, d_in // N_DEV, d)]
    cps_in = list(_chip_partial_sums(own_in, list(_push_sibling(own_in, "rs_sibling_in")), core, "cpsum_in"))
    grad_x, dng8, land_in = _dh_and_grad_x(dproj, w_full, xs, norm_g, dx2, cps_in, "dh")

    ng_parts, loss_parts = _allgather([dng8, loss_b], "ag_tail")

    def small_layouts(ng, fg, lg, lb, bs, ws):
        return [ng.reshape(1, d), fg.reshape(1, d), lg.reshape(N_GROUPS, GROUP_DIM), lb.reshape(N_GROUPS, GROUP_DIM),
                bs.reshape(N_GROUPS, SGU_CHUNK), ws.reshape(wsp_rows, SGU_CHUNK)]

    sm, loss_sum = _adam_small(
        [ng_parts] + list(small_parts),
        small_layouts(norm_g, final_norm_g, sgu_ln_g, sgu_ln_b, b_spatial, w_spatial),
        small_layouts(m_norm_g, m_final_norm_g, m_sgu_ln_g, m_sgu_ln_b, m_b_spatial, m_w_spatial),
        small_layouts(v_norm_g, v_final_norm_g, v_sgu_ln_g, v_sgu_ln_b, v_b_spatial, v_w_spatial),
        loss_parts, "adam_small")
    small_shapes = [norm_g.shape, final_norm_g.shape, sgu_ln_g.shape, sgu_ln_b.shape, b_spatial.shape, w_spatial.shape]
    sm = [[a.reshape(shp) for a, shp in zip(kind, small_shapes)] for kind in sm]

    res = _adam_shard(cps_in, list(land_in), chip, *[jnp.transpose(a[0]) for a in (w_in, m_w_in, v_w_in)], "adam0")
    big = [[jnp.transpose(r)[None] for r in res]]
    dev = jnp.reshape(_dev_index(cx, cy, cc), (1,)).astype(jnp.int32)
    res = _adam_shards_whole(own_up, list(land_up), dev, [w_up_a[0], w_up_b[0], w_out[0]],
                             [m_w_up_a[0], m_w_up_b[0], m_w_out[0]], [v_w_up_a[0], v_w_up_b[0], v_w_out[0]], "adam_up")
    for i in range(3):
        big.append([res[kd][i][None] for kd in range(4)])

    loss = loss_sum[0, 0]

    def per_kind(kd):
        return [sm[kd][0], big[0][kd], sm[kd][2], sm[kd][3], sm[kd][5], sm[kd][4], big[1][kd], big[2][kd], big[3][kd],
                sm[kd][1]]

    return (loss, grad_x[None], *per_kind(0), *per_kind(1), *per_kind(2), *per_kind(3))
```

```python
import functools
import math

import jax
import jax.numpy as jnp
from jax import lax
from jax.experimental import pallas as pl
from jax.experimental.pallas import tpu as pltpu

F32 = jnp.float32
BF16 = jnp.bfloat16
MESH = pl.DeviceIdType.MESH

N_DEV = 8
N_HEADS = 8
HEAD_DIM = 64
D_SB = N_HEADS * HEAD_DIM
N_GROUPS = 8
GROUP_DIM = 64
D_SGU = N_GROUPS * GROUP_DIM
SGU_CHUNK = 128
CHUNK = 64
EPS = 1e-6
LANES = 128
N_PAIRS = N_HEADS // 2
QKV_COLS = 3 * D_SB
ATT_BLOCK = 256
ATT_TILE = 128
CARRY_FLOOR = -90.0
R_UNREACHED = -1e30

ADAM_LR = 0.001
ADAM_B1 = 0.9
ADAM_B2 = 0.999
ADAM_EPS = 1e-08
ADAM_WD = 0.01
ADAM_STEP = 10

VMEM_LIMIT = 56 * 1024 * 1024


def _cparams(sem=None, vmem=VMEM_LIMIT):
    return pltpu.CompilerParams(dimension_semantics=sem, vmem_limit_bytes=vmem)


def _dot(a, b):
    return jnp.dot(a, b, preferred_element_type=F32)


def _dot_nt(a, b):
    return lax.dot_general(a, b, (((1,), (1,)), ((), ())), preferred_element_type=F32)


def _dot_tn(a, b):
    return lax.dot_general(a, b, (((0,), (0,)), ((), ())), preferred_element_type=F32)


def _split_hi_lo(a):
    hi = a.astype(BF16)
    lo = (a - hi.astype(F32)).astype(BF16)
    return hi, lo


def _sigmoid(x):
    return 1.0 / (1.0 + jnp.exp(-x))


_GELU_C = math.sqrt(2.0 / math.pi)


def _gelu_and_grad(x):
    x2 = x * x
    inner = _GELU_C * (x + 0.044715 * (x2 * x))
    t = jnp.tanh(inner)
    cdf = 0.5 * (1.0 + t)
    g = x * cdf
    dg = cdf + x * (0.5 * (1.0 - t * t)) * (_GELU_C * (1.0 + 3.0 * 0.044715 * x2))
    return g, dg


def _coords():
    return lax.axis_index("x"), lax.axis_index("y"), lax.axis_index("c")


def _dev_index(px, py, pc):
    return 4 * px + 2 * py + pc


def _allgather(blocks, name):
    n = len(blocks)

    def body(*refs):
        gather = _Gather(refs[:n], refs[n:2 * n], *refs[2 * n:])
        gather.issue()
        gather.finish()

    any_spec = pl.BlockSpec(memory_space=pl.ANY)
    return pl.pallas_call(
        body, name=name,
        out_shape=_gather_out_shapes(blocks),
        in_specs=[any_spec] * n, out_specs=[any_spec] * n,
        scratch_shapes=_gather_semaphores(n),
    )(*blocks)


def _gather_out_shapes(blocks):
    return [jax.ShapeDtypeStruct((N_DEV,) + b.shape, b.dtype) for b in blocks]


def _gather_semaphores(n):
    return [pltpu.SemaphoreType.DMA((n, 7)), pltpu.SemaphoreType.DMA((n, 7)), pltpu.SemaphoreType.DMA((n,))]


class _Gather:
    def __init__(self, ins, outs, send_sems, recv_sems, local_sems, relay=False):
        self.ins, self.outs = ins, outs
        self.send_sems, self.recv_sems, self.local_sems = send_sems, recv_sems, local_sems
        self.n = len(ins)
        self.relay = relay
        x, y, c = _coords()
        self.c = c
        self.me, self.sibling = (x, y, c), (x, y, 1 - c)
        self.chips = [(1 - x, y), (x, 1 - y), (1 - x, 1 - y)]

    def _copy(self, a, k, block, to, src=None):
        dst = self.outs[a].at[_dev_index(*block)]
        return pltpu.make_async_remote_copy(
            src_ref=dst if src is None else src, dst_ref=dst,
            send_sem=self.send_sems.at[a, k], recv_sem=self.recv_sems.at[a, k],
            device_id=to, device_id_type=MESH)

    def _mine(self):
        return [pltpu.make_async_copy(self.ins[a], self.outs[a].at[_dev_index(*self.me)], self.local_sems.at[a])
                for a in range(self.n)]

    def _first(self):
        first = []
        direct = self.chips[:2] if self.relay else self.chips
        for a in range(self.n):
            first.append(self._copy(a, 0, self.me, self.sibling, src=self.ins[a]))
            first += [self._copy(a, 1 + j, self.me, (*chip, self.c), src=self.ins[a])
                      for j, chip in enumerate(direct)]
        return first

    def issue(self):
        for cp in self._mine() + self._first():
            cp.start()

    def _pass_on(self, a, j):
        chip = self.chips[j]
        self._copy(a, 1 + j, (*chip, self.c), self.me).wait_recv()
        self._copy(a, 4 + j, (*chip, self.c), self.sibling).start()

    def finish(self):
        c = self.c
        if self.relay:
            for core in range(2):
                @pl.when(c == core)
                def _(core=core):
                    j_src, j_dst = core, 1 - core
                    for a in range(self.n):
                        self._pass_on(a, j_src)
                        self._copy(a, 3, (*self.chips[j_src], c), (*self.chips[j_dst], c)).start()
                    for a in range(self.n):
                        self._pass_on(a, j_dst)
                    for a in range(self.n):
                        self._pass_on(a, 2)
        else:
            for j in range(3):
                for a in range(self.n):
                    self._pass_on(a, j)
        for a in range(self.n):
            self._copy(a, 0, self.sibling, self.me).wait_recv()
            for j, chip in enumerate(self.chips):
                self._copy(a, 4 + j, (*chip, 1 - c), self.me).wait_recv()
        for a in range(self.n):
            for k in range(7):
                self._copy(a, k, self.me, self.sibling).wait_send()
        for cp in self._mine():
            cp.wait()


def _push_sibling(arrs, name):
    n = len(arrs)

    def body(*refs):
        ins, outs = refs[:n], refs[n:2 * n]
        send_sems, recv_sems = refs[2 * n:]
        x, y, c = _coords()
        sibling = (x, y, 1 - c)
        copies = []
        for a in range(n):
            for k in range(4):
                copies.append(pltpu.make_async_remote_copy(
                    src_ref=ins[a].at[k, 1 - c], dst_ref=outs[a].at[k],
                    send_sem=send_sems.at[a, k], recv_sem=recv_sems.at[a, k],
                    device_id=sibling, device_id_type=MESH))
        for cp in copies:
            cp.start()
        for cp in copies:
            cp.wait()

    any_spec = pl.BlockSpec(memory_space=pl.ANY)
    return pl.pallas_call(
        body, name=name,
        out_shape=[jax.ShapeDtypeStruct((4,) + a.shape[2:], a.dtype) for a in arrs],
        in_specs=[any_spec] * n, out_specs=[any_spec] * n,
        scratch_shapes=[pltpu.SemaphoreType.DMA((n, 4)), pltpu.SemaphoreType.DMA((n, 4))],
    )(*arrs)


def _chip_push_copies(ins, outs, send_sems, recv_sems):
    x, y, c = _coords()
    chips = [(1 - x, y), (x, 1 - y), (1 - x, 1 - y)]
    return [pltpu.make_async_remote_copy(
        src_ref=ins[a].at[2 * px + py], dst_ref=outs[a].at[r],
        send_sem=send_sems.at[a, r], recv_sem=recv_sems.at[a, r],
        device_id=(px, py, c), device_id_type=MESH)
        for a in range(len(ins)) for r, (px, py) in enumerate(chips)]


def _direct_push_copies(ins, outs, send_sems, recv_sems):
    x, y, c = _coords()
    copies = []
    for a in range(len(ins)):
        for r in range(1, N_DEV):
            px = 1 - x if r & 4 else x
            py = 1 - y if r & 2 else y
            pc = 1 - c if r & 1 else c
            copies.append(pltpu.make_async_remote_copy(
                src_ref=ins[a].at[_dev_index(px, py, pc)], dst_ref=outs[a].at[r - 1],
                send_sem=send_sems.at[a, r - 1], recv_sem=recv_sems.at[a, r - 1],
                device_id=(px, py, pc), device_id_type=MESH))
    return copies


def _chip_partial_sums(owns, lands, core, name):
    n = len(owns)
    blocks = [a.shape[2:] for a in owns]

    def body(core_ref, *refs):
        del core_ref
        for own_ref, land_ref, out_ref in zip(refs[:n], refs[n:2 * n], refs[2 * n:]):
            out_ref[...] = (own_ref[...].astype(F32) + land_ref[...].astype(F32)).astype(out_ref.dtype)

    return pl.pallas_call(
        body, name=name,
        out_shape=[jax.ShapeDtypeStruct((4,) + b, a.dtype) for a, b in zip(owns, blocks)],
        grid_spec=pltpu.PrefetchScalarGridSpec(
            num_scalar_prefetch=1, grid=(4,),
            in_specs=[pl.BlockSpec((None, None) + b, lambda k, core: (k, core[0], 0, 0)) for b in blocks]
            + [pl.BlockSpec((None,) + b, lambda k, core: (k, 0, 0)) for b in blocks],
            out_specs=[pl.BlockSpec((None,) + b, lambda k, core: (k, 0, 0)) for b in blocks]),
        compiler_params=_cparams(("parallel",)),
    )(core, *owns, *lands)


def _adamw_math(w, g, m, v):
    m = ADAM_B1 * m + (1.0 - ADAM_B1) * g
    v = ADAM_B2 * v + (1.0 - ADAM_B2) * (g * g)
    m_hat = m / (1.0 - ADAM_B1 ** ADAM_STEP)
    v_hat = v / (1.0 - ADAM_B2 ** ADAM_STEP)
    delta = -ADAM_LR * (m_hat / (jnp.sqrt(v_hat) + ADAM_EPS) + ADAM_WD * w)
    return delta, m, v


def _adam_shard(cps, lands, chip, w, m, v, name):
    rows, cols = w.shape
    tr = rows // 4
    nparts = len(cps)

    def body(chip_ref, *refs):
        del chip_ref
        cp_refs, land_refs = refs[:nparts], refs[nparts:2 * nparts]
        w_ref, m_ref, v_ref, g_out, d_out, m_out, v_out = refs[2 * nparts:]
        parts = []
        for cp_ref, land_ref in zip(cp_refs, land_refs):
            g_k = cp_ref[...].astype(F32)
            for r in range(3):
                g_k = g_k + land_ref[r].astype(F32)
            parts.append(g_k)
        g = parts[0] if nparts == 1 else jnp.concatenate(parts, axis=1)
        delta, m_new, v_new = _adamw_math(w_ref[...], g, m_ref[...], v_ref[...])
        g_out[...] = g
        d_out[...] = delta
        m_out[...] = m_new
        v_out[...] = v_new

    tile = pl.BlockSpec((tr, cols), lambda r, chip: (r, 0))
    out = jax.ShapeDtypeStruct((rows, cols), F32)
    return pl.pallas_call(
        body, name=name, out_shape=[out] * 4,
        grid_spec=pltpu.PrefetchScalarGridSpec(
            num_scalar_prefetch=1, grid=(rows // tr,),
            in_specs=[pl.BlockSpec((None, tr, a.shape[2]), lambda r, chip: (chip[0], r, 0)) for a in cps]
            + [pl.BlockSpec((3, tr, a.shape[2]), lambda r, chip: (0, r, 0)) for a in lands]
            + [tile, tile, tile],
            out_specs=[tile] * 4),
        compiler_params=_cparams(("parallel",)),
    )(chip, *cps, *lands, w, m, v)


def _adam_shards_whole(owns, lands, dev, ws, ms, vs, name):
    n = len(ws)

    def body(dev_ref, *refs):
        own_refs, land_refs = refs[:n], refs[n:2 * n]
        w_refs, m_refs, v_refs = refs[2 * n:3 * n], refs[3 * n:4 * n], refs[4 * n:5 * n]
        outs = refs[5 * n:]
        for i in range(n):
            g = own_refs[i][dev_ref[0]].astype(F32)
            for r in range(N_DEV - 1):
                g = g + land_refs[i][r].astype(F32)
            delta, m_new, v_new = _adamw_math(w_refs[i][...], g, m_refs[i][...], v_refs[i][...])
            outs[i][...] = g
            outs[n + i][...] = delta
            outs[2 * n + i][...] = m_new
            outs[3 * n + i][...] = v_new

    vmem = pl.BlockSpec(memory_space=pltpu.VMEM)
    res = pl.pallas_call(
        body, name=name, out_shape=[jax.ShapeDtypeStruct(w.shape, F32) for w in ws] * 4,
        in_specs=[pl.BlockSpec(memory_space=pltpu.SMEM)] + [vmem] * (5 * n), out_specs=[vmem] * (4 * n),
        compiler_params=_cparams(),
    )(dev, *owns, *lands, *ws, *ms, *vs)
    return [res[k * n:(k + 1) * n] for k in range(4)]


def _adam_small(parts, ws, ms, vs, loss_parts, name):
    n = len(ws)

    def body(*refs):
        p_refs, w_refs, m_refs, v_refs = refs[:n], refs[n:2 * n], refs[2 * n:3 * n], refs[3 * n:4 * n]
        loss_ref, outs, loss_out = refs[4 * n], refs[4 * n + 1:-1], refs[-1]
        total = loss_ref[0]
        for dev in range(1, N_DEV):
            total = total + loss_ref[dev]
        loss_out[...] = total
        for i in range(n):
            g = p_refs[i][0]
            for dev in range(1, N_DEV):
                g = g + p_refs[i][dev]
            if g.shape[0] != w_refs[i].shape[0]:
                g = jnp.sum(g, axis=0, keepdims=True)
            delta, m_new, v_new = _adamw_math(w_refs[i][...], g, m_refs[i][...], v_refs[i][...])
            outs[i][...] = g
            outs[n + i][...] = delta
            outs[2 * n + i][...] = m_new
            outs[3 * n + i][...] = v_new

    out_shapes = [jax.ShapeDtypeStruct(w.shape, F32) for w in ws] * 4
    out_shapes.append(jax.ShapeDtypeStruct(loss_parts.shape[1:], F32))
    res = pl.pallas_call(body, name=name, out_shape=out_shapes, compiler_params=_cparams())(
        *parts, *ws, *ms, *vs, loss_parts)
    return [res[k * n:(k + 1) * n] for k in range(4)], res[-1]


def _rmsnorm_fwd(x, g, riders, name):
    s, d = x.shape
    ts = 512
    nt = s // ts
    nr = len(riders)

    def body(x_ref, g_ref, *rest):
        rider_in, h_ref = rest[:nr], rest[nr]
        rider_out, staged = rest[nr + 1:2 * nr + 1], rest[2 * nr + 1:3 * nr + 1]
        sems = rest[3 * nr + 1:]
        i = pl.program_id(0)
        gather = _Gather(staged, rider_out, *sems, relay=True)

        @pl.when(i == 0)
        def _():
            for src, dst in zip(rider_in, staged):
                dst[...] = src[...].astype(BF16)
            gather.issue()

        xv = x_ref[...]
        rstd = lax.rsqrt(jnp.mean(xv * xv, axis=-1, keepdims=True) + EPS)
        h = xv * rstd * g_ref[...]
        h_ref[...] = h.astype(BF16)

        @pl.when(i == nt - 1)
        def _():
            gather.finish()

    any_spec = pl.BlockSpec(memory_space=pl.ANY)
    res = pl.pallas_call(
        body, name=name,
        out_shape=[jax.ShapeDtypeStruct((s, d), BF16)]
        + [jax.ShapeDtypeStruct((N_DEV,) + r.shape, BF16) for r in riders],
        grid=(nt,),
        in_specs=[pl.BlockSpec((ts, d), lambda i: (i, 0)), pl.BlockSpec((1, d), lambda i: (0, 0))]
        + [pl.BlockSpec(r.shape, lambda i: (0, 0)) for r in riders],
        out_specs=[pl.BlockSpec((ts, d), lambda i: (i, 0))] + [any_spec] * nr,
        scratch_shapes=[pltpu.VMEM(r.shape, BF16) for r in riders] + _gather_semaphores(nr),
        compiler_params=_cparams(("arbitrary",)),
    )(x, g, *riders)
    return res[0], res[1:]


def _inproj(h, wt, name):
    s, d = h.shape
    n = wt.shape[0]
    tn = 256
    n_qkv = QKV_COLS // tn

    def body(h_ref, w_ref, qkv_ref, rest_ref):
        j = pl.program_id(0)
        res = _dot_nt(h_ref[...], w_ref[...])

        @pl.when(j < n_qkv)
        def _():
            qkv_ref[...] = res.astype(BF16)

        @pl.when(j >= n_qkv)
        def _():
            rest_ref[...] = res

    return pl.pallas_call(
        body, name=name,
        out_shape=[jax.ShapeDtypeStruct((s, QKV_COLS), BF16), jax.ShapeDtypeStruct((s, n - QKV_COLS), F32)],
        grid=(n // tn,),
        in_specs=[pl.BlockSpec((s, d), lambda j: (0, 0)), pl.BlockSpec((tn, d), lambda j: (j, 0))],
        out_specs=[pl.BlockSpec((s, tn), lambda j: (0, jnp.minimum(j, n_qkv - 1))),
                   pl.BlockSpec((s, tn), lambda j: (0, jnp.maximum(j - n_qkv, 0)))],
        compiler_params=_cparams(("arbitrary",)),
    )(h, wt)


def _dw_in(h, dproj, name):
    s, d = h.shape
    n = dproj.shape[1]
    tn = 512

    def body(h_ref, b_ref, o_ref):
        o_ref[...] = _dot_tn(b_ref[...], h_ref[...]).astype(o_ref.dtype)

    return pl.pallas_call(
        body, name=name, out_shape=jax.ShapeDtypeStruct((n, d), BF16), grid=(n // tn,),
        in_specs=[pl.BlockSpec((s, d), lambda j: (0, 0)), pl.BlockSpec((s, tn), lambda j: (0, j))],
        out_specs=pl.BlockSpec((tn, d), lambda j: (j, 0)),
        compiler_params=_cparams(("parallel",)),
    )(h, dproj)


def _dh_and_grad_x(dproj, wt, x, g, dx2, cps, name):
    s, n = dproj.shape
    d = wt.shape[1]
    tm = min(s, 1024)
    tk = next(t for t in (1408, 512) if n % t == 0)
    nk = n // tk
    nm = s // tm
    nc = len(cps)

    def body(a_ref, w_ref, x_ref, g_ref, dx2_ref, *rest):
        cp_refs, gx_ref, dg_ref = rest[:nc], rest[nc], rest[nc + 1]
        land_refs = rest[nc + 2:2 * nc + 2]
        acc_ref, send_sems, recv_sems = rest[2 * nc + 2:]
        i, k = pl.program_id(0), pl.program_id(1)

        @pl.when((i == 0) & (k == 0))
        def _():
            for cp in _chip_push_copies(cp_refs, land_refs, send_sems, recv_sems):
                cp.start()
            dg_ref[...] = jnp.zeros_like(dg_ref)

        @pl.when(k == 0)
        def _():
            acc_ref[...] = jnp.zeros_like(acc_ref)

        acc_ref[...] += _dot(a_ref[...], w_ref[...])

        @pl.when(k == nk - 1)
        def _():
            dh = acc_ref[...]
            xv = x_ref[...]
            rstd = lax.rsqrt(jnp.mean(xv * xv, axis=-1, keepdims=True) + EPS)
            xhat = xv * rstd
            dg_ref[...] += jnp.sum((dh * xhat).reshape(tm // 8, 8, d), axis=0)
            dxh = dh * g_ref[...]
            gx_ref[...] = dx2_ref[...] + rstd * (dxh - xhat * jnp.mean(dxh * xhat, axis=-1, keepdims=True))

        @pl.when((i == nm - 1) & (k == nk - 1))
        def _():
            for cp in _chip_push_copies(cp_refs, land_refs, send_sems, recv_sems):
                cp.wait()

    any_spec = pl.BlockSpec(memory_space=pl.ANY)
    res = pl.pallas_call(
        body, name=name,
        out_shape=[jax.ShapeDtypeStruct((s, d), F32), jax.ShapeDtypeStruct((8, d), F32)]
        + [jax.ShapeDtypeStruct((3,) + a.shape[1:], a.dtype) for a in cps],
        grid=(nm, nk),
        in_specs=[pl.BlockSpec((tm, tk), lambda i, k: (i, k)), pl.BlockSpec((tk, d), lambda i, k: (k, 0)),
                  pl.BlockSpec((tm, d), lambda i, k: (i, 0)), pl.BlockSpec((1, d), lambda i, k: (0, 0)),
                  pl.BlockSpec((tm, d), lambda i, k: (i, 0))] + [any_spec] * nc,
        out_specs=[pl.BlockSpec((tm, d), lambda i, k: (i, 0)), pl.BlockSpec((8, d), lambda i, k: (0, 0))]
        + [any_spec] * nc,
        scratch_shapes=[pltpu.VMEM((tm, d), F32), pltpu.SemaphoreType.DMA((nc, 3)), pltpu.SemaphoreType.DMA((nc, 3))],
        compiler_params=_cparams(("arbitrary", "arbitrary")),
    )(dproj, wt, x, g, dx2, *cps)
    return res[0], res[1], res[2:]


def _log_sigmoids(z):
    l1p = jnp.log(1.0 + jnp.exp(-jnp.abs(z)))
    ls = jnp.minimum(z, 0.0) - l1p
    return ls, ls - z


def _strict_lower_ones(n):
    row = lax.broadcasted_iota(jnp.int32, (n, n), 0)
    col = lax.broadcasted_iota(jnp.int32, (n, n), 1)
    return row, col, (row > col).astype(BF16)


def _attn_fwd(qkv, riders, name):
    s = qkv.shape[0]
    tb, tt = ATT_BLOCK, ATT_TILE
    nq = s // tb
    per_block = tb // tt
    assert per_block == 2 and s // tt <= LANES, "two query tiles per grid step; one lane of saved carry per key tile"

    nr = len(riders)

    def sweep_body(gather, q_ref, k_ref, v_ref, o_ref, rs_ref, extra_ref, acc_ref, r_ref, rsv_ref, rmax_ref):
        i = pl.program_id(1)
        lane = lax.broadcasted_iota(jnp.int32, (tt, LANES), 1)
        hmask = [lane < HEAD_DIM, lane >= HEAD_DIM]
        row, col, tri = _strict_lower_ones(tt)
        tri2 = jnp.concatenate([tri, tri], axis=0)
        below = col < row
        qrows = [slice(u * tt, (u + 1) * tt) for u in range(per_block)]
        qm = [[jnp.where(m, q_ref[qrows[u], :], jnp.zeros((), BF16)) * jnp.asarray(HEAD_DIM ** -0.5, BF16)
               for m in hmask] for u in range(per_block)]
        acc_ref[...] = jnp.zeros_like(acc_ref)
        r_ref[...] = jnp.zeros_like(r_ref)
        rsv_ref[...] = jnp.full_like(rsv_ref, R_UNREACHED)

        def sweep(tiles, chains):
            heads, nch = range(2), range(len(chains))
            kv = []
            for t in tiles:
                rows = pl.ds(pl.multiple_of(t * tt, tt), tt)
                kv.append((k_ref[rows, :], v_ref[rows, :]))
            z = [[_dot_nt(qm[u][h], kv[ti][0]) for h in heads] for u, ti, _ in chains]
            ls, lk = [], []
            for n in nch:
                pairs = [_log_sigmoids(z[n][h]) for h in heads]
                ls.append([p[0] for p in pairs])
                lk.append([jnp.where(below, p[1], 0.0) if chains[n][2] else p[1] for p in pairs])
            cur = {u: [r_ref[h, qrows[u], :] for h in heads] for u in sorted({c[0] for c in chains})}
            r = []
            for n, (u, _, _) in enumerate(chains):
                r.append(cur[u])
                cur[u] = [cur[u][h] + jnp.sum(lk[n][h], axis=1, keepdims=True) for h in heads]
            for u in cur:
                rmax_ref[u] = jnp.max(jnp.maximum(cur[u][0], cur[u][1]))
            suffix = [[_dot(jnp.concatenate(_split_hi_lo(lk[n][h]), axis=1), tri2) for h in heads] for n in nch]
            w = []
            for n in nch:
                w_n = [jnp.exp(ls[n][h] + suffix[n][h] + r[n][h]) for h in heads]
                if chains[n][2]:
                    w_n = [jnp.where(below, w_h, 0.0) for w_h in w_n]
                w.append([w_h.astype(BF16) for w_h in w_n])
            pv = [[_dot(w[n][h], kv[chains[n][1]][1]) for h in heads] for n in nch]
            for u in cur:
                mine = [n for n in nch if chains[n][0] == u]
                for h in heads:
                    acc_ref[h, qrows[u], :] += functools.reduce(lambda a, b: a + b, [pv[n][h] for n in mine])
                    for n in mine:
                        if not chains[n][2]:
                            t = tiles[chains[n][1]]
                            rsv_ref[h, qrows[u], :] = jnp.where(lane == t, r[n][h], rsv_ref[h, qrows[u], :])
                    r_ref[h, qrows[u], :] = cur[u][h]

        first = per_block * i

        @pl.when(i == 0)
        def _():
            sweep([0, 1], [(0, 0, True), (1, 1, True), (1, 0, False)])

        @pl.when(i > 0)
        def _():
            sweep([first - 1, first, first + 1], [(0, 1, True), (1, 2, True), (0, 0, False), (1, 1, False)])

        for u in range(per_block):
            n_left = first + u - 1

            def live(c, n_left=n_left):
                jj, rmax = c
                return (jj < n_left) & (rmax >= CARRY_FLOOR)

            def step(c, u=u, n_left=n_left):
                jj, _ = c
                sweep([n_left - 1 - jj], [(u, 0, False)])
                return jj + 1, rmax_ref[u]

            swept, _ = lax.while_loop(live, step, (jnp.int32(0), rmax_ref[u]))
            extra_ref[pl.program_id(0), first + u] = swept.astype(F32)

        lane_b = lax.broadcasted_iota(jnp.int32, (tb, LANES), 1)
        o_ref[...] = jnp.where(lane_b < HEAD_DIM, acc_ref[0], acc_ref[1])
        rs_ref[...] = rsv_ref[...]

        @pl.when((pl.program_id(0) == N_PAIRS - 1) & (i == nq - 1))
        def _():
            gather.finish()

    def body(q_ref, k_ref, v_ref, *rest):
        rider_in, (o_ref, rs_ref, extra_ref) = rest[:nr], rest[nr:nr + 3]
        rider_out = rest[nr + 3:2 * nr + 3]
        acc_ref, r_ref, rsv_ref, rmax_ref = rest[2 * nr + 3:2 * nr + 7]
        gather = _Gather(rider_in, rider_out, *rest[2 * nr + 7:])

        @pl.when((pl.program_id(0) == 0) & (pl.program_id(1) == 0))
        def _():
            gather.issue()

        sweep_body(gather, q_ref, k_ref, v_ref, o_ref, rs_ref, extra_ref, acc_ref, r_ref, rsv_ref, rmax_ref)

    any_spec = pl.BlockSpec(memory_space=pl.ANY)
    res = pl.pallas_call(
        body, name=name,
        out_shape=[jax.ShapeDtypeStruct((s, D_SB), F32), jax.ShapeDtypeStruct((N_HEADS, s, LANES), F32),
                   jax.ShapeDtypeStruct((N_PAIRS, s // tt), F32)] + _gather_out_shapes(riders),
        grid=(N_PAIRS, nq),
        in_specs=[pl.BlockSpec((tb, LANES), lambda p, i: (i, p)),
                  pl.BlockSpec((s, LANES), lambda p, i: (0, N_PAIRS + p)),
                  pl.BlockSpec((s, LANES), lambda p, i: (0, 2 * N_PAIRS + p))] + [any_spec] * nr,
        out_specs=[pl.BlockSpec((tb, LANES), lambda p, i: (i, p)),
                   pl.BlockSpec((2, tb, LANES), lambda p, i: (p, i, 0)),
                   pl.BlockSpec(memory_space=pltpu.SMEM)] + [any_spec] * nr,
        scratch_shapes=[pltpu.VMEM((2, tb, LANES), F32), pltpu.VMEM((2, tb, LANES), F32),
                        pltpu.VMEM((2, tb, LANES), F32), pltpu.SMEM((per_block,), F32)] + _gather_semaphores(nr),
        compiler_params=_cparams(("arbitrary", "arbitrary")),
    )(qkv, qkv, qkv, *riders)
    return res[0], res[1], res[2], res[3:]


def _attn_bwd(qkv, do, rs, extra, dproj, smalls, cps, name):
    s = qkv.shape[0]
    tb, tt = ATT_BLOCK, ATT_TILE
    nq = s // tb
    per_block = tb // tt
    scale = HEAD_DIM ** -0.5

    ns, nc = len(smalls), len(cps)

    def sweep_body(q_ref, k_ref, v_ref, do_ref, rs_ref, extra_ref, dproj_hbm, out_hbm, dq_acc, dk_acc, dv_acc,
                   dqi_ref, pc_ref, stage_ref, out_sems):
        del dproj_hbm
        pair = pl.program_id(0)
        lane = lax.broadcasted_iota(jnp.int32, (tt, LANES), 1)
        hmask = [lane < HEAD_DIM, lane >= HEAD_DIM]
        row, col, tri = _strict_lower_ones(tt)
        tri2 = jnp.concatenate([tri, tri], axis=0)
        triu = (row < col).astype(BF16)
        below = col < row
        zero = jnp.zeros((), BF16)
        qrows = [slice(u * tt, (u + 1) * tt) for u in range(per_block)]
        dq_acc[...] = jnp.zeros_like(dq_acc)
        dk_acc[...] = jnp.zeros_like(dk_acc)
        dv_acc[...] = jnp.zeros_like(dv_acc)

        def qblock(i, carry):
            block_rows = pl.ds(pl.multiple_of(i * tb, tb), tb)
            q2 = q_ref[block_rows, :]
            do2 = do_ref[block_rows, :]
            qm = [[jnp.where(m, q2[qrows[u]], zero) * jnp.asarray(scale, BF16) for m in hmask]
                  for u in range(per_block)]
            dom = [[jnp.where(m, do2[qrows[u]], zero) for m in hmask] for u in range(per_block)]
            rs_i = [[rs_ref[h, pl.ds(pl.multiple_of(i * tb + u * tt, tt), tt), :] for h in range(2)]
                    for u in range(per_block)]
            dqi_ref[...] = jnp.zeros_like(dqi_ref)
            pc_ref[...] = jnp.zeros_like(pc_ref)
            first = per_block * i

            def sweep(tiles, chains):
                heads, nch = range(2), range(len(chains))
                krows = [pl.ds(pl.multiple_of(t * tt, tt), tt) for t in tiles]
                k2 = [k_ref[rows, :] for rows in krows]
                v2 = [v_ref[rows, :] for rows in krows]
                z = [[_dot_nt(qm[u][h], k2[ti]) for h in heads] for u, ti, _ in chains]
                dw = [[_dot_nt(dom[u][h], v2[ti]) for h in heads] for u, ti, _ in chains]
                ls, lk = [], []
                for n in nch:
                    pairs = [_log_sigmoids(z[n][h]) for h in heads]
                    ls.append([p[0] for p in pairs])
                    lk.append([jnp.where(below, p[1], 0.0) if chains[n][2] else p[1] for p in pairs])
                suffix = [[_dot(jnp.concatenate(_split_hi_lo(lk[n][h]), axis=1), tri2) for h in heads] for n in nch]
                w, g = [], []
                for n, (u, ti, diag) in enumerate(chains):
                    w_n = []
                    for h in heads:
                        logw = ls[n][h] + suffix[n][h]
                        if diag:
                            w_n.append(jnp.where(below, jnp.exp(logw), 0.0))
                        else:
                            carry_in = jnp.sum(jnp.where(lane == tiles[ti], rs_i[u][h], 0.0), axis=1, keepdims=True)
                            w_n.append(jnp.exp(logw + carry_in))
                    w.append([w_h.astype(BF16) for w_h in w_n])
                    g.append([w_n[h] * dw[n][h] for h in heads])
                prefix = [[_dot(g[n][h].astype(BF16), triu) for h in heads] for n in nch]
                cur = {u: [pc_ref[h, qrows[u], :] for h in heads] for u in sorted({c[0] for c in chains})}
                pc = []
                for n, (u, _, _) in enumerate(chains):
                    pc.append(cur[u])
                    cur[u] = [cur[u][h] + jnp.sum(g[n][h], axis=1, keepdims=True) for h in heads]
                dzb = []
                for n in nch:
                    dz_n = []
                    for h in heads:
                        sig = jnp.exp(ls[n][h])
                        dz = g[n][h] - sig * (g[n][h] + prefix[n][h] + pc[n][h])
                        if chains[n][2]:
                            dz = jnp.where(below, dz, 0.0)
                        dz_n.append(dz.astype(BF16))
                    dzb.append(dz_n)
                for u in cur:
                    for h in heads:
                        pc_ref[h, qrows[u], :] = cur[u][h]
                dq = [[_dot(dzb[n][h], jnp.where(hmask[h], k2[chains[n][1]], zero)) for h in heads] for n in nch]
                dk = [[_dot_tn(dzb[n][h], qm[chains[n][0]][h]) for h in heads] for n in nch]
                dv = [[_dot_tn(w[n][h], dom[chains[n][0]][h]) for h in heads] for n in nch]
                add = lambda a, b: a + b
                for u in cur:
                    dqi_ref[qrows[u], :] += functools.reduce(
                        add, [dq[n][h] for n in nch if chains[n][0] == u for h in heads])
                for ti in range(len(tiles)):
                    mine = [n for n in nch if chains[n][1] == ti]
                    dk_acc[krows[ti], :] += functools.reduce(add, [dk[n][h] for n in mine for h in heads])
                    dv_acc[krows[ti], :] += functools.reduce(add, [dv[n][h] for n in mine for h in heads])

            for u in range(per_block):
                n_left = first + u - 1
                n_extra = jnp.clip(extra_ref[pair, first + u].astype(jnp.int32), 0, jnp.maximum(n_left, 0))

                def step(t, c, u=u):
                    sweep([t], [(u, 0, False)])
                    return c

                lax.fori_loop(n_left - n_extra, n_left, step, 0)

            @pl.when(i == 0)
            def _():
                sweep([0, 1], [(0, 0, True), (1, 0, False), (1, 1, True)])

            @pl.when(i > 0)
            def _():
                sweep([first - 1, first, first + 1], [(0, 0, False), (1, 1, False), (0, 1, True), (1, 2, True)])

            dq_acc[block_rows, :] += dqi_ref[...] * scale
            return carry

        lax.fori_loop(0, nq, qblock, 0)
        copies = []
        for t, acc in enumerate((dq_acc, dk_acc, dv_acc)):
            stage_ref[t] = acc[...].astype(BF16)
            col0 = pl.multiple_of(t * D_SB + pair * LANES, LANES)
            copies.append(pltpu.make_async_copy(stage_ref.at[t], out_hbm.at[:, pl.ds(col0, LANES)], out_sems.at[t]))
        for cp in copies:
            cp.start()
        for cp in copies:
            cp.wait()

    def body(q_ref, k_ref, v_ref, do_ref, rs_ref, extra_ref, dproj_hbm, *rest):
        small_in, cp_in = rest[:ns], rest[ns:ns + nc]
        out_hbm = rest[ns + nc]
        small_out, land_out = rest[ns + nc + 1:2 * ns + nc + 1], rest[2 * ns + nc + 1:2 * (ns + nc) + 1]
        scratch = rest[2 * (ns + nc) + 1:]
        gather = _Gather(small_in, small_out, *scratch[7:10])
        pair = pl.program_id(0)

        @pl.when(pair == 0)
        def _():
            gather.issue()
            for cp in _direct_push_copies(cp_in, land_out, *scratch[10:]):
                cp.start()

        sweep_body(q_ref, k_ref, v_ref, do_ref, rs_ref, extra_ref, dproj_hbm, out_hbm, *scratch[:7])

        @pl.when(pair == N_PAIRS - 1)
        def _():
            gather.finish()
            for cp in _direct_push_copies(cp_in, land_out, *scratch[10:]):
                cp.wait()

    any_spec = pl.BlockSpec(memory_space=pl.ANY)
    res = pl.pallas_call(
        body, name=name,
        out_shape=[jax.ShapeDtypeStruct(dproj.shape, BF16)] + _gather_out_shapes(smalls)
        + [jax.ShapeDtypeStruct((N_DEV - 1,) + a.shape[1:], a.dtype) for a in cps],
        grid=(N_PAIRS,),
        in_specs=[pl.BlockSpec((s, LANES), lambda p: (0, p)),
                  pl.BlockSpec((s, LANES), lambda p: (0, N_PAIRS + p)),
                  pl.BlockSpec((s, LANES), lambda p: (0, 2 * N_PAIRS + p)),
                  pl.BlockSpec((s, LANES), lambda p: (0, p)),
                  pl.BlockSpec((2, s, LANES), lambda p: (p, 0, 0)),
                  pl.BlockSpec(memory_space=pltpu.SMEM),
                  any_spec] + [any_spec] * (ns + nc),
        out_specs=[any_spec] * (1 + ns + nc),
        scratch_shapes=[pltpu.VMEM((s, LANES), F32), pltpu.VMEM((s, LANES), F32), pltpu.VMEM((s, LANES), F32),
                        pltpu.VMEM((tb, LANES), F32), pltpu.VMEM((2, tb, LANES), F32),
                        pltpu.VMEM((3, s, LANES), BF16), pltpu.SemaphoreType.DMA((3,))]
        + _gather_semaphores(ns)
        + [pltpu.SemaphoreType.DMA((nc, N_DEV - 1)), pltpu.SemaphoreType.DMA((nc, N_DEV - 1))],
        input_output_aliases={6: 0},
        compiler_params=_cparams(("arbitrary",)),
    )(qkv, qkv, qkv, do, rs, extra, dproj, *smalls, *cps)
    return res[0], res[1:1 + ns], res[1 + ns:]


def _mid(o, rest, x, tgt, wua, wub, wout, fg, lng, lnb, wsp, bfull, gavg, name):
    s, d = x.shape
    ts = 256
    nt = s // ts
    nchunk = ts // SGU_CHUNK
    n_rest = rest.shape[1]

    def body(o_ref, rest_ref, x_ref, t_ref, wua_ref, wub_ref, wout_ref, fg_ref, lng_ref, lnb_ref, wsp_ref, bfull_ref,
             gavg_ref, loss_ref, dx2_ref, do_ref, dproj_ref, dwua_out, dwub_out, dwout_out, dfg_ref, dlng_ref,
             dlnb_ref, dwsp_ref, dbfull_ref, dwua_ref, dwub_ref, dwout_ref):
        step = pl.program_id(0)

        @pl.when(step == 0)
        def _():
            for ref in (loss_ref, dwua_ref, dwub_ref, dwout_ref, dfg_ref, dlng_ref, dlnb_ref, dwsp_ref, dbfull_ref):
                ref[...] = jnp.zeros_like(ref)

        gavg = gavg_ref[...]

        def gmean(a):
            return _dot(a.astype(BF16), gavg)

        def colsum8(a):
            return jnp.sum(a.reshape(ts // 8, 8, a.shape[1]), axis=0)

        z_a = rest_ref[:, 0:512]
        u_b = rest_ref[:, 512:1024]
        v_b = rest_ref[:, 1024:1536]
        z_b = rest_ref[:, 1536:2048]
        g_a = rest_ref[:, 2048:2048 + d]
        g_b = rest_ref[:, 2048 + d:2048 + 2 * d]
        ov = o_ref[...]
        sa = _sigmoid(z_a)
        silu_a = z_a * sa
        y_a = ov * silu_a
        ug, dug_du = _gelu_and_grad(u_b)
        vg, dvg_dv = _gelu_and_grad(v_b)
        mu = gmean(vg)
        cen = vg - mu
        rstd_g = lax.rsqrt(gmean(cen * cen) + EPS)
        vhat = cen * rstd_g
        vn = vhat * lng_ref[...] + lnb_ref[...]
        vnb = vn.astype(BF16)

        t_idx = lax.broadcasted_iota(jnp.int32, (SGU_CHUNK, SGU_CHUNK), 0)
        s_idx = lax.broadcasted_iota(jnp.int32, (SGU_CHUNK, SGU_CHUNK), 1)
        causal = (s_idx // CHUNK) <= (t_idx // CHUNK)
        wm = [jnp.where(causal, wsp_ref[g], 0.0) for g in range(N_GROUPS)]
        wmb = [w.astype(BF16) for w in wm]
        wmtb = [w.T.astype(BF16) for w in wm]
        lane = lax.broadcasted_iota(jnp.int32, (SGU_CHUNK, LANES), 1)
        first = lane < GROUP_DIM
        bfull = bfull_ref[...]

        mixed_rows = []
        for n in range(nchunk):
            r0, r1 = n * SGU_CHUNK, (n + 1) * SGU_CHUNK
            pieces = []
            for p in range(N_GROUPS // 2):
                blk = vnb[r0:r1, p * LANES:(p + 1) * LANES]
                pieces.append(jnp.where(first, _dot(wmb[2 * p], blk), _dot(wmb[2 * p + 1], blk)))
            mixed_rows.append(jnp.concatenate(pieces, axis=1) + bfull)
        mixed = jnp.concatenate(mixed_rows, axis=0)
        sg = ug * mixed
        sb = _sigmoid(z_b)
        silu_b = z_b * sb
        y_b = sg * silu_b
        y_ab = y_a.astype(BF16)
        y_bb = y_b.astype(BF16)
        p_a = _dot(y_ab, wua_ref[...])
        p_b = _dot(y_bb, wub_ref[...])
        ga_s = _sigmoid(g_a)
        gb_s = _sigmoid(g_b)
        merged_b = (ga_s * p_a + gb_s * p_b).astype(BF16)
        x2 = x_ref[...] + _dot(merged_b, wout_ref[...])
        rstd = lax.rsqrt(jnp.mean(x2 * x2, axis=-1, keepdims=True) + EPS)
        xhat = x2 * rstd
        fg_v = fg_ref[...]
        diff = xhat * fg_v - t_ref[...]
        loss_ref[...] += 0.5 * jnp.sum(jnp.sum(diff * diff, axis=-1, keepdims=True) * (1.0 / d))

        dy = diff * (1.0 / d)
        dfg_ref[...] += colsum8(dy * xhat)
        dxh = dy * fg_v
        dx2 = rstd * (dxh - xhat * jnp.mean(dxh * xhat, axis=-1, keepdims=True))
        dx2_ref[...] = dx2
        dx2b = dx2.astype(BF16)
        dwout_ref[...] += _dot_tn(merged_b, dx2b)
        dmerged = _dot_nt(dx2b, wout_ref[...])
        dp_a = dmerged * ga_s
        dp_b = dmerged * gb_s
        dproj_ref[:, QKV_COLS + 2048:QKV_COLS + 2048 + d] = (dmerged * p_a * (ga_s * (1.0 - ga_s))).astype(BF16)
        dproj_ref[:, QKV_COLS + 2048 + d:QKV_COLS + 2048 + 2 * d] = (dmerged * p_b * (gb_s * (1.0 - gb_s))).astype(BF16)
        dp_ab = dp_a.astype(BF16)
        dp_bb = dp_b.astype(BF16)
        dwua_ref[...] += _dot_tn(y_ab, dp_ab)
        dwub_ref[...] += _dot_tn(y_bb, dp_bb)
        dy_a = _dot_nt(dp_ab, wua_ref[...])
        dy_b = _dot_nt(dp_bb, wub_ref[...])
        do_ref[...] = (dy_a * silu_a).astype(BF16)
        dproj_ref[:, QKV_COLS:QKV_COLS + 512] = (dy_a * ov * (sa * (1.0 + z_a * (1.0 - sa)))).astype(BF16)
        dsg = dy_b * silu_b
        dproj_ref[:, QKV_COLS + 1536:QKV_COLS + 2048] = (dy_b * sg * (sb * (1.0 + z_b * (1.0 - sb)))).astype(BF16)
        dproj_ref[:, QKV_COLS + 512:QKV_COLS + 1024] = (dsg * mixed * dug_du).astype(BF16)
        dmixed = dsg * ug
        dmb = dmixed.astype(BF16)
        zero = jnp.zeros((), BF16)
        dvn_rows = []
        db = jnp.zeros((SGU_CHUNK, D_SGU), F32)
        for n in range(nchunk):
            r0, r1 = n * SGU_CHUNK, (n + 1) * SGU_CHUNK
            db = db + dmixed[r0:r1, :]
            pieces = []
            for p in range(N_GROUPS // 2):
                cols = slice(p * LANES, (p + 1) * LANES)
                dm_blk = dmb[r0:r1, cols]
                vn_blk = vnb[r0:r1, cols]
                dwsp_ref[2 * p] += _dot_nt(jnp.where(first, dm_blk, zero), vn_blk)
                dwsp_ref[2 * p + 1] += _dot_nt(jnp.where(first, zero, dm_blk), vn_blk)
                pieces.append(jnp.where(first, _dot(wmtb[2 * p], dm_blk), _dot(wmtb[2 * p + 1], dm_blk)))
            dvn_rows.append(jnp.concatenate(pieces, axis=1))
        dbfull_ref[...] += db
        dvn = jnp.concatenate(dvn_rows, axis=0)
        dlng_ref[...] += colsum8(dvn * vhat)
        dlnb_ref[...] += colsum8(dvn)
        dvhat = dvn * lng_ref[...]
        dcen = rstd_g * (dvhat - gmean(dvhat) - vhat * gmean(dvhat * vhat))
        dproj_ref[:, QKV_COLS + 1024:QKV_COLS + 1536] = (dcen * dvg_dv).astype(BF16)

        @pl.when(step == nt - 1)
        def _():
            for g in range(N_GROUPS):
                dwsp_ref[g] = jnp.where(causal, dwsp_ref[g], 0.0)
            cb = d // N_DEV
            for k in range(N_DEV):
                dwua_out[k] = dwua_ref[:, k * cb:(k + 1) * cb].astype(BF16)
                dwub_out[k] = dwub_ref[:, k * cb:(k + 1) * cb].astype(BF16)
            dwout_out[...] = dwout_ref[...].astype(BF16)

    def tile(cols):
        return pl.BlockSpec((ts, cols), lambda i: (i, 0))

    def whole(shape):
        return pl.BlockSpec(shape, lambda i: (0,) * len(shape))

    out_shapes = [
        jax.ShapeDtypeStruct((8, LANES), F32),
        jax.ShapeDtypeStruct((s, d), F32),
        jax.ShapeDtypeStruct((s, D_SB), BF16),
        jax.ShapeDtypeStruct((s, QKV_COLS + n_rest), BF16),
        jax.ShapeDtypeStruct((N_DEV, D_SB, d // N_DEV), BF16),
        jax.ShapeDtypeStruct((N_DEV, D_SGU, d // N_DEV), BF16),
        jax.ShapeDtypeStruct((d, d), BF16),
        jax.ShapeDtypeStruct((8, d), F32),
        jax.ShapeDtypeStruct((8, D_SGU), F32),
        jax.ShapeDtypeStruct((8, D_SGU), F32),
        jax.ShapeDtypeStruct((N_GROUPS, SGU_CHUNK, SGU_CHUNK), F32),
        jax.ShapeDtypeStruct((SGU_CHUNK, D_SGU), F32),
    ]
    out_specs = [whole((8, LANES)), tile(d), tile(D_SB), tile(QKV_COLS + n_rest), whole((N_DEV, D_SB, d // N_DEV)),
                 whole((N_DEV, D_SGU, d // N_DEV)), whole((d, d)), whole((8, d)), whole((8, D_SGU)), whole((8, D_SGU)),
                 whole((N_GROUPS, SGU_CHUNK, SGU_CHUNK)), whole((SGU_CHUNK, D_SGU))]
    in_specs = [tile(D_SB), tile(n_rest), tile(d), tile(d), whole((D_SB, d)), whole((D_SGU, d)), whole((d, d)),
                whole((1, d)), whole((1, D_SGU)), whole((1, D_SGU)), whole((N_GROUPS, SGU_CHUNK, SGU_CHUNK)),
                whole((SGU_CHUNK, D_SGU)), whole((D_SGU, D_SGU))]
    return pl.pallas_call(
        body, name=name, out_shape=out_shapes, grid=(nt,), in_specs=in_specs, out_specs=out_specs,
        scratch_shapes=[pltpu.VMEM((D_SB, d), F32), pltpu.VMEM((D_SGU, d), F32), pltpu.VMEM((d, d), F32)],
        compiler_params=_cparams(("arbitrary",)),
    )(o, rest, x, tgt, wua, wub, wout, fg, lng, lnb, wsp, bfull, gavg)


def _small_reduce(dfg8, dlng8, dlnb8, dbfull, name):
    d = dfg8.shape[1]

    def body(dfg_ref, dlng_ref, dlnb_ref, dbfull_ref, fg_out, lng_out, lnb_out, b_out):
        row = lax.broadcasted_iota(jnp.int32, (D_SGU, LANES), 0)
        col = lax.broadcasted_iota(jnp.int32, (D_SGU, LANES), 1)

        def select_sum(a, sel):
            hi, lo = _split_hi_lo(a)
            lo2 = (a - hi.astype(F32) - lo.astype(F32)).astype(BF16)
            return _dot(hi, sel) + _dot(lo, sel) + _dot(lo2, sel)

        def by_group(partials):
            v = jnp.sum(partials, axis=0, keepdims=True)
            g_of_lane = lax.broadcasted_iota(jnp.int32, (N_GROUPS, D_SGU), 1) // GROUP_DIM
            g_of_row = lax.broadcasted_iota(jnp.int32, (N_GROUPS, D_SGU), 0)
            spread = jnp.where(g_of_lane == g_of_row, jnp.broadcast_to(v, (N_GROUPS, D_SGU)), 0.0)
            return select_sum(spread, (row % GROUP_DIM == col).astype(BF16))[:, 0:GROUP_DIM]

        fg_out[...] = jnp.sum(dfg_ref[...], axis=0, keepdims=True)
        lng_out[...] = by_group(dlng_ref[...])
        lnb_out[...] = by_group(dlnb_ref[...])
        by_pos = select_sum(dbfull_ref[...], (row // GROUP_DIM == col).astype(BF16))
        b_out[...] = by_pos.T[0:N_GROUPS, :]

    return pl.pallas_call(
        body, name=name,
        out_shape=[jax.ShapeDtypeStruct((1, d), F32), jax.ShapeDtypeStruct((N_GROUPS, GROUP_DIM), F32),
                   jax.ShapeDtypeStruct((N_GROUPS, GROUP_DIM), F32), jax.ShapeDtypeStruct((N_GROUPS, SGU_CHUNK), F32)],
        compiler_params=_cparams(),
    )(dfg8, dlng8, dlnb8, dbfull)


def _from_block_major_cols(w):
    nb, r, c = w.shape
    return jnp.transpose(w, (1, 0, 2)).reshape(r, nb * c)


def kernel(x, norm_g, w_in, sgu_ln_g, sgu_ln_b, w_spatial, b_spatial, w_up_a, w_up_b, w_out, final_norm_g, loss_target, m_norm_g, m_w_in, m_sgu_ln_g, m_sgu_ln_b, m_w_spatial, m_b_spatial, m_w_up_a, m_w_up_b, m_w_out, m_final_norm_g, v_norm_g, v_w_in, v_sgu_ln_g, v_sgu_ln_b, v_w_spatial, v_b_spatial, v_w_up_a, v_w_up_b, v_w_out, v_final_norm_g):
    s, d = x.shape[1], x.shape[2]
    xs = x[0]
    tgt = loss_target[0]
    cx, cy, cc = _coords()
    core = jnp.reshape(cc, (1,)).astype(jnp.int32)
    chip = jnp.reshape(2 * cx + cy, (1,)).astype(jnp.int32)

    h, (g_win,) = _rmsnorm_fwd(xs, norm_g, [jnp.transpose(w_in[0])], "norm")
    d_in = N_DEV * w_in.shape[2]
    w_full = g_win.reshape(d_in, d)
    qkv, rest = _inproj(h, w_full, "inproj")
    o, rs, extra, (g_wua, g_wub, g_wout) = _attn_fwd(
        qkv, [w_up_a[0].astype(BF16), w_up_b[0].astype(BF16), w_out[0].astype(BF16)], "attn_fwd")
    wua_full = _from_block_major_cols(g_wua)
    wub_full = _from_block_major_cols(g_wub)
    wout_full = g_wout.reshape(d, d)

    lng = sgu_ln_g.reshape(1, D_SGU)
    lnb = sgu_ln_b.reshape(1, D_SGU)
    bfull = jnp.repeat(jnp.transpose(b_spatial[0]), GROUP_DIM, axis=1)
    grp = jnp.arange(D_SGU) // GROUP_DIM
    gavg = jnp.where(grp[:, None] == grp[None, :], 1.0 / GROUP_DIM, 0.0).astype(BF16)
    (loss_b, dx2, do, dproj, dwua, dwub, dwout, dfg8, dlng8, dlnb8, dwsp, dbfull) = _mid(
        o, rest, xs, tgt, wua_full, wub_full, wout_full, final_norm_g.reshape(1, d), lng, lnb, w_spatial[0], bfull,
        gavg, "mid")

    own_up = [dwua, dwub, dwout.reshape(N_DEV, d // N_DEV, d)]
    dfg, dlng, dlnb, db = _small_reduce(dfg8, dlng8, dlnb8, dbfull, "small_reduce")
    wsp_rows = N_GROUPS * SGU_CHUNK
    dproj, small_parts, land_up = _attn_bwd(
        qkv, do, rs, extra, dproj, [dfg, dlng, dlnb, db, dwsp.reshape(wsp_rows, SGU_CHUNK)], own_up, "attn_bwd")
    dwin = _dw_in(h, dproj, "dwin")

    own_in = [dwin.reshape(4, 2, d_in // N_DEV, d)]
    cps_in = list(_chip_partial_sums(own_in, list(_push_sibling(own_in, "rs_sibling_in")), core, "cpsum_in"))
    grad_x, dng8, land_in = _dh_and_grad_x(dproj, w_full, xs, norm_g, dx2, cps_in, "dh")

    ng_parts, loss_parts = _allgather([dng8, loss_b], "ag_tail")

    def small_layouts(ng, fg, lg, lb, bs, ws):
        return [ng.reshape(1, d), fg.reshape(1, d), lg.reshape(N_GROUPS, GROUP_DIM), lb.reshape(N_GROUPS, GROUP_DIM),
                bs.reshape(N_GROUPS, SGU_CHUNK), ws.reshape(wsp_rows, SGU_CHUNK)]

    sm, loss_sum = _adam_small(
        [ng_parts] + list(small_parts),
        small_layouts(norm_g, final_norm_g, sgu_ln_g, sgu_ln_b, b_spatial, w_spatial),
        small_layouts(m_norm_g, m_final_norm_g, m_sgu_ln_g, m_sgu_ln_b, m_b_spatial, m_w_spatial),
        small_layouts(v_norm_g, v_final_norm_g, v_sgu_ln_g, v_sgu_ln_b, v_b_spatial, v_w_spatial),
        loss_parts, "adam_small")
    small_shapes = [norm_g.shape, final_norm_g.shape, sgu_ln_g.shape, sgu_ln_b.shape, b_spatial.shape, w_spatial.shape]
    sm = [[a.reshape(shp) for a, shp in zip(kind, small_shapes)] for kind in sm]

    res = _adam_shard(cps_in, list(land_in), chip, *[jnp.transpose(a[0]) for a in (w_in, m_w_in, v_w_in)], "adam0")
    big = [[jnp.transpose(r)[None] for r in res]]
    dev = jnp.reshape(_dev_index(cx, cy, cc), (1,)).astype(jnp.int32)
    res = _adam_shards_whole(own_up, list(land_up), dev, [w_up_a[0], w_up_b[0], w_out[0]],
                             [m_w_up_a[0], m_w_up_b[0], m_w_out[0]], [v_w_up_a[0], v_w_up_b[0], v_w_out[0]], "adam_up")
    for i in range(3):
        big.append([res[kd][i][None] for kd in range(4)])

    loss = loss_sum[0, 0]

    def per_kind(kd):
        return [sm[kd][0], big[0][kd], sm[kd][2], sm[kd][3], sm[kd][5], sm[kd][4], big[1][kd], big[2][kd], big[3][kd],
                sm[kd][1]]

    return (loss, grad_x[None], *per_kind(0), *per_kind(1), *per_kind(2), *per_kind(3))
```

```python
import functools
import math

import jax
import jax.numpy as jnp
from jax import lax
from jax.experimental import pallas as pl
from jax.experimental.pallas import tpu as pltpu

F32 = jnp.float32
BF16 = jnp.bfloat16
MESH = pl.DeviceIdType.MESH

N_DEV = 8
N_HEADS = 8
HEAD_DIM = 64
D_SB = N_HEADS * HEAD_DIM
N_GROUPS = 8
GROUP_DIM = 64
D_SGU = N_GROUPS * GROUP_DIM
SGU_CHUNK = 128
CHUNK = 64
EPS = 1e-6
LANES = 128
N_PAIRS = N_HEADS // 2
QKV_COLS = 3 * D_SB
ATT_BLOCK = 256
ATT_TILE = 128
CARRY_FLOOR = -90.0
R_UNREACHED = -1e30

ADAM_LR = 0.001
ADAM_B1 = 0.9
ADAM_B2 = 0.999
ADAM_EPS = 1e-08
ADAM_WD = 0.01
ADAM_STEP = 10

VMEM_LIMIT = 56 * 1024 * 1024


def _cparams(sem=None, vmem=VMEM_LIMIT):
    return pltpu.CompilerParams(dimension_semantics=sem, vmem_limit_bytes=vmem)


def _dot(a, b):
    return jnp.dot(a, b, preferred_element_type=F32)


def _dot_nt(a, b):
    return lax.dot_general(a, b, (((1,), (1,)), ((), ())), preferred_element_type=F32)


def _dot_tn(a, b):
    return lax.dot_general(a, b, (((0,), (0,)), ((), ())), preferred_element_type=F32)


def _split_hi_lo(a):
    hi = a.astype(BF16)
    lo = (a - hi.astype(F32)).astype(BF16)
    return hi, lo


def _sigmoid(x):
    return 1.0 / (1.0 + jnp.exp(-x))


_GELU_C = math.sqrt(2.0 / math.pi)


def _gelu_and_grad(x):
    x2 = x * x
    inner = _GELU_C * (x + 0.044715 * (x2 * x))
    t = jnp.tanh(inner)
    cdf = 0.5 * (1.0 + t)
    g = x * cdf
    dg = cdf + x * (0.5 * (1.0 - t * t)) * (_GELU_C * (1.0 + 3.0 * 0.044715 * x2))
    return g, dg


def _coords():
    return lax.axis_index("x"), lax.axis_index("y"), lax.axis_index("c")


def _dev_index(px, py, pc):
    return 4 * px + 2 * py + pc


def _allgather(blocks, name):
    n = len(blocks)

    def body(*refs):
        gather = _Gather(refs[:n], refs[n:2 * n], *refs[2 * n:])
        gather.issue()
        gather.finish()

    any_spec = pl.BlockSpec(memory_space=pl.ANY)
    return pl.pallas_call(
        body, name=name,
        out_shape=_gather_out_shapes(blocks),
        in_specs=[any_spec] * n, out_specs=[any_spec] * n,
        scratch_shapes=_gather_semaphores(n),
    )(*blocks)


def _gather_out_shapes(blocks):
    return [jax.ShapeDtypeStruct((N_DEV,) + b.shape, b.dtype) for b in blocks]


def _gather_semaphores(n):
    return [pltpu.SemaphoreType.DMA((n, 7)), pltpu.SemaphoreType.DMA((n, 7)), pltpu.SemaphoreType.DMA((n,))]


class _Gather:
    def __init__(self, ins, outs, send_sems, recv_sems, local_sems, relay=False):
        self.ins, self.outs = ins, outs
        self.send_sems, self.recv_sems, self.local_sems = send_sems, recv_sems, local_sems
        self.n = len(ins)
        self.relay = relay
        x, y, c = _coords()
        self.c = c
        self.me, self.sibling = (x, y, c), (x, y, 1 - c)
        self.chips = [(1 - x, y), (x, 1 - y), (1 - x, 1 - y)]

    def _copy(self, a, k, block, to, src=None):
        dst = self.outs[a].at[_dev_index(*block)]
        return pltpu.make_async_remote_copy(
            src_ref=dst if src is None else src, dst_ref=dst,
            send_sem=self.send_sems.at[a, k], recv_sem=self.recv_sems.at[a, k],
            device_id=to, device_id_type=MESH)

    def _mine(self):
        return [pltpu.make_async_copy(self.ins[a], self.outs[a].at[_dev_index(*self.me)], self.local_sems.at[a])
                for a in range(self.n)]

    def _first(self):
        first = []
        direct = self.chips[:2] if self.relay else self.chips
        for a in range(self.n):
            first.append(self._copy(a, 0, self.me, self.sibling, src=self.ins[a]))
            first += [self._copy(a, 1 + j, self.me, (*chip, self.c), src=self.ins[a])
                      for j, chip in enumerate(direct)]
        return first

    def issue(self):
        for cp in self._mine() + self._first():
            cp.start()

    def _pass_on(self, a, j):
        chip = self.chips[j]
        self._copy(a, 1 + j, (*chip, self.c), self.me).wait_recv()
        self._copy(a, 4 + j, (*chip, self.c), self.sibling).start()

    def finish(self):
        c = self.c
        if self.relay:
            for core in range(2):
                @pl.when(c == core)
                def _(core=core):
                    j_src, j_dst = core, 1 - core
                    for a in range(self.n):
                        self._pass_on(a, j_src)
                        self._copy(a, 3, (*self.chips[j_src], c), (*self.chips[j_dst], c)).start()
                    for a in range(self.n):
                        self._pass_on(a, j_dst)
                    for a in range(self.n):
                        self._pass_on(a, 2)
        else:
            for j in range(3):
                for a in range(self.n):
                    self._pass_on(a, j)
        for a in range(self.n):
            self._copy(a, 0, self.sibling, self.me).wait_recv()
            for j, chip in enumerate(self.chips):
                self._copy(a, 4 + j, (*chip, 1 - c), self.me).wait_recv()
        for a in range(self.n):
            for k in range(7):
                self._copy(a, k, self.me, self.sibling).wait_send()
        for cp in self._mine():
            cp.wait()


def _push_sibling(arrs, name):
    n = len(arrs)

    def body(*refs):
        ins, outs = refs[:n], refs[n:2 * n]
        send_sems, recv_sems = refs[2 * n:]
        x, y, c = _coords()
        sibling = (x, y, 1 - c)
        copies = []
        for a in range(n):
            for k in range(4):
                copies.append(pltpu.make_async_remote_copy(
                    src_ref=ins[a].at[k, 1 - c], dst_ref=outs[a].at[k],
                    send_sem=send_sems.at[a, k], recv_sem=recv_sems.at[a, k],
                    device_id=sibling, device_id_type=MESH))
        for cp in copies:
            cp.start()
        for cp in copies:
            cp.wait()

    any_spec = pl.BlockSpec(memory_space=pl.ANY)
    return pl.pallas_call(
        body, name=name,
        out_shape=[jax.ShapeDtypeStruct((4,) + a.shape[2:], a.dtype) for a in arrs],
        in_specs=[any_spec] * n, out_specs=[any_spec] * n,
        scratch_shapes=[pltpu.SemaphoreType.DMA((n, 4)), pltpu.SemaphoreType.DMA((n, 4))],
    )(*arrs)


def _chip_push_copies(ins, outs, send_sems, recv_sems):
    x, y, c = _coords()
    chips = [(1 - x, y), (x, 1 - y), (1 - x, 1 - y)]
    return [pltpu.make_async_remote_copy(
        src_ref=ins[a].at[2 * px + py], dst_ref=outs[a].at[r],
        send_sem=send_sems.at[a, r], recv_sem=recv_sems.at[a, r],
        device_id=(px, py, c), device_id_type=MESH)
        for a in range(len(ins)) for r, (px, py) in enumerate(chips)]


def _direct_push_copies(ins, outs, send_sems, recv_sems):
    x, y, c = _coords()
    copies = []
    for a in range(len(ins)):
        for r in range(1, N_DEV):
            px = 1 - x if r & 4 else x
            py = 1 - y if r & 2 else y
            pc = 1 - c if r & 1 else c
            copies.append(pltpu.make_async_remote_copy(
                src_ref=ins[a].at[_dev_index(px, py, pc)], dst_ref=outs[a].at[r - 1],
                send_sem=send_sems.at[a, r - 1], recv_sem=recv_sems.at[a, r - 1],
                device_id=(px, py, pc), device_id_type=MESH))
    return copies


def _chip_partial_sums(owns, lands, core, name):
    n = len(owns)
    blocks = [a.shape[2:] for a in owns]

    def body(core_ref, *refs):
        del core_ref
        for own_ref, land_ref, out_ref in zip(refs[:n], refs[n:2 * n], refs[2 * n:]):
            out_ref[...] = (own_ref[...].astype(F32) + land_ref[...].astype(F32)).astype(out_ref.dtype)

    return pl.pallas_call(
        body, name=name,
        out_shape=[jax.ShapeDtypeStruct((4,) + b, a.dtype) for a, b in zip(owns, blocks)],
        grid_spec=pltpu.PrefetchScalarGridSpec(
            num_scalar_prefetch=1, grid=(4,),
            in_specs=[pl.BlockSpec((None, None) + b, lambda k, core: (k, core[0], 0, 0)) for b in blocks]
            + [pl.BlockSpec((None,) + b, lambda k, core: (k, 0, 0)) for b in blocks],
            out_specs=[pl.BlockSpec((None,) + b, lambda k, core: (k, 0, 0)) for b in blocks]),
        compiler_params=_cparams(("parallel",)),
    )(core, *owns, *lands)


def _adamw_math(w, g, m, v):
    m = ADAM_B1 * m + (1.0 - ADAM_B1) * g
    v = ADAM_B2 * v + (1.0 - ADAM_B2) * (g * g)
    m_hat = m / (1.0 - ADAM_B1 ** ADAM_STEP)
    v_hat = v / (1.0 - ADAM_B2 ** ADAM_STEP)
    delta = -ADAM_LR * (m_hat / (jnp.sqrt(v_hat) + ADAM_EPS) + ADAM_WD * w)
    return delta, m, v


def _adam_shard(cps, lands, chip, w, m, v, name):
    rows, cols = w.shape
    tr = rows // 2
    nparts = len(cps)

    def body(chip_ref, *refs):
        del chip_ref
        cp_refs, land_refs = refs[:nparts], refs[nparts:2 * nparts]
        w_ref, m_ref, v_ref, g_out, d_out, m_out, v_out = refs[2 * nparts:]
        parts = []
        for cp_ref, land_ref in zip(cp_refs, land_refs):
            g_k = cp_ref[...].astype(F32)
            for r in range(3):
                g_k = g_k + land_ref[r].astype(F32)
            parts.append(g_k)
        g = parts[0] if nparts == 1 else jnp.concatenate(parts, axis=1)
        delta, m_new, v_new = _adamw_math(w_ref[...], g, m_ref[...], v_ref[...])
        g_out[...] = g
        d_out[...] = delta
        m_out[...] = m_new
        v_out[...] = v_new

    tile = pl.BlockSpec((tr, cols), lambda r, chip: (r, 0))
    out = jax.ShapeDtypeStruct((rows, cols), F32)
    return pl.pallas_call(
        body, name=name, out_shape=[out] * 4,
        grid_spec=pltpu.PrefetchScalarGridSpec(
            num_scalar_prefetch=1, grid=(rows // tr,),
            in_specs=[pl.BlockSpec((None, tr, a.shape[2]), lambda r, chip: (chip[0], r, 0)) for a in cps]
            + [pl.BlockSpec((3, tr, a.shape[2]), lambda r, chip: (0, r, 0)) for a in lands]
            + [tile, tile, tile],
            out_specs=[tile] * 4),
        compiler_params=_cparams(("parallel",)),
    )(chip, *cps, *lands, w, m, v)


def _adam_shards_whole(owns, lands, dev, ws, ms, vs, name):
    n = len(ws)

    def body(dev_ref, *refs):
        own_refs, land_refs = refs[:n], refs[n:2 * n]
        w_refs, m_refs, v_refs = refs[2 * n:3 * n], refs[3 * n:4 * n], refs[4 * n:5 * n]
        outs = refs[5 * n:]
        for i in range(n):
            g = own_refs[i][dev_ref[0]].astype(F32)
            for r in range(N_DEV - 1):
                g = g + land_refs[i][r].astype(F32)
            delta, m_new, v_new = _adamw_math(w_refs[i][...], g, m_refs[i][...], v_refs[i][...])
            outs[i][...] = g
            outs[n + i][...] = delta
            outs[2 * n + i][...] = m_new
            outs[3 * n + i][...] = v_new

    vmem = pl.BlockSpec(memory_space=pltpu.VMEM)
    res = pl.pallas_call(
        body, name=name, out_shape=[jax.ShapeDtypeStruct(w.shape, F32) for w in ws] * 4,
        in_specs=[pl.BlockSpec(memory_space=pltpu.SMEM)] + [vmem] * (5 * n), out_specs=[vmem] * (4 * n),
        compiler_params=_cparams(),
    )(dev, *owns, *lands, *ws, *ms, *vs)
    return [res[k * n:(k + 1) * n] for k in range(4)]


def _adam_small(parts, ws, ms, vs, loss_parts, name):
    n = len(ws)

    def body(*refs):
        p_refs, w_refs, m_refs, v_refs = refs[:n], refs[n:2 * n], refs[2 * n:3 * n], refs[3 * n:4 * n]
        loss_ref, outs, loss_out = refs[4 * n], refs[4 * n + 1:-1], refs[-1]
        total = loss_ref[0]
        for dev in range(1, N_DEV):
            total = total + loss_ref[dev]
        loss_out[...] = total
        for i in range(n):
            g = p_refs[i][0]
            for dev in range(1, N_DEV):
                g = g + p_refs[i][dev]
            if g.shape[0] != w_refs[i].shape[0]:
                g = jnp.sum(g, axis=0, keepdims=True)
            delta, m_new, v_new = _adamw_math(w_refs[i][...], g, m_refs[i][...], v_refs[i][...])
            outs[i][...] = g
            outs[n + i][...] = delta
            outs[2 * n + i][...] = m_new
            outs[3 * n + i][...] = v_new

    out_shapes = [jax.ShapeDtypeStruct(w.shape, F32) for w in ws] * 4
    out_shapes.append(jax.ShapeDtypeStruct(loss_parts.shape[1:], F32))
    res = pl.pallas_call(body, name=name, out_shape=out_shapes, compiler_params=_cparams())(
        *parts, *ws, *ms, *vs, loss_parts)
    return [res[k * n:(k + 1) * n] for k in range(4)], res[-1]


def _rmsnorm_fwd(x, g, riders, name):
    s, d = x.shape
    ts = 512
    nt = s // ts
    nr = len(riders)

    def body(x_ref, g_ref, *rest):
        rider_in, h_ref = rest[:nr], rest[nr]
        rider_out, staged = rest[nr + 1:2 * nr + 1], rest[2 * nr + 1:3 * nr + 1]
        sems = rest[3 * nr + 1:]
        i = pl.program_id(0)
        gather = _Gather(staged, rider_out, *sems, relay=True)

        @pl.when(i == 0)
        def _():
            for src, dst in zip(rider_in, staged):
                dst[...] = src[...].astype(BF16)
            gather.issue()

        xv = x_ref[...]
        rstd = lax.rsqrt(jnp.mean(xv * xv, axis=-1, keepdims=True) + EPS)
        h = xv * rstd * g_ref[...]
        h_ref[...] = h.astype(BF16)

        @pl.when(i == nt - 1)
        def _():
            gather.finish()

    any_spec = pl.BlockSpec(memory_space=pl.ANY)
    res = pl.pallas_call(
        body, name=name,
        out_shape=[jax.ShapeDtypeStruct((s, d), BF16)]
        + [jax.ShapeDtypeStruct((N_DEV,) + r.shape, BF16) for r in riders],
        grid=(nt,),
        in_specs=[pl.BlockSpec((ts, d), lambda i: (i, 0)), pl.BlockSpec((1, d), lambda i: (0, 0))]
        + [pl.BlockSpec(r.shape, lambda i: (0, 0)) for r in riders],
        out_specs=[pl.BlockSpec((ts, d), lambda i: (i, 0))] + [any_spec] * nr,
        scratch_shapes=[pltpu.VMEM(r.shape, BF16) for r in riders] + _gather_semaphores(nr),
        compiler_params=_cparams(("arbitrary",)),
    )(x, g, *riders)
    return res[0], res[1:]


def _inproj(h, wt, name):
    s, d = h.shape
    n = wt.shape[0]
    tn = 256
    n_qkv = QKV_COLS // tn

    def body(h_ref, w_ref, qkv_ref, rest_ref):
        j = pl.program_id(0)
        res = _dot_nt(h_ref[...], w_ref[...])

        @pl.when(j < n_qkv)
        def _():
            qkv_ref[...] = res.astype(BF16)

        @pl.when(j >= n_qkv)
        def _():
            rest_ref[...] = res

    return pl.pallas_call(
        body, name=name,
        out_shape=[jax.ShapeDtypeStruct((s, QKV_COLS), BF16), jax.ShapeDtypeStruct((s, n - QKV_COLS), F32)],
        grid=(n // tn,),
        in_specs=[pl.BlockSpec((s, d), lambda j: (0, 0)), pl.BlockSpec((tn, d), lambda j: (j, 0))],
        out_specs=[pl.BlockSpec((s, tn), lambda j: (0, jnp.minimum(j, n_qkv - 1))),
                   pl.BlockSpec((s, tn), lambda j: (0, jnp.maximum(j - n_qkv, 0)))],
        compiler_params=_cparams(("arbitrary",)),
    )(h, wt)


def _dw_in(h, dproj, name):
    s, d = h.shape
    n = dproj.shape[1]
    tn = 512

    def body(h_ref, b_ref, o_ref):
        o_ref[...] = _dot_tn(b_ref[...], h_ref[...]).astype(o_ref.dtype)

    return pl.pallas_call(
        body, name=name, out_shape=jax.ShapeDtypeStruct((n, d), BF16), grid=(n // tn,),
        in_specs=[pl.BlockSpec((s, d), lambda j: (0, 0)), pl.BlockSpec((s, tn), lambda j: (0, j))],
        out_specs=pl.BlockSpec((tn, d), lambda j: (j, 0)),
        compiler_params=_cparams(("parallel",)),
    )(h, dproj)


def _dh_and_grad_x(dproj, wt, x, g, dx2, cps, name):
    s, n = dproj.shape
    d = wt.shape[1]
    tm = min(s, 1024)
    tk = next(t for t in (1408, 512) if n % t == 0)
    nk = n // tk
    nm = s // tm
    nc = len(cps)

    def body(a_ref, w_ref, x_ref, g_ref, dx2_ref, *rest):
        cp_refs, gx_ref, dg_ref = rest[:nc], rest[nc], rest[nc + 1]
        land_refs = rest[nc + 2:2 * nc + 2]
        acc_ref, send_sems, recv_sems = rest[2 * nc + 2:]
        i, k = pl.program_id(0), pl.program_id(1)

        @pl.when((i == 0) & (k == 0))
        def _():
            for cp in _chip_push_copies(cp_refs, land_refs, send_sems, recv_sems):
                cp.start()
            dg_ref[...] = jnp.zeros_like(dg_ref)

        @pl.when(k == 0)
        def _():
            acc_ref[...] = jnp.zeros_like(acc_ref)

        acc_ref[...] += _dot(a_ref[...], w_ref[...])

        @pl.when(k == nk - 1)
        def _():
            dh = acc_ref[...]
            xv = x_ref[...]
            rstd = lax.rsqrt(jnp.mean(xv * xv, axis=-1, keepdims=True) + EPS)
            xhat = xv * rstd
            dg_ref[...] += jnp.sum((dh * xhat).reshape(tm // 8, 8, d), axis=0)
            dxh = dh * g_ref[...]
            gx_ref[...] = dx2_ref[...] + rstd * (dxh - xhat * jnp.mean(dxh * xhat, axis=-1, keepdims=True))

        @pl.when((i == nm - 1) & (k == nk - 1))
        def _():
            for cp in _chip_push_copies(cp_refs, land_refs, send_sems, recv_sems):
                cp.wait()

    any_spec = pl.BlockSpec(memory_space=pl.ANY)
    res = pl.pallas_call(
        body, name=name,
        out_shape=[jax.ShapeDtypeStruct((s, d), F32), jax.ShapeDtypeStruct((8, d), F32)]
        + [jax.ShapeDtypeStruct((3,) + a.shape[1:], a.dtype) for a in cps],
        grid=(nm, nk),
        in_specs=[pl.BlockSpec((tm, tk), lambda i, k: (i, k)), pl.BlockSpec((tk, d), lambda i, k: (k, 0)),
                  pl.BlockSpec((tm, d), lambda i, k: (i, 0)), pl.BlockSpec((1, d), lambda i, k: (0, 0)),
                  pl.BlockSpec((tm, d), lambda i, k: (i, 0))] + [any_spec] * nc,
        out_specs=[pl.BlockSpec((tm, d), lambda i, k: (i, 0)), pl.BlockSpec((8, d), lambda i, k: (0, 0))]
        + [any_spec] * nc,
        scratch_shapes=[pltpu.VMEM((tm, d), F32), pltpu.SemaphoreType.DMA((nc, 3)), pltpu.SemaphoreType.DMA((nc, 3))],
        compiler_params=_cparams(("arbitrary", "arbitrary")),
    )(dproj, wt, x, g, dx2, *cps)
    return res[0], res[1], res[2:]


def _log_sigmoids(z):
    l1p = jnp.log(1.0 + jnp.exp(-jnp.abs(z)))
    ls = jnp.minimum(z, 0.0) - l1p
    return ls, ls - z


def _strict_lower_ones(n):
    row = lax.broadcasted_iota(jnp.int32, (n, n), 0)
    col = lax.broadcasted_iota(jnp.int32, (n, n), 1)
    return row, col, (row > col).astype(BF16)


def _attn_fwd(qkv, riders, name):
    s = qkv.shape[0]
    tb, tt = ATT_BLOCK, ATT_TILE
    nq = s // tb
    per_block = tb // tt
    assert per_block == 2 and s // tt <= LANES, "two query tiles per grid step; one lane of saved carry per key tile"

    nr = len(riders)

    def sweep_body(gather, q_ref, k_ref, v_ref, o_ref, rs_ref, extra_ref, acc_ref, r_ref, rsv_ref, rmax_ref):
        i = pl.program_id(1)
        lane = lax.broadcasted_iota(jnp.int32, (tt, LANES), 1)
        hmask = [lane < HEAD_DIM, lane >= HEAD_DIM]
        row, col, tri = _strict_lower_ones(tt)
        tri2 = jnp.concatenate([tri, tri], axis=0)
        below = col < row
        qrows = [slice(u * tt, (u + 1) * tt) for u in range(per_block)]
        qm = [[jnp.where(m, q_ref[qrows[u], :], jnp.zeros((), BF16)) * jnp.asarray(HEAD_DIM ** -0.5, BF16)
               for m in hmask] for u in range(per_block)]
        acc_ref[...] = jnp.zeros_like(acc_ref)
        r_ref[...] = jnp.zeros_like(r_ref)
        rsv_ref[...] = jnp.full_like(rsv_ref, R_UNREACHED)

        def sweep(tiles, chains):
            heads, nch = range(2), range(len(chains))
            kv = []
            for t in tiles:
                rows = pl.ds(pl.multiple_of(t * tt, tt), tt)
                kv.append((k_ref[rows, :], v_ref[rows, :]))
            z = [[_dot_nt(qm[u][h], kv[ti][0]) for h in heads] for u, ti, _ in chains]
            ls, lk = [], []
            for n in nch:
                pairs = [_log_sigmoids(z[n][h]) for h in heads]
                ls.append([p[0] for p in pairs])
                lk.append([jnp.where(below, p[1], 0.0) if chains[n][2] else p[1] for p in pairs])
            cur = {u: [r_ref[h, qrows[u], :] for h in heads] for u in sorted({c[0] for c in chains})}
            r = []
            for n, (u, _, _) in enumerate(chains):
                r.append(cur[u])
                cur[u] = [cur[u][h] + jnp.sum(lk[n][h], axis=1, keepdims=True) for h in heads]
            for u in cur:
                rmax_ref[u] = jnp.max(jnp.maximum(cur[u][0], cur[u][1]))
            suffix = [[_dot(jnp.concatenate(_split_hi_lo(lk[n][h]), axis=1), tri2) for h in heads] for n in nch]
            w = []
            for n in nch:
                w_n = [jnp.exp(ls[n][h] + suffix[n][h] + r[n][h]) for h in heads]
                if chains[n][2]:
                    w_n = [jnp.where(below, w_h, 0.0) for w_h in w_n]
                w.append([w_h.astype(BF16) for w_h in w_n])
            pv = [[_dot(w[n][h], kv[chains[n][1]][1]) for h in heads] for n in nch]
            for u in cur:
                mine = [n for n in nch if chains[n][0] == u]
                for h in heads:
                    acc_ref[h, qrows[u], :] += functools.reduce(lambda a, b: a + b, [pv[n][h] for n in mine])
                    for n in mine:
                        if not chains[n][2]:
                            t = tiles[chains[n][1]]
                            rsv_ref[h, qrows[u], :] = jnp.where(lane == t, r[n][h], rsv_ref[h, qrows[u], :])
                    r_ref[h, qrows[u], :] = cur[u][h]

        first = per_block * i

        @pl.when(i == 0)
        def _():
            sweep([0, 1], [(0, 0, True), (1, 1, True), (1, 0, False)])

        @pl.when(i > 0)
        def _():
            sweep([first - 1, first, first + 1], [(0, 1, True), (1, 2, True), (0, 0, False), (1, 1, False)])

        for u in range(per_block):
            n_left = first + u - 1

            def live(c, n_left=n_left):
                jj, rmax = c
                return (jj < n_left) & (rmax >= CARRY_FLOOR)

            def step(c, u=u, n_left=n_left):
                jj, _ = c
                sweep([n_left - 1 - jj], [(u, 0, False)])
                return jj + 1, rmax_ref[u]

            swept, _ = lax.while_loop(live, step, (jnp.int32(0), rmax_ref[u]))
            extra_ref[pl.program_id(0), first + u] = swept.astype(F32)

        lane_b = lax.broadcasted_iota(jnp.int32, (tb, LANES), 1)
        o_ref[...] = jnp.where(lane_b < HEAD_DIM, acc_ref[0], acc_ref[1])
        rs_ref[...] = rsv_ref[...]

        @pl.when((pl.program_id(0) == N_PAIRS - 1) & (i == nq - 1))
        def _():
            gather.finish()

    def body(q_ref, k_ref, v_ref, *rest):
        rider_in, (o_ref, rs_ref, extra_ref) = rest[:nr], rest[nr:nr + 3]
        rider_out = rest[nr + 3:2 * nr + 3]
        acc_ref, r_ref, rsv_ref, rmax_ref = rest[2 * nr + 3:2 * nr + 7]
        gather = _Gather(rider_in, rider_out, *rest[2 * nr + 7:])

        @pl.when((pl.program_id(0) == 0) & (pl.program_id(1) == 0))
        def _():
            gather.issue()

        sweep_body(gather, q_ref, k_ref, v_ref, o_ref, rs_ref, extra_ref, acc_ref, r_ref, rsv_ref, rmax_ref)

    any_spec = pl.BlockSpec(memory_space=pl.ANY)
    res = pl.pallas_call(
        body, name=name,
        out_shape=[jax.ShapeDtypeStruct((s, D_SB), F32), jax.ShapeDtypeStruct((N_HEADS, s, LANES), F32),
                   jax.ShapeDtypeStruct((N_PAIRS, s // tt), F32)] + _gather_out_shapes(riders),
        grid=(N_PAIRS, nq),
        in_specs=[pl.BlockSpec((tb, LANES), lambda p, i: (i, p)),
                  pl.BlockSpec((s, LANES), lambda p, i: (0, N_PAIRS + p)),
                  pl.BlockSpec((s, LANES), lambda p, i: (0, 2 * N_PAIRS + p))] + [any_spec] * nr,
        out_specs=[pl.BlockSpec((tb, LANES), lambda p, i: (i, p)),
                   pl.BlockSpec((2, tb, LANES), lambda p, i: (p, i, 0)),
                   pl.BlockSpec(memory_space=pltpu.SMEM)] + [any_spec] * nr,
        scratch_shapes=[pltpu.VMEM((2, tb, LANES), F32), pltpu.VMEM((2, tb, LANES), F32),
                        pltpu.VMEM((2, tb, LANES), F32), pltpu.SMEM((per_block,), F32)] + _gather_semaphores(nr),
        compiler_params=_cparams(("arbitrary", "arbitrary")),
    )(qkv, qkv, qkv, *riders)
    return res[0], res[1], res[2], res[3:]


def _attn_bwd(qkv, do, rs, extra, dproj, smalls, cps, name):
    s = qkv.shape[0]
    tb, tt = ATT_BLOCK, ATT_TILE
    nq = s // tb
    per_block = tb // tt
    scale = HEAD_DIM ** -0.5

    ns, nc = len(smalls), len(cps)

    def sweep_body(q_ref, k_ref, v_ref, do_ref, rs_ref, extra_ref, dproj_hbm, out_hbm, dq_acc, dk_acc, dv_acc,
                   dqi_ref, pc_ref, stage_ref, out_sems):
        del dproj_hbm
        pair = pl.program_id(0)
        lane = lax.broadcasted_iota(jnp.int32, (tt, LANES), 1)
        hmask = [lane < HEAD_DIM, lane >= HEAD_DIM]
        row, col, tri = _strict_lower_ones(tt)
        tri2 = jnp.concatenate([tri, tri], axis=0)
        triu = (row < col).astype(BF16)
        below = col < row
        zero = jnp.zeros((), BF16)
        qrows = [slice(u * tt, (u + 1) * tt) for u in range(per_block)]
        dq_acc[...] = jnp.zeros_like(dq_acc)
        dk_acc[...] = jnp.zeros_like(dk_acc)
        dv_acc[...] = jnp.zeros_like(dv_acc)

        def qblock(i, carry):
            block_rows = pl.ds(pl.multiple_of(i * tb, tb), tb)
            q2 = q_ref[block_rows, :]
            do2 = do_ref[block_rows, :]
            qm = [[jnp.where(m, q2[qrows[u]], zero) * jnp.asarray(scale, BF16) for m in hmask]
                  for u in range(per_block)]
            dom = [[jnp.where(m, do2[qrows[u]], zero) for m in hmask] for u in range(per_block)]
            rs_i = [[rs_ref[h, pl.ds(pl.multiple_of(i * tb + u * tt, tt), tt), :] for h in range(2)]
                    for u in range(per_block)]
            dqi_ref[...] = jnp.zeros_like(dqi_ref)
            pc_ref[...] = jnp.zeros_like(pc_ref)
            first = per_block * i

            def sweep(tiles, chains):
                heads, nch = range(2), range(len(chains))
                krows = [pl.ds(pl.multiple_of(t * tt, tt), tt) for t in tiles]
                k2 = [k_ref[rows, :] for rows in krows]
                v2 = [v_ref[rows, :] for rows in krows]
                z = [[_dot_nt(qm[u][h], k2[ti]) for h in heads] for u, ti, _ in chains]
                dw = [[_dot_nt(dom[u][h], v2[ti]) for h in heads] for u, ti, _ in chains]
                ls, lk = [], []
                for n in nch:
                    pairs = [_log_sigmoids(z[n][h]) for h in heads]
                    ls.append([p[0] for p in pairs])
                    lk.append([jnp.where(below, p[1], 0.0) if chains[n][2] else p[1] for p in pairs])
                suffix = [[_dot(jnp.concatenate(_split_hi_lo(lk[n][h]), axis=1), tri2) for h in heads] for n in nch]
                w, g = [], []
                for n, (u, ti, diag) in enumerate(chains):
                    w_n = []
                    for h in heads:
                        logw = ls[n][h] + suffix[n][h]
                        if diag:
                            w_n.append(jnp.where(below, jnp.exp(logw), 0.0))
                        else:
                            carry_in = jnp.sum(jnp.where(lane == tiles[ti], rs_i[u][h], 0.0), axis=1, keepdims=True)
                            w_n.append(jnp.exp(logw + carry_in))
                    w.append([w_h.astype(BF16) for w_h in w_n])
                    g.append([w_n[h] * dw[n][h] for h in heads])
                prefix = [[_dot(g[n][h].astype(BF16), triu) for h in heads] for n in nch]
                cur = {u: [pc_ref[h, qrows[u], :] for h in heads] for u in sorted({c[0] for c in chains})}
                pc = []
                for n, (u, _, _) in enumerate(chains):
                    pc.append(cur[u])
                    cur[u] = [cur[u][h] + jnp.sum(g[n][h], axis=1, keepdims=True) for h in heads]
                dzb = []
                for n in nch:
                    dz_n = []
                    for h in heads:
                        sig = jnp.exp(ls[n][h])
                        dz = g[n][h] - sig * (g[n][h] + prefix[n][h] + pc[n][h])
                        if chains[n][2]:
                            dz = jnp.where(below, dz, 0.0)
                        dz_n.append(dz.astype(BF16))
                    dzb.append(dz_n)
                for u in cur:
                    for h in heads:
                        pc_ref[h, qrows[u], :] = cur[u][h]
                dq = [[_dot(dzb[n][h], jnp.where(hmask[h], k2[chains[n][1]], zero)) for h in heads] for n in nch]
                dk = [[_dot_tn(dzb[n][h], qm[chains[n][0]][h]) for h in heads] for n in nch]
                dv = [[_dot_tn(w[n][h], dom[chains[n][0]][h]) for h in heads] for n in nch]
                add = lambda a, b: a + b
                for u in cur:
                    dqi_ref[qrows[u], :] += functools.reduce(
                        add, [dq[n][h] for n in nch if chains[n][0] == u for h in heads])
                for ti in range(len(tiles)):
                    mine = [n for n in nch if chains[n][1] == ti]
                    dk_acc[krows[ti], :] += functools.reduce(add, [dk[n][h] for n in mine for h in heads])
                    dv_acc[krows[ti], :] += functools.reduce(add, [dv[n][h] for n in mine for h in heads])

            for u in range(per_block):
                n_left = first + u - 1
                n_extra = jnp.clip(extra_ref[pair, first + u].astype(jnp.int32), 0, jnp.maximum(n_left, 0))

                def step(t, c, u=u):
                    sweep([t], [(u, 0, False)])
                    return c

                lax.fori_loop(n_left - n_extra, n_left, step, 0)

            @pl.when(i == 0)
            def _():
                sweep([0, 1], [(0, 0, True), (1, 0, False), (1, 1, True)])

            @pl.when(i > 0)
            def _():
                sweep([first - 1, first, first + 1], [(0, 0, False), (1, 1, False), (0, 1, True), (1, 2, True)])

            dq_acc[block_rows, :] += dqi_ref[...] * scale
            return carry

        lax.fori_loop(0, nq, qblock, 0)
        copies = []
        for t, acc in enumerate((dq_acc, dk_acc, dv_acc)):
            stage_ref[t] = acc[...].astype(BF16)
            col0 = pl.multiple_of(t * D_SB + pair * LANES, LANES)
            copies.append(pltpu.make_async_copy(stage_ref.at[t], out_hbm.at[:, pl.ds(col0, LANES)], out_sems.at[t]))
        for cp in copies:
            cp.start()
        for cp in copies:
            cp.wait()

    def body(q_ref, k_ref, v_ref, do_ref, rs_ref, extra_ref, dproj_hbm, *rest):
        small_in, cp_in = rest[:ns], rest[ns:ns + nc]
        out_hbm = rest[ns + nc]
        small_out, land_out = rest[ns + nc + 1:2 * ns + nc + 1], rest[2 * ns + nc + 1:2 * (ns + nc) + 1]
        scratch = rest[2 * (ns + nc) + 1:]
        gather = _Gather(small_in, small_out, *scratch[7:10])
        pair = pl.program_id(0)

        @pl.when(pair == 0)
        def _():
            gather.issue()
            for cp in _direct_push_copies(cp_in, land_out, *scratch[10:]):
                cp.start()

        sweep_body(q_ref, k_ref, v_ref, do_ref, rs_ref, extra_ref, dproj_hbm, out_hbm, *scratch[:7])

        @pl.when(pair == N_PAIRS - 1)
        def _():
            gather.finish()
            for cp in _direct_push_copies(cp_in, land_out, *scratch[10:]):
                cp.wait()

    any_spec = pl.BlockSpec(memory_space=pl.ANY)
    res = pl.pallas_call(
        body, name=name,
        out_shape=[jax.ShapeDtypeStruct(dproj.shape, BF16)] + _gather_out_shapes(smalls)
        + [jax.ShapeDtypeStruct((N_DEV - 1,) + a.shape[1:], a.dtype) for a in cps],
        grid=(N_PAIRS,),
        in_specs=[pl.BlockSpec((s, LANES), lambda p: (0, p)),
                  pl.BlockSpec((s, LANES), lambda p: (0, N_PAIRS + p)),
                  pl.BlockSpec((s, LANES), lambda p: (0, 2 * N_PAIRS + p)),
                  pl.BlockSpec((s, LANES), lambda p: (0, p)),
                  pl.BlockSpec((2, s, LANES), lambda p: (p, 0, 0)),
                  pl.BlockSpec(memory_space=pltpu.SMEM),
                  any_spec] + [any_spec] * (ns + nc),
        out_specs=[any_spec] * (1 + ns + nc),
        scratch_shapes=[pltpu.VMEM((s, LANES), F32), pltpu.VMEM((s, LANES), F32), pltpu.VMEM((s, LANES), F32),
                        pltpu.VMEM((tb, LANES), F32), pltpu.VMEM((2, tb, LANES), F32),
                        pltpu.VMEM((3, s, LANES), BF16), pltpu.SemaphoreType.DMA((3,))]
        + _gather_semaphores(ns)
        + [pltpu.SemaphoreType.DMA((nc, N_DEV - 1)), pltpu.SemaphoreType.DMA((nc, N_DEV - 1))],
        input_output_aliases={6: 0},
        compiler_params=_cparams(("arbitrary",)),
    )(qkv, qkv, qkv, do, rs, extra, dproj, *smalls, *cps)
    return res[0], res[1:1 + ns], res[1 + ns:]


def _mid(o, rest, x, tgt, wua, wub, wout, fg, lng, lnb, wsp, bfull, gavg, name):
    s, d = x.shape
    ts = 256
    nt = s // ts
    nchunk = ts // SGU_CHUNK
    n_rest = rest.shape[1]

    def body(o_ref, rest_ref, x_ref, t_ref, wua_ref, wub_ref, wout_ref, fg_ref, lng_ref, lnb_ref, wsp_ref, bfull_ref,
             gavg_ref, loss_ref, dx2_ref, do_ref, dproj_ref, dwua_out, dwub_out, dwout_out, dfg_ref, dlng_ref,
             dlnb_ref, dwsp_ref, dbfull_ref, dwua_ref, dwub_ref, dwout_ref):
        step = pl.program_id(0)

        @pl.when(step == 0)
        def _():
            for ref in (loss_ref, dwua_ref, dwub_ref, dwout_ref, dfg_ref, dlng_ref, dlnb_ref, dwsp_ref, dbfull_ref):
                ref[...] = jnp.zeros_like(ref)

        gavg = gavg_ref[...]

        def gmean(a):
            return _dot(a.astype(BF16), gavg)

        def colsum8(a):
            return jnp.sum(a.reshape(ts // 8, 8, a.shape[1]), axis=0)

        z_a = rest_ref[:, 0:512]
        u_b = rest_ref[:, 512:1024]
        v_b = rest_ref[:, 1024:1536]
        z_b = rest_ref[:, 1536:2048]
        g_a = rest_ref[:, 2048:2048 + d]
        g_b = rest_ref[:, 2048 + d:2048 + 2 * d]
        ov = o_ref[...]
        sa = _sigmoid(z_a)
        silu_a = z_a * sa
        y_a = ov * silu_a
        ug, dug_du = _gelu_and_grad(u_b)
        vg, dvg_dv = _gelu_and_grad(v_b)
        mu = gmean(vg)
        cen = vg - mu
        rstd_g = lax.rsqrt(gmean(cen * cen) + EPS)
        vhat = cen * rstd_g
        vn = vhat * lng_ref[...] + lnb_ref[...]
        vnb = vn.astype(BF16)

        t_idx = lax.broadcasted_iota(jnp.int32, (SGU_CHUNK, SGU_CHUNK), 0)
        s_idx = lax.broadcasted_iota(jnp.int32, (SGU_CHUNK, SGU_CHUNK), 1)
        causal = (s_idx // CHUNK) <= (t_idx // CHUNK)
        wm = [jnp.where(causal, wsp_ref[g], 0.0) for g in range(N_GROUPS)]
        wmb = [w.astype(BF16) for w in wm]
        wmtb = [w.T.astype(BF16) for w in wm]
        lane = lax.broadcasted_iota(jnp.int32, (SGU_CHUNK, LANES), 1)
        first = lane < GROUP_DIM
        bfull = bfull_ref[...]

        mixed_rows = []
        for n in range(nchunk):
            r0, r1 = n * SGU_CHUNK, (n + 1) * SGU_CHUNK
            pieces = []
            for p in range(N_GROUPS // 2):
                blk = vnb[r0:r1, p * LANES:(p + 1) * LANES]
                pieces.append(jnp.where(first, _dot(wmb[2 * p], blk), _dot(wmb[2 * p + 1], blk)))
            mixed_rows.append(jnp.concatenate(pieces, axis=1) + bfull)
        mixed = jnp.concatenate(mixed_rows, axis=0)
        sg = ug * mixed
        sb = _sigmoid(z_b)
        silu_b = z_b * sb
        y_b = sg * silu_b
        y_ab = y_a.astype(BF16)
        y_bb = y_b.astype(BF16)
        p_a = _dot(y_ab, wua_ref[...])
        p_b = _dot(y_bb, wub_ref[...])
        ga_s = _sigmoid(g_a)
        gb_s = _sigmoid(g_b)
        merged_b = (ga_s * p_a + gb_s * p_b).astype(BF16)
        x2 = x_ref[...] + _dot(merged_b, wout_ref[...])
        rstd = lax.rsqrt(jnp.mean(x2 * x2, axis=-1, keepdims=True) + EPS)
        xhat = x2 * rstd
        fg_v = fg_ref[...]
        diff = xhat * fg_v - t_ref[...]
        loss_ref[...] += 0.5 * jnp.sum(jnp.sum(diff * diff, axis=-1, keepdims=True) * (1.0 / d))

        dy = diff * (1.0 / d)
        dfg_ref[...] += colsum8(dy * xhat)
        dxh = dy * fg_v
        dx2 = rstd * (dxh - xhat * jnp.mean(dxh * xhat, axis=-1, keepdims=True))
        dx2_ref[...] = dx2
        dx2b = dx2.astype(BF16)
        dwout_ref[...] += _dot_tn(merged_b, dx2b)
        dmerged = _dot_nt(dx2b, wout_ref[...])
        dp_a = dmerged * ga_s
        dp_b = dmerged * gb_s
        dproj_ref[:, QKV_COLS + 2048:QKV_COLS + 2048 + d] = (dmerged * p_a * (ga_s * (1.0 - ga_s))).astype(BF16)
        dproj_ref[:, QKV_COLS + 2048 + d:QKV_COLS + 2048 + 2 * d] = (dmerged * p_b * (gb_s * (1.0 - gb_s))).astype(BF16)
        dp_ab = dp_a.astype(BF16)
        dp_bb = dp_b.astype(BF16)
        dwua_ref[...] += _dot_tn(y_ab, dp_ab)
        dwub_ref[...] += _dot_tn(y_bb, dp_bb)
        dy_a = _dot_nt(dp_ab, wua_ref[...])
        dy_b = _dot_nt(dp_bb, wub_ref[...])
        do_ref[...] = (dy_a * silu_a).astype(BF16)
        dproj_ref[:, QKV_COLS:QKV_COLS + 512] = (dy_a * ov * (sa * (1.0 + z_a * (1.0 - sa)))).astype(BF16)
        dsg = dy_b * silu_b
        dproj_ref[:, QKV_COLS + 1536:QKV_COLS + 2048] = (dy_b * sg * (sb * (1.0 + z_b * (1.0 - sb)))).astype(BF16)
        dproj_ref[:, QKV_COLS + 512:QKV_COLS + 1024] = (dsg * mixed * dug_du).astype(BF16)
        dmixed = dsg * ug
        dmb = dmixed.astype(BF16)
        zero = jnp.zeros((), BF16)
        dvn_rows = []
        db = jnp.zeros((SGU_CHUNK, D_SGU), F32)
        for n in range(nchunk):
            r0, r1 = n * SGU_CHUNK, (n + 1) * SGU_CHUNK
            db = db + dmixed[r0:r1, :]
            pieces = []
            for p in range(N_GROUPS // 2):
                cols = slice(p * LANES, (p + 1) * LANES)
                dm_blk = dmb[r0:r1, cols]
                vn_blk = vnb[r0:r1, cols]
                dwsp_ref[2 * p] += _dot_nt(jnp.where(first, dm_blk, zero), vn_blk)
                dwsp_ref[2 * p + 1] += _dot_nt(jnp.where(first, zero, dm_blk), vn_blk)
                pieces.append(jnp.where(first, _dot(wmtb[2 * p], dm_blk), _dot(wmtb[2 * p + 1], dm_blk)))
            dvn_rows.append(jnp.concatenate(pieces, axis=1))
        dbfull_ref[...] += db
        dvn = jnp.concatenate(dvn_rows, axis=0)
        dlng_ref[...] += colsum8(dvn * vhat)
        dlnb_ref[...] += colsum8(dvn)
        dvhat = dvn * lng_ref[...]
        dcen = rstd_g * (dvhat - gmean(dvhat) - vhat * gmean(dvhat * vhat))
        dproj_ref[:, QKV_COLS + 1024:QKV_COLS + 1536] = (dcen * dvg_dv).astype(BF16)

        @pl.when(step == nt - 1)
        def _():
            for g in range(N_GROUPS):
                dwsp_ref[g] = jnp.where(causal, dwsp_ref[g], 0.0)
            cb = d // N_DEV
            for k in range(N_DEV):
                dwua_out[k] = dwua_ref[:, k * cb:(k + 1) * cb].astype(BF16)
                dwub_out[k] = dwub_ref[:, k * cb:(k + 1) * cb].astype(BF16)
            dwout_out[...] = dwout_ref[...].astype(BF16)

    def tile(cols):
        return pl.BlockSpec((ts, cols), lambda i: (i, 0))

    def whole(shape):
        return pl.BlockSpec(shape, lambda i: (0,) * len(shape))

    out_shapes = [
        jax.ShapeDtypeStruct((8, LANES), F32),
        jax.ShapeDtypeStruct((s, d), F32),
        jax.ShapeDtypeStruct((s, D_SB), BF16),
        jax.ShapeDtypeStruct((s, QKV_COLS + n_rest), BF16),
        jax.ShapeDtypeStruct((N_DEV, D_SB, d // N_DEV), BF16),
        jax.ShapeDtypeStruct((N_DEV, D_SGU, d // N_DEV), BF16),
        jax.ShapeDtypeStruct((d, d), BF16),
        jax.ShapeDtypeStruct((8, d), F32),
        jax.ShapeDtypeStruct((8, D_SGU), F32),
        jax.ShapeDtypeStruct((8, D_SGU), F32),
        jax.ShapeDtypeStruct((N_GROUPS, SGU_CHUNK, SGU_CHUNK), F32),
        jax.ShapeDtypeStruct((SGU_CHUNK, D_SGU), F32),
    ]
    out_specs = [whole((8, LANES)), tile(d), tile(D_SB), tile(QKV_COLS + n_rest), whole((N_DEV, D_SB, d // N_DEV)),
                 whole((N_DEV, D_SGU, d // N_DEV)), whole((d, d)), whole((8, d)), whole((8, D_SGU)), whole((8, D_SGU)),
                 whole((N_GROUPS, SGU_CHUNK, SGU_CHUNK)), whole((SGU_CHUNK, D_SGU))]
    in_specs = [tile(D_SB), tile(n_rest), tile(d), tile(d), whole((D_SB, d)), whole((D_SGU, d)), whole((d, d)),
                whole((1, d)), whole((1, D_SGU)), whole((1, D_SGU)), whole((N_GROUPS, SGU_CHUNK, SGU_CHUNK)),
                whole((SGU_CHUNK, D_SGU)), whole((D_SGU, D_SGU))]
    return pl.pallas_call(
        body, name=name, out_shape=out_shapes, grid=(nt,), in_specs=in_specs, out_specs=out_specs,
        scratch_shapes=[pltpu.VMEM((D_SB, d), F32), pltpu.VMEM((D_SGU, d), F32), pltpu.VMEM((d, d), F32)],
        compiler_params=_cparams(("arbitrary",)),
    )(o, rest, x, tgt, wua, wub, wout, fg, lng, lnb, wsp, bfull, gavg)


def _small_reduce(dfg8, dlng8, dlnb8, dbfull, name):
    d = dfg8.shape[1]

    def body(dfg_ref, dlng_ref, dlnb_ref, dbfull_ref, fg_out, lng_out, lnb_out, b_out):
        row = lax.broadcasted_iota(jnp.int32, (D_SGU, LANES), 0)
        col = lax.broadcasted_iota(jnp.int32, (D_SGU, LANES), 1)

        def select_sum(a, sel):
            hi, lo = _split_hi_lo(a)
            lo2 = (a - hi.astype(F32) - lo.astype(F32)).astype(BF16)
            return _dot(hi, sel) + _dot(lo, sel) + _dot(lo2, sel)

        def by_group(partials):
            v = jnp.sum(partials, axis=0, keepdims=True)
            g_of_lane = lax.broadcasted_iota(jnp.int32, (N_GROUPS, D_SGU), 1) // GROUP_DIM
            g_of_row = lax.broadcasted_iota(jnp.int32, (N_GROUPS, D_SGU), 0)
            spread = jnp.where(g_of_lane == g_of_row, jnp.broadcast_to(v, (N_GROUPS, D_SGU)), 0.0)
            return select_sum(spread, (row % GROUP_DIM == col).astype(BF16))[:, 0:GROUP_DIM]

        fg_out[...] = jnp.sum(dfg_ref[...], axis=0, keepdims=True)
        lng_out[...] = by_group(dlng_ref[...])
        lnb_out[...] = by_group(dlnb_ref[...])
        by_pos = select_sum(dbfull_ref[...], (row // GROUP_DIM == col).astype(BF16))
        b_out[...] = by_pos.T[0:N_GROUPS, :]

    return pl.pallas_call(
        body, name=name,
        out_shape=[jax.ShapeDtypeStruct((1, d), F32), jax.ShapeDtypeStruct((N_GROUPS, GROUP_DIM), F32),
                   jax.ShapeDtypeStruct((N_GROUPS, GROUP_DIM), F32), jax.ShapeDtypeStruct((N_GROUPS, SGU_CHUNK), F32)],
        compiler_params=_cparams(),
    )(dfg8, dlng8, dlnb8, dbfull)


def _from_block_major_cols(w):
    nb, r, c = w.shape
    return jnp.transpose(w, (1, 0, 2)).reshape(r, nb * c)


def kernel(x, norm_g, w_in, sgu_ln_g, sgu_ln_b, w_spatial, b_spatial, w_up_a, w_up_b, w_out, final_norm_g, loss_target, m_norm_g, m_w_in, m_sgu_ln_g, m_sgu_ln_b, m_w_spatial, m_b_spatial, m_w_up_a, m_w_up_b, m_w_out, m_final_norm_g, v_norm_g, v_w_in, v_sgu_ln_g, v_sgu_ln_b, v_w_spatial, v_b_spatial, v_w_up_a, v_w_up_b, v_w_out, v_final_norm_g):
    s, d = x.shape[1], x.shape[2]
    xs = x[0]
    tgt = loss_target[0]
    cx, cy, cc = _coords()
    core = jnp.reshape(cc, (1,)).astype(jnp.int32)
    chip = jnp.reshape(2 * cx + cy, (1,)).astype(jnp.int32)

    h, (g_win,) = _rmsnorm_fwd(xs, norm_g, [jnp.transpose(w_in[0])], "norm")
    d_in = N_DEV * w_in.shape[2]
    w_full = g_win.reshape(d_in, d)
    qkv, rest = _inproj(h, w_full, "inproj")
    o, rs, extra, (g_wua, g_wub, g_wout) = _attn_fwd(
        qkv, [w_up_a[0].astype(BF16), w_up_b[0].astype(BF16), w_out[0].astype(BF16)], "attn_fwd")
    wua_full = _from_block_major_cols(g_wua)
    wub_full = _from_block_major_cols(g_wub)
    wout_full = g_wout.reshape(d, d)

    lng = sgu_ln_g.reshape(1, D_SGU)
    lnb = sgu_ln_b.reshape(1, D_SGU)
    bfull = jnp.repeat(jnp.transpose(b_spatial[0]), GROUP_DIM, axis=1)
    grp = jnp.arange(D_SGU) // GROUP_DIM
    gavg = jnp.where(grp[:, None] == grp[None, :], 1.0 / GROUP_DIM, 0.0).astype(BF16)
    (loss_b, dx2, do, dproj, dwua, dwub, dwout, dfg8, dlng8, dlnb8, dwsp, dbfull) = _mid(
        o, rest, xs, tgt, wua_full, wub_full, wout_full, final_norm_g.reshape(1, d), lng, lnb, w_spatial[0], bfull,
        gavg, "mid")

    own_up = [dwua, dwub, dwout.reshape(N_DEV, d // N_DEV, d)]
    dfg, dlng, dlnb, db = _small_reduce(dfg8, dlng8, dlnb8, dbfull, "small_reduce")
    wsp_rows = N_GROUPS * SGU_CHUNK
    dproj, small_parts, land_up = _attn_bwd(
        qkv, do, rs, extra, dproj, [dfg, dlng, dlnb, db, dwsp.reshape(wsp_rows, SGU_CHUNK)], own_up, "attn_bwd")
    dwin = _dw_in(h, dproj, "dwin")

    own_in = [dwin.reshape(4, 2, d_in // N_DEV, d)]
    cps_in = list(_chip_partial_sums(own_in, list(_push_sibling(own_in, "rs_sibling_in")), core, "cpsum_in"))
    grad_x, dng8, land_in = _dh_and_grad_x(dproj, w_full, xs, norm_g, dx2, cps_in, "dh")

    ng_parts, loss_parts = _allgather([dng8, loss_b], "ag_tail")

    def small_layouts(ng, fg, lg, lb, bs, ws):
        return [ng.reshape(1, d), fg.reshape(1, d), lg.reshape(N_GROUPS, GROUP_DIM), lb.reshape(N_GROUPS, GROUP_DIM),
                bs.reshape(N_GROUPS, SGU_CHUNK), ws.reshape(wsp_rows, SGU_CHUNK)]

    sm, loss_sum = _adam_small(
        [ng_parts] + list(small_parts),
        small_layouts(norm_g, final_norm_g, sgu_ln_g, sgu_ln_b, b_spatial, w_spatial),
        small_layouts(m_norm_g, m_final_norm_g, m_sgu_ln_g, m_sgu_ln_b, m_b_spatial, m_w_spatial),
        small_layouts(v_norm_g, v_final_norm_g, v_sgu_ln_g, v_sgu_ln_b, v_b_spatial, v_w_spatial),
        loss_parts, "adam_small")
    small_shapes = [norm_g.shape, final_norm_g.shape, sgu_ln_g.shape, sgu_ln_b.shape, b_spatial.shape, w_spatial.shape]
    sm = [[a.reshape(shp) for a, shp in zip(kind, small_shapes)] for kind in sm]

    res = _adam_shard(cps_in, list(land_in), chip, *[jnp.transpose(a[0]) for a in (w_in, m_w_in, v_w_in)], "adam0")
    big = [[jnp.transpose(r)[None] for r in res]]
    dev = jnp.reshape(_dev_index(cx, cy, cc), (1,)).astype(jnp.int32)
    res = _adam_shards_whole(own_up, list(land_up), dev, [w_up_a[0], w_up_b[0], w_out[0]],
                             [m_w_up_a[0], m_w_up_b[0], m_w_out[0]], [v_w_up_a[0], v_w_up_b[0], v_w_out[0]], "adam_up")
    for i in range(3):
        big.append([res[kd][i][None] for kd in range(4)])

    loss = loss_sum[0, 0]

    def per_kind(kd):
        return [sm[kd][0], big[0][kd], sm[kd][2], sm[kd][3], sm[kd][5], sm[kd][4], big[1][kd], big[2][kd], big[3][kd],
                sm[kd][1]]

    return (loss, grad_x[None], *per_kind(0), *per_kind(1), *per_kind(2), *per_kind(3))
```
